```python
import jax, jax.numpy as jnp
from jax import lax
import numpy as np

D_MODEL = 1024
BATCH = 16
SEQ = 256
DEPTH = 2
DEC_BATCH = 2
DEC_SEQ = 4096
PAST_LEN = 512

GRID_W = 64
N_HEADS = 8
N_KV_HEADS = 2
HEAD_DIM = 64
ROPE_AXIS_DIM = HEAD_DIM // 2
ROPE_THETA = 10000.0
Q_BLOCK = 128
D_ATTN = N_HEADS * HEAD_DIM
D_KV = N_KV_HEADS * HEAD_DIM
CHUNK = 128
N_SGU_GROUPS = 4
SGU_GROUP_DIM = 64
D_SGU = N_SGU_GROUPS * SGU_GROUP_DIM
CONV_WIDTH = 3
N_CONV_GROUPS = 4
D_CONV = 256
D_MIX = D_ATTN + D_SGU + D_CONV
D_IN = D_ATTN + 2 * D_KV + 2 * D_SGU + 3 * D_CONV
N_EXPERT_GROUPS = 4
EXPERTS_PER_GROUP = 4
N_EXPERTS = N_EXPERT_GROUPS * EXPERTS_PER_GROUP
TOP_K_IN_GROUP = 2
D_EXPERT = 256
EPS = 1e-6

kernel_name = 'hybrid_diffusion_parallel_heads_step'


def rms_norm(x, g):
    xf = x.astype(jnp.float32)
    y = xf * lax.rsqrt(jnp.mean(xf * xf, axis=-1, keepdims=True) + EPS)
    return (y * g.astype(jnp.float32)).astype(x.dtype)


def axial_rope_tables(n):
    rows = n // GRID_W
    row = jnp.repeat(jnp.arange(rows, dtype=jnp.float32), GRID_W)
    col = jnp.tile(jnp.arange(GRID_W, dtype=jnp.float32), rows)
    half = ROPE_AXIS_DIM // 2
    inv_freq = 1.0 / (ROPE_THETA ** (jnp.arange(half, dtype=jnp.float32) * 2.0 / ROPE_AXIS_DIM))
    ang = jnp.stack([row[:, None] * inv_freq, col[:, None] * inv_freq], axis=1)
    return jnp.cos(ang), jnp.sin(ang)


def apply_axial_rope(x, cos, sin):
    B, n, H, Dh = x.shape
    xf = x.astype(jnp.float32).reshape(B, n, H, 2, 2, ROPE_AXIS_DIM // 2)
    a = xf[..., 0, :]
    b = xf[..., 1, :]
    cs = cos[None, :, None]
    sn = sin[None, :, None]
    out = jnp.stack([a * cs - b * sn, b * cs + a * sn], axis=-2)
    return out.reshape(B, n, H, Dh).astype(x.dtype)


def block_attention(q, k, v):
    B, L, H, Dh = q.shape
    KV = k.shape[2]
    G = H // KV
    nb = L // Q_BLOCK
    qb = jnp.moveaxis(q.reshape(B, nb, Q_BLOCK, KV, G, Dh), 1, 0)
    scale = Dh ** -0.5

    def one_block(qblk):
        s = jnp.einsum('bqkgd,bmkd->bkgqm', qblk, k).astype(jnp.float32) * scale
        p = jax.nn.softmax(s, axis=-1).astype(v.dtype)
        return jnp.einsum('bkgqm,bmkd->bqkgd', p, v)

    o = lax.map(one_block, qb)
    return jnp.moveaxis(o, 0, 1).reshape(B, L, H * Dh)


def chunk_sgu(u, v, w_s, b_s):
    B, L, C = v.shape
    nc = L // CHUNK
    vr = v.reshape(B, nc, CHUNK, N_SGU_GROUPS, SGU_GROUP_DIM)
    mixed = jnp.einsum('gpq,bnqgc->bnpgc', w_s, vr) + jnp.transpose(b_s)[:, :, None]
    return u * mixed.reshape(B, L, C)


def short_gated_conv(bg, cg, xs, w):
    z = cg * xs
    zp = jnp.pad(z, ((0, 0), (1, 1), (0, 0)))
    conv = zp[:, :-2] * w[0] + zp[:, 1:-1] * w[1] + zp[:, 2:] * w[2]
    return bg * conv


def hierarchical_moe(h, rg_w, rg_b, re_w, re_b, w1, w3, w2):
    B, L, D = h.shape
    t = h.reshape(B * L, D)
    g_logits = (t @ rg_w + rg_b).astype(jnp.float32)
    g_prob = jax.nn.softmax(g_logits, axis=-1)
    g_idx = jnp.argmax(g_logits, axis=-1)
    p_g = jnp.take_along_axis(g_prob, g_idx[:, None], axis=-1)
    e_all = jnp.einsum('td,gde->tge', t, re_w) + re_b
    e_logits = jnp.take_along_axis(e_all, g_idx[:, None, None], axis=1)[:, 0].astype(jnp.float32)
    e_prob = jax.nn.softmax(e_logits, axis=-1)
    top_v, top_i = lax.top_k(e_prob, TOP_K_IN_GROUP)
    top_v = top_v / jnp.sum(top_v, axis=-1, keepdims=True)
    expert_id = g_idx[:, None] * EXPERTS_PER_GROUP + top_i
    weight = (p_g * top_v).astype(t.dtype)
    gates = jnp.sum(jax.nn.one_hot(expert_id, N_EXPERTS, dtype=t.dtype) * weight[..., None], axis=1)
    hid = jax.nn.silu(jnp.einsum('td,edf->tef', t, w1)) * jnp.einsum('td,edf->tef', t, w3)
    out = jnp.einsum('tef,efd->td', hid * gates[:, :, None], w2)
    return out.reshape(B, L, D)


def trunk_layer(x, mod, p, rope, ctx_kv):
    B, L, _ = x.shape
    shift1, scale1, gate1, shift2, scale2, gate2 = jnp.split(mod[:, None, :], 6, axis=-1)
    h = rms_norm(x, p['norm1']) * (1.0 + scale1) + shift1
    proj = h @ p['w_in']
    sizes = (D_ATTN, D_KV, D_KV, D_SGU, D_SGU, D_CONV, D_CONV, D_CONV)
    offs = [int(o) for o in np.cumsum(sizes)[:-1]]
    q, k, v, su, sv, cb, cc, cx = jnp.split(proj, offs, axis=-1)
    q = rms_norm(q.reshape(B, L, N_HEADS, HEAD_DIM), p['q_norm'])
    k = rms_norm(k.reshape(B, L, N_KV_HEADS, HEAD_DIM), p['k_norm'])
    v = v.reshape(B, L, N_KV_HEADS, HEAD_DIM)
    if rope is not None:
        q = apply_axial_rope(q, rope[0], rope[1])
        k = apply_axial_rope(k, rope[0], rope[1])
    if ctx_kv is not None:
        k_all = jnp.concatenate([k, ctx_kv[0]], axis=1)
        v_all = jnp.concatenate([v, ctx_kv[1]], axis=1)
    else:
        k_all, v_all = k, v
    attn = block_attention(q, k_all, v_all)
    sgu = chunk_sgu(su, sv, p['sgu_w'], p['sgu_b'])
    conv = short_gated_conv(cb, cc, cx, p['conv_w'])
    on = p['out_norm']
    merged = jnp.concatenate([
        rms_norm(attn, on[:D_ATTN]),
        rms_norm(sgu, on[D_ATTN:D_ATTN + D_SGU]),
        rms_norm(conv, on[D_ATTN + D_SGU:]),
    ], axis=-1)
    x = x + gate1 * (merged @ p['w_out'])
    h2 = rms_norm(x, p['norm2']) * (1.0 + scale2) + shift2
    x = x + gate2 * hierarchical_moe(h2, p['router_g_w'], p['router_g_b'], p['router_e_w'],
                                     p['router_e_b'], p['moe_w1'], p['moe_w3'], p['moe_w2'])
    return x, k, v


def setup_inputs(seed: int = 0) -> dict:
    key = jax.random.key(seed)
    ks = jax.random.split(key, 32)
    f32 = jnp.float32

    def nrm(k, shape, scale):
        return jax.random.normal(k, shape, f32) * scale

    def gain(k, shape):
        return 1.0 + 0.02 * jax.random.normal(k, shape, f32)

    D = D_MODEL
    return {
        'x_prompt': nrm(ks[0], (BATCH, SEQ, D), 1.0),
        'x_sample': nrm(ks[1], (DEC_BATCH, DEC_SEQ, D), 1.0),
        'cache_k': nrm(ks[2], (DEC_BATCH, DEPTH, PAST_LEN, N_KV_HEADS, HEAD_DIM), 1.0),
        'cache_v': nrm(ks[3], (DEC_BATCH, DEPTH, PAST_LEN, N_KV_HEADS, HEAD_DIM), 0.5),
        'c': nrm(ks[4], (DEC_BATCH, D), 1.0),
        'c_ctx': nrm(ks[5], (D,), 1.0),
        'w_ada': nrm(ks[6], (DEPTH, D, 6 * D), 0.5 * D ** -0.5),
        'b_ada': nrm(ks[7], (DEPTH, 6 * D), 0.02),
        'norm1': gain(ks[8], (DEPTH, D)),
        'w_in': nrm(ks[9], (DEPTH, D, D_IN), D ** -0.5),
        'q_norm': gain(ks[10], (DEPTH, HEAD_DIM)),
        'k_norm': gain(ks[11], (DEPTH, HEAD_DIM)),
        'sgu_w': nrm(ks[12], (DEPTH, N_SGU_GROUPS, CHUNK, CHUNK), CHUNK ** -0.5),
        'sgu_b': gain(ks[13], (DEPTH, N_SGU_GROUPS, CHUNK)),
        'conv_w': nrm(ks[14], (DEPTH, CONV_WIDTH, D_CONV), CONV_WIDTH ** -0.5),
        'out_norm': gain(ks[15], (DEPTH, D_MIX)),
        'w_out': nrm(ks[16], (DEPTH, D_MIX, D), D_MIX ** -0.5),
        'norm2': gain(ks[17], (DEPTH, D)),
        'router_g_w': nrm(ks[18], (DEPTH, D, N_EXPERT_GROUPS), D ** -0.5),
        'router_g_b': nrm(ks[19], (DEPTH, N_EXPERT_GROUPS), 0.01),
        'router_e_w': nrm(ks[20], (DEPTH, N_EXPERT_GROUPS, D, EXPERTS_PER_GROUP), D ** -0.5),
        'router_e_b': nrm(ks[21], (DEPTH, N_EXPERT_GROUPS, EXPERTS_PER_GROUP), 0.01),
        'moe_w1': nrm(ks[22], (DEPTH, N_EXPERTS, D, D_EXPERT), D ** -0.5),
        'moe_w3': nrm(ks[23], (DEPTH, N_EXPERTS, D, D_EXPERT), D ** -0.5),
        'moe_w2': nrm(ks[24], (DEPTH, N_EXPERTS, D_EXPERT, D), D_EXPERT ** -0.5),
        'final_norm': gain(ks[25], (D,)),
    }


def reference(x_prompt, x_sample, cache_k, cache_v, c, c_ctx, w_ada, b_ada, norm1, w_in,
              q_norm, k_norm, sgu_w, sgu_b, conv_w, out_norm, w_out, norm2, router_g_w,
              router_g_b, router_e_w, router_e_b, moe_w1, moe_w3, moe_w2, final_norm):
    def layer_params(l):
        return dict(norm1=norm1[l], w_in=w_in[l], q_norm=q_norm[l], k_norm=k_norm[l],
                    sgu_w=sgu_w[l], sgu_b=sgu_b[l], conv_w=conv_w[l], out_norm=out_norm[l],
                    w_out=w_out[l], norm2=norm2[l], router_g_w=router_g_w[l],
                    router_g_b=router_g_b[l], router_e_w=router_e_w[l], router_e_b=router_e_b[l],
                    moe_w1=moe_w1[l], moe_w3=moe_w3[l], moe_w2=moe_w2[l])

    xp = x_prompt
    ctx_ks = []
    ctx_vs = []
    for l in range(DEPTH):
        mod_ctx = jax.nn.silu(c_ctx)[None, :] @ w_ada[l] + b_ada[l]
        xp, k_l, v_l = trunk_layer(xp, mod_ctx, layer_params(l), None, None)
        ctx_ks.append(k_l)
        ctx_vs.append(v_l)
    y_prompt = rms_norm(xp, final_norm)
    new_cache_k = jnp.stack(ctx_ks, axis=1)
    new_cache_v = jnp.stack(ctx_vs, axis=1)

    cos, sin = axial_rope_tables(x_sample.shape[1])
    xs = x_sample
    for l in range(DEPTH):
        mod_lat = jax.nn.silu(c) @ w_ada[l] + b_ada[l]
        xs, _, _ = trunk_layer(xs, mod_lat, layer_params(l), (cos, sin),
                               (cache_k[:, l], cache_v[:, l]))
    y_sample = rms_norm(xs, final_norm)

    return (y_prompt, y_sample, new_cache_k, new_cache_v)
```

```python
import functools

import jax
import jax.numpy as jnp
import numpy as np
from jax import lax
from jax.experimental import pallas as pl
from jax.experimental.pallas import tpu as pltpu

D_MODEL = 1024
BATCH = 16
SEQ = 256
DEPTH = 2
DEC_BATCH = 2
DEC_SEQ = 4096
PAST_LEN = 512
GRID_W = 64
N_HEADS = 8
N_KV_HEADS = 2
HEAD_DIM = 64
ROPE_AXIS_DIM = HEAD_DIM // 2
ROPE_THETA = 10000.0
D_ATTN = N_HEADS * HEAD_DIM
D_KV = N_KV_HEADS * HEAD_DIM
CHUNK = 128
N_SGU_GROUPS = 4
SGU_GROUP_DIM = 64
D_SGU = N_SGU_GROUPS * SGU_GROUP_DIM
D_CONV = 256
D_MIX = D_ATTN + D_SGU + D_CONV
D_IN = D_ATTN + 2 * D_KV + 2 * D_SGU + 3 * D_CONV
N_EXPERT_GROUPS = 4
EXPERTS_PER_GROUP = 4
N_EXPERTS = N_EXPERT_GROUPS * EXPERTS_PER_GROUP
D_EXPERT = 256
D_GROUP_HID = EXPERTS_PER_GROUP * D_EXPERT
EPS = 1e-6

T_CTX = BATCH * SEQ
T_LAT = DEC_BATCH * DEC_SEQ
T_ALL = T_CTX + T_LAT
TM = 512
N_TILES = T_ALL // TM
N_CTX_TILES = T_CTX // TM
LAT_TILES_PER_BATCH = DEC_SEQ // TM
TM_MOE = 1024
LANES = 128
SUBLANES = 8
N_MOD_ROWS = 8
ROUTER_COLS = 128
VMEM_LIMIT = 48 * 1024 * 1024

OFF_Q = 0
OFF_KV = D_ATTN
OFF_SGU = OFF_KV + 2 * D_KV
OFF_CONV = OFF_SGU + 2 * D_SGU

BF16 = jnp.bfloat16
F32 = jnp.float32
NEG_BIG = -1e30


def _params(n_grid_dims):
    return pltpu.CompilerParams(
        dimension_semantics=("arbitrary",) * n_grid_dims,
        vmem_limit_bytes=VMEM_LIMIT,
    )


def _mod_row_of_tile(i):
    return jnp.where(i < N_CTX_TILES, 0, 1 + (i - N_CTX_TILES) // LAT_TILES_PER_BATCH)


def _rope_block_of_tile(i):
    return jnp.where(i < N_CTX_TILES, 0, 1 + (i - N_CTX_TILES) % LAT_TILES_PER_BATCH)


def _rms(x, gain):
    ms = jnp.mean(x * x, axis=-1, keepdims=True)
    return x * lax.rsqrt(ms + EPS) * gain


MOD_TN = 512


def _mod_kernel(ct_ref, w_ref, b_ref, o_ref):
    c = ct_ref[...]
    s = c / (1.0 + jnp.exp(-c))
    w = w_ref[...]
    row = lax.broadcasted_iota(jnp.int32, (N_MOD_ROWS, MOD_TN), 0)
    out = jnp.zeros((N_MOD_ROWS, MOD_TN), F32)
    for r in range(3):
        acc = jnp.sum(w * s[:, r:r + 1], axis=0, keepdims=True)
        out = jnp.where(row == r, acc, out)
    o_ref[...] = out + b_ref[...]


def _modulation(cvecs_t, w_ada, b_ada):
    n_col = 6 * D_MODEL
    return pl.pallas_call(
        _mod_kernel,
        grid=(DEPTH, n_col // MOD_TN),
        in_specs=[
            pl.BlockSpec((D_MODEL, N_MOD_ROWS), lambda l, j: (0, 0)),
            pl.BlockSpec((None, D_MODEL, MOD_TN), lambda l, j: (l, 0, j)),
            pl.BlockSpec((None, 1, MOD_TN), lambda l, j: (l, 0, j)),
        ],
        out_specs=pl.BlockSpec((None, N_MOD_ROWS, MOD_TN), lambda l, j: (l, 0, j)),
        out_shape=jax.ShapeDtypeStruct((DEPTH, N_MOD_ROWS, n_col), F32),
        compiler_params=_params(2),
        name="adaln_modulation",
    )(cvecs_t, w_ada, b_ada.reshape(DEPTH, 1, n_col))


def _swap16(x):
    lane = lax.broadcasted_iota(jnp.int32, x.shape, 1)
    up = pltpu.roll(x, 16, 1)
    down = pltpu.roll(x, LANES - 16, 1)
    return jnp.where((lane & 16) != 0, up, down)


def _rope(x, cos_t, sin_t):
    cols = []
    for j in range(x.shape[1] // LANES):
        xc = x[:, j * LANES:(j + 1) * LANES]
        cols.append(xc * cos_t + _swap16(xc) * sin_t)
    return cols[0] if len(cols) == 1 else jnp.concatenate(cols, axis=1)


def _proj_kernel(x_ref, mod_ref, g1_ref, w_ref, qg_ref, kg_ref, cos_ref, sin_ref, bq_ref,
                 sw_ref, sb_ref, on_ref,
                 qt_ref, k_ref, vt_ref, k32_ref, v32_ref, sgu_ref, cbz_ref):
    x = x_ref[...]
    mod = mod_ref[...]
    h = _rms(x, g1_ref[...]) * (1.0 + mod[1:2, :]) + mod[0:1, :]
    hb = h.astype(BF16)
    cos_t = cos_ref[...]
    sin_t = sin_ref[...]

    q = jnp.dot(hb, w_ref[:, OFF_Q:OFF_Q + D_ATTN], preferred_element_type=F32)
    qms = jnp.dot((q * q).astype(BF16), bq_ref[...], preferred_element_type=F32)
    qn = q * lax.rsqrt(qms + EPS) * qg_ref[...]
    qr = _rope(qn, cos_t, sin_t) * (HEAD_DIM ** -0.5)
    qt_ref[...] = qr.T.astype(BF16)

    kv = jnp.dot(hb, w_ref[:, OFF_KV:OFF_KV + 2 * D_KV], preferred_element_type=F32)
    k = kv[:, :D_KV]
    v = kv[:, D_KV:]
    kms = jnp.dot((k * k).astype(BF16), bq_ref[:D_KV, :D_KV], preferred_element_type=F32)
    kn = k * lax.rsqrt(kms + EPS) * kg_ref[...]
    k32_ref[...] = kn
    v32_ref[...] = v
    k_ref[...] = _rope(kn, cos_t, sin_t).astype(BF16)
    vt_ref[...] = v.T.astype(BF16)

    uv = jnp.dot(hb, w_ref[:, OFF_SGU:OFF_SGU + 2 * D_SGU], preferred_element_type=F32)
    su = uv[:, :D_SGU]
    sv = uv[:, D_SGU:]
    grp = lax.broadcasted_iota(jnp.int32, (CHUNK, D_SGU), 1) // SGU_GROUP_DIM
    on_sgu = on_ref[:, D_ATTN:D_ATTN + D_SGU]
    for n in range(TM // CHUNK):
        svc = sv[n * CHUNK:(n + 1) * CHUNK, :]
        mixed = sb_ref[...]
        for pair in range(N_SGU_GROUPS // 2):
            rhs = jnp.concatenate(
                [jnp.where(grp == 2 * pair, svc, 0.0), jnp.where(grp == 2 * pair + 1, svc, 0.0)],
                axis=0).astype(BF16)
            mixed = mixed + jnp.dot(sw_ref[pair], rhs, preferred_element_type=F32)
        sgu = su[n * CHUNK:(n + 1) * CHUNK, :] * mixed
        sgu_ref[n * CHUNK:(n + 1) * CHUNK, :] = _rms(sgu, on_sgu).astype(BF16)

    c3 = jnp.dot(hb, w_ref[:, OFF_CONV:OFF_CONV + 3 * D_CONV], preferred_element_type=F32)
    cbz_ref[:, :D_CONV] = c3[:, :D_CONV]
    cbz_ref[:, D_CONV:] = c3[:, D_CONV:2 * D_CONV] * c3[:, 2 * D_CONV:]


def _proj(l, x_all, mod, g1, w_in, qg, kg, cos_t, sin_t, bq, sgu_w, sgu_b, on):
    lay = lambda *rest: (lambda i: (l,) + rest)
    return pl.pallas_call(
        _proj_kernel,
        grid=(N_TILES,),
        in_specs=[
            pl.BlockSpec((TM, D_MODEL), lambda i: (i, 0)),
            pl.BlockSpec((None, None, 6, D_MODEL), lambda i: (l, _mod_row_of_tile(i), 0, 0)),
            pl.BlockSpec((None, 1, D_MODEL), lay(0, 0)),
            pl.BlockSpec((None, D_MODEL, D_IN), lay(0, 0)),
            pl.BlockSpec((None, 1, D_ATTN), lay(0, 0)),
            pl.BlockSpec((None, 1, D_KV), lay(0, 0)),
            pl.BlockSpec((TM, LANES), lambda i: (_rope_block_of_tile(i), 0)),
            pl.BlockSpec((TM, LANES), lambda i: (_rope_block_of_tile(i), 0)),
            pl.BlockSpec((D_ATTN, D_ATTN), lambda i: (0, 0)),
            pl.BlockSpec((None, N_SGU_GROUPS // 2, CHUNK, 2 * CHUNK), lay(0, 0, 0)),
            pl.BlockSpec((None, CHUNK, D_SGU), lay(0, 0)),
            pl.BlockSpec((None, 1, D_MIX), lay(0, 0)),
        ],
        out_specs=[
            pl.BlockSpec((None, D_ATTN, TM), lambda i: (i, 0, 0)),
            pl.BlockSpec((TM, D_KV), lambda i: (i, 0)),
            pl.BlockSpec((None, D_KV, TM), lambda i: (i, 0, 0)),
            pl.BlockSpec((TM, D_KV), lambda i: (i, 0)),
            pl.BlockSpec((TM, D_KV), lambda i: (i, 0)),
            pl.BlockSpec((TM, D_SGU), lambda i: (i, 0)),
            pl.BlockSpec((TM, 2 * D_CONV), lambda i: (i, 0)),
        ],
        out_shape=[
            jax.ShapeDtypeStruct((N_TILES, D_ATTN, TM), BF16),
            jax.ShapeDtypeStruct((T_ALL, D_KV), BF16),
            jax.ShapeDtypeStruct((N_TILES, D_KV, TM), BF16),
            jax.ShapeDtypeStruct((T_ALL, D_KV), F32),
            jax.ShapeDtypeStruct((T_ALL, D_KV), F32),
            jax.ShapeDtypeStruct((T_ALL, D_SGU), BF16),
            jax.ShapeDtypeStruct((T_ALL, 2 * D_CONV), F32),
        ],
        compiler_params=_params(1),
        name=f"proj_l{l}",
    )(x_all, mod, g1, w_in, qg, kg, cos_t, sin_t, bq, sgu_w, sgu_b, on)


def _pair_queries(qt, kv_idx, tq):
    qf = qt.astype(F32)
    mine = lax.broadcasted_iota(jnp.int32, (2 * HEAD_DIM, tq), 0) // HEAD_DIM == kv_idx
    qe = qf[:HEAD_DIM]
    qo = qf[HEAD_DIM:]
    rhs = jnp.concatenate(
        [jnp.where(mine, jnp.concatenate([qe, qe], axis=0), 0.0),
         jnp.where(mine, jnp.concatenate([qo, qo], axis=0), 0.0)], axis=1)
    return rhs.astype(BF16)


def _softmax_step(carry, k_c, vt_c, rhs):
    m, l, acc = carry
    st = jnp.dot(k_c, rhs, preferred_element_type=F32)
    m_new = jnp.maximum(m, jnp.max(st, axis=0, keepdims=True))
    alpha = jnp.exp(m - m_new)
    p = jnp.exp(st - m_new)
    l = alpha * l + jnp.sum(p, axis=0, keepdims=True)
    acc = alpha * acc + jnp.dot(vt_c, p.astype(BF16), preferred_element_type=F32)
    return m_new, l, acc


def _attn_finish(carry, tq):
    _, l, acc = carry
    o = acc * (1.0 / l)
    return jnp.concatenate([o[:, :tq], o[:, tq:]], axis=0)


def _attn_ctx_kernel(qt_ref, k_ref, vt_ref, o_ref):
    tq = SEQ
    k_c = k_ref[...]
    for pair in range(N_HEADS // 2):
        kv_idx = pair // (N_HEADS // 2 // N_KV_HEADS)
        rhs = _pair_queries(qt_ref[pair * LANES:(pair + 1) * LANES, :], kv_idx, tq)
        vt_c = vt_ref[kv_idx * HEAD_DIM:(kv_idx + 1) * HEAD_DIM, :]
        init = (jnp.full((1, 2 * tq), NEG_BIG, F32), jnp.zeros((1, 2 * tq), F32),
                jnp.zeros((HEAD_DIM, 2 * tq), F32))
        carry = _softmax_step(init, k_c, vt_c, rhs)
        o_ref[:, pair * LANES:(pair + 1) * LANES] = _attn_finish(carry, tq).T.astype(BF16)


def _attn_ctx(l, qt, k, vt):
    halves = TM // SEQ
    return pl.pallas_call(
        _attn_ctx_kernel,
        grid=(BATCH,),
        in_specs=[
            pl.BlockSpec((None, D_ATTN, SEQ), lambda b: (b // halves, 0, b % halves)),
            pl.BlockSpec((SEQ, D_KV), lambda b: (b, 0)),
            pl.BlockSpec((None, D_KV, SEQ), lambda b: (b // halves, 0, b % halves)),
        ],
        out_specs=pl.BlockSpec((SEQ, D_ATTN), lambda b: (b, 0)),
        out_shape=jax.ShapeDtypeStruct((T_CTX, D_ATTN), BF16),
        compiler_params=_params(1),
        name=f"attn_ctx_l{l}",
    )(qt, k, vt)


KC = 512


def _attn_lat_kernel(qt_ref, k_ref, vt_ref, ck_ref, cv_ref, o_ref):
    tq = TM
    pair = pl.program_id(2)
    kv_idx = pair // (N_HEADS // 2 // N_KV_HEADS)
    rhs = _pair_queries(qt_ref[...], kv_idx, tq)
    v_row = pl.multiple_of(kv_idx * HEAD_DIM, HEAD_DIM)

    def body(c, carry):
        k_c = k_ref[pl.ds(pl.multiple_of(c * KC, KC), KC), :]
        vt_c = vt_ref[c, pl.ds(v_row, HEAD_DIM), :]
        return _softmax_step(carry, k_c, vt_c, rhs)

    init = (jnp.full((1, 2 * tq), NEG_BIG, F32), jnp.zeros((1, 2 * tq), F32),
            jnp.zeros((HEAD_DIM, 2 * tq), F32))
    carry = lax.fori_loop(0, DEC_SEQ // KC, body, init)
    cvt = cv_ref[...].T
    cvt = jnp.where(kv_idx == 0, cvt[:HEAD_DIM], cvt[HEAD_DIM:]).astype(BF16)
    carry = _softmax_step(carry, ck_ref[...].astype(BF16), cvt, rhs)
    o_ref[...] = _attn_finish(carry, tq).T.astype(BF16)


def _attn_lat(l, qt, k, vt, cache_k, cache_v):
    n_q = DEC_SEQ // TM
    first = N_CTX_TILES
    ctx_blocks = T_CTX // DEC_SEQ
    return pl.pallas_call(
        _attn_lat_kernel,
        grid=(DEC_BATCH, n_q, N_HEADS // 2),
        in_specs=[
            pl.BlockSpec((None, LANES, TM), lambda b, i, p: (first + b * n_q + i, p, 0)),
            pl.BlockSpec((DEC_SEQ, D_KV), lambda b, i, p: (ctx_blocks + b, 0)),
            pl.BlockSpec((n_q, D_KV, TM), lambda b, i, p: (ctx_blocks + b, 0, 0)),
            pl.BlockSpec((None, None, PAST_LEN, D_KV), lambda b, i, p: (b, l, 0, 0)),
            pl.BlockSpec((None, None, PAST_LEN, D_KV), lambda b, i, p: (b, l, 0, 0)),
        ],
        out_specs=pl.BlockSpec((TM, LANES), lambda b, i, p: (b * n_q + i, p)),
        out_shape=jax.ShapeDtypeStruct((T_LAT, D_ATTN), BF16),
        compiler_params=_params(3),
        name=f"attn_lat_l{l}",
    )(qt, k, vt, cache_k, cache_v)


def _first_max_index(vals, valid, rowf):
    masked = jnp.where(valid, vals, -jnp.inf)
    mx = jnp.max(masked, axis=0, keepdims=True)
    idx = jnp.min(jnp.where(masked == mx, rowf, float(SUBLANES)), axis=0, keepdims=True)
    return mx, idx


def _merge_kernel(x_ref, actx_ref, alat_ref, sgu_ref, cbz_ref, prev_ref, next_ref, cw_ref, on_ref,
                  wo_ref, mod_ref, g2_ref, wr_ref, wrhi_ref, rb_ref,
                  x1_ref, h2_ref, gates_ref):
    i = pl.program_id(0)
    is_ctx = i < N_CTX_TILES
    mod = mod_ref[...]
    on = on_ref[...]

    cbz = cbz_ref[...]
    cb = cbz[:, :D_CONV]
    z = cbz[:, D_CONV:]
    seq_mask = jnp.where(is_ctx, SEQ - 1, DEC_SEQ - 1)
    row = lax.broadcasted_iota(jnp.int32, (TM, D_CONV), 0)
    pos = (i * TM + row) & seq_mask
    z_prev = jnp.where(row == 0, prev_ref[SUBLANES - 1:SUBLANES, D_CONV:], pltpu.roll(z, 1, 0))
    z_next = jnp.where(row == TM - 1, next_ref[0:1, D_CONV:], pltpu.roll(z, TM - 1, 0))
    z_prev = jnp.where(pos == 0, 0.0, z_prev)
    z_next = jnp.where(pos == seq_mask, 0.0, z_next)
    cw = cw_ref[...]
    conv = cb * (z_prev * cw[0:1, :] + z * cw[1:2, :] + z_next * cw[2:3, :])
    conv_n = _rms(conv, on[:, D_ATTN + D_SGU:]).astype(BF16)

    attn = jnp.where(is_ctx, actx_ref[...].astype(F32), alat_ref[...].astype(F32))
    attn_n = _rms(attn, on[:, :D_ATTN]).astype(BF16)
    merged = jnp.concatenate([attn_n, sgu_ref[...], conv_n], axis=1)
    x1 = x_ref[...] + mod[2:3, :] * jnp.dot(merged, wo_ref[...], preferred_element_type=F32)
    x1_ref[...] = x1

    h2 = _rms(x1, g2_ref[...]) * (1.0 + mod[4:5, :]) + mod[3:4, :]
    hb = h2.astype(BF16)
    h2_ref[...] = hb

    h_lo = (h2 - hb.astype(F32)).astype(BF16)
    a = jnp.dot(hb, wr_ref[...], preferred_element_type=F32)
    b = jnp.dot(h_lo, wrhi_ref[...], preferred_element_type=F32)
    logits = a[:, :ROUTER_COLS] + a[:, ROUTER_COLS:] + b + rb_ref[...]
    lt = logits.T

    rowi = lax.broadcasted_iota(jnp.int32, (SUBLANES, TM), 0).astype(F32)
    valid = rowi < float(N_EXPERT_GROUPS)
    g_log = lt[0:SUBLANES]
    g_max, g_idx = _first_max_index(g_log, valid, rowi)
    p_g = 1.0 / jnp.sum(jnp.where(valid, jnp.exp(g_log - g_max), 0.0), axis=0, keepdims=True)
    e_log = jnp.zeros((SUBLANES, TM), F32)
    for g in range(N_EXPERT_GROUPS):
        e_log = jnp.where(g_idx == g, lt[(g + 1) * SUBLANES:(g + 2) * SUBLANES], e_log)
    e_max, i1 = _first_max_index(e_log, valid, rowi)
    e_exp = jnp.where(valid, jnp.exp(e_log - e_max), 0.0)
    e_prob = e_exp / jnp.sum(e_exp, axis=0, keepdims=True)
    v1 = jnp.max(e_prob, axis=0, keepdims=True)
    rest = jnp.logical_and(valid, rowi != i1)
    v2, i2 = _first_max_index(e_prob, rest, rowi)
    denom = v1 + v2
    w_sel = p_g * jnp.where(rowi == i1, v1 / denom, jnp.where(rowi == i2, v2 / denom, 0.0))
    pieces = []
    pad = jnp.zeros((LANES - SUBLANES, TM), F32)
    for g in range(N_EXPERT_GROUPS):
        pieces += [jnp.where(g_idx == g, w_sel, 0.0), pad]
    gates_ref[...] = jnp.concatenate(pieces, axis=0).T


def _merge(l, x_all, attn_ctx, attn_lat, sgu_n, cbz, conv_w, on, w_out, mod, g2, wr_cat, wr_hi, rb):
    lay = lambda *rest: (lambda i: (l,) + rest)
    rows8 = TM // SUBLANES
    return pl.pallas_call(
        _merge_kernel,
        grid=(N_TILES,),
        in_specs=[
            pl.BlockSpec((TM, D_MODEL), lambda i: (i, 0)),
            pl.BlockSpec((TM, D_ATTN), lambda i: (jnp.minimum(i, N_CTX_TILES - 1), 0)),
            pl.BlockSpec((TM, D_ATTN), lambda i: (jnp.maximum(i - N_CTX_TILES, 0), 0)),
            pl.BlockSpec((TM, D_SGU), lambda i: (i, 0)),
            pl.BlockSpec((TM, 2 * D_CONV), lambda i: (i, 0)),
            pl.BlockSpec((SUBLANES, 2 * D_CONV), lambda i: (jnp.maximum(i * rows8 - 1, 0), 0)),
            pl.BlockSpec((SUBLANES, 2 * D_CONV),
                         lambda i: (jnp.minimum((i + 1) * rows8, T_ALL // SUBLANES - 1), 0)),
            pl.BlockSpec((None, 3, D_CONV), lay(0, 0)),
            pl.BlockSpec((None, 1, D_MIX), lay(0, 0)),
            pl.BlockSpec((None, D_MIX, D_MODEL), lay(0, 0)),
            pl.BlockSpec((None, None, 6, D_MODEL), lambda i: (l, _mod_row_of_tile(i), 0, 0)),
            pl.BlockSpec((None, 1, D_MODEL), lay(0, 0)),
            pl.BlockSpec((None, D_MODEL, 2 * ROUTER_COLS), lay(0, 0)),
            pl.BlockSpec((None, D_MODEL, ROUTER_COLS), lay(0, 0)),
            pl.BlockSpec((None, 1, ROUTER_COLS), lay(0, 0)),
        ],
        out_specs=[
            pl.BlockSpec((TM, D_MODEL), lambda i: (i, 0)),
            pl.BlockSpec((TM, D_MODEL), lambda i: (i, 0)),
            pl.BlockSpec((TM, N_EXPERT_GROUPS * LANES), lambda i: (i, 0)),
        ],
        out_shape=[
            jax.ShapeDtypeStruct((T_ALL, D_MODEL), F32),
            jax.ShapeDtypeStruct((T_ALL, D_MODEL), BF16),
            jax.ShapeDtypeStruct((T_ALL, N_EXPERT_GROUPS * LANES), F32),
        ],
        compiler_params=_params(1),
        name=f"merge_l{l}",
    )(x_all, attn_ctx, attn_lat, sgu_n, cbz, cbz, cbz, conv_w, on, w_out, mod, g2, wr_cat, wr_hi, rb)


MOE_ROWS = 512


def _moe_kernel(h_ref, gates_ref, w1_ref, w3_ref, w2_ref, x1_ref, mod_ref, fn_ref, o_ref, *, final):
    g = pl.program_id(1)
    gate2 = mod_ref[...][5:6, :]
    for r in range(TM_MOE // MOE_ROWS):
        rows = slice(r * MOE_ROWS, (r + 1) * MOE_ROWS)
        h = h_ref[rows, :]
        a = jnp.dot(h, w1_ref[...], preferred_element_type=F32)
        b = jnp.dot(h, w3_ref[...], preferred_element_type=F32)
        hid = a / (1.0 + jnp.exp(-a)) * b
        gates = gates_ref[rows, :]
        cols = []
        for e in range(EXPERTS_PER_GROUP):
            cols.append(hid[:, e * D_EXPERT:(e + 1) * D_EXPERT] * gates[:, e:e + 1])
        hid = jnp.concatenate(cols, axis=1).astype(BF16)
        y = gate2 * jnp.dot(hid, w2_ref[...], preferred_element_type=F32)

        @pl.when(g == 0)
        def _():
            o_ref[rows, :] = x1_ref[rows, :] + y

        @pl.when(g != 0)
        def _():
            o_ref[rows, :] = o_ref[rows, :] + y

    if final:
        @pl.when(g == N_EXPERT_GROUPS - 1)
        def _():
            o_ref[...] = _rms(o_ref[...], fn_ref[...])


def _moe(l, h2, gates, w1, w3, w2, x1, mod, final_norm, final):
    tiles_per_lat = DEC_SEQ // TM_MOE
    n_ctx = T_CTX // TM_MOE
    mod_row = lambda i: jnp.where(i < n_ctx, 0, 1 + (i - n_ctx) // tiles_per_lat)
    return pl.pallas_call(
        functools.partial(_moe_kernel, final=final),
        grid=(T_ALL // TM_MOE, N_EXPERT_GROUPS),
        in_specs=[
            pl.BlockSpec((TM_MOE, D_MODEL), lambda i, g: (i, 0)),
            pl.BlockSpec((TM_MOE, LANES), lambda i, g: (i, g)),
            pl.BlockSpec((None, None, D_MODEL, D_GROUP_HID), lambda i, g: (l, g, 0, 0)),
            pl.BlockSpec((None, None, D_MODEL, D_GROUP_HID), lambda i, g: (l, g, 0, 0)),
            pl.BlockSpec((None, None, D_GROUP_HID, D_MODEL), lambda i, g: (l, g, 0, 0)),
            pl.BlockSpec((TM_MOE, D_MODEL), lambda i, g: (i, 0)),
            pl.BlockSpec((None, None, 6, D_MODEL), lambda i, g: (l, mod_row(i), 0, 0)),
            pl.BlockSpec((1, D_MODEL), lambda i, g: (0, 0)),
        ],
        out_specs=pl.BlockSpec((TM_MOE, D_MODEL), lambda i, g: (i, 0)),
        out_shape=jax.ShapeDtypeStruct((T_ALL, D_MODEL), F32),
        compiler_params=_params(2),
        name=f"moe_l{l}",
    )(h2, gates, w1, w3, w2, x1, mod, final_norm)


def _rope_tables():
    rows = DEC_SEQ // GRID_W
    row = jnp.repeat(jnp.arange(rows, dtype=F32), GRID_W)
    col = jnp.tile(jnp.arange(GRID_W, dtype=F32), rows)
    half = ROPE_AXIS_DIM // 2
    inv_freq = 1.0 / (ROPE_THETA ** (jnp.arange(half, dtype=F32) * 2.0 / ROPE_AXIS_DIM))
    ar = row[:, None] * inv_freq
    ac = col[:, None] * inv_freq
    cos64 = jnp.concatenate([jnp.cos(ar), jnp.cos(ar), jnp.cos(ac), jnp.cos(ac)], axis=1)
    sin64 = jnp.concatenate([-jnp.sin(ar), jnp.sin(ar), -jnp.sin(ac), jnp.sin(ac)], axis=1)
    cos_t = jnp.concatenate([jnp.ones((TM, LANES), F32), jnp.tile(cos64, (1, 2))], axis=0)
    sin_t = jnp.concatenate([jnp.zeros((TM, LANES), F32), jnp.tile(sin64, (1, 2))], axis=0)
    return cos_t, sin_t


def _router_weights(router_g_w, router_g_b, router_e_w, router_e_b):
    w = jnp.zeros((DEPTH, D_MODEL, ROUTER_COLS), F32)
    b = jnp.zeros((DEPTH, 1, ROUTER_COLS), F32)
    w = w.at[:, :, 0:N_EXPERT_GROUPS].set(router_g_w)
    b = b.at[:, 0, 0:N_EXPERT_GROUPS].set(router_g_b)
    for g in range(N_EXPERT_GROUPS):
        lo = (g + 1) * SUBLANES
        w = w.at[:, :, lo:lo + EXPERTS_PER_GROUP].set(router_e_w[:, g])
        b = b.at[:, 0, lo:lo + EXPERTS_PER_GROUP].set(router_e_b[:, g])
    w_hi = w.astype(BF16)
    w_lo = (w - w_hi.astype(F32)).astype(BF16)
    return jnp.concatenate([w_hi, w_lo], axis=2), w_hi, b


def kernel(x_prompt, x_sample, cache_k, cache_v, c, c_ctx, w_ada, b_ada, norm1, w_in, q_norm, k_norm,
           sgu_w, sgu_b, conv_w, out_norm, w_out, norm2, router_g_w, router_g_b, router_e_w,
           router_e_b, moe_w1, moe_w3, moe_w2, final_norm):
    x_all = jnp.concatenate([x_prompt.reshape(T_CTX, D_MODEL), x_sample.reshape(T_LAT, D_MODEL)], axis=0)

    cvecs = jnp.concatenate([c_ctx[None, :], c, jnp.zeros((N_MOD_ROWS - 1 - DEC_BATCH, D_MODEL), F32)], axis=0)
    mod = _modulation(cvecs.T, w_ada, b_ada).reshape(DEPTH, N_MOD_ROWS, 6, D_MODEL)

    w_in_b = w_in.astype(BF16)
    w_out_b = w_out.astype(BF16)
    g1 = norm1.reshape(DEPTH, 1, D_MODEL)
    g2 = norm2.reshape(DEPTH, 1, D_MODEL)
    on = out_norm.reshape(DEPTH, 1, D_MIX)
    qg = jnp.tile(q_norm, (1, N_HEADS)).reshape(DEPTH, 1, D_ATTN)
    kg = jnp.tile(k_norm, (1, N_KV_HEADS)).reshape(DEPTH, 1, D_KV)
    cos_t, sin_t = _rope_tables()
    bq = jnp.asarray(np.kron(np.eye(N_HEADS), np.full((HEAD_DIM, HEAD_DIM), 1.0 / HEAD_DIM)), BF16)
    sgu_w_pairs = sgu_w.astype(BF16).reshape(DEPTH, N_SGU_GROUPS // 2, 2, CHUNK, CHUNK)
    sgu_w_pairs = jnp.concatenate([sgu_w_pairs[:, :, 0], sgu_w_pairs[:, :, 1]], axis=-1)
    sgu_bias = jnp.repeat(jnp.swapaxes(sgu_b, 1, 2), SGU_GROUP_DIM, axis=2)
    wr_cat, wr_hi, rb = _router_weights(router_g_w, router_g_b, router_e_w, router_e_b)
    w1 = moe_w1.astype(BF16).reshape(DEPTH, N_EXPERT_GROUPS, EXPERTS_PER_GROUP, D_MODEL, D_EXPERT)
    w1 = jnp.swapaxes(w1, 2, 3).reshape(DEPTH, N_EXPERT_GROUPS, D_MODEL, D_GROUP_HID)
    w3 = moe_w3.astype(BF16).reshape(DEPTH, N_EXPERT_GROUPS, EXPERTS_PER_GROUP, D_MODEL, D_EXPERT)
    w3 = jnp.swapaxes(w3, 2, 3).reshape(DEPTH, N_EXPERT_GROUPS, D_MODEL, D_GROUP_HID)
    w2 = moe_w2.astype(BF16).reshape(DEPTH, N_EXPERT_GROUPS, D_GROUP_HID, D_MODEL)
    fn = final_norm.reshape(1, D_MODEL)
    ck = cache_k.reshape(DEC_BATCH, DEPTH, PAST_LEN, D_KV)
    cv = cache_v.reshape(DEC_BATCH, DEPTH, PAST_LEN, D_KV)

    ctx_ks, ctx_vs = [], []
    for l in range(DEPTH):
        qt, k, vt, k32, v32, sgu_n, cbz = _proj(l, x_all, mod, g1, w_in_b, qg, kg, cos_t, sin_t, bq,
                                                sgu_w_pairs, sgu_bias, on)
        attn_ctx = _attn_ctx(l, qt, k, vt)
        attn_lat = _attn_lat(l, qt, k, vt, ck, cv)
        x1, h2, gates = _merge(l, x_all, attn_ctx, attn_lat, sgu_n, cbz, conv_w, on, w_out_b, mod, g2,
                               wr_cat, wr_hi, rb)
        x_all = _moe(l, h2, gates, w1, w3, w2, x1, mod, fn, final=(l == DEPTH - 1))
        ctx_ks.append(k32[:T_CTX].reshape(BATCH, SEQ, N_KV_HEADS, HEAD_DIM))
        ctx_vs.append(v32[:T_CTX].reshape(BATCH, SEQ, N_KV_HEADS, HEAD_DIM))

    y_prompt = x_all[:T_CTX].reshape(BATCH, SEQ, D_MODEL)
    y_sample = x_all[T_CTX:].reshape(DEC_BATCH, DEC_SEQ, D_MODEL)
    return (y_prompt, y_sample, jnp.stack(ctx_ks, axis=1), jnp.stack(ctx_vs, axis=1))
```

```python
import functools
import math

import jax
import jax.numpy as jnp
import numpy as np
from jax import lax
from jax.experimental import pallas as pl
from jax.experimental.pallas import tpu as pltpu

D_MODEL = 1024
BATCH = 16
SEQ = 256
DEPTH = 2
DEC_BATCH = 2
DEC_SEQ = 4096
PAST_LEN = 512
GRID_W = 64
N_HEADS = 8
N_KV_HEADS = 2
HEAD_DIM = 64
ROPE_AXIS_DIM = HEAD_DIM // 2
ROPE_THETA = 10000.0
D_ATTN = N_HEADS * HEAD_DIM
D_KV = N_KV_HEADS * HEAD_DIM
CHUNK = 128
N_SGU_GROUPS = 4
SGU_GROUP_DIM = 64
D_SGU = N_SGU_GROUPS * SGU_GROUP_DIM
D_CONV = 256
D_MIX = D_ATTN + D_SGU + D_CONV
D_IN = D_ATTN + 2 * D_KV + 2 * D_SGU + 3 * D_CONV
N_EXPERT_GROUPS = 4
EXPERTS_PER_GROUP = 4
N_EXPERTS = N_EXPERT_GROUPS * EXPERTS_PER_GROUP
D_EXPERT = 256
D_GROUP_HID = EXPERTS_PER_GROUP * D_EXPERT
EPS = 1e-6

T_CTX = BATCH * SEQ
T_LAT = DEC_BATCH * DEC_SEQ
T_ALL = T_CTX + T_LAT
TM = 512
N_TILES = T_ALL // TM
N_CTX_TILES = T_CTX // TM
LAT_TILES_PER_BATCH = DEC_SEQ // TM
TM_MOE = 1024
LANES = 128
SUBLANES = 8
BF16_ROWS = 16
N_MOD_ROWS = 8
N_MOD_PARTS = 6
ROUTER_COLS = 128
VMEM_LIMIT = 48 * 1024 * 1024

OFF_Q = 0
OFF_KV = D_ATTN
OFF_SGU = OFF_KV + 2 * D_KV
OFF_CONV = OFF_SGU + 2 * D_SGU

BF16 = jnp.bfloat16
F32 = jnp.float32
NEG_BIG = -1e30
Q_SCALE = HEAD_DIM ** -0.5 * math.log2(math.e)


def _params(n_grid_dims):
    return pltpu.CompilerParams(
        dimension_semantics=("arbitrary",) * n_grid_dims,
        vmem_limit_bytes=VMEM_LIMIT,
    )


def _mod_row_of_tile(i, tile=TM):
    n_ctx = T_CTX // tile
    return jnp.where(i < n_ctx, 0, 1 + (i - n_ctx) // (DEC_SEQ // tile))


def _mod_part(mod_ref, part, row):
    return mod_ref[part, pl.ds(row, 1), :]


def _rope_block_of_tile(i):
    return jnp.where(i < N_CTX_TILES, 0, 1 + (i - N_CTX_TILES) % LAT_TILES_PER_BATCH)


def _rms(x, gain):
    ms = jnp.mean(x * x, axis=-1, keepdims=True)
    return x * lax.rsqrt(ms + EPS) * gain


def _ctx_tile(i):
    return jnp.minimum(i, N_CTX_TILES - 1)


def _lat_tile(i):
    return jnp.maximum(i - N_CTX_TILES, 0)


MOD_TN = 512


def _mod_kernel(ct_ref, w_ref, b_ref, o_ref):
    c = ct_ref[...]
    s = c / (1.0 + jnp.exp(-c))
    w = w_ref[...]
    row = lax.broadcasted_iota(jnp.int32, (N_MOD_ROWS, MOD_TN), 0)
    out = jnp.zeros((N_MOD_ROWS, MOD_TN), F32)
    for r in range(1 + DEC_BATCH):
        acc = jnp.sum(w * s[:, r:r + 1], axis=0, keepdims=True)
        out = jnp.where(row == r, acc, out)
    o_ref[...] = out + b_ref[...]


def _modulation(cvecs_t, w_ada, b_ada):
    n_col = N_MOD_PARTS * D_MODEL
    per_part = D_MODEL // MOD_TN
    return pl.pallas_call(
        _mod_kernel,
        grid=(DEPTH, n_col // MOD_TN),
        in_specs=[
            pl.BlockSpec((D_MODEL, N_MOD_ROWS), lambda l, j: (0, 0)),
            pl.BlockSpec((None, D_MODEL, MOD_TN), lambda l, j: (l, 0, j)),
            pl.BlockSpec((None, 1, MOD_TN), lambda l, j: (l, 0, j)),
        ],
        out_specs=pl.BlockSpec((None, None, N_MOD_ROWS, MOD_TN),
                               lambda l, j: (l, j // per_part, 0, j % per_part)),
        out_shape=jax.ShapeDtypeStruct((DEPTH, N_MOD_PARTS, N_MOD_ROWS, D_MODEL), F32),
        compiler_params=_params(2),
        name="adaln_modulation",
    )(cvecs_t, w_ada, b_ada.reshape(DEPTH, 1, n_col))


_MOD_SPEC = lambda l, nd: pl.BlockSpec(
    (None, N_MOD_PARTS, N_MOD_ROWS, D_MODEL), lambda *idx: (l, 0, 0, 0))


def _swap16(x):
    lane = lax.broadcasted_iota(jnp.int32, x.shape, 1)
    up = pltpu.roll(x, 16, 1)
    down = pltpu.roll(x, LANES - 16, 1)
    return jnp.where((lane & 16) != 0, up, down)


def _rope(x, cos_t, sin_t):
    cols = []
    for j in range(x.shape[1] // LANES):
        xc = x[:, j * LANES:(j + 1) * LANES]
        cols.append(xc * cos_t + _swap16(xc) * sin_t)
    return cols[0] if len(cols) == 1 else jnp.concatenate(cols, axis=1)


def _proj_kernel(*refs, split_x):
    if split_x:
        xa_ref, xb_ref = refs[:2]
        refs = refs[2:]
    else:
        x_ref = refs[0]
        refs = refs[1:]
    (mod_ref, g1_ref, w_ref, qg_ref, kg_ref, cos_ref, sin_ref, bq_ref, sw_ref, sb_ref, on_ref,
     qt_ref, k_ref, vt_ref, k32_ref, v32_ref, sgu_ref, cbz_ref) = refs
    i = pl.program_id(0)
    is_ctx = i < N_CTX_TILES
    x = jnp.where(is_ctx, xa_ref[...], xb_ref[...]) if split_x else x_ref[...]
    mrow = _mod_row_of_tile(i)
    h = _rms(x, g1_ref[...]) * (1.0 + _mod_part(mod_ref, 1, mrow)) + _mod_part(mod_ref, 0, mrow)
    hb = h.astype(BF16)
    cos_t = cos_ref[...]
    sin_t = sin_ref[...]

    q = jnp.dot(hb, w_ref[:, OFF_Q:OFF_Q + D_ATTN], preferred_element_type=F32)
    qms = jnp.dot((q * q).astype(BF16), bq_ref[...], preferred_element_type=F32)
    qn = q * lax.rsqrt(qms + EPS) * qg_ref[...]
    qr = _rope(qn, cos_t, sin_t) * Q_SCALE
    qt_ref[...] = qr.T.astype(BF16)

    kv = jnp.dot(hb, w_ref[:, OFF_KV:OFF_KV + 2 * D_KV], preferred_element_type=F32)
    k = kv[:, :D_KV]
    v = kv[:, D_KV:]
    kms = jnp.dot((k * k).astype(BF16), bq_ref[:D_KV, :D_KV], preferred_element_type=F32)
    kn = k * lax.rsqrt(kms + EPS) * kg_ref[...]

    @pl.when(is_ctx)
    def _():
        k32_ref[...] = kn
        v32_ref[...] = v

    k_ref[...] = _rope(kn, cos_t, sin_t).astype(BF16)
    vt_ref[...] = v.T.astype(BF16)

    uv = jnp.dot(hb, w_ref[:, OFF_SGU:OFF_SGU + 2 * D_SGU], preferred_element_type=F32)
    su = uv[:, :D_SGU]
    sv = uv[:, D_SGU:]
    grp = lax.broadcasted_iota(jnp.int32, (CHUNK, D_SGU), 1) // SGU_GROUP_DIM
    on_sgu = on_ref[:, D_ATTN:D_ATTN + D_SGU]
    for n in range(TM // CHUNK):
        svc = sv[n * CHUNK:(n + 1) * CHUNK, :]
        mixed = sb_ref[...]
        for pair in range(N_SGU_GROUPS // 2):
            rhs = jnp.concatenate(
                [jnp.where(grp == 2 * pair, svc, 0.0), jnp.where(grp == 2 * pair + 1, svc, 0.0)],
                axis=0).astype(BF16)
            mixed = mixed + jnp.dot(sw_ref[pair], rhs, preferred_element_type=F32)
        sgu = su[n * CHUNK:(n + 1) * CHUNK, :] * mixed
        sgu_ref[n * CHUNK:(n + 1) * CHUNK, :] = _rms(sgu, on_sgu).astype(BF16)

    c3 = jnp.dot(hb, w_ref[:, OFF_CONV:OFF_CONV + 3 * D_CONV], preferred_element_type=F32)
    cbz_ref[:, :D_CONV] = c3[:, :D_CONV]
    cbz_ref[:, D_CONV:] = c3[:, D_CONV:2 * D_CONV] * c3[:, 2 * D_CONV:]


def _x_specs(xs):
    if len(xs) == 1:
        return [pl.BlockSpec((TM, D_MODEL), lambda i: (i, 0))]
    return [pl.BlockSpec((TM, D_MODEL), lambda i: (_ctx_tile(i), 0)),
            pl.BlockSpec((TM, D_MODEL), lambda i: (_lat_tile(i), 0))]


def _proj(l, xs, mod, g1, w_in, qg, kg, cos_t, sin_t, bq, sgu_w, sgu_b, on):
    lay = lambda *rest: (lambda i: (l,) + rest)
    return pl.pallas_call(
        functools.partial(_proj_kernel, split_x=len(xs) == 2),
        grid=(N_TILES,),
        in_specs=_x_specs(xs) + [
            _MOD_SPEC(l, 1),
            pl.BlockSpec((None, 1, D_MODEL), lay(0, 0)),
            pl.BlockSpec((None, D_MODEL, D_IN), lay(0, 0)),
            pl.BlockSpec((None, 1, D_ATTN), lay(0, 0)),
            pl.BlockSpec((None, 1, D_KV), lay(0, 0)),
            pl.BlockSpec((TM, LANES), lambda i: (_rope_block_of_tile(i), 0)),
            pl.BlockSpec((TM, LANES), lambda i: (_rope_block_of_tile(i), 0)),
            pl.BlockSpec((D_ATTN, D_ATTN), lambda i: (0, 0)),
            pl.BlockSpec((None, N_SGU_GROUPS // 2, CHUNK, 2 * CHUNK), lay(0, 0, 0)),
            pl.BlockSpec((None, CHUNK, D_SGU), lay(0, 0)),
            pl.BlockSpec((None, 1, D_MIX), lay(0, 0)),
        ],
        out_specs=[
            pl.BlockSpec((None, D_ATTN, TM), lambda i: (i, 0, 0)),
            pl.BlockSpec((TM, D_KV), lambda i: (i, 0)),
            pl.BlockSpec((None, D_KV, TM), lambda i: (i, 0, 0)),
            pl.BlockSpec((TM, D_KV), lambda i: (_ctx_tile(i), 0)),
            pl.BlockSpec((TM, D_KV), lambda i: (_ctx_tile(i), 0)),
            pl.BlockSpec((TM, D_SGU), lambda i: (i, 0)),
            pl.BlockSpec((TM, 2 * D_CONV), lambda i: (i, 0)),
        ],
        out_shape=[
            jax.ShapeDtypeStruct((N_TILES, D_ATTN, TM), BF16),
            jax.ShapeDtypeStruct((T_ALL, D_KV), BF16),
            jax.ShapeDtypeStruct((N_TILES, D_KV, TM), BF16),
            jax.ShapeDtypeStruct((T_CTX, D_KV), F32),
            jax.ShapeDtypeStruct((T_CTX, D_KV), F32),
            jax.ShapeDtypeStruct((T_ALL, D_SGU), BF16),
            jax.ShapeDtypeStruct((T_ALL, 2 * D_CONV), F32),
        ],
        compiler_params=_params(1),
        name=f"proj_l{l}",
    )(*xs, mod, g1, w_in, qg, kg, cos_t, sin_t, bq, sgu_w, sgu_b, on)


ACC_ROWS = HEAD_DIM + BF16_ROWS


def _pair_queries(qt, kv_idx, tq):
    qf = qt.astype(F32)
    mine = lax.broadcasted_iota(jnp.int32, (2 * HEAD_DIM, tq), 0) // HEAD_DIM == kv_idx
    qe = qf[:HEAD_DIM]
    qo = qf[HEAD_DIM:]
    rhs = jnp.concatenate(
        [jnp.where(mine, jnp.concatenate([qe, qe], axis=0), 0.0),
         jnp.where(mine, jnp.concatenate([qo, qo], axis=0), 0.0)], axis=1)
    return rhs.astype(BF16)


def _with_ones(vt_c):
    return jnp.concatenate([vt_c, jnp.ones((BF16_ROWS, vt_c.shape[1]), BF16)], axis=0)


def _flash_pair(rhs, chunks, tq):
    m = jnp.full((1, 2 * tq), NEG_BIG, F32)
    acc = jnp.zeros((ACC_ROWS, 2 * tq), F32)
    scores = lambda c: jnp.dot(chunks[c][0](), rhs, preferred_element_type=F32)
    st = scores(0)
    for c in range(len(chunks)):
        m_new = jnp.maximum(m, jnp.max(st, axis=0, keepdims=True))
        st_next = scores(c + 1) if c + 1 < len(chunks) else None
        alpha = jnp.exp2(m - m_new)
        p = jnp.exp2(st - m_new).astype(BF16)
        acc = alpha * acc + jnp.dot(chunks[c][1](), p, preferred_element_type=F32)
        m = m_new
        st = st_next
    o = acc[:HEAD_DIM] * (1.0 / acc[HEAD_DIM:HEAD_DIM + 1])
    return jnp.concatenate([o[:, :tq], o[:, tq:]], axis=0)


def _attn_ctx_kernel(qt_ref, k_ref, vt_ref, o_ref):
    tq = SEQ
    for pair in range(N_HEADS // 2):
        kv_idx = pair // (N_HEADS // 2 // N_KV_HEADS)
        rhs = _pair_queries(qt_ref[pair * LANES:(pair + 1) * LANES, :], kv_idx, tq)
        chunks = [(lambda: k_ref[...],
                   lambda: _with_ones(vt_ref[kv_idx * HEAD_DIM:(kv_idx + 1) * HEAD_DIM, :]))]
        o_ref[:, pair * LANES:(pair + 1) * LANES] = _flash_pair(rhs, chunks, tq).T.astype(BF16)


def _attn_ctx(l, qt, k, vt):
    halves = TM // SEQ
    return pl.pallas_call(
        _attn_ctx_kernel,
        grid=(BATCH,),
        in_specs=[
            pl.BlockSpec((None, D_ATTN, SEQ), lambda b: (b // halves, 0, b % halves)),
            pl.BlockSpec((SEQ, D_KV), lambda b: (b, 0)),
            pl.BlockSpec((None, D_KV, SEQ), lambda b: (b // halves, 0, b % halves)),
        ],
        out_specs=pl.BlockSpec((SEQ, D_ATTN), lambda b: (b, 0)),
        out_shape=jax.ShapeDtypeStruct((T_CTX, D_ATTN), BF16),
        compiler_params=_params(1),
        name=f"attn_ctx_l{l}",
    )(qt, k, vt)


KC = 512


def _attn_lat_kernel(qt_ref, k_ref, vt_ref, ck_ref, cv_ref, o_ref):
    tq = TM
    pair = pl.program_id(2)
    kv_idx = pair // (N_HEADS // 2 // N_KV_HEADS)
    rhs = _pair_queries(qt_ref[...], kv_idx, tq)
    v_row = pl.multiple_of(kv_idx * HEAD_DIM, HEAD_DIM)

    def latent_chunk(c):
        return (lambda: k_ref[c * KC:(c + 1) * KC, :],
                lambda: _with_ones(vt_ref[c, pl.ds(v_row, HEAD_DIM), :]))

    def cached_vt():
        cvt = cv_ref[...].T
        return _with_ones(jnp.where(kv_idx == 0, cvt[:HEAD_DIM], cvt[HEAD_DIM:]).astype(BF16))

    chunks = [latent_chunk(c) for c in range(DEC_SEQ // KC)]
    chunks.append((lambda: ck_ref[...].astype(BF16), cached_vt))
    o_ref[...] = _flash_pair(rhs, chunks, tq).T.astype(BF16)


def _attn_lat(l, qt, k, vt, cache_k, cache_v):
    n_q = DEC_SEQ // TM
    first = N_CTX_TILES
    ctx_blocks = T_CTX // DEC_SEQ
    return pl.pallas_call(
        _attn_lat_kernel,
        grid=(DEC_BATCH, n_q, N_HEADS // 2),
        in_specs=[
            pl.BlockSpec((None, LANES, TM), lambda b, i, p: (first + b * n_q + i, p, 0)),
            pl.BlockSpec((DEC_SEQ, D_KV), lambda b, i, p: (ctx_blocks + b, 0)),
            pl.BlockSpec((n_q, D_KV, TM), lambda b, i, p: (ctx_blocks + b, 0, 0)),
            pl.BlockSpec((None, None, PAST_LEN, D_KV), lambda b, i, p: (b, l, 0, 0)),
            pl.BlockSpec((None, None, PAST_LEN, D_KV), lambda b, i, p: (b, l, 0, 0)),
        ],
        out_specs=pl.BlockSpec((TM, LANES), lambda b, i, p: (b * n_q + i, p)),
        out_shape=jax.ShapeDtypeStruct((T_LAT, D_ATTN), BF16),
        compiler_params=_params(3),
        name=f"attn_lat_l{l}",
    )(qt, k, vt, cache_k, cache_v)


def _first_max_index(vals, valid, rowf):
    masked = jnp.where(valid, vals, -jnp.inf)
    mx = jnp.max(masked, axis=0, keepdims=True)
    idx = jnp.min(jnp.where(masked == mx, rowf, float(SUBLANES)), axis=0, keepdims=True)
    return mx, idx


def _merge_kernel(*refs, split_x):
    if split_x:
        xa_ref, xb_ref = refs[:2]
        refs = refs[2:]
    else:
        x_ref = refs[0]
        refs = refs[1:]
    (actx_ref, alat_ref, sgu_ref, cbz_ref, prev_ref, next_ref, cw_ref, on_ref,
     wo_ref, mod_ref, g2_ref, wr_ref, wrhi_ref, rb_ref, x1_ref, h2_ref, gates_ref) = refs
    i = pl.program_id(0)
    is_ctx = i < N_CTX_TILES
    mrow = _mod_row_of_tile(i)
    on = on_ref[...]

    cbz = cbz_ref[...]
    cb = cbz[:, :D_CONV]
    z = cbz[:, D_CONV:]
    seq_mask = jnp.where(is_ctx, SEQ - 1, DEC_SEQ - 1)
    row = lax.broadcasted_iota(jnp.int32, (TM, D_CONV), 0)
    pos = (i * TM + row) & seq_mask
    z_prev = jnp.where(row == 0, prev_ref[SUBLANES - 1:SUBLANES, D_CONV:], pltpu.roll(z, 1, 0))
    z_next = jnp.where(row == TM - 1, next_ref[0:1, D_CONV:], pltpu.roll(z, TM - 1, 0))
    z_prev = jnp.where(pos == 0, 0.0, z_prev)
    z_next = jnp.where(pos == seq_mask, 0.0, z_next)
    cw = cw_ref[...]
    conv = cb * (z_prev * cw[0:1, :] + z * cw[1:2, :] + z_next * cw[2:3, :])
    conv_n = _rms(conv, on[:, D_ATTN + D_SGU:]).astype(BF16)

    attn = jnp.where(is_ctx, actx_ref[...].astype(F32), alat_ref[...].astype(F32))
    attn_n = _rms(attn, on[:, :D_ATTN]).astype(BF16)
    merged = jnp.concatenate([attn_n, sgu_ref[...], conv_n], axis=1)
    x = jnp.where(is_ctx, xa_ref[...], xb_ref[...]) if split_x else x_ref[...]
    x1 = x + _mod_part(mod_ref, 2, mrow) * jnp.dot(merged, wo_ref[...], preferred_element_type=F32)
    x1_ref[...] = x1

    h2 = _rms(x1, g2_ref[...]) * (1.0 + _mod_part(mod_ref, 4, mrow)) + _mod_part(mod_ref, 3, mrow)
    hb = h2.astype(BF16)
    h2_ref[...] = hb

    h_lo = (h2 - hb.astype(F32)).astype(BF16)
    a = jnp.dot(hb, wr_ref[...], preferred_element_type=F32)
    b = jnp.dot(h_lo, wrhi_ref[...], preferred_element_type=F32)
    logits = a[:, :ROUTER_COLS] + a[:, ROUTER_COLS:] + b + rb_ref[...]
    lt = logits.T

    rowi = lax.broadcasted_iota(jnp.int32, (SUBLANES, TM), 0).astype(F32)
    valid = rowi < float(N_EXPERT_GROUPS)
    g_log = lt[0:SUBLANES]
    g_max, g_idx = _first_max_index(g_log, valid, rowi)
    p_g = 1.0 / jnp.sum(jnp.where(valid, jnp.exp(g_log - g_max), 0.0), axis=0, keepdims=True)
    e_log = jnp.zeros((SUBLANES, TM), F32)
    for g in range(N_EXPERT_GROUPS):
        e_log = jnp.where(g_idx == g, lt[(g + 1) * SUBLANES:(g + 2) * SUBLANES], e_log)
    e_max, i1 = _first_max_index(e_log, valid, rowi)
    e_exp = jnp.where(valid, jnp.exp(e_log - e_max), 0.0)
    e_prob = e_exp / jnp.sum(e_exp, axis=0, keepdims=True)
    v1 = jnp.max(e_prob, axis=0, keepdims=True)
    rest = jnp.logical_and(valid, rowi != i1)
    v2, i2 = _first_max_index(e_prob, rest, rowi)
    denom = v1 + v2
    w_sel = p_g * jnp.where(rowi == i1, v1 / denom, jnp.where(rowi == i2, v2 / denom, 0.0))
    pieces = []
    pad = jnp.zeros((LANES - SUBLANES, TM), F32)
    for g in range(N_EXPERT_GROUPS):
        pieces += [jnp.where(g_idx == g, w_sel, 0.0), pad]
    gates_ref[...] = jnp.concatenate(pieces, axis=0).T


def _merge(l, xs, attn_ctx, attn_lat, sgu_n, cbz, conv_w, on, w_out, mod, g2, wr_cat, wr_hi, rb):
    lay = lambda *rest: (lambda i: (l,) + rest)
    rows8 = TM // SUBLANES
    return pl.pallas_call(
        functools.partial(_merge_kernel, split_x=len(xs) == 2),
        grid=(N_TILES,),
        in_specs=_x_specs(xs) + [
            pl.BlockSpec((TM, D_ATTN), lambda i: (_ctx_tile(i), 0)),
            pl.BlockSpec((TM, D_ATTN), lambda i: (_lat_tile(i), 0)),
            pl.BlockSpec((TM, D_SGU), lambda i: (i, 0)),
            pl.BlockSpec((TM, 2 * D_CONV), lambda i: (i, 0)),
            pl.BlockSpec((SUBLANES, 2 * D_CONV), lambda i: (jnp.maximum(i * rows8 - 1, 0), 0)),
            pl.BlockSpec((SUBLANES, 2 * D_CONV),
                         lambda i: (jnp.minimum((i + 1) * rows8, T_ALL // SUBLANES - 1), 0)),
            pl.BlockSpec((None, 3, D_CONV), lay(0, 0)),
            pl.BlockSpec((None, 1, D_MIX), lay(0, 0)),
            pl.BlockSpec((None, D_MIX, D_MODEL), lay(0, 0)),
            _MOD_SPEC(l, 1),
            pl.BlockSpec((None, 1, D_MODEL), lay(0, 0)),
            pl.BlockSpec((None, D_MODEL, 2 * ROUTER_COLS), lay(0, 0)),
            pl.BlockSpec((None, D_MODEL, ROUTER_COLS), lay(0, 0)),
            pl.BlockSpec((None, 1, ROUTER_COLS), lay(0, 0)),
        ],
        out_specs=[
            pl.BlockSpec((TM, D_MODEL), lambda i: (i, 0)),
            pl.BlockSpec((TM, D_MODEL), lambda i: (i, 0)),
            pl.BlockSpec((TM, N_EXPERT_GROUPS * LANES), lambda i: (i, 0)),
        ],
        out_shape=[
            jax.ShapeDtypeStruct((T_ALL, D_MODEL), F32),
            jax.ShapeDtypeStruct((T_ALL, D_MODEL), BF16),
            jax.ShapeDtypeStruct((T_ALL, N_EXPERT_GROUPS * LANES), F32),
        ],
        compiler_params=_params(1),
        name=f"merge_l{l}",
    )(*xs, attn_ctx, attn_lat, sgu_n, cbz, cbz, cbz, conv_w, on, w_out, mod, g2, wr_cat, wr_hi, rb)


MOE_ROWS = 512
N_CTX_MOE_TILES = T_CTX // TM_MOE


def _moe_kernel(h_ref, gates_ref, w1_ref, w3_ref, w2_ref, x1_ref, mod_ref, fn_ref, *out_refs, final):
    i = pl.program_id(0)
    g = pl.program_id(1)
    gate2 = _mod_part(mod_ref, 5, _mod_row_of_tile(i, TM_MOE))
    if final:
        octx_ref, olat_ref, acc_ref = out_refs
    else:
        acc_ref, = out_refs
    for r in range(TM_MOE // MOE_ROWS):
        rows = slice(r * MOE_ROWS, (r + 1) * MOE_ROWS)
        h = h_ref[rows, :]
        gates = gates_ref[rows, :]
        cols = []
        for e in range(EXPERTS_PER_GROUP):
            a = jnp.dot(h, w1_ref[e], preferred_element_type=F32)
            b = jnp.dot(h, w3_ref[e], preferred_element_type=F32)
            cols.append((a / (1.0 + jnp.exp(-a)) * b * gates[:, e:e + 1]).astype(BF16))
        hid = jnp.concatenate(cols, axis=1)
        y = gate2 * jnp.dot(hid, w2_ref[...], preferred_element_type=F32)

        @pl.when(g == 0)
        def _():
            acc_ref[rows, :] = x1_ref[rows, :] + y

        @pl.when(g != 0)
        def _():
            acc_ref[rows, :] = acc_ref[rows, :] + y

    if final:
        last = g == N_EXPERT_GROUPS - 1

        @pl.when(jnp.logical_and(last, i < N_CTX_MOE_TILES))
        def _():
            octx_ref[...] = _rms(acc_ref[...], fn_ref[...])

        @pl.when(jnp.logical_and(last, i >= N_CTX_MOE_TILES))
        def _():
            olat_ref[...] = _rms(acc_ref[...], fn_ref[...])


def _moe(l, h2, gates, w1, w3, w2, x1, mod, final_norm, final):
    tile = pl.BlockSpec((TM_MOE, D_MODEL), lambda i, g: (i, 0))
    if final:
        out_specs = [
            pl.BlockSpec((TM_MOE, D_MODEL), lambda i, g: (jnp.minimum(i, N_CTX_MOE_TILES - 1), 0)),
            pl.BlockSpec((TM_MOE, D_MODEL), lambda i, g: (jnp.maximum(i - N_CTX_MOE_TILES, 0), 0)),
        ]
        out_shape = [jax.ShapeDtypeStruct((T_CTX, D_MODEL), F32), jax.ShapeDtypeStruct((T_LAT, D_MODEL), F32)]
        scratch = [pltpu.VMEM((TM_MOE, D_MODEL), F32)]
    else:
        out_specs = tile
        out_shape = jax.ShapeDtypeStruct((T_ALL, D_MODEL), F32)
        scratch = []
    return pl.pallas_call(
        functools.partial(_moe_kernel, final=final),
        grid=(T_ALL // TM_MOE, N_EXPERT_GROUPS),
        in_specs=[
            tile,
            pl.BlockSpec((TM_MOE, LANES), lambda i, g: (i, g)),
            pl.BlockSpec((None, EXPERTS_PER_GROUP, D_MODEL, D_EXPERT), lambda i, g: (l, g, 0, 0)),
            pl.BlockSpec((None, EXPERTS_PER_GROUP, D_MODEL, D_EXPERT), lambda i, g: (l, g, 0, 0)),
            pl.BlockSpec((None, None, D_GROUP_HID, D_MODEL), lambda i, g: (l, g, 0, 0)),
            tile,
            _MOD_SPEC(l, 2),
            pl.BlockSpec((1, D_MODEL), lambda i, g: (0, 0)),
        ],
        out_specs=out_specs,
        out_shape=out_shape,
        scratch_shapes=scratch,
        compiler_params=_params(2),
        name=f"moe_l{l}",
    )(h2, gates, w1, w3, w2, x1, mod, final_norm)


def _rope_tables():
    rows = DEC_SEQ // GRID_W
    row = np.repeat(np.arange(rows, dtype=np.float32), GRID_W)
    col = np.tile(np.arange(GRID_W, dtype=np.float32), rows)
    half = ROPE_AXIS_DIM // 2
    inv_freq = (1.0 / (np.float32(ROPE_THETA) ** (np.arange(half, dtype=np.float32) * np.float32(2.0)
                                                  / np.float32(ROPE_AXIS_DIM)))).astype(np.float32)
    ar = row[:, None] * inv_freq
    ac = col[:, None] * inv_freq
    cos64 = np.concatenate([np.cos(ar), np.cos(ar), np.cos(ac), np.cos(ac)], axis=1)
    sin64 = np.concatenate([-np.sin(ar), np.sin(ar), -np.sin(ac), np.sin(ac)], axis=1)
    cos_t = np.concatenate([np.ones((TM, LANES), np.float32), np.tile(cos64, (1, 2))], axis=0)
    sin_t = np.concatenate([np.zeros((TM, LANES), np.float32), np.tile(sin64, (1, 2))], axis=0)
    return jnp.asarray(cos_t, F32), jnp.asarray(sin_t, F32)


def _router_weights(router_g_w, router_g_b, router_e_w, router_e_b):
    zw = jnp.zeros((DEPTH, D_MODEL, SUBLANES - N_EXPERT_GROUPS), F32)
    zb = jnp.zeros((DEPTH, SUBLANES - N_EXPERT_GROUPS), F32)
    w_cols, b_cols = [router_g_w, zw], [router_g_b, zb]
    for g in range(N_EXPERT_GROUPS):
        w_cols += [router_e_w[:, g], zw]
        b_cols += [router_e_b[:, g], zb]
    used = (1 + N_EXPERT_GROUPS) * SUBLANES
    w_cols.append(jnp.zeros((DEPTH, D_MODEL, ROUTER_COLS - used), F32))
    b_cols.append(jnp.zeros((DEPTH, ROUTER_COLS - used), F32))
    w = jnp.concatenate(w_cols, axis=2)
    b = jnp.concatenate(b_cols, axis=1).reshape(DEPTH, 1, ROUTER_COLS)
    w_hi = w.astype(BF16)
    w_lo = (w - w_hi.astype(F32)).astype(BF16)
    return jnp.concatenate([w_hi, w_lo], axis=2), w_hi, b


def kernel(x_prompt, x_sample, cache_k, cache_v, c, c_ctx, w_ada, b_ada, norm1, w_in, q_norm, k_norm,
           sgu_w, sgu_b, conv_w, out_norm, w_out, norm2, router_g_w, router_g_b, router_e_w,
           router_e_b, moe_w1, moe_w3, moe_w2, final_norm):
    xs = (x_prompt.reshape(T_CTX, D_MODEL), x_sample.reshape(T_LAT, D_MODEL))

    cvecs = jnp.concatenate([c_ctx[None, :], c, jnp.zeros((N_MOD_ROWS - 1 - DEC_BATCH, D_MODEL), F32)], axis=0)
    mod = _modulation(cvecs.T, w_ada, b_ada)

    w_in_b = w_in.astype(BF16)
    w_out_b = w_out.astype(BF16)
    g1 = norm1.reshape(DEPTH, 1, D_MODEL)
    g2 = norm2.reshape(DEPTH, 1, D_MODEL)
    on = out_norm.reshape(DEPTH, 1, D_MIX)
    qg = jnp.tile(q_norm, (1, N_HEADS)).reshape(DEPTH, 1, D_ATTN)
    kg = jnp.tile(k_norm, (1, N_KV_HEADS)).reshape(DEPTH, 1, D_KV)
    cos_t, sin_t = _rope_tables()
    bq = jnp.asarray(np.kron(np.eye(N_HEADS), np.full((HEAD_DIM, HEAD_DIM), 1.0 / HEAD_DIM)), BF16)
    sgu_w_pairs = sgu_w.astype(BF16).reshape(DEPTH, N_SGU_GROUPS // 2, 2, CHUNK, CHUNK)
    sgu_w_pairs = jnp.concatenate([sgu_w_pairs[:, :, 0], sgu_w_pairs[:, :, 1]], axis=-1)
    sgu_bias = jnp.repeat(jnp.swapaxes(sgu_b, 1, 2), SGU_GROUP_DIM, axis=2)
    wr_cat, wr_hi, rb = _router_weights(router_g_w, router_g_b, router_e_w, router_e_b)
    w1 = moe_w1.astype(BF16)
    w3 = moe_w3.astype(BF16)
    w2 = moe_w2.astype(BF16).reshape(DEPTH, N_EXPERT_GROUPS, D_GROUP_HID, D_MODEL)
    fn = final_norm.reshape(1, D_MODEL)
    ck = cache_k.reshape(DEC_BATCH, DEPTH, PAST_LEN, D_KV)
    cv = cache_v.reshape(DEC_BATCH, DEPTH, PAST_LEN, D_KV)

    ctx_ks, ctx_vs = [], []
    for l in range(DEPTH):
        qt, k, vt, k32, v32, sgu_n, cbz = _proj(l, xs, mod, g1, w_in_b, qg, kg, cos_t, sin_t, bq,
                                                sgu_w_pairs, sgu_bias, on)
        attn_ctx = _attn_ctx(l, qt, k, vt)
        attn_lat = _attn_lat(l, qt, k, vt, ck, cv)
        x1, h2, gates = _merge(l, xs, attn_ctx, attn_lat, sgu_n, cbz, conv_w, on, w_out_b, mod, g2,
                               wr_cat, wr_hi, rb)
        out = _moe(l, h2, gates, w1, w3, w2, x1, mod, fn, final=(l == DEPTH - 1))
        xs = (out,)
        ctx_ks.append(k32.reshape(BATCH, SEQ, N_KV_HEADS, HEAD_DIM))
        ctx_vs.append(v32.reshape(BATCH, SEQ, N_KV_HEADS, HEAD_DIM))

    y_prompt, y_sample = out
    return (y_prompt.reshape(BATCH, SEQ, D_MODEL), y_sample.reshape(DEC_BATCH, DEC_SEQ, D_MODEL),
            jnp.stack(ctx_ks, axis=1), jnp.stack(ctx_vs, axis=1))
```

```python
import functools
import math

import jax
import jax.numpy as jnp
import numpy as np
from jax import lax
from jax.experimental import pallas as pl
from jax.experimental.pallas import tpu as pltpu

D_MODEL = 1024
BATCH = 16
SEQ = 256
DEPTH = 2
DEC_BATCH = 2
DEC_SEQ = 4096
PAST_LEN = 512
GRID_W = 64
N_HEADS = 8
N_KV_HEADS = 2
HEAD_DIM = 64
ROPE_AXIS_DIM = HEAD_DIM // 2
ROPE_THETA = 10000.0
D_ATTN = N_HEADS * HEAD_DIM
D_KV = N_KV_HEADS * HEAD_DIM
CHUNK = 128
N_SGU_GROUPS = 4
SGU_GROUP_DIM = 64
D_SGU = N_SGU_GROUPS * SGU_GROUP_DIM
D_CONV = 256
D_MIX = D_ATTN + D_SGU + D_CONV
D_IN = D_ATTN + 2 * D_KV + 2 * D_SGU + 3 * D_CONV
N_EXPERT_GROUPS = 4
EXPERTS_PER_GROUP = 4
N_EXPERTS = N_EXPERT_GROUPS * EXPERTS_PER_GROUP
D_EXPERT = 256
D_GROUP_HID = EXPERTS_PER_GROUP * D_EXPERT
EPS = 1e-6

T_CTX = BATCH * SEQ
T_LAT = DEC_BATCH * DEC_SEQ
T_ALL = T_CTX + T_LAT
TM = 512
N_TILES = T_ALL // TM
N_CTX_TILES = T_CTX // TM
LAT_TILES_PER_BATCH = DEC_SEQ // TM
TM_MOE = 1024
LANES = 128
SUBLANES = 8
BF16_ROWS = 16
N_MOD_ROWS = 8
N_MOD_PARTS = 6
ROUTER_COLS = 128
VMEM_LIMIT = 48 * 1024 * 1024

OFF_Q = 0
OFF_KV = D_ATTN
OFF_SGU = OFF_KV + 2 * D_KV
OFF_CONV = OFF_SGU + 2 * D_SGU

BF16 = jnp.bfloat16
F32 = jnp.float32
NEG_BIG = -1e30
Q_SCALE = HEAD_DIM ** -0.5 * math.log2(math.e)


def _params(n_grid_dims):
    return pltpu.CompilerParams(
        dimension_semantics=("arbitrary",) * n_grid_dims,
        vmem_limit_bytes=VMEM_LIMIT,
    )


def _mod_row_of_tile(i, tile=TM):
    n_ctx = T_CTX // tile
    return jnp.where(i < n_ctx, 0, 1 + (i - n_ctx) // (DEC_SEQ // tile))


def _mod_part(mod_ref, part, row):
    return mod_ref[part, pl.ds(row, 1), :]


def _rope_block_of_tile(i):
    return jnp.where(i < N_CTX_TILES, 0, 1 + (i - N_CTX_TILES) % LAT_TILES_PER_BATCH)


def _rms(x, gain):
    ms = jnp.mean(x * x, axis=-1, keepdims=True)
    return x * lax.rsqrt(ms + EPS) * gain


def _ctx_tile(i):
    return jnp.minimum(i, N_CTX_TILES - 1)


def _lat_tile(i):
    return jnp.maximum(i - N_CTX_TILES, 0)


MOD_TN = 512


def _mod_kernel(ct_ref, w_ref, b_ref, o_ref):
    c = ct_ref[...]
    s = c / (1.0 + jnp.exp(-c))
    w = w_ref[...]
    row = lax.broadcasted_iota(jnp.int32, (N_MOD_ROWS, MOD_TN), 0)
    out = jnp.zeros((N_MOD_ROWS, MOD_TN), F32)
    for r in range(1 + DEC_BATCH):
        acc = jnp.sum(w * s[:, r:r + 1], axis=0, keepdims=True)
        out = jnp.where(row == r, acc, out)
    o_ref[...] = out + b_ref[...]


def _modulation(cvecs_t, w_ada, b_ada):
    n_col = N_MOD_PARTS * D_MODEL
    per_part = D_MODEL // MOD_TN
    return pl.pallas_call(
        _mod_kernel,
        grid=(DEPTH, n_col // MOD_TN),
        in_specs=[
            pl.BlockSpec((D_MODEL, N_MOD_ROWS), lambda l, j: (0, 0)),
            pl.BlockSpec((None, D_MODEL, MOD_TN), lambda l, j: (l, 0, j)),
            pl.BlockSpec((None, 1, MOD_TN), lambda l, j: (l, 0, j)),
        ],
        out_specs=pl.BlockSpec((None, None, N_MOD_ROWS, MOD_TN),
                               lambda l, j: (l, j // per_part, 0, j % per_part)),
        out_shape=jax.ShapeDtypeStruct((DEPTH, N_MOD_PARTS, N_MOD_ROWS, D_MODEL), F32),
        compiler_params=_params(2),
        name="adaln_modulation",
    )(cvecs_t, w_ada, b_ada.reshape(DEPTH, 1, n_col))


_MOD_SPEC = lambda l, nd: pl.BlockSpec(
    (None, N_MOD_PARTS, N_MOD_ROWS, D_MODEL), lambda *idx: (l, 0, 0, 0))


def _swap16(x):
    lane = lax.broadcasted_iota(jnp.int32, x.shape, 1)
    up = pltpu.roll(x, 16, 1)
    down = pltpu.roll(x, LANES - 16, 1)
    return jnp.where((lane & 16) != 0, up, down)


def _rope(x, cos_t, sin_t):
    cols = []
    for j in range(x.shape[1] // LANES):
        xc = x[:, j * LANES:(j + 1) * LANES]
        cols.append(xc * cos_t + _swap16(xc) * sin_t)
    return cols[0] if len(cols) == 1 else jnp.concatenate(cols, axis=1)


def _proj_kernel(*refs, split_x):
    if split_x:
        xa_ref, xb_ref = refs[:2]
        refs = refs[2:]
    else:
        x_ref = refs[0]
        refs = refs[1:]
    (mod_ref, g1_ref, w_ref, qg_ref, kg_ref, cos_ref, sin_ref, bq_ref, sw_ref, sb_ref, on_ref,
     qt_ref, k_ref, vt_ref, k32_ref, v32_ref, sgu_ref, cbz_ref) = refs
    i = pl.program_id(0)
    is_ctx = i < N_CTX_TILES
    x = jnp.where(is_ctx, xa_ref[...], xb_ref[...]) if split_x else x_ref[...]
    mrow = _mod_row_of_tile(i)
    h = _rms(x, g1_ref[...]) * (1.0 + _mod_part(mod_ref, 1, mrow)) + _mod_part(mod_ref, 0, mrow)
    hb = h.astype(BF16)
    cos_t = cos_ref[...]
    sin_t = sin_ref[...]

    q = jnp.dot(hb, w_ref[:, OFF_Q:OFF_Q + D_ATTN], preferred_element_type=F32)
    qms = jnp.dot((q * q).astype(BF16), bq_ref[...], preferred_element_type=F32)
    qn = q * lax.rsqrt(qms + EPS) * qg_ref[...]
    qr = _rope(qn, cos_t, sin_t) * Q_SCALE
    qt_ref[...] = qr.T.astype(BF16)

    kv = jnp.dot(hb, w_ref[:, OFF_KV:OFF_KV + 2 * D_KV], preferred_element_type=F32)
    k = kv[:, :D_KV]
    v = kv[:, D_KV:]
    kms = jnp.dot((k * k).astype(BF16), bq_ref[:D_KV, :D_KV], preferred_element_type=F32)
    kn = k * lax.rsqrt(kms + EPS) * kg_ref[...]

    @pl.when(is_ctx)
    def _():
        k32_ref[...] = kn
        v32_ref[...] = v

    k_ref[...] = _rope(kn, cos_t, sin_t).astype(BF16)
    vt_ref[...] = v.T.astype(BF16)

    uv = jnp.dot(hb, w_ref[:, OFF_SGU:OFF_SGU + 2 * D_SGU], preferred_element_type=F32)
    su = uv[:, :D_SGU]
    sv = uv[:, D_SGU:]
    grp = lax.broadcasted_iota(jnp.int32, (CHUNK, D_SGU), 1) // SGU_GROUP_DIM
    on_sgu = on_ref[:, D_ATTN:D_ATTN + D_SGU]
    for n in range(TM // CHUNK):
        svc = sv[n * CHUNK:(n + 1) * CHUNK, :]
        mixed = sb_ref[...]
        for pair in range(N_SGU_GROUPS // 2):
            rhs = jnp.concatenate(
                [jnp.where(grp == 2 * pair, svc, 0.0), jnp.where(grp == 2 * pair + 1, svc, 0.0)],
                axis=0).astype(BF16)
            mixed = mixed + jnp.dot(sw_ref[pair], rhs, preferred_element_type=F32)
        sgu = su[n * CHUNK:(n + 1) * CHUNK, :] * mixed
        sgu_ref[n * CHUNK:(n + 1) * CHUNK, :] = _rms(sgu, on_sgu).astype(BF16)

    c3 = jnp.dot(hb, w_ref[:, OFF_CONV:OFF_CONV + 3 * D_CONV], preferred_element_type=F32)
    cbz_ref[:, :D_CONV] = c3[:, :D_CONV]
    cbz_ref[:, D_CONV:] = c3[:, D_CONV:2 * D_CONV] * c3[:, 2 * D_CONV:]


def _x_specs(xs):
    if len(xs) == 1:
        return [pl.BlockSpec((TM, D_MODEL), lambda i: (i, 0))]
    return [pl.BlockSpec((TM, D_MODEL), lambda i: (_ctx_tile(i), 0)),
            pl.BlockSpec((TM, D_MODEL), lambda i: (_lat_tile(i), 0))]


def _proj(l, xs, mod, g1, w_in, qg, kg, cos_t, sin_t, bq, sgu_w, sgu_b, on):
    lay = lambda *rest: (lambda i: (l,) + rest)
    return pl.pallas_call(
        functools.partial(_proj_kernel, split_x=len(xs) == 2),
        grid=(N_TILES,),
        in_specs=_x_specs(xs) + [
            _MOD_SPEC(l, 1),
            pl.BlockSpec((None, 1, D_MODEL), lay(0, 0)),
            pl.BlockSpec((None, D_MODEL, D_IN), lay(0, 0)),
            pl.BlockSpec((None, 1, D_ATTN), lay(0, 0)),
            pl.BlockSpec((None, 1, D_KV), lay(0, 0)),
            pl.BlockSpec((TM, LANES), lambda i: (_rope_block_of_tile(i), 0)),
            pl.BlockSpec((TM, LANES), lambda i: (_rope_block_of_tile(i), 0)),
            pl.BlockSpec((D_ATTN, D_ATTN), lambda i: (0, 0)),
            pl.BlockSpec((None, N_SGU_GROUPS // 2, CHUNK, 2 * CHUNK), lay(0, 0, 0)),
            pl.BlockSpec((None, CHUNK, D_SGU), lay(0, 0)),
            pl.BlockSpec((None, 1, D_MIX), lay(0, 0)),
        ],
        out_specs=[
            pl.BlockSpec((None, D_ATTN, TM), lambda i: (i, 0, 0)),
            pl.BlockSpec((TM, D_KV), lambda i: (i, 0)),
            pl.BlockSpec((None, D_KV, TM), lambda i: (i, 0, 0)),
            pl.BlockSpec((TM, D_KV), lambda i: (_ctx_tile(i), 0)),
            pl.BlockSpec((TM, D_KV), lambda i: (_ctx_tile(i), 0)),
            pl.BlockSpec((TM, D_SGU), lambda i: (i, 0)),
            pl.BlockSpec((TM, 2 * D_CONV), lambda i: (i, 0)),
        ],
        out_shape=[
            jax.ShapeDtypeStruct((N_TILES, D_ATTN, TM), BF16),
            jax.ShapeDtypeStruct((T_ALL, D_KV), BF16),
            jax.ShapeDtypeStruct((N_TILES, D_KV, TM), BF16),
            jax.ShapeDtypeStruct((T_CTX, D_KV), F32),
            jax.ShapeDtypeStruct((T_CTX, D_KV), F32),
            jax.ShapeDtypeStruct((T_ALL, D_SGU), BF16),
            jax.ShapeDtypeStruct((T_ALL, 2 * D_CONV), F32),
        ],
        compiler_params=_params(1),
        name=f"proj_l{l}",
    )(*xs, mod, g1, w_in, qg, kg, cos_t, sin_t, bq, sgu_w, sgu_b, on)


ACC_ROWS = HEAD_DIM + BF16_ROWS


def _pair_queries(qt, kv_idx, tq):
    qf = qt.astype(F32)
    mine = lax.broadcasted_iota(jnp.int32, (2 * HEAD_DIM, tq), 0) // HEAD_DIM == kv_idx
    qe = qf[:HEAD_DIM]
    qo = qf[HEAD_DIM:]
    rhs = jnp.concatenate(
        [jnp.where(mine, jnp.concatenate([qe, qe], axis=0), 0.0),
         jnp.where(mine, jnp.concatenate([qo, qo], axis=0), 0.0)], axis=1)
    return rhs.astype(BF16)


def _with_ones(vt_c):
    return jnp.concatenate([vt_c, jnp.ones((BF16_ROWS, vt_c.shape[1]), BF16)], axis=0)


def _flash_pair(rhs, chunks, tq):
    m = jnp.full((1, 2 * tq), NEG_BIG, F32)
    acc = jnp.zeros((ACC_ROWS, 2 * tq), F32)
    scores = lambda c: jnp.dot(chunks[c][0](), rhs, preferred_element_type=F32)
    st = scores(0)
    for c in range(len(chunks)):
        m_new = jnp.maximum(m, jnp.max(st, axis=0, keepdims=True))
        st_next = scores(c + 1) if c + 1 < len(chunks) else None
        alpha = jnp.exp2(m - m_new)
        p = jnp.exp2(st - m_new).astype(BF16)
        acc = alpha * acc + jnp.dot(chunks[c][1](), p, preferred_element_type=F32)
        m = m_new
        st = st_next
    o = acc[:HEAD_DIM] * (1.0 / acc[HEAD_DIM:HEAD_DIM + 1])
    return jnp.concatenate([o[:, :tq], o[:, tq:]], axis=0)


def _attn_ctx_kernel(qt_ref, k_ref, vt_ref, o_ref):
    tq = SEQ
    for pair in range(N_HEADS // 2):
        kv_idx = pair // (N_HEADS // 2 // N_KV_HEADS)
        rhs = _pair_queries(qt_ref[pair * LANES:(pair + 1) * LANES, :], kv_idx, tq)
        chunks = [(lambda: k_ref[...],
                   lambda: _with_ones(vt_ref[kv_idx * HEAD_DIM:(kv_idx + 1) * HEAD_DIM, :]))]
        o_ref[:, pair * LANES:(pair + 1) * LANES] = _flash_pair(rhs, chunks, tq).T.astype(BF16)


def _attn_ctx(l, qt, k, vt):
    halves = TM // SEQ
    return pl.pallas_call(
        _attn_ctx_kernel,
        grid=(BATCH,),
        in_specs=[
            pl.BlockSpec((None, D_ATTN, SEQ), lambda b: (b // halves, 0, b % halves)),
            pl.BlockSpec((SEQ, D_KV), lambda b: (b, 0)),
            pl.BlockSpec((None, D_KV, SEQ), lambda b: (b // halves, 0, b % halves)),
        ],
        out_specs=pl.BlockSpec((SEQ, D_ATTN), lambda b: (b, 0)),
        out_shape=jax.ShapeDtypeStruct((T_CTX, D_ATTN), BF16),
        compiler_params=_params(1),
        name=f"attn_ctx_l{l}",
    )(qt, k, vt)


KC = 512


def _attn_lat_kernel(qt_ref, k_ref, vt_ref, ck_ref, cv_ref, o_ref):
    tq = TM
    pair = pl.program_id(2)
    kv_idx = pair // (N_HEADS // 2 // N_KV_HEADS)
    rhs = _pair_queries(qt_ref[...], kv_idx, tq)
    v_row = pl.multiple_of(kv_idx * HEAD_DIM, HEAD_DIM)

    def latent_chunk(c):
        return (lambda: k_ref[c * KC:(c + 1) * KC, :],
                lambda: _with_ones(vt_ref[c, pl.ds(v_row, HEAD_DIM), :]))

    def cached_vt():
        cvt = cv_ref[...].T
        return _with_ones(jnp.where(kv_idx == 0, cvt[:HEAD_DIM], cvt[HEAD_DIM:]).astype(BF16))

    chunks = [latent_chunk(c) for c in range(DEC_SEQ // KC)]
    chunks.append((lambda: ck_ref[...].astype(BF16), cached_vt))
    o_ref[...] = _flash_pair(rhs, chunks, tq).T.astype(BF16)


def _attn_lat(l, qt, k, vt, cache_k, cache_v):
    n_q = DEC_SEQ // TM
    first = N_CTX_TILES
    ctx_blocks = T_CTX // DEC_SEQ
    return pl.pallas_call(
        _attn_lat_kernel,
        grid=(DEC_BATCH, n_q, N_HEADS // 2),
        in_specs=[
            pl.BlockSpec((None, LANES, TM), lambda b, i, p: (first + b * n_q + i, p, 0)),
            pl.BlockSpec((DEC_SEQ, D_KV), lambda b, i, p: (ctx_blocks + b, 0)),
            pl.BlockSpec((n_q, D_KV, TM), lambda b, i, p: (ctx_blocks + b, 0, 0)),
            pl.BlockSpec((None, None, PAST_LEN, D_KV), lambda b, i, p: (b, l, 0, 0)),
            pl.BlockSpec((None, None, PAST_LEN, D_KV), lambda b, i, p: (b, l, 0, 0)),
        ],
        out_specs=pl.BlockSpec((TM, LANES), lambda b, i, p: (b * n_q + i, p)),
        out_shape=jax.ShapeDtypeStruct((T_LAT, D_ATTN), BF16),
        compiler_params=_params(3),
        name=f"attn_lat_l{l}",
    )(qt, k, vt, cache_k, cache_v)


def _first_max_index(vals, valid, rowf):
    masked = jnp.where(valid, vals, -jnp.inf)
    mx = jnp.max(masked, axis=0, keepdims=True)
    idx = jnp.min(jnp.where(masked == mx, rowf, float(SUBLANES)), axis=0, keepdims=True)
    return mx, idx


def _merge_kernel(*refs, split_x):
    if split_x:
        xa_ref, xb_ref = refs[:2]
        refs = refs[2:]
    else:
        x_ref = refs[0]
        refs = refs[1:]
    (actx_ref, alat_ref, sgu_ref, cbz_ref, prev_ref, next_ref, cw_ref, on_ref,
     wo_ref, mod_ref, g2_ref, wr_ref, wrhi_ref, rb_ref, x1_ref, h2_ref, route_ref, gate4_ref, cnt_ref) = refs
    i = pl.program_id(0)
    is_ctx = i < N_CTX_TILES
    mrow = _mod_row_of_tile(i)
    on = on_ref[...]

    cbz = cbz_ref[...]
    cb = cbz[:, :D_CONV]
    z = cbz[:, D_CONV:]
    seq_mask = jnp.where(is_ctx, SEQ - 1, DEC_SEQ - 1)
    row = lax.broadcasted_iota(jnp.int32, (TM, D_CONV), 0)
    pos = (i * TM + row) & seq_mask
    z_prev = jnp.where(row == 0, prev_ref[SUBLANES - 1:SUBLANES, D_CONV:], pltpu.roll(z, 1, 0))
    z_next = jnp.where(row == TM - 1, next_ref[0:1, D_CONV:], pltpu.roll(z, TM - 1, 0))
    z_prev = jnp.where(pos == 0, 0.0, z_prev)
    z_next = jnp.where(pos == seq_mask, 0.0, z_next)
    cw = cw_ref[...]
    conv = cb * (z_prev * cw[0:1, :] + z * cw[1:2, :] + z_next * cw[2:3, :])
    conv_n = _rms(conv, on[:, D_ATTN + D_SGU:]).astype(BF16)

    attn = jnp.where(is_ctx, actx_ref[...].astype(F32), alat_ref[...].astype(F32))
    attn_n = _rms(attn, on[:, :D_ATTN]).astype(BF16)
    merged = jnp.concatenate([attn_n, sgu_ref[...], conv_n], axis=1)
    x = jnp.where(is_ctx, xa_ref[...], xb_ref[...]) if split_x else x_ref[...]
    x1 = x + _mod_part(mod_ref, 2, mrow) * jnp.dot(merged, wo_ref[...], preferred_element_type=F32)
    x1_ref[...] = x1

    h2 = _rms(x1, g2_ref[...]) * (1.0 + _mod_part(mod_ref, 4, mrow)) + _mod_part(mod_ref, 3, mrow)
    hb = h2.astype(BF16)
    h2_ref[...] = hb

    h_lo = (h2 - hb.astype(F32)).astype(BF16)
    a = jnp.dot(hb, wr_ref[...], preferred_element_type=F32)
    b = jnp.dot(h_lo, wrhi_ref[...], preferred_element_type=F32)
    logits = a[:, :ROUTER_COLS] + a[:, ROUTER_COLS:] + b + rb_ref[...]
    lt = logits.T

    rowi = lax.broadcasted_iota(jnp.int32, (SUBLANES, TM), 0).astype(F32)
    valid = rowi < float(N_EXPERT_GROUPS)
    g_log = lt[0:SUBLANES]
    g_max, g_idx = _first_max_index(g_log, valid, rowi)
    p_g = 1.0 / jnp.sum(jnp.where(valid, jnp.exp(g_log - g_max), 0.0), axis=0, keepdims=True)
    e_log = jnp.zeros((SUBLANES, TM), F32)
    for g in range(N_EXPERT_GROUPS):
        e_log = jnp.where(g_idx == g, lt[(g + 1) * SUBLANES:(g + 2) * SUBLANES], e_log)
    e_max, i1 = _first_max_index(e_log, valid, rowi)
    e_exp = jnp.where(valid, jnp.exp(e_log - e_max), 0.0)
    e_prob = e_exp / jnp.sum(e_exp, axis=0, keepdims=True)
    v1 = jnp.max(e_prob, axis=0, keepdims=True)
    rest = jnp.logical_and(valid, rowi != i1)
    v2, i2 = _first_max_index(e_prob, rest, rowi)
    denom = v1 + v2
    w_sel = p_g * jnp.where(rowi == i1, v1 / denom, jnp.where(rowi == i2, v2 / denom, 0.0))
    route_ref[...] = jnp.broadcast_to(g_idx, (SUBLANES, TM))
    pad = jnp.zeros((LANES - SUBLANES, TM), F32)
    gate4_ref[...] = jnp.concatenate([w_sel, pad], axis=0).T
    counts = jnp.sum(jnp.where(rowi == g_idx, 1.0, 0.0), axis=1, keepdims=True)
    cnt_ref[...] = jnp.broadcast_to(counts, (SUBLANES, LANES)).astype(jnp.int32)


def _merge(l, xs, attn_ctx, attn_lat, sgu_n, cbz, conv_w, on, w_out, mod, g2, wr_cat, wr_hi, rb):
    lay = lambda *rest: (lambda i: (l,) + rest)
    rows8 = TM // SUBLANES
    return pl.pallas_call(
        functools.partial(_merge_kernel, split_x=len(xs) == 2),
        grid=(N_TILES,),
        in_specs=_x_specs(xs) + [
            pl.BlockSpec((TM, D_ATTN), lambda i: (_ctx_tile(i), 0)),
            pl.BlockSpec((TM, D_ATTN), lambda i: (_lat_tile(i), 0)),
            pl.BlockSpec((TM, D_SGU), lambda i: (i, 0)),
            pl.BlockSpec((TM, 2 * D_CONV), lambda i: (i, 0)),
            pl.BlockSpec((SUBLANES, 2 * D_CONV), lambda i: (jnp.maximum(i * rows8 - 1, 0), 0)),
            pl.BlockSpec((SUBLANES, 2 * D_CONV),
                         lambda i: (jnp.minimum((i + 1) * rows8, T_ALL // SUBLANES - 1), 0)),
            pl.BlockSpec((None, 3, D_CONV), lay(0, 0)),
            pl.BlockSpec((None, 1, D_MIX), lay(0, 0)),
            pl.BlockSpec((None, D_MIX, D_MODEL), lay(0, 0)),
            _MOD_SPEC(l, 1),
            pl.BlockSpec((None, 1, D_MODEL), lay(0, 0)),
            pl.BlockSpec((None, D_MODEL, 2 * ROUTER_COLS), lay(0, 0)),
            pl.BlockSpec((None, D_MODEL, ROUTER_COLS), lay(0, 0)),
            pl.BlockSpec((None, 1, ROUTER_COLS), lay(0, 0)),
        ],
        out_specs=[
            pl.BlockSpec((TM, D_MODEL), lambda i: (i, 0)),
            pl.BlockSpec((TM, D_MODEL), lambda i: (i, 0)),
            pl.BlockSpec((None, SUBLANES, TM), lambda i: (i, 0, 0)),
            pl.BlockSpec((TM, LANES), lambda i: (i, 0)),
            pl.BlockSpec((None, SUBLANES, LANES), lambda i: (i, 0, 0)),
        ],
        out_shape=[
            jax.ShapeDtypeStruct((T_ALL, D_MODEL), F32),
            jax.ShapeDtypeStruct((T_ALL, D_MODEL), BF16),
            jax.ShapeDtypeStruct((N_TILES, SUBLANES, TM), F32),
            jax.ShapeDtypeStruct((T_ALL, LANES), F32),
            jax.ShapeDtypeStruct((N_TILES, SUBLANES, LANES), jnp.int32),
        ],
        compiler_params=_params(1),
        name=f"merge_l{l}",
    )(*xs, attn_ctx, attn_lat, sgu_n, cbz, cbz, cbz, conv_w, on, w_out, mod, g2, wr_cat, wr_hi, rb)


RB = 128
MOE_VMEM_LIMIT = 56 * 1024 * 1024


def _split3(x):
    hi = x.astype(BF16)
    r1 = x - hi.astype(F32)
    mid = r1.astype(BF16)
    lo = (r1 - mid.astype(F32)).astype(BF16)
    return hi, mid, lo


def _moe_kernel(cnt_ref, h_ref, route_ref, gate4_ref, tri_ref, w1_ref, w3_ref, w2_ref, x1_ref, mod_ref, fn_ref,
                *out_refs, final):
    i = pl.program_id(0)
    gate2 = _mod_part(mod_ref, 5, _mod_row_of_tile(i))
    counts = [cnt_ref[i, g] for g in range(N_EXPERT_GROUPS)]
    starts = [jnp.int32(0)]
    for g in range(N_EXPERT_GROUPS - 1):
        starts.append(starts[-1] + counts[g])
    stops = [starts[g] + counts[g] for g in range(N_EXPERT_GROUPS)]

    g_idx = route_ref[0:1, :]
    rowi = lax.broadcasted_iota(jnp.int32, (SUBLANES, TM), 0).astype(F32)
    onehot = jnp.where(rowi == g_idx, 1.0, 0.0)
    incl = jnp.dot(onehot.astype(BF16), tri_ref[...], preferred_element_type=F32)
    pos = jnp.sum(onehot * incl, axis=0, keepdims=True) - 1.0
    for g in range(1, N_EXPERT_GROUPS):
        pos = pos + jnp.where(g_idx == float(g), starts[g].astype(F32), 0.0)
    sub = lax.broadcasted_iota(jnp.int32, (TM, TM), 0).astype(F32)
    perm = jnp.where(sub == pos, 1.0, 0.0).astype(BF16)
    pos_col = jnp.broadcast_to(pos, (LANES, TM)).T
    lane = lax.broadcasted_iota(jnp.int32, (TM, TM), 1).astype(F32)
    perm_t = jnp.where(lane == jnp.concatenate([pos_col] * (TM // LANES), axis=1), 1.0, 0.0).astype(BF16)

    xs = jnp.dot(perm, h_ref[...], preferred_element_type=F32).astype(BF16)
    hi, mid, lo = _split3(gate4_ref[...])
    packed = (hi.astype(F32) + pltpu.roll(mid.astype(F32), EXPERTS_PER_GROUP, 1)
              + pltpu.roll(lo.astype(F32), 2 * EXPERTS_PER_GROUP, 1)).astype(BF16)
    gsp = jnp.dot(perm, packed, preferred_element_type=F32)
    gs = (gsp + pltpu.roll(gsp, LANES - EXPERTS_PER_GROUP, 1)
          + pltpu.roll(gsp, LANES - 2 * EXPERTS_PER_GROUP, 1))

    blocks = []
    for blk in range(TM // RB):
        r0 = blk * RB
        xb = xs[r0:r0 + RB, :]
        gb = gs[r0:r0 + RB, :]
        rowpos = lax.broadcasted_iota(jnp.int32, (RB, LANES), 0) + r0
        g_lo = sum((stops[g] <= r0).astype(jnp.int32) for g in range(N_EXPERT_GROUPS))
        g_hi = N_EXPERT_GROUPS - sum((starts[g] >= r0 + RB).astype(jnp.int32) for g in range(N_EXPERT_GROUPS))

        def group_ffn(g, acc, xb=xb, gb=gb, rowpos=rowpos):
            start = stop = jnp.int32(0)
            for k in range(N_EXPERT_GROUPS):
                start = jnp.where(g == k, starts[k], start)
                stop = jnp.where(g == k, stops[k], stop)
            gm = jnp.where(jnp.logical_and(rowpos >= start, rowpos < stop), gb, 0.0)
            cols = []
            for e in range(EXPERTS_PER_GROUP):
                a = jnp.dot(xb, w1_ref[g * EXPERTS_PER_GROUP + e], preferred_element_type=F32)
                b = jnp.dot(xb, w3_ref[g * EXPERTS_PER_GROUP + e], preferred_element_type=F32)
                cols.append((a / (1.0 + jnp.exp(-a)) * b * gm[:, e:e + 1]).astype(BF16))
            return acc + jnp.dot(jnp.concatenate(cols, axis=1), w2_ref[g], preferred_element_type=F32)

        blocks.append(lax.fori_loop(g_lo, g_hi, group_ffn, jnp.zeros((RB, D_MODEL), F32)))
    ys = jnp.concatenate(blocks, axis=0)

    y = jnp.dot(perm_t, ys.astype(BF16), preferred_element_type=F32)
    out = x1_ref[...] + gate2 * y

    if final:
        octx_ref, olat_ref = out_refs
        out = _rms(out, fn_ref[...])

        @pl.when(i < N_CTX_TILES)
        def _():
            octx_ref[...] = out

        @pl.when(i >= N_CTX_TILES)
        def _():
            olat_ref[...] = out
    else:
        out_refs[0][...] = out


def _moe(l, counts, h2, route, gate4, tri, w1, w3, w2, x1, mod, final_norm, final):
    tile = pl.BlockSpec((TM, D_MODEL), lambda i, c: (i, 0))
    resident = pl.Buffered(1)
    if final:
        out_specs = [pl.BlockSpec((TM, D_MODEL), lambda i, c: (_ctx_tile(i), 0)),
                     pl.BlockSpec((TM, D_MODEL), lambda i, c: (_lat_tile(i), 0))]
        out_shape = [jax.ShapeDtypeStruct((T_CTX, D_MODEL), F32), jax.ShapeDtypeStruct((T_LAT, D_MODEL), F32)]
    else:
        out_specs = tile
        out_shape = jax.ShapeDtypeStruct((T_ALL, D_MODEL), F32)
    grid_spec = pltpu.PrefetchScalarGridSpec(
        num_scalar_prefetch=1,
        grid=(N_TILES,),
        in_specs=[
            tile,
            pl.BlockSpec((None, SUBLANES, TM), lambda i, c: (i, 0, 0)),
            pl.BlockSpec((TM, LANES), lambda i, c: (i, 0)),
            pl.BlockSpec((TM, TM), lambda i, c: (0, 0), pipeline_mode=resident),
            pl.BlockSpec((None, N_EXPERTS, D_MODEL, D_EXPERT), lambda i, c: (l, 0, 0, 0), pipeline_mode=resident),
            pl.BlockSpec((None, N_EXPERTS, D_MODEL, D_EXPERT), lambda i, c: (l, 0, 0, 0), pipeline_mode=resident),
            pl.BlockSpec((None, N_EXPERT_GROUPS, D_GROUP_HID, D_MODEL), lambda i, c: (l, 0, 0, 0),
                         pipeline_mode=resident),
            tile,
            pl.BlockSpec((None, N_MOD_PARTS, N_MOD_ROWS, D_MODEL), lambda i, c: (l, 0, 0, 0)),
            pl.BlockSpec((1, D_MODEL), lambda i, c: (0, 0)),
        ],
        out_specs=out_specs,
    )
    return pl.pallas_call(
        functools.partial(_moe_kernel, final=final),
        grid_spec=grid_spec,
        out_shape=out_shape,
        compiler_params=pltpu.CompilerParams(dimension_semantics=("arbitrary",),
                                             vmem_limit_bytes=MOE_VMEM_LIMIT),
        name=f"moe_l{l}",
    )(counts, h2, route, gate4, tri, w1, w3, w2, x1, mod, final_norm)


def _rope_tables():
    rows = DEC_SEQ // GRID_W
    row = np.repeat(np.arange(rows, dtype=np.float32), GRID_W)
    col = np.tile(np.arange(GRID_W, dtype=np.float32), rows)
    half = ROPE_AXIS_DIM // 2
    inv_freq = (1.0 / (np.float32(ROPE_THETA) ** (np.arange(half, dtype=np.float32) * np.float32(2.0)
                                                  / np.float32(ROPE_AXIS_DIM)))).astype(np.float32)
    ar = row[:, None] * inv_freq
    ac = col[:, None] * inv_freq
    cos64 = np.concatenate([np.cos(ar), np.cos(ar), np.cos(ac), np.cos(ac)], axis=1)
    sin64 = np.concatenate([-np.sin(ar), np.sin(ar), -np.sin(ac), np.sin(ac)], axis=1)
    cos_t = np.concatenate([np.ones((TM, LANES), np.float32), np.tile(cos64, (1, 2))], axis=0)
    sin_t = np.concatenate([np.zeros((TM, LANES), np.float32), np.tile(sin64, (1, 2))], axis=0)
    return jnp.asarray(cos_t, F32), jnp.asarray(sin_t, F32)


def _router_weights(router_g_w, router_g_b, router_e_w, router_e_b):
    zw = jnp.zeros((DEPTH, D_MODEL, SUBLANES - N_EXPERT_GROUPS), F32)
    zb = jnp.zeros((DEPTH, SUBLANES - N_EXPERT_GROUPS), F32)
    w_cols, b_cols = [router_g_w, zw], [router_g_b, zb]
    for g in range(N_EXPERT_GROUPS):
        w_cols += [router_e_w[:, g], zw]
        b_cols += [router_e_b[:, g], zb]
    used = (1 + N_EXPERT_GROUPS) * SUBLANES
    w_cols.append(jnp.zeros((DEPTH, D_MODEL, ROUTER_COLS - used), F32))
    b_cols.append(jnp.zeros((DEPTH, ROUTER_COLS - used), F32))
    w = jnp.concatenate(w_cols, axis=2)
    b = jnp.concatenate(b_cols, axis=1).reshape(DEPTH, 1, ROUTER_COLS)
    w_hi = w.astype(BF16)
    w_lo = (w - w_hi.astype(F32)).astype(BF16)
    return jnp.concatenate([w_hi, w_lo], axis=2), w_hi, b


def kernel(x_prompt, x_sample, cache_k, cache_v, c, c_ctx, w_ada, b_ada, norm1, w_in, q_norm, k_norm,
           sgu_w, sgu_b, conv_w, out_norm, w_out, norm2, router_g_w, router_g_b, router_e_w,
           router_e_b, moe_w1, moe_w3, moe_w2, final_norm):
    xs = (x_prompt.reshape(T_CTX, D_MODEL), x_sample.reshape(T_LAT, D_MODEL))

    cvecs = jnp.concatenate([c_ctx[None, :], c, jnp.zeros((N_MOD_ROWS - 1 - DEC_BATCH, D_MODEL), F32)], axis=0)
    mod = _modulation(cvecs.T, w_ada, b_ada)

    w_in_b = w_in.astype(BF16)
    w_out_b = w_out.astype(BF16)
    g1 = norm1.reshape(DEPTH, 1, D_MODEL)
    g2 = norm2.reshape(DEPTH, 1, D_MODEL)
    on = out_norm.reshape(DEPTH, 1, D_MIX)
    qg = jnp.tile(q_norm, (1, N_HEADS)).reshape(DEPTH, 1, D_ATTN)
    kg = jnp.tile(k_norm, (1, N_KV_HEADS)).reshape(DEPTH, 1, D_KV)
    cos_t, sin_t = _rope_tables()
    bq = jnp.asarray(np.kron(np.eye(N_HEADS), np.full((HEAD_DIM, HEAD_DIM), 1.0 / HEAD_DIM)), BF16)
    sgu_w_pairs = sgu_w.astype(BF16).reshape(DEPTH, N_SGU_GROUPS // 2, 2, CHUNK, CHUNK)
    sgu_w_pairs = jnp.concatenate([sgu_w_pairs[:, :, 0], sgu_w_pairs[:, :, 1]], axis=-1)
    sgu_bias = jnp.repeat(jnp.swapaxes(sgu_b, 1, 2), SGU_GROUP_DIM, axis=2)
    wr_cat, wr_hi, rb = _router_weights(router_g_w, router_g_b, router_e_w, router_e_b)
    w1 = moe_w1.astype(BF16)
    w3 = moe_w3.astype(BF16)
    w2 = moe_w2.astype(BF16).reshape(DEPTH, N_EXPERT_GROUPS, D_GROUP_HID, D_MODEL)
    fn = final_norm.reshape(1, D_MODEL)
    tri = jnp.asarray(np.triu(np.ones((TM, TM), np.float32)), BF16)
    ck = cache_k.reshape(DEC_BATCH, DEPTH, PAST_LEN, D_KV)
    cv = cache_v.reshape(DEC_BATCH, DEPTH, PAST_LEN, D_KV)

    ctx_ks, ctx_vs = [], []
    for l in range(DEPTH):
        qt, k, vt, k32, v32, sgu_n, cbz = _proj(l, xs, mod, g1, w_in_b, qg, kg, cos_t, sin_t, bq,
                                                sgu_w_pairs, sgu_bias, on)
        attn_ctx = _attn_ctx(l, qt, k, vt)
        attn_lat = _attn_lat(l, qt, k, vt, ck, cv)
        x1, h2, route, gate4, cnt = _merge(l, xs, attn_ctx, attn_lat, sgu_n, cbz, conv_w, on, w_out_b, mod, g2,
                                           wr_cat, wr_hi, rb)
        counts = cnt[:, :N_EXPERT_GROUPS, 0]
        out = _moe(l, counts, h2, route, gate4, tri, w1, w3, w2, x1, mod, fn, final=(l == DEPTH - 1))
        xs = (out,)
        ctx_ks.append(k32.reshape(BATCH, SEQ, N_KV_HEADS, HEAD_DIM))
        ctx_vs.append(v32.reshape(BATCH, SEQ, N_KV_HEADS, HEAD_DIM))

    y_prompt, y_sample = out
    return (y_prompt.reshape(BATCH, SEQ, D_MODEL), y_sample.reshape(DEC_BATCH, DEC_SEQ, D_MODEL),
            jnp.stack(ctx_ks, axis=1), jnp.stack(ctx_vs, axis=1))
```

```python
import functools
import math

import jax
import jax.numpy as jnp
import numpy as np
from jax import lax
from jax.experimental import pallas as pl
from jax.experimental.pallas import tpu as pltpu

D_MODEL = 1024
BATCH = 16
SEQ = 256
DEPTH = 2
DEC_BATCH = 2
DEC_SEQ = 4096
PAST_LEN = 512
GRID_W = 64
N_HEADS = 8
N_KV_HEADS = 2
HEAD_DIM = 64
ROPE_AXIS_DIM = HEAD_DIM // 2
ROPE_THETA = 10000.0
D_ATTN = N_HEADS * HEAD_DIM
D_KV = N_KV_HEADS * HEAD_DIM
CHUNK = 128
N_SGU_GROUPS = 4
SGU_GROUP_DIM = 64
D_SGU = N_SGU_GROUPS * SGU_GROUP_DIM
D_CONV = 256
D_MIX = D_ATTN + D_SGU + D_CONV
D_IN = D_ATTN + 2 * D_KV + 2 * D_SGU + 3 * D_CONV
N_EXPERT_GROUPS = 4
EXPERTS_PER_GROUP = 4
N_EXPERTS = N_EXPERT_GROUPS * EXPERTS_PER_GROUP
D_EXPERT = 256
D_GROUP_HID = EXPERTS_PER_GROUP * D_EXPERT
EPS = 1e-6

T_CTX = BATCH * SEQ
T_LAT = DEC_BATCH * DEC_SEQ
T_ALL = T_CTX + T_LAT
TM = 512
N_TILES = T_ALL // TM
N_CTX_TILES = T_CTX // TM
LAT_TILES_PER_BATCH = DEC_SEQ // TM
TM_MOE = 1024
LANES = 128
SUBLANES = 8
BF16_ROWS = 16
N_MOD_ROWS = 8
N_MOD_PARTS = 6
ROUTER_COLS = 128
VMEM_LIMIT = 48 * 1024 * 1024

OFF_Q = 0
OFF_KV = D_ATTN
OFF_SGU = OFF_KV + 2 * D_KV
OFF_CONV = OFF_SGU + 2 * D_SGU

BF16 = jnp.bfloat16
F32 = jnp.float32
NEG_BIG = -1e30
Q_SCALE = HEAD_DIM ** -0.5 * math.log2(math.e)


def _params(n_grid_dims):
    return pltpu.CompilerParams(
        dimension_semantics=("arbitrary",) * n_grid_dims,
        vmem_limit_bytes=VMEM_LIMIT,
    )


def _mod_row_of_tile(i, tile=TM):
    n_ctx = T_CTX // tile
    return jnp.where(i < n_ctx, 0, 1 + (i - n_ctx) // (DEC_SEQ // tile))


def _mod_part(mod_ref, part, row):
    return mod_ref[part, pl.ds(row, 1), :]


def _rope_block_of_tile(i):
    return jnp.where(i < N_CTX_TILES, 0, 1 + (i - N_CTX_TILES) % LAT_TILES_PER_BATCH)


def _rms(x, gain):
    ms = jnp.mean(x * x, axis=-1, keepdims=True)
    return x * lax.rsqrt(ms + EPS) * gain


def _ctx_tile(i):
    return jnp.minimum(i, N_CTX_TILES - 1)


def _lat_tile(i):
    return jnp.maximum(i - N_CTX_TILES, 0)


MOD_TN = 512


def _mod_kernel(ct_ref, w_ref, b_ref, o_ref):
    c = ct_ref[...]
    s = c / (1.0 + jnp.exp(-c))
    w = w_ref[...]
    row = lax.broadcasted_iota(jnp.int32, (N_MOD_ROWS, MOD_TN), 0)
    out = jnp.zeros((N_MOD_ROWS, MOD_TN), F32)
    for r in range(1 + DEC_BATCH):
        acc = jnp.sum(w * s[:, r:r + 1], axis=0, keepdims=True)
        out = jnp.where(row == r, acc, out)
    o_ref[...] = out + b_ref[...]


def _modulation(cvecs_t, w_ada, b_ada):
    n_col = N_MOD_PARTS * D_MODEL
    per_part = D_MODEL // MOD_TN
    return pl.pallas_call(
        _mod_kernel,
        grid=(DEPTH, n_col // MOD_TN),
        in_specs=[
            pl.BlockSpec((D_MODEL, N_MOD_ROWS), lambda l, j: (0, 0)),
            pl.BlockSpec((None, D_MODEL, MOD_TN), lambda l, j: (l, 0, j)),
            pl.BlockSpec((None, 1, MOD_TN), lambda l, j: (l, 0, j)),
        ],
        out_specs=pl.BlockSpec((None, None, N_MOD_ROWS, MOD_TN),
                               lambda l, j: (l, j // per_part, 0, j % per_part)),
        out_shape=jax.ShapeDtypeStruct((DEPTH, N_MOD_PARTS, N_MOD_ROWS, D_MODEL), F32),
        compiler_params=_params(2),
        name="adaln_modulation",
    )(cvecs_t, w_ada, b_ada.reshape(DEPTH, 1, n_col))


_MOD_SPEC = lambda l, nd: pl.BlockSpec(
    (None, N_MOD_PARTS, N_MOD_ROWS, D_MODEL), lambda *idx: (l, 0, 0, 0))


def _swap16(x):
    lane = lax.broadcasted_iota(jnp.int32, x.shape, 1)
    up = pltpu.roll(x, 16, 1)
    down = pltpu.roll(x, LANES - 16, 1)
    return jnp.where((lane & 16) != 0, up, down)


def _rope(x, cos_t, sin_t):
    cols = []
    for j in range(x.shape[1] // LANES):
        xc = x[:, j * LANES:(j + 1) * LANES]
        cols.append(xc * cos_t + _swap16(xc) * sin_t)
    return cols[0] if len(cols) == 1 else jnp.concatenate(cols, axis=1)


def _proj_kernel(*refs, split_x):
    if split_x:
        xa_ref, xb_ref = refs[:2]
        refs = refs[2:]
    else:
        x_ref = refs[0]
        refs = refs[1:]
    (mod_ref, g1_ref, w_ref, qg_ref, kg_ref, cos_ref, sin_ref, bq_ref, sw_ref, sb_ref, on_ref,
     qt_ref, k_ref, vt_ref, k32_ref, v32_ref, sgu_ref, cbz_ref) = refs
    i = pl.program_id(0)
    is_ctx = i < N_CTX_TILES
    x = jnp.where(is_ctx, xa_ref[...], xb_ref[...]) if split_x else x_ref[...]
    mrow = _mod_row_of_tile(i)
    h = _rms(x, g1_ref[...]) * (1.0 + _mod_part(mod_ref, 1, mrow)) + _mod_part(mod_ref, 0, mrow)
    hb = h.astype(BF16)
    cos_t = cos_ref[...]
    sin_t = sin_ref[...]

    q = jnp.dot(hb, w_ref[:, OFF_Q:OFF_Q + D_ATTN], preferred_element_type=F32)
    qms = jnp.dot((q * q).astype(BF16), bq_ref[...], preferred_element_type=F32)
    qn = q * lax.rsqrt(qms + EPS) * qg_ref[...]
    qr = _rope(qn, cos_t, sin_t) * Q_SCALE
    qt_ref[...] = qr.T.astype(BF16)

    kv = jnp.dot(hb, w_ref[:, OFF_KV:OFF_KV + 2 * D_KV], preferred_element_type=F32)
    k = kv[:, :D_KV]
    v = kv[:, D_KV:]
    kms = jnp.dot((k * k).astype(BF16), bq_ref[:D_KV, :D_KV], preferred_element_type=F32)
    kn = k * lax.rsqrt(kms + EPS) * kg_ref[...]

    @pl.when(is_ctx)
    def _():
        k32_ref[...] = kn
        v32_ref[...] = v

    k_ref[...] = _rope(kn, cos_t, sin_t).astype(BF16)
    vt_ref[...] = v.T.astype(BF16)

    uv = jnp.dot(hb, w_ref[:, OFF_SGU:OFF_SGU + 2 * D_SGU], preferred_element_type=F32)
    su = uv[:, :D_SGU]
    sv = uv[:, D_SGU:]
    grp = lax.broadcasted_iota(jnp.int32, (CHUNK, D_SGU), 1) // SGU_GROUP_DIM
    on_sgu = on_ref[:, D_ATTN:D_ATTN + D_SGU]
    for n in range(TM // CHUNK):
        svc = sv[n * CHUNK:(n + 1) * CHUNK, :]
        mixed = sb_ref[...]
        for pair in range(N_SGU_GROUPS // 2):
            rhs = jnp.concatenate(
                [jnp.where(grp == 2 * pair, svc, 0.0), jnp.where(grp == 2 * pair + 1, svc, 0.0)],
                axis=0).astype(BF16)
            mixed = mixed + jnp.dot(sw_ref[pair], rhs, preferred_element_type=F32)
        sgu = su[n * CHUNK:(n + 1) * CHUNK, :] * mixed
        sgu_ref[n * CHUNK:(n + 1) * CHUNK, :] = _rms(sgu, on_sgu).astype(BF16)

    c3 = jnp.dot(hb, w_ref[:, OFF_CONV:OFF_CONV + 3 * D_CONV], preferred_element_type=F32)
    cbz_ref[:, :D_CONV] = c3[:, :D_CONV]
    cbz_ref[:, D_CONV:] = c3[:, D_CONV:2 * D_CONV] * c3[:, 2 * D_CONV:]


def _x_specs(xs):
    if len(xs) == 1:
        return [pl.BlockSpec((TM, D_MODEL), lambda i: (i, 0))]
    return [pl.BlockSpec((TM, D_MODEL), lambda i: (_ctx_tile(i), 0)),
            pl.BlockSpec((TM, D_MODEL), lambda i: (_lat_tile(i), 0))]


def _proj(l, xs, mod, g1, w_in, qg, kg, cos_t, sin_t, bq, sgu_w, sgu_b, on):
    lay = lambda *rest: (lambda i: (l,) + rest)
    return pl.pallas_call(
        functools.partial(_proj_kernel, split_x=len(xs) == 2),
        grid=(N_TILES,),
        in_specs=_x_specs(xs) + [
            _MOD_SPEC(l, 1),
            pl.BlockSpec((None, 1, D_MODEL), lay(0, 0)),
            pl.BlockSpec((None, D_MODEL, D_IN), lay(0, 0)),
            pl.BlockSpec((None, 1, D_ATTN), lay(0, 0)),
            pl.BlockSpec((None, 1, D_KV), lay(0, 0)),
            pl.BlockSpec((TM, LANES), lambda i: (_rope_block_of_tile(i), 0)),
            pl.BlockSpec((TM, LANES), lambda i: (_rope_block_of_tile(i), 0)),
            pl.BlockSpec((D_ATTN, D_ATTN), lambda i: (0, 0)),
            pl.BlockSpec((None, N_SGU_GROUPS // 2, CHUNK, 2 * CHUNK), lay(0, 0, 0)),
            pl.BlockSpec((None, CHUNK, D_SGU), lay(0, 0)),
            pl.BlockSpec((None, 1, D_MIX), lay(0, 0)),
        ],
        out_specs=[
            pl.BlockSpec((None, D_ATTN, TM), lambda i: (i, 0, 0)),
            pl.BlockSpec((TM, D_KV), lambda i: (i, 0)),
            pl.BlockSpec((None, D_KV, TM), lambda i: (i, 0, 0)),
            pl.BlockSpec((TM, D_KV), lambda i: (_ctx_tile(i), 0)),
            pl.BlockSpec((TM, D_KV), lambda i: (_ctx_tile(i), 0)),
            pl.BlockSpec((TM, D_SGU), lambda i: (i, 0)),
            pl.BlockSpec((TM, 2 * D_CONV), lambda i: (i, 0)),
        ],
        out_shape=[
            jax.ShapeDtypeStruct((N_TILES, D_ATTN, TM), BF16),
            jax.ShapeDtypeStruct((T_ALL, D_KV), BF16),
            jax.ShapeDtypeStruct((N_TILES, D_KV, TM), BF16),
            jax.ShapeDtypeStruct((T_CTX, D_KV), F32),
            jax.ShapeDtypeStruct((T_CTX, D_KV), F32),
            jax.ShapeDtypeStruct((T_ALL, D_SGU), BF16),
            jax.ShapeDtypeStruct((T_ALL, 2 * D_CONV), F32),
        ],
        compiler_params=_params(1),
        name=f"proj_l{l}",
    )(*xs, mod, g1, w_in, qg, kg, cos_t, sin_t, bq, sgu_w, sgu_b, on)


ACC_ROWS = HEAD_DIM + BF16_ROWS
N_PAIRS = N_HEADS // 2
PAIRS_PER_KV = N_PAIRS // N_KV_HEADS


def _pair_queries(qt, kv_idx, tq):
    qf = qt.astype(F32)
    mine = lax.broadcasted_iota(jnp.int32, (2 * HEAD_DIM, tq), 0) // HEAD_DIM == kv_idx
    qe = qf[:HEAD_DIM]
    qo = qf[HEAD_DIM:]
    rhs = jnp.concatenate(
        [jnp.where(mine, jnp.concatenate([qe, qe], axis=0), 0.0),
         jnp.where(mine, jnp.concatenate([qo, qo], axis=0), 0.0)], axis=1)
    return rhs.astype(BF16)


def _with_ones(vt_c):
    return jnp.concatenate([vt_c, jnp.ones((BF16_ROWS, vt_c.shape[1]), BF16)], axis=0)


def _flash_pairs(rhs_list, chunk_lists, tq):
    n_pairs, n_chunks = len(rhs_list), len(chunk_lists[0])
    scores = lambda s, c: jnp.dot(chunk_lists[s][c][0](), rhs_list[s], preferred_element_type=F32)
    m = [jnp.full((1, 2 * tq), NEG_BIG, F32)] * n_pairs
    acc = [jnp.zeros((ACC_ROWS, 2 * tq), F32)] * n_pairs
    st = [scores(s, 0) for s in range(n_pairs)]
    for c in range(n_chunks):
        for s in range(n_pairs):
            m_new = jnp.maximum(m[s], jnp.max(st[s], axis=0, keepdims=True))
            st_next = scores(s, c + 1) if c + 1 < n_chunks else None
            alpha = jnp.exp2(m[s] - m_new)
            p = jnp.exp2(st[s] - m_new).astype(BF16)
            acc[s] = alpha * acc[s] + jnp.dot(chunk_lists[s][c][1](), p, preferred_element_type=F32)
            m[s] = m_new
            st[s] = st_next
    outs = []
    for s in range(n_pairs):
        o = acc[s][:HEAD_DIM] * (1.0 / acc[s][HEAD_DIM:HEAD_DIM + 1])
        outs.append(jnp.concatenate([o[:, :tq], o[:, tq:]], axis=0))
    return outs


def _attn_ctx_kernel(qt_ref, k_ref, vt_ref, o_ref):
    tq = SEQ
    rhs_list, chunk_lists = [], []
    for pair in range(N_PAIRS):
        kv_idx = pair // PAIRS_PER_KV
        rhs_list.append(_pair_queries(qt_ref[pair * LANES:(pair + 1) * LANES, :], kv_idx, tq))
        chunk_lists.append([(lambda: k_ref[...],
                             lambda kv_idx=kv_idx: _with_ones(vt_ref[kv_idx * HEAD_DIM:(kv_idx + 1) * HEAD_DIM, :]))])
    for pair, o_t in enumerate(_flash_pairs(rhs_list, chunk_lists, tq)):
        o_ref[:, pair * LANES:(pair + 1) * LANES] = o_t.T.astype(BF16)


def _attn_ctx(l, qt, k, vt):
    halves = TM // SEQ
    return pl.pallas_call(
        _attn_ctx_kernel,
        grid=(BATCH,),
        in_specs=[
            pl.BlockSpec((None, D_ATTN, SEQ), lambda b: (b // halves, 0, b % halves)),
            pl.BlockSpec((SEQ, D_KV), lambda b: (b, 0)),
            pl.BlockSpec((None, D_KV, SEQ), lambda b: (b // halves, 0, b % halves)),
        ],
        out_specs=pl.BlockSpec((SEQ, D_ATTN), lambda b: (b, 0)),
        out_shape=jax.ShapeDtypeStruct((T_CTX, D_ATTN), BF16),
        compiler_params=_params(1),
        name=f"attn_ctx_l{l}",
    )(qt, k, vt)


KC = 512
LAT_PAIRS_PER_STEP = 2


def _attn_lat_kernel(qt_ref, k_ref, vt_ref, ck_ref, cv_ref, o_ref):
    tq = TM
    cvt = cv_ref[...].T
    rhs_list, chunk_lists = [], []
    for j in range(LAT_PAIRS_PER_STEP):
        kv_idx = (pl.program_id(2) * LAT_PAIRS_PER_STEP + j) // PAIRS_PER_KV
        rhs_list.append(_pair_queries(qt_ref[j * LANES:(j + 1) * LANES, :], kv_idx, tq))
        v_row = pl.multiple_of(kv_idx * HEAD_DIM, HEAD_DIM)

        def latent_chunk(c, v_row=v_row):
            return (lambda: k_ref[c * KC:(c + 1) * KC, :],
                    lambda: _with_ones(vt_ref[c, pl.ds(v_row, HEAD_DIM), :]))

        def cached_vt(kv_idx=kv_idx):
            return _with_ones(jnp.where(kv_idx == 0, cvt[:HEAD_DIM], cvt[HEAD_DIM:]).astype(BF16))

        chunks = [latent_chunk(c) for c in range(DEC_SEQ // KC)]
        chunks.append((lambda: ck_ref[...].astype(BF16), cached_vt))
        chunk_lists.append(chunks)
    for j, o_t in enumerate(_flash_pairs(rhs_list, chunk_lists, tq)):
        o_ref[:, j * LANES:(j + 1) * LANES] = o_t.T.astype(BF16)


def _attn_lat(l, qt, k, vt, cache_k, cache_v):
    n_q = DEC_SEQ // TM
    first = N_CTX_TILES
    ctx_blocks = T_CTX // DEC_SEQ
    width = LAT_PAIRS_PER_STEP * LANES
    return pl.pallas_call(
        _attn_lat_kernel,
        grid=(DEC_BATCH, n_q, N_PAIRS // LAT_PAIRS_PER_STEP),
        in_specs=[
            pl.BlockSpec((None, width, TM), lambda b, i, p: (first + b * n_q + i, p, 0)),
            pl.BlockSpec((DEC_SEQ, D_KV), lambda b, i, p: (ctx_blocks + b, 0)),
            pl.BlockSpec((n_q, D_KV, TM), lambda b, i, p: (ctx_blocks + b, 0, 0)),
            pl.BlockSpec((None, None, PAST_LEN, D_KV), lambda b, i, p: (b, l, 0, 0)),
            pl.BlockSpec((None, None, PAST_LEN, D_KV), lambda b, i, p: (b, l, 0, 0)),
        ],
        out_specs=pl.BlockSpec((TM, width), lambda b, i, p: (b * n_q + i, p)),
        out_shape=jax.ShapeDtypeStruct((T_LAT, D_ATTN), BF16),
        compiler_params=_params(3),
        name=f"attn_lat_l{l}",
    )(qt, k, vt, cache_k, cache_v)


def _first_max_index(vals, valid, rowf):
    masked = jnp.where(valid, vals, -jnp.inf)
    mx = jnp.max(masked, axis=0, keepdims=True)
    idx = jnp.min(jnp.where(masked == mx, rowf, float(SUBLANES)), axis=0, keepdims=True)
    return mx, idx


def _merge_kernel(*refs, split_x):
    if split_x:
        xa_ref, xb_ref = refs[:2]
        refs = refs[2:]
    else:
        x_ref = refs[0]
        refs = refs[1:]
    (actx_ref, alat_ref, sgu_ref, cbz_ref, prev_ref, next_ref, cw_ref, on_ref,
     wo_ref, mod_ref, g2_ref, wr_ref, wrhi_ref, rb_ref, x1_ref, h2_ref, route_ref, gate4_ref, cnt_ref) = refs
    i = pl.program_id(0)
    is_ctx = i < N_CTX_TILES
    mrow = _mod_row_of_tile(i)
    on = on_ref[...]

    cbz = cbz_ref[...]
    cb = cbz[:, :D_CONV]
    z = cbz[:, D_CONV:]
    seq_mask = jnp.where(is_ctx, SEQ - 1, DEC_SEQ - 1)
    row = lax.broadcasted_iota(jnp.int32, (TM, D_CONV), 0)
    pos = (i * TM + row) & seq_mask
    z_prev = jnp.where(row == 0, prev_ref[SUBLANES - 1:SUBLANES, D_CONV:], pltpu.roll(z, 1, 0))
    z_next = jnp.where(row == TM - 1, next_ref[0:1, D_CONV:], pltpu.roll(z, TM - 1, 0))
    z_prev = jnp.where(pos == 0, 0.0, z_prev)
    z_next = jnp.where(pos == seq_mask, 0.0, z_next)
    cw = cw_ref[...]
    conv = cb * (z_prev * cw[0:1, :] + z * cw[1:2, :] + z_next * cw[2:3, :])
    conv_n = _rms(conv, on[:, D_ATTN + D_SGU:]).astype(BF16)

    attn = jnp.where(is_ctx, actx_ref[...].astype(F32), alat_ref[...].astype(F32))
    attn_n = _rms(attn, on[:, :D_ATTN]).astype(BF16)
    merged = jnp.concatenate([attn_n, sgu_ref[...], conv_n], axis=1)
    x = jnp.where(is_ctx, xa_ref[...], xb_ref[...]) if split_x else x_ref[...]
    x1 = x + _mod_part(mod_ref, 2, mrow) * jnp.dot(merged, wo_ref[...], preferred_element_type=F32)
    x1_ref[...] = x1

    h2 = _rms(x1, g2_ref[...]) * (1.0 + _mod_part(mod_ref, 4, mrow)) + _mod_part(mod_ref, 3, mrow)
    hb = h2.astype(BF16)
    h2_ref[...] = hb

    h_lo = (h2 - hb.astype(F32)).astype(BF16)
    a = jnp.dot(hb, wr_ref[...], preferred_element_type=F32)
    b = jnp.dot(h_lo, wrhi_ref[...], preferred_element_type=F32)
    logits = a[:, :ROUTER_COLS] + a[:, ROUTER_COLS:] + b + rb_ref[...]
    lt = logits.T

    rowi = lax.broadcasted_iota(jnp.int32, (SUBLANES, TM), 0).astype(F32)
    valid = rowi < float(N_EXPERT_GROUPS)
    g_log = lt[0:SUBLANES]
    g_max, g_idx = _first_max_index(g_log, valid, rowi)
    p_g = 1.0 / jnp.sum(jnp.where(valid, jnp.exp(g_log - g_max), 0.0), axis=0, keepdims=True)
    e_log = jnp.zeros((SUBLANES, TM), F32)
    for g in range(N_EXPERT_GROUPS):
        e_log = jnp.where(g_idx == g, lt[(g + 1) * SUBLANES:(g + 2) * SUBLANES], e_log)
    e_max, i1 = _first_max_index(e_log, valid, rowi)
    e_exp = jnp.where(valid, jnp.exp(e_log - e_max), 0.0)
    e_prob = e_exp / jnp.sum(e_exp, axis=0, keepdims=True)
    v1 = jnp.max(e_prob, axis=0, keepdims=True)
    rest = jnp.logical_and(valid, rowi != i1)
    v2, i2 = _first_max_index(e_prob, rest, rowi)
    denom = v1 + v2
    w_sel = p_g * jnp.where(rowi == i1, v1 / denom, jnp.where(rowi == i2, v2 / denom, 0.0))
    route_ref[...] = jnp.broadcast_to(g_idx, (SUBLANES, TM))
    pad = jnp.zeros((LANES - SUBLANES, TM), F32)
    gate4_ref[...] = jnp.concatenate([w_sel, pad], axis=0).T
    counts = jnp.sum(jnp.where(rowi == g_idx, 1.0, 0.0), axis=1, keepdims=True)
    cnt_ref[...] = jnp.broadcast_to(counts, (SUBLANES, LANES)).astype(jnp.int32)


def _merge(l, xs, attn_ctx, attn_lat, sgu_n, cbz, conv_w, on, w_out, mod, g2, wr_cat, wr_hi, rb):
    lay = lambda *rest: (lambda i: (l,) + rest)
    rows8 = TM // SUBLANES
    return pl.pallas_call(
        functools.partial(_merge_kernel, split_x=len(xs) == 2),
        grid=(N_TILES,),
        in_specs=_x_specs(xs) + [
            pl.BlockSpec((TM, D_ATTN), lambda i: (_ctx_tile(i), 0)),
            pl.BlockSpec((TM, D_ATTN), lambda i: (_lat_tile(i), 0)),
            pl.BlockSpec((TM, D_SGU), lambda i: (i, 0)),
            pl.BlockSpec((TM, 2 * D_CONV), lambda i: (i, 0)),
            pl.BlockSpec((SUBLANES, 2 * D_CONV), lambda i: (jnp.maximum(i * rows8 - 1, 0), 0)),
            pl.BlockSpec((SUBLANES, 2 * D_CONV),
                         lambda i: (jnp.minimum((i + 1) * rows8, T_ALL // SUBLANES - 1), 0)),
            pl.BlockSpec((None, 3, D_CONV), lay(0, 0)),
            pl.BlockSpec((None, 1, D_MIX), lay(0, 0)),
            pl.BlockSpec((None, D_MIX, D_MODEL), lay(0, 0)),
            _MOD_SPEC(l, 1),
            pl.BlockSpec((None, 1, D_MODEL), lay(0, 0)),
            pl.BlockSpec((None, D_MODEL, 2 * ROUTER_COLS), lay(0, 0)),
            pl.BlockSpec((None, D_MODEL, ROUTER_COLS), lay(0, 0)),
            pl.BlockSpec((None, 1, ROUTER_COLS), lay(0, 0)),
        ],
        out_specs=[
            pl.BlockSpec((TM, D_MODEL), lambda i: (i, 0)),
            pl.BlockSpec((TM, D_MODEL), lambda i: (i, 0)),
            pl.BlockSpec((None, SUBLANES, TM), lambda i: (i, 0, 0)),
            pl.BlockSpec((TM, LANES), lambda i: (i, 0)),
            pl.BlockSpec((None, SUBLANES, LANES), lambda i: (i, 0, 0)),
        ],
        out_shape=[
            jax.ShapeDtypeStruct((T_ALL, D_MODEL), F32),
            jax.ShapeDtypeStruct((T_ALL, D_MODEL), BF16),
            jax.ShapeDtypeStruct((N_TILES, SUBLANES, TM), F32),
            jax.ShapeDtypeStruct((T_ALL, LANES), F32),
            jax.ShapeDtypeStruct((N_TILES, SUBLANES, LANES), jnp.int32),
        ],
        compiler_params=_params(1),
        name=f"merge_l{l}",
    )(*xs, attn_ctx, attn_lat, sgu_n, cbz, cbz, cbz, conv_w, on, w_out, mod, g2, wr_cat, wr_hi, rb)


RB = 128
MOE_VMEM_LIMIT = 56 * 1024 * 1024


def _split3(x):
    hi = x.astype(BF16)
    r1 = x - hi.astype(F32)
    mid = r1.astype(BF16)
    lo = (r1 - mid.astype(F32)).astype(BF16)
    return hi, mid, lo


def _moe_kernel(cnt_ref, h_ref, route_ref, gate4_ref, tri_ref, w1_ref, w3_ref, w2_ref, x1_ref, mod_ref, fn_ref,
                *out_refs, final):
    i = pl.program_id(0)
    gate2 = _mod_part(mod_ref, 5, _mod_row_of_tile(i))
    counts = [cnt_ref[i, g] for g in range(N_EXPERT_GROUPS)]
    starts = [jnp.int32(0)]
    for g in range(N_EXPERT_GROUPS - 1):
        starts.append(starts[-1] + counts[g])
    stops = [starts[g] + counts[g] for g in range(N_EXPERT_GROUPS)]

    g_idx = route_ref[0:1, :]
    rowi = lax.broadcasted_iota(jnp.int32, (SUBLANES, TM), 0).astype(F32)
    onehot = jnp.where(rowi == g_idx, 1.0, 0.0)
    incl = jnp.dot(onehot.astype(BF16), tri_ref[...], preferred_element_type=F32)
    pos = jnp.sum(onehot * incl, axis=0, keepdims=True) - 1.0
    for g in range(1, N_EXPERT_GROUPS):
        pos = pos + jnp.where(g_idx == float(g), starts[g].astype(F32), 0.0)
    sub = lax.broadcasted_iota(jnp.int32, (TM, TM), 0).astype(F32)
    perm = jnp.where(sub == pos, 1.0, 0.0).astype(BF16)
    pos_col = jnp.broadcast_to(pos, (LANES, TM)).T
    lane = lax.broadcasted_iota(jnp.int32, (TM, TM), 1).astype(F32)
    perm_t = jnp.where(lane == jnp.concatenate([pos_col] * (TM // LANES), axis=1), 1.0, 0.0).astype(BF16)

    xs = jnp.dot(perm, h_ref[...], preferred_element_type=F32).astype(BF16)
    hi, mid, lo = _split3(gate4_ref[...])
    packed = (hi.astype(F32) + pltpu.roll(mid.astype(F32), EXPERTS_PER_GROUP, 1)
              + pltpu.roll(lo.astype(F32), 2 * EXPERTS_PER_GROUP, 1)).astype(BF16)
    gsp = jnp.dot(perm, packed, preferred_element_type=F32)
    gs = (gsp + pltpu.roll(gsp, LANES - EXPERTS_PER_GROUP, 1)
          + pltpu.roll(gsp, LANES - 2 * EXPERTS_PER_GROUP, 1))

    blocks = []
    for blk in range(TM // RB):
        r0 = blk * RB
        xb = xs[r0:r0 + RB, :]
        gb = gs[r0:r0 + RB, :]
        rowpos = lax.broadcasted_iota(jnp.int32, (RB, LANES), 0) + r0
        g_lo = sum((stops[g] <= r0).astype(jnp.int32) for g in range(N_EXPERT_GROUPS))
        g_hi = N_EXPERT_GROUPS - sum((starts[g] >= r0 + RB).astype(jnp.int32) for g in range(N_EXPERT_GROUPS))

        def group_ffn(g, acc, xb=xb, gb=gb, rowpos=rowpos):
            start = stop = jnp.int32(0)
            for k in range(N_EXPERT_GROUPS):
                start = jnp.where(g == k, starts[k], start)
                stop = jnp.where(g == k, stops[k], stop)
            gm = jnp.where(jnp.logical_and(rowpos >= start, rowpos < stop), gb, 0.0)
            cols = []
            for e in range(EXPERTS_PER_GROUP):
                a = jnp.dot(xb, w1_ref[g * EXPERTS_PER_GROUP + e], preferred_element_type=F32)
                b = jnp.dot(xb, w3_ref[g * EXPERTS_PER_GROUP + e], preferred_element_type=F32)
                cols.append((a / (1.0 + jnp.exp(-a)) * b * gm[:, e:e + 1]).astype(BF16))
            return acc + jnp.dot(jnp.concatenate(cols, axis=1), w2_ref[g], preferred_element_type=F32)

        blocks.append(lax.fori_loop(g_lo, g_hi, group_ffn, jnp.zeros((RB, D_MODEL), F32)))
    ys = jnp.concatenate(blocks, axis=0)

    y = jnp.dot(perm_t, ys.astype(BF16), preferred_element_type=F32)
    out = x1_ref[...] + gate2 * y

    if final:
        octx_ref, olat_ref = out_refs
        out = _rms(out, fn_ref[...])

        @pl.when(i < N_CTX_TILES)
        def _():
            octx_ref[...] = out

        @pl.when(i >= N_CTX_TILES)
        def _():
            olat_ref[...] = out
    else:
        out_refs[0][...] = out


def _moe(l, counts, h2, route, gate4, tri, w1, w3, w2, x1, mod, final_norm, final):
    tile = pl.BlockSpec((TM, D_MODEL), lambda i, c: (i, 0))
    resident = pl.Buffered(1)
    if final:
        out_specs = [pl.BlockSpec((TM, D_MODEL), lambda i, c: (_ctx_tile(i), 0)),
                     pl.BlockSpec((TM, D_MODEL), lambda i, c: (_lat_tile(i), 0))]
        out_shape = [jax.ShapeDtypeStruct((T_CTX, D_MODEL), F32), jax.ShapeDtypeStruct((T_LAT, D_MODEL), F32)]
    else:
        out_specs = tile
        out_shape = jax.ShapeDtypeStruct((T_ALL, D_MODEL), F32)
    grid_spec = pltpu.PrefetchScalarGridSpec(
        num_scalar_prefetch=1,
        grid=(N_TILES,),
        in_specs=[
            tile,
            pl.BlockSpec((None, SUBLANES, TM), lambda i, c: (i, 0, 0)),
            pl.BlockSpec((TM, LANES), lambda i, c: (i, 0)),
            pl.BlockSpec((TM, TM), lambda i, c: (0, 0), pipeline_mode=resident),
            pl.BlockSpec((None, N_EXPERTS, D_MODEL, D_EXPERT), lambda i, c: (l, 0, 0, 0), pipeline_mode=resident),
            pl.BlockSpec((None, N_EXPERTS, D_MODEL, D_EXPERT), lambda i, c: (l, 0, 0, 0), pipeline_mode=resident),
            pl.BlockSpec((None, N_EXPERT_GROUPS, D_GROUP_HID, D_MODEL), lambda i, c: (l, 0, 0, 0),
                         pipeline_mode=resident),
            tile,
            pl.BlockSpec((None, N_MOD_PARTS, N_MOD_ROWS, D_MODEL), lambda i, c: (l, 0, 0, 0)),
            pl.BlockSpec((1, D_MODEL), lambda i, c: (0, 0)),
        ],
        out_specs=out_specs,
    )
    return pl.pallas_call(
        functools.partial(_moe_kernel, final=final),
        grid_spec=grid_spec,
        out_shape=out_shape,
        compiler_params=pltpu.CompilerParams(dimension_semantics=("arbitrary",),
                                             vmem_limit_bytes=MOE_VMEM_LIMIT),
        name=f"moe_l{l}",
    )(counts, h2, route, gate4, tri, w1, w3, w2, x1, mod, final_norm)


def _rope_tables():
    rows = DEC_SEQ // GRID_W
    row = np.repeat(np.arange(rows, dtype=np.float32), GRID_W)
    col = np.tile(np.arange(GRID_W, dtype=np.float32), rows)
    half = ROPE_AXIS_DIM // 2
    inv_freq = (1.0 / (np.float32(ROPE_THETA) ** (np.arange(half, dtype=np.float32) * np.float32(2.0)
                                                  / np.float32(ROPE_AXIS_DIM)))).astype(np.float32)
    ar = row[:, None] * inv_freq
    ac = col[:, None] * inv_freq
    cos64 = np.concatenate([np.cos(ar), np.cos(ar), np.cos(ac), np.cos(ac)], axis=1)
    sin64 = np.concatenate([-np.sin(ar), np.sin(ar), -np.sin(ac), np.sin(ac)], axis=1)
    cos_t = np.concatenate([np.ones((TM, LANES), np.float32), np.tile(cos64, (1, 2))], axis=0)
    sin_t = np.concatenate([np.zeros((TM, LANES), np.float32), np.tile(sin64, (1, 2))], axis=0)
    return jnp.asarray(cos_t, F32), jnp.asarray(sin_t, F32)


def _router_weights(router_g_w, router_g_b, router_e_w, router_e_b):
    zw = jnp.zeros((DEPTH, D_MODEL, SUBLANES - N_EXPERT_GROUPS), F32)
    zb = jnp.zeros((DEPTH, SUBLANES - N_EXPERT_GROUPS), F32)
    w_cols, b_cols = [router_g_w, zw], [router_g_b, zb]
    for g in range(N_EXPERT_GROUPS):
        w_cols += [router_e_w[:, g], zw]
        b_cols += [router_e_b[:, g], zb]
    used = (1 + N_EXPERT_GROUPS) * SUBLANES
    w_cols.append(jnp.zeros((DEPTH, D_MODEL, ROUTER_COLS - used), F32))
    b_cols.append(jnp.zeros((DEPTH, ROUTER_COLS - used), F32))
    w = jnp.concatenate(w_cols, axis=2)
    b = jnp.concatenate(b_cols, axis=1).reshape(DEPTH, 1, ROUTER_COLS)
    w_hi = w.astype(BF16)
    w_lo = (w - w_hi.astype(F32)).astype(BF16)
    return jnp.concatenate([w_hi, w_lo], axis=2), w_hi, b


def kernel(x_prompt, x_sample, cache_k, cache_v, c, c_ctx, w_ada, b_ada, norm1, w_in, q_norm, k_norm,
           sgu_w, sgu_b, conv_w, out_norm, w_out, norm2, router_g_w, router_g_b, router_e_w,
           router_e_b, moe_w1, moe_w3, moe_w2, final_norm):
    xs = (x_prompt.reshape(T_CTX, D_MODEL), x_sample.reshape(T_LAT, D_MODEL))

    cvecs = jnp.concatenate([c_ctx[None, :], c, jnp.zeros((N_MOD_ROWS - 1 - DEC_BATCH, D_MODEL), F32)], axis=0)
    mod = _modulation(cvecs.T, w_ada, b_ada)

    w_in_b = w_in.astype(BF16)
    w_out_b = w_out.astype(BF16)
    g1 = norm1.reshape(DEPTH, 1, D_MODEL)
    g2 = norm2.reshape(DEPTH, 1, D_MODEL)
    on = out_norm.reshape(DEPTH, 1, D_MIX)
    qg = jnp.tile(q_norm, (1, N_HEADS)).reshape(DEPTH, 1, D_ATTN)
    kg = jnp.tile(k_norm, (1, N_KV_HEADS)).reshape(DEPTH, 1, D_KV)
    cos_t, sin_t = _rope_tables()
    bq = jnp.asarray(np.kron(np.eye(N_HEADS), np.full((HEAD_DIM, HEAD_DIM), 1.0 / HEAD_DIM)), BF16)
    sgu_w_pairs = sgu_w.astype(BF16).reshape(DEPTH, N_SGU_GROUPS // 2, 2, CHUNK, CHUNK)
    sgu_w_pairs = jnp.concatenate([sgu_w_pairs[:, :, 0], sgu_w_pairs[:, :, 1]], axis=-1)
    sgu_bias = jnp.repeat(jnp.swapaxes(sgu_b, 1, 2), SGU_GROUP_DIM, axis=2)
    wr_cat, wr_hi, rb = _router_weights(router_g_w, router_g_b, router_e_w, router_e_b)
    w1 = moe_w1.astype(BF16)
    w3 = moe_w3.astype(BF16)
    w2 = moe_w2.astype(BF16).reshape(DEPTH, N_EXPERT_GROUPS, D_GROUP_HID, D_MODEL)
    fn = final_norm.reshape(1, D_MODEL)
    tri = jnp.asarray(np.triu(np.ones((TM, TM), np.float32)), BF16)
    ck = cache_k.reshape(DEC_BATCH, DEPTH, PAST_LEN, D_KV)
    cv = cache_v.reshape(DEC_BATCH, DEPTH, PAST_LEN, D_KV)

    ctx_ks, ctx_vs = [], []
    for l in range(DEPTH):
        qt, k, vt, k32, v32, sgu_n, cbz = _proj(l, xs, mod, g1, w_in_b, qg, kg, cos_t, sin_t, bq,
                                                sgu_w_pairs, sgu_bias, on)
        attn_ctx = _attn_ctx(l, qt, k, vt)
        attn_lat = _attn_lat(l, qt, k, vt, ck, cv)
        x1, h2, route, gate4, cnt = _merge(l, xs, attn_ctx, attn_lat, sgu_n, cbz, conv_w, on, w_out_b, mod, g2,
                                           wr_cat, wr_hi, rb)
        counts = cnt[:, :N_EXPERT_GROUPS, 0]
        out = _moe(l, counts, h2, route, gate4, tri, w1, w3, w2, x1, mod, fn, final=(l == DEPTH - 1))
        xs = (out,)
        ctx_ks.append(k32.reshape(BATCH, SEQ, N_KV_HEADS, HEAD_DIM))
        ctx_vs.append(v32.reshape(BATCH, SEQ, N_KV_HEADS, HEAD_DIM))

    y_prompt, y_sample = out
    return (y_prompt.reshape(BATCH, SEQ, D_MODEL), y_sample.reshape(DEC_BATCH, DEC_SEQ, D_MODEL),
            jnp.stack(ctx_ks, axis=1), jnp.stack(ctx_vs, axis=1))
```

```python
import functools
import math

import jax
import jax.numpy as jnp
import numpy as np
from jax import lax
from jax.experimental import pallas as pl
from jax.experimental.pallas import tpu as pltpu

D_MODEL = 1024
BATCH = 16
SEQ = 256
DEPTH = 2
DEC_BATCH = 2
DEC_SEQ = 4096
PAST_LEN = 512
GRID_W = 64
N_HEADS = 8
N_KV_HEADS = 2
HEAD_DIM = 64
ROPE_AXIS_DIM = HEAD_DIM // 2
ROPE_THETA = 10000.0
D_ATTN = N_HEADS * HEAD_DIM
D_KV = N_KV_HEADS * HEAD_DIM
CHUNK = 128
N_SGU_GROUPS = 4
SGU_GROUP_DIM = 64
D_SGU = N_SGU_GROUPS * SGU_GROUP_DIM
D_CONV = 256
D_MIX = D_ATTN + D_SGU + D_CONV
D_IN = D_ATTN + 2 * D_KV + 2 * D_SGU + 3 * D_CONV
N_EXPERT_GROUPS = 4
EXPERTS_PER_GROUP = 4
N_EXPERTS = N_EXPERT_GROUPS * EXPERTS_PER_GROUP
D_EXPERT = 256
D_GROUP_HID = EXPERTS_PER_GROUP * D_EXPERT
EPS = 1e-6

T_CTX = BATCH * SEQ
T_LAT = DEC_BATCH * DEC_SEQ
T_ALL = T_CTX + T_LAT
TM = 512
N_TILES = T_ALL // TM
N_CTX_TILES = T_CTX // TM
LAT_TILES_PER_BATCH = DEC_SEQ // TM
TM_MOE = 1024
LANES = 128
SUBLANES = 8
BF16_ROWS = 16
N_MOD_ROWS = 8
N_MOD_PARTS = 6
ROUTER_COLS = 128
VMEM_LIMIT = 48 * 1024 * 1024

OFF_Q = 0
OFF_KV = D_ATTN
OFF_SGU = OFF_KV + 2 * D_KV
OFF_CONV = OFF_SGU + 2 * D_SGU

BF16 = jnp.bfloat16
F32 = jnp.float32
NEG_BIG = -1e30
Q_SCALE = HEAD_DIM ** -0.5 * math.log2(math.e)


def _params(n_grid_dims):
    return pltpu.CompilerParams(
        dimension_semantics=("arbitrary",) * n_grid_dims,
        vmem_limit_bytes=VMEM_LIMIT,
    )


def _mod_row_of_tile(i, tile=TM):
    n_ctx = T_CTX // tile
    return jnp.where(i < n_ctx, 0, 1 + (i - n_ctx) // (DEC_SEQ // tile))


def _mod_part(mod_ref, part, row):
    return mod_ref[part, pl.ds(row, 1), :]


def _rope_block_of_tile(i):
    return jnp.where(i < N_CTX_TILES, 0, 1 + (i - N_CTX_TILES) % LAT_TILES_PER_BATCH)


def _rms(x, gain):
    ms = jnp.mean(x * x, axis=-1, keepdims=True)
    return x * lax.rsqrt(ms + EPS) * gain


def _ctx_tile(i):
    return jnp.minimum(i, N_CTX_TILES - 1)


def _lat_tile(i):
    return jnp.maximum(i - N_CTX_TILES, 0)


MOD_TN = 512


def _mod_kernel(ct_ref, w_ref, b_ref, o_ref):
    c = ct_ref[...]
    s = c / (1.0 + jnp.exp(-c))
    w = w_ref[...]
    row = lax.broadcasted_iota(jnp.int32, (N_MOD_ROWS, MOD_TN), 0)
    out = jnp.zeros((N_MOD_ROWS, MOD_TN), F32)
    for r in range(1 + DEC_BATCH):
        acc = jnp.sum(w * s[:, r:r + 1], axis=0, keepdims=True)
        out = jnp.where(row == r, acc, out)
    o_ref[...] = out + b_ref[...]


def _modulation(cvecs_t, w_ada, b_ada):
    n_col = N_MOD_PARTS * D_MODEL
    per_part = D_MODEL // MOD_TN
    return pl.pallas_call(
        _mod_kernel,
        grid=(DEPTH, n_col // MOD_TN),
        in_specs=[
            pl.BlockSpec((D_MODEL, N_MOD_ROWS), lambda l, j: (0, 0)),
            pl.BlockSpec((None, D_MODEL, MOD_TN), lambda l, j: (l, 0, j)),
            pl.BlockSpec((None, 1, MOD_TN), lambda l, j: (l, 0, j)),
        ],
        out_specs=pl.BlockSpec((None, None, N_MOD_ROWS, MOD_TN),
                               lambda l, j: (l, j // per_part, 0, j % per_part)),
        out_shape=jax.ShapeDtypeStruct((DEPTH, N_MOD_PARTS, N_MOD_ROWS, D_MODEL), F32),
        compiler_params=_params(2),
        name="adaln_modulation",
    )(cvecs_t, w_ada, b_ada.reshape(DEPTH, 1, n_col))


_MOD_SPEC = lambda l, nd: pl.BlockSpec(
    (None, N_MOD_PARTS, N_MOD_ROWS, D_MODEL), lambda *idx: (l, 0, 0, 0))


def _swap16(x):
    lane = lax.broadcasted_iota(jnp.int32, x.shape, 1)
    up = pltpu.roll(x, 16, 1)
    down = pltpu.roll(x, LANES - 16, 1)
    return jnp.where((lane & 16) != 0, up, down)


def _rope(x, cos_t, sin_t):
    cols = []
    for j in range(x.shape[1] // LANES):
        xc = x[:, j * LANES:(j + 1) * LANES]
        cols.append(xc * cos_t + _swap16(xc) * sin_t)
    return cols[0] if len(cols) == 1 else jnp.concatenate(cols, axis=1)


def _proj_kernel(*refs, split_x):
    if split_x:
        xa_ref, xb_ref = refs[:2]
        refs = refs[2:]
    else:
        x_ref = refs[0]
        refs = refs[1:]
    (mod_ref, g1_ref, w_ref, qg_ref, kg_ref, cos_ref, sin_ref, bq_ref, sw_ref, sb_ref, on_ref,
     qt_ref, k_ref, vt_ref, k32_ref, v32_ref, sgu_ref, cbz_ref) = refs
    i = pl.program_id(0)
    is_ctx = i < N_CTX_TILES
    x = jnp.where(is_ctx, xa_ref[...], xb_ref[...]) if split_x else x_ref[...]
    mrow = _mod_row_of_tile(i)
    h = _rms(x, g1_ref[...]) * (1.0 + _mod_part(mod_ref, 1, mrow)) + _mod_part(mod_ref, 0, mrow)
    hb = h.astype(BF16)
    cos_t = cos_ref[...]
    sin_t = sin_ref[...]

    q = jnp.dot(hb, w_ref[:, OFF_Q:OFF_Q + D_ATTN], preferred_element_type=F32)
    qms = jnp.dot((q * q).astype(BF16), bq_ref[...], preferred_element_type=F32)
    qn = q * lax.rsqrt(qms + EPS) * qg_ref[...]
    qr = _rope(qn, cos_t, sin_t) * Q_SCALE
    qt_ref[...] = qr.T.astype(BF16)

    kv = jnp.dot(hb, w_ref[:, OFF_KV:OFF_KV + 2 * D_KV], preferred_element_type=F32)
    k = kv[:, :D_KV]
    v = kv[:, D_KV:]
    kms = jnp.dot((k * k).astype(BF16), bq_ref[:D_KV, :D_KV], preferred_element_type=F32)
    kn = k * lax.rsqrt(kms + EPS) * kg_ref[...]

    @pl.when(is_ctx)
    def _():
        k32_ref[...] = kn
        v32_ref[...] = v

    k_ref[...] = _rope(kn, cos_t, sin_t).astype(BF16)
    vt_ref[...] = v.T.astype(BF16)

    uv = jnp.dot(hb, w_ref[:, OFF_SGU:OFF_SGU + 2 * D_SGU], preferred_element_type=F32)
    su = uv[:, :D_SGU]
    sv = uv[:, D_SGU:]
    grp = lax.broadcasted_iota(jnp.int32, (CHUNK, D_SGU), 1) // SGU_GROUP_DIM
    on_sgu = on_ref[:, D_ATTN:D_ATTN + D_SGU]
    for n in range(TM // CHUNK):
        svc = sv[n * CHUNK:(n + 1) * CHUNK, :]
        mixed = sb_ref[...]
        for pair in range(N_SGU_GROUPS // 2):
            rhs = jnp.concatenate(
                [jnp.where(grp == 2 * pair, svc, 0.0), jnp.where(grp == 2 * pair + 1, svc, 0.0)],
                axis=0).astype(BF16)
            mixed = mixed + jnp.dot(sw_ref[pair], rhs, preferred_element_type=F32)
        sgu = su[n * CHUNK:(n + 1) * CHUNK, :] * mixed
        sgu_ref[n * CHUNK:(n + 1) * CHUNK, :] = _rms(sgu, on_sgu).astype(BF16)

    c3 = jnp.dot(hb, w_ref[:, OFF_CONV:OFF_CONV + 3 * D_CONV], preferred_element_type=F32)
    cbz_ref[:, :D_CONV] = c3[:, :D_CONV]
    cbz_ref[:, D_CONV:] = c3[:, D_CONV:2 * D_CONV] * c3[:, 2 * D_CONV:]


def _x_specs(xs):
    if len(xs) == 1:
        return [pl.BlockSpec((TM, D_MODEL), lambda i: (i, 0))]
    return [pl.BlockSpec((TM, D_MODEL), lambda i: (_ctx_tile(i), 0)),
            pl.BlockSpec((TM, D_MODEL), lambda i: (_lat_tile(i), 0))]


def _proj(l, xs, mod, g1, w_in, qg, kg, cos_t, sin_t, bq, sgu_w, sgu_b, on):
    lay = lambda *rest: (lambda i: (l,) + rest)
    return pl.pallas_call(
        functools.partial(_proj_kernel, split_x=len(xs) == 2),
        grid=(N_TILES,),
        in_specs=_x_specs(xs) + [
            _MOD_SPEC(l, 1),
            pl.BlockSpec((None, 1, D_MODEL), lay(0, 0)),
            pl.BlockSpec((None, D_MODEL, D_IN), lay(0, 0)),
            pl.BlockSpec((None, 1, D_ATTN), lay(0, 0)),
            pl.BlockSpec((None, 1, D_KV), lay(0, 0)),
            pl.BlockSpec((TM, LANES), lambda i: (_rope_block_of_tile(i), 0)),
            pl.BlockSpec((TM, LANES), lambda i: (_rope_block_of_tile(i), 0)),
            pl.BlockSpec((D_ATTN, D_ATTN), lambda i: (0, 0)),
            pl.BlockSpec((None, N_SGU_GROUPS // 2, CHUNK, 2 * CHUNK), lay(0, 0, 0)),
            pl.BlockSpec((None, CHUNK, D_SGU), lay(0, 0)),
            pl.BlockSpec((None, 1, D_MIX), lay(0, 0)),
        ],
        out_specs=[
            pl.BlockSpec((None, D_ATTN, TM), lambda i: (i, 0, 0)),
            pl.BlockSpec((TM, D_KV), lambda i: (i, 0)),
            pl.BlockSpec((None, D_KV, TM), lambda i: (i, 0, 0)),
            pl.BlockSpec((TM, D_KV), lambda i: (_ctx_tile(i), 0)),
            pl.BlockSpec((TM, D_KV), lambda i: (_ctx_tile(i), 0)),
            pl.BlockSpec((TM, D_SGU), lambda i: (i, 0)),
            pl.BlockSpec((TM, 2 * D_CONV), lambda i: (i, 0)),
        ],
        out_shape=[
            jax.ShapeDtypeStruct((N_TILES, D_ATTN, TM), BF16),
            jax.ShapeDtypeStruct((T_ALL, D_KV), BF16),
            jax.ShapeDtypeStruct((N_TILES, D_KV, TM), BF16),
            jax.ShapeDtypeStruct((T_CTX, D_KV), F32),
            jax.ShapeDtypeStruct((T_CTX, D_KV), F32),
            jax.ShapeDtypeStruct((T_ALL, D_SGU), BF16),
            jax.ShapeDtypeStruct((T_ALL, 2 * D_CONV), F32),
        ],
        compiler_params=_params(1),
        name=f"proj_l{l}",
    )(*xs, mod, g1, w_in, qg, kg, cos_t, sin_t, bq, sgu_w, sgu_b, on)


ACC_ROWS = HEAD_DIM + BF16_ROWS
N_PAIRS = N_HEADS // 2
PAIRS_PER_KV = N_PAIRS // N_KV_HEADS


def _pair_queries(qt, kv_idx, tq):
    qf = qt.astype(F32)
    mine = lax.broadcasted_iota(jnp.int32, (2 * HEAD_DIM, tq), 0) // HEAD_DIM == kv_idx
    qe = qf[:HEAD_DIM]
    qo = qf[HEAD_DIM:]
    rhs = jnp.concatenate(
        [jnp.where(mine, jnp.concatenate([qe, qe], axis=0), 0.0),
         jnp.where(mine, jnp.concatenate([qo, qo], axis=0), 0.0)], axis=1)
    return rhs.astype(BF16)


def _with_ones(vt_c):
    return jnp.concatenate([vt_c, jnp.ones((BF16_ROWS, vt_c.shape[1]), BF16)], axis=0)


def _flash_pairs(rhs_list, chunk_lists, tq):
    n_pairs, n_chunks = len(rhs_list), len(chunk_lists[0])
    scores = lambda s, c: jnp.dot(chunk_lists[s][c][0](), rhs_list[s], preferred_element_type=F32)
    m = [jnp.full((1, 2 * tq), NEG_BIG, F32)] * n_pairs
    acc = [jnp.zeros((ACC_ROWS, 2 * tq), F32)] * n_pairs
    st = [scores(s, 0) for s in range(n_pairs)]
    for c in range(n_chunks):
        for s in range(n_pairs):
            m_new = jnp.maximum(m[s], jnp.max(st[s], axis=0, keepdims=True))
            st_next = scores(s, c + 1) if c + 1 < n_chunks else None
            alpha = jnp.exp2(m[s] - m_new)
            p = jnp.exp2(st[s] - m_new).astype(BF16)
            acc[s] = alpha * acc[s] + jnp.dot(chunk_lists[s][c][1](), p, preferred_element_type=F32)
            m[s] = m_new
            st[s] = st_next
    outs = []
    for s in range(n_pairs):
        o = acc[s][:HEAD_DIM] * (1.0 / acc[s][HEAD_DIM:HEAD_DIM + 1])
        outs.append(jnp.concatenate([o[:, :tq], o[:, tq:]], axis=0))
    return outs


def _attn_ctx_kernel(qt_ref, k_ref, vt_ref, o_ref):
    tq = SEQ
    rhs_list, chunk_lists = [], []
    for pair in range(N_PAIRS):
        kv_idx = pair // PAIRS_PER_KV
        rhs_list.append(_pair_queries(qt_ref[pair * LANES:(pair + 1) * LANES, :], kv_idx, tq))
        chunk_lists.append([(lambda: k_ref[...],
                             lambda kv_idx=kv_idx: _with_ones(vt_ref[kv_idx * HEAD_DIM:(kv_idx + 1) * HEAD_DIM, :]))])
    for pair, o_t in enumerate(_flash_pairs(rhs_list, chunk_lists, tq)):
        o_ref[:, pair * LANES:(pair + 1) * LANES] = o_t.T.astype(BF16)


def _attn_ctx(l, qt, k, vt):
    halves = TM // SEQ
    return pl.pallas_call(
        _attn_ctx_kernel,
        grid=(BATCH,),
        in_specs=[
            pl.BlockSpec((None, D_ATTN, SEQ), lambda b: (b // halves, 0, b % halves)),
            pl.BlockSpec((SEQ, D_KV), lambda b: (b, 0)),
            pl.BlockSpec((None, D_KV, SEQ), lambda b: (b // halves, 0, b % halves)),
        ],
        out_specs=pl.BlockSpec((SEQ, D_ATTN), lambda b: (b, 0)),
        out_shape=jax.ShapeDtypeStruct((T_CTX, D_ATTN), BF16),
        compiler_params=_params(1),
        name=f"attn_ctx_l{l}",
    )(qt, k, vt)


KC = 512
LAT_PAIRS_PER_STEP = 2


def _attn_lat_kernel(qt_ref, k_ref, vt_ref, ck_ref, cv_ref, o_ref):
    tq = TM
    cvt = cv_ref[...].T
    rhs_list, chunk_lists = [], []
    for j in range(LAT_PAIRS_PER_STEP):
        kv_idx = (pl.program_id(2) * LAT_PAIRS_PER_STEP + j) // PAIRS_PER_KV
        rhs_list.append(_pair_queries(qt_ref[j * LANES:(j + 1) * LANES, :], kv_idx, tq))
        v_row = pl.multiple_of(kv_idx * HEAD_DIM, HEAD_DIM)

        def latent_chunk(c, v_row=v_row):
            tile, lane0 = (c * KC) // TM, (c * KC) % TM
            return (lambda: k_ref[c * KC:(c + 1) * KC, :],
                    lambda: _with_ones(vt_ref[tile, pl.ds(v_row, HEAD_DIM), lane0:lane0 + KC]))

        def cached_chunk(c, kv_idx=kv_idx):
            return (lambda: ck_ref[c * KC:(c + 1) * KC, :].astype(BF16),
                    lambda: _with_ones(jnp.where(kv_idx == 0, cvt[:HEAD_DIM, c * KC:(c + 1) * KC],
                                                 cvt[HEAD_DIM:, c * KC:(c + 1) * KC]).astype(BF16)))

        chunks = [latent_chunk(c) for c in range(DEC_SEQ // KC)]
        chunks += [cached_chunk(c) for c in range(PAST_LEN // KC)]
        chunk_lists.append(chunks)
    for j, o_t in enumerate(_flash_pairs(rhs_list, chunk_lists, tq)):
        o_ref[:, j * LANES:(j + 1) * LANES] = o_t.T.astype(BF16)


def _attn_lat(l, qt, k, vt, cache_k, cache_v):
    n_q = DEC_SEQ // TM
    first = N_CTX_TILES
    ctx_blocks = T_CTX // DEC_SEQ
    width = LAT_PAIRS_PER_STEP * LANES
    return pl.pallas_call(
        _attn_lat_kernel,
        grid=(DEC_BATCH, n_q, N_PAIRS // LAT_PAIRS_PER_STEP),
        in_specs=[
            pl.BlockSpec((None, width, TM), lambda b, i, p: (first + b * n_q + i, p, 0)),
            pl.BlockSpec((DEC_SEQ, D_KV), lambda b, i, p: (ctx_blocks + b, 0)),
            pl.BlockSpec((n_q, D_KV, TM), lambda b, i, p: (ctx_blocks + b, 0, 0)),
            pl.BlockSpec((None, None, PAST_LEN, D_KV), lambda b, i, p: (b, l, 0, 0)),
            pl.BlockSpec((None, None, PAST_LEN, D_KV), lambda b, i, p: (b, l, 0, 0)),
        ],
        out_specs=pl.BlockSpec((TM, width), lambda b, i, p: (b * n_q + i, p)),
        out_shape=jax.ShapeDtypeStruct((T_LAT, D_ATTN), BF16),
        compiler_params=_params(3),
        name=f"attn_lat_l{l}",
    )(qt, k, vt, cache_k, cache_v)


def _first_max_index(vals, valid, rowf):
    masked = jnp.where(valid, vals, -jnp.inf)
    mx = jnp.max(masked, axis=0, keepdims=True)
    idx = jnp.min(jnp.where(masked == mx, rowf, float(SUBLANES)), axis=0, keepdims=True)
    return mx, idx


def _merge_kernel(*refs, split_x):
    if split_x:
        xa_ref, xb_ref = refs[:2]
        refs = refs[2:]
    else:
        x_ref = refs[0]
        refs = refs[1:]
    (actx_ref, alat_ref, sgu_ref, cbz_ref, prev_ref, next_ref, cw_ref, on_ref,
     wo_ref, mod_ref, g2_ref, wr_ref, wrhi_ref, rb_ref, x1_ref, h2_ref, route_ref, gate4_ref, cnt_ref) = refs
    i = pl.program_id(0)
    is_ctx = i < N_CTX_TILES
    mrow = _mod_row_of_tile(i)
    on = on_ref[...]

    cbz = cbz_ref[...]
    cb = cbz[:, :D_CONV]
    z = cbz[:, D_CONV:]
    seq_mask = jnp.where(is_ctx, SEQ - 1, DEC_SEQ - 1)
    row = lax.broadcasted_iota(jnp.int32, (TM, D_CONV), 0)
    pos = (i * TM + row) & seq_mask
    z_prev = jnp.where(row == 0, prev_ref[SUBLANES - 1:SUBLANES, D_CONV:], pltpu.roll(z, 1, 0))
    z_next = jnp.where(row == TM - 1, next_ref[0:1, D_CONV:], pltpu.roll(z, TM - 1, 0))
    z_prev = jnp.where(pos == 0, 0.0, z_prev)
    z_next = jnp.where(pos == seq_mask, 0.0, z_next)
    cw = cw_ref[...]
    conv = cb * (z_prev * cw[0:1, :] + z * cw[1:2, :] + z_next * cw[2:3, :])
    conv_n = _rms(conv, on[:, D_ATTN + D_SGU:]).astype(BF16)

    attn = jnp.where(is_ctx, actx_ref[...].astype(F32), alat_ref[...].astype(F32))
    attn_n = _rms(attn, on[:, :D_ATTN]).astype(BF16)
    merged = jnp.concatenate([attn_n, sgu_ref[...], conv_n], axis=1)
    x = jnp.where(is_ctx, xa_ref[...], xb_ref[...]) if split_x else x_ref[...]
    x1 = x + _mod_part(mod_ref, 2, mrow) * jnp.dot(merged, wo_ref[...], preferred_element_type=F32)
    x1_ref[...] = x1

    h2 = _rms(x1, g2_ref[...]) * (1.0 + _mod_part(mod_ref, 4, mrow)) + _mod_part(mod_ref, 3, mrow)
    hb = h2.astype(BF16)
    h2_ref[...] = hb

    h_lo = (h2 - hb.astype(F32)).astype(BF16)
    a = jnp.dot(hb, wr_ref[...], preferred_element_type=F32)
    b = jnp.dot(h_lo, wrhi_ref[...], preferred_element_type=F32)
    logits = a[:, :ROUTER_COLS] + a[:, ROUTER_COLS:] + b + rb_ref[...]
    lt = logits.T

    rowi = lax.broadcasted_iota(jnp.int32, (SUBLANES, TM), 0).astype(F32)
    valid = rowi < float(N_EXPERT_GROUPS)
    g_log = lt[0:SUBLANES]
    g_max, g_idx = _first_max_index(g_log, valid, rowi)
    p_g = 1.0 / jnp.sum(jnp.where(valid, jnp.exp(g_log - g_max), 0.0), axis=0, keepdims=True)
    e_log = jnp.zeros((SUBLANES, TM), F32)
    for g in range(N_EXPERT_GROUPS):
        e_log = jnp.where(g_idx == g, lt[(g + 1) * SUBLANES:(g + 2) * SUBLANES], e_log)
    e_max, i1 = _first_max_index(e_log, valid, rowi)
    e_exp = jnp.where(valid, jnp.exp(e_log - e_max), 0.0)
    e_prob = e_exp / jnp.sum(e_exp, axis=0, keepdims=True)
    v1 = jnp.max(e_prob, axis=0, keepdims=True)
    rest = jnp.logical_and(valid, rowi != i1)
    v2, i2 = _first_max_index(e_prob, rest, rowi)
    denom = v1 + v2
    w_sel = p_g * jnp.where(rowi == i1, v1 / denom, jnp.where(rowi == i2, v2 / denom, 0.0))
    route_ref[...] = jnp.broadcast_to(g_idx, (SUBLANES, TM))
    pad = jnp.zeros((LANES - SUBLANES, TM), F32)
    gate4_ref[...] = jnp.concatenate([w_sel, pad], axis=0).T
    counts = jnp.sum(jnp.where(rowi == g_idx, 1.0, 0.0), axis=1, keepdims=True)
    cnt_ref[...] = jnp.broadcast_to(counts, (SUBLANES, LANES)).astype(jnp.int32)


def _merge(l, xs, attn_ctx, attn_lat, sgu_n, cbz, conv_w, on, w_out, mod, g2, wr_cat, wr_hi, rb):
    lay = lambda *rest: (lambda i: (l,) + rest)
    rows8 = TM // SUBLANES
    return pl.pallas_call(
        functools.partial(_merge_kernel, split_x=len(xs) == 2),
        grid=(N_TILES,),
        in_specs=_x_specs(xs) + [
            pl.BlockSpec((TM, D_ATTN), lambda i: (_ctx_tile(i), 0)),
            pl.BlockSpec((TM, D_ATTN), lambda i: (_lat_tile(i), 0)),
            pl.BlockSpec((TM, D_SGU), lambda i: (i, 0)),
            pl.BlockSpec((TM, 2 * D_CONV), lambda i: (i, 0)),
            pl.BlockSpec((SUBLANES, 2 * D_CONV), lambda i: (jnp.maximum(i * rows8 - 1, 0), 0)),
            pl.BlockSpec((SUBLANES, 2 * D_CONV),
                         lambda i: (jnp.minimum((i + 1) * rows8, T_ALL // SUBLANES - 1), 0)),
            pl.BlockSpec((None, 3, D_CONV), lay(0, 0)),
            pl.BlockSpec((None, 1, D_MIX), lay(0, 0)),
            pl.BlockSpec((None, D_MIX, D_MODEL), lay(0, 0)),
            _MOD_SPEC(l, 1),
            pl.BlockSpec((None, 1, D_MODEL), lay(0, 0)),
            pl.BlockSpec((None, D_MODEL, 2 * ROUTER_COLS), lay(0, 0)),
            pl.BlockSpec((None, D_MODEL, ROUTER_COLS), lay(0, 0)),
            pl.BlockSpec((None, 1, ROUTER_COLS), lay(0, 0)),
        ],
        out_specs=[
            pl.BlockSpec((TM, D_MODEL), lambda i: (i, 0)),
            pl.BlockSpec((TM, D_MODEL), lambda i: (i, 0)),
            pl.BlockSpec((None, SUBLANES, TM), lambda i: (i, 0, 0)),
            pl.BlockSpec((TM, LANES), lambda i: (i, 0)),
            pl.BlockSpec((None, SUBLANES, LANES), lambda i: (i, 0, 0)),
        ],
        out_shape=[
            jax.ShapeDtypeStruct((T_ALL, D_MODEL), F32),
            jax.ShapeDtypeStruct((T_ALL, D_MODEL), BF16),
            jax.ShapeDtypeStruct((N_TILES, SUBLANES, TM), F32),
            jax.ShapeDtypeStruct((T_ALL, LANES), F32),
            jax.ShapeDtypeStruct((N_TILES, SUBLANES, LANES), jnp.int32),
        ],
        compiler_params=_params(1),
        name=f"merge_l{l}",
    )(*xs, attn_ctx, attn_lat, sgu_n, cbz, cbz, cbz, conv_w, on, w_out, mod, g2, wr_cat, wr_hi, rb)


RB = 144
SORT_ROWS = TM + LANES
SORT_ROWS_PAD = -(-(SORT_ROWS + RB) // BF16_ROWS) * BF16_ROWS
MOE_VMEM_LIMIT = 56 * 1024 * 1024


def _split3(x):
    hi = x.astype(BF16)
    r1 = x - hi.astype(F32)
    mid = r1.astype(BF16)
    lo = (r1 - mid.astype(F32)).astype(BF16)
    return hi, mid, lo


def _moe_kernel(cnt_ref, h_ref, route_ref, gate4_ref, tri_ref, w1_ref, w3_ref, w2_ref, x1_ref, mod_ref, fn_ref,
                *refs, final):
    out_refs, (xs_ref, gs_ref, zs_ref) = refs[:-3], refs[-3:]
    i = pl.program_id(0)
    gate2 = _mod_part(mod_ref, 5, _mod_row_of_tile(i))
    counts = [cnt_ref[i, g] for g in range(N_EXPERT_GROUPS)]
    starts = [jnp.int32(0)]
    for g in range(N_EXPERT_GROUPS - 1):
        starts.append(starts[-1] + (counts[g] + (BF16_ROWS - 1)) // BF16_ROWS * BF16_ROWS)

    g_idx = route_ref[0:1, :]
    rowi = lax.broadcasted_iota(jnp.int32, (SUBLANES, TM), 0).astype(F32)
    onehot = jnp.where(rowi == g_idx, 1.0, 0.0)
    incl = jnp.dot(onehot.astype(BF16), tri_ref[...], preferred_element_type=F32)
    pos = jnp.sum(onehot * incl, axis=0, keepdims=True) - 1.0
    for g in range(1, N_EXPERT_GROUPS):
        pos = pos + jnp.where(g_idx == float(g), starts[g].astype(F32), 0.0)
    sub = lax.broadcasted_iota(jnp.int32, (SORT_ROWS, TM), 0).astype(F32)
    perm = jnp.where(sub == pos, 1.0, 0.0).astype(BF16)
    pos_col = jnp.broadcast_to(pos, (LANES, TM)).T
    lane = lax.broadcasted_iota(jnp.int32, (TM, SORT_ROWS), 1).astype(F32)
    perm_t = jnp.where(lane == jnp.concatenate([pos_col] * (SORT_ROWS // LANES), axis=1), 1.0, 0.0).astype(BF16)

    xs_ref[:SORT_ROWS, :] = jnp.dot(perm, h_ref[...], preferred_element_type=F32).astype(BF16)
    xs_ref[SORT_ROWS:, :] = jnp.zeros((SORT_ROWS_PAD - SORT_ROWS, D_MODEL), BF16)
    hi, mid, lo = _split3(gate4_ref[...])
    packed = (hi.astype(F32) + pltpu.roll(mid.astype(F32), EXPERTS_PER_GROUP, 1)
              + pltpu.roll(lo.astype(F32), 2 * EXPERTS_PER_GROUP, 1)).astype(BF16)
    gsp = jnp.dot(perm, packed, preferred_element_type=F32)
    gs_ref[:SORT_ROWS, :] = (gsp + pltpu.roll(gsp, LANES - EXPERTS_PER_GROUP, 1)
                             + pltpu.roll(gsp, LANES - 2 * EXPERTS_PER_GROUP, 1))
    gs_ref[SORT_ROWS:, :] = jnp.zeros((SORT_ROWS_PAD - SORT_ROWS, LANES), F32)
    zs_ref[...] = jnp.zeros((SORT_ROWS_PAD, D_MODEL), BF16)

    for g in range(N_EXPERT_GROUPS):
        def block_ffn(blk, carry, g=g):
            row0 = pl.multiple_of(starts[g] + blk * RB, BF16_ROWS)
            xb = xs_ref[pl.ds(row0, RB), :]
            gb = gs_ref[pl.ds(row0, RB), :]
            cols = []
            for e in range(EXPERTS_PER_GROUP):
                a = jnp.dot(xb, w1_ref[g * EXPERTS_PER_GROUP + e], preferred_element_type=F32)
                b = jnp.dot(xb, w3_ref[g * EXPERTS_PER_GROUP + e], preferred_element_type=F32)
                cols.append((a / (1.0 + jnp.exp(-a)) * b * gb[:, e:e + 1]).astype(BF16))
            z = jnp.dot(jnp.concatenate(cols, axis=1), w2_ref[g], preferred_element_type=F32)
            zs_ref[pl.ds(row0, RB), :] = z.astype(BF16)
            return carry

        lax.fori_loop(0, (counts[g] + (RB - 1)) // RB, block_ffn, 0)

    y = jnp.dot(perm_t, zs_ref[:SORT_ROWS, :], preferred_element_type=F32)
    out = x1_ref[...] + gate2 * y

    if final:
        octx_ref, olat_ref = out_refs
        out = _rms(out, fn_ref[...])

        @pl.when(i < N_CTX_TILES)
        def _():
            octx_ref[...] = out

        @pl.when(i >= N_CTX_TILES)
        def _():
            olat_ref[...] = out
    else:
        out_refs[0][...] = out


def _moe(l, counts, h2, route, gate4, tri, w1, w3, w2, x1, mod, final_norm, final):
    tile = pl.BlockSpec((TM, D_MODEL), lambda i, c: (i, 0))
    resident = pl.Buffered(1)
    if final:
        out_specs = [pl.BlockSpec((TM, D_MODEL), lambda i, c: (_ctx_tile(i), 0)),
                     pl.BlockSpec((TM, D_MODEL), lambda i, c: (_lat_tile(i), 0))]
        out_shape = [jax.ShapeDtypeStruct((T_CTX, D_MODEL), F32), jax.ShapeDtypeStruct((T_LAT, D_MODEL), F32)]
    else:
        out_specs = tile
        out_shape = jax.ShapeDtypeStruct((T_ALL, D_MODEL), F32)
    grid_spec = pltpu.PrefetchScalarGridSpec(
        num_scalar_prefetch=1,
        grid=(N_TILES,),
        in_specs=[
            tile,
            pl.BlockSpec((None, SUBLANES, TM), lambda i, c: (i, 0, 0)),
            pl.BlockSpec((TM, LANES), lambda i, c: (i, 0)),
            pl.BlockSpec((TM, TM), lambda i, c: (0, 0), pipeline_mode=resident),
            pl.BlockSpec((None, N_EXPERTS, D_MODEL, D_EXPERT), lambda i, c: (l, 0, 0, 0), pipeline_mode=resident),
            pl.BlockSpec((None, N_EXPERTS, D_MODEL, D_EXPERT), lambda i, c: (l, 0, 0, 0), pipeline_mode=resident),
            pl.BlockSpec((None, N_EXPERT_GROUPS, D_GROUP_HID, D_MODEL), lambda i, c: (l, 0, 0, 0),
                         pipeline_mode=resident),
            tile,
            pl.BlockSpec((None, N_MOD_PARTS, N_MOD_ROWS, D_MODEL), lambda i, c: (l, 0, 0, 0)),
            pl.BlockSpec((1, D_MODEL), lambda i, c: (0, 0)),
        ],
        out_specs=out_specs,
        scratch_shapes=[pltpu.VMEM((SORT_ROWS_PAD, D_MODEL), BF16),
                        pltpu.VMEM((SORT_ROWS_PAD, LANES), F32),
                        pltpu.VMEM((SORT_ROWS_PAD, D_MODEL), BF16)],
    )
    return pl.pallas_call(
        functools.partial(_moe_kernel, final=final),
        grid_spec=grid_spec,
        out_shape=out_shape,
        compiler_params=pltpu.CompilerParams(dimension_semantics=("arbitrary",),
                                             vmem_limit_bytes=MOE_VMEM_LIMIT),
        name=f"moe_l{l}",
    )(counts, h2, route, gate4, tri, w1, w3, w2, x1, mod, final_norm)


def _rope_tables():
    rows = DEC_SEQ // GRID_W
    row = np.repeat(np.arange(rows, dtype=np.float32), GRID_W)
    col = np.tile(np.arange(GRID_W, dtype=np.float32), rows)
    half = ROPE_AXIS_DIM // 2
    inv_freq = (1.0 / (np.float32(ROPE_THETA) ** (np.arange(half, dtype=np.float32) * np.float32(2.0)
                                                  / np.float32(ROPE_AXIS_DIM)))).astype(np.float32)
    ar = row[:, None] * inv_freq
    ac = col[:, None] * inv_freq
    cos64 = np.concatenate([np.cos(ar), np.cos(ar), np.cos(ac), np.cos(ac)], axis=1)
    sin64 = np.concatenate([-np.sin(ar), np.sin(ar), -np.sin(ac), np.sin(ac)], axis=1)
    cos_t = np.concatenate([np.ones((TM, LANES), np.float32), np.tile(cos64, (1, 2))], axis=0)
    sin_t = np.concatenate([np.zeros((TM, LANES), np.float32), np.tile(sin64, (1, 2))], axis=0)
    return jnp.asarray(cos_t, F32), jnp.asarray(sin_t, F32)


def _router_weights(router_g_w, router_g_b, router_e_w, router_e_b):
    zw = jnp.zeros((DEPTH, D_MODEL, SUBLANES - N_EXPERT_GROUPS), F32)
    zb = jnp.zeros((DEPTH, SUBLANES - N_EXPERT_GROUPS), F32)
    w_cols, b_cols = [router_g_w, zw], [router_g_b, zb]
    for g in range(N_EXPERT_GROUPS):
        w_cols += [router_e_w[:, g], zw]
        b_cols += [router_e_b[:, g], zb]
    used = (1 + N_EXPERT_GROUPS) * SUBLANES
    w_cols.append(jnp.zeros((DEPTH, D_MODEL, ROUTER_COLS - used), F32))
    b_cols.append(jnp.zeros((DEPTH, ROUTER_COLS - used), F32))
    w = jnp.concatenate(w_cols, axis=2)
    b = jnp.concatenate(b_cols, axis=1).reshape(DEPTH, 1, ROUTER_COLS)
    w_hi = w.astype(BF16)
    w_lo = (w - w_hi.astype(F32)).astype(BF16)
    return jnp.concatenate([w_hi, w_lo], axis=2), w_hi, b


def kernel(x_prompt, x_sample, cache_k, cache_v, c, c_ctx, w_ada, b_ada, norm1, w_in, q_norm, k_norm,
           sgu_w, sgu_b, conv_w, out_norm, w_out, norm2, router_g_w, router_g_b, router_e_w,
           router_e_b, moe_w1, moe_w3, moe_w2, final_norm):
    xs = (x_prompt.reshape(T_CTX, D_MODEL), x_sample.reshape(T_LAT, D_MODEL))

    cvecs = jnp.concatenate([c_ctx[None, :], c, jnp.zeros((N_MOD_ROWS - 1 - DEC_BATCH, D_MODEL), F32)], axis=0)
    mod = _modulation(cvecs.T, w_ada, b_ada)

    w_in_b = w_in.astype(BF16)
    w_out_b = w_out.astype(BF16)
    g1 = norm1.reshape(DEPTH, 1, D_MODEL)
    g2 = norm2.reshape(DEPTH, 1, D_MODEL)
    on = out_norm.reshape(DEPTH, 1, D_MIX)
    qg = jnp.tile(q_norm, (1, N_HEADS)).reshape(DEPTH, 1, D_ATTN)
    kg = jnp.tile(k_norm, (1, N_KV_HEADS)).reshape(DEPTH, 1, D_KV)
    cos_t, sin_t = _rope_tables()
    bq = jnp.asarray(np.kron(np.eye(N_HEADS), np.full((HEAD_DIM, HEAD_DIM), 1.0 / HEAD_DIM)), BF16)
    sgu_w_pairs = sgu_w.astype(BF16).reshape(DEPTH, N_SGU_GROUPS // 2, 2, CHUNK, CHUNK)
    sgu_w_pairs = jnp.concatenate([sgu_w_pairs[:, :, 0], sgu_w_pairs[:, :, 1]], axis=-1)
    sgu_bias = jnp.repeat(jnp.swapaxes(sgu_b, 1, 2), SGU_GROUP_DIM, axis=2)
    wr_cat, wr_hi, rb = _router_weights(router_g_w, router_g_b, router_e_w, router_e_b)
    w1 = moe_w1.astype(BF16)
    w3 = moe_w3.astype(BF16)
    w2 = moe_w2.astype(BF16).reshape(DEPTH, N_EXPERT_GROUPS, D_GROUP_HID, D_MODEL)
    fn = final_norm.reshape(1, D_MODEL)
    tri = jnp.asarray(np.triu(np.ones((TM, TM), np.float32)), BF16)
    ck = cache_k.reshape(DEC_BATCH, DEPTH, PAST_LEN, D_KV)
    cv = cache_v.reshape(DEC_BATCH, DEPTH, PAST_LEN, D_KV)

    ctx_ks, ctx_vs = [], []
    for l in range(DEPTH):
        qt, k, vt, k32, v32, sgu_n, cbz = _proj(l, xs, mod, g1, w_in_b, qg, kg, cos_t, sin_t, bq,
                                                sgu_w_pairs, sgu_bias, on)
        attn_ctx = _attn_ctx(l, qt, k, vt)
        attn_lat = _attn_lat(l, qt, k, vt, ck, cv)
        x1, h2, route, gate4, cnt = _merge(l, xs, attn_ctx, attn_lat, sgu_n, cbz, conv_w, on, w_out_b, mod, g2,
                                           wr_cat, wr_hi, rb)
        counts = cnt[:, :N_EXPERT_GROUPS, 0]
        out = _moe(l, counts, h2, route, gate4, tri, w1, w3, w2, x1, mod, fn, final=(l == DEPTH - 1))
        xs = (out,)
        ctx_ks.append(k32.reshape(BATCH, SEQ, N_KV_HEADS, HEAD_DIM))
        ctx_vs.append(v32.reshape(BATCH, SEQ, N_KV_HEADS, HEAD_DIM))

    y_prompt, y_sample = out
    return (y_prompt.reshape(BATCH, SEQ, D_MODEL), y_sample.reshape(DEC_BATCH, DEC_SEQ, D_MODEL),
            jnp.stack(ctx_ks, axis=1), jnp.stack(ctx_vs, axis=1))
```

```python
import functools
import math

import jax
import jax.numpy as jnp
import numpy as np
from jax import lax
from jax.experimental import pallas as pl
from jax.experimental.pallas import tpu as pltpu

D_MODEL = 1024
BATCH = 16
SEQ = 256
DEPTH = 2
DEC_BATCH = 2
DEC_SEQ = 4096
PAST_LEN = 512
GRID_W = 64
N_HEADS = 8
N_KV_HEADS = 2
HEAD_DIM = 64
ROPE_AXIS_DIM = HEAD_DIM // 2
ROPE_THETA = 10000.0
D_ATTN = N_HEADS * HEAD_DIM
D_KV = N_KV_HEADS * HEAD_DIM
CHUNK = 128
N_SGU_GROUPS = 4
SGU_GROUP_DIM = 64
D_SGU = N_SGU_GROUPS * SGU_GROUP_DIM
D_CONV = 256
D_MIX = D_ATTN + D_SGU + D_CONV
D_IN = D_ATTN + 2 * D_KV + 2 * D_SGU + 3 * D_CONV
N_EXPERT_GROUPS = 4
EXPERTS_PER_GROUP = 4
N_EXPERTS = N_EXPERT_GROUPS * EXPERTS_PER_GROUP
D_EXPERT = 256
D_GROUP_HID = EXPERTS_PER_GROUP * D_EXPERT
EPS = 1e-6

T_CTX = BATCH * SEQ
T_LAT = DEC_BATCH * DEC_SEQ
T_ALL = T_CTX + T_LAT
TM = 512
N_TILES = T_ALL // TM
N_CTX_TILES = T_CTX // TM
LAT_TILES_PER_BATCH = DEC_SEQ // TM
TM_MOE = 1024
LANES = 128
SUBLANES = 8
BF16_ROWS = 16
N_MOD_ROWS = 8
N_MOD_PARTS = 6
ROUTER_COLS = 128
VMEM_LIMIT = 48 * 1024 * 1024

OFF_Q = 0
OFF_KV = D_ATTN
OFF_SGU = OFF_KV + 2 * D_KV
OFF_CONV = OFF_SGU + 2 * D_SGU

BF16 = jnp.bfloat16
F32 = jnp.float32
NEG_BIG = -1e30
Q_SCALE = HEAD_DIM ** -0.5 * math.log2(math.e)


def _params(n_grid_dims):
    return pltpu.CompilerParams(
        dimension_semantics=("arbitrary",) * n_grid_dims,
        vmem_limit_bytes=VMEM_LIMIT,
    )


def _mod_row_of_tile(i, tile=TM):
    n_ctx = T_CTX // tile
    return jnp.where(i < n_ctx, 0, 1 + (i - n_ctx) // (DEC_SEQ // tile))


def _mod_part(mod_ref, part, row):
    return mod_ref[part, pl.ds(row, 1), :]


def _rope_block_of_tile(i):
    return jnp.where(i < N_CTX_TILES, 0, 1 + (i - N_CTX_TILES) % LAT_TILES_PER_BATCH)


def _rms(x, gain):
    ms = jnp.mean(x * x, axis=-1, keepdims=True)
    return x * lax.rsqrt(ms + EPS) * gain


def _ctx_tile(i):
    return jnp.minimum(i, N_CTX_TILES - 1)


def _lat_tile(i):
    return jnp.maximum(i - N_CTX_TILES, 0)


MOD_TN = 512


def _mod_kernel(ct_ref, w_ref, b_ref, o_ref):
    c = ct_ref[...]
    s = c / (1.0 + jnp.exp(-c))
    w = w_ref[...]
    row = lax.broadcasted_iota(jnp.int32, (N_MOD_ROWS, MOD_TN), 0)
    out = jnp.zeros((N_MOD_ROWS, MOD_TN), F32)
    for r in range(1 + DEC_BATCH):
        acc = jnp.sum(w * s[:, r:r + 1], axis=0, keepdims=True)
        out = jnp.where(row == r, acc, out)
    o_ref[...] = out + b_ref[...]


def _modulation(cvecs_t, w_ada, b_ada):
    n_col = N_MOD_PARTS * D_MODEL
    per_part = D_MODEL // MOD_TN
    return pl.pallas_call(
        _mod_kernel,
        grid=(DEPTH, n_col // MOD_TN),
        in_specs=[
            pl.BlockSpec((D_MODEL, N_MOD_ROWS), lambda l, j: (0, 0)),
            pl.BlockSpec((None, D_MODEL, MOD_TN), lambda l, j: (l, 0, j)),
            pl.BlockSpec((None, 1, MOD_TN), lambda l, j: (l, 0, j)),
        ],
        out_specs=pl.BlockSpec((None, None, N_MOD_ROWS, MOD_TN),
                               lambda l, j: (l, j // per_part, 0, j % per_part)),
        out_shape=jax.ShapeDtypeStruct((DEPTH, N_MOD_PARTS, N_MOD_ROWS, D_MODEL), F32),
        compiler_params=_params(2),
        name="adaln_modulation",
    )(cvecs_t, w_ada, b_ada.reshape(DEPTH, 1, n_col))


_MOD_SPEC = lambda l, nd: pl.BlockSpec(
    (None, N_MOD_PARTS, N_MOD_ROWS, D_MODEL), lambda *idx: (l, 0, 0, 0))


def _swap16(x):
    lane = lax.broadcasted_iota(jnp.int32, x.shape, 1)
    up = pltpu.roll(x, 16, 1)
    down = pltpu.roll(x, LANES - 16, 1)
    return jnp.where((lane & 16) != 0, up, down)


def _rope(x, cos_t, sin_t):
    cols = []
    for j in range(x.shape[1] // LANES):
        xc = x[:, j * LANES:(j + 1) * LANES]
        cols.append(xc * cos_t + _swap16(xc) * sin_t)
    return cols[0] if len(cols) == 1 else jnp.concatenate(cols, axis=1)


def _proj_kernel(*refs, split_x):
    n_x = 2 if split_x else 1
    x_refs, refs, (scr_even, scr_odd) = refs[:n_x], refs[n_x:-2], refs[-2:]
    step = functools.partial(_proj_step, x_refs, refs, split_x)
    i = pl.program_id(0)

    @pl.when(i == 0)
    def _():
        scr_odd[...] = jnp.zeros((TM, D_IN), F32)

    @pl.when(i % 2 == 0)
    def _():
        step(scr_even, scr_odd)

    @pl.when(i % 2 == 1)
    def _():
        step(scr_odd, scr_even)


def _proj_step(x_refs, refs, split_x, new_ref, proj):
    (mod_ref, g1_ref, w_ref, qg_ref, kg_ref, cos_ref, sin_ref, bq_ref, sw_ref, sb_ref, on_ref,
     qt_ref, k_ref, vt_ref, k32_ref, v32_ref, sgu_ref, cbz_ref) = refs
    i = pl.program_id(0)
    cur = _stage1_tile(i)

    is_ctx = i <= N_CTX_TILES
    cos_t = cos_ref[...]
    sin_t = sin_ref[...]

    q = proj[:, OFF_Q:OFF_Q + D_ATTN]
    qms = jnp.dot((q * q).astype(BF16), bq_ref[...], preferred_element_type=F32)
    qn = q * lax.rsqrt(qms + EPS) * qg_ref[...]
    qr = _rope(qn, cos_t, sin_t) * Q_SCALE
    qt_ref[...] = qr.T.astype(BF16)

    k = proj[:, OFF_KV:OFF_KV + D_KV]
    v = proj[:, OFF_KV + D_KV:OFF_KV + 2 * D_KV]
    kms = jnp.dot((k * k).astype(BF16), bq_ref[:D_KV, :D_KV], preferred_element_type=F32)
    kn = k * lax.rsqrt(kms + EPS) * kg_ref[...]
    k_ref[...] = _rope(kn, cos_t, sin_t).astype(BF16)
    vt_ref[...] = v.T.astype(BF16)

    su = proj[:, OFF_SGU:OFF_SGU + D_SGU]
    sv = proj[:, OFF_SGU + D_SGU:OFF_SGU + 2 * D_SGU]
    n_chunks = TM // CHUNK
    sv_wide = jnp.concatenate([sv[n * CHUNK:(n + 1) * CHUNK, :] for n in range(n_chunks)], axis=1)
    grp = lax.broadcasted_iota(jnp.int32, (CHUNK, n_chunks * D_SGU), 1) // SGU_GROUP_DIM % N_SGU_GROUPS
    mixed_wide = jnp.zeros((CHUNK, n_chunks * D_SGU), F32)
    for pair in range(N_SGU_GROUPS // 2):
        rhs = jnp.concatenate(
            [jnp.where(grp == 2 * pair, sv_wide, 0.0), jnp.where(grp == 2 * pair + 1, sv_wide, 0.0)],
            axis=0).astype(BF16)
        mixed_wide = mixed_wide + jnp.dot(sw_ref[pair], rhs, preferred_element_type=F32)
    on_sgu = on_ref[:, D_ATTN:D_ATTN + D_SGU]
    for n in range(n_chunks):
        sgu = su[n * CHUNK:(n + 1) * CHUNK, :] * (mixed_wide[:, n * D_SGU:(n + 1) * D_SGU] + sb_ref[...])
        sgu_ref[n * CHUNK:(n + 1) * CHUNK, :] = _rms(sgu, on_sgu).astype(BF16)

    cbz_ref[:, :D_CONV] = proj[:, OFF_CONV:OFF_CONV + D_CONV]
    cbz_ref[:, D_CONV:] = (proj[:, OFF_CONV + D_CONV:OFF_CONV + 2 * D_CONV]
                           * proj[:, OFF_CONV + 2 * D_CONV:OFF_CONV + 3 * D_CONV])

    x = jnp.where(cur < N_CTX_TILES, x_refs[0][...], x_refs[1][...]) if split_x else x_refs[0][...]
    mrow = _mod_row_of_tile(cur)
    h = _rms(x, g1_ref[...] * (1.0 + _mod_part(mod_ref, 1, mrow))) + _mod_part(mod_ref, 0, mrow)
    new_ref[...] = jnp.dot(h.astype(BF16), w_ref[...], preferred_element_type=F32)

    @pl.when(is_ctx)
    def _():
        k32_ref[...] = kn
        v32_ref[...] = v


def _x_specs(xs, tile_of_step=lambda i: i):
    if len(xs) == 1:
        return [pl.BlockSpec((TM, D_MODEL), lambda i: (tile_of_step(i), 0))]
    return [pl.BlockSpec((TM, D_MODEL), lambda i: (_ctx_tile(tile_of_step(i)), 0)),
            pl.BlockSpec((TM, D_MODEL), lambda i: (_lat_tile(tile_of_step(i)), 0))]


def _stage1_tile(i):
    return jnp.minimum(i, N_TILES - 1)


def _stage2_tile(i):
    return jnp.maximum(i - 1, 0)


def _proj(l, xs, mod, g1, w_in, qg, kg, cos_t, sin_t, bq, sgu_w, sgu_b, on):
    lay = lambda *rest: (lambda i: (l,) + rest)
    s2 = _stage2_tile
    return pl.pallas_call(
        functools.partial(_proj_kernel, split_x=len(xs) == 2),
        grid=(N_TILES + 1,),
        in_specs=_x_specs(xs, _stage1_tile) + [
            _MOD_SPEC(l, 1),
            pl.BlockSpec((None, 1, D_MODEL), lay(0, 0)),
            pl.BlockSpec((None, D_MODEL, D_IN), lay(0, 0)),
            pl.BlockSpec((None, 1, D_ATTN), lay(0, 0)),
            pl.BlockSpec((None, 1, D_KV), lay(0, 0)),
            pl.BlockSpec((TM, LANES), lambda i: (_rope_block_of_tile(s2(i)), 0)),
            pl.BlockSpec((TM, LANES), lambda i: (_rope_block_of_tile(s2(i)), 0)),
            pl.BlockSpec((D_ATTN, D_ATTN), lambda i: (0, 0)),
            pl.BlockSpec((None, N_SGU_GROUPS // 2, CHUNK, 2 * CHUNK), lay(0, 0, 0)),
            pl.BlockSpec((None, CHUNK, D_SGU), lay(0, 0)),
            pl.BlockSpec((None, 1, D_MIX), lay(0, 0)),
        ],
        out_specs=[
            pl.BlockSpec((None, D_ATTN, TM), lambda i: (s2(i), 0, 0)),
            pl.BlockSpec((TM, D_KV), lambda i: (s2(i), 0)),
            pl.BlockSpec((None, D_KV, TM), lambda i: (s2(i), 0, 0)),
            pl.BlockSpec((TM, D_KV), lambda i: (_ctx_tile(s2(i)), 0)),
            pl.BlockSpec((TM, D_KV), lambda i: (_ctx_tile(s2(i)), 0)),
            pl.BlockSpec((TM, D_SGU), lambda i: (s2(i), 0)),
            pl.BlockSpec((TM, 2 * D_CONV), lambda i: (s2(i), 0)),
        ],
        out_shape=[
            jax.ShapeDtypeStruct((N_TILES, D_ATTN, TM), BF16),
            jax.ShapeDtypeStruct((T_ALL, D_KV), BF16),
            jax.ShapeDtypeStruct((N_TILES, D_KV, TM), BF16),
            jax.ShapeDtypeStruct((T_CTX, D_KV), F32),
            jax.ShapeDtypeStruct((T_CTX, D_KV), F32),
            jax.ShapeDtypeStruct((T_ALL, D_SGU), BF16),
            jax.ShapeDtypeStruct((T_ALL, 2 * D_CONV), F32),
        ],
        scratch_shapes=[pltpu.VMEM((TM, D_IN), F32), pltpu.VMEM((TM, D_IN), F32)],
        compiler_params=_params(1),
        name=f"proj_l{l}",
    )(*xs, mod, g1, w_in, qg, kg, cos_t, sin_t, bq, sgu_w, sgu_b, on)


ACC_ROWS = HEAD_DIM + BF16_ROWS
N_PAIRS = N_HEADS // 2
PAIRS_PER_KV = N_PAIRS // N_KV_HEADS


def _pair_queries(qt, kv_idx, tq):
    qf = qt.astype(F32)
    mine = lax.broadcasted_iota(jnp.int32, (2 * HEAD_DIM, tq), 0) // HEAD_DIM == kv_idx
    qe = qf[:HEAD_DIM]
    qo = qf[HEAD_DIM:]
    rhs = jnp.concatenate(
        [jnp.where(mine, jnp.concatenate([qe, qe], axis=0), 0.0),
         jnp.where(mine, jnp.concatenate([qo, qo], axis=0), 0.0)], axis=1)
    return rhs.astype(BF16)


def _with_ones(vt_c):
    return jnp.concatenate([vt_c, jnp.ones((BF16_ROWS, vt_c.shape[1]), BF16)], axis=0)


def _flash_pairs(rhs_list, chunk_lists, tq):
    n_pairs, n_chunks = len(rhs_list), len(chunk_lists[0])
    scores = lambda s, c: jnp.dot(chunk_lists[s][c][0](), rhs_list[s], preferred_element_type=F32)
    m = [jnp.full((1, 2 * tq), NEG_BIG, F32)] * n_pairs
    acc = [jnp.zeros((ACC_ROWS, 2 * tq), F32)] * n_pairs
    st = [scores(s, 0) for s in range(n_pairs)]
    for c in range(n_chunks):
        for s in range(n_pairs):
            m_new = jnp.maximum(m[s], jnp.max(st[s], axis=0, keepdims=True))
            st_next = scores(s, c + 1) if c + 1 < n_chunks else None
            alpha = jnp.exp2(m[s] - m_new)
            p = jnp.exp2(st[s] - m_new).astype(BF16)
            acc[s] = alpha * acc[s] + jnp.dot(chunk_lists[s][c][1](), p, preferred_element_type=F32)
            m[s] = m_new
            st[s] = st_next
    outs = []
    for s in range(n_pairs):
        o = acc[s][:HEAD_DIM] * (1.0 / acc[s][HEAD_DIM:HEAD_DIM + 1])
        outs.append(jnp.concatenate([o[:, :tq], o[:, tq:]], axis=0))
    return outs


def _attn_ctx_kernel(qt_ref, k_ref, vt_ref, o_ref):
    tq = SEQ
    rhs_list, chunk_lists = [], []
    for pair in range(N_PAIRS):
        kv_idx = pair // PAIRS_PER_KV
        rhs_list.append(_pair_queries(qt_ref[pair * LANES:(pair + 1) * LANES, :], kv_idx, tq))
        chunk_lists.append([(lambda: k_ref[...],
                             lambda kv_idx=kv_idx: _with_ones(vt_ref[kv_idx * HEAD_DIM:(kv_idx + 1) * HEAD_DIM, :]))])
    for pair, o_t in enumerate(_flash_pairs(rhs_list, chunk_lists, tq)):
        o_ref[:, pair * LANES:(pair + 1) * LANES] = o_t.T.astype(BF16)


def _attn_ctx(l, qt, k, vt):
    halves = TM // SEQ
    return pl.pallas_call(
        _attn_ctx_kernel,
        grid=(BATCH,),
        in_specs=[
            pl.BlockSpec((None, D_ATTN, SEQ), lambda b: (b // halves, 0, b % halves)),
            pl.BlockSpec((SEQ, D_KV), lambda b: (b, 0)),
            pl.BlockSpec((None, D_KV, SEQ), lambda b: (b // halves, 0, b % halves)),
        ],
        out_specs=pl.BlockSpec((SEQ, D_ATTN), lambda b: (b, 0)),
        out_shape=jax.ShapeDtypeStruct((T_CTX, D_ATTN), BF16),
        compiler_params=_params(1),
        name=f"attn_ctx_l{l}",
    )(qt, k, vt)


KC = 512
LAT_PAIRS_PER_STEP = 2


def _attn_lat_kernel(qt_ref, k_ref, vt_ref, ck_ref, cv_ref, o_ref):
    tq = TM
    cvt = cv_ref[...].T
    rhs_list, chunk_lists = [], []
    for j in range(LAT_PAIRS_PER_STEP):
        kv_idx = (pl.program_id(2) * LAT_PAIRS_PER_STEP + j) // PAIRS_PER_KV
        rhs_list.append(_pair_queries(qt_ref[j * LANES:(j + 1) * LANES, :], kv_idx, tq))
        v_row = pl.multiple_of(kv_idx * HEAD_DIM, HEAD_DIM)

        def latent_chunk(c, v_row=v_row):
            tile, lane0 = (c * KC) // TM, (c * KC) % TM
            return (lambda: k_ref[c * KC:(c + 1) * KC, :],
                    lambda: _with_ones(vt_ref[tile, pl.ds(v_row, HEAD_DIM), lane0:lane0 + KC]))

        def cached_chunk(c, kv_idx=kv_idx):
            return (lambda: ck_ref[c * KC:(c + 1) * KC, :].astype(BF16),
                    lambda: _with_ones(jnp.where(kv_idx == 0, cvt[:HEAD_DIM, c * KC:(c + 1) * KC],
                                                 cvt[HEAD_DIM:, c * KC:(c + 1) * KC]).astype(BF16)))

        chunks = [latent_chunk(c) for c in range(DEC_SEQ // KC)]
        chunks += [cached_chunk(c) for c in range(PAST_LEN // KC)]
        chunk_lists.append(chunks)
    for j, o_t in enumerate(_flash_pairs(rhs_list, chunk_lists, tq)):
        o_ref[:, j * LANES:(j + 1) * LANES] = o_t.T.astype(BF16)


def _attn_lat(l, qt, k, vt, cache_k, cache_v):
    n_q = DEC_SEQ // TM
    first = N_CTX_TILES
    ctx_blocks = T_CTX // DEC_SEQ
    width = LAT_PAIRS_PER_STEP * LANES
    return pl.pallas_call(
        _attn_lat_kernel,
        grid=(DEC_BATCH, n_q, N_PAIRS // LAT_PAIRS_PER_STEP),
        in_specs=[
            pl.BlockSpec((None, width, TM), lambda b, i, p: (first + b * n_q + i, p, 0)),
            pl.BlockSpec((DEC_SEQ, D_KV), lambda b, i, p: (ctx_blocks + b, 0)),
            pl.BlockSpec((n_q, D_KV, TM), lambda b, i, p: (ctx_blocks + b, 0, 0)),
            pl.BlockSpec((None, None, PAST_LEN, D_KV), lambda b, i, p: (b, l, 0, 0)),
            pl.BlockSpec((None, None, PAST_LEN, D_KV), lambda b, i, p: (b, l, 0, 0)),
        ],
        out_specs=pl.BlockSpec((TM, width), lambda b, i, p: (b * n_q + i, p)),
        out_shape=jax.ShapeDtypeStruct((T_LAT, D_ATTN), BF16),
        compiler_params=_params(3),
        name=f"attn_lat_l{l}",
    )(qt, k, vt, cache_k, cache_v)


def _first_max_index(vals, valid, rowf):
    masked = jnp.where(valid, vals, -jnp.inf)
    mx = jnp.max(masked, axis=0, keepdims=True)
    idx = jnp.min(jnp.where(masked == mx, rowf, float(SUBLANES)), axis=0, keepdims=True)
    return mx, idx


def _merge_kernel(*refs, split_x):
    n_x = 2 if split_x else 1
    x_refs, refs, (scr_even, scr_odd) = refs[:n_x], refs[n_x:-2], refs[-2:]
    step = functools.partial(_merge_step, x_refs, refs, split_x)
    i = pl.program_id(0)

    @pl.when(i == 0)
    def _():
        scr_odd[...] = jnp.zeros((TM, D_MODEL), F32)

    @pl.when(i % 2 == 0)
    def _():
        step(scr_even, scr_odd)

    @pl.when(i % 2 == 1)
    def _():
        step(scr_odd, scr_even)


def _merge_step(x_refs, refs, split_x, new_x1_ref, prev_x1_ref):
    (actx_ref, alat_ref, sgu_ref, cbz_ref, prev_ref, next_ref, cw_ref, on_ref,
     wo_ref, mod_ref, g2_ref, wr_ref, wrhi_ref, rb_ref, x1_ref, h2_ref, route_ref, gate4_ref, cnt_ref) = refs
    i = pl.program_id(0)
    cur = _stage1_tile(i)
    is_ctx = cur < N_CTX_TILES
    mrow = _mod_row_of_tile(cur)
    on = on_ref[...]

    cbz = cbz_ref[...]
    cb = cbz[:, :D_CONV]
    z = cbz[:, D_CONV:]
    seq_mask = jnp.where(is_ctx, SEQ - 1, DEC_SEQ - 1)
    row = lax.broadcasted_iota(jnp.int32, (TM, D_CONV), 0)
    pos = (cur * TM + row) & seq_mask
    z_prev = jnp.where(row == 0, prev_ref[SUBLANES - 1:SUBLANES, D_CONV:], pltpu.roll(z, 1, 0))
    z_next = jnp.where(row == TM - 1, next_ref[0:1, D_CONV:], pltpu.roll(z, TM - 1, 0))
    z_prev = jnp.where(pos == 0, 0.0, z_prev)
    z_next = jnp.where(pos == seq_mask, 0.0, z_next)
    cw = cw_ref[...]
    conv = cb * (z_prev * cw[0:1, :] + z * cw[1:2, :] + z_next * cw[2:3, :])
    conv_n = _rms(conv, on[:, D_ATTN + D_SGU:]).astype(BF16)

    attn = jnp.where(is_ctx, actx_ref[...].astype(F32), alat_ref[...].astype(F32))
    attn_n = _rms(attn, on[:, :D_ATTN]).astype(BF16)
    merged = jnp.concatenate([attn_n, sgu_ref[...], conv_n], axis=1)
    x = jnp.where(is_ctx, x_refs[0][...], x_refs[1][...]) if split_x else x_refs[0][...]
    x1 = x + _mod_part(mod_ref, 2, mrow) * jnp.dot(merged, wo_ref[...], preferred_element_type=F32)
    x1_ref[...] = x1
    new_x1_ref[...] = x1
    _route_tile(prev_x1_ref[...], _mod_row_of_tile(_stage2_tile(i)),
                mod_ref, g2_ref, wr_ref, wrhi_ref, rb_ref, h2_ref, route_ref, gate4_ref, cnt_ref)


def _route_tile(x1, mrow, mod_ref, g2_ref, wr_ref, wrhi_ref, rb_ref, h2_ref, route_ref, gate4_ref, cnt_ref):
    h2 = _rms(x1, g2_ref[...] * (1.0 + _mod_part(mod_ref, 4, mrow))) + _mod_part(mod_ref, 3, mrow)
    hb = h2.astype(BF16)
    h2_ref[...] = hb

    h_lo = (h2 - hb.astype(F32)).astype(BF16)
    a = jnp.dot(hb, wr_ref[...], preferred_element_type=F32)
    b = jnp.dot(h_lo, wrhi_ref[...], preferred_element_type=F32)
    logits = a[:, :ROUTER_COLS] + a[:, ROUTER_COLS:] + b + rb_ref[...]
    lt = logits.T

    rowi = lax.broadcasted_iota(jnp.int32, (SUBLANES, TM), 0).astype(F32)
    valid = rowi < float(N_EXPERT_GROUPS)
    g_log = lt[0:SUBLANES]
    g_max, g_idx = _first_max_index(g_log, valid, rowi)
    p_g = 1.0 / jnp.sum(jnp.where(valid, jnp.exp(g_log - g_max), 0.0), axis=0, keepdims=True)
    e_log = jnp.zeros((SUBLANES, TM), F32)
    for g in range(N_EXPERT_GROUPS):
        e_log = jnp.where(g_idx == g, lt[(g + 1) * SUBLANES:(g + 2) * SUBLANES], e_log)
    e_max, i1 = _first_max_index(e_log, valid, rowi)
    e_exp = jnp.where(valid, jnp.exp(e_log - e_max), 0.0)
    e_prob = e_exp / jnp.sum(e_exp, axis=0, keepdims=True)
    v1 = jnp.max(e_prob, axis=0, keepdims=True)
    rest = jnp.logical_and(valid, rowi != i1)
    v2, i2 = _first_max_index(e_prob, rest, rowi)
    denom = v1 + v2
    w_sel = p_g * jnp.where(rowi == i1, v1 / denom, jnp.where(rowi == i2, v2 / denom, 0.0))
    route_ref[...] = jnp.broadcast_to(g_idx, (SUBLANES, TM))
    pad = jnp.zeros((LANES - SUBLANES, TM), F32)
    gate4_ref[...] = jnp.concatenate([w_sel, pad], axis=0).T
    counts = jnp.sum(jnp.where(rowi == g_idx, 1.0, 0.0), axis=1, keepdims=True)
    cnt_ref[...] = jnp.broadcast_to(counts, (SUBLANES, LANES)).astype(jnp.int32)


def _merge(l, xs, attn_ctx, attn_lat, sgu_n, cbz, conv_w, on, w_out, mod, g2, wr_cat, wr_hi, rb):
    lay = lambda *rest: (lambda i: (l,) + rest)
    rows8 = TM // SUBLANES
    s1, s2 = _stage1_tile, _stage2_tile
    return pl.pallas_call(
        functools.partial(_merge_kernel, split_x=len(xs) == 2),
        grid=(N_TILES + 1,),
        in_specs=_x_specs(xs, s1) + [
            pl.BlockSpec((TM, D_ATTN), lambda i: (_ctx_tile(s1(i)), 0)),
            pl.BlockSpec((TM, D_ATTN), lambda i: (_lat_tile(s1(i)), 0)),
            pl.BlockSpec((TM, D_SGU), lambda i: (s1(i), 0)),
            pl.BlockSpec((TM, 2 * D_CONV), lambda i: (s1(i), 0)),
            pl.BlockSpec((SUBLANES, 2 * D_CONV), lambda i: (jnp.maximum(s1(i) * rows8 - 1, 0), 0)),
            pl.BlockSpec((SUBLANES, 2 * D_CONV),
                         lambda i: (jnp.minimum((s1(i) + 1) * rows8, T_ALL // SUBLANES - 1), 0)),
            pl.BlockSpec((None, 3, D_CONV), lay(0, 0)),
            pl.BlockSpec((None, 1, D_MIX), lay(0, 0)),
            pl.BlockSpec((None, D_MIX, D_MODEL), lay(0, 0)),
            _MOD_SPEC(l, 1),
            pl.BlockSpec((None, 1, D_MODEL), lay(0, 0)),
            pl.BlockSpec((None, D_MODEL, 2 * ROUTER_COLS), lay(0, 0)),
            pl.BlockSpec((None, D_MODEL, ROUTER_COLS), lay(0, 0)),
            pl.BlockSpec((None, 1, ROUTER_COLS), lay(0, 0)),
        ],
        out_specs=[
            pl.BlockSpec((TM, D_MODEL), lambda i: (s1(i), 0)),
            pl.BlockSpec((TM, D_MODEL), lambda i: (s2(i), 0)),
            pl.BlockSpec((None, SUBLANES, TM), lambda i: (s2(i), 0, 0)),
            pl.BlockSpec((TM, LANES), lambda i: (s2(i), 0)),
            pl.BlockSpec((None, SUBLANES, LANES), lambda i: (s2(i), 0, 0)),
        ],
        out_shape=[
            jax.ShapeDtypeStruct((T_ALL, D_MODEL), F32),
            jax.ShapeDtypeStruct((T_ALL, D_MODEL), BF16),
            jax.ShapeDtypeStruct((N_TILES, SUBLANES, TM), F32),
            jax.ShapeDtypeStruct((T_ALL, LANES), F32),
            jax.ShapeDtypeStruct((N_TILES, SUBLANES, LANES), jnp.int32),
        ],
        scratch_shapes=[pltpu.VMEM((TM, D_MODEL), F32), pltpu.VMEM((TM, D_MODEL), F32)],
        compiler_params=_params(1),
        name=f"merge_l{l}",
    )(*xs, attn_ctx, attn_lat, sgu_n, cbz, cbz, cbz, conv_w, on, w_out, mod, g2, wr_cat, wr_hi, rb)


RB = 144
SORT_ROWS = TM + LANES
SORT_ROWS_PAD = -(-(SORT_ROWS + RB) // BF16_ROWS) * BF16_ROWS
MOE_VMEM_LIMIT = 56 * 1024 * 1024


def _split3(x):
    hi = x.astype(BF16)
    r1 = x - hi.astype(F32)
    mid = r1.astype(BF16)
    lo = (r1 - mid.astype(F32)).astype(BF16)
    return hi, mid, lo


def _moe_kernel(cnt_ref, h_ref, route_ref, gate4_ref, tri_ref, w1_ref, w3_ref, w2_ref, x1_ref, mod_ref, fn_ref,
                *refs, final):
    out_refs, (xs_ref, gs_ref, zs_ref) = refs[:-3], refs[-3:]
    i = pl.program_id(0)
    gate2 = _mod_part(mod_ref, 5, _mod_row_of_tile(i))
    counts = [cnt_ref[i, g] for g in range(N_EXPERT_GROUPS)]
    starts = [jnp.int32(0)]
    for g in range(N_EXPERT_GROUPS - 1):
        starts.append(starts[-1] + (counts[g] + (BF16_ROWS - 1)) // BF16_ROWS * BF16_ROWS)

    g_idx = route_ref[0:1, :]
    rowi = lax.broadcasted_iota(jnp.int32, (SUBLANES, TM), 0).astype(F32)
    onehot = jnp.where(rowi == g_idx, 1.0, 0.0)
    incl = jnp.dot(onehot.astype(BF16), tri_ref[...], preferred_element_type=F32)
    pos = jnp.sum(onehot * incl, axis=0, keepdims=True) - 1.0
    for g in range(1, N_EXPERT_GROUPS):
        pos = pos + jnp.where(g_idx == float(g), starts[g].astype(F32), 0.0)
    sub = lax.broadcasted_iota(jnp.int32, (SORT_ROWS, TM), 0).astype(F32)
    perm = jnp.where(sub == pos, 1.0, 0.0).astype(BF16)
    pos_col = jnp.broadcast_to(pos, (LANES, TM)).T
    lane = lax.broadcasted_iota(jnp.int32, (TM, SORT_ROWS), 1).astype(F32)
    perm_t = jnp.where(lane == jnp.concatenate([pos_col] * (SORT_ROWS // LANES), axis=1), 1.0, 0.0).astype(BF16)

    xs_ref[:SORT_ROWS, :] = jnp.dot(perm, h_ref[...], preferred_element_type=F32).astype(BF16)
    xs_ref[SORT_ROWS:, :] = jnp.zeros((SORT_ROWS_PAD - SORT_ROWS, D_MODEL), BF16)
    hi, mid, lo = _split3(gate4_ref[...])
    packed = (hi.astype(F32) + pltpu.roll(mid.astype(F32), EXPERTS_PER_GROUP, 1)
              + pltpu.roll(lo.astype(F32), 2 * EXPERTS_PER_GROUP, 1)).astype(BF16)
    gsp = jnp.dot(perm, packed, preferred_element_type=F32)
    gs_ref[:SORT_ROWS, :] = (gsp + pltpu.roll(gsp, LANES - EXPERTS_PER_GROUP, 1)
                             + pltpu.roll(gsp, LANES - 2 * EXPERTS_PER_GROUP, 1))
    gs_ref[SORT_ROWS:, :] = jnp.zeros((SORT_ROWS_PAD - SORT_ROWS, LANES), F32)
    zs_ref[...] = jnp.zeros((SORT_ROWS_PAD, D_MODEL), BF16)

    for g in range(N_EXPERT_GROUPS):
        def block_ffn(blk, carry, g=g):
            row0 = pl.multiple_of(starts[g] + blk * RB, BF16_ROWS)
            xb = xs_ref[pl.ds(row0, RB), :]
            gb = gs_ref[pl.ds(row0, RB), :]
            cols = []
            for e in range(EXPERTS_PER_GROUP):
                a = jnp.dot(xb, w1_ref[g * EXPERTS_PER_GROUP + e], preferred_element_type=F32)
                b = jnp.dot(xb, w3_ref[g * EXPERTS_PER_GROUP + e], preferred_element_type=F32)
                cols.append((a / (1.0 + jnp.exp(-a)) * b * gb[:, e:e + 1]).astype(BF16))
            z = jnp.dot(jnp.concatenate(cols, axis=1), w2_ref[g], preferred_element_type=F32)
            zs_ref[pl.ds(row0, RB), :] = z.astype(BF16)
            return carry

        lax.fori_loop(0, (counts[g] + (RB - 1)) // RB, block_ffn, 0)

    y = jnp.dot(perm_t, zs_ref[:SORT_ROWS, :], preferred_element_type=F32)
    out = x1_ref[...] + gate2 * y

    if final:
        octx_ref, olat_ref = out_refs
        out = _rms(out, fn_ref[...])

        @pl.when(i < N_CTX_TILES)
        def _():
            octx_ref[...] = out

        @pl.when(i >= N_CTX_TILES)
        def _():
            olat_ref[...] = out
    else:
        out_refs[0][...] = out


def _moe(l, counts, h2, route, gate4, tri, w1, w3, w2, x1, mod, final_norm, final):
    tile = pl.BlockSpec((TM, D_MODEL), lambda i, c: (i, 0))
    resident = pl.Buffered(1)
    if final:
        out_specs = [pl.BlockSpec((TM, D_MODEL), lambda i, c: (_ctx_tile(i), 0)),
                     pl.BlockSpec((TM, D_MODEL), lambda i, c: (_lat_tile(i), 0))]
        out_shape = [jax.ShapeDtypeStruct((T_CTX, D_MODEL), F32), jax.ShapeDtypeStruct((T_LAT, D_MODEL), F32)]
    else:
        out_specs = tile
        out_shape = jax.ShapeDtypeStruct((T_ALL, D_MODEL), F32)
    grid_spec = pltpu.PrefetchScalarGridSpec(
        num_scalar_prefetch=1,
        grid=(N_TILES,),
        in_specs=[
            tile,
            pl.BlockSpec((None, SUBLANES, TM), lambda i, c: (i, 0, 0)),
            pl.BlockSpec((TM, LANES), lambda i, c: (i, 0)),
            pl.BlockSpec((TM, TM), lambda i, c: (0, 0), pipeline_mode=resident),
            pl.BlockSpec((None, N_EXPERTS, D_MODEL, D_EXPERT), lambda i, c: (l, 0, 0, 0), pipeline_mode=resident),
            pl.BlockSpec((None, N_EXPERTS, D_MODEL, D_EXPERT), lambda i, c: (l, 0, 0, 0), pipeline_mode=resident),
            pl.BlockSpec((None, N_EXPERT_GROUPS, D_GROUP_HID, D_MODEL), lambda i, c: (l, 0, 0, 0),
                         pipeline_mode=resident),
            tile,
            pl.BlockSpec((None, N_MOD_PARTS, N_MOD_ROWS, D_MODEL), lambda i, c: (l, 0, 0, 0)),
            pl.BlockSpec((1, D_MODEL), lambda i, c: (0, 0)),
        ],
        out_specs=out_specs,
        scratch_shapes=[pltpu.VMEM((SORT_ROWS_PAD, D_MODEL), BF16),
                        pltpu.VMEM((SORT_ROWS_PAD, LANES), F32),
                        pltpu.VMEM((SORT_ROWS_PAD, D_MODEL), BF16)],
    )
    return pl.pallas_call(
        functools.partial(_moe_kernel, final=final),
        grid_spec=grid_spec,
        out_shape=out_shape,
        compiler_params=pltpu.CompilerParams(dimension_semantics=("arbitrary",),
                                             vmem_limit_bytes=MOE_VMEM_LIMIT),
        name=f"moe_l{l}",
    )(counts, h2, route, gate4, tri, w1, w3, w2, x1, mod, final_norm)


def _rope_tables():
    rows = DEC_SEQ // GRID_W
    row = np.repeat(np.arange(rows, dtype=np.float32), GRID_W)
    col = np.tile(np.arange(GRID_W, dtype=np.float32), rows)
    half = ROPE_AXIS_DIM // 2
    inv_freq = (1.0 / (np.float32(ROPE_THETA) ** (np.arange(half, dtype=np.float32) * np.float32(2.0)
                                                  / np.float32(ROPE_AXIS_DIM)))).astype(np.float32)
    ar = row[:, None] * inv_freq
    ac = col[:, None] * inv_freq
    cos64 = np.concatenate([np.cos(ar), np.cos(ar), np.cos(ac), np.cos(ac)], axis=1)
    sin64 = np.concatenate([-np.sin(ar), np.sin(ar), -np.sin(ac), np.sin(ac)], axis=1)
    cos_t = np.concatenate([np.ones((TM, LANES), np.float32), np.tile(cos64, (1, 2))], axis=0)
    sin_t = np.concatenate([np.zeros((TM, LANES), np.float32), np.tile(sin64, (1, 2))], axis=0)
    return jnp.asarray(cos_t, F32), jnp.asarray(sin_t, F32)


def _router_weights(router_g_w, router_g_b, router_e_w, router_e_b):
    zw = jnp.zeros((DEPTH, D_MODEL, SUBLANES - N_EXPERT_GROUPS), F32)
    zb = jnp.zeros((DEPTH, SUBLANES - N_EXPERT_GROUPS), F32)
    w_cols, b_cols = [router_g_w, zw], [router_g_b, zb]
    for g in range(N_EXPERT_GROUPS):
        w_cols += [router_e_w[:, g], zw]
        b_cols += [router_e_b[:, g], zb]
    used = (1 + N_EXPERT_GROUPS) * SUBLANES
    w_cols.append(jnp.zeros((DEPTH, D_MODEL, ROUTER_COLS - used), F32))
    b_cols.append(jnp.zeros((DEPTH, ROUTER_COLS - used), F32))
    w = jnp.concatenate(w_cols, axis=2)
    b = jnp.concatenate(b_cols, axis=1).reshape(DEPTH, 1, ROUTER_COLS)
    w_hi = w.astype(BF16)
    w_lo = (w - w_hi.astype(F32)).astype(BF16)
    return jnp.concatenate([w_hi, w_lo], axis=2), w_hi, b


def kernel(x_prompt, x_sample, cache_k, cache_v, c, c_ctx, w_ada, b_ada, norm1, w_in, q_norm, k_norm,
           sgu_w, sgu_b, conv_w, out_norm, w_out, norm2, router_g_w, router_g_b, router_e_w,
           router_e_b, moe_w1, moe_w3, moe_w2, final_norm):
    xs = (x_prompt.reshape(T_CTX, D_MODEL), x_sample.reshape(T_LAT, D_MODEL))

    cvecs = jnp.concatenate([c_ctx[None, :], c, jnp.zeros((N_MOD_ROWS - 1 - DEC_BATCH, D_MODEL), F32)], axis=0)
    mod = _modulation(cvecs.T, w_ada, b_ada)

    w_in_b = w_in.astype(BF16)
    w_out_b = w_out.astype(BF16)
    g1 = norm1.reshape(DEPTH, 1, D_MODEL)
    g2 = norm2.reshape(DEPTH, 1, D_MODEL)
    on = out_norm.reshape(DEPTH, 1, D_MIX)
    qg = jnp.tile(q_norm, (1, N_HEADS)).reshape(DEPTH, 1, D_ATTN)
    kg = jnp.tile(k_norm, (1, N_KV_HEADS)).reshape(DEPTH, 1, D_KV)
    cos_t, sin_t = _rope_tables()
    bq = jnp.asarray(np.kron(np.eye(N_HEADS), np.full((HEAD_DIM, HEAD_DIM), 1.0 / HEAD_DIM)), BF16)
    sgu_w_pairs = sgu_w.astype(BF16).reshape(DEPTH, N_SGU_GROUPS // 2, 2, CHUNK, CHUNK)
    sgu_w_pairs = jnp.concatenate([sgu_w_pairs[:, :, 0], sgu_w_pairs[:, :, 1]], axis=-1)
    sgu_bias = jnp.repeat(jnp.swapaxes(sgu_b, 1, 2), SGU_GROUP_DIM, axis=2)
    wr_cat, wr_hi, rb = _router_weights(router_g_w, router_g_b, router_e_w, router_e_b)
    w1 = moe_w1.astype(BF16)
    w3 = moe_w3.astype(BF16)
    w2 = moe_w2.astype(BF16).reshape(DEPTH, N_EXPERT_GROUPS, D_GROUP_HID, D_MODEL)
    fn = final_norm.reshape(1, D_MODEL)
    tri = jnp.asarray(np.triu(np.ones((TM, TM), np.float32)), BF16)
    ck = cache_k.reshape(DEC_BATCH, DEPTH, PAST_LEN, D_KV)
    cv = cache_v.reshape(DEC_BATCH, DEPTH, PAST_LEN, D_KV)

    ctx_ks, ctx_vs = [], []
    for l in range(DEPTH):
        qt, k, vt, k32, v32, sgu_n, cbz = _proj(l, xs, mod, g1, w_in_b, qg, kg, cos_t, sin_t, bq,
                                                sgu_w_pairs, sgu_bias, on)
        attn_ctx = _attn_ctx(l, qt, k, vt)
        attn_lat = _attn_lat(l, qt, k, vt, ck, cv)
        x1, h2, route, gate4, cnt = _merge(l, xs, attn_ctx, attn_lat, sgu_n, cbz, conv_w, on, w_out_b, mod, g2,
                                           wr_cat, wr_hi, rb)
        counts = cnt[:, :N_EXPERT_GROUPS, 0]
        out = _moe(l, counts, h2, route, gate4, tri, w1, w3, w2, x1, mod, fn, final=(l == DEPTH - 1))
        xs = (out,)
        ctx_ks.append(k32.reshape(BATCH, SEQ, N_KV_HEADS, HEAD_DIM))
        ctx_vs.append(v32.reshape(BATCH, SEQ, N_KV_HEADS, HEAD_DIM))

    y_prompt, y_sample = out
    return (y_prompt.reshape(BATCH, SEQ, D_MODEL), y_sample.reshape(DEC_BATCH, DEC_SEQ, D_MODEL),
            jnp.stack(ctx_ks, axis=1), jnp.stack(ctx_vs, axis=1))
```

```python
import functools
import math

import jax
import jax.numpy as jnp
import numpy as np
from jax import lax
from jax.experimental import pallas as pl
from jax.experimental.pallas import tpu as pltpu

D_MODEL = 1024
BATCH = 16
SEQ = 256
DEPTH = 2
DEC_BATCH = 2
DEC_SEQ = 4096
PAST_LEN = 512
GRID_W = 64
N_HEADS = 8
N_KV_HEADS = 2
HEAD_DIM = 64
ROPE_AXIS_DIM = HEAD_DIM // 2
ROPE_THETA = 10000.0
D_ATTN = N_HEADS * HEAD_DIM
D_KV = N_KV_HEADS * HEAD_DIM
CHUNK = 128
N_SGU_GROUPS = 4
SGU_GROUP_DIM = 64
D_SGU = N_SGU_GROUPS * SGU_GROUP_DIM
D_CONV = 256
D_MIX = D_ATTN + D_SGU + D_CONV
D_IN = D_ATTN + 2 * D_KV + 2 * D_SGU + 3 * D_CONV
N_EXPERT_GROUPS = 4
EXPERTS_PER_GROUP = 4
N_EXPERTS = N_EXPERT_GROUPS * EXPERTS_PER_GROUP
D_EXPERT = 256
D_GROUP_HID = EXPERTS_PER_GROUP * D_EXPERT
EPS = 1e-6

T_CTX = BATCH * SEQ
T_LAT = DEC_BATCH * DEC_SEQ
T_ALL = T_CTX + T_LAT
TM = 512
N_TILES = T_ALL // TM
N_CTX_TILES = T_CTX // TM
LAT_TILES_PER_BATCH = DEC_SEQ // TM
TM_MOE = 1024
LANES = 128
SUBLANES = 8
BF16_ROWS = 16
N_MOD_ROWS = 8
N_MOD_PARTS = 6
ROUTER_COLS = 128
VMEM_LIMIT = 48 * 1024 * 1024

OFF_Q = 0
OFF_KV = D_ATTN
OFF_SGU = OFF_KV + 2 * D_KV
OFF_CONV = OFF_SGU + 2 * D_SGU

BF16 = jnp.bfloat16
F32 = jnp.float32
NEG_BIG = -1e30
Q_SCALE = HEAD_DIM ** -0.5 * math.log2(math.e)


def _params(n_grid_dims):
    return pltpu.CompilerParams(
        dimension_semantics=("arbitrary",) * n_grid_dims,
        vmem_limit_bytes=VMEM_LIMIT,
    )


def _mod_row_of_tile(i, tile=TM):
    n_ctx = T_CTX // tile
    return jnp.where(i < n_ctx, 0, 1 + (i - n_ctx) // (DEC_SEQ // tile))


def _mod_part(mod_ref, part, row):
    return mod_ref[part, pl.ds(row, 1), :]


def _rope_block_of_tile(i):
    return jnp.where(i < N_CTX_TILES, 0, 1 + (i - N_CTX_TILES) % LAT_TILES_PER_BATCH)


def _rms(x, gain):
    ms = jnp.mean(x * x, axis=-1, keepdims=True)
    return x * lax.rsqrt(ms + EPS) * gain


def _ctx_tile(i):
    return jnp.minimum(i, N_CTX_TILES - 1)


def _lat_tile(i):
    return jnp.maximum(i - N_CTX_TILES, 0)


MOD_PARTS_PER_STEP = 2
MOD_TN = MOD_PARTS_PER_STEP * D_MODEL


def _mod_kernel(ct_ref, w_ref, b_ref, o_ref):
    c = ct_ref[...]
    s = c / (1.0 + jnp.exp(-c))
    row = lax.broadcasted_iota(jnp.int32, (N_MOD_ROWS, D_MODEL), 0)
    for part in range(MOD_PARTS_PER_STEP):
        cols = slice(part * D_MODEL, (part + 1) * D_MODEL)
        w = w_ref[:, cols]
        out = jnp.zeros((N_MOD_ROWS, D_MODEL), F32)
        for r in range(1 + DEC_BATCH):
            acc = jnp.sum(w * s[:, r:r + 1], axis=0, keepdims=True)
            out = jnp.where(row == r, acc, out)
        o_ref[part] = out + b_ref[:, cols]


def _modulation(cvecs_t, w_ada, b_ada):
    n_col = N_MOD_PARTS * D_MODEL
    return pl.pallas_call(
        _mod_kernel,
        grid=(DEPTH, n_col // MOD_TN),
        in_specs=[
            pl.BlockSpec((D_MODEL, N_MOD_ROWS), lambda l, j: (0, 0)),
            pl.BlockSpec((None, D_MODEL, MOD_TN), lambda l, j: (l, 0, j)),
            pl.BlockSpec((None, 1, MOD_TN), lambda l, j: (l, 0, j)),
        ],
        out_specs=pl.BlockSpec((None, MOD_PARTS_PER_STEP, N_MOD_ROWS, D_MODEL), lambda l, j: (l, j, 0, 0)),
        out_shape=jax.ShapeDtypeStruct((DEPTH, N_MOD_PARTS, N_MOD_ROWS, D_MODEL), F32),
        compiler_params=_params(2),
        name="adaln_modulation",
    )(cvecs_t, w_ada, b_ada.reshape(DEPTH, 1, n_col))


_MOD_SPEC = lambda l, nd: pl.BlockSpec(
    (None, N_MOD_PARTS, N_MOD_ROWS, D_MODEL), lambda *idx: (l, 0, 0, 0))


def _swap16(x):
    lane = lax.broadcasted_iota(jnp.int32, x.shape, 1)
    up = pltpu.roll(x, 16, 1)
    down = pltpu.roll(x, LANES - 16, 1)
    return jnp.where((lane & 16) != 0, up, down)


def _rope(x, cos_t, sin_t):
    cols = []
    for j in range(x.shape[1] // LANES):
        xc = x[:, j * LANES:(j + 1) * LANES]
        cols.append(xc * cos_t + _swap16(xc) * sin_t)
    return cols[0] if len(cols) == 1 else jnp.concatenate(cols, axis=1)


def _proj_kernel(*refs, split_x):
    n_x = 2 if split_x else 1
    x_refs, refs, (scr_even, scr_odd) = refs[:n_x], refs[n_x:-2], refs[-2:]
    step = functools.partial(_proj_step, x_refs, refs, split_x)
    i = pl.program_id(0)

    @pl.when(i == 0)
    def _():
        scr_odd[...] = jnp.zeros((TM, D_IN), F32)

    @pl.when(i % 2 == 0)
    def _():
        step(scr_even, scr_odd)

    @pl.when(i % 2 == 1)
    def _():
        step(scr_odd, scr_even)


def _proj_step(x_refs, refs, split_x, new_ref, proj):
    (mod_ref, g1_ref, w_ref, qg_ref, kg_ref, cos_ref, sin_ref, bq_ref, sw_ref, sb_ref, on_ref,
     qt_ref, k_ref, vt_ref, k32_ref, v32_ref, sgu_ref, cbz_ref) = refs
    i = pl.program_id(0)
    cur = _stage1_tile(i)

    is_ctx = i <= N_CTX_TILES
    cos_t = cos_ref[...]
    sin_t = sin_ref[...]

    q = proj[:, OFF_Q:OFF_Q + D_ATTN]
    qms = jnp.dot((q * q).astype(BF16), bq_ref[...], preferred_element_type=F32)
    qn = q * lax.rsqrt(qms + EPS) * qg_ref[...]
    qr = _rope(qn, cos_t, sin_t) * Q_SCALE
    qt_ref[...] = qr.T.astype(BF16)

    k = proj[:, OFF_KV:OFF_KV + D_KV]
    v = proj[:, OFF_KV + D_KV:OFF_KV + 2 * D_KV]
    kms = jnp.dot((k * k).astype(BF16), bq_ref[:D_KV, :D_KV], preferred_element_type=F32)
    kn = k * lax.rsqrt(kms + EPS) * kg_ref[...]
    k_ref[...] = _rope(kn, cos_t, sin_t).astype(BF16)
    vt_ref[...] = v.T.astype(BF16)

    su = proj[:, OFF_SGU:OFF_SGU + D_SGU]
    sv = proj[:, OFF_SGU + D_SGU:OFF_SGU + 2 * D_SGU]
    n_chunks = TM // CHUNK
    sv_wide = jnp.concatenate([sv[n * CHUNK:(n + 1) * CHUNK, :] for n in range(n_chunks)], axis=1)
    grp = lax.broadcasted_iota(jnp.int32, (CHUNK, n_chunks * D_SGU), 1) // SGU_GROUP_DIM % N_SGU_GROUPS
    mixed_wide = jnp.zeros((CHUNK, n_chunks * D_SGU), F32)
    for pair in range(N_SGU_GROUPS // 2):
        rhs = jnp.concatenate(
            [jnp.where(grp == 2 * pair, sv_wide, 0.0), jnp.where(grp == 2 * pair + 1, sv_wide, 0.0)],
            axis=0).astype(BF16)
        mixed_wide = mixed_wide + jnp.dot(sw_ref[pair], rhs, preferred_element_type=F32)
    on_sgu = on_ref[:, D_ATTN:D_ATTN + D_SGU]
    for n in range(n_chunks):
        sgu = su[n * CHUNK:(n + 1) * CHUNK, :] * (mixed_wide[:, n * D_SGU:(n + 1) * D_SGU] + sb_ref[...])
        sgu_ref[n * CHUNK:(n + 1) * CHUNK, :] = _rms(sgu, on_sgu).astype(BF16)

    cbz_ref[:, :D_CONV] = proj[:, OFF_CONV:OFF_CONV + D_CONV]
    cbz_ref[:, D_CONV:] = (proj[:, OFF_CONV + D_CONV:OFF_CONV + 2 * D_CONV]
                           * proj[:, OFF_CONV + 2 * D_CONV:OFF_CONV + 3 * D_CONV])

    x = jnp.where(cur < N_CTX_TILES, x_refs[0][...], x_refs[1][...]) if split_x else x_refs[0][...]
    mrow = _mod_row_of_tile(cur)
    h = _rms(x, g1_ref[...] * (1.0 + _mod_part(mod_ref, 1, mrow))) + _mod_part(mod_ref, 0, mrow)
    new_ref[...] = jnp.dot(h.astype(BF16), w_ref[...], preferred_element_type=F32)

    @pl.when(is_ctx)
    def _():
        k32_ref[...] = kn
        v32_ref[...] = v


def _x_specs(xs, tile_of_step=lambda i: i):
    if len(xs) == 1:
        return [pl.BlockSpec((TM, D_MODEL), lambda i: (tile_of_step(i), 0))]
    return [pl.BlockSpec((TM, D_MODEL), lambda i: (_ctx_tile(tile_of_step(i)), 0)),
            pl.BlockSpec((TM, D_MODEL), lambda i: (_lat_tile(tile_of_step(i)), 0))]


def _stage1_tile(i):
    return jnp.minimum(i, N_TILES - 1)


def _stage2_tile(i):
    return jnp.maximum(i - 1, 0)


def _proj(l, xs, mod, g1, w_in, qg, kg, cos_t, sin_t, bq, sgu_w, sgu_b, on):
    lay = lambda *rest: (lambda i: (l,) + rest)
    s2 = _stage2_tile
    return pl.pallas_call(
        functools.partial(_proj_kernel, split_x=len(xs) == 2),
        grid=(N_TILES + 1,),
        in_specs=_x_specs(xs, _stage1_tile) + [
            _MOD_SPEC(l, 1),
            pl.BlockSpec((None, 1, D_MODEL), lay(0, 0)),
            pl.BlockSpec((None, D_MODEL, D_IN), lay(0, 0)),
            pl.BlockSpec((None, 1, D_ATTN), lay(0, 0)),
            pl.BlockSpec((None, 1, D_KV), lay(0, 0)),
            pl.BlockSpec((TM, LANES), lambda i: (_rope_block_of_tile(s2(i)), 0)),
            pl.BlockSpec((TM, LANES), lambda i: (_rope_block_of_tile(s2(i)), 0)),
            pl.BlockSpec((D_ATTN, D_ATTN), lambda i: (0, 0)),
            pl.BlockSpec((None, N_SGU_GROUPS // 2, CHUNK, 2 * CHUNK), lay(0, 0, 0)),
            pl.BlockSpec((None, CHUNK, D_SGU), lay(0, 0)),
            pl.BlockSpec((None, 1, D_MIX), lay(0, 0)),
        ],
        out_specs=[
            pl.BlockSpec((None, D_ATTN, TM), lambda i: (s2(i), 0, 0)),
            pl.BlockSpec((TM, D_KV), lambda i: (s2(i), 0)),
            pl.BlockSpec((None, D_KV, TM), lambda i: (s2(i), 0, 0)),
            pl.BlockSpec((TM, D_KV), lambda i: (_ctx_tile(s2(i)), 0)),
            pl.BlockSpec((TM, D_KV), lambda i: (_ctx_tile(s2(i)), 0)),
            pl.BlockSpec((TM, D_SGU), lambda i: (s2(i), 0)),
            pl.BlockSpec((TM, 2 * D_CONV), lambda i: (s2(i), 0)),
        ],
        out_shape=[
            jax.ShapeDtypeStruct((N_TILES, D_ATTN, TM), BF16),
            jax.ShapeDtypeStruct((T_ALL, D_KV), BF16),
            jax.ShapeDtypeStruct((N_TILES, D_KV, TM), BF16),
            jax.ShapeDtypeStruct((T_CTX, D_KV), F32),
            jax.ShapeDtypeStruct((T_CTX, D_KV), F32),
            jax.ShapeDtypeStruct((T_ALL, D_SGU), BF16),
            jax.ShapeDtypeStruct((T_ALL, 2 * D_CONV), F32),
        ],
        scratch_shapes=[pltpu.VMEM((TM, D_IN), F32), pltpu.VMEM((TM, D_IN), F32)],
        compiler_params=_params(1),
        name=f"proj_l{l}",
    )(*xs, mod, g1, w_in, qg, kg, cos_t, sin_t, bq, sgu_w, sgu_b, on)


ACC_ROWS = HEAD_DIM + BF16_ROWS
N_PAIRS = N_HEADS // 2
PAIRS_PER_KV = N_PAIRS // N_KV_HEADS


def _pair_queries(qt, kv_idx, tq):
    qf = qt.astype(F32)
    mine = lax.broadcasted_iota(jnp.int32, (2 * HEAD_DIM, tq), 0) // HEAD_DIM == kv_idx
    qe = qf[:HEAD_DIM]
    qo = qf[HEAD_DIM:]
    rhs = jnp.concatenate(
        [jnp.where(mine, jnp.concatenate([qe, qe], axis=0), 0.0),
         jnp.where(mine, jnp.concatenate([qo, qo], axis=0), 0.0)], axis=1)
    return rhs.astype(BF16)


def _with_ones(vt_c):
    return jnp.concatenate([vt_c, jnp.ones((BF16_ROWS, vt_c.shape[1]), BF16)], axis=0)


def _flash_pairs(rhs_list, chunk_lists, tq):
    n_pairs, n_chunks = len(rhs_list), len(chunk_lists[0])
    scores = lambda s, c: jnp.dot(chunk_lists[s][c][0](), rhs_list[s], preferred_element_type=F32)
    m = [jnp.full((1, 2 * tq), NEG_BIG, F32)] * n_pairs
    acc = [jnp.zeros((ACC_ROWS, 2 * tq), F32)] * n_pairs
    st = [scores(s, 0) for s in range(n_pairs)]
    for c in range(n_chunks):
        for s in range(n_pairs):
            m_new = jnp.maximum(m[s], jnp.max(st[s], axis=0, keepdims=True))
            st_next = scores(s, c + 1) if c + 1 < n_chunks else None
            alpha = jnp.exp2(m[s] - m_new)
            p = jnp.exp2(st[s] - m_new).astype(BF16)
            acc[s] = alpha * acc[s] + jnp.dot(chunk_lists[s][c][1](), p, preferred_element_type=F32)
            m[s] = m_new
            st[s] = st_next
    outs = []
    for s in range(n_pairs):
        o = acc[s][:HEAD_DIM] * (1.0 / acc[s][HEAD_DIM:HEAD_DIM + 1])
        outs.append(jnp.concatenate([o[:, :tq], o[:, tq:]], axis=0))
    return outs


def _attn_ctx_kernel(qt_ref, k_ref, vt_ref, o_ref):
    tq = SEQ
    rhs_list, chunk_lists = [], []
    for pair in range(N_PAIRS):
        kv_idx = pair // PAIRS_PER_KV
        rhs_list.append(_pair_queries(qt_ref[pair * LANES:(pair + 1) * LANES, :], kv_idx, tq))
        chunk_lists.append([(lambda: k_ref[...],
                             lambda kv_idx=kv_idx: _with_ones(vt_ref[kv_idx * HEAD_DIM:(kv_idx + 1) * HEAD_DIM, :]))])
    for pair, o_t in enumerate(_flash_pairs(rhs_list, chunk_lists, tq)):
        o_ref[:, pair * LANES:(pair + 1) * LANES] = o_t.T.astype(BF16)


def _attn_ctx(l, qt, k, vt):
    halves = TM // SEQ
    return pl.pallas_call(
        _attn_ctx_kernel,
        grid=(BATCH,),
        in_specs=[
            pl.BlockSpec((None, D_ATTN, SEQ), lambda b: (b // halves, 0, b % halves)),
            pl.BlockSpec((SEQ, D_KV), lambda b: (b, 0)),
            pl.BlockSpec((None, D_KV, SEQ), lambda b: (b // halves, 0, b % halves)),
        ],
        out_specs=pl.BlockSpec((SEQ, D_ATTN), lambda b: (b, 0)),
        out_shape=jax.ShapeDtypeStruct((T_CTX, D_ATTN), BF16),
        compiler_params=_params(1),
        name=f"attn_ctx_l{l}",
    )(qt, k, vt)


KC = 512
LAT_PAIRS_PER_STEP = 2


def _attn_lat_kernel(qt_ref, k_ref, vt_ref, ck_ref, cv_ref, o_ref):
    tq = TM
    cvt = cv_ref[...].T
    rhs_list, chunk_lists = [], []
    for j in range(LAT_PAIRS_PER_STEP):
        kv_idx = (pl.program_id(2) * LAT_PAIRS_PER_STEP + j) // PAIRS_PER_KV
        rhs_list.append(_pair_queries(qt_ref[j * LANES:(j + 1) * LANES, :], kv_idx, tq))
        v_row = pl.multiple_of(kv_idx * HEAD_DIM, HEAD_DIM)

        def latent_chunk(c, v_row=v_row):
            tile, lane0 = (c * KC) // TM, (c * KC) % TM
            return (lambda: k_ref[c * KC:(c + 1) * KC, :],
                    lambda: _with_ones(vt_ref[tile, pl.ds(v_row, HEAD_DIM), lane0:lane0 + KC]))

        def cached_chunk(c, kv_idx=kv_idx):
            return (lambda: ck_ref[c * KC:(c + 1) * KC, :].astype(BF16),
                    lambda: _with_ones(jnp.where(kv_idx == 0, cvt[:HEAD_DIM, c * KC:(c + 1) * KC],
                                                 cvt[HEAD_DIM:, c * KC:(c + 1) * KC]).astype(BF16)))

        chunks = [latent_chunk(c) for c in range(DEC_SEQ // KC)]
        chunks += [cached_chunk(c) for c in range(PAST_LEN // KC)]
        chunk_lists.append(chunks)
    for j, o_t in enumerate(_flash_pairs(rhs_list, chunk_lists, tq)):
        o_ref[:, j * LANES:(j + 1) * LANES] = o_t.T.astype(BF16)


def _attn_lat(l, qt, k, vt, cache_k, cache_v):
    n_q = DEC_SEQ // TM
    first = N_CTX_TILES
    ctx_blocks = T_CTX // DEC_SEQ
    width = LAT_PAIRS_PER_STEP * LANES
    return pl.pallas_call(
        _attn_lat_kernel,
        grid=(DEC_BATCH, n_q, N_PAIRS // LAT_PAIRS_PER_STEP),
        in_specs=[
            pl.BlockSpec((None, width, TM), lambda b, i, p: (first + b * n_q + i, p, 0)),
            pl.BlockSpec((DEC_SEQ, D_KV), lambda b, i, p: (ctx_blocks + b, 0)),
            pl.BlockSpec((n_q, D_KV, TM), lambda b, i, p: (ctx_blocks + b, 0, 0)),
            pl.BlockSpec((None, None, PAST_LEN, D_KV), lambda b, i, p: (b, l, 0, 0)),
            pl.BlockSpec((None, None, PAST_LEN, D_KV), lambda b, i, p: (b, l, 0, 0)),
        ],
        out_specs=pl.BlockSpec((TM, width), lambda b, i, p: (b * n_q + i, p)),
        out_shape=jax.ShapeDtypeStruct((T_LAT, D_ATTN), BF16),
        compiler_params=_params(3),
        name=f"attn_lat_l{l}",
    )(qt, k, vt, cache_k, cache_v)


def _first_max_index(vals, valid, rowf):
    masked = jnp.where(valid, vals, -jnp.inf)
    mx = jnp.max(masked, axis=0, keepdims=True)
    idx = jnp.min(jnp.where(masked == mx, rowf, float(SUBLANES)), axis=0, keepdims=True)
    return mx, idx


def _merge_kernel(*refs, split_x):
    n_x = 2 if split_x else 1
    x_refs, refs = refs[:n_x], refs[n_x:]
    (actx_ref, alat_ref, sgu_ref, cbz_ref, prev_ref, next_ref, cw_ref, on_ref,
     wo_ref, mod_ref, g2_ref, wr_ref, wrhi_ref, rb_ref, x1_ref, h2_ref, route_ref, gate4_ref, cnt_ref) = refs
    cur = pl.program_id(0)
    is_ctx = cur < N_CTX_TILES
    mrow = _mod_row_of_tile(cur)
    on = on_ref[...]

    cbz = cbz_ref[...]
    cb = cbz[:, :D_CONV]
    z = cbz[:, D_CONV:]
    seq_mask = jnp.where(is_ctx, SEQ - 1, DEC_SEQ - 1)
    row = lax.broadcasted_iota(jnp.int32, (TM, D_CONV), 0)
    pos = (cur * TM + row) & seq_mask
    z_prev = jnp.where(row == 0, prev_ref[SUBLANES - 1:SUBLANES, D_CONV:], pltpu.roll(z, 1, 0))
    z_next = jnp.where(row == TM - 1, next_ref[0:1, D_CONV:], pltpu.roll(z, TM - 1, 0))
    z_prev = jnp.where(pos == 0, 0.0, z_prev)
    z_next = jnp.where(pos == seq_mask, 0.0, z_next)
    cw = cw_ref[...]
    conv = cb * (z_prev * cw[0:1, :] + z * cw[1:2, :] + z_next * cw[2:3, :])
    conv_n = _rms(conv, on[:, D_ATTN + D_SGU:]).astype(BF16)

    attn = jnp.where(is_ctx, actx_ref[...].astype(F32), alat_ref[...].astype(F32))
    attn_n = _rms(attn, on[:, :D_ATTN]).astype(BF16)
    merged = jnp.concatenate([attn_n, sgu_ref[...], conv_n], axis=1)
    x = jnp.where(is_ctx, x_refs[0][...], x_refs[1][...]) if split_x else x_refs[0][...]
    x1 = x + _mod_part(mod_ref, 2, mrow) * jnp.dot(merged, wo_ref[...], preferred_element_type=F32)
    x1_ref[...] = x1
    _route_tile(x1, mrow, mod_ref, g2_ref, wr_ref, wrhi_ref, rb_ref, h2_ref, route_ref, gate4_ref, cnt_ref)


def _route_tile(x1, mrow, mod_ref, g2_ref, wr_ref, wrhi_ref, rb_ref, h2_ref, route_ref, gate4_ref, cnt_ref):
    h2 = _rms(x1, g2_ref[...] * (1.0 + _mod_part(mod_ref, 4, mrow))) + _mod_part(mod_ref, 3, mrow)
    hb = h2.astype(BF16)
    h2_ref[...] = hb

    h_lo = (h2 - hb.astype(F32)).astype(BF16)
    a = jnp.dot(hb, wr_ref[...], preferred_element_type=F32)
    b = jnp.dot(h_lo, wrhi_ref[...], preferred_element_type=F32)
    logits = a[:, :ROUTER_COLS] + a[:, ROUTER_COLS:] + b + rb_ref[...]
    lt = logits.T

    rowi = lax.broadcasted_iota(jnp.int32, (SUBLANES, TM), 0).astype(F32)
    valid = rowi < float(N_EXPERT_GROUPS)
    g_log = lt[0:SUBLANES]
    g_max, g_idx = _first_max_index(g_log, valid, rowi)
    p_g = 1.0 / jnp.sum(jnp.where(valid, jnp.exp(g_log - g_max), 0.0), axis=0, keepdims=True)
    e_log = jnp.zeros((SUBLANES, TM), F32)
    for g in range(N_EXPERT_GROUPS):
        e_log = jnp.where(g_idx == g, lt[(g + 1) * SUBLANES:(g + 2) * SUBLANES], e_log)
    e_max, i1 = _first_max_index(e_log, valid, rowi)
    e_exp = jnp.where(valid, jnp.exp(e_log - e_max), 0.0)
    e_prob = e_exp / jnp.sum(e_exp, axis=0, keepdims=True)
    v1 = jnp.max(e_prob, axis=0, keepdims=True)
    rest = jnp.logical_and(valid, rowi != i1)
    v2, i2 = _first_max_index(e_prob, rest, rowi)
    denom = v1 + v2
    w_sel = p_g * jnp.where(rowi == i1, v1 / denom, jnp.where(rowi == i2, v2 / denom, 0.0))
    route_ref[...] = jnp.broadcast_to(g_idx, (SUBLANES, TM))
    pad = jnp.zeros((LANES - SUBLANES, TM), F32)
    gate4_ref[...] = jnp.concatenate([w_sel, pad], axis=0).T
    counts = jnp.sum(jnp.where(rowi == g_idx, 1.0, 0.0), axis=1, keepdims=True)
    cnt_ref[...] = jnp.broadcast_to(counts, (SUBLANES, LANES)).astype(jnp.int32)


def _merge(l, xs, attn_ctx, attn_lat, sgu_n, cbz, conv_w, on, w_out, mod, g2, wr_cat, wr_hi, rb):
    lay = lambda *rest: (lambda i: (l,) + rest)
    rows8 = TM // SUBLANES
    return pl.pallas_call(
        functools.partial(_merge_kernel, split_x=len(xs) == 2),
        grid=(N_TILES,),
        in_specs=_x_specs(xs) + [
            pl.BlockSpec((TM, D_ATTN), lambda i: (_ctx_tile(i), 0)),
            pl.BlockSpec((TM, D_ATTN), lambda i: (_lat_tile(i), 0)),
            pl.BlockSpec((TM, D_SGU), lambda i: (i, 0)),
            pl.BlockSpec((TM, 2 * D_CONV), lambda i: (i, 0)),
            pl.BlockSpec((SUBLANES, 2 * D_CONV), lambda i: (jnp.maximum(i * rows8 - 1, 0), 0)),
            pl.BlockSpec((SUBLANES, 2 * D_CONV),
                         lambda i: (jnp.minimum((i + 1) * rows8, T_ALL // SUBLANES - 1), 0)),
            pl.BlockSpec((None, 3, D_CONV), lay(0, 0)),
            pl.BlockSpec((None, 1, D_MIX), lay(0, 0)),
            pl.BlockSpec((None, D_MIX, D_MODEL), lay(0, 0)),
            _MOD_SPEC(l, 1),
            pl.BlockSpec((None, 1, D_MODEL), lay(0, 0)),
            pl.BlockSpec((None, D_MODEL, 2 * ROUTER_COLS), lay(0, 0)),
            pl.BlockSpec((None, D_MODEL, ROUTER_COLS), lay(0, 0)),
            pl.BlockSpec((None, 1, ROUTER_COLS), lay(0, 0)),
        ],
        out_specs=[
            pl.BlockSpec((TM, D_MODEL), lambda i: (i, 0)),
            pl.BlockSpec((TM, D_MODEL), lambda i: (i, 0)),
            pl.BlockSpec((None, SUBLANES, TM), lambda i: (i, 0, 0)),
            pl.BlockSpec((TM, LANES), lambda i: (i, 0)),
            pl.BlockSpec((None, SUBLANES, LANES), lambda i: (i, 0, 0)),
        ],
        out_shape=[
            jax.ShapeDtypeStruct((T_ALL, D_MODEL), F32),
            jax.ShapeDtypeStruct((T_ALL, D_MODEL), BF16),
            jax.ShapeDtypeStruct((N_TILES, SUBLANES, TM), F32),
            jax.ShapeDtypeStruct((T_ALL, LANES), F32),
            jax.ShapeDtypeStruct((N_TILES, SUBLANES, LANES), jnp.int32),
        ],
        compiler_params=_params(1),
        name=f"merge_l{l}",
    )(*xs, attn_ctx, attn_lat, sgu_n, cbz, cbz, cbz, conv_w, on, w_out, mod, g2, wr_cat, wr_hi, rb)


RB = 144
SORT_ROWS = TM + LANES
SORT_ROWS_PAD = -(-(SORT_ROWS + RB) // BF16_ROWS) * BF16_ROWS
MOE_VMEM_LIMIT = 56 * 1024 * 1024


def _split3(x):
    hi = x.astype(BF16)
    r1 = x - hi.astype(F32)
    mid = r1.astype(BF16)
    lo = (r1 - mid.astype(F32)).astype(BF16)
    return hi, mid, lo


def _moe_kernel(cnt_ref, h_ref, route_ref, gate4_ref, tri_ref, w1_ref, w3_ref, w2_ref, x1_ref, mod_ref, fn_ref,
                *refs, final):
    out_refs, (xs_ref, gs_ref, zs_ref) = refs[:-3], refs[-3:]
    i = pl.program_id(0)
    gate2 = _mod_part(mod_ref, 5, _mod_row_of_tile(i))
    counts = [cnt_ref[i, g] for g in range(N_EXPERT_GROUPS)]
    starts = [jnp.int32(0)]
    for g in range(N_EXPERT_GROUPS - 1):
        starts.append(starts[-1] + (counts[g] + (BF16_ROWS - 1)) // BF16_ROWS * BF16_ROWS)

    g_idx = route_ref[0:1, :]
    rowi = lax.broadcasted_iota(jnp.int32, (SUBLANES, TM), 0).astype(F32)
    onehot = jnp.where(rowi == g_idx, 1.0, 0.0)
    incl = jnp.dot(onehot.astype(BF16), tri_ref[...], preferred_element_type=F32)
    pos = jnp.sum(onehot * incl, axis=0, keepdims=True) - 1.0
    for g in range(1, N_EXPERT_GROUPS):
        pos = pos + jnp.where(g_idx == float(g), starts[g].astype(F32), 0.0)
    sub = lax.broadcasted_iota(jnp.int32, (SORT_ROWS, TM), 0).astype(F32)
    perm = jnp.where(sub == pos, 1.0, 0.0).astype(BF16)
    pos_col = jnp.broadcast_to(pos, (LANES, TM)).T
    lane = lax.broadcasted_iota(jnp.int32, (TM, SORT_ROWS), 1).astype(F32)
    perm_t = jnp.where(lane == jnp.concatenate([pos_col] * (SORT_ROWS // LANES), axis=1), 1.0, 0.0).astype(BF16)

    xs_ref[:SORT_ROWS, :] = jnp.dot(perm, h_ref[...], preferred_element_type=F32).astype(BF16)
    xs_ref[SORT_ROWS:, :] = jnp.zeros((SORT_ROWS_PAD - SORT_ROWS, D_MODEL), BF16)
    hi, mid, lo = _split3(gate4_ref[...])
    packed = (hi.astype(F32) + pltpu.roll(mid.astype(F32), EXPERTS_PER_GROUP, 1)
              + pltpu.roll(lo.astype(F32), 2 * EXPERTS_PER_GROUP, 1)).astype(BF16)
    gsp = jnp.dot(perm, packed, preferred_element_type=F32)
    gs_ref[:SORT_ROWS, :] = (gsp + pltpu.roll(gsp, LANES - EXPERTS_PER_GROUP, 1)
                             + pltpu.roll(gsp, LANES - 2 * EXPERTS_PER_GROUP, 1))
    gs_ref[SORT_ROWS:, :] = jnp.zeros((SORT_ROWS_PAD - SORT_ROWS, LANES), F32)
    zs_ref[...] = jnp.zeros((SORT_ROWS_PAD, D_MODEL), BF16)

    for g in range(N_EXPERT_GROUPS):
        def block_ffn(blk, carry, g=g):
            row0 = pl.multiple_of(starts[g] + blk * RB, BF16_ROWS)
            xb = xs_ref[pl.ds(row0, RB), :]
            gb = gs_ref[pl.ds(row0, RB), :]
            cols = []
            for e in range(EXPERTS_PER_GROUP):
                a = jnp.dot(xb, w1_ref[g * EXPERTS_PER_GROUP + e], preferred_element_type=F32)
                b = jnp.dot(xb, w3_ref[g * EXPERTS_PER_GROUP + e], preferred_element_type=F32)
                cols.append((a / (1.0 + jnp.exp(-a)) * b * gb[:, e:e + 1]).astype(BF16))
            z = jnp.dot(jnp.concatenate(cols, axis=1), w2_ref[g], preferred_element_type=F32)
            zs_ref[pl.ds(row0, RB), :] = z.astype(BF16)
            return carry

        lax.fori_loop(0, (counts[g] + (RB - 1)) // RB, block_ffn, 0)

    y = jnp.dot(perm_t, zs_ref[:SORT_ROWS, :], preferred_element_type=F32)
    out = x1_ref[...] + gate2 * y

    if final:
        octx_ref, olat_ref = out_refs
        out = _rms(out, fn_ref[...])
        olat_ref[...] = out

        @pl.when(i < N_CTX_TILES)
        def _():
            octx_ref[...] = out
    else:
        out_refs[0][...] = out


def _moe(l, counts, h2, route, gate4, tri, w1, w3, w2, x1, mod, final_norm, final):
    tile = pl.BlockSpec((TM, D_MODEL), lambda i, c: (i, 0))
    resident = pl.Buffered(1)
    if final:
        out_specs = [pl.BlockSpec((TM, D_MODEL), lambda i, c: (_ctx_tile(i), 0)),
                     pl.BlockSpec((TM, D_MODEL), lambda i, c: (_lat_tile(i), 0))]
        out_shape = [jax.ShapeDtypeStruct((T_CTX, D_MODEL), F32), jax.ShapeDtypeStruct((T_LAT, D_MODEL), F32)]
    else:
        out_specs = tile
        out_shape = jax.ShapeDtypeStruct((T_ALL, D_MODEL), F32)
    grid_spec = pltpu.PrefetchScalarGridSpec(
        num_scalar_prefetch=1,
        grid=(N_TILES,),
        in_specs=[
            tile,
            pl.BlockSpec((None, SUBLANES, TM), lambda i, c: (i, 0, 0)),
            pl.BlockSpec((TM, LANES), lambda i, c: (i, 0)),
            pl.BlockSpec((TM, TM), lambda i, c: (0, 0), pipeline_mode=resident),
            pl.BlockSpec((None, N_EXPERTS, D_MODEL, D_EXPERT), lambda i, c: (l, 0, 0, 0), pipeline_mode=resident),
            pl.BlockSpec((None, N_EXPERTS, D_MODEL, D_EXPERT), lambda i, c: (l, 0, 0, 0), pipeline_mode=resident),
            pl.BlockSpec((None, N_EXPERT_GROUPS, D_GROUP_HID, D_MODEL), lambda i, c: (l, 0, 0, 0),
                         pipeline_mode=resident),
            tile,
            pl.BlockSpec((None, N_MOD_PARTS, N_MOD_ROWS, D_MODEL), lambda i, c: (l, 0, 0, 0)),
            pl.BlockSpec((1, D_MODEL), lambda i, c: (0, 0)),
        ],
        out_specs=out_specs,
        scratch_shapes=[pltpu.VMEM((SORT_ROWS_PAD, D_MODEL), BF16),
                        pltpu.VMEM((SORT_ROWS_PAD, LANES), F32),
                        pltpu.VMEM((SORT_ROWS_PAD, D_MODEL), BF16)],
    )
    return pl.pallas_call(
        functools.partial(_moe_kernel, final=final),
        grid_spec=grid_spec,
        out_shape=out_shape,
        compiler_params=pltpu.CompilerParams(dimension_semantics=("arbitrary",),
                                             vmem_limit_bytes=MOE_VMEM_LIMIT),
        name=f"moe_l{l}",
    )(counts, h2, route, gate4, tri, w1, w3, w2, x1, mod, final_norm)


def _rope_tables():
    rows = DEC_SEQ // GRID_W
    row = np.repeat(np.arange(rows, dtype=np.float32), GRID_W)
    col = np.tile(np.arange(GRID_W, dtype=np.float32), rows)
    half = ROPE_AXIS_DIM // 2
    inv_freq = (1.0 / (np.float32(ROPE_THETA) ** (np.arange(half, dtype=np.float32) * np.float32(2.0)
                                                  / np.float32(ROPE_AXIS_DIM)))).astype(np.float32)
    ar = row[:, None] * inv_freq
    ac = col[:, None] * inv_freq
    cos64 = np.concatenate([np.cos(ar), np.cos(ar), np.cos(ac), np.cos(ac)], axis=1)
    sin64 = np.concatenate([-np.sin(ar), np.sin(ar), -np.sin(ac), np.sin(ac)], axis=1)
    cos_t = np.concatenate([np.ones((TM, LANES), np.float32), np.tile(cos64, (1, 2))], axis=0)
    sin_t = np.concatenate([np.zeros((TM, LANES), np.float32), np.tile(sin64, (1, 2))], axis=0)
    return jnp.asarray(cos_t, F32), jnp.asarray(sin_t, F32)


def _router_weights(router_g_w, router_g_b, router_e_w, router_e_b):
    zw = jnp.zeros((DEPTH, D_MODEL, SUBLANES - N_EXPERT_GROUPS), F32)
    zb = jnp.zeros((DEPTH, SUBLANES - N_EXPERT_GROUPS), F32)
    w_cols, b_cols = [router_g_w, zw], [router_g_b, zb]
    for g in range(N_EXPERT_GROUPS):
        w_cols += [router_e_w[:, g], zw]
        b_cols += [router_e_b[:, g], zb]
    used = (1 + N_EXPERT_GROUPS) * SUBLANES
    w_cols.append(jnp.zeros((DEPTH, D_MODEL, ROUTER_COLS - used), F32))
    b_cols.append(jnp.zeros((DEPTH, ROUTER_COLS - used), F32))
    w = jnp.concatenate(w_cols, axis=2)
    b = jnp.concatenate(b_cols, axis=1).reshape(DEPTH, 1, ROUTER_COLS)
    w_hi = w.astype(BF16)
    w_lo = (w - w_hi.astype(F32)).astype(BF16)
    return jnp.concatenate([w_hi, w_lo], axis=2), w_hi, b


def kernel(x_prompt, x_sample, cache_k, cache_v, c, c_ctx, w_ada, b_ada, norm1, w_in, q_norm, k_norm,
           sgu_w, sgu_b, conv_w, out_norm, w_out, norm2, router_g_w, router_g_b, router_e_w,
           router_e_b, moe_w1, moe_w3, moe_w2, final_norm):
    xs = (x_prompt.reshape(T_CTX, D_MODEL), x_sample.reshape(T_LAT, D_MODEL))

    cvecs = jnp.concatenate([c_ctx[None, :], c, jnp.zeros((N_MOD_ROWS - 1 - DEC_BATCH, D_MODEL), F32)], axis=0)
    mod = _modulation(cvecs.T, w_ada, b_ada)

    w_in_b = w_in.astype(BF16)
    w_out_b = w_out.astype(BF16)
    g1 = norm1.reshape(DEPTH, 1, D_MODEL)
    g2 = norm2.reshape(DEPTH, 1, D_MODEL)
    on = out_norm.reshape(DEPTH, 1, D_MIX)
    qg = jnp.tile(q_norm, (1, N_HEADS)).reshape(DEPTH, 1, D_ATTN)
    kg = jnp.tile(k_norm, (1, N_KV_HEADS)).reshape(DEPTH, 1, D_KV)
    cos_t, sin_t = _rope_tables()
    bq = jnp.asarray(np.kron(np.eye(N_HEADS), np.full((HEAD_DIM, HEAD_DIM), 1.0 / HEAD_DIM)), BF16)
    sgu_w_pairs = sgu_w.astype(BF16).reshape(DEPTH, N_SGU_GROUPS // 2, 2, CHUNK, CHUNK)
    sgu_w_pairs = jnp.concatenate([sgu_w_pairs[:, :, 0], sgu_w_pairs[:, :, 1]], axis=-1)
    sgu_bias = jnp.repeat(jnp.swapaxes(sgu_b, 1, 2), SGU_GROUP_DIM, axis=2)
    wr_cat, wr_hi, rb = _router_weights(router_g_w, router_g_b, router_e_w, router_e_b)
    w1 = moe_w1.astype(BF16)
    w3 = moe_w3.astype(BF16)
    w2 = moe_w2.astype(BF16).reshape(DEPTH, N_EXPERT_GROUPS, D_GROUP_HID, D_MODEL)
    fn = final_norm.reshape(1, D_MODEL)
    tri = jnp.asarray(np.triu(np.ones((TM, TM), np.float32)), BF16)
    ck = cache_k.reshape(DEC_BATCH, DEPTH, PAST_LEN, D_KV)
    cv = cache_v.reshape(DEC_BATCH, DEPTH, PAST_LEN, D_KV)

    ctx_ks, ctx_vs = [], []
    for l in range(DEPTH):
        qt, k, vt, k32, v32, sgu_n, cbz = _proj(l, xs, mod, g1, w_in_b, qg, kg, cos_t, sin_t, bq,
                                                sgu_w_pairs, sgu_bias, on)
        attn_ctx = _attn_ctx(l, qt, k, vt)
        attn_lat = _attn_lat(l, qt, k, vt, ck, cv)
        x1, h2, route, gate4, cnt = _merge(l, xs, attn_ctx, attn_lat, sgu_n, cbz, conv_w, on, w_out_b, mod, g2,
                                           wr_cat, wr_hi, rb)
        counts = cnt[:, :N_EXPERT_GROUPS, 0]
        out = _moe(l, counts, h2, route, gate4, tri, w1, w3, w2, x1, mod, fn, final=(l == DEPTH - 1))
        xs = (out,)
        ctx_ks.append(k32.reshape(BATCH, SEQ, N_KV_HEADS, HEAD_DIM))
        ctx_vs.append(v32.reshape(BATCH, SEQ, N_KV_HEADS, HEAD_DIM))

    y_prompt, y_sample = out
    return (y_prompt.reshape(BATCH, SEQ, D_MODEL), y_sample.reshape(DEC_BATCH, DEC_SEQ, D_MODEL),
            jnp.stack(ctx_ks, axis=1), jnp.stack(ctx_vs, axis=1))
```

```python
import functools
import math

import jax
import jax.numpy as jnp
import numpy as np
from jax import lax
from jax.experimental import pallas as pl
from jax.experimental.pallas import tpu as pltpu

D_MODEL = 1024
BATCH = 16
SEQ = 256
DEPTH = 2
DEC_BATCH = 2
DEC_SEQ = 4096
PAST_LEN = 512
GRID_W = 64
N_HEADS = 8
N_KV_HEADS = 2
HEAD_DIM = 64
ROPE_AXIS_DIM = HEAD_DIM // 2
ROPE_THETA = 10000.0
D_ATTN = N_HEADS * HEAD_DIM
D_KV = N_KV_HEADS * HEAD_DIM
CHUNK = 128
N_SGU_GROUPS = 4
SGU_GROUP_DIM = 64
D_SGU = N_SGU_GROUPS * SGU_GROUP_DIM
D_CONV = 256
D_MIX = D_ATTN + D_SGU + D_CONV
D_IN = D_ATTN + 2 * D_KV + 2 * D_SGU + 3 * D_CONV
N_EXPERT_GROUPS = 4
EXPERTS_PER_GROUP = 4
N_EXPERTS = N_EXPERT_GROUPS * EXPERTS_PER_GROUP
D_EXPERT = 256
D_GROUP_HID = EXPERTS_PER_GROUP * D_EXPERT
EPS = 1e-6

T_CTX = BATCH * SEQ
T_LAT = DEC_BATCH * DEC_SEQ
T_ALL = T_CTX + T_LAT
TM = 512
N_TILES = T_ALL // TM
N_CTX_TILES = T_CTX // TM
LAT_TILES_PER_BATCH = DEC_SEQ // TM
TM_MOE = 1024
LANES = 128
SUBLANES = 8
BF16_ROWS = 16
N_MOD_ROWS = 8
N_MOD_PARTS = 6
ROUTER_COLS = 128
VMEM_LIMIT = 48 * 1024 * 1024

OFF_Q = 0
OFF_KV = D_ATTN
OFF_SGU = OFF_KV + 2 * D_KV
OFF_CONV = OFF_SGU + 2 * D_SGU

BF16 = jnp.bfloat16
F32 = jnp.float32
NEG_BIG = -1e30
Q_SCALE = HEAD_DIM ** -0.5 * math.log2(math.e)


def _params(n_grid_dims):
    return pltpu.CompilerParams(
        dimension_semantics=("arbitrary",) * n_grid_dims,
        vmem_limit_bytes=VMEM_LIMIT,
    )


def _mod_row_of_tile(i, tile=TM):
    n_ctx = T_CTX // tile
    return jnp.where(i < n_ctx, 0, 1 + (i - n_ctx) // (DEC_SEQ // tile))


def _mod_part(mod_ref, part, row):
    return mod_ref[part, pl.ds(row, 1), :]


def _rope_block_of_tile(i):
    return jnp.where(i < N_CTX_TILES, 0, 1 + (i - N_CTX_TILES) % LAT_TILES_PER_BATCH)


def _rms(x, gain):
    ms = jnp.mean(x * x, axis=-1, keepdims=True)
    return x * lax.rsqrt(ms + EPS) * gain


def _ctx_tile(i):
    return jnp.minimum(i, N_CTX_TILES - 1)


def _lat_tile(i):
    return jnp.maximum(i - N_CTX_TILES, 0)


MOD_PARTS_PER_STEP = 2
MOD_TN = MOD_PARTS_PER_STEP * D_MODEL


def _mod_kernel(ct_ref, w_ref, b_ref, o_ref):
    c = ct_ref[...]
    s = c / (1.0 + jnp.exp(-c))
    row = lax.broadcasted_iota(jnp.int32, (N_MOD_ROWS, D_MODEL), 0)
    for part in range(MOD_PARTS_PER_STEP):
        cols = slice(part * D_MODEL, (part + 1) * D_MODEL)
        w = w_ref[:, cols]
        out = jnp.zeros((N_MOD_ROWS, D_MODEL), F32)
        for r in range(1 + DEC_BATCH):
            acc = jnp.sum(w * s[:, r:r + 1], axis=0, keepdims=True)
            out = jnp.where(row == r, acc, out)
        o_ref[part] = out + b_ref[:, cols]


def _modulation(cvecs_t, w_ada, b_ada):
    n_col = N_MOD_PARTS * D_MODEL
    return pl.pallas_call(
        _mod_kernel,
        grid=(DEPTH, n_col // MOD_TN),
        in_specs=[
            pl.BlockSpec((D_MODEL, N_MOD_ROWS), lambda l, j: (0, 0)),
            pl.BlockSpec((None, D_MODEL, MOD_TN), lambda l, j: (l, 0, j)),
            pl.BlockSpec((None, 1, MOD_TN), lambda l, j: (l, 0, j)),
        ],
        out_specs=pl.BlockSpec((None, MOD_PARTS_PER_STEP, N_MOD_ROWS, D_MODEL), lambda l, j: (l, j, 0, 0)),
        out_shape=jax.ShapeDtypeStruct((DEPTH, N_MOD_PARTS, N_MOD_ROWS, D_MODEL), F32),
        compiler_params=_params(2),
        name="adaln_modulation",
    )(cvecs_t, w_ada, b_ada.reshape(DEPTH, 1, n_col))


_MOD_SPEC = lambda l, nd: pl.BlockSpec(
    (None, N_MOD_PARTS, N_MOD_ROWS, D_MODEL), lambda *idx: (l, 0, 0, 0))


def _swap16(x):
    lane = lax.broadcasted_iota(jnp.int32, x.shape, 1)
    up = pltpu.roll(x, 16, 1)
    down = pltpu.roll(x, LANES - 16, 1)
    return jnp.where((lane & 16) != 0, up, down)


def _rope(x, cos_t, sin_t):
    cols = []
    for j in range(x.shape[1] // LANES):
        xc = x[:, j * LANES:(j + 1) * LANES]
        cols.append(xc * cos_t + _swap16(xc) * sin_t)
    return cols[0] if len(cols) == 1 else jnp.concatenate(cols, axis=1)


def _proj_kernel(*refs, split_x):
    n_x = 2 if split_x else 1
    x_refs, refs, (scr_even, scr_odd) = refs[:n_x], refs[n_x:-2], refs[-2:]
    step = functools.partial(_proj_step, x_refs, refs, split_x)
    i = pl.program_id(0)

    @pl.when(i == 0)
    def _():
        scr_odd[...] = jnp.zeros((TM, D_IN), F32)

    @pl.when(i % 2 == 0)
    def _():
        step(scr_even, scr_odd)

    @pl.when(i % 2 == 1)
    def _():
        step(scr_odd, scr_even)


def _proj_step(x_refs, refs, split_x, new_ref, proj):
    (mod_ref, g1_ref, w_ref, qg_ref, kg_ref, cos_ref, sin_ref, bq_ref, sw_ref, sb_ref, on_ref,
     qt_ref, k_ref, vt_ref, k32_ref, v32_ref, sgu_ref, cbz_ref) = refs
    i = pl.program_id(0)
    cur = _stage1_tile(i)

    is_ctx = i <= N_CTX_TILES
    cos_t = cos_ref[...]
    sin_t = sin_ref[...]

    q = proj[:, OFF_Q:OFF_Q + D_ATTN]
    qms = jnp.dot((q * q).astype(BF16), bq_ref[...], preferred_element_type=F32)
    qn = q * lax.rsqrt(qms + EPS) * qg_ref[...]
    qr = _rope(qn, cos_t, sin_t) * Q_SCALE
    qt_ref[...] = qr.T.astype(BF16)

    k = proj[:, OFF_KV:OFF_KV + D_KV]
    v = proj[:, OFF_KV + D_KV:OFF_KV + 2 * D_KV]
    kms = jnp.dot((k * k).astype(BF16), bq_ref[:D_KV, :D_KV], preferred_element_type=F32)
    kn = k * lax.rsqrt(kms + EPS) * kg_ref[...]
    k_ref[...] = _rope(kn, cos_t, sin_t).astype(BF16)
    vt_ref[...] = v.T.astype(BF16)

    su = proj[:, OFF_SGU:OFF_SGU + D_SGU]
    sv = proj[:, OFF_SGU + D_SGU:OFF_SGU + 2 * D_SGU]
    n_chunks = TM // CHUNK
    sv_wide = jnp.concatenate([sv[n * CHUNK:(n + 1) * CHUNK, :] for n in range(n_chunks)], axis=1)
    grp = lax.broadcasted_iota(jnp.int32, (CHUNK, n_chunks * D_SGU), 1) // SGU_GROUP_DIM % N_SGU_GROUPS
    mixed_wide = jnp.zeros((CHUNK, n_chunks * D_SGU), F32)
    for pair in range(N_SGU_GROUPS // 2):
        rhs = jnp.concatenate(
            [jnp.where(grp == 2 * pair, sv_wide, 0.0), jnp.where(grp == 2 * pair + 1, sv_wide, 0.0)],
            axis=0).astype(BF16)
        mixed_wide = mixed_wide + jnp.dot(sw_ref[pair], rhs, preferred_element_type=F32)
    on_sgu = on_ref[:, D_ATTN:D_ATTN + D_SGU]
    for n in range(n_chunks):
        sgu = su[n * CHUNK:(n + 1) * CHUNK, :] * (mixed_wide[:, n * D_SGU:(n + 1) * D_SGU] + sb_ref[...])
        sgu_ref[n * CHUNK:(n + 1) * CHUNK, :] = _rms(sgu, on_sgu).astype(BF16)

    cbz_ref[:, :D_CONV] = proj[:, OFF_CONV:OFF_CONV + D_CONV]
    cbz_ref[:, D_CONV:] = (proj[:, OFF_CONV + D_CONV:OFF_CONV + 2 * D_CONV]
                           * proj[:, OFF_CONV + 2 * D_CONV:OFF_CONV + 3 * D_CONV])

    x = jnp.where(cur < N_CTX_TILES, x_refs[0][...], x_refs[1][...]) if split_x else x_refs[0][...]
    mrow = _mod_row_of_tile(cur)
    h = _rms(x, g1_ref[...] * (1.0 + _mod_part(mod_ref, 1, mrow))) + _mod_part(mod_ref, 0, mrow)
    new_ref[...] = jnp.dot(h.astype(BF16), w_ref[...], preferred_element_type=F32)

    @pl.when(is_ctx)
    def _():
        k32_ref[...] = kn
        v32_ref[...] = v


def _x_specs(xs, tile_of_step=lambda i: i):
    if len(xs) == 1:
        return [pl.BlockSpec((TM, D_MODEL), lambda i: (tile_of_step(i), 0))]
    return [pl.BlockSpec((TM, D_MODEL), lambda i: (_ctx_tile(tile_of_step(i)), 0)),
            pl.BlockSpec((TM, D_MODEL), lambda i: (_lat_tile(tile_of_step(i)), 0))]


def _stage1_tile(i):
    return jnp.minimum(i, N_TILES - 1)


def _stage2_tile(i):
    return jnp.maximum(i - 1, 0)


def _proj(l, xs, mod, g1, w_in, qg, kg, cos_t, sin_t, bq, sgu_w, sgu_b, on):
    lay = lambda *rest: (lambda i: (l,) + rest)
    s2 = _stage2_tile
    return pl.pallas_call(
        functools.partial(_proj_kernel, split_x=len(xs) == 2),
        grid=(N_TILES + 1,),
        in_specs=_x_specs(xs, _stage1_tile) + [
            _MOD_SPEC(l, 1),
            pl.BlockSpec((None, 1, D_MODEL), lay(0, 0)),
            pl.BlockSpec((None, D_MODEL, D_IN), lay(0, 0)),
            pl.BlockSpec((None, 1, D_ATTN), lay(0, 0)),
            pl.BlockSpec((None, 1, D_KV), lay(0, 0)),
            pl.BlockSpec((TM, LANES), lambda i: (_rope_block_of_tile(s2(i)), 0)),
            pl.BlockSpec((TM, LANES), lambda i: (_rope_block_of_tile(s2(i)), 0)),
            pl.BlockSpec((D_ATTN, D_ATTN), lambda i: (0, 0)),
            pl.BlockSpec((None, N_SGU_GROUPS // 2, CHUNK, 2 * CHUNK), lay(0, 0, 0)),
            pl.BlockSpec((None, CHUNK, D_SGU), lay(0, 0)),
            pl.BlockSpec((None, 1, D_MIX), lay(0, 0)),
        ],
        out_specs=[
            pl.BlockSpec((None, D_ATTN, TM), lambda i: (s2(i), 0, 0)),
            pl.BlockSpec((TM, D_KV), lambda i: (s2(i), 0)),
            pl.BlockSpec((None, D_KV, TM), lambda i: (s2(i), 0, 0)),
            pl.BlockSpec((TM, D_KV), lambda i: (_ctx_tile(s2(i)), 0)),
            pl.BlockSpec((TM, D_KV), lambda i: (_ctx_tile(s2(i)), 0)),
            pl.BlockSpec((TM, D_SGU), lambda i: (s2(i), 0)),
            pl.BlockSpec((TM, 2 * D_CONV), lambda i: (s2(i), 0)),
        ],
        out_shape=[
            jax.ShapeDtypeStruct((N_TILES, D_ATTN, TM), BF16),
            jax.ShapeDtypeStruct((T_ALL, D_KV), BF16),
            jax.ShapeDtypeStruct((N_TILES, D_KV, TM), BF16),
            jax.ShapeDtypeStruct((T_CTX, D_KV), F32),
            jax.ShapeDtypeStruct((T_CTX, D_KV), F32),
            jax.ShapeDtypeStruct((T_ALL, D_SGU), BF16),
            jax.ShapeDtypeStruct((T_ALL, 2 * D_CONV), F32),
        ],
        scratch_shapes=[pltpu.VMEM((TM, D_IN), F32), pltpu.VMEM((TM, D_IN), F32)],
        compiler_params=_params(1),
        name=f"proj_l{l}",
    )(*xs, mod, g1, w_in, qg, kg, cos_t, sin_t, bq, sgu_w, sgu_b, on)


ACC_ROWS = HEAD_DIM + BF16_ROWS
N_PAIRS = N_HEADS // 2
PAIRS_PER_KV = N_PAIRS // N_KV_HEADS


def _pair_queries(qt, kv_idx, tq):
    qf = qt.astype(F32)
    mine = lax.broadcasted_iota(jnp.int32, (2 * HEAD_DIM, tq), 0) // HEAD_DIM == kv_idx
    qe = qf[:HEAD_DIM]
    qo = qf[HEAD_DIM:]
    rhs = jnp.concatenate(
        [jnp.where(mine, jnp.concatenate([qe, qe], axis=0), 0.0),
         jnp.where(mine, jnp.concatenate([qo, qo], axis=0), 0.0)], axis=1)
    return rhs.astype(BF16)


def _with_ones(vt_c):
    return jnp.concatenate([vt_c, jnp.ones((BF16_ROWS, vt_c.shape[1]), BF16)], axis=0)


def _flash_pairs(rhs_list, chunk_lists, tq):
    n_pairs, n_chunks = len(rhs_list), len(chunk_lists[0])
    scores = lambda s, c: jnp.dot(chunk_lists[s][c][0](), rhs_list[s], preferred_element_type=F32)
    m = [jnp.full((1, 2 * tq), NEG_BIG, F32)] * n_pairs
    acc = [jnp.zeros((ACC_ROWS, 2 * tq), F32)] * n_pairs
    st = [scores(s, 0) for s in range(n_pairs)]
    for c in range(n_chunks):
        for s in range(n_pairs):
            m_new = jnp.maximum(m[s], jnp.max(st[s], axis=0, keepdims=True))
            st_next = scores(s, c + 1) if c + 1 < n_chunks else None
            alpha = jnp.exp2(m[s] - m_new)
            p = jnp.exp2(st[s] - m_new).astype(BF16)
            acc[s] = alpha * acc[s] + jnp.dot(chunk_lists[s][c][1](), p, preferred_element_type=F32)
            m[s] = m_new
            st[s] = st_next
    outs = []
    for s in range(n_pairs):
        o = acc[s][:HEAD_DIM] * (1.0 / acc[s][HEAD_DIM:HEAD_DIM + 1])
        outs.append(jnp.concatenate([o[:, :tq], o[:, tq:]], axis=0))
    return outs


def _attn_ctx_kernel(qt_ref, k_ref, vt_ref, o_ref):
    tq = SEQ
    rhs_list, chunk_lists = [], []
    for pair in range(N_PAIRS):
        kv_idx = pair // PAIRS_PER_KV
        rhs_list.append(_pair_queries(qt_ref[pair * LANES:(pair + 1) * LANES, :], kv_idx, tq))
        chunk_lists.append([(lambda: k_ref[...],
                             lambda kv_idx=kv_idx: _with_ones(vt_ref[kv_idx * HEAD_DIM:(kv_idx + 1) * HEAD_DIM, :]))])
    for pair, o_t in enumerate(_flash_pairs(rhs_list, chunk_lists, tq)):
        o_ref[:, pair * LANES:(pair + 1) * LANES] = o_t.T.astype(BF16)


def _attn_ctx(l, qt, k, vt):
    halves = TM // SEQ
    return pl.pallas_call(
        _attn_ctx_kernel,
        grid=(BATCH,),
        in_specs=[
            pl.BlockSpec((None, D_ATTN, SEQ), lambda b: (b // halves, 0, b % halves)),
            pl.BlockSpec((SEQ, D_KV), lambda b: (b, 0)),
            pl.BlockSpec((None, D_KV, SEQ), lambda b: (b // halves, 0, b % halves)),
        ],
        out_specs=pl.BlockSpec((SEQ, D_ATTN), lambda b: (b, 0)),
        out_shape=jax.ShapeDtypeStruct((T_CTX, D_ATTN), BF16),
        compiler_params=_params(1),
        name=f"attn_ctx_l{l}",
    )(qt, k, vt)


KC = 512
LAT_PAIRS_PER_STEP = 2


def _attn_lat_kernel(qt_ref, k_ref, vt_ref, ck_ref, cv_ref, w1_ref, w3_ref, w2_ref,
                     o_ref, w1b_ref, w3b_ref, w2b_ref):
    w1b_ref[...] = w1_ref[...].astype(BF16)
    w3b_ref[...] = w3_ref[...].astype(BF16)
    w2b_ref[...] = w2_ref[...].astype(BF16)
    tq = TM
    cvt = cv_ref[...].T
    rhs_list, chunk_lists = [], []
    for j in range(LAT_PAIRS_PER_STEP):
        kv_idx = (pl.program_id(2) * LAT_PAIRS_PER_STEP + j) // PAIRS_PER_KV
        rhs_list.append(_pair_queries(qt_ref[j * LANES:(j + 1) * LANES, :], kv_idx, tq))
        v_row = pl.multiple_of(kv_idx * HEAD_DIM, HEAD_DIM)

        def latent_chunk(c, v_row=v_row):
            tile, lane0 = (c * KC) // TM, (c * KC) % TM
            return (lambda: k_ref[c * KC:(c + 1) * KC, :],
                    lambda: _with_ones(vt_ref[tile, pl.ds(v_row, HEAD_DIM), lane0:lane0 + KC]))

        def cached_chunk(c, kv_idx=kv_idx):
            return (lambda: ck_ref[c * KC:(c + 1) * KC, :].astype(BF16),
                    lambda: _with_ones(jnp.where(kv_idx == 0, cvt[:HEAD_DIM, c * KC:(c + 1) * KC],
                                                 cvt[HEAD_DIM:, c * KC:(c + 1) * KC]).astype(BF16)))

        chunks = [latent_chunk(c) for c in range(DEC_SEQ // KC)]
        chunks += [cached_chunk(c) for c in range(PAST_LEN // KC)]
        chunk_lists.append(chunks)
    for j, o_t in enumerate(_flash_pairs(rhs_list, chunk_lists, tq)):
        o_ref[:, j * LANES:(j + 1) * LANES] = o_t.T.astype(BF16)


def _attn_lat(l, qt, k, vt, cache_k, cache_v, moe_w1, moe_w3, moe_w2):
    n_q = DEC_SEQ // TM
    first = N_CTX_TILES
    ctx_blocks = T_CTX // DEC_SEQ
    width = LAT_PAIRS_PER_STEP * LANES
    n_p = N_PAIRS // LAT_PAIRS_PER_STEP
    per_expert = DEC_BATCH * n_q * n_p // N_EXPERTS
    step = lambda b, i, p: (b * n_q + i) * n_p + p
    w_in_spec = lambda rows, cols: pl.BlockSpec(
        (None, None, rows // per_expert, cols),
        lambda b, i, p: (l, step(b, i, p) // per_expert, step(b, i, p) % per_expert, 0))
    w_out_spec = lambda rows, cols: pl.BlockSpec(
        (None, rows // per_expert, cols),
        lambda b, i, p: (step(b, i, p) // per_expert, step(b, i, p) % per_expert, 0))
    return pl.pallas_call(
        _attn_lat_kernel,
        grid=(DEC_BATCH, n_q, n_p),
        in_specs=[
            pl.BlockSpec((None, width, TM), lambda b, i, p: (first + b * n_q + i, p, 0)),
            pl.BlockSpec((DEC_SEQ, D_KV), lambda b, i, p: (ctx_blocks + b, 0)),
            pl.BlockSpec((n_q, D_KV, TM), lambda b, i, p: (ctx_blocks + b, 0, 0)),
            pl.BlockSpec((None, None, PAST_LEN, D_KV), lambda b, i, p: (b, l, 0, 0)),
            pl.BlockSpec((None, None, PAST_LEN, D_KV), lambda b, i, p: (b, l, 0, 0)),
            w_in_spec(D_MODEL, D_EXPERT),
            w_in_spec(D_MODEL, D_EXPERT),
            w_in_spec(D_EXPERT, D_MODEL),
        ],
        out_specs=[
            pl.BlockSpec((TM, width), lambda b, i, p: (b * n_q + i, p)),
            w_out_spec(D_MODEL, D_EXPERT),
            w_out_spec(D_MODEL, D_EXPERT),
            w_out_spec(D_EXPERT, D_MODEL),
        ],
        out_shape=[
            jax.ShapeDtypeStruct((T_LAT, D_ATTN), BF16),
            jax.ShapeDtypeStruct((N_EXPERTS, D_MODEL, D_EXPERT), BF16),
            jax.ShapeDtypeStruct((N_EXPERTS, D_MODEL, D_EXPERT), BF16),
            jax.ShapeDtypeStruct((N_EXPERTS, D_EXPERT, D_MODEL), BF16),
        ],
        compiler_params=_params(3),
        name=f"attn_lat_l{l}",
    )(qt, k, vt, cache_k, cache_v, moe_w1, moe_w3, moe_w2)


def _first_max_index(vals, valid, rowf):
    masked = jnp.where(valid, vals, -jnp.inf)
    mx = jnp.max(masked, axis=0, keepdims=True)
    idx = jnp.min(jnp.where(masked == mx, rowf, float(SUBLANES)), axis=0, keepdims=True)
    return mx, idx


def _merge_kernel(*refs, split_x):
    n_x = 2 if split_x else 1
    x_refs, refs = refs[:n_x], refs[n_x:]
    (actx_ref, alat_ref, sgu_ref, cbz_ref, prev_ref, next_ref, cw_ref, on_ref,
     wo_ref, mod_ref, g2_ref, wr_ref, wrhi_ref, rb_ref, x1_ref, h2_ref, route_ref, gate4_ref, cnt_ref) = refs
    cur = pl.program_id(0)
    is_ctx = cur < N_CTX_TILES
    mrow = _mod_row_of_tile(cur)
    on = on_ref[...]

    cbz = cbz_ref[...]
    cb = cbz[:, :D_CONV]
    z = cbz[:, D_CONV:]
    seq_mask = jnp.where(is_ctx, SEQ - 1, DEC_SEQ - 1)
    row = lax.broadcasted_iota(jnp.int32, (TM, D_CONV), 0)
    pos = (cur * TM + row) & seq_mask
    z_prev = jnp.where(row == 0, prev_ref[SUBLANES - 1:SUBLANES, D_CONV:], pltpu.roll(z, 1, 0))
    z_next = jnp.where(row == TM - 1, next_ref[0:1, D_CONV:], pltpu.roll(z, TM - 1, 0))
    z_prev = jnp.where(pos == 0, 0.0, z_prev)
    z_next = jnp.where(pos == seq_mask, 0.0, z_next)
    cw = cw_ref[...]
    conv = cb * (z_prev * cw[0:1, :] + z * cw[1:2, :] + z_next * cw[2:3, :])
    conv_n = _rms(conv, on[:, D_ATTN + D_SGU:]).astype(BF16)

    attn = jnp.where(is_ctx, actx_ref[...].astype(F32), alat_ref[...].astype(F32))
    attn_n = _rms(attn, on[:, :D_ATTN]).astype(BF16)
    merged = jnp.concatenate([attn_n, sgu_ref[...], conv_n], axis=1)
    x = jnp.where(is_ctx, x_refs[0][...], x_refs[1][...]) if split_x else x_refs[0][...]
    x1 = x + _mod_part(mod_ref, 2, mrow) * jnp.dot(merged, wo_ref[...], preferred_element_type=F32)
    x1_ref[...] = x1
    _route_tile(x1, mrow, mod_ref, g2_ref, wr_ref, wrhi_ref, rb_ref, h2_ref, route_ref, gate4_ref, cnt_ref)


def _route_tile(x1, mrow, mod_ref, g2_ref, wr_ref, wrhi_ref, rb_ref, h2_ref, route_ref, gate4_ref, cnt_ref):
    h2 = _rms(x1, g2_ref[...] * (1.0 + _mod_part(mod_ref, 4, mrow))) + _mod_part(mod_ref, 3, mrow)
    hb = h2.astype(BF16)
    h2_ref[...] = hb

    h_lo = (h2 - hb.astype(F32)).astype(BF16)
    a = jnp.dot(hb, wr_ref[...], preferred_element_type=F32)
    b = jnp.dot(h_lo, wrhi_ref[...], preferred_element_type=F32)
    logits = a[:, :ROUTER_COLS] + a[:, ROUTER_COLS:] + b + rb_ref[...]
    lt = logits.T

    rowi = lax.broadcasted_iota(jnp.int32, (SUBLANES, TM), 0).astype(F32)
    valid = rowi < float(N_EXPERT_GROUPS)
    g_log = lt[0:SUBLANES]
    g_max, g_idx = _first_max_index(g_log, valid, rowi)
    p_g = 1.0 / jnp.sum(jnp.where(valid, jnp.exp(g_log - g_max), 0.0), axis=0, keepdims=True)
    e_log = jnp.zeros((SUBLANES, TM), F32)
    for g in range(N_EXPERT_GROUPS):
        e_log = jnp.where(g_idx == g, lt[(g + 1) * SUBLANES:(g + 2) * SUBLANES], e_log)
    e_max, i1 = _first_max_index(e_log, valid, rowi)
    e_exp = jnp.where(valid, jnp.exp(e_log - e_max), 0.0)
    e_prob = e_exp / jnp.sum(e_exp, axis=0, keepdims=True)
    v1 = jnp.max(e_prob, axis=0, keepdims=True)
    rest = jnp.logical_and(valid, rowi != i1)
    v2, i2 = _first_max_index(e_prob, rest, rowi)
    denom = v1 + v2
    w_sel = p_g * jnp.where(rowi == i1, v1 / denom, jnp.where(rowi == i2, v2 / denom, 0.0))
    route_ref[...] = jnp.broadcast_to(g_idx, (SUBLANES, TM))
    pad = jnp.zeros((LANES - SUBLANES, TM), F32)
    gate4_ref[...] = jnp.concatenate([w_sel, pad], axis=0).T
    counts = jnp.sum(jnp.where(rowi == g_idx, 1.0, 0.0), axis=1, keepdims=True)
    cnt_ref[...] = jnp.broadcast_to(counts, (SUBLANES, LANES)).astype(jnp.int32)


def _merge(l, xs, attn_ctx, attn_lat, sgu_n, cbz, conv_w, on, w_out, mod, g2, wr_cat, wr_hi, rb):
    lay = lambda *rest: (lambda i: (l,) + rest)
    rows8 = TM // SUBLANES
    return pl.pallas_call(
        functools.partial(_merge_kernel, split_x=len(xs) == 2),
        grid=(N_TILES,),
        in_specs=_x_specs(xs) + [
            pl.BlockSpec((TM, D_ATTN), lambda i: (_ctx_tile(i), 0)),
            pl.BlockSpec((TM, D_ATTN), lambda i: (_lat_tile(i), 0)),
            pl.BlockSpec((TM, D_SGU), lambda i: (i, 0)),
            pl.BlockSpec((TM, 2 * D_CONV), lambda i: (i, 0)),
            pl.BlockSpec((SUBLANES, 2 * D_CONV), lambda i: (jnp.maximum(i * rows8 - 1, 0), 0)),
            pl.BlockSpec((SUBLANES, 2 * D_CONV),
                         lambda i: (jnp.minimum((i + 1) * rows8, T_ALL // SUBLANES - 1), 0)),
            pl.BlockSpec((None, 3, D_CONV), lay(0, 0)),
            pl.BlockSpec((None, 1, D_MIX), lay(0, 0)),
            pl.BlockSpec((None, D_MIX, D_MODEL), lay(0, 0)),
            _MOD_SPEC(l, 1),
            pl.BlockSpec((None, 1, D_MODEL), lay(0, 0)),
            pl.BlockSpec((None, D_MODEL, 2 * ROUTER_COLS), lay(0, 0)),
            pl.BlockSpec((None, D_MODEL, ROUTER_COLS), lay(0, 0)),
            pl.BlockSpec((None, 1, ROUTER_COLS), lay(0, 0)),
        ],
        out_specs=[
            pl.BlockSpec((TM, D_MODEL), lambda i: (i, 0)),
            pl.BlockSpec((TM, D_MODEL), lambda i: (i, 0)),
            pl.BlockSpec((None, SUBLANES, TM), lambda i: (i, 0, 0)),
            pl.BlockSpec((TM, LANES), lambda i: (i, 0)),
            pl.BlockSpec((None, SUBLANES, LANES), lambda i: (i, 0, 0)),
        ],
        out_shape=[
            jax.ShapeDtypeStruct((T_ALL, D_MODEL), F32),
            jax.ShapeDtypeStruct((T_ALL, D_MODEL), BF16),
            jax.ShapeDtypeStruct((N_TILES, SUBLANES, TM), F32),
            jax.ShapeDtypeStruct((T_ALL, LANES), F32),
            jax.ShapeDtypeStruct((N_TILES, SUBLANES, LANES), jnp.int32),
        ],
        compiler_params=_params(1),
        name=f"merge_l{l}",
    )(*xs, attn_ctx, attn_lat, sgu_n, cbz, cbz, cbz, conv_w, on, w_out, mod, g2, wr_cat, wr_hi, rb)


RB = 144
SORT_ROWS = TM + LANES
SORT_ROWS_PAD = -(-(SORT_ROWS + RB) // BF16_ROWS) * BF16_ROWS
MOE_VMEM_LIMIT = 56 * 1024 * 1024


def _split3(x):
    hi = x.astype(BF16)
    r1 = x - hi.astype(F32)
    mid = r1.astype(BF16)
    lo = (r1 - mid.astype(F32)).astype(BF16)
    return hi, mid, lo


def _moe_kernel(cnt_ref, h_ref, route_ref, gate4_ref, tri_ref, w1_ref, w3_ref, w2_ref, x1_ref, mod_ref, fn_ref,
                *refs, final):
    out_refs, (xs_ref, gs_ref, zs_ref) = refs[:-3], refs[-3:]
    i = pl.program_id(0)
    gate2 = _mod_part(mod_ref, 5, _mod_row_of_tile(i))
    counts = [cnt_ref[i, g] for g in range(N_EXPERT_GROUPS)]
    starts = [jnp.int32(0)]
    for g in range(N_EXPERT_GROUPS - 1):
        starts.append(starts[-1] + (counts[g] + (BF16_ROWS - 1)) // BF16_ROWS * BF16_ROWS)

    g_idx = route_ref[0:1, :]
    rowi = lax.broadcasted_iota(jnp.int32, (SUBLANES, TM), 0).astype(F32)
    onehot = jnp.where(rowi == g_idx, 1.0, 0.0)
    incl = jnp.dot(onehot.astype(BF16), tri_ref[...], preferred_element_type=F32)
    pos = jnp.sum(onehot * incl, axis=0, keepdims=True) - 1.0
    for g in range(1, N_EXPERT_GROUPS):
        pos = pos + jnp.where(g_idx == float(g), starts[g].astype(F32), 0.0)
    sub = lax.broadcasted_iota(jnp.int32, (SORT_ROWS, TM), 0).astype(F32)
    perm = jnp.where(sub == pos, 1.0, 0.0).astype(BF16)
    pos_col = jnp.broadcast_to(pos, (LANES, TM)).T
    lane = lax.broadcasted_iota(jnp.int32, (TM, SORT_ROWS), 1).astype(F32)
    perm_t = jnp.where(lane == jnp.concatenate([pos_col] * (SORT_ROWS // LANES), axis=1), 1.0, 0.0).astype(BF16)

    xs_ref[:SORT_ROWS, :] = jnp.dot(perm, h_ref[...], preferred_element_type=F32).astype(BF16)
    xs_ref[SORT_ROWS:, :] = jnp.zeros((SORT_ROWS_PAD - SORT_ROWS, D_MODEL), BF16)
    hi, mid, lo = _split3(gate4_ref[...])
    packed = (hi.astype(F32) + pltpu.roll(mid.astype(F32), EXPERTS_PER_GROUP, 1)
              + pltpu.roll(lo.astype(F32), 2 * EXPERTS_PER_GROUP, 1)).astype(BF16)
    gsp = jnp.dot(perm, packed, preferred_element_type=F32)
    gs_ref[:SORT_ROWS, :] = (gsp + pltpu.roll(gsp, LANES - EXPERTS_PER_GROUP, 1)
                             + pltpu.roll(gsp, LANES - 2 * EXPERTS_PER_GROUP, 1))
    gs_ref[SORT_ROWS:, :] = jnp.zeros((SORT_ROWS_PAD - SORT_ROWS, LANES), F32)
    zs_ref[...] = jnp.zeros((SORT_ROWS_PAD, D_MODEL), BF16)

    for g in range(N_EXPERT_GROUPS):
        def block_ffn(blk, carry, g=g):
            row0 = pl.multiple_of(starts[g] + blk * RB, BF16_ROWS)
            xb = xs_ref[pl.ds(row0, RB), :]
            gb = gs_ref[pl.ds(row0, RB), :]
            cols = []
            for e in range(EXPERTS_PER_GROUP):
                a = jnp.dot(xb, w1_ref[g * EXPERTS_PER_GROUP + e], preferred_element_type=F32)
                b = jnp.dot(xb, w3_ref[g * EXPERTS_PER_GROUP + e], preferred_element_type=F32)
                cols.append((a / (1.0 + jnp.exp(-a)) * b * gb[:, e:e + 1]).astype(BF16))
            z = jnp.dot(jnp.concatenate(cols, axis=1), w2_ref[g], preferred_element_type=F32)
            zs_ref[pl.ds(row0, RB), :] = z.astype(BF16)
            return carry

        lax.fori_loop(0, (counts[g] + (RB - 1)) // RB, block_ffn, 0)

    y = jnp.dot(perm_t, zs_ref[:SORT_ROWS, :], preferred_element_type=F32)
    out = x1_ref[...] + gate2 * y

    if final:
        octx_ref, olat_ref = out_refs
        out = _rms(out, fn_ref[...])
        olat_ref[...] = out

        @pl.when(i < N_CTX_TILES)
        def _():
            octx_ref[...] = out
    else:
        out_refs[0][...] = out


def _moe(l, counts, h2, route, gate4, tri, w1, w3, w2, x1, mod, final_norm, final):
    tile = pl.BlockSpec((TM, D_MODEL), lambda i, c: (i, 0))
    resident = pl.Buffered(1)
    if final:
        out_specs = [pl.BlockSpec((TM, D_MODEL), lambda i, c: (_ctx_tile(i), 0)),
                     pl.BlockSpec((TM, D_MODEL), lambda i, c: (_lat_tile(i), 0))]
        out_shape = [jax.ShapeDtypeStruct((T_CTX, D_MODEL), F32), jax.ShapeDtypeStruct((T_LAT, D_MODEL), F32)]
    else:
        out_specs = tile
        out_shape = jax.ShapeDtypeStruct((T_ALL, D_MODEL), F32)
    grid_spec = pltpu.PrefetchScalarGridSpec(
        num_scalar_prefetch=1,
        grid=(N_TILES,),
        in_specs=[
            tile,
            pl.BlockSpec((None, SUBLANES, TM), lambda i, c: (i, 0, 0)),
            pl.BlockSpec((TM, LANES), lambda i, c: (i, 0)),
            pl.BlockSpec((TM, TM), lambda i, c: (0, 0), pipeline_mode=resident),
            pl.BlockSpec((N_EXPERTS, D_MODEL, D_EXPERT), lambda i, c: (0, 0, 0), pipeline_mode=resident),
            pl.BlockSpec((N_EXPERTS, D_MODEL, D_EXPERT), lambda i, c: (0, 0, 0), pipeline_mode=resident),
            pl.BlockSpec((N_EXPERT_GROUPS, D_GROUP_HID, D_MODEL), lambda i, c: (0, 0, 0), pipeline_mode=resident),
            tile,
            pl.BlockSpec((None, N_MOD_PARTS, N_MOD_ROWS, D_MODEL), lambda i, c: (l, 0, 0, 0)),
            pl.BlockSpec((1, D_MODEL), lambda i, c: (0, 0)),
        ],
        out_specs=out_specs,
        scratch_shapes=[pltpu.VMEM((SORT_ROWS_PAD, D_MODEL), BF16),
                        pltpu.VMEM((SORT_ROWS_PAD, LANES), F32),
                        pltpu.VMEM((SORT_ROWS_PAD, D_MODEL), BF16)],
    )
    return pl.pallas_call(
        functools.partial(_moe_kernel, final=final),
        grid_spec=grid_spec,
        out_shape=out_shape,
        compiler_params=pltpu.CompilerParams(dimension_semantics=("arbitrary",),
                                             vmem_limit_bytes=MOE_VMEM_LIMIT),
        name=f"moe_l{l}",
    )(counts, h2, route, gate4, tri, w1, w3, w2, x1, mod, final_norm)


def _rope_tables():
    rows = DEC_SEQ // GRID_W
    row = np.repeat(np.arange(rows, dtype=np.float32), GRID_W)
    col = np.tile(np.arange(GRID_W, dtype=np.float32), rows)
    half = ROPE_AXIS_DIM // 2
    inv_freq = (1.0 / (np.float32(ROPE_THETA) ** (np.arange(half, dtype=np.float32) * np.float32(2.0)
                                                  / np.float32(ROPE_AXIS_DIM)))).astype(np.float32)
    ar = row[:, None] * inv_freq
    ac = col[:, None] * inv_freq
    cos64 = np.concatenate([np.cos(ar), np.cos(ar), np.cos(ac), np.cos(ac)], axis=1)
    sin64 = np.concatenate([-np.sin(ar), np.sin(ar), -np.sin(ac), np.sin(ac)], axis=1)
    cos_t = np.concatenate([np.ones((TM, LANES), np.float32), np.tile(cos64, (1, 2))], axis=0)
    sin_t = np.concatenate([np.zeros((TM, LANES), np.float32), np.tile(sin64, (1, 2))], axis=0)
    return jnp.asarray(cos_t, F32), jnp.asarray(sin_t, F32)


def _router_weights(router_g_w, router_g_b, router_e_w, router_e_b):
    zw = jnp.zeros((DEPTH, D_MODEL, SUBLANES - N_EXPERT_GROUPS), F32)
    zb = jnp.zeros((DEPTH, SUBLANES - N_EXPERT_GROUPS), F32)
    w_cols, b_cols = [router_g_w, zw], [router_g_b, zb]
    for g in range(N_EXPERT_GROUPS):
        w_cols += [router_e_w[:, g], zw]
        b_cols += [router_e_b[:, g], zb]
    used = (1 + N_EXPERT_GROUPS) * SUBLANES
    w_cols.append(jnp.zeros((DEPTH, D_MODEL, ROUTER_COLS - used), F32))
    b_cols.append(jnp.zeros((DEPTH, ROUTER_COLS - used), F32))
    w = jnp.concatenate(w_cols, axis=2)
    b = jnp.concatenate(b_cols, axis=1).reshape(DEPTH, 1, ROUTER_COLS)
    w_hi = w.astype(BF16)
    w_lo = (w - w_hi.astype(F32)).astype(BF16)
    return jnp.concatenate([w_hi, w_lo], axis=2), w_hi, b


def kernel(x_prompt, x_sample, cache_k, cache_v, c, c_ctx, w_ada, b_ada, norm1, w_in, q_norm, k_norm,
           sgu_w, sgu_b, conv_w, out_norm, w_out, norm2, router_g_w, router_g_b, router_e_w,
           router_e_b, moe_w1, moe_w3, moe_w2, final_norm):
    xs = (x_prompt.reshape(T_CTX, D_MODEL), x_sample.reshape(T_LAT, D_MODEL))

    cvecs = jnp.concatenate([c_ctx[None, :], c, jnp.zeros((N_MOD_ROWS - 1 - DEC_BATCH, D_MODEL), F32)], axis=0)
    mod = _modulation(cvecs.T, w_ada, b_ada)

    w_in_b = w_in.astype(BF16)
    w_out_b = w_out.astype(BF16)
    g1 = norm1.reshape(DEPTH, 1, D_MODEL)
    g2 = norm2.reshape(DEPTH, 1, D_MODEL)
    on = out_norm.reshape(DEPTH, 1, D_MIX)
    qg = jnp.tile(q_norm, (1, N_HEADS)).reshape(DEPTH, 1, D_ATTN)
    kg = jnp.tile(k_norm, (1, N_KV_HEADS)).reshape(DEPTH, 1, D_KV)
    cos_t, sin_t = _rope_tables()
    bq = jnp.asarray(np.kron(np.eye(N_HEADS), np.full((HEAD_DIM, HEAD_DIM), 1.0 / HEAD_DIM)), BF16)
    sgu_w_pairs = sgu_w.astype(BF16).reshape(DEPTH, N_SGU_GROUPS // 2, 2, CHUNK, CHUNK)
    sgu_w_pairs = jnp.concatenate([sgu_w_pairs[:, :, 0], sgu_w_pairs[:, :, 1]], axis=-1)
    sgu_bias = jnp.repeat(jnp.swapaxes(sgu_b, 1, 2), SGU_GROUP_DIM, axis=2)
    wr_cat, wr_hi, rb = _router_weights(router_g_w, router_g_b, router_e_w, router_e_b)
    fn = final_norm.reshape(1, D_MODEL)
    tri = jnp.asarray(np.triu(np.ones((TM, TM), np.float32)), BF16)
    ck = cache_k.reshape(DEC_BATCH, DEPTH, PAST_LEN, D_KV)
    cv = cache_v.reshape(DEC_BATCH, DEPTH, PAST_LEN, D_KV)

    ctx_ks, ctx_vs = [], []
    for l in range(DEPTH):
        qt, k, vt, k32, v32, sgu_n, cbz = _proj(l, xs, mod, g1, w_in_b, qg, kg, cos_t, sin_t, bq,
                                                sgu_w_pairs, sgu_bias, on)
        attn_ctx = _attn_ctx(l, qt, k, vt)
        attn_lat, w1, w3, w2 = _attn_lat(l, qt, k, vt, ck, cv, moe_w1, moe_w3, moe_w2)
        w2 = w2.reshape(N_EXPERT_GROUPS, D_GROUP_HID, D_MODEL)
        x1, h2, route, gate4, cnt = _merge(l, xs, attn_ctx, attn_lat, sgu_n, cbz, conv_w, on, w_out_b, mod, g2,
                                           wr_cat, wr_hi, rb)
        counts = cnt[:, :N_EXPERT_GROUPS, 0]
        out = _moe(l, counts, h2, route, gate4, tri, w1, w3, w2, x1, mod, fn, final=(l == DEPTH - 1))
        xs = (out,)
        ctx_ks.append(k32.reshape(BATCH, SEQ, N_KV_HEADS, HEAD_DIM))
        ctx_vs.append(v32.reshape(BATCH, SEQ, N_KV_HEADS, HEAD_DIM))

    y_prompt, y_sample = out
    return (y_prompt.reshape(BATCH, SEQ, D_MODEL), y_sample.reshape(DEC_BATCH, DEC_SEQ, D_MODEL),
            jnp.stack(ctx_ks, axis=1), jnp.stack(ctx_vs, axis=1))
```

```python
import functools
import math

import jax
import jax.numpy as jnp
import numpy as np
from jax import lax
from jax.experimental import pallas as pl
from jax.experimental.pallas import tpu as pltpu

D_MODEL = 1024
BATCH = 16
SEQ = 256
DEPTH = 2
DEC_BATCH = 2
DEC_SEQ = 4096
PAST_LEN = 512
GRID_W = 64
N_HEADS = 8
N_KV_HEADS = 2
HEAD_DIM = 64
ROPE_AXIS_DIM = HEAD_DIM // 2
ROPE_THETA = 10000.0
D_ATTN = N_HEADS * HEAD_DIM
D_KV = N_KV_HEADS * HEAD_DIM
CHUNK = 128
N_SGU_GROUPS = 4
SGU_GROUP_DIM = 64
D_SGU = N_SGU_GROUPS * SGU_GROUP_DIM
D_CONV = 256
D_MIX = D_ATTN + D_SGU + D_CONV
D_IN = D_ATTN + 2 * D_KV + 2 * D_SGU + 3 * D_CONV
N_EXPERT_GROUPS = 4
EXPERTS_PER_GROUP = 4
N_EXPERTS = N_EXPERT_GROUPS * EXPERTS_PER_GROUP
D_EXPERT = 256
D_GROUP_HID = EXPERTS_PER_GROUP * D_EXPERT
EPS = 1e-6

T_CTX = BATCH * SEQ
T_LAT = DEC_BATCH * DEC_SEQ
T_ALL = T_CTX + T_LAT
TM = 512
N_TILES = T_ALL // TM
N_CTX_TILES = T_CTX // TM
LAT_TILES_PER_BATCH = DEC_SEQ // TM
TM_MOE = 1024
LANES = 128
SUBLANES = 8
BF16_ROWS = 16
N_MOD_ROWS = 8
N_MOD_PARTS = 6
ROUTER_COLS = 128
VMEM_LIMIT = 48 * 1024 * 1024

OFF_Q = 0
OFF_KV = D_ATTN
OFF_SGU = OFF_KV + 2 * D_KV
OFF_CONV = OFF_SGU + 2 * D_SGU

BF16 = jnp.bfloat16
F32 = jnp.float32
NEG_BIG = -1e30
Q_SCALE = HEAD_DIM ** -0.5 * math.log2(math.e)


def _params(n_grid_dims):
    return pltpu.CompilerParams(
        dimension_semantics=("arbitrary",) * n_grid_dims,
        vmem_limit_bytes=VMEM_LIMIT,
    )


def _mod_row_of_tile(i, tile=TM):
    n_ctx = T_CTX // tile
    return jnp.where(i < n_ctx, 0, 1 + (i - n_ctx) // (DEC_SEQ // tile))


def _mod_part(mod_ref, part, row):
    return mod_ref[part, pl.ds(row, 1), :]


def _rope_block_of_tile(i):
    return jnp.where(i < N_CTX_TILES, 0, 1 + (i - N_CTX_TILES) % LAT_TILES_PER_BATCH)


def _rms(x, gain):
    ms = jnp.mean(x * x, axis=-1, keepdims=True)
    return x * lax.rsqrt(ms + EPS) * gain


def _ctx_tile(i):
    return jnp.minimum(i, N_CTX_TILES - 1)


def _lat_tile(i):
    return jnp.maximum(i - N_CTX_TILES, 0)


MOD_PARTS_PER_STEP = 2
MOD_TN = MOD_PARTS_PER_STEP * D_MODEL


def _mod_kernel(ct_ref, w_ref, b_ref, o_ref):
    c = ct_ref[...]
    s = c / (1.0 + jnp.exp(-c))
    row = lax.broadcasted_iota(jnp.int32, (N_MOD_ROWS, D_MODEL), 0)
    for part in range(MOD_PARTS_PER_STEP):
        cols = slice(part * D_MODEL, (part + 1) * D_MODEL)
        w = w_ref[:, cols]
        out = jnp.zeros((N_MOD_ROWS, D_MODEL), F32)
        for r in range(1 + DEC_BATCH):
            acc = jnp.sum(w * s[:, r:r + 1], axis=0, keepdims=True)
            out = jnp.where(row == r, acc, out)
        o_ref[part] = out + b_ref[:, cols]


def _modulation(cvecs_t, w_ada, b_ada):
    n_col = N_MOD_PARTS * D_MODEL
    return pl.pallas_call(
        _mod_kernel,
        grid=(DEPTH, n_col // MOD_TN),
        in_specs=[
            pl.BlockSpec((D_MODEL, N_MOD_ROWS), lambda l, j: (0, 0)),
            pl.BlockSpec((None, D_MODEL, MOD_TN), lambda l, j: (l, 0, j)),
            pl.BlockSpec((None, 1, MOD_TN), lambda l, j: (l, 0, j)),
        ],
        out_specs=pl.BlockSpec((None, MOD_PARTS_PER_STEP, N_MOD_ROWS, D_MODEL), lambda l, j: (l, j, 0, 0)),
        out_shape=jax.ShapeDtypeStruct((DEPTH, N_MOD_PARTS, N_MOD_ROWS, D_MODEL), F32),
        compiler_params=_params(2),
        name="adaln_modulation",
    )(cvecs_t, w_ada, b_ada.reshape(DEPTH, 1, n_col))


_MOD_SPEC = lambda l, nd: pl.BlockSpec(
    (None, N_MOD_PARTS, N_MOD_ROWS, D_MODEL), lambda *idx: (l, 0, 0, 0))


def _swap16(x):
    lane = lax.broadcasted_iota(jnp.int32, x.shape, 1)
    up = pltpu.roll(x, 16, 1)
    down = pltpu.roll(x, LANES - 16, 1)
    return jnp.where((lane & 16) != 0, up, down)


def _rope(x, cos_t, sin_t):
    cols = []
    for j in range(x.shape[1] // LANES):
        xc = x[:, j * LANES:(j + 1) * LANES]
        cols.append(xc * cos_t + _swap16(xc) * sin_t)
    return cols[0] if len(cols) == 1 else jnp.concatenate(cols, axis=1)


def _proj_kernel(*refs, split_x):
    n_x = 2 if split_x else 1
    x_refs, refs, (scr_even, scr_odd) = refs[:n_x], refs[n_x:-2], refs[-2:]
    step = functools.partial(_proj_step, x_refs, refs, split_x)
    i = pl.program_id(0)

    @pl.when(i == 0)
    def _():
        scr_odd[...] = jnp.zeros((TM, D_IN), F32)

    @pl.when(i % 2 == 0)
    def _():
        step(scr_even, scr_odd)

    @pl.when(i % 2 == 1)
    def _():
        step(scr_odd, scr_even)


def _proj_step(x_refs, refs, split_x, new_ref, proj):
    (mod_ref, g1_ref, w_ref, qg_ref, kg_ref, cos_ref, sin_ref, bq_ref, sw_ref, sb_ref, on_ref,
     qt_ref, k_ref, vt_ref, k32_ref, v32_ref, sgu_ref, cbz_ref) = refs
    i = pl.program_id(0)
    cur = _stage1_tile(i)

    is_ctx = i <= N_CTX_TILES
    cos_t = cos_ref[...]
    sin_t = sin_ref[...]

    q = proj[:, OFF_Q:OFF_Q + D_ATTN]
    qms = jnp.dot((q * q).astype(BF16), bq_ref[...], preferred_element_type=F32)
    qn = q * lax.rsqrt(qms + EPS) * qg_ref[...]
    qr = _rope(qn, cos_t, sin_t) * Q_SCALE
    qt_ref[...] = qr.T.astype(BF16)

    k = proj[:, OFF_KV:OFF_KV + D_KV]
    v = proj[:, OFF_KV + D_KV:OFF_KV + 2 * D_KV]
    kms = jnp.dot((k * k).astype(BF16), bq_ref[:D_KV, :D_KV], preferred_element_type=F32)
    kn = k * lax.rsqrt(kms + EPS) * kg_ref[...]
    k_ref[...] = _rope(kn, cos_t, sin_t).astype(BF16)
    vt_ref[...] = v.T.astype(BF16)

    su = proj[:, OFF_SGU:OFF_SGU + D_SGU]
    sv = proj[:, OFF_SGU + D_SGU:OFF_SGU + 2 * D_SGU]
    n_chunks = TM // CHUNK
    sv_wide = jnp.concatenate([sv[n * CHUNK:(n + 1) * CHUNK, :] for n in range(n_chunks)], axis=1)
    grp = lax.broadcasted_iota(jnp.int32, (CHUNK, n_chunks * D_SGU), 1) // SGU_GROUP_DIM % N_SGU_GROUPS
    mixed_wide = jnp.zeros((CHUNK, n_chunks * D_SGU), F32)
    for pair in range(N_SGU_GROUPS // 2):
        rhs = jnp.concatenate(
            [jnp.where(grp == 2 * pair, sv_wide, 0.0), jnp.where(grp == 2 * pair + 1, sv_wide, 0.0)],
            axis=0).astype(BF16)
        mixed_wide = mixed_wide + jnp.dot(sw_ref[pair], rhs, preferred_element_type=F32)
    on_sgu = on_ref[:, D_ATTN:D_ATTN + D_SGU]
    for n in range(n_chunks):
        sgu = su[n * CHUNK:(n + 1) * CHUNK, :] * (mixed_wide[:, n * D_SGU:(n + 1) * D_SGU] + sb_ref[...])
        sgu_ref[n * CHUNK:(n + 1) * CHUNK, :] = _rms(sgu, on_sgu).astype(BF16)

    cbz_ref[:, :D_CONV] = proj[:, OFF_CONV:OFF_CONV + D_CONV]
    cbz_ref[:, D_CONV:] = (proj[:, OFF_CONV + D_CONV:OFF_CONV + 2 * D_CONV]
                           * proj[:, OFF_CONV + 2 * D_CONV:OFF_CONV + 3 * D_CONV])

    x = jnp.where(cur < N_CTX_TILES, x_refs[0][...], x_refs[1][...]) if split_x else x_refs[0][...]
    mrow = _mod_row_of_tile(cur)
    h = _rms(x, g1_ref[...] * (1.0 + _mod_part(mod_ref, 1, mrow))) + _mod_part(mod_ref, 0, mrow)
    new_ref[...] = jnp.dot(h.astype(BF16), w_ref[...], preferred_element_type=F32)

    @pl.when(is_ctx)
    def _():
        k32_ref[...] = kn
        v32_ref[...] = v


def _x_specs(xs, tile_of_step=lambda i: i):
    if len(xs) == 1:
        return [pl.BlockSpec((TM, D_MODEL), lambda i: (tile_of_step(i), 0))]
    return [pl.BlockSpec((TM, D_MODEL), lambda i: (_ctx_tile(tile_of_step(i)), 0)),
            pl.BlockSpec((TM, D_MODEL), lambda i: (_lat_tile(tile_of_step(i)), 0))]


def _stage1_tile(i):
    return jnp.minimum(i, N_TILES - 1)


def _stage2_tile(i):
    return jnp.maximum(i - 1, 0)


def _proj(l, xs, mod, g1, w_in, qg, kg, cos_t, sin_t, bq, sgu_w, sgu_b, on):
    lay = lambda *rest: (lambda i: (l,) + rest)
    s2 = _stage2_tile
    return pl.pallas_call(
        functools.partial(_proj_kernel, split_x=len(xs) == 2),
        grid=(N_TILES + 1,),
        in_specs=_x_specs(xs, _stage1_tile) + [
            _MOD_SPEC(l, 1),
            pl.BlockSpec((None, 1, D_MODEL), lay(0, 0)),
            pl.BlockSpec((None, D_MODEL, D_IN), lay(0, 0)),
            pl.BlockSpec((None, 1, D_ATTN), lay(0, 0)),
            pl.BlockSpec((None, 1, D_KV), lay(0, 0)),
            pl.BlockSpec((TM, LANES), lambda i: (_rope_block_of_tile(s2(i)), 0)),
            pl.BlockSpec((TM, LANES), lambda i: (_rope_block_of_tile(s2(i)), 0)),
            pl.BlockSpec((D_ATTN, D_ATTN), lambda i: (0, 0)),
            pl.BlockSpec((None, N_SGU_GROUPS // 2, CHUNK, 2 * CHUNK), lay(0, 0, 0)),
            pl.BlockSpec((None, CHUNK, D_SGU), lay(0, 0)),
            pl.BlockSpec((None, 1, D_MIX), lay(0, 0)),
        ],
        out_specs=[
            pl.BlockSpec((None, D_ATTN, TM), lambda i: (s2(i), 0, 0)),
            pl.BlockSpec((TM, D_KV), lambda i: (s2(i), 0)),
            pl.BlockSpec((None, D_KV, TM), lambda i: (s2(i), 0, 0)),
            pl.BlockSpec((TM, D_KV), lambda i: (_ctx_tile(s2(i)), 0)),
            pl.BlockSpec((TM, D_KV), lambda i: (_ctx_tile(s2(i)), 0)),
            pl.BlockSpec((TM, D_SGU), lambda i: (s2(i), 0)),
            pl.BlockSpec((TM, 2 * D_CONV), lambda i: (s2(i), 0)),
        ],
        out_shape=[
            jax.ShapeDtypeStruct((N_TILES, D_ATTN, TM), BF16),
            jax.ShapeDtypeStruct((T_ALL, D_KV), BF16),
            jax.ShapeDtypeStruct((N_TILES, D_KV, TM), BF16),
            jax.ShapeDtypeStruct((T_CTX, D_KV), F32),
            jax.ShapeDtypeStruct((T_CTX, D_KV), F32),
            jax.ShapeDtypeStruct((T_ALL, D_SGU), BF16),
            jax.ShapeDtypeStruct((T_ALL, 2 * D_CONV), F32),
        ],
        scratch_shapes=[pltpu.VMEM((TM, D_IN), F32), pltpu.VMEM((TM, D_IN), F32)],
        compiler_params=_params(1),
        name=f"proj_l{l}",
    )(*xs, mod, g1, w_in, qg, kg, cos_t, sin_t, bq, sgu_w, sgu_b, on)


ACC_ROWS = HEAD_DIM + BF16_ROWS
N_PAIRS = N_HEADS // 2
PAIRS_PER_KV = N_PAIRS // N_KV_HEADS


def _pair_queries(qt, kv_idx, tq):
    qf = qt.astype(F32)
    mine = lax.broadcasted_iota(jnp.int32, (2 * HEAD_DIM, tq), 0) // HEAD_DIM == kv_idx
    qe = qf[:HEAD_DIM]
    qo = qf[HEAD_DIM:]
    rhs = jnp.concatenate(
        [jnp.where(mine, jnp.concatenate([qe, qe], axis=0), 0.0),
         jnp.where(mine, jnp.concatenate([qo, qo], axis=0), 0.0)], axis=1)
    return rhs.astype(BF16)


def _with_ones(vt_c):
    return jnp.concatenate([vt_c, jnp.ones((BF16_ROWS, vt_c.shape[1]), BF16)], axis=0)


def _flash_pairs(rhs_list, chunk_lists, tq, shift=None):
    n_pairs, n_chunks = len(rhs_list), len(chunk_lists[0])
    scores = lambda s, c: jnp.dot(chunk_lists[s][c][0](), rhs_list[s], preferred_element_type=F32)
    m = [jnp.full((1, 2 * tq), NEG_BIG, F32)] * n_pairs
    acc = [jnp.zeros((ACC_ROWS, 2 * tq), F32)] * n_pairs
    st = [scores(s, 0) for s in range(n_pairs)]
    for c in range(n_chunks):
        for s in range(n_pairs):
            if shift is None:
                m_new = jnp.maximum(m[s], jnp.max(st[s], axis=0, keepdims=True))
            st_next = scores(s, c + 1) if c + 1 < n_chunks else None
            if shift is None:
                alpha = jnp.exp2(m[s] - m_new)
                p = jnp.exp2(st[s] - m_new).astype(BF16)
                acc[s] = alpha * acc[s] + jnp.dot(chunk_lists[s][c][1](), p, preferred_element_type=F32)
                m[s] = m_new
            else:
                p = jnp.exp2(st[s] - shift).astype(BF16)
                acc[s] = acc[s] + jnp.dot(chunk_lists[s][c][1](), p, preferred_element_type=F32)
            st[s] = st_next
    outs = []
    for s in range(n_pairs):
        o = acc[s][:HEAD_DIM] * (1.0 / acc[s][HEAD_DIM:HEAD_DIM + 1])
        outs.append(jnp.concatenate([o[:, :tq], o[:, tq:]], axis=0))
    return outs


def _attn_ctx_kernel(qt_ref, k_ref, vt_ref, o_ref):
    tq = SEQ
    rhs_list, chunk_lists = [], []
    for pair in range(N_PAIRS):
        kv_idx = pair // PAIRS_PER_KV
        rhs_list.append(_pair_queries(qt_ref[pair * LANES:(pair + 1) * LANES, :], kv_idx, tq))
        chunk_lists.append([(lambda: k_ref[...],
                             lambda kv_idx=kv_idx: _with_ones(vt_ref[kv_idx * HEAD_DIM:(kv_idx + 1) * HEAD_DIM, :]))])
    for pair, o_t in enumerate(_flash_pairs(rhs_list, chunk_lists, tq)):
        o_ref[:, pair * LANES:(pair + 1) * LANES] = o_t.T.astype(BF16)


def _attn_ctx(l, qt, k, vt):
    halves = TM // SEQ
    return pl.pallas_call(
        _attn_ctx_kernel,
        grid=(BATCH,),
        in_specs=[
            pl.BlockSpec((None, D_ATTN, SEQ), lambda b: (b // halves, 0, b % halves)),
            pl.BlockSpec((SEQ, D_KV), lambda b: (b, 0)),
            pl.BlockSpec((None, D_KV, SEQ), lambda b: (b // halves, 0, b % halves)),
        ],
        out_specs=pl.BlockSpec((SEQ, D_ATTN), lambda b: (b, 0)),
        out_shape=jax.ShapeDtypeStruct((T_CTX, D_ATTN), BF16),
        compiler_params=_params(1),
        name=f"attn_ctx_l{l}",
    )(qt, k, vt)


KC = 512
LAT_PAIRS_PER_STEP = 2


MAX_SHIFT = 60.0
BOUND_SLACK = 1.05


def _score_bound_sq(qg_ref, kg_ref, ck):
    qg = qg_ref[...]
    kg = kg_ref[...]
    gq2 = jnp.max(qg * qg, axis=1, keepdims=True)
    gk2 = jnp.max(kg * kg, axis=1, keepdims=True)
    sq = ck * ck
    first = lax.broadcasted_iota(jnp.int32, sq.shape, 1) < HEAD_DIM
    n0 = jnp.sum(jnp.where(first, sq, 0.0), axis=1, keepdims=True)
    n1 = jnp.sum(jnp.where(first, 0.0, sq), axis=1, keepdims=True)
    kc2 = jnp.max(jnp.maximum(n0, n1), axis=0, keepdims=True)
    return (BOUND_SLACK * Q_SCALE) ** 2 * HEAD_DIM * gq2 * jnp.maximum(HEAD_DIM * gk2, kc2)


def _attn_lat_kernel(qt_ref, k_ref, vt_ref, ck_ref, cv_ref, qg_ref, kg_ref, w1_ref, w3_ref, w2_ref,
                     o_ref, w1b_ref, w3b_ref, w2b_ref):
    w1b_ref[...] = w1_ref[...].astype(BF16)
    w3b_ref[...] = w3_ref[...].astype(BF16)
    w2b_ref[...] = w2_ref[...].astype(BF16)
    tq = TM
    cvt = cv_ref[...].T
    rhs_list, chunk_lists = [], []
    for j in range(LAT_PAIRS_PER_STEP):
        kv_idx = (pl.program_id(2) * LAT_PAIRS_PER_STEP + j) // PAIRS_PER_KV
        rhs_list.append(_pair_queries(qt_ref[j * LANES:(j + 1) * LANES, :], kv_idx, tq))
        v_row = pl.multiple_of(kv_idx * HEAD_DIM, HEAD_DIM)

        def latent_chunk(c, v_row=v_row):
            tile, lane0 = (c * KC) // TM, (c * KC) % TM
            return (lambda: k_ref[c * KC:(c + 1) * KC, :],
                    lambda: _with_ones(vt_ref[tile, pl.ds(v_row, HEAD_DIM), lane0:lane0 + KC]))

        def cached_chunk(c, kv_idx=kv_idx):
            return (lambda: ck_ref[c * KC:(c + 1) * KC, :].astype(BF16),
                    lambda: _with_ones(jnp.where(kv_idx == 0, cvt[:HEAD_DIM, c * KC:(c + 1) * KC],
                                                 cvt[HEAD_DIM:, c * KC:(c + 1) * KC]).astype(BF16)))

        chunks = [latent_chunk(c) for c in range(DEC_SEQ // KC)]
        chunks += [cached_chunk(c) for c in range(PAST_LEN // KC)]
        chunk_lists.append(chunks)
    def run(shift):
        for j, o_t in enumerate(_flash_pairs(rhs_list, chunk_lists, tq, shift)):
            o_ref[:, j * LANES:(j + 1) * LANES] = o_t.T.astype(BF16)

    bound_sq = _score_bound_sq(qg_ref, kg_ref, ck_ref[...])
    small = bound_sq[0, 0] <= MAX_SHIFT ** 2

    @pl.when(small)
    def _():
        run(jnp.sqrt(bound_sq))

    @pl.when(jnp.logical_not(small))
    def _():
        run(None)


def _attn_lat(l, qt, k, vt, cache_k, cache_v, qg, kg, moe_w1, moe_w3, moe_w2):
    n_q = DEC_SEQ // TM
    first = N_CTX_TILES
    ctx_blocks = T_CTX // DEC_SEQ
    width = LAT_PAIRS_PER_STEP * LANES
    n_p = N_PAIRS // LAT_PAIRS_PER_STEP
    per_expert = DEC_BATCH * n_q * n_p // N_EXPERTS
    step = lambda b, i, p: (b * n_q + i) * n_p + p
    w_in_spec = lambda rows, cols: pl.BlockSpec(
        (None, None, rows // per_expert, cols),
        lambda b, i, p: (l, step(b, i, p) // per_expert, step(b, i, p) % per_expert, 0))
    w_out_spec = lambda rows, cols: pl.BlockSpec(
        (None, rows // per_expert, cols),
        lambda b, i, p: (step(b, i, p) // per_expert, step(b, i, p) % per_expert, 0))
    return pl.pallas_call(
        _attn_lat_kernel,
        grid=(DEC_BATCH, n_q, n_p),
        in_specs=[
            pl.BlockSpec((None, width, TM), lambda b, i, p: (first + b * n_q + i, p, 0)),
            pl.BlockSpec((DEC_SEQ, D_KV), lambda b, i, p: (ctx_blocks + b, 0)),
            pl.BlockSpec((n_q, D_KV, TM), lambda b, i, p: (ctx_blocks + b, 0, 0)),
            pl.BlockSpec((None, None, PAST_LEN, D_KV), lambda b, i, p: (b, l, 0, 0)),
            pl.BlockSpec((None, None, PAST_LEN, D_KV), lambda b, i, p: (b, l, 0, 0)),
            pl.BlockSpec((None, 1, D_ATTN), lambda b, i, p: (l, 0, 0)),
            pl.BlockSpec((None, 1, D_KV), lambda b, i, p: (l, 0, 0)),
            w_in_spec(D_MODEL, D_EXPERT),
            w_in_spec(D_MODEL, D_EXPERT),
            w_in_spec(D_EXPERT, D_MODEL),
        ],
        out_specs=[
            pl.BlockSpec((TM, width), lambda b, i, p: (b * n_q + i, p)),
            w_out_spec(D_MODEL, D_EXPERT),
            w_out_spec(D_MODEL, D_EXPERT),
            w_out_spec(D_EXPERT, D_MODEL),
        ],
        out_shape=[
            jax.ShapeDtypeStruct((T_LAT, D_ATTN), BF16),
            jax.ShapeDtypeStruct((N_EXPERTS, D_MODEL, D_EXPERT), BF16),
            jax.ShapeDtypeStruct((N_EXPERTS, D_MODEL, D_EXPERT), BF16),
            jax.ShapeDtypeStruct((N_EXPERTS, D_EXPERT, D_MODEL), BF16),
        ],
        compiler_params=_params(3),
        name=f"attn_lat_l{l}",
    )(qt, k, vt, cache_k, cache_v, qg, kg, moe_w1, moe_w3, moe_w2)


def _first_max_index(vals, valid, rowf):
    masked = jnp.where(valid, vals, -jnp.inf)
    mx = jnp.max(masked, axis=0, keepdims=True)
    idx = jnp.min(jnp.where(masked == mx, rowf, float(SUBLANES)), axis=0, keepdims=True)
    return mx, idx


def _merge_kernel(*refs, split_x):
    n_x = 2 if split_x else 1
    x_refs, refs = refs[:n_x], refs[n_x:]
    (actx_ref, alat_ref, sgu_ref, cbz_ref, prev_ref, next_ref, cw_ref, on_ref,
     wo_ref, mod_ref, g2_ref, wr_ref, wrhi_ref, rb_ref, x1_ref, h2_ref, route_ref, gate4_ref, cnt_ref) = refs
    cur = pl.program_id(0)
    is_ctx = cur < N_CTX_TILES
    mrow = _mod_row_of_tile(cur)
    on = on_ref[...]

    cbz = cbz_ref[...]
    cb = cbz[:, :D_CONV]
    z = cbz[:, D_CONV:]
    seq_mask = jnp.where(is_ctx, SEQ - 1, DEC_SEQ - 1)
    row = lax.broadcasted_iota(jnp.int32, (TM, D_CONV), 0)
    pos = (cur * TM + row) & seq_mask
    z_prev = jnp.where(row == 0, prev_ref[SUBLANES - 1:SUBLANES, D_CONV:], pltpu.roll(z, 1, 0))
    z_next = jnp.where(row == TM - 1, next_ref[0:1, D_CONV:], pltpu.roll(z, TM - 1, 0))
    z_prev = jnp.where(pos == 0, 0.0, z_prev)
    z_next = jnp.where(pos == seq_mask, 0.0, z_next)
    cw = cw_ref[...]
    conv = cb * (z_prev * cw[0:1, :] + z * cw[1:2, :] + z_next * cw[2:3, :])
    conv_n = _rms(conv, on[:, D_ATTN + D_SGU:]).astype(BF16)

    counts = jnp.zeros((SUBLANES, 1), F32)
    for sub in range(TM // MERGE_ROWS):
        rows = slice(sub * MERGE_ROWS, (sub + 1) * MERGE_ROWS)
        attn = jnp.where(is_ctx, actx_ref[rows, :], alat_ref[rows, :]).astype(F32)
        attn_n = _rms(attn, on[:, :D_ATTN]).astype(BF16)
        merged = jnp.concatenate([attn_n, sgu_ref[rows, :], conv_n[rows, :]], axis=1)
        x = jnp.where(is_ctx, x_refs[0][rows, :], x_refs[1][rows, :]) if split_x else x_refs[0][rows, :]
        x1 = x + _mod_part(mod_ref, 2, mrow) * jnp.dot(merged, wo_ref[...], preferred_element_type=F32)
        x1_ref[rows, :] = x1
        counts = counts + _route_rows(x1, rows, mrow, mod_ref, g2_ref, wr_ref, wrhi_ref, rb_ref,
                                      h2_ref, route_ref, gate4_ref)
    cnt_ref[...] = jnp.broadcast_to(counts, (SUBLANES, LANES)).astype(jnp.int32)


MERGE_ROWS = TM


def _route_rows(x1, rows, mrow, mod_ref, g2_ref, wr_ref, wrhi_ref, rb_ref, h2_ref, route_ref, gate4_ref):
    n = x1.shape[0]
    h2 = _rms(x1, g2_ref[...] * (1.0 + _mod_part(mod_ref, 4, mrow))) + _mod_part(mod_ref, 3, mrow)
    hb = h2.astype(BF16)
    h2_ref[rows, :] = hb

    h_lo = (h2 - hb.astype(F32)).astype(BF16)
    a = jnp.dot(hb, wr_ref[...], preferred_element_type=F32)
    b = jnp.dot(h_lo, wrhi_ref[...], preferred_element_type=F32)
    logits = a[:, :ROUTER_COLS] + a[:, ROUTER_COLS:] + b + rb_ref[...]
    lt = logits.T

    rowi = lax.broadcasted_iota(jnp.int32, (SUBLANES, n), 0).astype(F32)
    valid = rowi < float(N_EXPERT_GROUPS)
    g_log = lt[0:SUBLANES]
    g_max, g_idx = _first_max_index(g_log, valid, rowi)
    p_g = 1.0 / jnp.sum(jnp.where(valid, jnp.exp(g_log - g_max), 0.0), axis=0, keepdims=True)
    e_log = jnp.zeros((SUBLANES, n), F32)
    for g in range(N_EXPERT_GROUPS):
        e_log = jnp.where(g_idx == g, lt[(g + 1) * SUBLANES:(g + 2) * SUBLANES], e_log)
    e_max, i1 = _first_max_index(e_log, valid, rowi)
    e_exp = jnp.where(valid, jnp.exp(e_log - e_max), 0.0)
    e_prob = e_exp / jnp.sum(e_exp, axis=0, keepdims=True)
    v1 = jnp.max(e_prob, axis=0, keepdims=True)
    rest = jnp.logical_and(valid, rowi != i1)
    v2, i2 = _first_max_index(e_prob, rest, rowi)
    denom = v1 + v2
    w_sel = p_g * jnp.where(rowi == i1, v1 / denom, jnp.where(rowi == i2, v2 / denom, 0.0))
    route_ref[:, rows] = jnp.broadcast_to(g_idx, (SUBLANES, n))
    pad = jnp.zeros((LANES - SUBLANES, n), F32)
    gate4_ref[rows, :] = jnp.concatenate([w_sel, pad], axis=0).T
    return jnp.sum(jnp.where(rowi == g_idx, 1.0, 0.0), axis=1, keepdims=True)


def _merge(l, xs, attn_ctx, attn_lat, sgu_n, cbz, conv_w, on, w_out, mod, g2, wr_cat, wr_hi, rb):
    lay = lambda *rest: (lambda i: (l,) + rest)
    rows8 = TM // SUBLANES
    return pl.pallas_call(
        functools.partial(_merge_kernel, split_x=len(xs) == 2),
        grid=(N_TILES,),
        in_specs=_x_specs(xs) + [
            pl.BlockSpec((TM, D_ATTN), lambda i: (_ctx_tile(i), 0)),
            pl.BlockSpec((TM, D_ATTN), lambda i: (_lat_tile(i), 0)),
            pl.BlockSpec((TM, D_SGU), lambda i: (i, 0)),
            pl.BlockSpec((TM, 2 * D_CONV), lambda i: (i, 0)),
            pl.BlockSpec((SUBLANES, 2 * D_CONV), lambda i: (jnp.maximum(i * rows8 - 1, 0), 0)),
            pl.BlockSpec((SUBLANES, 2 * D_CONV),
                         lambda i: (jnp.minimum((i + 1) * rows8, T_ALL // SUBLANES - 1), 0)),
            pl.BlockSpec((None, 3, D_CONV), lay(0, 0)),
            pl.BlockSpec((None, 1, D_MIX), lay(0, 0)),
            pl.BlockSpec((None, D_MIX, D_MODEL), lay(0, 0)),
            _MOD_SPEC(l, 1),
            pl.BlockSpec((None, 1, D_MODEL), lay(0, 0)),
            pl.BlockSpec((None, D_MODEL, 2 * ROUTER_COLS), lay(0, 0)),
            pl.BlockSpec((None, D_MODEL, ROUTER_COLS), lay(0, 0)),
            pl.BlockSpec((None, 1, ROUTER_COLS), lay(0, 0)),
        ],
        out_specs=[
            pl.BlockSpec((TM, D_MODEL), lambda i: (i, 0)),
            pl.BlockSpec((TM, D_MODEL), lambda i: (i, 0)),
            pl.BlockSpec((None, SUBLANES, TM), lambda i: (i, 0, 0)),
            pl.BlockSpec((TM, LANES), lambda i: (i, 0)),
            pl.BlockSpec((None, SUBLANES, LANES), lambda i: (i, 0, 0)),
        ],
        out_shape=[
            jax.ShapeDtypeStruct((T_ALL, D_MODEL), F32),
            jax.ShapeDtypeStruct((T_ALL, D_MODEL), BF16),
            jax.ShapeDtypeStruct((N_TILES, SUBLANES, TM), F32),
            jax.ShapeDtypeStruct((T_ALL, LANES), F32),
            jax.ShapeDtypeStruct((N_TILES, SUBLANES, LANES), jnp.int32),
        ],
        compiler_params=_params(1),
        name=f"merge_l{l}",
    )(*xs, attn_ctx, attn_lat, sgu_n, cbz, cbz, cbz, conv_w, on, w_out, mod, g2, wr_cat, wr_hi, rb)


RB = 144
SORT_ROWS = TM + LANES
SORT_ROWS_PAD = -(-(SORT_ROWS + RB) // BF16_ROWS) * BF16_ROWS
MOE_VMEM_LIMIT = 56 * 1024 * 1024


def _split3(x):
    hi = x.astype(BF16)
    r1 = x - hi.astype(F32)
    mid = r1.astype(BF16)
    lo = (r1 - mid.astype(F32)).astype(BF16)
    return hi, mid, lo


def _moe_kernel(cnt_ref, h_ref, route_ref, gate4_ref, tri_ref, w1_ref, w3_ref, w2_ref, x1_ref, mod_ref, fn_ref,
                *refs, final):
    out_refs, (xs_ref, gs_ref, zs_ref) = refs[:-3], refs[-3:]
    i = pl.program_id(0)
    gate2 = _mod_part(mod_ref, 5, _mod_row_of_tile(i))
    counts = [cnt_ref[i, g] for g in range(N_EXPERT_GROUPS)]
    starts = [jnp.int32(0)]
    for g in range(N_EXPERT_GROUPS - 1):
        starts.append(starts[-1] + (counts[g] + (BF16_ROWS - 1)) // BF16_ROWS * BF16_ROWS)

    g_idx = route_ref[0:1, :]
    rowi = lax.broadcasted_iota(jnp.int32, (SUBLANES, TM), 0).astype(F32)
    onehot = jnp.where(rowi == g_idx, 1.0, 0.0)
    incl = jnp.dot(onehot.astype(BF16), tri_ref[...], preferred_element_type=F32)
    pos = jnp.sum(onehot * incl, axis=0, keepdims=True) - 1.0
    for g in range(1, N_EXPERT_GROUPS):
        pos = pos + jnp.where(g_idx == float(g), starts[g].astype(F32), 0.0)
    sub = lax.broadcasted_iota(jnp.int32, (SORT_ROWS, TM), 0).astype(F32)
    perm = jnp.where(sub == pos, 1.0, 0.0).astype(BF16)
    pos_col = jnp.broadcast_to(pos, (LANES, TM)).T
    lane = lax.broadcasted_iota(jnp.int32, (TM, SORT_ROWS), 1).astype(F32)
    perm_t = jnp.where(lane == jnp.concatenate([pos_col] * (SORT_ROWS // LANES), axis=1), 1.0, 0.0).astype(BF16)

    xs_ref[:SORT_ROWS, :] = jnp.dot(perm, h_ref[...], preferred_element_type=F32).astype(BF16)
    xs_ref[SORT_ROWS:, :] = jnp.zeros((SORT_ROWS_PAD - SORT_ROWS, D_MODEL), BF16)
    hi, mid, lo = _split3(gate4_ref[...])
    packed = (hi.astype(F32) + pltpu.roll(mid.astype(F32), EXPERTS_PER_GROUP, 1)
              + pltpu.roll(lo.astype(F32), 2 * EXPERTS_PER_GROUP, 1)).astype(BF16)
    gsp = jnp.dot(perm, packed, preferred_element_type=F32)
    gs_ref[:SORT_ROWS, :] = (gsp + pltpu.roll(gsp, LANES - EXPERTS_PER_GROUP, 1)
                             + pltpu.roll(gsp, LANES - 2 * EXPERTS_PER_GROUP, 1))
    gs_ref[SORT_ROWS:, :] = jnp.zeros((SORT_ROWS_PAD - SORT_ROWS, LANES), F32)
    zs_ref[...] = jnp.zeros((SORT_ROWS_PAD, D_MODEL), BF16)

    for g in range(N_EXPERT_GROUPS):
        def block_ffn(blk, carry, g=g):
            row0 = pl.multiple_of(starts[g] + blk * RB, BF16_ROWS)
            xb = xs_ref[pl.ds(row0, RB), :]
            gb = gs_ref[pl.ds(row0, RB), :]
            cols = []
            for e in range(EXPERTS_PER_GROUP):
                a = jnp.dot(xb, w1_ref[g * EXPERTS_PER_GROUP + e], preferred_element_type=F32)
                b = jnp.dot(xb, w3_ref[g * EXPERTS_PER_GROUP + e], preferred_element_type=F32)
                cols.append((a / (1.0 + jnp.exp(-a)) * b * gb[:, e:e + 1]).astype(BF16))
            z = jnp.dot(jnp.concatenate(cols, axis=1), w2_ref[g], preferred_element_type=F32)
            zs_ref[pl.ds(row0, RB), :] = z.astype(BF16)
            return carry

        lax.fori_loop(0, (counts[g] + (RB - 1)) // RB, block_ffn, 0)

    y = jnp.dot(perm_t, zs_ref[:SORT_ROWS, :], preferred_element_type=F32)
    out = x1_ref[...] + gate2 * y

    if final:
        octx_ref, olat_ref = out_refs
        out = _rms(out, fn_ref[...])
        olat_ref[...] = out

        @pl.when(i < N_CTX_TILES)
        def _():
            octx_ref[...] = out
    else:
        out_refs[0][...] = out


def _moe(l, counts, h2, route, gate4, tri, w1, w3, w2, x1, mod, final_norm, final):
    tile = pl.BlockSpec((TM, D_MODEL), lambda i, c: (i, 0))
    resident = pl.Buffered(1)
    if final:
        out_specs = [pl.BlockSpec((TM, D_MODEL), lambda i, c: (_ctx_tile(i), 0)),
                     pl.BlockSpec((TM, D_MODEL), lambda i, c: (_lat_tile(i), 0))]
        out_shape = [jax.ShapeDtypeStruct((T_CTX, D_MODEL), F32), jax.ShapeDtypeStruct((T_LAT, D_MODEL), F32)]
    else:
        out_specs = tile
        out_shape = jax.ShapeDtypeStruct((T_ALL, D_MODEL), F32)
    grid_spec = pltpu.PrefetchScalarGridSpec(
        num_scalar_prefetch=1,
        grid=(N_TILES,),
        in_specs=[
            tile,
            pl.BlockSpec((None, SUBLANES, TM), lambda i, c: (i, 0, 0)),
            pl.BlockSpec((TM, LANES), lambda i, c: (i, 0)),
            pl.BlockSpec((TM, TM), lambda i, c: (0, 0), pipeline_mode=resident),
            pl.BlockSpec((N_EXPERTS, D_MODEL, D_EXPERT), lambda i, c: (0, 0, 0), pipeline_mode=resident),
            pl.BlockSpec((N_EXPERTS, D_MODEL, D_EXPERT), lambda i, c: (0, 0, 0), pipeline_mode=resident),
            pl.BlockSpec((N_EXPERT_GROUPS, D_GROUP_HID, D_MODEL), lambda i, c: (0, 0, 0), pipeline_mode=resident),
            tile,
            pl.BlockSpec((None, N_MOD_PARTS, N_MOD_ROWS, D_MODEL), lambda i, c: (l, 0, 0, 0)),
            pl.BlockSpec((1, D_MODEL), lambda i, c: (0, 0)),
        ],
        out_specs=out_specs,
        scratch_shapes=[pltpu.VMEM((SORT_ROWS_PAD, D_MODEL), BF16),
                        pltpu.VMEM((SORT_ROWS_PAD, LANES), F32),
                        pltpu.VMEM((SORT_ROWS_PAD, D_MODEL), BF16)],
    )
    return pl.pallas_call(
        functools.partial(_moe_kernel, final=final),
        grid_spec=grid_spec,
        out_shape=out_shape,
        compiler_params=pltpu.CompilerParams(dimension_semantics=("arbitrary",),
                                             vmem_limit_bytes=MOE_VMEM_LIMIT),
        name=f"moe_l{l}",
    )(counts, h2, route, gate4, tri, w1, w3, w2, x1, mod, final_norm)


def _rope_tables():
    rows = DEC_SEQ // GRID_W
    row = np.repeat(np.arange(rows, dtype=np.float32), GRID_W)
    col = np.tile(np.arange(GRID_W, dtype=np.float32), rows)
    half = ROPE_AXIS_DIM // 2
    inv_freq = (1.0 / (np.float32(ROPE_THETA) ** (np.arange(half, dtype=np.float32) * np.float32(2.0)
                                                  / np.float32(ROPE_AXIS_DIM)))).astype(np.float32)
    ar = row[:, None] * inv_freq
    ac = col[:, None] * inv_freq
    cos64 = np.concatenate([np.cos(ar), np.cos(ar), np.cos(ac), np.cos(ac)], axis=1)
    sin64 = np.concatenate([-np.sin(ar), np.sin(ar), -np.sin(ac), np.sin(ac)], axis=1)
    cos_t = np.concatenate([np.ones((TM, LANES), np.float32), np.tile(cos64, (1, 2))], axis=0)
    sin_t = np.concatenate([np.zeros((TM, LANES), np.float32), np.tile(sin64, (1, 2))], axis=0)
    return jnp.asarray(cos_t, F32), jnp.asarray(sin_t, F32)


def _router_weights(router_g_w, router_g_b, router_e_w, router_e_b):
    zw = jnp.zeros((DEPTH, D_MODEL, SUBLANES - N_EXPERT_GROUPS), F32)
    zb = jnp.zeros((DEPTH, SUBLANES - N_EXPERT_GROUPS), F32)
    w_cols, b_cols = [router_g_w, zw], [router_g_b, zb]
    for g in range(N_EXPERT_GROUPS):
        w_cols += [router_e_w[:, g], zw]
        b_cols += [router_e_b[:, g], zb]
    used = (1 + N_EXPERT_GROUPS) * SUBLANES
    w_cols.append(jnp.zeros((DEPTH, D_MODEL, ROUTER_COLS - used), F32))
    b_cols.append(jnp.zeros((DEPTH, ROUTER_COLS - used), F32))
    w = jnp.concatenate(w_cols, axis=2)
    b = jnp.concatenate(b_cols, axis=1).reshape(DEPTH, 1, ROUTER_COLS)
    w_hi = w.astype(BF16)
    w_lo = (w - w_hi.astype(F32)).astype(BF16)
    return jnp.concatenate([w_hi, w_lo], axis=2), w_hi, b


def kernel(x_prompt, x_sample, cache_k, cache_v, c, c_ctx, w_ada, b_ada, norm1, w_in, q_norm, k_norm,
           sgu_w, sgu_b, conv_w, out_norm, w_out, norm2, router_g_w, router_g_b, router_e_w,
           router_e_b, moe_w1, moe_w3, moe_w2, final_norm):
    xs = (x_prompt.reshape(T_CTX, D_MODEL), x_sample.reshape(T_LAT, D_MODEL))

    cvecs = jnp.concatenate([c_ctx[None, :], c, jnp.zeros((N_MOD_ROWS - 1 - DEC_BATCH, D_MODEL), F32)], axis=0)
    mod = _modulation(cvecs.T, w_ada, b_ada)

    w_in_b = w_in.astype(BF16)
    w_out_b = w_out.astype(BF16)
    g1 = norm1.reshape(DEPTH, 1, D_MODEL)
    g2 = norm2.reshape(DEPTH, 1, D_MODEL)
    on = out_norm.reshape(DEPTH, 1, D_MIX)
    qg = jnp.tile(q_norm, (1, N_HEADS)).reshape(DEPTH, 1, D_ATTN)
    kg = jnp.tile(k_norm, (1, N_KV_HEADS)).reshape(DEPTH, 1, D_KV)
    cos_t, sin_t = _rope_tables()
    bq = jnp.asarray(np.kron(np.eye(N_HEADS), np.full((HEAD_DIM, HEAD_DIM), 1.0 / HEAD_DIM)), BF16)
    sgu_w_pairs = sgu_w.astype(BF16).reshape(DEPTH, N_SGU_GROUPS // 2, 2, CHUNK, CHUNK)
    sgu_w_pairs = jnp.concatenate([sgu_w_pairs[:, :, 0], sgu_w_pairs[:, :, 1]], axis=-1)
    sgu_bias = jnp.repeat(jnp.swapaxes(sgu_b, 1, 2), SGU_GROUP_DIM, axis=2)
    wr_cat, wr_hi, rb = _router_weights(router_g_w, router_g_b, router_e_w, router_e_b)
    fn = final_norm.reshape(1, D_MODEL)
    tri = jnp.asarray(np.triu(np.ones((TM, TM), np.float32)), BF16)
    ck = cache_k.reshape(DEC_BATCH, DEPTH, PAST_LEN, D_KV)
    cv = cache_v.reshape(DEC_BATCH, DEPTH, PAST_LEN, D_KV)

    ctx_ks, ctx_vs = [], []
    for l in range(DEPTH):
        qt, k, vt, k32, v32, sgu_n, cbz = _proj(l, xs, mod, g1, w_in_b, qg, kg, cos_t, sin_t, bq,
                                                sgu_w_pairs, sgu_bias, on)
        attn_ctx = _attn_ctx(l, qt, k, vt)
        attn_lat, w1, w3, w2 = _attn_lat(l, qt, k, vt, ck, cv, qg, kg, moe_w1, moe_w3, moe_w2)
        w2 = w2.reshape(N_EXPERT_GROUPS, D_GROUP_HID, D_MODEL)
        x1, h2, route, gate4, cnt = _merge(l, xs, attn_ctx, attn_lat, sgu_n, cbz, conv_w, on, w_out_b, mod, g2,
                                           wr_cat, wr_hi, rb)
        counts = cnt[:, :N_EXPERT_GROUPS, 0]
        out = _moe(l, counts, h2, route, gate4, tri, w1, w3, w2, x1, mod, fn, final=(l == DEPTH - 1))
        xs = (out,)
        ctx_ks.append(k32.reshape(BATCH, SEQ, N_KV_HEADS, HEAD_DIM))
        ctx_vs.append(v32.reshape(BATCH, SEQ, N_KV_HEADS, HEAD_DIM))

    y_prompt, y_sample = out
    return (y_prompt.reshape(BATCH, SEQ, D_MODEL), y_sample.reshape(DEC_BATCH, DEC_SEQ, D_MODEL),
            jnp.stack(ctx_ks, axis=1), jnp.stack(ctx_vs, axis=1))
```

```python
import functools
import math

import jax
import jax.numpy as jnp
import numpy as np
from jax import lax
from jax.experimental import pallas as pl
from jax.experimental.pallas import tpu as pltpu

D_MODEL = 1024
BATCH = 16
SEQ = 256
DEPTH = 2
DEC_BATCH = 2
DEC_SEQ = 4096
PAST_LEN = 512
GRID_W = 64
N_HEADS = 8
N_KV_HEADS = 2
HEAD_DIM = 64
ROPE_AXIS_DIM = HEAD_DIM // 2
ROPE_THETA = 10000.0
D_ATTN = N_HEADS * HEAD_DIM
D_KV = N_KV_HEADS * HEAD_DIM
CHUNK = 128
N_SGU_GROUPS = 4
SGU_GROUP_DIM = 64
D_SGU = N_SGU_GROUPS * SGU_GROUP_DIM
D_CONV = 256
D_MIX = D_ATTN + D_SGU + D_CONV
D_IN = D_ATTN + 2 * D_KV + 2 * D_SGU + 3 * D_CONV
N_EXPERT_GROUPS = 4
EXPERTS_PER_GROUP = 4
N_EXPERTS = N_EXPERT_GROUPS * EXPERTS_PER_GROUP
D_EXPERT = 256
D_GROUP_HID = EXPERTS_PER_GROUP * D_EXPERT
EPS = 1e-6

T_CTX = BATCH * SEQ
T_LAT = DEC_BATCH * DEC_SEQ
T_ALL = T_CTX + T_LAT
TM = 512
N_TILES = T_ALL // TM
N_CTX_TILES = T_CTX // TM
LAT_TILES_PER_BATCH = DEC_SEQ // TM
TM_MOE = 1024
LANES = 128
SUBLANES = 8
BF16_ROWS = 16
N_MOD_ROWS = 8
N_MOD_PARTS = 6
ROUTER_COLS = 128
VMEM_LIMIT = 48 * 1024 * 1024

OFF_Q = 0
OFF_KV = D_ATTN
OFF_SGU = OFF_KV + 2 * D_KV
OFF_CONV = OFF_SGU + 2 * D_SGU

BF16 = jnp.bfloat16
F32 = jnp.float32
NEG_BIG = -1e30
Q_SCALE = HEAD_DIM ** -0.5 * math.log2(math.e)


def _params(n_grid_dims):
    return pltpu.CompilerParams(
        dimension_semantics=("arbitrary",) * n_grid_dims,
        vmem_limit_bytes=VMEM_LIMIT,
    )


def _mod_row_of_tile(i, tile=TM):
    n_ctx = T_CTX // tile
    return jnp.where(i < n_ctx, 0, 1 + (i - n_ctx) // (DEC_SEQ // tile))


def _mod_part(mod_ref, part, row):
    return mod_ref[part, pl.ds(row, 1), :]


def _rope_block_of_tile(i):
    return jnp.where(i < N_CTX_TILES, 0, 1 + (i - N_CTX_TILES) % LAT_TILES_PER_BATCH)


def _rms(x, gain):
    ms = jnp.mean(x * x, axis=-1, keepdims=True)
    return x * lax.rsqrt(ms + EPS) * gain


def _ctx_tile(i):
    return jnp.minimum(i, N_CTX_TILES - 1)


def _lat_tile(i):
    return jnp.maximum(i - N_CTX_TILES, 0)


MOD_PARTS_PER_STEP = 2
MOD_TN = MOD_PARTS_PER_STEP * D_MODEL


def _mod_kernel(ct_ref, w_ref, b_ref, o_ref):
    c = ct_ref[...]
    s = c / (1.0 + jnp.exp(-c))
    row = lax.broadcasted_iota(jnp.int32, (N_MOD_ROWS, D_MODEL), 0)
    for part in range(MOD_PARTS_PER_STEP):
        cols = slice(part * D_MODEL, (part + 1) * D_MODEL)
        w = w_ref[:, cols]
        out = jnp.zeros((N_MOD_ROWS, D_MODEL), F32)
        for r in range(1 + DEC_BATCH):
            acc = jnp.sum(w * s[:, r:r + 1], axis=0, keepdims=True)
            out = jnp.where(row == r, acc, out)
        o_ref[part] = out + b_ref[:, cols]


def _modulation(cvecs_t, w_ada, b_ada):
    n_col = N_MOD_PARTS * D_MODEL
    return pl.pallas_call(
        _mod_kernel,
        grid=(DEPTH, n_col // MOD_TN),
        in_specs=[
            pl.BlockSpec((D_MODEL, N_MOD_ROWS), lambda l, j: (0, 0)),
            pl.BlockSpec((None, D_MODEL, MOD_TN), lambda l, j: (l, 0, j)),
            pl.BlockSpec((None, 1, MOD_TN), lambda l, j: (l, 0, j)),
        ],
        out_specs=pl.BlockSpec((None, MOD_PARTS_PER_STEP, N_MOD_ROWS, D_MODEL), lambda l, j: (l, j, 0, 0)),
        out_shape=jax.ShapeDtypeStruct((DEPTH, N_MOD_PARTS, N_MOD_ROWS, D_MODEL), F32),
        compiler_params=_params(2),
        name="adaln_modulation",
    )(cvecs_t, w_ada, b_ada.reshape(DEPTH, 1, n_col))


_MOD_SPEC = lambda l, nd: pl.BlockSpec(
    (None, N_MOD_PARTS, N_MOD_ROWS, D_MODEL), lambda *idx: (l, 0, 0, 0))


def _swap16(x):
    lane = lax.broadcasted_iota(jnp.int32, x.shape, 1)
    up = pltpu.roll(x, 16, 1)
    down = pltpu.roll(x, LANES - 16, 1)
    return jnp.where((lane & 16) != 0, up, down)


def _rope(x, cos_t, sin_t):
    cols = []
    for j in range(x.shape[1] // LANES):
        xc = x[:, j * LANES:(j + 1) * LANES]
        cols.append(xc * cos_t + _swap16(xc) * sin_t)
    return cols[0] if len(cols) == 1 else jnp.concatenate(cols, axis=1)


def _proj_kernel(*refs, split_x):
    n_x = 2 if split_x else 1
    x_refs, refs, (scr_even, scr_odd) = refs[:n_x], refs[n_x:-2], refs[-2:]
    step = functools.partial(_proj_step, x_refs, refs, split_x)
    i = pl.program_id(0)

    @pl.when(i == 0)
    def _():
        scr_odd[...] = jnp.zeros((TM, D_IN), F32)

    @pl.when(i % 2 == 0)
    def _():
        step(scr_even, scr_odd)

    @pl.when(i % 2 == 1)
    def _():
        step(scr_odd, scr_even)


def _proj_step(x_refs, refs, split_x, new_ref, proj):
    (mod_ref, g1_ref, w_ref, qg_ref, kg_ref, cos_ref, sin_ref, bq_ref, sw_ref, sb_ref, on_ref,
     qt_ref, k_ref, vt_ref, k32_ref, v32_ref, sgu_ref, cbz_ref) = refs
    i = pl.program_id(0)
    cur = _stage1_tile(i)

    is_ctx = i <= N_CTX_TILES
    cos_t = cos_ref[...]
    sin_t = sin_ref[...]

    q = proj[:, OFF_Q:OFF_Q + D_ATTN]
    qms = jnp.dot((q * q).astype(BF16), bq_ref[...], preferred_element_type=F32)
    qn = q * lax.rsqrt(qms + EPS) * qg_ref[...]
    qr = _rope(qn, cos_t, sin_t) * Q_SCALE
    qt_ref[...] = qr.T.astype(BF16)

    k = proj[:, OFF_KV:OFF_KV + D_KV]
    v = proj[:, OFF_KV + D_KV:OFF_KV + 2 * D_KV]
    kms = jnp.dot((k * k).astype(BF16), bq_ref[:D_KV, :D_KV], preferred_element_type=F32)
    kn = k * lax.rsqrt(kms + EPS) * kg_ref[...]
    k_ref[...] = _rope(kn, cos_t, sin_t).astype(BF16)
    vt_ref[...] = v.T.astype(BF16)

    su = proj[:, OFF_SGU:OFF_SGU + D_SGU]
    sv = proj[:, OFF_SGU + D_SGU:OFF_SGU + 2 * D_SGU]
    n_chunks = TM // CHUNK
    sv_wide = jnp.concatenate([sv[n * CHUNK:(n + 1) * CHUNK, :] for n in range(n_chunks)], axis=1)
    grp = lax.broadcasted_iota(jnp.int32, (CHUNK, n_chunks * D_SGU), 1) // SGU_GROUP_DIM % N_SGU_GROUPS
    mixed_wide = jnp.zeros((CHUNK, n_chunks * D_SGU), F32)
    for pair in range(N_SGU_GROUPS // 2):
        rhs = jnp.concatenate(
            [jnp.where(grp == 2 * pair, sv_wide, 0.0), jnp.where(grp == 2 * pair + 1, sv_wide, 0.0)],
            axis=0).astype(BF16)
        mixed_wide = mixed_wide + jnp.dot(sw_ref[pair], rhs, preferred_element_type=F32)
    on_sgu = on_ref[:, D_ATTN:D_ATTN + D_SGU]
    for n in range(n_chunks):
        sgu = su[n * CHUNK:(n + 1) * CHUNK, :] * (mixed_wide[:, n * D_SGU:(n + 1) * D_SGU] + sb_ref[...])
        sgu_ref[n * CHUNK:(n + 1) * CHUNK, :] = _rms(sgu, on_sgu).astype(BF16)

    cbz_ref[:, :D_CONV] = proj[:, OFF_CONV:OFF_CONV + D_CONV]
    cbz_ref[:, D_CONV:] = (proj[:, OFF_CONV + D_CONV:OFF_CONV + 2 * D_CONV]
                           * proj[:, OFF_CONV + 2 * D_CONV:OFF_CONV + 3 * D_CONV])

    x = jnp.where(cur < N_CTX_TILES, x_refs[0][...], x_refs[1][...]) if split_x else x_refs[0][...]
    mrow = _mod_row_of_tile(cur)
    h = _rms(x, g1_ref[...] * (1.0 + _mod_part(mod_ref, 1, mrow))) + _mod_part(mod_ref, 0, mrow)
    new_ref[...] = jnp.dot(h.astype(BF16), w_ref[...], preferred_element_type=F32)

    @pl.when(is_ctx)
    def _():
        k32_ref[...] = kn
        v32_ref[...] = v


def _x_specs(xs, tile_of_step=lambda i: i):
    if len(xs) == 1:
        return [pl.BlockSpec((TM, D_MODEL), lambda i: (tile_of_step(i), 0))]
    return [pl.BlockSpec((TM, D_MODEL), lambda i: (_ctx_tile(tile_of_step(i)), 0)),
            pl.BlockSpec((TM, D_MODEL), lambda i: (_lat_tile(tile_of_step(i)), 0))]


def _stage1_tile(i):
    return jnp.minimum(i, N_TILES - 1)


def _stage2_tile(i):
    return jnp.maximum(i - 1, 0)


def _proj(l, xs, mod, g1, w_in, qg, kg, cos_t, sin_t, bq, sgu_w, sgu_b, on):
    lay = lambda *rest: (lambda i: (l,) + rest)
    s2 = _stage2_tile
    return pl.pallas_call(
        functools.partial(_proj_kernel, split_x=len(xs) == 2),
        grid=(N_TILES + 1,),
        in_specs=_x_specs(xs, _stage1_tile) + [
            _MOD_SPEC(l, 1),
            pl.BlockSpec((None, 1, D_MODEL), lay(0, 0)),
            pl.BlockSpec((None, D_MODEL, D_IN), lay(0, 0)),
            pl.BlockSpec((None, 1, D_ATTN), lay(0, 0)),
            pl.BlockSpec((None, 1, D_KV), lay(0, 0)),
            pl.BlockSpec((TM, LANES), lambda i: (_rope_block_of_tile(s2(i)), 0)),
            pl.BlockSpec((TM, LANES), lambda i: (_rope_block_of_tile(s2(i)), 0)),
            pl.BlockSpec((D_ATTN, D_ATTN), lambda i: (0, 0)),
            pl.BlockSpec((None, N_SGU_GROUPS // 2, CHUNK, 2 * CHUNK), lay(0, 0, 0)),
            pl.BlockSpec((None, CHUNK, D_SGU), lay(0, 0)),
            pl.BlockSpec((None, 1, D_MIX), lay(0, 0)),
        ],
        out_specs=[
            pl.BlockSpec((None, D_ATTN, TM), lambda i: (s2(i), 0, 0)),
            pl.BlockSpec((TM, D_KV), lambda i: (s2(i), 0)),
            pl.BlockSpec((None, D_KV, TM), lambda i: (s2(i), 0, 0)),
            pl.BlockSpec((TM, D_KV), lambda i: (_ctx_tile(s2(i)), 0)),
            pl.BlockSpec((TM, D_KV), lambda i: (_ctx_tile(s2(i)), 0)),
            pl.BlockSpec((TM, D_SGU), lambda i: (s2(i), 0)),
            pl.BlockSpec((TM, 2 * D_CONV), lambda i: (s2(i), 0)),
        ],
        out_shape=[
            jax.ShapeDtypeStruct((N_TILES, D_ATTN, TM), BF16),
            jax.ShapeDtypeStruct((T_ALL, D_KV), BF16),
            jax.ShapeDtypeStruct((N_TILES, D_KV, TM), BF16),
            jax.ShapeDtypeStruct((T_CTX, D_KV), F32),
            jax.ShapeDtypeStruct((T_CTX, D_KV), F32),
            jax.ShapeDtypeStruct((T_ALL, D_SGU), BF16),
            jax.ShapeDtypeStruct((T_ALL, 2 * D_CONV), F32),
        ],
        scratch_shapes=[pltpu.VMEM((TM, D_IN), F32), pltpu.VMEM((TM, D_IN), F32)],
        compiler_params=_params(1),
        name=f"proj_l{l}",
    )(*xs, mod, g1, w_in, qg, kg, cos_t, sin_t, bq, sgu_w, sgu_b, on)


ACC_ROWS = HEAD_DIM + BF16_ROWS
N_PAIRS = N_HEADS // 2
PAIRS_PER_KV = N_PAIRS // N_KV_HEADS


def _pair_queries(qt, kv_idx, tq):
    qf = qt.astype(F32)
    mine = lax.broadcasted_iota(jnp.int32, (2 * HEAD_DIM, tq), 0) // HEAD_DIM == kv_idx
    qe = qf[:HEAD_DIM]
    qo = qf[HEAD_DIM:]
    rhs = jnp.concatenate(
        [jnp.where(mine, jnp.concatenate([qe, qe], axis=0), 0.0),
         jnp.where(mine, jnp.concatenate([qo, qo], axis=0), 0.0)], axis=1)
    return rhs.astype(BF16)


def _with_ones(vt_c):
    return jnp.concatenate([vt_c, jnp.ones((BF16_ROWS, vt_c.shape[1]), BF16)], axis=0)


def _flash_pairs(rhs_list, chunk_lists, tq, shift=None):
    n_pairs, n_chunks = len(rhs_list), len(chunk_lists[0])
    scores = lambda s, c: jnp.dot(chunk_lists[s][c][0](), rhs_list[s], preferred_element_type=F32)
    m = [jnp.full((1, 2 * tq), NEG_BIG, F32)] * n_pairs
    acc = [jnp.zeros((ACC_ROWS, 2 * tq), F32)] * n_pairs
    st = [scores(s, 0) for s in range(n_pairs)]
    for c in range(n_chunks):
        for s in range(n_pairs):
            if shift is None:
                m_new = jnp.maximum(m[s], jnp.max(st[s], axis=0, keepdims=True))
            st_next = scores(s, c + 1) if c + 1 < n_chunks else None
            if shift is None:
                alpha = jnp.exp2(m[s] - m_new)
                p = jnp.exp2(st[s] - m_new).astype(BF16)
                acc[s] = alpha * acc[s] + jnp.dot(chunk_lists[s][c][1](), p, preferred_element_type=F32)
                m[s] = m_new
            else:
                p = jnp.exp2(st[s] - shift).astype(BF16)
                acc[s] = acc[s] + jnp.dot(chunk_lists[s][c][1](), p, preferred_element_type=F32)
            st[s] = st_next
    outs = []
    for s in range(n_pairs):
        o = acc[s][:HEAD_DIM] * (1.0 / acc[s][HEAD_DIM:HEAD_DIM + 1])
        outs.append(jnp.concatenate([o[:, :tq], o[:, tq:]], axis=0))
    return outs


def _attn_ctx_kernel(qt_ref, k_ref, vt_ref, o_ref):
    tq = SEQ
    rhs_list, chunk_lists = [], []
    for pair in range(N_PAIRS):
        kv_idx = pair // PAIRS_PER_KV
        rhs_list.append(_pair_queries(qt_ref[pair * LANES:(pair + 1) * LANES, :], kv_idx, tq))
        chunk_lists.append([(lambda: k_ref[...],
                             lambda kv_idx=kv_idx: _with_ones(vt_ref[kv_idx * HEAD_DIM:(kv_idx + 1) * HEAD_DIM, :]))])
    for pair, o_t in enumerate(_flash_pairs(rhs_list, chunk_lists, tq)):
        o_ref[:, pair * LANES:(pair + 1) * LANES] = o_t.T.astype(BF16)


def _attn_ctx(l, qt, k, vt):
    halves = TM // SEQ
    return pl.pallas_call(
        _attn_ctx_kernel,
        grid=(BATCH,),
        in_specs=[
            pl.BlockSpec((None, D_ATTN, SEQ), lambda b: (b // halves, 0, b % halves)),
            pl.BlockSpec((SEQ, D_KV), lambda b: (b, 0)),
            pl.BlockSpec((None, D_KV, SEQ), lambda b: (b // halves, 0, b % halves)),
        ],
        out_specs=pl.BlockSpec((SEQ, D_ATTN), lambda b: (b, 0)),
        out_shape=jax.ShapeDtypeStruct((T_CTX, D_ATTN), BF16),
        compiler_params=_params(1),
        name=f"attn_ctx_l{l}",
    )(qt, k, vt)


KC = 512
LAT_PAIRS_PER_STEP = 2


MAX_SHIFT = 60.0
BOUND_SLACK = 1.05


def _score_bound_sq(qg_ref, kg_ref, ck):
    qg = qg_ref[...]
    kg = kg_ref[...]
    gq2 = jnp.max(qg * qg, axis=1, keepdims=True)
    k2 = HEAD_DIM * jnp.max(kg * kg, axis=1, keepdims=True)
    if ck is not None:
        sq = ck * ck
        first = lax.broadcasted_iota(jnp.int32, sq.shape, 1) < HEAD_DIM
        n0 = jnp.sum(jnp.where(first, sq, 0.0), axis=1, keepdims=True)
        n1 = jnp.sum(jnp.where(first, 0.0, sq), axis=1, keepdims=True)
        k2 = jnp.maximum(k2, jnp.max(jnp.maximum(n0, n1), axis=0, keepdims=True))
    return (BOUND_SLACK * Q_SCALE) ** 2 * HEAD_DIM * gq2 * k2


def _run_with_score_bound(run, bound):
    small = bound <= MAX_SHIFT

    @pl.when(small)
    def _():
        run(bound)

    @pl.when(jnp.logical_not(small))
    def _():
        run(None)


def _attn_lat_kernel(qt_ref, k_ref, vt_ref, ck_ref, cv_ref, qg_ref, kg_ref, w1_ref, w3_ref, w2_ref,
                     o_ref, w1b_ref, w3b_ref, w2b_ref, bound_ref):
    tq = TM

    def run(shift):
        w1b_ref[...] = w1_ref[...].astype(BF16)
        w3b_ref[...] = w3_ref[...].astype(BF16)
        w2b_ref[...] = w2_ref[...].astype(BF16)
        cvt = cv_ref[...].T
        rhs_list, chunk_lists = [], []
        for j in range(LAT_PAIRS_PER_STEP):
            kv_idx = (pl.program_id(2) * LAT_PAIRS_PER_STEP + j) // PAIRS_PER_KV
            rhs_list.append(_pair_queries(qt_ref[j * LANES:(j + 1) * LANES, :], kv_idx, tq))
            v_row = pl.multiple_of(kv_idx * HEAD_DIM, HEAD_DIM)

            def latent_chunk(c, v_row=v_row):
                tile, lane0 = (c * KC) // TM, (c * KC) % TM
                return (lambda: k_ref[c * KC:(c + 1) * KC, :],
                        lambda: _with_ones(vt_ref[tile, pl.ds(v_row, HEAD_DIM), lane0:lane0 + KC]))

            def cached_chunk(c, kv_idx=kv_idx):
                return (lambda: ck_ref[c * KC:(c + 1) * KC, :].astype(BF16),
                        lambda: _with_ones(jnp.where(kv_idx == 0, cvt[:HEAD_DIM, c * KC:(c + 1) * KC],
                                                     cvt[HEAD_DIM:, c * KC:(c + 1) * KC]).astype(BF16)))

            chunks = [latent_chunk(c) for c in range(DEC_SEQ // KC)]
            chunks += [cached_chunk(c) for c in range(PAST_LEN // KC)]
            chunk_lists.append(chunks)
        for j, o_t in enumerate(_flash_pairs(rhs_list, chunk_lists, tq, shift)):
            o_ref[:, j * LANES:(j + 1) * LANES] = o_t.T.astype(BF16)

    @pl.when(jnp.logical_and(pl.program_id(1) == 0, pl.program_id(2) == 0))
    def _():
        bound_ref[0] = jnp.sqrt(_score_bound_sq(qg_ref, kg_ref, ck_ref[...]))[0, 0]

    _run_with_score_bound(run, bound_ref[0])


def _attn_lat(l, qt, k, vt, cache_k, cache_v, qg, kg, moe_w1, moe_w3, moe_w2):
    n_q = DEC_SEQ // TM
    first = N_CTX_TILES
    ctx_blocks = T_CTX // DEC_SEQ
    width = LAT_PAIRS_PER_STEP * LANES
    n_p = N_PAIRS // LAT_PAIRS_PER_STEP
    per_expert = DEC_BATCH * n_q * n_p // N_EXPERTS
    step = lambda b, i, p: (b * n_q + i) * n_p + p
    w_in_spec = lambda rows, cols: pl.BlockSpec(
        (None, None, rows // per_expert, cols),
        lambda b, i, p: (l, step(b, i, p) // per_expert, step(b, i, p) % per_expert, 0))
    w_out_spec = lambda rows, cols: pl.BlockSpec(
        (None, rows // per_expert, cols),
        lambda b, i, p: (step(b, i, p) // per_expert, step(b, i, p) % per_expert, 0))
    return pl.pallas_call(
        _attn_lat_kernel,
        grid=(DEC_BATCH, n_q, n_p),
        in_specs=[
            pl.BlockSpec((None, width, TM), lambda b, i, p: (first + b * n_q + i, p, 0)),
            pl.BlockSpec((DEC_SEQ, D_KV), lambda b, i, p: (ctx_blocks + b, 0)),
            pl.BlockSpec((n_q, D_KV, TM), lambda b, i, p: (ctx_blocks + b, 0, 0)),
            pl.BlockSpec((None, None, PAST_LEN, D_KV), lambda b, i, p: (b, l, 0, 0)),
            pl.BlockSpec((None, None, PAST_LEN, D_KV), lambda b, i, p: (b, l, 0, 0)),
            pl.BlockSpec((None, 1, D_ATTN), lambda b, i, p: (l, 0, 0)),
            pl.BlockSpec((None, 1, D_KV), lambda b, i, p: (l, 0, 0)),
            w_in_spec(D_MODEL, D_EXPERT),
            w_in_spec(D_MODEL, D_EXPERT),
            w_in_spec(D_EXPERT, D_MODEL),
        ],
        out_specs=[
            pl.BlockSpec((TM, width), lambda b, i, p: (b * n_q + i, p)),
            w_out_spec(D_MODEL, D_EXPERT),
            w_out_spec(D_MODEL, D_EXPERT),
            w_out_spec(D_EXPERT, D_MODEL),
        ],
        out_shape=[
            jax.ShapeDtypeStruct((T_LAT, D_ATTN), BF16),
            jax.ShapeDtypeStruct((N_EXPERTS, D_MODEL, D_EXPERT), BF16),
            jax.ShapeDtypeStruct((N_EXPERTS, D_MODEL, D_EXPERT), BF16),
            jax.ShapeDtypeStruct((N_EXPERTS, D_EXPERT, D_MODEL), BF16),
        ],
        scratch_shapes=[pltpu.SMEM((1,), F32)],
        compiler_params=_params(3),
        name=f"attn_lat_l{l}",
    )(qt, k, vt, cache_k, cache_v, qg, kg, moe_w1, moe_w3, moe_w2)


def _first_max_index(vals, valid, rowf):
    masked = jnp.where(valid, vals, -jnp.inf)
    mx = jnp.max(masked, axis=0, keepdims=True)
    idx = jnp.min(jnp.where(masked == mx, rowf, float(SUBLANES)), axis=0, keepdims=True)
    return mx, idx


def _merge_kernel(*refs, split_x):
    n_x = 2 if split_x else 1
    x_refs, refs = refs[:n_x], refs[n_x:]
    (actx_ref, alat_ref, sgu_ref, cbz_ref, prev_ref, next_ref, cw_ref, on_ref,
     wo_ref, mod_ref, g2_ref, wr_ref, wrhi_ref, rb_ref, x1_ref, h2_ref, route_ref, gate4_ref, cnt_ref) = refs
    cur = pl.program_id(0)
    is_ctx = cur < N_CTX_TILES
    mrow = _mod_row_of_tile(cur)
    on = on_ref[...]

    cbz = cbz_ref[...]
    cb = cbz[:, :D_CONV]
    z = cbz[:, D_CONV:]
    seq_mask = jnp.where(is_ctx, SEQ - 1, DEC_SEQ - 1)
    row = lax.broadcasted_iota(jnp.int32, (TM, D_CONV), 0)
    pos = (cur * TM + row) & seq_mask
    z_prev = jnp.where(row == 0, prev_ref[SUBLANES - 1:SUBLANES, D_CONV:], pltpu.roll(z, 1, 0))
    z_next = jnp.where(row == TM - 1, next_ref[0:1, D_CONV:], pltpu.roll(z, TM - 1, 0))
    z_prev = jnp.where(pos == 0, 0.0, z_prev)
    z_next = jnp.where(pos == seq_mask, 0.0, z_next)
    cw = cw_ref[...]
    conv = cb * (z_prev * cw[0:1, :] + z * cw[1:2, :] + z_next * cw[2:3, :])
    conv_n = _rms(conv, on[:, D_ATTN + D_SGU:]).astype(BF16)

    counts = jnp.zeros((SUBLANES, 1), F32)
    for sub in range(TM // MERGE_ROWS):
        rows = slice(sub * MERGE_ROWS, (sub + 1) * MERGE_ROWS)
        attn = jnp.where(is_ctx, actx_ref[rows, :], alat_ref[rows, :]).astype(F32)
        attn_n = _rms(attn, on[:, :D_ATTN]).astype(BF16)
        merged = jnp.concatenate([attn_n, sgu_ref[rows, :], conv_n[rows, :]], axis=1)
        x = jnp.where(is_ctx, x_refs[0][rows, :], x_refs[1][rows, :]) if split_x else x_refs[0][rows, :]
        x1 = x + _mod_part(mod_ref, 2, mrow) * jnp.dot(merged, wo_ref[...], preferred_element_type=F32)
        x1_ref[rows, :] = x1
        counts = counts + _route_rows(x1, rows, mrow, mod_ref, g2_ref, wr_ref, wrhi_ref, rb_ref,
                                      h2_ref, route_ref, gate4_ref)
    cnt_ref[...] = jnp.broadcast_to(counts, (SUBLANES, LANES)).astype(jnp.int32)


MERGE_ROWS = TM


def _route_rows(x1, rows, mrow, mod_ref, g2_ref, wr_ref, wrhi_ref, rb_ref, h2_ref, route_ref, gate4_ref):
    n = x1.shape[0]
    h2 = _rms(x1, g2_ref[...] * (1.0 + _mod_part(mod_ref, 4, mrow))) + _mod_part(mod_ref, 3, mrow)
    hb = h2.astype(BF16)
    h2_ref[rows, :] = hb

    h_lo = (h2 - hb.astype(F32)).astype(BF16)
    a = jnp.dot(hb, wr_ref[...], preferred_element_type=F32)
    b = jnp.dot(h_lo, wrhi_ref[...], preferred_element_type=F32)
    logits = a[:, :ROUTER_COLS] + a[:, ROUTER_COLS:] + b + rb_ref[...]
    lt = logits.T

    rowi = lax.broadcasted_iota(jnp.int32, (SUBLANES, n), 0).astype(F32)
    valid = rowi < float(N_EXPERT_GROUPS)
    g_log = lt[0:SUBLANES]
    g_max, g_idx = _first_max_index(g_log, valid, rowi)
    p_g = 1.0 / jnp.sum(jnp.where(valid, jnp.exp(g_log - g_max), 0.0), axis=0, keepdims=True)
    e_log = jnp.zeros((SUBLANES, n), F32)
    for g in range(N_EXPERT_GROUPS):
        e_log = jnp.where(g_idx == g, lt[(g + 1) * SUBLANES:(g + 2) * SUBLANES], e_log)
    e_max, i1 = _first_max_index(e_log, valid, rowi)
    e_exp = jnp.where(valid, jnp.exp(e_log - e_max), 0.0)
    e_prob = e_exp / jnp.sum(e_exp, axis=0, keepdims=True)
    v1 = jnp.max(e_prob, axis=0, keepdims=True)
    rest = jnp.logical_and(valid, rowi != i1)
    v2, i2 = _first_max_index(e_prob, rest, rowi)
    denom = v1 + v2
    w_sel = p_g * jnp.where(rowi == i1, v1 / denom, jnp.where(rowi == i2, v2 / denom, 0.0))
    route_ref[:, rows] = jnp.broadcast_to(g_idx, (SUBLANES, n))
    pad = jnp.zeros((LANES - SUBLANES, n), F32)
    gate4_ref[rows, :] = jnp.concatenate([w_sel, pad], axis=0).T
    return jnp.sum(jnp.where(rowi == g_idx, 1.0, 0.0), axis=1, keepdims=True)


def _merge(l, xs, attn_ctx, attn_lat, sgu_n, cbz, conv_w, on, w_out, mod, g2, wr_cat, wr_hi, rb):
    lay = lambda *rest: (lambda i: (l,) + rest)
    rows8 = TM // SUBLANES
    return pl.pallas_call(
        functools.partial(_merge_kernel, split_x=len(xs) == 2),
        grid=(N_TILES,),
        in_specs=_x_specs(xs) + [
            pl.BlockSpec((TM, D_ATTN), lambda i: (_ctx_tile(i), 0)),
            pl.BlockSpec((TM, D_ATTN), lambda i: (_lat_tile(i), 0)),
            pl.BlockSpec((TM, D_SGU), lambda i: (i, 0)),
            pl.BlockSpec((TM, 2 * D_CONV), lambda i: (i, 0)),
            pl.BlockSpec((SUBLANES, 2 * D_CONV), lambda i: (jnp.maximum(i * rows8 - 1, 0), 0)),
            pl.BlockSpec((SUBLANES, 2 * D_CONV),
                         lambda i: (jnp.minimum((i + 1) * rows8, T_ALL // SUBLANES - 1), 0)),
            pl.BlockSpec((None, 3, D_CONV), lay(0, 0)),
            pl.BlockSpec((None, 1, D_MIX), lay(0, 0)),
            pl.BlockSpec((None, D_MIX, D_MODEL), lay(0, 0)),
            _MOD_SPEC(l, 1),
            pl.BlockSpec((None, 1, D_MODEL), lay(0, 0)),
            pl.BlockSpec((None, D_MODEL, 2 * ROUTER_COLS), lay(0, 0)),
            pl.BlockSpec((None, D_MODEL, ROUTER_COLS), lay(0, 0)),
            pl.BlockSpec((None, 1, ROUTER_COLS), lay(0, 0)),
        ],
        out_specs=[
            pl.BlockSpec((TM, D_MODEL), lambda i: (i, 0)),
            pl.BlockSpec((TM, D_MODEL), lambda i: (i, 0)),
            pl.BlockSpec((None, SUBLANES, TM), lambda i: (i, 0, 0)),
            pl.BlockSpec((TM, LANES), lambda i: (i, 0)),
            pl.BlockSpec((None, SUBLANES, LANES), lambda i: (i, 0, 0)),
        ],
        out_shape=[
            jax.ShapeDtypeStruct((T_ALL, D_MODEL), F32),
            jax.ShapeDtypeStruct((T_ALL, D_MODEL), BF16),
            jax.ShapeDtypeStruct((N_TILES, SUBLANES, TM), F32),
            jax.ShapeDtypeStruct((T_ALL, LANES), F32),
            jax.ShapeDtypeStruct((N_TILES, SUBLANES, LANES), jnp.int32),
        ],
        compiler_params=_params(1),
        name=f"merge_l{l}",
    )(*xs, attn_ctx, attn_lat, sgu_n, cbz, cbz, cbz, conv_w, on, w_out, mod, g2, wr_cat, wr_hi, rb)


RB = 144
SORT_ROWS = TM + LANES
SORT_ROWS_PAD = -(-(SORT_ROWS + RB) // BF16_ROWS) * BF16_ROWS
MOE_VMEM_LIMIT = 56 * 1024 * 1024


def _split3(x):
    hi = x.astype(BF16)
    r1 = x - hi.astype(F32)
    mid = r1.astype(BF16)
    lo = (r1 - mid.astype(F32)).astype(BF16)
    return hi, mid, lo


def _moe_kernel(cnt_ref, h_ref, route_ref, gate4_ref, tri_ref, w1_ref, w3_ref, w2_ref, x1_ref, mod_ref, fn_ref,
                *refs, final):
    out_refs, (xs_ref, gs_ref, zs_ref) = refs[:-3], refs[-3:]
    i = pl.program_id(0)
    gate2 = _mod_part(mod_ref, 5, _mod_row_of_tile(i))
    counts = [cnt_ref[i, g] for g in range(N_EXPERT_GROUPS)]
    starts = [jnp.int32(0)]
    for g in range(N_EXPERT_GROUPS - 1):
        starts.append(starts[-1] + (counts[g] + (BF16_ROWS - 1)) // BF16_ROWS * BF16_ROWS)

    g_idx = route_ref[0:1, :]
    rowi = lax.broadcasted_iota(jnp.int32, (SUBLANES, TM), 0).astype(F32)
    onehot = jnp.where(rowi == g_idx, 1.0, 0.0)
    incl = jnp.dot(onehot.astype(BF16), tri_ref[...], preferred_element_type=F32)
    pos = jnp.sum(onehot * incl, axis=0, keepdims=True) - 1.0
    for g in range(1, N_EXPERT_GROUPS):
        pos = pos + jnp.where(g_idx == float(g), starts[g].astype(F32), 0.0)
    sub = lax.broadcasted_iota(jnp.int32, (SORT_ROWS, TM), 0).astype(F32)
    perm = jnp.where(sub == pos, 1.0, 0.0).astype(BF16)
    pos_col = jnp.broadcast_to(pos, (LANES, TM)).T
    lane = lax.broadcasted_iota(jnp.int32, (TM, SORT_ROWS), 1).astype(F32)
    perm_t = jnp.where(lane == jnp.concatenate([pos_col] * (SORT_ROWS // LANES), axis=1), 1.0, 0.0).astype(BF16)

    xs_ref[:SORT_ROWS, :] = jnp.dot(perm, h_ref[...], preferred_element_type=F32).astype(BF16)
    xs_ref[SORT_ROWS:, :] = jnp.zeros((SORT_ROWS_PAD - SORT_ROWS, D_MODEL), BF16)
    hi, mid, lo = _split3(gate4_ref[...])
    packed = (hi.astype(F32) + pltpu.roll(mid.astype(F32), EXPERTS_PER_GROUP, 1)
              + pltpu.roll(lo.astype(F32), 2 * EXPERTS_PER_GROUP, 1)).astype(BF16)
    gsp = jnp.dot(perm, packed, preferred_element_type=F32)
    gs_ref[:SORT_ROWS, :] = (gsp + pltpu.roll(gsp, LANES - EXPERTS_PER_GROUP, 1)
                             + pltpu.roll(gsp, LANES - 2 * EXPERTS_PER_GROUP, 1))
    gs_ref[SORT_ROWS:, :] = jnp.zeros((SORT_ROWS_PAD - SORT_ROWS, LANES), F32)
    zs_ref[...] = jnp.zeros((SORT_ROWS_PAD, D_MODEL), BF16)

    for g in range(N_EXPERT_GROUPS):
        def block_ffn(blk, carry, g=g):
            row0 = pl.multiple_of(starts[g] + blk * RB, BF16_ROWS)
            xb = xs_ref[pl.ds(row0, RB), :]
            gb = gs_ref[pl.ds(row0, RB), :]
            cols = []
            for e in range(EXPERTS_PER_GROUP):
                a = jnp.dot(xb, w1_ref[g * EXPERTS_PER_GROUP + e], preferred_element_type=F32)
                b = jnp.dot(xb, w3_ref[g * EXPERTS_PER_GROUP + e], preferred_element_type=F32)
                cols.append((a / (1.0 + jnp.exp(-a)) * b * gb[:, e:e + 1]).astype(BF16))
            z = jnp.dot(jnp.concatenate(cols, axis=1), w2_ref[g], preferred_element_type=F32)
            zs_ref[pl.ds(row0, RB), :] = z.astype(BF16)
            return carry

        lax.fori_loop(0, (counts[g] + (RB - 1)) // RB, block_ffn, 0)

    y = jnp.dot(perm_t, zs_ref[:SORT_ROWS, :], preferred_element_type=F32)
    out = x1_ref[...] + gate2 * y

    if final:
        octx_ref, olat_ref = out_refs
        out = _rms(out, fn_ref[...])
        olat_ref[...] = out

        @pl.when(i < N_CTX_TILES)
        def _():
            octx_ref[...] = out
    else:
        out_refs[0][...] = out


def _moe(l, counts, h2, route, gate4, tri, w1, w3, w2, x1, mod, final_norm, final):
    tile = pl.BlockSpec((TM, D_MODEL), lambda i, c: (i, 0))
    resident = pl.Buffered(1)
    if final:
        out_specs = [pl.BlockSpec((TM, D_MODEL), lambda i, c: (_ctx_tile(i), 0)),
                     pl.BlockSpec((TM, D_MODEL), lambda i, c: (_lat_tile(i), 0))]
        out_shape = [jax.ShapeDtypeStruct((T_CTX, D_MODEL), F32), jax.ShapeDtypeStruct((T_LAT, D_MODEL), F32)]
    else:
        out_specs = tile
        out_shape = jax.ShapeDtypeStruct((T_ALL, D_MODEL), F32)
    grid_spec = pltpu.PrefetchScalarGridSpec(
        num_scalar_prefetch=1,
        grid=(N_TILES,),
        in_specs=[
            tile,
            pl.BlockSpec((None, SUBLANES, TM), lambda i, c: (i, 0, 0)),
            pl.BlockSpec((TM, LANES), lambda i, c: (i, 0)),
            pl.BlockSpec((TM, TM), lambda i, c: (0, 0), pipeline_mode=resident),
            pl.BlockSpec((N_EXPERTS, D_MODEL, D_EXPERT), lambda i, c: (0, 0, 0), pipeline_mode=resident),
            pl.BlockSpec((N_EXPERTS, D_MODEL, D_EXPERT), lambda i, c: (0, 0, 0), pipeline_mode=resident),
            pl.BlockSpec((N_EXPERT_GROUPS, D_GROUP_HID, D_MODEL), lambda i, c: (0, 0, 0), pipeline_mode=resident),
            tile,
            pl.BlockSpec((None, N_MOD_PARTS, N_MOD_ROWS, D_MODEL), lambda i, c: (l, 0, 0, 0)),
            pl.BlockSpec((1, D_MODEL), lambda i, c: (0, 0)),
        ],
        out_specs=out_specs,
        scratch_shapes=[pltpu.VMEM((SORT_ROWS_PAD, D_MODEL), BF16),
                        pltpu.VMEM((SORT_ROWS_PAD, LANES), F32),
                        pltpu.VMEM((SORT_ROWS_PAD, D_MODEL), BF16)],
    )
    return pl.pallas_call(
        functools.partial(_moe_kernel, final=final),
        grid_spec=grid_spec,
        out_shape=out_shape,
        compiler_params=pltpu.CompilerParams(dimension_semantics=("arbitrary",),
                                             vmem_limit_bytes=MOE_VMEM_LIMIT),
        name=f"moe_l{l}",
    )(counts, h2, route, gate4, tri, w1, w3, w2, x1, mod, final_norm)


def _rope_tables():
    rows = DEC_SEQ // GRID_W
    row = np.repeat(np.arange(rows, dtype=np.float32), GRID_W)
    col = np.tile(np.arange(GRID_W, dtype=np.float32), rows)
    half = ROPE_AXIS_DIM // 2
    inv_freq = (1.0 / (np.float32(ROPE_THETA) ** (np.arange(half, dtype=np.float32) * np.float32(2.0)
                                                  / np.float32(ROPE_AXIS_DIM)))).astype(np.float32)
    ar = row[:, None] * inv_freq
    ac = col[:, None] * inv_freq
    cos64 = np.concatenate([np.cos(ar), np.cos(ar), np.cos(ac), np.cos(ac)], axis=1)
    sin64 = np.concatenate([-np.sin(ar), np.sin(ar), -np.sin(ac), np.sin(ac)], axis=1)
    cos_t = np.concatenate([np.ones((TM, LANES), np.float32), np.tile(cos64, (1, 2))], axis=0)
    sin_t = np.concatenate([np.zeros((TM, LANES), np.float32), np.tile(sin64, (1, 2))], axis=0)
    return jnp.asarray(cos_t, F32), jnp.asarray(sin_t, F32)


def _router_weights(router_g_w, router_g_b, router_e_w, router_e_b):
    zw = jnp.zeros((DEPTH, D_MODEL, SUBLANES - N_EXPERT_GROUPS), F32)
    zb = jnp.zeros((DEPTH, SUBLANES - N_EXPERT_GROUPS), F32)
    w_cols, b_cols = [router_g_w, zw], [router_g_b, zb]
    for g in range(N_EXPERT_GROUPS):
        w_cols += [router_e_w[:, g], zw]
        b_cols += [router_e_b[:, g], zb]
    used = (1 + N_EXPERT_GROUPS) * SUBLANES
    w_cols.append(jnp.zeros((DEPTH, D_MODEL, ROUTER_COLS - used), F32))
    b_cols.append(jnp.zeros((DEPTH, ROUTER_COLS - used), F32))
    w = jnp.concatenate(w_cols, axis=2)
    b = jnp.concatenate(b_cols, axis=1).reshape(DEPTH, 1, ROUTER_COLS)
    w_hi = w.astype(BF16)
    w_lo = (w - w_hi.astype(F32)).astype(BF16)
    return jnp.concatenate([w_hi, w_lo], axis=2), w_hi, b


def kernel(x_prompt, x_sample, cache_k, cache_v, c, c_ctx, w_ada, b_ada, norm1, w_in, q_norm, k_norm,
           sgu_w, sgu_b, conv_w, out_norm, w_out, norm2, router_g_w, router_g_b, router_e_w,
           router_e_b, moe_w1, moe_w3, moe_w2, final_norm):
    xs = (x_prompt.reshape(T_CTX, D_MODEL), x_sample.reshape(T_LAT, D_MODEL))

    cvecs = jnp.concatenate([c_ctx[None, :], c, jnp.zeros((N_MOD_ROWS - 1 - DEC_BATCH, D_MODEL), F32)], axis=0)
    mod = _modulation(cvecs.T, w_ada, b_ada)

    w_in_b = w_in.astype(BF16)
    w_out_b = w_out.astype(BF16)
    g1 = norm1.reshape(DEPTH, 1, D_MODEL)
    g2 = norm2.reshape(DEPTH, 1, D_MODEL)
    on = out_norm.reshape(DEPTH, 1, D_MIX)
    qg = jnp.tile(q_norm, (1, N_HEADS)).reshape(DEPTH, 1, D_ATTN)
    kg = jnp.tile(k_norm, (1, N_KV_HEADS)).reshape(DEPTH, 1, D_KV)
    cos_t, sin_t = _rope_tables()
    bq = jnp.asarray(np.kron(np.eye(N_HEADS), np.full((HEAD_DIM, HEAD_DIM), 1.0 / HEAD_DIM)), BF16)
    sgu_w_pairs = sgu_w.astype(BF16).reshape(DEPTH, N_SGU_GROUPS // 2, 2, CHUNK, CHUNK)
    sgu_w_pairs = jnp.concatenate([sgu_w_pairs[:, :, 0], sgu_w_pairs[:, :, 1]], axis=-1)
    sgu_bias = jnp.repeat(jnp.swapaxes(sgu_b, 1, 2), SGU_GROUP_DIM, axis=2)
    wr_cat, wr_hi, rb = _router_weights(router_g_w, router_g_b, router_e_w, router_e_b)
    fn = final_norm.reshape(1, D_MODEL)
    tri = jnp.asarray(np.triu(np.ones((TM, TM), np.float32)), BF16)
    ck = cache_k.reshape(DEC_BATCH, DEPTH, PAST_LEN, D_KV)
    cv = cache_v.reshape(DEC_BATCH, DEPTH, PAST_LEN, D_KV)

    ctx_ks, ctx_vs = [], []
    for l in range(DEPTH):
        qt, k, vt, k32, v32, sgu_n, cbz = _proj(l, xs, mod, g1, w_in_b, qg, kg, cos_t, sin_t, bq,
                                                sgu_w_pairs, sgu_bias, on)
        attn_ctx = _attn_ctx(l, qt, k, vt)
        attn_lat, w1, w3, w2 = _attn_lat(l, qt, k, vt, ck, cv, qg, kg, moe_w1, moe_w3, moe_w2)
        w2 = w2.reshape(N_EXPERT_GROUPS, D_GROUP_HID, D_MODEL)
        x1, h2, route, gate4, cnt = _merge(l, xs, attn_ctx, attn_lat, sgu_n, cbz, conv_w, on, w_out_b, mod, g2,
                                           wr_cat, wr_hi, rb)
        counts = cnt[:, :N_EXPERT_GROUPS, 0]
        out = _moe(l, counts, h2, route, gate4, tri, w1, w3, w2, x1, mod, fn, final=(l == DEPTH - 1))
        xs = (out,)
        ctx_ks.append(k32.reshape(BATCH, SEQ, N_KV_HEADS, HEAD_DIM))
        ctx_vs.append(v32.reshape(BATCH, SEQ, N_KV_HEADS, HEAD_DIM))

    y_prompt, y_sample = out
    return (y_prompt.reshape(BATCH, SEQ, D_MODEL), y_sample.reshape(DEC_BATCH, DEC_SEQ, D_MODEL),
            jnp.stack(ctx_ks, axis=1), jnp.stack(ctx_vs, axis=1))
```

```python
import functools
import math

import jax
import jax.numpy as jnp
import numpy as np
from jax import lax
from jax.experimental import pallas as pl
from jax.experimental.pallas import tpu as pltpu

D_MODEL = 1024
BATCH = 16
SEQ = 256
DEPTH = 2
DEC_BATCH = 2
DEC_SEQ = 4096
PAST_LEN = 512
GRID_W = 64
N_HEADS = 8
N_KV_HEADS = 2
HEAD_DIM = 64
ROPE_AXIS_DIM = HEAD_DIM // 2
ROPE_THETA = 10000.0
D_ATTN = N_HEADS * HEAD_DIM
D_KV = N_KV_HEADS * HEAD_DIM
CHUNK = 128
N_SGU_GROUPS = 4
SGU_GROUP_DIM = 64
D_SGU = N_SGU_GROUPS * SGU_GROUP_DIM
D_CONV = 256
D_MIX = D_ATTN + D_SGU + D_CONV
D_IN = D_ATTN + 2 * D_KV + 2 * D_SGU + 3 * D_CONV
N_EXPERT_GROUPS = 4
EXPERTS_PER_GROUP = 4
N_EXPERTS = N_EXPERT_GROUPS * EXPERTS_PER_GROUP
D_EXPERT = 256
D_GROUP_HID = EXPERTS_PER_GROUP * D_EXPERT
EPS = 1e-6

T_CTX = BATCH * SEQ
T_LAT = DEC_BATCH * DEC_SEQ
T_ALL = T_CTX + T_LAT
TM = 512
N_TILES = T_ALL // TM
N_CTX_TILES = T_CTX // TM
LAT_TILES_PER_BATCH = DEC_SEQ // TM
TM_MOE = 1024
LANES = 128
SUBLANES = 8
BF16_ROWS = 16
N_MOD_ROWS = 8
N_MOD_PARTS = 6
ROUTER_COLS = 128
VMEM_LIMIT = 48 * 1024 * 1024

OFF_Q = 0
OFF_KV = D_ATTN
OFF_SGU = OFF_KV + 2 * D_KV
OFF_CONV = OFF_SGU + 2 * D_SGU

BF16 = jnp.bfloat16
F32 = jnp.float32
NEG_BIG = -1e30
Q_SCALE = HEAD_DIM ** -0.5 * math.log2(math.e)


def _params(n_grid_dims):
    return pltpu.CompilerParams(
        dimension_semantics=("arbitrary",) * n_grid_dims,
        vmem_limit_bytes=VMEM_LIMIT,
    )


def _mod_row_of_tile(i, tile=TM):
    n_ctx = T_CTX // tile
    return jnp.where(i < n_ctx, 0, 1 + (i - n_ctx) // (DEC_SEQ // tile))


def _mod_part(mod_ref, part, row):
    return mod_ref[part, pl.ds(row, 1), :]


def _rope_block_of_tile(i):
    return jnp.where(i < N_CTX_TILES, 0, 1 + (i - N_CTX_TILES) % LAT_TILES_PER_BATCH)


def _rms(x, gain):
    ms = jnp.mean(x * x, axis=-1, keepdims=True)
    return x * lax.rsqrt(ms + EPS) * gain


def _ctx_tile(i):
    return jnp.minimum(i, N_CTX_TILES - 1)


def _lat_tile(i):
    return jnp.maximum(i - N_CTX_TILES, 0)


MOD_PARTS_PER_STEP = 2
MOD_TN = MOD_PARTS_PER_STEP * D_MODEL


def _mod_kernel(ct_ref, w_ref, b_ref, o_ref):
    c = ct_ref[...]
    s = c / (1.0 + jnp.exp(-c))
    row = lax.broadcasted_iota(jnp.int32, (N_MOD_ROWS, D_MODEL), 0)
    for part in range(MOD_PARTS_PER_STEP):
        cols = slice(part * D_MODEL, (part + 1) * D_MODEL)
        w = w_ref[:, cols]
        out = jnp.zeros((N_MOD_ROWS, D_MODEL), F32)
        for r in range(1 + DEC_BATCH):
            acc = jnp.sum(w * s[:, r:r + 1], axis=0, keepdims=True)
            out = jnp.where(row == r, acc, out)
        o_ref[part] = out + b_ref[:, cols]


def _modulation(cvecs_t, w_ada, b_ada):
    n_col = N_MOD_PARTS * D_MODEL
    return pl.pallas_call(
        _mod_kernel,
        grid=(DEPTH, n_col // MOD_TN),
        in_specs=[
            pl.BlockSpec((D_MODEL, N_MOD_ROWS), lambda l, j: (0, 0)),
            pl.BlockSpec((None, D_MODEL, MOD_TN), lambda l, j: (l, 0, j)),
            pl.BlockSpec((None, 1, MOD_TN), lambda l, j: (l, 0, j)),
        ],
        out_specs=pl.BlockSpec((None, MOD_PARTS_PER_STEP, N_MOD_ROWS, D_MODEL), lambda l, j: (l, j, 0, 0)),
        out_shape=jax.ShapeDtypeStruct((DEPTH, N_MOD_PARTS, N_MOD_ROWS, D_MODEL), F32),
        compiler_params=_params(2),
        name="adaln_modulation",
    )(cvecs_t, w_ada, b_ada.reshape(DEPTH, 1, n_col))


_MOD_SPEC = lambda l, nd: pl.BlockSpec(
    (None, N_MOD_PARTS, N_MOD_ROWS, D_MODEL), lambda *idx: (l, 0, 0, 0))


def _swap16(x):
    lane = lax.broadcasted_iota(jnp.int32, x.shape, 1)
    up = pltpu.roll(x, 16, 1)
    down = pltpu.roll(x, LANES - 16, 1)
    return jnp.where((lane & 16) != 0, up, down)


def _rope(x, cos_t, sin_t):
    cols = []
    for j in range(x.shape[1] // LANES):
        xc = x[:, j * LANES:(j + 1) * LANES]
        cols.append(xc * cos_t + _swap16(xc) * sin_t)
    return cols[0] if len(cols) == 1 else jnp.concatenate(cols, axis=1)


def _proj_kernel(*refs, split_x):
    n_x = 2 if split_x else 1
    x_refs, refs, (scr_even, scr_odd) = refs[:n_x], refs[n_x:-2], refs[-2:]
    step = functools.partial(_proj_step, x_refs, refs, split_x)
    i = pl.program_id(0)

    @pl.when(i == 0)
    def _():
        scr_odd[...] = jnp.zeros((TM, D_IN), F32)

    @pl.when(i % 2 == 0)
    def _():
        step(scr_even, scr_odd)

    @pl.when(i % 2 == 1)
    def _():
        step(scr_odd, scr_even)


def _proj_step(x_refs, refs, split_x, new_ref, proj):
    (mod_ref, g1_ref, w_ref, qg_ref, kg_ref, cos_ref, sin_ref, bq_ref, sw_ref, sb_ref, on_ref,
     qt_ref, k_ref, vt_ref, k32_ref, v32_ref, sgu_ref, cbz_ref) = refs
    i = pl.program_id(0)
    cur = _stage1_tile(i)

    is_ctx = i <= N_CTX_TILES
    cos_t = cos_ref[...]
    sin_t = sin_ref[...]

    q = proj[:, OFF_Q:OFF_Q + D_ATTN]
    qms = jnp.dot((q * q).astype(BF16), bq_ref[...], preferred_element_type=F32)
    qn = q * lax.rsqrt(qms + EPS) * qg_ref[...]
    qr = _rope(qn, cos_t, sin_t) * Q_SCALE
    qt_ref[...] = qr.T.astype(BF16)

    k = proj[:, OFF_KV:OFF_KV + D_KV]
    v = proj[:, OFF_KV + D_KV:OFF_KV + 2 * D_KV]
    kms = jnp.dot((k * k).astype(BF16), bq_ref[:D_KV, :D_KV], preferred_element_type=F32)
    kn = k * lax.rsqrt(kms + EPS) * kg_ref[...]
    k_ref[...] = _rope(kn, cos_t, sin_t).astype(BF16)
    vt_ref[...] = v.T.astype(BF16)

    su = proj[:, OFF_SGU:OFF_SGU + D_SGU]
    sv = proj[:, OFF_SGU + D_SGU:OFF_SGU + 2 * D_SGU]
    n_chunks = TM // CHUNK
    sv_wide = jnp.concatenate([sv[n * CHUNK:(n + 1) * CHUNK, :] for n in range(n_chunks)], axis=1)
    grp = lax.broadcasted_iota(jnp.int32, (CHUNK, n_chunks * D_SGU), 1) // SGU_GROUP_DIM % N_SGU_GROUPS
    mixed_wide = jnp.zeros((CHUNK, n_chunks * D_SGU), F32)
    for pair in range(N_SGU_GROUPS // 2):
        rhs = jnp.concatenate(
            [jnp.where(grp == 2 * pair, sv_wide, 0.0), jnp.where(grp == 2 * pair + 1, sv_wide, 0.0)],
            axis=0).astype(BF16)
        mixed_wide = mixed_wide + jnp.dot(sw_ref[pair], rhs, preferred_element_type=F32)
    on_sgu = on_ref[:, D_ATTN:D_ATTN + D_SGU]
    for n in range(n_chunks):
        sgu = su[n * CHUNK:(n + 1) * CHUNK, :] * (mixed_wide[:, n * D_SGU:(n + 1) * D_SGU] + sb_ref[...])
        sgu_ref[n * CHUNK:(n + 1) * CHUNK, :] = _rms(sgu, on_sgu).astype(BF16)

    cbz_ref[:, :D_CONV] = proj[:, OFF_CONV:OFF_CONV + D_CONV]
    cbz_ref[:, D_CONV:] = (proj[:, OFF_CONV + D_CONV:OFF_CONV + 2 * D_CONV]
                           * proj[:, OFF_CONV + 2 * D_CONV:OFF_CONV + 3 * D_CONV])

    x = jnp.where(cur < N_CTX_TILES, x_refs[0][...], x_refs[1][...]) if split_x else x_refs[0][...]
    mrow = _mod_row_of_tile(cur)
    h = _rms(x, g1_ref[...] * (1.0 + _mod_part(mod_ref, 1, mrow))) + _mod_part(mod_ref, 0, mrow)
    new_ref[...] = jnp.dot(h.astype(BF16), w_ref[...], preferred_element_type=F32)

    @pl.when(is_ctx)
    def _():
        k32_ref[...] = kn
        v32_ref[...] = v


def _x_specs(xs, tile_of_step=lambda i: i):
    if len(xs) == 1:
        return [pl.BlockSpec((TM, D_MODEL), lambda i: (tile_of_step(i), 0))]
    return [pl.BlockSpec((TM, D_MODEL), lambda i: (_ctx_tile(tile_of_step(i)), 0)),
            pl.BlockSpec((TM, D_MODEL), lambda i: (_lat_tile(tile_of_step(i)), 0))]


def _stage1_tile(i):
    return jnp.minimum(i, N_TILES - 1)


def _stage2_tile(i):
    return jnp.maximum(i - 1, 0)


def _proj(l, xs, mod, g1, w_in, qg, kg, cos_t, sin_t, bq, sgu_w, sgu_b, on):
    lay = lambda *rest: (lambda i: (l,) + rest)
    s2 = _stage2_tile
    return pl.pallas_call(
        functools.partial(_proj_kernel, split_x=len(xs) == 2),
        grid=(N_TILES + 1,),
        in_specs=_x_specs(xs, _stage1_tile) + [
            _MOD_SPEC(l, 1),
            pl.BlockSpec((None, 1, D_MODEL), lay(0, 0)),
            pl.BlockSpec((None, D_MODEL, D_IN), lay(0, 0)),
            pl.BlockSpec((None, 1, D_ATTN), lay(0, 0)),
            pl.BlockSpec((None, 1, D_KV), lay(0, 0)),
            pl.BlockSpec((TM, LANES), lambda i: (_rope_block_of_tile(s2(i)), 0)),
            pl.BlockSpec((TM, LANES), lambda i: (_rope_block_of_tile(s2(i)), 0)),
            pl.BlockSpec((D_ATTN, D_ATTN), lambda i: (0, 0)),
            pl.BlockSpec((None, N_SGU_GROUPS // 2, CHUNK, 2 * CHUNK), lay(0, 0, 0)),
            pl.BlockSpec((None, CHUNK, D_SGU), lay(0, 0)),
            pl.BlockSpec((None, 1, D_MIX), lay(0, 0)),
        ],
        out_specs=[
            pl.BlockSpec((None, D_ATTN, TM), lambda i: (s2(i), 0, 0)),
            pl.BlockSpec((TM, D_KV), lambda i: (s2(i), 0)),
            pl.BlockSpec((None, D_KV, TM), lambda i: (s2(i), 0, 0)),
            pl.BlockSpec((TM, D_KV), lambda i: (_ctx_tile(s2(i)), 0)),
            pl.BlockSpec((TM, D_KV), lambda i: (_ctx_tile(s2(i)), 0)),
            pl.BlockSpec((TM, D_SGU), lambda i: (s2(i), 0)),
            pl.BlockSpec((TM, 2 * D_CONV), lambda i: (s2(i), 0)),
        ],
        out_shape=[
            jax.ShapeDtypeStruct((N_TILES, D_ATTN, TM), BF16),
            jax.ShapeDtypeStruct((T_ALL, D_KV), BF16),
            jax.ShapeDtypeStruct((N_TILES, D_KV, TM), BF16),
            jax.ShapeDtypeStruct((T_CTX, D_KV), F32),
            jax.ShapeDtypeStruct((T_CTX, D_KV), F32),
            jax.ShapeDtypeStruct((T_ALL, D_SGU), BF16),
            jax.ShapeDtypeStruct((T_ALL, 2 * D_CONV), F32),
        ],
        scratch_shapes=[pltpu.VMEM((TM, D_IN), F32), pltpu.VMEM((TM, D_IN), F32)],
        compiler_params=_params(1),
        name=f"proj_l{l}",
    )(*xs, mod, g1, w_in, qg, kg, cos_t, sin_t, bq, sgu_w, sgu_b, on)


ACC_ROWS = HEAD_DIM + BF16_ROWS
N_PAIRS = N_HEADS // 2
PAIRS_PER_KV = N_PAIRS // N_KV_HEADS


def _pair_queries(qt, kv_idx, tq):
    qf = qt.astype(F32)
    mine = lax.broadcasted_iota(jnp.int32, (2 * HEAD_DIM, tq), 0) // HEAD_DIM == kv_idx
    qe = qf[:HEAD_DIM]
    qo = qf[HEAD_DIM:]
    rhs = jnp.concatenate(
        [jnp.where(mine, jnp.concatenate([qe, qe], axis=0), 0.0),
         jnp.where(mine, jnp.concatenate([qo, qo], axis=0), 0.0)], axis=1)
    return rhs.astype(BF16)


def _with_ones(vt_c):
    return jnp.concatenate([vt_c, jnp.ones((BF16_ROWS, vt_c.shape[1]), BF16)], axis=0)


def _flash_pairs(rhs_list, chunk_lists, tq, shift=None):
    n_pairs, n_chunks = len(rhs_list), len(chunk_lists[0])
    scores = lambda s, c: jnp.dot(chunk_lists[s][c][0](), rhs_list[s], preferred_element_type=F32)
    m = [jnp.full((1, 2 * tq), NEG_BIG, F32)] * n_pairs
    acc = [jnp.zeros((ACC_ROWS, 2 * tq), F32)] * n_pairs
    st = [scores(s, 0) for s in range(n_pairs)]
    for c in range(n_chunks):
        for s in range(n_pairs):
            if shift is None:
                m_new = jnp.maximum(m[s], jnp.max(st[s], axis=0, keepdims=True))
            st_next = scores(s, c + 1) if c + 1 < n_chunks else None
            if shift is None:
                alpha = jnp.exp2(m[s] - m_new)
                p = jnp.exp2(st[s] - m_new).astype(BF16)
                acc[s] = alpha * acc[s] + jnp.dot(chunk_lists[s][c][1](), p, preferred_element_type=F32)
                m[s] = m_new
            else:
                p = jnp.exp2(st[s] - shift).astype(BF16)
                acc[s] = acc[s] + jnp.dot(chunk_lists[s][c][1](), p, preferred_element_type=F32)
            st[s] = st_next
    outs = []
    for s in range(n_pairs):
        o = acc[s][:HEAD_DIM] * (1.0 / acc[s][HEAD_DIM:HEAD_DIM + 1])
        outs.append(jnp.concatenate([o[:, :tq], o[:, tq:]], axis=0))
    return outs


def _attn_ctx_kernel(qt_ref, k_ref, vt_ref, o_ref):
    tq = SEQ
    rhs_list, chunk_lists = [], []
    for pair in range(N_PAIRS):
        kv_idx = pair // PAIRS_PER_KV
        rhs_list.append(_pair_queries(qt_ref[pair * LANES:(pair + 1) * LANES, :], kv_idx, tq))
        chunk_lists.append([(lambda: k_ref[...],
                             lambda kv_idx=kv_idx: _with_ones(vt_ref[kv_idx * HEAD_DIM:(kv_idx + 1) * HEAD_DIM, :]))])
    for pair, o_t in enumerate(_flash_pairs(rhs_list, chunk_lists, tq)):
        o_ref[:, pair * LANES:(pair + 1) * LANES] = o_t.T.astype(BF16)


def _attn_ctx(l, qt, k, vt):
    halves = TM // SEQ
    return pl.pallas_call(
        _attn_ctx_kernel,
        grid=(BATCH,),
        in_specs=[
            pl.BlockSpec((None, D_ATTN, SEQ), lambda b: (b // halves, 0, b % halves)),
            pl.BlockSpec((SEQ, D_KV), lambda b: (b, 0)),
            pl.BlockSpec((None, D_KV, SEQ), lambda b: (b // halves, 0, b % halves)),
        ],
        out_specs=pl.BlockSpec((SEQ, D_ATTN), lambda b: (b, 0)),
        out_shape=jax.ShapeDtypeStruct((T_CTX, D_ATTN), BF16),
        compiler_params=_params(1),
        name=f"attn_ctx_l{l}",
    )(qt, k, vt)


KC = 512
LAT_PAIRS_PER_STEP = 2


MAX_SHIFT = 60.0
BOUND_SLACK = 1.05


def _score_bound_sq(qg_ref, kg_ref, ck):
    qg = qg_ref[...]
    kg = kg_ref[...]
    gq2 = jnp.max(qg * qg, axis=1, keepdims=True)
    k2 = HEAD_DIM * jnp.max(kg * kg, axis=1, keepdims=True)
    if ck is not None:
        sq = ck * ck
        first = lax.broadcasted_iota(jnp.int32, sq.shape, 1) < HEAD_DIM
        n0 = jnp.sum(jnp.where(first, sq, 0.0), axis=1, keepdims=True)
        n1 = jnp.sum(jnp.where(first, 0.0, sq), axis=1, keepdims=True)
        k2 = jnp.maximum(k2, jnp.max(jnp.maximum(n0, n1), axis=0, keepdims=True))
    return (BOUND_SLACK * Q_SCALE) ** 2 * HEAD_DIM * gq2 * k2


def _run_with_score_bound(run, bound):
    small = bound <= MAX_SHIFT

    @pl.when(small)
    def _():
        run(bound)

    @pl.when(jnp.logical_not(small))
    def _():
        run(None)


def _attn_lat_kernel(qt_ref, k_ref, vt_ref, ck_ref, cv_ref, qg_ref, kg_ref, w1_ref, w3_ref, w2_ref,
                     o_ref, w1b_ref, w3b_ref, w2b_ref, bound_ref):
    tq = TM

    def run(shift):
        w1b_ref[...] = w1_ref[...].astype(BF16)
        w3b_ref[...] = w3_ref[...].astype(BF16)
        w2b_ref[...] = w2_ref[...].astype(BF16)
        cvt = cv_ref[...].T
        rhs_list, chunk_lists = [], []
        for j in range(LAT_PAIRS_PER_STEP):
            kv_idx = (pl.program_id(2) * LAT_PAIRS_PER_STEP + j) // PAIRS_PER_KV
            rhs_list.append(_pair_queries(qt_ref[j * LANES:(j + 1) * LANES, :], kv_idx, tq))
            v_row = pl.multiple_of(kv_idx * HEAD_DIM, HEAD_DIM)

            def latent_chunk(c, v_row=v_row):
                tile, lane0 = (c * KC) // TM, (c * KC) % TM
                return (lambda: k_ref[c * KC:(c + 1) * KC, :],
                        lambda: _with_ones(vt_ref[tile, pl.ds(v_row, HEAD_DIM), lane0:lane0 + KC]))

            def cached_chunk(c, kv_idx=kv_idx):
                return (lambda: ck_ref[c * KC:(c + 1) * KC, :].astype(BF16),
                        lambda: _with_ones(jnp.where(kv_idx == 0, cvt[:HEAD_DIM, c * KC:(c + 1) * KC],
                                                     cvt[HEAD_DIM:, c * KC:(c + 1) * KC]).astype(BF16)))

            chunks = [latent_chunk(c) for c in range(DEC_SEQ // KC)]
            chunks += [cached_chunk(c) for c in range(PAST_LEN // KC)]
            chunk_lists.append(chunks)
        for j, o_t in enumerate(_flash_pairs(rhs_list, chunk_lists, tq, shift)):
            o_ref[:, j * LANES:(j + 1) * LANES] = o_t.T.astype(BF16)

    @pl.when(jnp.logical_and(pl.program_id(1) == 0, pl.program_id(2) == 0))
    def _():
        bound_ref[0] = jnp.sqrt(_score_bound_sq(qg_ref, kg_ref, ck_ref[...]))[0, 0]

    _run_with_score_bound(run, bound_ref[0])


def _attn_lat(l, qt, k, vt, cache_k, cache_v, qg, kg, moe_w1, moe_w3, moe_w2):
    n_q = DEC_SEQ // TM
    first = N_CTX_TILES
    ctx_blocks = T_CTX // DEC_SEQ
    width = LAT_PAIRS_PER_STEP * LANES
    n_p = N_PAIRS // LAT_PAIRS_PER_STEP
    per_expert = DEC_BATCH * n_q * n_p // N_EXPERTS
    step = lambda b, i, p: (b * n_q + i) * n_p + p
    w_in_spec = lambda rows, cols: pl.BlockSpec(
        (None, None, rows // per_expert, cols),
        lambda b, i, p: (l, step(b, i, p) // per_expert, step(b, i, p) % per_expert, 0))
    w_out_spec = lambda rows, cols: pl.BlockSpec(
        (None, rows // per_expert, cols),
        lambda b, i, p: (step(b, i, p) // per_expert, step(b, i, p) % per_expert, 0))
    return pl.pallas_call(
        _attn_lat_kernel,
        grid=(DEC_BATCH, n_q, n_p),
        in_specs=[
            pl.BlockSpec((None, width, TM), lambda b, i, p: (first + b * n_q + i, p, 0)),
            pl.BlockSpec((DEC_SEQ, D_KV), lambda b, i, p: (ctx_blocks + b, 0)),
            pl.BlockSpec((n_q, D_KV, TM), lambda b, i, p: (ctx_blocks + b, 0, 0)),
            pl.BlockSpec((None, None, PAST_LEN, D_KV), lambda b, i, p: (b, l, 0, 0)),
            pl.BlockSpec((None, None, PAST_LEN, D_KV), lambda b, i, p: (b, l, 0, 0)),
            pl.BlockSpec((None, 1, D_ATTN), lambda b, i, p: (l, 0, 0)),
            pl.BlockSpec((None, 1, D_KV), lambda b, i, p: (l, 0, 0)),
            w_in_spec(D_MODEL, D_EXPERT),
            w_in_spec(D_MODEL, D_EXPERT),
            w_in_spec(D_EXPERT, D_MODEL),
        ],
        out_specs=[
            pl.BlockSpec((TM, width), lambda b, i, p: (b * n_q + i, p)),
            w_out_spec(D_MODEL, D_EXPERT),
            w_out_spec(D_MODEL, D_EXPERT),
            w_out_spec(D_EXPERT, D_MODEL),
        ],
        out_shape=[
            jax.ShapeDtypeStruct((T_LAT, D_ATTN), BF16),
            jax.ShapeDtypeStruct((N_EXPERTS, D_MODEL, D_EXPERT), BF16),
            jax.ShapeDtypeStruct((N_EXPERTS, D_MODEL, D_EXPERT), BF16),
            jax.ShapeDtypeStruct((N_EXPERTS, D_EXPERT, D_MODEL), BF16),
        ],
        scratch_shapes=[pltpu.SMEM((1,), F32)],
        compiler_params=_params(3),
        name=f"attn_lat_l{l}",
    )(qt, k, vt, cache_k, cache_v, qg, kg, moe_w1, moe_w3, moe_w2)


def _first_max_index(vals, valid, rowf):
    masked = jnp.where(valid, vals, -jnp.inf)
    mx = jnp.max(masked, axis=0, keepdims=True)
    idx = jnp.min(jnp.where(masked == mx, rowf, float(SUBLANES)), axis=0, keepdims=True)
    return mx, idx


def _merge_kernel(*refs, split_x):
    n_x = 2 if split_x else 1
    x_refs, refs = refs[:n_x], refs[n_x:]
    (actx_ref, alat_ref, sgu_ref, cbz_ref, prev_ref, next_ref, cw_ref, on_ref,
     wo_ref, mod_ref, g2_ref, wr_ref, wrhi_ref, rb_ref, x1_ref, h2_ref, route_ref, gate4_ref, cnt_ref) = refs
    cur = pl.program_id(0)
    is_ctx = cur < N_CTX_TILES
    mrow = _mod_row_of_tile(cur)
    on = on_ref[...]

    cbz = cbz_ref[...]
    cb = cbz[:, :D_CONV]
    z = cbz[:, D_CONV:]
    seq_mask = jnp.where(is_ctx, SEQ - 1, DEC_SEQ - 1)
    row = lax.broadcasted_iota(jnp.int32, (TM, D_CONV), 0)
    pos = (cur * TM + row) & seq_mask
    z_prev = jnp.where(row == 0, prev_ref[SUBLANES - 1:SUBLANES, D_CONV:], pltpu.roll(z, 1, 0))
    z_next = jnp.where(row == TM - 1, next_ref[0:1, D_CONV:], pltpu.roll(z, TM - 1, 0))
    z_prev = jnp.where(pos == 0, 0.0, z_prev)
    z_next = jnp.where(pos == seq_mask, 0.0, z_next)
    cw = cw_ref[...]
    conv = cb * (z_prev * cw[0:1, :] + z * cw[1:2, :] + z_next * cw[2:3, :])
    conv_n = _rms(conv, on[:, D_ATTN + D_SGU:]).astype(BF16)

    counts = jnp.zeros((SUBLANES, 1), F32)
    for sub in range(TM // MERGE_ROWS):
        rows = slice(sub * MERGE_ROWS, (sub + 1) * MERGE_ROWS)
        attn = jnp.where(is_ctx, actx_ref[rows, :], alat_ref[rows, :]).astype(F32)
        attn_n = _rms(attn, on[:, :D_ATTN]).astype(BF16)
        merged = jnp.concatenate([attn_n, sgu_ref[rows, :], conv_n[rows, :]], axis=1)
        x = jnp.where(is_ctx, x_refs[0][rows, :], x_refs[1][rows, :]) if split_x else x_refs[0][rows, :]
        x1 = x + _mod_part(mod_ref, 2, mrow) * jnp.dot(merged, wo_ref[...], preferred_element_type=F32)
        x1_ref[rows, :] = x1
        counts = counts + _route_rows(x1, rows, mrow, mod_ref, g2_ref, wr_ref, wrhi_ref, rb_ref,
                                      h2_ref, route_ref, gate4_ref)
    cnt_ref[...] = jnp.broadcast_to(counts, (SUBLANES, LANES)).astype(jnp.int32)


MERGE_ROWS = TM


def _route_rows(x1, rows, mrow, mod_ref, g2_ref, wr_ref, wrhi_ref, rb_ref, h2_ref, route_ref, gate4_ref):
    n = x1.shape[0]
    h2 = _rms(x1, g2_ref[...] * (1.0 + _mod_part(mod_ref, 4, mrow))) + _mod_part(mod_ref, 3, mrow)
    hb = h2.astype(BF16)
    h2_ref[rows, :] = hb

    h_lo = (h2 - hb.astype(F32)).astype(BF16)
    a = jnp.dot(hb, wr_ref[...], preferred_element_type=F32)
    b = jnp.dot(h_lo, wrhi_ref[...], preferred_element_type=F32)
    logits = a[:, :ROUTER_COLS] + a[:, ROUTER_COLS:] + b + rb_ref[...]
    lt = logits.T

    rowi = lax.broadcasted_iota(jnp.int32, (SUBLANES, n), 0).astype(F32)
    valid = rowi < float(N_EXPERT_GROUPS)
    g_log = lt[0:SUBLANES]
    g_max, g_idx = _first_max_index(g_log, valid, rowi)
    p_g = 1.0 / jnp.sum(jnp.where(valid, jnp.exp(g_log - g_max), 0.0), axis=0, keepdims=True)
    e_log = jnp.zeros((SUBLANES, n), F32)
    for g in range(N_EXPERT_GROUPS):
        e_log = jnp.where(g_idx == g, lt[(g + 1) * SUBLANES:(g + 2) * SUBLANES], e_log)
    e_max, i1 = _first_max_index(e_log, valid, rowi)
    e_exp = jnp.where(valid, jnp.exp(e_log - e_max), 0.0)
    e_prob = e_exp / jnp.sum(e_exp, axis=0, keepdims=True)
    v1 = jnp.max(e_prob, axis=0, keepdims=True)
    rest = jnp.logical_and(valid, rowi != i1)
    v2, i2 = _first_max_index(e_prob, rest, rowi)
    denom = v1 + v2
    w_sel = p_g * jnp.where(rowi == i1, v1 / denom, jnp.where(rowi == i2, v2 / denom, 0.0))
    route_ref[:, rows] = jnp.broadcast_to(g_idx, (SUBLANES, n))
    pad = jnp.zeros((LANES - SUBLANES, n), F32)
    gate4_ref[rows, :] = jnp.concatenate([w_sel, pad], axis=0).T
    return jnp.sum(jnp.where(rowi == g_idx, 1.0, 0.0), axis=1, keepdims=True)


def _merge(l, xs, attn_ctx, attn_lat, sgu_n, cbz, conv_w, on, w_out, mod, g2, wr_cat, wr_hi, rb):
    lay = lambda *rest: (lambda i: (l,) + rest)
    rows8 = TM // SUBLANES
    return pl.pallas_call(
        functools.partial(_merge_kernel, split_x=len(xs) == 2),
        grid=(N_TILES,),
        in_specs=_x_specs(xs) + [
            pl.BlockSpec((TM, D_ATTN), lambda i: (_ctx_tile(i), 0)),
            pl.BlockSpec((TM, D_ATTN), lambda i: (_lat_tile(i), 0)),
            pl.BlockSpec((TM, D_SGU), lambda i: (i, 0)),
            pl.BlockSpec((TM, 2 * D_CONV), lambda i: (i, 0)),
            pl.BlockSpec((SUBLANES, 2 * D_CONV), lambda i: (jnp.maximum(i * rows8 - 1, 0), 0)),
            pl.BlockSpec((SUBLANES, 2 * D_CONV),
                         lambda i: (jnp.minimum((i + 1) * rows8, T_ALL // SUBLANES - 1), 0)),
            pl.BlockSpec((None, 3, D_CONV), lay(0, 0)),
            pl.BlockSpec((None, 1, D_MIX), lay(0, 0)),
            pl.BlockSpec((None, D_MIX, D_MODEL), lay(0, 0)),
            _MOD_SPEC(l, 1),
            pl.BlockSpec((None, 1, D_MODEL), lay(0, 0)),
            pl.BlockSpec((None, D_MODEL, 2 * ROUTER_COLS), lay(0, 0)),
            pl.BlockSpec((None, D_MODEL, ROUTER_COLS), lay(0, 0)),
            pl.BlockSpec((None, 1, ROUTER_COLS), lay(0, 0)),
        ],
        out_specs=[
            pl.BlockSpec((TM, D_MODEL), lambda i: (i, 0)),
            pl.BlockSpec((TM, D_MODEL), lambda i: (i, 0)),
            pl.BlockSpec((None, SUBLANES, TM), lambda i: (i, 0, 0)),
            pl.BlockSpec((TM, LANES), lambda i: (i, 0)),
            pl.BlockSpec((None, SUBLANES, LANES), lambda i: (i, 0, 0)),
        ],
        out_shape=[
            jax.ShapeDtypeStruct((T_ALL, D_MODEL), F32),
            jax.ShapeDtypeStruct((T_ALL, D_MODEL), BF16),
            jax.ShapeDtypeStruct((N_TILES, SUBLANES, TM), F32),
            jax.ShapeDtypeStruct((T_ALL, LANES), F32),
            jax.ShapeDtypeStruct((N_TILES, SUBLANES, LANES), jnp.int32),
        ],
        compiler_params=_params(1),
        name=f"merge_l{l}",
    )(*xs, attn_ctx, attn_lat, sgu_n, cbz, cbz, cbz, conv_w, on, w_out, mod, g2, wr_cat, wr_hi, rb)


RB = 144
SORT_ROWS = TM + LANES
SORT_ROWS_PAD = -(-(SORT_ROWS + RB) // BF16_ROWS) * BF16_ROWS
MOE_VMEM_LIMIT = 56 * 1024 * 1024


def _split3(x):
    hi = x.astype(BF16)
    r1 = x - hi.astype(F32)
    mid = r1.astype(BF16)
    lo = (r1 - mid.astype(F32)).astype(BF16)
    return hi, mid, lo


def _moe_kernel(cnt_ref, h_ref, route_ref, gate4_ref, tri_ref, w1_ref, w3_ref, w2_ref, x1_ref, mod_ref, fn_ref,
                *refs, final):
    out_refs, (xs_ref, gs_ref, zs_ref) = refs[:-3], refs[-3:]
    i = pl.program_id(0)
    gate2 = _mod_part(mod_ref, 5, _mod_row_of_tile(i))
    counts = [cnt_ref[i, g] for g in range(N_EXPERT_GROUPS)]
    starts = [jnp.int32(0)]
    for g in range(N_EXPERT_GROUPS - 1):
        starts.append(starts[-1] + (counts[g] + (BF16_ROWS - 1)) // BF16_ROWS * BF16_ROWS)

    g_idx = route_ref[0:1, :]
    rowi = lax.broadcasted_iota(jnp.int32, (SUBLANES, TM), 0).astype(F32)
    onehot = jnp.where(rowi == g_idx, 1.0, 0.0)
    incl = jnp.dot(onehot.astype(BF16), tri_ref[...], preferred_element_type=F32)
    pos = jnp.sum(onehot * incl, axis=0, keepdims=True) - 1.0
    for g in range(1, N_EXPERT_GROUPS):
        pos = pos + jnp.where(g_idx == float(g), starts[g].astype(F32), 0.0)
    sub = lax.broadcasted_iota(jnp.int32, (SORT_ROWS, TM), 0).astype(F32)
    perm = jnp.where(sub == pos, 1.0, 0.0).astype(BF16)
    pos_col = jnp.broadcast_to(pos, (LANES, TM)).T
    lane = lax.broadcasted_iota(jnp.int32, (TM, SORT_ROWS), 1).astype(F32)
    perm_t = jnp.where(lane == jnp.concatenate([pos_col] * (SORT_ROWS // LANES), axis=1), 1.0, 0.0).astype(BF16)

    xs_ref[:SORT_ROWS, :] = jnp.dot(perm, h_ref[...], preferred_element_type=F32).astype(BF16)
    xs_ref[SORT_ROWS:, :] = jnp.zeros((SORT_ROWS_PAD - SORT_ROWS, D_MODEL), BF16)
    hi, mid, lo = _split3(gate4_ref[...])
    packed = (hi.astype(F32) + pltpu.roll(mid.astype(F32), EXPERTS_PER_GROUP, 1)
              + pltpu.roll(lo.astype(F32), 2 * EXPERTS_PER_GROUP, 1)).astype(BF16)
    gsp = jnp.dot(perm, packed, preferred_element_type=F32)
    gs_ref[:SORT_ROWS, :] = (gsp + pltpu.roll(gsp, LANES - EXPERTS_PER_GROUP, 1)
                             + pltpu.roll(gsp, LANES - 2 * EXPERTS_PER_GROUP, 1))
    gs_ref[SORT_ROWS:, :] = jnp.zeros((SORT_ROWS_PAD - SORT_ROWS, LANES), F32)
    zs_ref[...] = jnp.zeros((SORT_ROWS_PAD, D_MODEL), BF16)

    def block_ffn(g, blk):
        row0 = pl.multiple_of(starts[g] + blk * RB, BF16_ROWS)
        xb = xs_ref[pl.ds(row0, RB), :]
        gb = gs_ref[pl.ds(row0, RB), :]
        cols = []
        for e in range(EXPERTS_PER_GROUP):
            a = jnp.dot(xb, w1_ref[g * EXPERTS_PER_GROUP + e], preferred_element_type=F32)
            b = jnp.dot(xb, w3_ref[g * EXPERTS_PER_GROUP + e], preferred_element_type=F32)
            cols.append((a / (1.0 + jnp.exp(-a)) * b * gb[:, e:e + 1]).astype(BF16))
        z = jnp.dot(jnp.concatenate(cols, axis=1), w2_ref[g], preferred_element_type=F32)
        zs_ref[pl.ds(row0, RB), :] = z.astype(BF16)

    for g in range(N_EXPERT_GROUPS):
        def extra_block(blk, carry, g=g):
            block_ffn(g, blk)
            return carry

        lax.fori_loop(1, (counts[g] + (RB - 1)) // RB, extra_block, 0)
    for g in range(N_EXPERT_GROUPS):
        block_ffn(g, 0)

    y = jnp.dot(perm_t, zs_ref[:SORT_ROWS, :], preferred_element_type=F32)
    out = x1_ref[...] + gate2 * y

    if final:
        octx_ref, olat_ref = out_refs
        out = _rms(out, fn_ref[...])
        olat_ref[...] = out

        @pl.when(i < N_CTX_TILES)
        def _():
            octx_ref[...] = out
    else:
        out_refs[0][...] = out


def _moe(l, counts, h2, route, gate4, tri, w1, w3, w2, x1, mod, final_norm, final):
    tile = pl.BlockSpec((TM, D_MODEL), lambda i, c: (i, 0))
    resident = pl.Buffered(1)
    if final:
        out_specs = [pl.BlockSpec((TM, D_MODEL), lambda i, c: (_ctx_tile(i), 0)),
                     pl.BlockSpec((TM, D_MODEL), lambda i, c: (_lat_tile(i), 0))]
        out_shape = [jax.ShapeDtypeStruct((T_CTX, D_MODEL), F32), jax.ShapeDtypeStruct((T_LAT, D_MODEL), F32)]
    else:
        out_specs = tile
        out_shape = jax.ShapeDtypeStruct((T_ALL, D_MODEL), F32)
    grid_spec = pltpu.PrefetchScalarGridSpec(
        num_scalar_prefetch=1,
        grid=(N_TILES,),
        in_specs=[
            tile,
            pl.BlockSpec((None, SUBLANES, TM), lambda i, c: (i, 0, 0)),
            pl.BlockSpec((TM, LANES), lambda i, c: (i, 0)),
            pl.BlockSpec((TM, TM), lambda i, c: (0, 0), pipeline_mode=resident),
            pl.BlockSpec((N_EXPERTS, D_MODEL, D_EXPERT), lambda i, c: (0, 0, 0), pipeline_mode=resident),
            pl.BlockSpec((N_EXPERTS, D_MODEL, D_EXPERT), lambda i, c: (0, 0, 0), pipeline_mode=resident),
            pl.BlockSpec((N_EXPERT_GROUPS, D_GROUP_HID, D_MODEL), lambda i, c: (0, 0, 0), pipeline_mode=resident),
            tile,
            pl.BlockSpec((None, N_MOD_PARTS, N_MOD_ROWS, D_MODEL), lambda i, c: (l, 0, 0, 0)),
            pl.BlockSpec((1, D_MODEL), lambda i, c: (0, 0)),
        ],
        out_specs=out_specs,
        scratch_shapes=[pltpu.VMEM((SORT_ROWS_PAD, D_MODEL), BF16),
                        pltpu.VMEM((SORT_ROWS_PAD, LANES), F32),
                        pltpu.VMEM((SORT_ROWS_PAD, D_MODEL), BF16)],
    )
    return pl.pallas_call(
        functools.partial(_moe_kernel, final=final),
        grid_spec=grid_spec,
        out_shape=out_shape,
        compiler_params=pltpu.CompilerParams(dimension_semantics=("arbitrary",),
                                             vmem_limit_bytes=MOE_VMEM_LIMIT),
        name=f"moe_l{l}",
    )(counts, h2, route, gate4, tri, w1, w3, w2, x1, mod, final_norm)


def _rope_tables():
    rows = DEC_SEQ // GRID_W
    row = np.repeat(np.arange(rows, dtype=np.float32), GRID_W)
    col = np.tile(np.arange(GRID_W, dtype=np.float32), rows)
    half = ROPE_AXIS_DIM // 2
    inv_freq = (1.0 / (np.float32(ROPE_THETA) ** (np.arange(half, dtype=np.float32) * np.float32(2.0)
                                                  / np.float32(ROPE_AXIS_DIM)))).astype(np.float32)
    ar = row[:, None] * inv_freq
    ac = col[:, None] * inv_freq
    cos64 = np.concatenate([np.cos(ar), np.cos(ar), np.cos(ac), np.cos(ac)], axis=1)
    sin64 = np.concatenate([-np.sin(ar), np.sin(ar), -np.sin(ac), np.sin(ac)], axis=1)
    cos_t = np.concatenate([np.ones((TM, LANES), np.float32), np.tile(cos64, (1, 2))], axis=0)
    sin_t = np.concatenate([np.zeros((TM, LANES), np.float32), np.tile(sin64, (1, 2))], axis=0)
    return jnp.asarray(cos_t, F32), jnp.asarray(sin_t, F32)


def _router_weights(router_g_w, router_g_b, router_e_w, router_e_b):
    zw = jnp.zeros((DEPTH, D_MODEL, SUBLANES - N_EXPERT_GROUPS), F32)
    zb = jnp.zeros((DEPTH, SUBLANES - N_EXPERT_GROUPS), F32)
    w_cols, b_cols = [router_g_w, zw], [router_g_b, zb]
    for g in range(N_EXPERT_GROUPS):
        w_cols += [router_e_w[:, g], zw]
        b_cols += [router_e_b[:, g], zb]
    used = (1 + N_EXPERT_GROUPS) * SUBLANES
    w_cols.append(jnp.zeros((DEPTH, D_MODEL, ROUTER_COLS - used), F32))
    b_cols.append(jnp.zeros((DEPTH, ROUTER_COLS - used), F32))
    w = jnp.concatenate(w_cols, axis=2)
    b = jnp.concatenate(b_cols, axis=1).reshape(DEPTH, 1, ROUTER_COLS)
    w_hi = w.astype(BF16)
    w_lo = (w - w_hi.astype(F32)).astype(BF16)
    return jnp.concatenate([w_hi, w_lo], axis=2), w_hi, b


def kernel(x_prompt, x_sample, cache_k, cache_v, c, c_ctx, w_ada, b_ada, norm1, w_in, q_norm, k_norm,
           sgu_w, sgu_b, conv_w, out_norm, w_out, norm2, router_g_w, router_g_b, router_e_w,
           router_e_b, moe_w1, moe_w3, moe_w2, final_norm):
    xs = (x_prompt.reshape(T_CTX, D_MODEL), x_sample.reshape(T_LAT, D_MODEL))

    cvecs = jnp.concatenate([c_ctx[None, :], c, jnp.zeros((N_MOD_ROWS - 1 - DEC_BATCH, D_MODEL), F32)], axis=0)
    mod = _modulation(cvecs.T, w_ada, b_ada)

    w_in_b = w_in.astype(BF16)
    w_out_b = w_out.astype(BF16)
    g1 = norm1.reshape(DEPTH, 1, D_MODEL)
    g2 = norm2.reshape(DEPTH, 1, D_MODEL)
    on = out_norm.reshape(DEPTH, 1, D_MIX)
    qg = jnp.tile(q_norm, (1, N_HEADS)).reshape(DEPTH, 1, D_ATTN)
    kg = jnp.tile(k_norm, (1, N_KV_HEADS)).reshape(DEPTH, 1, D_KV)
    cos_t, sin_t = _rope_tables()
    bq = jnp.asarray(np.kron(np.eye(N_HEADS), np.full((HEAD_DIM, HEAD_DIM), 1.0 / HEAD_DIM)), BF16)
    sgu_w_pairs = sgu_w.astype(BF16).reshape(DEPTH, N_SGU_GROUPS // 2, 2, CHUNK, CHUNK)
    sgu_w_pairs = jnp.concatenate([sgu_w_pairs[:, :, 0], sgu_w_pairs[:, :, 1]], axis=-1)
    sgu_bias = jnp.repeat(jnp.swapaxes(sgu_b, 1, 2), SGU_GROUP_DIM, axis=2)
    wr_cat, wr_hi, rb = _router_weights(router_g_w, router_g_b, router_e_w, router_e_b)
    fn = final_norm.reshape(1, D_MODEL)
    tri = jnp.asarray(np.triu(np.ones((TM, TM), np.float32)), BF16)
    ck = cache_k.reshape(DEC_BATCH, DEPTH, PAST_LEN, D_KV)
    cv = cache_v.reshape(DEC_BATCH, DEPTH, PAST_LEN, D_KV)

    ctx_ks, ctx_vs = [], []
    for l in range(DEPTH):
        qt, k, vt, k32, v32, sgu_n, cbz = _proj(l, xs, mod, g1, w_in_b, qg, kg, cos_t, sin_t, bq,
                                                sgu_w_pairs, sgu_bias, on)
        attn_ctx = _attn_ctx(l, qt, k, vt)
        attn_lat, w1, w3, w2 = _attn_lat(l, qt, k, vt, ck, cv, qg, kg, moe_w1, moe_w3, moe_w2)
        w2 = w2.reshape(N_EXPERT_GROUPS, D_GROUP_HID, D_MODEL)
        x1, h2, route, gate4, cnt = _merge(l, xs, attn_ctx, attn_lat, sgu_n, cbz, conv_w, on, w_out_b, mod, g2,
                                           wr_cat, wr_hi, rb)
        counts = cnt[:, :N_EXPERT_GROUPS, 0]
        out = _moe(l, counts, h2, route, gate4, tri, w1, w3, w2, x1, mod, fn, final=(l == DEPTH - 1))
        xs = (out,)
        ctx_ks.append(k32.reshape(BATCH, SEQ, N_KV_HEADS, HEAD_DIM))
        ctx_vs.append(v32.reshape(BATCH, SEQ, N_KV_HEADS, HEAD_DIM))

    y_prompt, y_sample = out
    return (y_prompt.reshape(BATCH, SEQ, D_MODEL), y_sample.reshape(DEC_BATCH, DEC_SEQ, D_MODEL),
            jnp.stack(ctx_ks, axis=1), jnp.stack(ctx_vs, axis=1))
```

```python
import functools
import math

import jax
import jax.numpy as jnp
import numpy as np
from jax import lax
from jax.experimental import pallas as pl
from jax.experimental.pallas import tpu as pltpu

D_MODEL = 1024
BATCH = 16
SEQ = 256
DEPTH = 2
DEC_BATCH = 2
DEC_SEQ = 4096
PAST_LEN = 512
GRID_W = 64
N_HEADS = 8
N_KV_HEADS = 2
HEAD_DIM = 64
ROPE_AXIS_DIM = HEAD_DIM // 2
ROPE_THETA = 10000.0
D_ATTN = N_HEADS * HEAD_DIM
D_KV = N_KV_HEADS * HEAD_DIM
CHUNK = 128
N_SGU_GROUPS = 4
SGU_GROUP_DIM = 64
D_SGU = N_SGU_GROUPS * SGU_GROUP_DIM
D_CONV = 256
D_MIX = D_ATTN + D_SGU + D_CONV
D_IN = D_ATTN + 2 * D_KV + 2 * D_SGU + 3 * D_CONV
N_EXPERT_GROUPS = 4
EXPERTS_PER_GROUP = 4
N_EXPERTS = N_EXPERT_GROUPS * EXPERTS_PER_GROUP
D_EXPERT = 256
D_GROUP_HID = EXPERTS_PER_GROUP * D_EXPERT
EPS = 1e-6

T_CTX = BATCH * SEQ
T_LAT = DEC_BATCH * DEC_SEQ
T_ALL = T_CTX + T_LAT
TM = 512
N_TILES = T_ALL // TM
N_CTX_TILES = T_CTX // TM
LAT_TILES_PER_BATCH = DEC_SEQ // TM
TM_MOE = 1024
LANES = 128
SUBLANES = 8
BF16_ROWS = 16
N_MOD_ROWS = 8
N_MOD_PARTS = 6
ROUTER_COLS = 128
VMEM_LIMIT = 48 * 1024 * 1024

OFF_Q = 0
OFF_KV = D_ATTN
OFF_SGU = OFF_KV + 2 * D_KV
OFF_CONV = OFF_SGU + 2 * D_SGU

BF16 = jnp.bfloat16
F32 = jnp.float32
NEG_BIG = -1e30
Q_SCALE = HEAD_DIM ** -0.5 * math.log2(math.e)


def _params(n_grid_dims):
    return pltpu.CompilerParams(
        dimension_semantics=("arbitrary",) * n_grid_dims,
        vmem_limit_bytes=VMEM_LIMIT,
    )


def _mod_row_of_tile(i, tile=TM):
    n_ctx = T_CTX // tile
    return jnp.where(i < n_ctx, 0, 1 + (i - n_ctx) // (DEC_SEQ // tile))


def _mod_part(mod_ref, part, row):
    return mod_ref[part, pl.ds(row, 1), :]


def _rope_block_of_tile(i):
    return jnp.where(i < N_CTX_TILES, 0, 1 + (i - N_CTX_TILES) % LAT_TILES_PER_BATCH)


def _rms(x, gain):
    ms = jnp.mean(x * x, axis=-1, keepdims=True)
    return x * lax.rsqrt(ms + EPS) * gain


def _ctx_tile(i):
    return jnp.minimum(i, N_CTX_TILES - 1)


def _lat_tile(i):
    return jnp.maximum(i - N_CTX_TILES, 0)


MOD_PARTS_PER_STEP = 2
MOD_TN = MOD_PARTS_PER_STEP * D_MODEL


def _mod_kernel(ct_ref, w_ref, b_ref, o_ref):
    c = ct_ref[...]
    s = c / (1.0 + jnp.exp(-c))
    row = lax.broadcasted_iota(jnp.int32, (N_MOD_ROWS, D_MODEL), 0)
    for part in range(MOD_PARTS_PER_STEP):
        cols = slice(part * D_MODEL, (part + 1) * D_MODEL)
        w = w_ref[:, cols]
        out = jnp.zeros((N_MOD_ROWS, D_MODEL), F32)
        for r in range(1 + DEC_BATCH):
            acc = jnp.sum(w * s[:, r:r + 1], axis=0, keepdims=True)
            out = jnp.where(row == r, acc, out)
        o_ref[part] = out + b_ref[:, cols]


def _modulation(cvecs_t, w_ada, b_ada):
    n_col = N_MOD_PARTS * D_MODEL
    return pl.pallas_call(
        _mod_kernel,
        grid=(DEPTH, n_col // MOD_TN),
        in_specs=[
            pl.BlockSpec((D_MODEL, N_MOD_ROWS), lambda l, j: (0, 0)),
            pl.BlockSpec((None, D_MODEL, MOD_TN), lambda l, j: (l, 0, j)),
            pl.BlockSpec((None, 1, MOD_TN), lambda l, j: (l, 0, j)),
        ],
        out_specs=pl.BlockSpec((None, MOD_PARTS_PER_STEP, N_MOD_ROWS, D_MODEL), lambda l, j: (l, j, 0, 0)),
        out_shape=jax.ShapeDtypeStruct((DEPTH, N_MOD_PARTS, N_MOD_ROWS, D_MODEL), F32),
        compiler_params=_params(2),
        name="adaln_modulation",
    )(cvecs_t, w_ada, b_ada.reshape(DEPTH, 1, n_col))


_MOD_SPEC = lambda l, nd: pl.BlockSpec(
    (None, N_MOD_PARTS, N_MOD_ROWS, D_MODEL), lambda *idx: (l, 0, 0, 0))


def _swap16(x):
    lane = lax.broadcasted_iota(jnp.int32, x.shape, 1)
    up = pltpu.roll(x, 16, 1)
    down = pltpu.roll(x, LANES - 16, 1)
    return jnp.where((lane & 16) != 0, up, down)


def _rope(x, cos_t, sin_t):
    cols = []
    for j in range(x.shape[1] // LANES):
        xc = x[:, j * LANES:(j + 1) * LANES]
        cols.append(xc * cos_t + _swap16(xc) * sin_t)
    return cols[0] if len(cols) == 1 else jnp.concatenate(cols, axis=1)


def _proj_kernel(*refs, split_x):
    n_x = 2 if split_x else 1
    x_refs, refs, (scr_even, scr_odd) = refs[:n_x], refs[n_x:-2], refs[-2:]
    step = functools.partial(_proj_step, x_refs, refs, split_x)
    i = pl.program_id(0)

    @pl.when(i == 0)
    def _():
        scr_odd[...] = jnp.zeros((TM, D_IN), F32)

    @pl.when(i % 2 == 0)
    def _():
        step(scr_even, scr_odd)

    @pl.when(i % 2 == 1)
    def _():
        step(scr_odd, scr_even)


def _proj_step(x_refs, refs, split_x, new_ref, proj):
    (mod_ref, g1_ref, w_ref, qg_ref, kg_ref, cos_ref, sin_ref, bq_ref, sw_ref, sb_ref, on_ref,
     qt_ref, k_ref, vt_ref, k32_ref, v32_ref, sgu_ref, cbz_ref) = refs
    i = pl.program_id(0)
    cur = _stage1_tile(i)

    is_ctx = i <= N_CTX_TILES
    cos_t = cos_ref[...]
    sin_t = sin_ref[...]

    q = proj[:, OFF_Q:OFF_Q + D_ATTN]
    qms = jnp.dot((q * q).astype(BF16), bq_ref[...], preferred_element_type=F32)
    qn = q * lax.rsqrt(qms + EPS) * qg_ref[...]
    qr = _rope(qn, cos_t, sin_t) * Q_SCALE
    qt_ref[...] = qr.T.astype(BF16)

    k = proj[:, OFF_KV:OFF_KV + D_KV]
    v = proj[:, OFF_KV + D_KV:OFF_KV + 2 * D_KV]
    kms = jnp.dot((k * k).astype(BF16), bq_ref[:D_KV, :D_KV], preferred_element_type=F32)
    kn = k * lax.rsqrt(kms + EPS) * kg_ref[...]
    k_ref[...] = _rope(kn, cos_t, sin_t).astype(BF16)
    vt_ref[...] = v.T.astype(BF16)

    su = proj[:, OFF_SGU:OFF_SGU + D_SGU]
    sv = proj[:, OFF_SGU + D_SGU:OFF_SGU + 2 * D_SGU]
    n_chunks = TM // CHUNK
    sv_wide = jnp.concatenate([sv[n * CHUNK:(n + 1) * CHUNK, :] for n in range(n_chunks)], axis=1)
    grp = lax.broadcasted_iota(jnp.int32, (CHUNK, n_chunks * D_SGU), 1) // SGU_GROUP_DIM % N_SGU_GROUPS
    mixed_wide = jnp.zeros((CHUNK, n_chunks * D_SGU), F32)
    for pair in range(N_SGU_GROUPS // 2):
        rhs = jnp.concatenate(
            [jnp.where(grp == 2 * pair, sv_wide, 0.0), jnp.where(grp == 2 * pair + 1, sv_wide, 0.0)],
            axis=0).astype(BF16)
        mixed_wide = mixed_wide + jnp.dot(sw_ref[pair], rhs, preferred_element_type=F32)
    on_sgu = on_ref[:, D_ATTN:D_ATTN + D_SGU]
    for n in range(n_chunks):
        sgu = su[n * CHUNK:(n + 1) * CHUNK, :] * (mixed_wide[:, n * D_SGU:(n + 1) * D_SGU] + sb_ref[...])
        sgu_ref[n * CHUNK:(n + 1) * CHUNK, :] = _rms(sgu, on_sgu).astype(BF16)

    cbz_ref[:, :D_CONV] = proj[:, OFF_CONV:OFF_CONV + D_CONV]
    cbz_ref[:, D_CONV:] = (proj[:, OFF_CONV + D_CONV:OFF_CONV + 2 * D_CONV]
                           * proj[:, OFF_CONV + 2 * D_CONV:OFF_CONV + 3 * D_CONV])

    x = jnp.where(cur < N_CTX_TILES, x_refs[0][...], x_refs[1][...]) if split_x else x_refs[0][...]
    mrow = _mod_row_of_tile(cur)
    h = _rms(x, g1_ref[...] * (1.0 + _mod_part(mod_ref, 1, mrow))) + _mod_part(mod_ref, 0, mrow)
    new_ref[...] = jnp.dot(h.astype(BF16), w_ref[...], preferred_element_type=F32)

    @pl.when(is_ctx)
    def _():
        k32_ref[...] = kn
        v32_ref[...] = v


def _x_specs(xs, tile_of_step=lambda i: i):
    if len(xs) == 1:
        return [pl.BlockSpec((TM, D_MODEL), lambda i: (tile_of_step(i), 0))]
    return [pl.BlockSpec((TM, D_MODEL), lambda i: (_ctx_tile(tile_of_step(i)), 0)),
            pl.BlockSpec((TM, D_MODEL), lambda i: (_lat_tile(tile_of_step(i)), 0))]


def _stage1_tile(i):
    return jnp.minimum(i, N_TILES - 1)


def _stage2_tile(i):
    return jnp.maximum(i - 1, 0)


def _proj(l, xs, mod, g1, w_in, qg, kg, cos_t, sin_t, bq, sgu_w, sgu_b, on):
    lay = lambda *rest: (lambda i: (l,) + rest)
    s2 = _stage2_tile
    return pl.pallas_call(
        functools.partial(_proj_kernel, split_x=len(xs) == 2),
        grid=(N_TILES + 1,),
        in_specs=_x_specs(xs, _stage1_tile) + [
            _MOD_SPEC(l, 1),
            pl.BlockSpec((None, 1, D_MODEL), lay(0, 0)),
            pl.BlockSpec((None, D_MODEL, D_IN), lay(0, 0)),
            pl.BlockSpec((None, 1, D_ATTN), lay(0, 0)),
            pl.BlockSpec((None, 1, D_KV), lay(0, 0)),
            pl.BlockSpec((TM, LANES), lambda i: (_rope_block_of_tile(s2(i)), 0)),
            pl.BlockSpec((TM, LANES), lambda i: (_rope_block_of_tile(s2(i)), 0)),
            pl.BlockSpec((D_ATTN, D_ATTN), lambda i: (0, 0)),
            pl.BlockSpec((None, N_SGU_GROUPS // 2, CHUNK, 2 * CHUNK), lay(0, 0, 0)),
            pl.BlockSpec((None, CHUNK, D_SGU), lay(0, 0)),
            pl.BlockSpec((None, 1, D_MIX), lay(0, 0)),
        ],
        out_specs=[
            pl.BlockSpec((None, D_ATTN, TM), lambda i: (s2(i), 0, 0)),
            pl.BlockSpec((TM, D_KV), lambda i: (s2(i), 0)),
            pl.BlockSpec((None, D_KV, TM), lambda i: (s2(i), 0, 0)),
            pl.BlockSpec((TM, D_KV), lambda i: (_ctx_tile(s2(i)), 0)),
            pl.BlockSpec((TM, D_KV), lambda i: (_ctx_tile(s2(i)), 0)),
            pl.BlockSpec((TM, D_SGU), lambda i: (s2(i), 0)),
            pl.BlockSpec((TM, 2 * D_CONV), lambda i: (s2(i), 0)),
        ],
        out_shape=[
            jax.ShapeDtypeStruct((N_TILES, D_ATTN, TM), BF16),
            jax.ShapeDtypeStruct((T_ALL, D_KV), BF16),
            jax.ShapeDtypeStruct((N_TILES, D_KV, TM), BF16),
            jax.ShapeDtypeStruct((T_CTX, D_KV), F32),
            jax.ShapeDtypeStruct((T_CTX, D_KV), F32),
            jax.ShapeDtypeStruct((T_ALL, D_SGU), BF16),
            jax.ShapeDtypeStruct((T_ALL, 2 * D_CONV), F32),
        ],
        scratch_shapes=[pltpu.VMEM((TM, D_IN), F32), pltpu.VMEM((TM, D_IN), F32)],
        compiler_params=_params(1),
        name=f"proj_l{l}",
    )(*xs, mod, g1, w_in, qg, kg, cos_t, sin_t, bq, sgu_w, sgu_b, on)


ACC_ROWS = HEAD_DIM + BF16_ROWS
N_PAIRS = N_HEADS // 2
PAIRS_PER_KV = N_PAIRS // N_KV_HEADS


def _pair_queries(qt, kv_idx, tq):
    qf = qt.astype(F32)
    mine = lax.broadcasted_iota(jnp.int32, (2 * HEAD_DIM, tq), 0) // HEAD_DIM == kv_idx
    qe = qf[:HEAD_DIM]
    qo = qf[HEAD_DIM:]
    rhs = jnp.concatenate(
        [jnp.where(mine, jnp.concatenate([qe, qe], axis=0), 0.0),
         jnp.where(mine, jnp.concatenate([qo, qo], axis=0), 0.0)], axis=1)
    return rhs.astype(BF16)


def _with_ones(vt_c):
    return jnp.concatenate([vt_c, jnp.ones((BF16_ROWS, vt_c.shape[1]), BF16)], axis=0)


def _flash_pairs(rhs_list, chunk_lists, tq, shift=None):
    n_pairs, n_chunks = len(rhs_list), len(chunk_lists[0])
    scores = lambda s, c: jnp.dot(chunk_lists[s][c][0](), rhs_list[s], preferred_element_type=F32)
    m = [jnp.full((1, 2 * tq), NEG_BIG, F32)] * n_pairs
    acc = [jnp.zeros((ACC_ROWS, 2 * tq), F32)] * n_pairs
    st = [scores(s, 0) for s in range(n_pairs)]
    for c in range(n_chunks):
        for s in range(n_pairs):
            if shift is None:
                m_new = jnp.maximum(m[s], jnp.max(st[s], axis=0, keepdims=True))
            st_next = scores(s, c + 1) if c + 1 < n_chunks else None
            if shift is None:
                alpha = jnp.exp2(m[s] - m_new)
                p = jnp.exp2(st[s] - m_new).astype(BF16)
                acc[s] = alpha * acc[s] + jnp.dot(chunk_lists[s][c][1](), p, preferred_element_type=F32)
                m[s] = m_new
            else:
                p = jnp.exp2(st[s] - shift).astype(BF16)
                acc[s] = acc[s] + jnp.dot(chunk_lists[s][c][1](), p, preferred_element_type=F32)
            st[s] = st_next
    outs = []
    for s in range(n_pairs):
        o = acc[s][:HEAD_DIM] * (1.0 / acc[s][HEAD_DIM:HEAD_DIM + 1])
        outs.append(jnp.concatenate([o[:, :tq], o[:, tq:]], axis=0))
    return outs


def _attn_ctx_kernel(qt_ref, k_ref, vt_ref, o_ref):
    tq = SEQ
    rhs_list, chunk_lists = [], []
    for pair in range(N_PAIRS):
        kv_idx = pair // PAIRS_PER_KV
        rhs_list.append(_pair_queries(qt_ref[pair * LANES:(pair + 1) * LANES, :], kv_idx, tq))
        chunk_lists.append([(lambda: k_ref[...],
                             lambda kv_idx=kv_idx: _with_ones(vt_ref[kv_idx * HEAD_DIM:(kv_idx + 1) * HEAD_DIM, :]))])
    for pair, o_t in enumerate(_flash_pairs(rhs_list, chunk_lists, tq)):
        o_ref[:, pair * LANES:(pair + 1) * LANES] = o_t.T.astype(BF16)


def _attn_ctx(l, qt, k, vt):
    halves = TM // SEQ
    return pl.pallas_call(
        _attn_ctx_kernel,
        grid=(BATCH,),
        in_specs=[
            pl.BlockSpec((None, D_ATTN, SEQ), lambda b: (b // halves, 0, b % halves)),
            pl.BlockSpec((SEQ, D_KV), lambda b: (b, 0)),
            pl.BlockSpec((None, D_KV, SEQ), lambda b: (b // halves, 0, b % halves)),
        ],
        out_specs=pl.BlockSpec((SEQ, D_ATTN), lambda b: (b, 0)),
        out_shape=jax.ShapeDtypeStruct((T_CTX, D_ATTN), BF16),
        compiler_params=_params(1),
        name=f"attn_ctx_l{l}",
    )(qt, k, vt)


KC = 512
LAT_PAIRS_PER_STEP = 2


MAX_SHIFT = 60.0
BOUND_SLACK = 1.05


def _score_bound_sq(qg_ref, kg_ref, ck):
    qg = qg_ref[...]
    kg = kg_ref[...]
    gq2 = jnp.max(qg * qg, axis=1, keepdims=True)
    k2 = HEAD_DIM * jnp.max(kg * kg, axis=1, keepdims=True)
    if ck is not None:
        sq = ck * ck
        first = lax.broadcasted_iota(jnp.int32, sq.shape, 1) < HEAD_DIM
        n0 = jnp.sum(jnp.where(first, sq, 0.0), axis=1, keepdims=True)
        n1 = jnp.sum(jnp.where(first, 0.0, sq), axis=1, keepdims=True)
        k2 = jnp.maximum(k2, jnp.max(jnp.maximum(n0, n1), axis=0, keepdims=True))
    return (BOUND_SLACK * Q_SCALE) ** 2 * HEAD_DIM * gq2 * k2


def _run_with_score_bound(run, bound):
    small = bound <= MAX_SHIFT

    @pl.when(small)
    def _():
        run(bound)

    @pl.when(jnp.logical_not(small))
    def _():
        run(None)


def _attn_lat_kernel(qt_ref, k_ref, vt_ref, ck_ref, cv_ref, qg_ref, kg_ref, w1_ref, w3_ref, w2_ref,
                     o_ref, w1b_ref, w3b_ref, w2b_ref, bound_ref):
    tq = TM

    def run(shift):
        w1b_ref[...] = w1_ref[...].astype(BF16)
        w3b_ref[...] = w3_ref[...].astype(BF16)
        w2b_ref[...] = w2_ref[...].astype(BF16)
        cvt = cv_ref[...].T
        rhs_list, chunk_lists = [], []
        for j in range(LAT_PAIRS_PER_STEP):
            kv_idx = (pl.program_id(2) * LAT_PAIRS_PER_STEP + j) // PAIRS_PER_KV
            rhs_list.append(_pair_queries(qt_ref[j * LANES:(j + 1) * LANES, :], kv_idx, tq))
            v_row = pl.multiple_of(kv_idx * HEAD_DIM, HEAD_DIM)

            def latent_chunk(c, v_row=v_row):
                tile, lane0 = (c * KC) // TM, (c * KC) % TM
                return (lambda: k_ref[c * KC:(c + 1) * KC, :],
                        lambda: _with_ones(vt_ref[tile, pl.ds(v_row, HEAD_DIM), lane0:lane0 + KC]))

            def cached_chunk(c, kv_idx=kv_idx):
                return (lambda: ck_ref[c * KC:(c + 1) * KC, :].astype(BF16),
                        lambda: _with_ones(jnp.where(kv_idx == 0, cvt[:HEAD_DIM, c * KC:(c + 1) * KC],
                                                     cvt[HEAD_DIM:, c * KC:(c + 1) * KC]).astype(BF16)))

            chunks = [latent_chunk(c) for c in range(DEC_SEQ // KC)]
            chunks += [cached_chunk(c) for c in range(PAST_LEN // KC)]
            chunk_lists.append(chunks)
        for j, o_t in enumerate(_flash_pairs(rhs_list, chunk_lists, tq, shift)):
            o_ref[:, j * LANES:(j + 1) * LANES] = o_t.T.astype(BF16)

    @pl.when(jnp.logical_and(pl.program_id(1) == 0, pl.program_id(2) == 0))
    def _():
        bound_ref[0] = jnp.sqrt(_score_bound_sq(qg_ref, kg_ref, ck_ref[...]))[0, 0]

    _run_with_score_bound(run, bound_ref[0])


def _attn_lat(l, qt, k, vt, cache_k, cache_v, qg, kg, moe_w1, moe_w3, moe_w2):
    n_q = DEC_SEQ // TM
    first = N_CTX_TILES
    ctx_blocks = T_CTX // DEC_SEQ
    width = LAT_PAIRS_PER_STEP * LANES
    n_p = N_PAIRS // LAT_PAIRS_PER_STEP
    per_expert = DEC_BATCH * n_q * n_p // N_EXPERTS
    step = lambda b, i, p: (b * n_q + i) * n_p + p
    w_in_spec = lambda rows, cols: pl.BlockSpec(
        (None, None, rows // per_expert, cols),
        lambda b, i, p: (l, step(b, i, p) // per_expert, step(b, i, p) % per_expert, 0))
    w_out_spec = lambda rows, cols: pl.BlockSpec(
        (None, rows // per_expert, cols),
        lambda b, i, p: (step(b, i, p) // per_expert, step(b, i, p) % per_expert, 0))
    return pl.pallas_call(
        _attn_lat_kernel,
        grid=(DEC_BATCH, n_q, n_p),
        in_specs=[
            pl.BlockSpec((None, width, TM), lambda b, i, p: (first + b * n_q + i, p, 0)),
            pl.BlockSpec((DEC_SEQ, D_KV), lambda b, i, p: (ctx_blocks + b, 0)),
            pl.BlockSpec((n_q, D_KV, TM), lambda b, i, p: (ctx_blocks + b, 0, 0)),
            pl.BlockSpec((None, None, PAST_LEN, D_KV), lambda b, i, p: (b, l, 0, 0)),
            pl.BlockSpec((None, None, PAST_LEN, D_KV), lambda b, i, p: (b, l, 0, 0)),
            pl.BlockSpec((None, 1, D_ATTN), lambda b, i, p: (l, 0, 0)),
            pl.BlockSpec((None, 1, D_KV), lambda b, i, p: (l, 0, 0)),
            w_in_spec(D_MODEL, D_EXPERT),
            w_in_spec(D_MODEL, D_EXPERT),
            w_in_spec(D_EXPERT, D_MODEL),
        ],
        out_specs=[
            pl.BlockSpec((TM, width), lambda b, i, p: (b * n_q + i, p)),
            w_out_spec(D_MODEL, D_EXPERT),
            w_out_spec(D_MODEL, D_EXPERT),
            w_out_spec(D_EXPERT, D_MODEL),
        ],
        out_shape=[
            jax.ShapeDtypeStruct((T_LAT, D_ATTN), BF16),
            jax.ShapeDtypeStruct((N_EXPERTS, D_MODEL, D_EXPERT), BF16),
            jax.ShapeDtypeStruct((N_EXPERTS, D_MODEL, D_EXPERT), BF16),
            jax.ShapeDtypeStruct((N_EXPERTS, D_EXPERT, D_MODEL), BF16),
        ],
        scratch_shapes=[pltpu.SMEM((1,), F32)],
        compiler_params=_params(3),
        name=f"attn_lat_l{l}",
    )(qt, k, vt, cache_k, cache_v, qg, kg, moe_w1, moe_w3, moe_w2)


def _first_max_index(vals, valid, rowf):
    masked = jnp.where(valid, vals, -jnp.inf)
    mx = jnp.max(masked, axis=0, keepdims=True)
    idx = jnp.min(jnp.where(masked == mx, rowf, float(SUBLANES)), axis=0, keepdims=True)
    return mx, idx


def _merge_kernel(*refs, split_x):
    n_x = 2 if split_x else 1
    x_refs, refs = refs[:n_x], refs[n_x:]
    (actx_ref, alat_ref, sgu_ref, cbz_ref, prev_ref, next_ref, cw_ref, on_ref,
     wo_ref, mod_ref, g2_ref, wr_ref, wrhi_ref, rb_ref, x1_ref, h2_ref, route_ref, gate4_ref, cnt_ref) = refs
    cur = pl.program_id(0)
    is_ctx = cur < N_CTX_TILES
    mrow = _mod_row_of_tile(cur)
    on = on_ref[...]

    cbz = cbz_ref[...]
    cb = cbz[:, :D_CONV]
    z = cbz[:, D_CONV:]
    seq_mask = jnp.where(is_ctx, SEQ - 1, DEC_SEQ - 1)
    row = lax.broadcasted_iota(jnp.int32, (TM, D_CONV), 0)
    pos = (cur * TM + row) & seq_mask
    z_prev = jnp.where(row == 0, prev_ref[SUBLANES - 1:SUBLANES, D_CONV:], pltpu.roll(z, 1, 0))
    z_next = jnp.where(row == TM - 1, next_ref[0:1, D_CONV:], pltpu.roll(z, TM - 1, 0))
    z_prev = jnp.where(pos == 0, 0.0, z_prev)
    z_next = jnp.where(pos == seq_mask, 0.0, z_next)
    cw = cw_ref[...]
    conv = cb * (z_prev * cw[0:1, :] + z * cw[1:2, :] + z_next * cw[2:3, :])
    conv_n = _rms(conv, on[:, D_ATTN + D_SGU:]).astype(BF16)

    counts = jnp.zeros((SUBLANES, 1), F32)
    for sub in range(TM // MERGE_ROWS):
        rows = slice(sub * MERGE_ROWS, (sub + 1) * MERGE_ROWS)
        attn = jnp.where(is_ctx, actx_ref[rows, :], alat_ref[rows, :]).astype(F32)
        attn_n = _rms(attn, on[:, :D_ATTN]).astype(BF16)
        merged = jnp.concatenate([attn_n, sgu_ref[rows, :], conv_n[rows, :]], axis=1)
        x = jnp.where(is_ctx, x_refs[0][rows, :], x_refs[1][rows, :]) if split_x else x_refs[0][rows, :]
        x1 = x + _mod_part(mod_ref, 2, mrow) * jnp.dot(merged, wo_ref[...], preferred_element_type=F32)
        x1_ref[rows, :] = x1
        counts = counts + _route_rows(x1, rows, mrow, mod_ref, g2_ref, wr_ref, wrhi_ref, rb_ref,
                                      h2_ref, route_ref, gate4_ref)
    cnt_ref[...] = jnp.broadcast_to(counts, (SUBLANES, LANES)).astype(jnp.int32)


MERGE_ROWS = TM


def _route_rows(x1, rows, mrow, mod_ref, g2_ref, wr_ref, wrhi_ref, rb_ref, h2_ref, route_ref, gate4_ref):
    n = x1.shape[0]
    h2 = _rms(x1, g2_ref[...] * (1.0 + _mod_part(mod_ref, 4, mrow))) + _mod_part(mod_ref, 3, mrow)
    hb = h2.astype(BF16)
    h2_ref[rows, :] = hb

    h_lo = (h2 - hb.astype(F32)).astype(BF16)
    a = jnp.dot(hb, wr_ref[...], preferred_element_type=F32)
    b = jnp.dot(h_lo, wrhi_ref[...], preferred_element_type=F32)
    logits = a[:, :ROUTER_COLS] + a[:, ROUTER_COLS:] + b + rb_ref[...]
    lt = logits.T

    rowi = lax.broadcasted_iota(jnp.int32, (SUBLANES, n), 0).astype(F32)
    valid = rowi < float(N_EXPERT_GROUPS)
    g_log = lt[0:SUBLANES]
    g_max, g_idx = _first_max_index(g_log, valid, rowi)
    p_g = 1.0 / jnp.sum(jnp.where(valid, jnp.exp(g_log - g_max), 0.0), axis=0, keepdims=True)
    e_log = jnp.zeros((SUBLANES, n), F32)
    for g in range(N_EXPERT_GROUPS):
        e_log = jnp.where(g_idx == g, lt[(g + 1) * SUBLANES:(g + 2) * SUBLANES], e_log)
    e_max, i1 = _first_max_index(e_log, valid, rowi)
    e_exp = jnp.where(valid, jnp.exp(e_log - e_max), 0.0)
    e_prob = e_exp / jnp.sum(e_exp, axis=0, keepdims=True)
    v1 = jnp.max(e_prob, axis=0, keepdims=True)
    rest = jnp.logical_and(valid, rowi != i1)
    v2, i2 = _first_max_index(e_prob, rest, rowi)
    denom = v1 + v2
    w_sel = p_g * jnp.where(rowi == i1, v1 / denom, jnp.where(rowi == i2, v2 / denom, 0.0))
    route_ref[:, rows] = jnp.broadcast_to(g_idx, (SUBLANES, n))
    pad = jnp.zeros((LANES - SUBLANES, n), F32)
    gate4_ref[rows, :] = jnp.concatenate([w_sel, pad], axis=0).T
    return jnp.sum(jnp.where(rowi == g_idx, 1.0, 0.0), axis=1, keepdims=True)


def _merge(l, xs, attn_ctx, attn_lat, sgu_n, cbz, conv_w, on, w_out, mod, g2, wr_cat, wr_hi, rb):
    lay = lambda *rest: (lambda i: (l,) + rest)
    rows8 = TM // SUBLANES
    return pl.pallas_call(
        functools.partial(_merge_kernel, split_x=len(xs) == 2),
        grid=(N_TILES,),
        in_specs=_x_specs(xs) + [
            pl.BlockSpec((TM, D_ATTN), lambda i: (_ctx_tile(i), 0)),
            pl.BlockSpec((TM, D_ATTN), lambda i: (_lat_tile(i), 0)),
            pl.BlockSpec((TM, D_SGU), lambda i: (i, 0)),
            pl.BlockSpec((TM, 2 * D_CONV), lambda i: (i, 0)),
            pl.BlockSpec((SUBLANES, 2 * D_CONV), lambda i: (jnp.maximum(i * rows8 - 1, 0), 0)),
            pl.BlockSpec((SUBLANES, 2 * D_CONV),
                         lambda i: (jnp.minimum((i + 1) * rows8, T_ALL // SUBLANES - 1), 0)),
            pl.BlockSpec((None, 3, D_CONV), lay(0, 0)),
            pl.BlockSpec((None, 1, D_MIX), lay(0, 0)),
            pl.BlockSpec((None, D_MIX, D_MODEL), lay(0, 0)),
            _MOD_SPEC(l, 1),
            pl.BlockSpec((None, 1, D_MODEL), lay(0, 0)),
            pl.BlockSpec((None, D_MODEL, 2 * ROUTER_COLS), lay(0, 0)),
            pl.BlockSpec((None, D_MODEL, ROUTER_COLS), lay(0, 0)),
            pl.BlockSpec((None, 1, ROUTER_COLS), lay(0, 0)),
        ],
        out_specs=[
            pl.BlockSpec((TM, D_MODEL), lambda i: (i, 0)),
            pl.BlockSpec((TM, D_MODEL), lambda i: (i, 0)),
            pl.BlockSpec((None, SUBLANES, TM), lambda i: (i, 0, 0)),
            pl.BlockSpec((TM, LANES), lambda i: (i, 0)),
            pl.BlockSpec((None, SUBLANES, LANES), lambda i: (i, 0, 0)),
        ],
        out_shape=[
            jax.ShapeDtypeStruct((T_ALL, D_MODEL), F32),
            jax.ShapeDtypeStruct((T_ALL, D_MODEL), BF16),
            jax.ShapeDtypeStruct((N_TILES, SUBLANES, TM), F32),
            jax.ShapeDtypeStruct((T_ALL, LANES), F32),
            jax.ShapeDtypeStruct((N_TILES, SUBLANES, LANES), jnp.int32),
        ],
        compiler_params=_params(1),
        name=f"merge_l{l}",
    )(*xs, attn_ctx, attn_lat, sgu_n, cbz, cbz, cbz, conv_w, on, w_out, mod, g2, wr_cat, wr_hi, rb)


RB = 144
SORT_ROWS = TM + LANES
SORT_ROWS_PAD = -(-(SORT_ROWS + RB) // BF16_ROWS) * BF16_ROWS
MOE_VMEM_LIMIT = 56 * 1024 * 1024


def _split3(x):
    hi = x.astype(BF16)
    r1 = x - hi.astype(F32)
    mid = r1.astype(BF16)
    lo = (r1 - mid.astype(F32)).astype(BF16)
    return hi, mid, lo


def _moe_kernel(cnt_ref, h_ref, route_ref, gate4_ref, tri_ref, w1_ref, w3_ref, w2_ref, x1_ref, mod_ref, fn_ref,
                *refs, final):
    out_refs, (xs_ref, gs_ref, zs_ref) = refs[:-3], refs[-3:]
    i = pl.program_id(0)
    gate2 = _mod_part(mod_ref, 5, _mod_row_of_tile(i))
    counts = [cnt_ref[i, g] for g in range(N_EXPERT_GROUPS)]
    starts = [jnp.int32(0)]
    for g in range(N_EXPERT_GROUPS - 1):
        starts.append(starts[-1] + (counts[g] + (BF16_ROWS - 1)) // BF16_ROWS * BF16_ROWS)

    one_block_each = counts[0] <= RB
    for g in range(1, N_EXPERT_GROUPS):
        one_block_each = jnp.logical_and(one_block_each, counts[g] <= RB)
    tile = functools.partial(_moe_tile, i, gate2, counts, starts, h_ref, route_ref, gate4_ref, tri_ref, w1_ref,
                             w3_ref, w2_ref, x1_ref, fn_ref, out_refs, xs_ref, gs_ref, zs_ref, final)

    @pl.when(one_block_each)
    def _():
        tile(with_extra_blocks=False)

    @pl.when(jnp.logical_not(one_block_each))
    def _():
        tile(with_extra_blocks=True)


def _moe_tile(i, gate2, counts, starts, h_ref, route_ref, gate4_ref, tri_ref, w1_ref, w3_ref, w2_ref, x1_ref,
              fn_ref, out_refs, xs_ref, gs_ref, zs_ref, final, with_extra_blocks):
    g_idx = route_ref[0:1, :]
    rowi = lax.broadcasted_iota(jnp.int32, (SUBLANES, TM), 0).astype(F32)
    onehot = jnp.where(rowi == g_idx, 1.0, 0.0)
    incl = jnp.dot(onehot.astype(BF16), tri_ref[...], preferred_element_type=F32)
    pos = jnp.sum(onehot * incl, axis=0, keepdims=True) - 1.0
    for g in range(1, N_EXPERT_GROUPS):
        pos = pos + jnp.where(g_idx == float(g), starts[g].astype(F32), 0.0)
    sub = lax.broadcasted_iota(jnp.int32, (SORT_ROWS, TM), 0).astype(F32)
    perm = jnp.where(sub == pos, 1.0, 0.0).astype(BF16)
    pos_col = jnp.broadcast_to(pos, (LANES, TM)).T
    lane = lax.broadcasted_iota(jnp.int32, (TM, SORT_ROWS), 1).astype(F32)
    perm_t = jnp.where(lane == jnp.concatenate([pos_col] * (SORT_ROWS // LANES), axis=1), 1.0, 0.0).astype(BF16)

    xs_ref[:SORT_ROWS, :] = jnp.dot(perm, h_ref[...], preferred_element_type=F32).astype(BF16)
    xs_ref[SORT_ROWS:, :] = jnp.zeros((SORT_ROWS_PAD - SORT_ROWS, D_MODEL), BF16)
    hi, mid, lo = _split3(gate4_ref[...])
    packed = (hi.astype(F32) + pltpu.roll(mid.astype(F32), EXPERTS_PER_GROUP, 1)
              + pltpu.roll(lo.astype(F32), 2 * EXPERTS_PER_GROUP, 1)).astype(BF16)
    gsp = jnp.dot(perm, packed, preferred_element_type=F32)
    gs_ref[:SORT_ROWS, :] = (gsp + pltpu.roll(gsp, LANES - EXPERTS_PER_GROUP, 1)
                             + pltpu.roll(gsp, LANES - 2 * EXPERTS_PER_GROUP, 1))
    gs_ref[SORT_ROWS:, :] = jnp.zeros((SORT_ROWS_PAD - SORT_ROWS, LANES), F32)
    zs_ref[...] = jnp.zeros((SORT_ROWS_PAD, D_MODEL), BF16)

    def block_ffn(g, blk):
        row0 = pl.multiple_of(starts[g] + blk * RB, BF16_ROWS)
        xb = xs_ref[pl.ds(row0, RB), :]
        gb = gs_ref[pl.ds(row0, RB), :]
        cols = []
        for e in range(EXPERTS_PER_GROUP):
            a = jnp.dot(xb, w1_ref[g * EXPERTS_PER_GROUP + e], preferred_element_type=F32)
            b = jnp.dot(xb, w3_ref[g * EXPERTS_PER_GROUP + e], preferred_element_type=F32)
            cols.append((a / (1.0 + jnp.exp(-a)) * b * gb[:, e:e + 1]).astype(BF16))
        z = jnp.dot(jnp.concatenate(cols, axis=1), w2_ref[g], preferred_element_type=F32)
        zs_ref[pl.ds(row0, RB), :] = z.astype(BF16)

    if with_extra_blocks:
        for g in range(N_EXPERT_GROUPS):
            def extra_block(blk, carry, g=g):
                block_ffn(g, blk)
                return carry

            lax.fori_loop(1, (counts[g] + (RB - 1)) // RB, extra_block, 0)
    for g in range(N_EXPERT_GROUPS):
        block_ffn(g, 0)

    y = jnp.dot(perm_t, zs_ref[:SORT_ROWS, :], preferred_element_type=F32)
    out = x1_ref[...] + gate2 * y

    if final:
        octx_ref, olat_ref = out_refs
        out = _rms(out, fn_ref[...])
        olat_ref[...] = out

        @pl.when(i < N_CTX_TILES)
        def _():
            octx_ref[...] = out
    else:
        out_refs[0][...] = out


def _moe(l, counts, h2, route, gate4, tri, w1, w3, w2, x1, mod, final_norm, final):
    tile = pl.BlockSpec((TM, D_MODEL), lambda i, c: (i, 0))
    resident = pl.Buffered(1)
    if final:
        out_specs = [pl.BlockSpec((TM, D_MODEL), lambda i, c: (_ctx_tile(i), 0)),
                     pl.BlockSpec((TM, D_MODEL), lambda i, c: (_lat_tile(i), 0))]
        out_shape = [jax.ShapeDtypeStruct((T_CTX, D_MODEL), F32), jax.ShapeDtypeStruct((T_LAT, D_MODEL), F32)]
    else:
        out_specs = tile
        out_shape = jax.ShapeDtypeStruct((T_ALL, D_MODEL), F32)
    grid_spec = pltpu.PrefetchScalarGridSpec(
        num_scalar_prefetch=1,
        grid=(N_TILES,),
        in_specs=[
            tile,
            pl.BlockSpec((None, SUBLANES, TM), lambda i, c: (i, 0, 0)),
            pl.BlockSpec((TM, LANES), lambda i, c: (i, 0)),
            pl.BlockSpec((TM, TM), lambda i, c: (0, 0), pipeline_mode=resident),
            pl.BlockSpec((N_EXPERTS, D_MODEL, D_EXPERT), lambda i, c: (0, 0, 0), pipeline_mode=resident),
            pl.BlockSpec((N_EXPERTS, D_MODEL, D_EXPERT), lambda i, c: (0, 0, 0), pipeline_mode=resident),
            pl.BlockSpec((N_EXPERT_GROUPS, D_GROUP_HID, D_MODEL), lambda i, c: (0, 0, 0), pipeline_mode=resident),
            tile,
            pl.BlockSpec((None, N_MOD_PARTS, N_MOD_ROWS, D_MODEL), lambda i, c: (l, 0, 0, 0)),
            pl.BlockSpec((1, D_MODEL), lambda i, c: (0, 0)),
        ],
        out_specs=out_specs,
        scratch_shapes=[pltpu.VMEM((SORT_ROWS_PAD, D_MODEL), BF16),
                        pltpu.VMEM((SORT_ROWS_PAD, LANES), F32),
                        pltpu.VMEM((SORT_ROWS_PAD, D_MODEL), BF16)],
    )
    return pl.pallas_call(
        functools.partial(_moe_kernel, final=final),
        grid_spec=grid_spec,
        out_shape=out_shape,
        compiler_params=pltpu.CompilerParams(dimension_semantics=("arbitrary",),
                                             vmem_limit_bytes=MOE_VMEM_LIMIT),
        name=f"moe_l{l}",
    )(counts, h2, route, gate4, tri, w1, w3, w2, x1, mod, final_norm)


def _rope_tables():
    rows = DEC_SEQ // GRID_W
    row = np.repeat(np.arange(rows, dtype=np.float32), GRID_W)
    col = np.tile(np.arange(GRID_W, dtype=np.float32), rows)
    half = ROPE_AXIS_DIM // 2
    inv_freq = (1.0 / (np.float32(ROPE_THETA) ** (np.arange(half, dtype=np.float32) * np.float32(2.0)
                                                  / np.float32(ROPE_AXIS_DIM)))).astype(np.float32)
    ar = row[:, None] * inv_freq
    ac = col[:, None] * inv_freq
    cos64 = np.concatenate([np.cos(ar), np.cos(ar), np.cos(ac), np.cos(ac)], axis=1)
    sin64 = np.concatenate([-np.sin(ar), np.sin(ar), -np.sin(ac), np.sin(ac)], axis=1)
    cos_t = np.concatenate([np.ones((TM, LANES), np.float32), np.tile(cos64, (1, 2))], axis=0)
    sin_t = np.concatenate([np.zeros((TM, LANES), np.float32), np.tile(sin64, (1, 2))], axis=0)
    return jnp.asarray(cos_t, F32), jnp.asarray(sin_t, F32)


def _router_weights(router_g_w, router_g_b, router_e_w, router_e_b):
    zw = jnp.zeros((DEPTH, D_MODEL, SUBLANES - N_EXPERT_GROUPS), F32)
    zb = jnp.zeros((DEPTH, SUBLANES - N_EXPERT_GROUPS), F32)
    w_cols, b_cols = [router_g_w, zw], [router_g_b, zb]
    for g in range(N_EXPERT_GROUPS):
        w_cols += [router_e_w[:, g], zw]
        b_cols += [router_e_b[:, g], zb]
    used = (1 + N_EXPERT_GROUPS) * SUBLANES
    w_cols.append(jnp.zeros((DEPTH, D_MODEL, ROUTER_COLS - used), F32))
    b_cols.append(jnp.zeros((DEPTH, ROUTER_COLS - used), F32))
    w = jnp.concatenate(w_cols, axis=2)
    b = jnp.concatenate(b_cols, axis=1).reshape(DEPTH, 1, ROUTER_COLS)
    w_hi = w.astype(BF16)
    w_lo = (w - w_hi.astype(F32)).astype(BF16)
    return jnp.concatenate([w_hi, w_lo], axis=2), w_hi, b


def kernel(x_prompt, x_sample, cache_k, cache_v, c, c_ctx, w_ada, b_ada, norm1, w_in, q_norm, k_norm,
           sgu_w, sgu_b, conv_w, out_norm, w_out, norm2, router_g_w, router_g_b, router_e_w,
           router_e_b, moe_w1, moe_w3, moe_w2, final_norm):
    xs = (x_prompt.reshape(T_CTX, D_MODEL), x_sample.reshape(T_LAT, D_MODEL))

    cvecs = jnp.concatenate([c_ctx[None, :], c, jnp.zeros((N_MOD_ROWS - 1 - DEC_BATCH, D_MODEL), F32)], axis=0)
    mod = _modulation(cvecs.T, w_ada, b_ada)

    w_in_b = w_in.astype(BF16)
    w_out_b = w_out.astype(BF16)
    g1 = norm1.reshape(DEPTH, 1, D_MODEL)
    g2 = norm2.reshape(DEPTH, 1, D_MODEL)
    on = out_norm.reshape(DEPTH, 1, D_MIX)
    qg = jnp.tile(q_norm, (1, N_HEADS)).reshape(DEPTH, 1, D_ATTN)
    kg = jnp.tile(k_norm, (1, N_KV_HEADS)).reshape(DEPTH, 1, D_KV)
    cos_t, sin_t = _rope_tables()
    bq = jnp.asarray(np.kron(np.eye(N_HEADS), np.full((HEAD_DIM, HEAD_DIM), 1.0 / HEAD_DIM)), BF16)
    sgu_w_pairs = sgu_w.astype(BF16).reshape(DEPTH, N_SGU_GROUPS // 2, 2, CHUNK, CHUNK)
    sgu_w_pairs = jnp.concatenate([sgu_w_pairs[:, :, 0], sgu_w_pairs[:, :, 1]], axis=-1)
    sgu_bias = jnp.repeat(jnp.swapaxes(sgu_b, 1, 2), SGU_GROUP_DIM, axis=2)
    wr_cat, wr_hi, rb = _router_weights(router_g_w, router_g_b, router_e_w, router_e_b)
    fn = final_norm.reshape(1, D_MODEL)
    tri = jnp.asarray(np.triu(np.ones((TM, TM), np.float32)), BF16)
    ck = cache_k.reshape(DEC_BATCH, DEPTH, PAST_LEN, D_KV)
    cv = cache_v.reshape(DEC_BATCH, DEPTH, PAST_LEN, D_KV)

    ctx_ks, ctx_vs = [], []
    for l in range(DEPTH):
        qt, k, vt, k32, v32, sgu_n, cbz = _proj(l, xs, mod, g1, w_in_b, qg, kg, cos_t, sin_t, bq,
                                                sgu_w_pairs, sgu_bias, on)
        attn_ctx = _attn_ctx(l, qt, k, vt)
        attn_lat, w1, w3, w2 = _attn_lat(l, qt, k, vt, ck, cv, qg, kg, moe_w1, moe_w3, moe_w2)
        w2 = w2.reshape(N_EXPERT_GROUPS, D_GROUP_HID, D_MODEL)
        x1, h2, route, gate4, cnt = _merge(l, xs, attn_ctx, attn_lat, sgu_n, cbz, conv_w, on, w_out_b, mod, g2,
                                           wr_cat, wr_hi, rb)
        counts = cnt[:, :N_EXPERT_GROUPS, 0]
        out = _moe(l, counts, h2, route, gate4, tri, w1, w3, w2, x1, mod, fn, final=(l == DEPTH - 1))
        xs = (out,)
        ctx_ks.append(k32.reshape(BATCH, SEQ, N_KV_HEADS, HEAD_DIM))
        ctx_vs.append(v32.reshape(BATCH, SEQ, N_KV_HEADS, HEAD_DIM))

    y_prompt, y_sample = out
    return (y_prompt.reshape(BATCH, SEQ, D_MODEL), y_sample.reshape(DEC_BATCH, DEC_SEQ, D_MODEL),
            jnp.stack(ctx_ks, axis=1), jnp.stack(ctx_vs, axis=1))
```

```python
import functools
import math

import jax
import jax.numpy as jnp
import numpy as np
from jax import lax
from jax.experimental import pallas as pl
from jax.experimental.pallas import tpu as pltpu

D_MODEL = 1024
BATCH = 16
SEQ = 256
DEPTH = 2
DEC_BATCH = 2
DEC_SEQ = 4096
PAST_LEN = 512
GRID_W = 64
N_HEADS = 8
N_KV_HEADS = 2
HEAD_DIM = 64
ROPE_AXIS_DIM = HEAD_DIM // 2
ROPE_THETA = 10000.0
D_ATTN = N_HEADS * HEAD_DIM
D_KV = N_KV_HEADS * HEAD_DIM
CHUNK = 128
N_SGU_GROUPS = 4
SGU_GROUP_DIM = 64
D_SGU = N_SGU_GROUPS * SGU_GROUP_DIM
D_CONV = 256
D_MIX = D_ATTN + D_SGU + D_CONV
D_IN = D_ATTN + 2 * D_KV + 2 * D_SGU + 3 * D_CONV
N_EXPERT_GROUPS = 4
EXPERTS_PER_GROUP = 4
N_EXPERTS = N_EXPERT_GROUPS * EXPERTS_PER_GROUP
D_EXPERT = 256
D_GROUP_HID = EXPERTS_PER_GROUP * D_EXPERT
EPS = 1e-6

T_CTX = BATCH * SEQ
T_LAT = DEC_BATCH * DEC_SEQ
T_ALL = T_CTX + T_LAT
TM = 512
N_TILES = T_ALL // TM
N_CTX_TILES = T_CTX // TM
LAT_TILES_PER_BATCH = DEC_SEQ // TM
TM_MOE = 1024
LANES = 128
SUBLANES = 8
BF16_ROWS = 16
N_MOD_ROWS = 8
N_MOD_PARTS = 6
ROUTER_COLS = 128
VMEM_LIMIT = 48 * 1024 * 1024

OFF_Q = 0
OFF_KV = D_ATTN
OFF_SGU = OFF_KV + 2 * D_KV
OFF_CONV = OFF_SGU + 2 * D_SGU

BF16 = jnp.bfloat16
F32 = jnp.float32
NEG_BIG = -1e30
Q_SCALE = HEAD_DIM ** -0.5 * math.log2(math.e)


def _params(n_grid_dims):
    return pltpu.CompilerParams(
        dimension_semantics=("arbitrary",) * n_grid_dims,
        vmem_limit_bytes=VMEM_LIMIT,
    )


def _mod_row_of_tile(i, tile=TM):
    n_ctx = T_CTX // tile
    return jnp.where(i < n_ctx, 0, 1 + (i - n_ctx) // (DEC_SEQ // tile))


def _mod_part(mod_ref, part, row):
    return mod_ref[part, pl.ds(row, 1), :]


def _rope_block_of_tile(i):
    return jnp.where(i < N_CTX_TILES, 0, 1 + (i - N_CTX_TILES) % LAT_TILES_PER_BATCH)


def _rms(x, gain):
    ms = jnp.mean(x * x, axis=-1, keepdims=True)
    return x * lax.rsqrt(ms + EPS) * gain


def _ctx_tile(i):
    return jnp.minimum(i, N_CTX_TILES - 1)


def _lat_tile(i):
    return jnp.maximum(i - N_CTX_TILES, 0)


MOD_PARTS_PER_STEP = 2
MOD_TN = MOD_PARTS_PER_STEP * D_MODEL


def _mod_kernel(ct_ref, w_ref, b_ref, o_ref):
    c = ct_ref[...]
    s = c / (1.0 + jnp.exp(-c))
    row = lax.broadcasted_iota(jnp.int32, (N_MOD_ROWS, D_MODEL), 0)
    for part in range(MOD_PARTS_PER_STEP):
        cols = slice(part * D_MODEL, (part + 1) * D_MODEL)
        w = w_ref[:, cols]
        out = jnp.zeros((N_MOD_ROWS, D_MODEL), F32)
        for r in range(1 + DEC_BATCH):
            acc = jnp.sum(w * s[:, r:r + 1], axis=0, keepdims=True)
            out = jnp.where(row == r, acc, out)
        o_ref[part] = out + b_ref[:, cols]


def _modulation(cvecs_t, w_ada, b_ada):
    n_col = N_MOD_PARTS * D_MODEL
    return pl.pallas_call(
        _mod_kernel,
        grid=(DEPTH, n_col // MOD_TN),
        in_specs=[
            pl.BlockSpec((D_MODEL, N_MOD_ROWS), lambda l, j: (0, 0)),
            pl.BlockSpec((None, D_MODEL, MOD_TN), lambda l, j: (l, 0, j)),
            pl.BlockSpec((None, 1, MOD_TN), lambda l, j: (l, 0, j)),
        ],
        out_specs=pl.BlockSpec((None, MOD_PARTS_PER_STEP, N_MOD_ROWS, D_MODEL), lambda l, j: (l, j, 0, 0)),
        out_shape=jax.ShapeDtypeStruct((DEPTH, N_MOD_PARTS, N_MOD_ROWS, D_MODEL), F32),
        compiler_params=_params(2),
        name="adaln_modulation",
    )(cvecs_t, w_ada, b_ada.reshape(DEPTH, 1, n_col))


_MOD_SPEC = lambda l, nd: pl.BlockSpec(
    (None, N_MOD_PARTS, N_MOD_ROWS, D_MODEL), lambda *idx: (l, 0, 0, 0))


def _swap16(x):
    lane = lax.broadcasted_iota(jnp.int32, x.shape, 1)
    up = pltpu.roll(x, 16, 1)
    down = pltpu.roll(x, LANES - 16, 1)
    return jnp.where((lane & 16) != 0, up, down)


def _rope(x, cos_t, sin_t):
    cols = []
    for j in range(x.shape[1] // LANES):
        xc = x[:, j * LANES:(j + 1) * LANES]
        cols.append(xc * cos_t + _swap16(xc) * sin_t)
    return cols[0] if len(cols) == 1 else jnp.concatenate(cols, axis=1)


def _proj_kernel(*refs, split_x):
    n_x = 2 if split_x else 1
    x_refs, refs, (scr_even, scr_odd) = refs[:n_x], refs[n_x:-2], refs[-2:]
    step = functools.partial(_proj_step, x_refs, refs, split_x)
    i = pl.program_id(0)

    @pl.when(i == 0)
    def _():
        scr_odd[...] = jnp.zeros((TM, D_IN), F32)

    last = i == N_TILES

    @pl.when(jnp.logical_and(i % 2 == 0, jnp.logical_not(last)))
    def _():
        step(scr_even, scr_odd)

    @pl.when(jnp.logical_and(i % 2 == 1, jnp.logical_not(last)))
    def _():
        step(scr_odd, scr_even)

    @pl.when(last)
    def _():
        step(None, scr_even if N_TILES % 2 == 1 else scr_odd)


def _proj_step(x_refs, refs, split_x, new_ref, proj):
    (mod_ref, g1_ref, w_ref, qg_ref, kg_ref, cos_ref, sin_ref, bq_ref, sw_ref, sb_ref, on_ref,
     qt_ref, k_ref, vt_ref, k32_ref, v32_ref, sgu_ref, cbz_ref) = refs
    i = pl.program_id(0)
    cur = _stage1_tile(i)

    is_ctx = i <= N_CTX_TILES
    cos_t = cos_ref[...]
    sin_t = sin_ref[...]

    q = proj[:, OFF_Q:OFF_Q + D_ATTN]
    q2 = (q * q).astype(BF16)
    half = D_ATTN // 2
    qms = jnp.concatenate(
        [jnp.dot(q2[:, :half], bq_ref[:half, :half], preferred_element_type=F32),
         jnp.dot(q2[:, half:], bq_ref[half:, half:], preferred_element_type=F32)], axis=1)
    qn = q * lax.rsqrt(qms + EPS) * qg_ref[...]
    qr = _rope(qn, cos_t, sin_t) * Q_SCALE
    qt_ref[...] = qr.T.astype(BF16)

    k = proj[:, OFF_KV:OFF_KV + D_KV]
    v = proj[:, OFF_KV + D_KV:OFF_KV + 2 * D_KV]
    kms = jnp.dot((k * k).astype(BF16), bq_ref[:D_KV, :D_KV], preferred_element_type=F32)
    kn = k * lax.rsqrt(kms + EPS) * kg_ref[...]
    k_ref[...] = _rope(kn, cos_t, sin_t).astype(BF16)
    vt_ref[...] = v.T.astype(BF16)

    su = proj[:, OFF_SGU:OFF_SGU + D_SGU]
    sv = proj[:, OFF_SGU + D_SGU:OFF_SGU + 2 * D_SGU]
    n_chunks = TM // CHUNK
    sv_wide = jnp.concatenate([sv[n * CHUNK:(n + 1) * CHUNK, :] for n in range(n_chunks)], axis=1)
    grp = lax.broadcasted_iota(jnp.int32, (CHUNK, n_chunks * D_SGU), 1) // SGU_GROUP_DIM % N_SGU_GROUPS
    mixed_wide = jnp.zeros((CHUNK, n_chunks * D_SGU), F32)
    for pair in range(N_SGU_GROUPS // 2):
        rhs = jnp.concatenate(
            [jnp.where(grp == 2 * pair, sv_wide, 0.0), jnp.where(grp == 2 * pair + 1, sv_wide, 0.0)],
            axis=0).astype(BF16)
        mixed_wide = mixed_wide + jnp.dot(sw_ref[pair], rhs, preferred_element_type=F32)
    on_sgu = on_ref[:, D_ATTN:D_ATTN + D_SGU]
    for n in range(n_chunks):
        sgu = su[n * CHUNK:(n + 1) * CHUNK, :] * (mixed_wide[:, n * D_SGU:(n + 1) * D_SGU] + sb_ref[...])
        sgu_ref[n * CHUNK:(n + 1) * CHUNK, :] = _rms(sgu, on_sgu).astype(BF16)

    cbz_ref[:, :D_CONV] = proj[:, OFF_CONV:OFF_CONV + D_CONV]
    cbz_ref[:, D_CONV:] = (proj[:, OFF_CONV + D_CONV:OFF_CONV + 2 * D_CONV]
                           * proj[:, OFF_CONV + 2 * D_CONV:OFF_CONV + 3 * D_CONV])

    if new_ref is not None:
        x = jnp.where(cur < N_CTX_TILES, x_refs[0][...], x_refs[1][...]) if split_x else x_refs[0][...]
        mrow = _mod_row_of_tile(cur)
        h = _rms(x, g1_ref[...] * (1.0 + _mod_part(mod_ref, 1, mrow))) + _mod_part(mod_ref, 0, mrow)
        new_ref[...] = jnp.dot(h.astype(BF16), w_ref[...], preferred_element_type=F32)

    @pl.when(is_ctx)
    def _():
        k32_ref[...] = kn
        v32_ref[...] = v


def _x_specs(xs, tile_of_step=lambda i: i):
    if len(xs) == 1:
        return [pl.BlockSpec((TM, D_MODEL), lambda i: (tile_of_step(i), 0))]
    return [pl.BlockSpec((TM, D_MODEL), lambda i: (_ctx_tile(tile_of_step(i)), 0)),
            pl.BlockSpec((TM, D_MODEL), lambda i: (_lat_tile(tile_of_step(i)), 0))]


def _stage1_tile(i):
    return jnp.minimum(i, N_TILES - 1)


def _stage2_tile(i):
    return jnp.maximum(i - 1, 0)


def _proj(l, xs, mod, g1, w_in, qg, kg, cos_t, sin_t, bq, sgu_w, sgu_b, on):
    lay = lambda *rest: (lambda i: (l,) + rest)
    s2 = _stage2_tile
    return pl.pallas_call(
        functools.partial(_proj_kernel, split_x=len(xs) == 2),
        grid=(N_TILES + 1,),
        in_specs=_x_specs(xs, _stage1_tile) + [
            _MOD_SPEC(l, 1),
            pl.BlockSpec((None, 1, D_MODEL), lay(0, 0)),
            pl.BlockSpec((None, D_MODEL, D_IN), lay(0, 0)),
            pl.BlockSpec((None, 1, D_ATTN), lay(0, 0)),
            pl.BlockSpec((None, 1, D_KV), lay(0, 0)),
            pl.BlockSpec((TM, LANES), lambda i: (_rope_block_of_tile(s2(i)), 0)),
            pl.BlockSpec((TM, LANES), lambda i: (_rope_block_of_tile(s2(i)), 0)),
            pl.BlockSpec((D_ATTN, D_ATTN), lambda i: (0, 0)),
            pl.BlockSpec((None, N_SGU_GROUPS // 2, CHUNK, 2 * CHUNK), lay(0, 0, 0)),
            pl.BlockSpec((None, CHUNK, D_SGU), lay(0, 0)),
            pl.BlockSpec((None, 1, D_MIX), lay(0, 0)),
        ],
        out_specs=[
            pl.BlockSpec((None, D_ATTN, TM), lambda i: (s2(i), 0, 0)),
            pl.BlockSpec((TM, D_KV), lambda i: (s2(i), 0)),
            pl.BlockSpec((None, D_KV, TM), lambda i: (s2(i), 0, 0)),
            pl.BlockSpec((TM, D_KV), lambda i: (_ctx_tile(s2(i)), 0)),
            pl.BlockSpec((TM, D_KV), lambda i: (_ctx_tile(s2(i)), 0)),
            pl.BlockSpec((TM, D_SGU), lambda i: (s2(i), 0)),
            pl.BlockSpec((TM, 2 * D_CONV), lambda i: (s2(i), 0)),
        ],
        out_shape=[
            jax.ShapeDtypeStruct((N_TILES, D_ATTN, TM), BF16),
            jax.ShapeDtypeStruct((T_ALL, D_KV), BF16),
            jax.ShapeDtypeStruct((N_TILES, D_KV, TM), BF16),
            jax.ShapeDtypeStruct((T_CTX, D_KV), F32),
            jax.ShapeDtypeStruct((T_CTX, D_KV), F32),
            jax.ShapeDtypeStruct((T_ALL, D_SGU), BF16),
            jax.ShapeDtypeStruct((T_ALL, 2 * D_CONV), F32),
        ],
        scratch_shapes=[pltpu.VMEM((TM, D_IN), F32), pltpu.VMEM((TM, D_IN), F32)],
        compiler_params=_params(1),
        name=f"proj_l{l}",
    )(*xs, mod, g1, w_in, qg, kg, cos_t, sin_t, bq, sgu_w, sgu_b, on)


ACC_ROWS = HEAD_DIM + BF16_ROWS
N_PAIRS = N_HEADS // 2
PAIRS_PER_KV = N_PAIRS // N_KV_HEADS


def _pair_queries(qt, kv_idx, tq):
    qf = qt.astype(F32)
    mine = lax.broadcasted_iota(jnp.int32, (2 * HEAD_DIM, tq), 0) // HEAD_DIM == kv_idx
    qe = qf[:HEAD_DIM]
    qo = qf[HEAD_DIM:]
    rhs = jnp.concatenate(
        [jnp.where(mine, jnp.concatenate([qe, qe], axis=0), 0.0),
         jnp.where(mine, jnp.concatenate([qo, qo], axis=0), 0.0)], axis=1)
    return rhs.astype(BF16)


def _with_ones(vt_c):
    return jnp.concatenate([vt_c, jnp.ones((BF16_ROWS, vt_c.shape[1]), BF16)], axis=0)


def _flash_pairs(rhs_list, chunk_lists, tq, shift=None):
    n_pairs, n_chunks = len(rhs_list), len(chunk_lists[0])
    scores = lambda s, c: jnp.dot(chunk_lists[s][c][0](), rhs_list[s], preferred_element_type=F32)
    m = [jnp.full((1, 2 * tq), NEG_BIG, F32)] * n_pairs
    acc = [jnp.zeros((ACC_ROWS, 2 * tq), F32)] * n_pairs
    st = [scores(s, 0) for s in range(n_pairs)]
    for c in range(n_chunks):
        for s in range(n_pairs):
            if shift is None:
                m_new = jnp.maximum(m[s], jnp.max(st[s], axis=0, keepdims=True))
            st_next = scores(s, c + 1) if c + 1 < n_chunks else None
            if shift is None:
                alpha = jnp.exp2(m[s] - m_new)
                p = jnp.exp2(st[s] - m_new).astype(BF16)
                acc[s] = alpha * acc[s] + jnp.dot(chunk_lists[s][c][1](), p, preferred_element_type=F32)
                m[s] = m_new
            else:
                p = jnp.exp2(st[s] - shift).astype(BF16)
                acc[s] = acc[s] + jnp.dot(chunk_lists[s][c][1](), p, preferred_element_type=F32)
            st[s] = st_next
    outs = []
    for s in range(n_pairs):
        o = acc[s][:HEAD_DIM] * (1.0 / acc[s][HEAD_DIM:HEAD_DIM + 1])
        outs.append(jnp.concatenate([o[:, :tq], o[:, tq:]], axis=0))
    return outs


def _attn_ctx_kernel(qt_ref, k_ref, vt_ref, o_ref):
    tq = SEQ
    rhs_list, chunk_lists = [], []
    for pair in range(N_PAIRS):
        kv_idx = pair // PAIRS_PER_KV
        rhs_list.append(_pair_queries(qt_ref[pair * LANES:(pair + 1) * LANES, :], kv_idx, tq))
        chunk_lists.append([(lambda: k_ref[...],
                             lambda kv_idx=kv_idx: _with_ones(vt_ref[kv_idx * HEAD_DIM:(kv_idx + 1) * HEAD_DIM, :]))])
    for pair, o_t in enumerate(_flash_pairs(rhs_list, chunk_lists, tq)):
        o_ref[:, pair * LANES:(pair + 1) * LANES] = o_t.T.astype(BF16)


def _attn_ctx(l, qt, k, vt):
    halves = TM // SEQ
    return pl.pallas_call(
        _attn_ctx_kernel,
        grid=(BATCH,),
        in_specs=[
            pl.BlockSpec((None, D_ATTN, SEQ), lambda b: (b // halves, 0, b % halves)),
            pl.BlockSpec((SEQ, D_KV), lambda b: (b, 0)),
            pl.BlockSpec((None, D_KV, SEQ), lambda b: (b // halves, 0, b % halves)),
        ],
        out_specs=pl.BlockSpec((SEQ, D_ATTN), lambda b: (b, 0)),
        out_shape=jax.ShapeDtypeStruct((T_CTX, D_ATTN), BF16),
        compiler_params=_params(1),
        name=f"attn_ctx_l{l}",
    )(qt, k, vt)


KC = 512
LAT_PAIRS_PER_STEP = 2


MAX_SHIFT = 60.0
BOUND_SLACK = 1.05


def _score_bound_sq(qg_ref, kg_ref, ck):
    qg = qg_ref[...]
    kg = kg_ref[...]
    gq2 = jnp.max(qg * qg, axis=1, keepdims=True)
    k2 = HEAD_DIM * jnp.max(kg * kg, axis=1, keepdims=True)
    if ck is not None:
        sq = ck * ck
        first = lax.broadcasted_iota(jnp.int32, sq.shape, 1) < HEAD_DIM
        n0 = jnp.sum(jnp.where(first, sq, 0.0), axis=1, keepdims=True)
        n1 = jnp.sum(jnp.where(first, 0.0, sq), axis=1, keepdims=True)
        k2 = jnp.maximum(k2, jnp.max(jnp.maximum(n0, n1), axis=0, keepdims=True))
    return (BOUND_SLACK * Q_SCALE) ** 2 * HEAD_DIM * gq2 * k2


def _run_with_score_bound(run, bound):
    small = bound <= MAX_SHIFT

    @pl.when(small)
    def _():
        run(bound)

    @pl.when(jnp.logical_not(small))
    def _():
        run(None)


def _attn_lat_kernel(qt_ref, k_ref, vt_ref, ck_ref, cv_ref, qg_ref, kg_ref, w1_ref, w3_ref, w2_ref,
                     o_ref, w1b_ref, w3b_ref, w2b_ref, bound_ref):
    tq = TM

    def run(shift):
        w1b_ref[...] = w1_ref[...].astype(BF16)
        w3b_ref[...] = w3_ref[...].astype(BF16)
        w2b_ref[...] = w2_ref[...].astype(BF16)
        cvt = cv_ref[...].T
        rhs_list, chunk_lists = [], []
        for j in range(LAT_PAIRS_PER_STEP):
            kv_idx = (pl.program_id(2) * LAT_PAIRS_PER_STEP + j) // PAIRS_PER_KV
            rhs_list.append(_pair_queries(qt_ref[j * LANES:(j + 1) * LANES, :], kv_idx, tq))
            v_row = pl.multiple_of(kv_idx * HEAD_DIM, HEAD_DIM)

            def latent_chunk(c, v_row=v_row):
                tile, lane0 = (c * KC) // TM, (c * KC) % TM
                return (lambda: k_ref[c * KC:(c + 1) * KC, :],
                        lambda: _with_ones(vt_ref[tile, pl.ds(v_row, HEAD_DIM), lane0:lane0 + KC]))

            def cached_chunk(c, kv_idx=kv_idx):
                return (lambda: ck_ref[c * KC:(c + 1) * KC, :].astype(BF16),
                        lambda: _with_ones(jnp.where(kv_idx == 0, cvt[:HEAD_DIM, c * KC:(c + 1) * KC],
                                                     cvt[HEAD_DIM:, c * KC:(c + 1) * KC]).astype(BF16)))

            chunks = [latent_chunk(c) for c in range(DEC_SEQ // KC)]
            chunks += [cached_chunk(c) for c in range(PAST_LEN // KC)]
            chunk_lists.append(chunks)
        for j, o_t in enumerate(_flash_pairs(rhs_list, chunk_lists, tq, shift)):
            o_ref[:, j * LANES:(j + 1) * LANES] = o_t.T.astype(BF16)

    @pl.when(jnp.logical_and(pl.program_id(1) == 0, pl.program_id(2) == 0))
    def _():
        bound_ref[0] = jnp.sqrt(_score_bound_sq(qg_ref, kg_ref, ck_ref[...]))[0, 0]

    _run_with_score_bound(run, bound_ref[0])


def _attn_lat(l, qt, k, vt, cache_k, cache_v, qg, kg, moe_w1, moe_w3, moe_w2):
    n_q = DEC_SEQ // TM
    first = N_CTX_TILES
    ctx_blocks = T_CTX // DEC_SEQ
    width = LAT_PAIRS_PER_STEP * LANES
    n_p = N_PAIRS // LAT_PAIRS_PER_STEP
    per_expert = DEC_BATCH * n_q * n_p // N_EXPERTS
    step = lambda b, i, p: (b * n_q + i) * n_p + p
    w_in_spec = lambda rows, cols: pl.BlockSpec(
        (None, None, rows // per_expert, cols),
        lambda b, i, p: (l, step(b, i, p) // per_expert, step(b, i, p) % per_expert, 0))
    w_out_spec = lambda rows, cols: pl.BlockSpec(
        (None, rows // per_expert, cols),
        lambda b, i, p: (step(b, i, p) // per_expert, step(b, i, p) % per_expert, 0))
    return pl.pallas_call(
        _attn_lat_kernel,
        grid=(DEC_BATCH, n_q, n_p),
        in_specs=[
            pl.BlockSpec((None, width, TM), lambda b, i, p: (first + b * n_q + i, p, 0)),
            pl.BlockSpec((DEC_SEQ, D_KV), lambda b, i, p: (ctx_blocks + b, 0)),
            pl.BlockSpec((n_q, D_KV, TM), lambda b, i, p: (ctx_blocks + b, 0, 0)),
            pl.BlockSpec((None, None, PAST_LEN, D_KV), lambda b, i, p: (b, l, 0, 0)),
            pl.BlockSpec((None, None, PAST_LEN, D_KV), lambda b, i, p: (b, l, 0, 0)),
            pl.BlockSpec((None, 1, D_ATTN), lambda b, i, p: (l, 0, 0)),
            pl.BlockSpec((None, 1, D_KV), lambda b, i, p: (l, 0, 0)),
            w_in_spec(D_MODEL, D_EXPERT),
            w_in_spec(D_MODEL, D_EXPERT),
            w_in_spec(D_EXPERT, D_MODEL),
        ],
        out_specs=[
            pl.BlockSpec((TM, width), lambda b, i, p: (b * n_q + i, p)),
            w_out_spec(D_MODEL, D_EXPERT),
            w_out_spec(D_MODEL, D_EXPERT),
            w_out_spec(D_EXPERT, D_MODEL),
        ],
        out_shape=[
            jax.ShapeDtypeStruct((T_LAT, D_ATTN), BF16),
            jax.ShapeDtypeStruct((N_EXPERTS, D_MODEL, D_EXPERT), BF16),
            jax.ShapeDtypeStruct((N_EXPERTS, D_MODEL, D_EXPERT), BF16),
            jax.ShapeDtypeStruct((N_EXPERTS, D_EXPERT, D_MODEL), BF16),
        ],
        scratch_shapes=[pltpu.SMEM((1,), F32)],
        compiler_params=_params(3),
        name=f"attn_lat_l{l}",
    )(qt, k, vt, cache_k, cache_v, qg, kg, moe_w1, moe_w3, moe_w2)


def _first_max_index(vals, valid, rowf):
    masked = jnp.where(valid, vals, -jnp.inf)
    mx = jnp.max(masked, axis=0, keepdims=True)
    idx = jnp.min(jnp.where(masked == mx, rowf, float(SUBLANES)), axis=0, keepdims=True)
    return mx, idx


def _merge_kernel(*refs, split_x):
    n_x = 2 if split_x else 1
    x_refs, refs = refs[:n_x], refs[n_x:]
    (actx_ref, alat_ref, sgu_ref, cbz_ref, prev_ref, next_ref, cw_ref, on_ref,
     wo_ref, mod_ref, g2_ref, wr_ref, wrhi_ref, rb_ref, x1_ref, h2_ref, route_ref, gate4_ref, cnt_ref) = refs
    cur = pl.program_id(0)
    is_ctx = cur < N_CTX_TILES
    mrow = _mod_row_of_tile(cur)
    on = on_ref[...]

    cbz = cbz_ref[...]
    cb = cbz[:, :D_CONV]
    z = cbz[:, D_CONV:]
    seq_mask = jnp.where(is_ctx, SEQ - 1, DEC_SEQ - 1)
    row = lax.broadcasted_iota(jnp.int32, (TM, D_CONV), 0)
    pos = (cur * TM + row) & seq_mask
    z_prev = jnp.where(row == 0, prev_ref[SUBLANES - 1:SUBLANES, D_CONV:], pltpu.roll(z, 1, 0))
    z_next = jnp.where(row == TM - 1, next_ref[0:1, D_CONV:], pltpu.roll(z, TM - 1, 0))
    z_prev = jnp.where(pos == 0, 0.0, z_prev)
    z_next = jnp.where(pos == seq_mask, 0.0, z_next)
    cw = cw_ref[...]
    conv = cb * (z_prev * cw[0:1, :] + z * cw[1:2, :] + z_next * cw[2:3, :])
    conv_n = _rms(conv, on[:, D_ATTN + D_SGU:]).astype(BF16)

    counts = jnp.zeros((SUBLANES, 1), F32)
    for sub in range(TM // MERGE_ROWS):
        rows = slice(sub * MERGE_ROWS, (sub + 1) * MERGE_ROWS)
        attn = jnp.where(is_ctx, actx_ref[rows, :], alat_ref[rows, :]).astype(F32)
        attn_n = _rms(attn, on[:, :D_ATTN]).astype(BF16)
        merged = jnp.concatenate([attn_n, sgu_ref[rows, :], conv_n[rows, :]], axis=1)
        x = jnp.where(is_ctx, x_refs[0][rows, :], x_refs[1][rows, :]) if split_x else x_refs[0][rows, :]
        x1 = x + _mod_part(mod_ref, 2, mrow) * jnp.dot(merged, wo_ref[...], preferred_element_type=F32)
        x1_ref[rows, :] = x1
        counts = counts + _route_rows(x1, rows, mrow, mod_ref, g2_ref, wr_ref, wrhi_ref, rb_ref,
                                      h2_ref, route_ref, gate4_ref)
    cnt_ref[...] = jnp.broadcast_to(counts, (SUBLANES, LANES)).astype(jnp.int32)


MERGE_ROWS = TM


def _route_rows(x1, rows, mrow, mod_ref, g2_ref, wr_ref, wrhi_ref, rb_ref, h2_ref, route_ref, gate4_ref):
    n = x1.shape[0]
    h2 = _rms(x1, g2_ref[...] * (1.0 + _mod_part(mod_ref, 4, mrow))) + _mod_part(mod_ref, 3, mrow)
    hb = h2.astype(BF16)
    h2_ref[rows, :] = hb

    h_lo = (h2 - hb.astype(F32)).astype(BF16)
    a = jnp.dot(hb, wr_ref[...], preferred_element_type=F32)
    b = jnp.dot(h_lo, wrhi_ref[...], preferred_element_type=F32)
    logits = a[:, :ROUTER_COLS] + a[:, ROUTER_COLS:] + b + rb_ref[...]
    lt = logits.T

    rowi = lax.broadcasted_iota(jnp.int32, (SUBLANES, n), 0).astype(F32)
    valid = rowi < float(N_EXPERT_GROUPS)
    g_log = lt[0:SUBLANES]
    g_max, g_idx = _first_max_index(g_log, valid, rowi)
    p_g = 1.0 / jnp.sum(jnp.where(valid, jnp.exp(g_log - g_max), 0.0), axis=0, keepdims=True)
    e_log = jnp.zeros((SUBLANES, n), F32)
    for g in range(N_EXPERT_GROUPS):
        e_log = jnp.where(g_idx == g, lt[(g + 1) * SUBLANES:(g + 2) * SUBLANES], e_log)
    e_max, i1 = _first_max_index(e_log, valid, rowi)
    e_exp = jnp.where(valid, jnp.exp(e_log - e_max), 0.0)
    e_prob = e_exp / jnp.sum(e_exp, axis=0, keepdims=True)
    v1 = jnp.max(e_prob, axis=0, keepdims=True)
    rest = jnp.logical_and(valid, rowi != i1)
    v2, i2 = _first_max_index(e_prob, rest, rowi)
    denom = v1 + v2
    w_sel = p_g * jnp.where(rowi == i1, v1 / denom, jnp.where(rowi == i2, v2 / denom, 0.0))
    route_ref[:, rows] = jnp.broadcast_to(g_idx, (SUBLANES, n))
    pad = jnp.zeros((LANES - SUBLANES, n), F32)
    gate4_ref[rows, :] = jnp.concatenate([w_sel, pad], axis=0).T
    return jnp.sum(jnp.where(rowi == g_idx, 1.0, 0.0), axis=1, keepdims=True)


def _merge(l, xs, attn_ctx, attn_lat, sgu_n, cbz, conv_w, on, w_out, mod, g2, wr_cat, wr_hi, rb):
    lay = lambda *rest: (lambda i: (l,) + rest)
    rows8 = TM // SUBLANES
    return pl.pallas_call(
        functools.partial(_merge_kernel, split_x=len(xs) == 2),
        grid=(N_TILES,),
        in_specs=_x_specs(xs) + [
            pl.BlockSpec((TM, D_ATTN), lambda i: (_ctx_tile(i), 0)),
            pl.BlockSpec((TM, D_ATTN), lambda i: (_lat_tile(i), 0)),
            pl.BlockSpec((TM, D_SGU), lambda i: (i, 0)),
            pl.BlockSpec((TM, 2 * D_CONV), lambda i: (i, 0)),
            pl.BlockSpec((SUBLANES, 2 * D_CONV), lambda i: (jnp.maximum(i * rows8 - 1, 0), 0)),
            pl.BlockSpec((SUBLANES, 2 * D_CONV),
                         lambda i: (jnp.minimum((i + 1) * rows8, T_ALL // SUBLANES - 1), 0)),
            pl.BlockSpec((None, 3, D_CONV), lay(0, 0)),
            pl.BlockSpec((None, 1, D_MIX), lay(0, 0)),
            pl.BlockSpec((None, D_MIX, D_MODEL), lay(0, 0)),
            _MOD_SPEC(l, 1),
            pl.BlockSpec((None, 1, D_MODEL), lay(0, 0)),
            pl.BlockSpec((None, D_MODEL, 2 * ROUTER_COLS), lay(0, 0)),
            pl.BlockSpec((None, D_MODEL, ROUTER_COLS), lay(0, 0)),
            pl.BlockSpec((None, 1, ROUTER_COLS), lay(0, 0)),
        ],
        out_specs=[
            pl.BlockSpec((TM, D_MODEL), lambda i: (i, 0)),
            pl.BlockSpec((TM, D_MODEL), lambda i: (i, 0)),
            pl.BlockSpec((None, SUBLANES, TM), lambda i: (i, 0, 0)),
            pl.BlockSpec((TM, LANES), lambda i: (i, 0)),
            pl.BlockSpec((None, SUBLANES, LANES), lambda i: (i, 0, 0)),
        ],
        out_shape=[
            jax.ShapeDtypeStruct((T_ALL, D_MODEL), F32),
            jax.ShapeDtypeStruct((T_ALL, D_MODEL), BF16),
            jax.ShapeDtypeStruct((N_TILES, SUBLANES, TM), F32),
            jax.ShapeDtypeStruct((T_ALL, LANES), F32),
            jax.ShapeDtypeStruct((N_TILES, SUBLANES, LANES), jnp.int32),
        ],
        compiler_params=_params(1),
        name=f"merge_l{l}",
    )(*xs, attn_ctx, attn_lat, sgu_n, cbz, cbz, cbz, conv_w, on, w_out, mod, g2, wr_cat, wr_hi, rb)


RB = 144
SORT_ROWS = TM + LANES
SORT_ROWS_PAD = -(-(SORT_ROWS + RB) // BF16_ROWS) * BF16_ROWS
MOE_VMEM_LIMIT = 56 * 1024 * 1024


def _split3(x):
    hi = x.astype(BF16)
    r1 = x - hi.astype(F32)
    mid = r1.astype(BF16)
    lo = (r1 - mid.astype(F32)).astype(BF16)
    return hi, mid, lo


def _moe_kernel(cnt_ref, h_ref, route_ref, gate4_ref, tri_ref, w1_ref, w3_ref, w2_ref, x1_ref, mod_ref, fn_ref,
                *refs, final):
    out_refs, (xs_ref, gs_ref, zs_ref) = refs[:-3], refs[-3:]
    i = pl.program_id(0)
    gate2 = _mod_part(mod_ref, 5, _mod_row_of_tile(i))
    counts = [cnt_ref[i, g] for g in range(N_EXPERT_GROUPS)]
    starts = [jnp.int32(0)]
    for g in range(N_EXPERT_GROUPS - 1):
        starts.append(starts[-1] + (counts[g] + (BF16_ROWS - 1)) // BF16_ROWS * BF16_ROWS)

    g_idx = route_ref[0:1, :]
    rowi = lax.broadcasted_iota(jnp.int32, (SUBLANES, TM), 0).astype(F32)
    onehot = jnp.where(rowi == g_idx, 1.0, 0.0)
    incl = jnp.dot(onehot.astype(BF16), tri_ref[...], preferred_element_type=F32)
    pos = jnp.sum(onehot * incl, axis=0, keepdims=True) - 1.0
    for g in range(1, N_EXPERT_GROUPS):
        pos = pos + jnp.where(g_idx == float(g), starts[g].astype(F32), 0.0)
    sub = lax.broadcasted_iota(jnp.int32, (SORT_ROWS, TM), 0).astype(F32)
    perm = jnp.where(sub == pos, 1.0, 0.0).astype(BF16)
    pos_col = jnp.broadcast_to(pos, (LANES, TM)).T
    lane = lax.broadcasted_iota(jnp.int32, (TM, SORT_ROWS), 1).astype(F32)
    perm_t = jnp.where(lane == jnp.concatenate([pos_col] * (SORT_ROWS // LANES), axis=1), 1.0, 0.0).astype(BF16)

    xs_ref[:SORT_ROWS, :] = jnp.dot(perm, h_ref[...], preferred_element_type=F32).astype(BF16)
    xs_ref[SORT_ROWS:, :] = jnp.zeros((SORT_ROWS_PAD - SORT_ROWS, D_MODEL), BF16)
    hi, mid, lo = _split3(gate4_ref[...])
    packed = (hi.astype(F32) + pltpu.roll(mid.astype(F32), EXPERTS_PER_GROUP, 1)
              + pltpu.roll(lo.astype(F32), 2 * EXPERTS_PER_GROUP, 1)).astype(BF16)
    gsp = jnp.dot(perm, packed, preferred_element_type=F32)
    gs_ref[:SORT_ROWS, :] = (gsp + pltpu.roll(gsp, LANES - EXPERTS_PER_GROUP, 1)
                             + pltpu.roll(gsp, LANES - 2 * EXPERTS_PER_GROUP, 1))
    gs_ref[SORT_ROWS:, :] = jnp.zeros((SORT_ROWS_PAD - SORT_ROWS, LANES), F32)
    zs_ref[...] = jnp.zeros((SORT_ROWS_PAD, D_MODEL), BF16)

    def block_ffn(g, blk):
        row0 = pl.multiple_of(starts[g] + blk * RB, BF16_ROWS)
        xb = xs_ref[pl.ds(row0, RB), :]
        gb = gs_ref[pl.ds(row0, RB), :]
        cols = []
        for e in range(EXPERTS_PER_GROUP):
            a = jnp.dot(xb, w1_ref[g * EXPERTS_PER_GROUP + e], preferred_element_type=F32)
            b = jnp.dot(xb, w3_ref[g * EXPERTS_PER_GROUP + e], preferred_element_type=F32)
            cols.append((a / (1.0 + jnp.exp(-a)) * b * gb[:, e:e + 1]).astype(BF16))
        z = jnp.dot(jnp.concatenate(cols, axis=1), w2_ref[g], preferred_element_type=F32)
        zs_ref[pl.ds(row0, RB), :] = z.astype(BF16)

    for g in range(N_EXPERT_GROUPS):
        def extra_block(blk, carry, g=g):
            block_ffn(g, blk)
            return carry

        lax.fori_loop(1, (counts[g] + (RB - 1)) // RB, extra_block, 0)
    for g in range(N_EXPERT_GROUPS):
        block_ffn(g, 0)

    y = jnp.dot(perm_t, zs_ref[:SORT_ROWS, :], preferred_element_type=F32)
    out = x1_ref[...] + gate2 * y

    if final:
        octx_ref, olat_ref = out_refs
        out = _rms(out, fn_ref[...])
        olat_ref[...] = out

        @pl.when(i < N_CTX_TILES)
        def _():
            octx_ref[...] = out
    else:
        out_refs[0][...] = out


def _moe(l, counts, h2, route, gate4, tri, w1, w3, w2, x1, mod, final_norm, final):
    tile = pl.BlockSpec((TM, D_MODEL), lambda i, c: (i, 0))
    resident = pl.Buffered(1)
    if final:
        out_specs = [pl.BlockSpec((TM, D_MODEL), lambda i, c: (_ctx_tile(i), 0)),
                     pl.BlockSpec((TM, D_MODEL), lambda i, c: (_lat_tile(i), 0))]
        out_shape = [jax.ShapeDtypeStruct((T_CTX, D_MODEL), F32), jax.ShapeDtypeStruct((T_LAT, D_MODEL), F32)]
    else:
        out_specs = tile
        out_shape = jax.ShapeDtypeStruct((T_ALL, D_MODEL), F32)
    grid_spec = pltpu.PrefetchScalarGridSpec(
        num_scalar_prefetch=1,
        grid=(N_TILES,),
        in_specs=[
            tile,
            pl.BlockSpec((None, SUBLANES, TM), lambda i, c: (i, 0, 0)),
            pl.BlockSpec((TM, LANES), lambda i, c: (i, 0)),
            pl.BlockSpec((TM, TM), lambda i, c: (0, 0), pipeline_mode=resident),
            pl.BlockSpec((N_EXPERTS, D_MODEL, D_EXPERT), lambda i, c: (0, 0, 0), pipeline_mode=resident),
            pl.BlockSpec((N_EXPERTS, D_MODEL, D_EXPERT), lambda i, c: (0, 0, 0), pipeline_mode=resident),
            pl.BlockSpec((N_EXPERT_GROUPS, D_GROUP_HID, D_MODEL), lambda i, c: (0, 0, 0), pipeline_mode=resident),
            tile,
            pl.BlockSpec((None, N_MOD_PARTS, N_MOD_ROWS, D_MODEL), lambda i, c: (l, 0, 0, 0)),
            pl.BlockSpec((1, D_MODEL), lambda i, c: (0, 0)),
        ],
        out_specs=out_specs,
        scratch_shapes=[pltpu.VMEM((SORT_ROWS_PAD, D_MODEL), BF16),
                        pltpu.VMEM((SORT_ROWS_PAD, LANES), F32),
                        pltpu.VMEM((SORT_ROWS_PAD, D_MODEL), BF16)],
    )
    return pl.pallas_call(
        functools.partial(_moe_kernel, final=final),
        grid_spec=grid_spec,
        out_shape=out_shape,
        compiler_params=pltpu.CompilerParams(dimension_semantics=("arbitrary",),
                                             vmem_limit_bytes=MOE_VMEM_LIMIT),
        name=f"moe_l{l}",
    )(counts, h2, route, gate4, tri, w1, w3, w2, x1, mod, final_norm)


def _rope_tables():
    rows = DEC_SEQ // GRID_W
    row = np.repeat(np.arange(rows, dtype=np.float32), GRID_W)
    col = np.tile(np.arange(GRID_W, dtype=np.float32), rows)
    half = ROPE_AXIS_DIM // 2
    inv_freq = (1.0 / (np.float32(ROPE_THETA) ** (np.arange(half, dtype=np.float32) * np.float32(2.0)
                                                  / np.float32(ROPE_AXIS_DIM)))).astype(np.float32)
    ar = row[:, None] * inv_freq
    ac = col[:, None] * inv_freq
    cos64 = np.concatenate([np.cos(ar), np.cos(ar), np.cos(ac), np.cos(ac)], axis=1)
    sin64 = np.concatenate([-np.sin(ar), np.sin(ar), -np.sin(ac), np.sin(ac)], axis=1)
    cos_t = np.concatenate([np.ones((TM, LANES), np.float32), np.tile(cos64, (1, 2))], axis=0)
    sin_t = np.concatenate([np.zeros((TM, LANES), np.float32), np.tile(sin64, (1, 2))], axis=0)
    return jnp.asarray(cos_t, F32), jnp.asarray(sin_t, F32)


def _router_weights(router_g_w, router_g_b, router_e_w, router_e_b):
    zw = jnp.zeros((DEPTH, D_MODEL, SUBLANES - N_EXPERT_GROUPS), F32)
    zb = jnp.zeros((DEPTH, SUBLANES - N_EXPERT_GROUPS), F32)
    w_cols, b_cols = [router_g_w, zw], [router_g_b, zb]
    for g in range(N_EXPERT_GROUPS):
        w_cols += [router_e_w[:, g], zw]
        b_cols += [router_e_b[:, g], zb]
    used = (1 + N_EXPERT_GROUPS) * SUBLANES
    w_cols.append(jnp.zeros((DEPTH, D_MODEL, ROUTER_COLS - used), F32))
    b_cols.append(jnp.zeros((DEPTH, ROUTER_COLS - used), F32))
    w = jnp.concatenate(w_cols, axis=2)
    b = jnp.concatenate(b_cols, axis=1).reshape(DEPTH, 1, ROUTER_COLS)
    w_hi = w.astype(BF16)
    w_lo = (w - w_hi.astype(F32)).astype(BF16)
    return jnp.concatenate([w_hi, w_lo], axis=2), w_hi, b


def kernel(x_prompt, x_sample, cache_k, cache_v, c, c_ctx, w_ada, b_ada, norm1, w_in, q_norm, k_norm,
           sgu_w, sgu_b, conv_w, out_norm, w_out, norm2, router_g_w, router_g_b, router_e_w,
           router_e_b, moe_w1, moe_w3, moe_w2, final_norm):
    xs = (x_prompt.reshape(T_CTX, D_MODEL), x_sample.reshape(T_LAT, D_MODEL))

    cvecs = jnp.concatenate([c_ctx[None, :], c, jnp.zeros((N_MOD_ROWS - 1 - DEC_BATCH, D_MODEL), F32)], axis=0)
    mod = _modulation(cvecs.T, w_ada, b_ada)

    w_in_b = w_in.astype(BF16)
    w_out_b = w_out.astype(BF16)
    g1 = norm1.reshape(DEPTH, 1, D_MODEL)
    g2 = norm2.reshape(DEPTH, 1, D_MODEL)
    on = out_norm.reshape(DEPTH, 1, D_MIX)
    qg = jnp.tile(q_norm, (1, N_HEADS)).reshape(DEPTH, 1, D_ATTN)
    kg = jnp.tile(k_norm, (1, N_KV_HEADS)).reshape(DEPTH, 1, D_KV)
    cos_t, sin_t = _rope_tables()
    bq = jnp.asarray(np.kron(np.eye(N_HEADS), np.full((HEAD_DIM, HEAD_DIM), 1.0 / HEAD_DIM)), BF16)
    sgu_w_pairs = sgu_w.astype(BF16).reshape(DEPTH, N_SGU_GROUPS // 2, 2, CHUNK, CHUNK)
    sgu_w_pairs = jnp.concatenate([sgu_w_pairs[:, :, 0], sgu_w_pairs[:, :, 1]], axis=-1)
    sgu_bias = jnp.repeat(jnp.swapaxes(sgu_b, 1, 2), SGU_GROUP_DIM, axis=2)
    wr_cat, wr_hi, rb = _router_weights(router_g_w, router_g_b, router_e_w, router_e_b)
    fn = final_norm.reshape(1, D_MODEL)
    tri = jnp.asarray(np.triu(np.ones((TM, TM), np.float32)), BF16)
    ck = cache_k.reshape(DEC_BATCH, DEPTH, PAST_LEN, D_KV)
    cv = cache_v.reshape(DEC_BATCH, DEPTH, PAST_LEN, D_KV)

    ctx_ks, ctx_vs = [], []
    for l in range(DEPTH):
        qt, k, vt, k32, v32, sgu_n, cbz = _proj(l, xs, mod, g1, w_in_b, qg, kg, cos_t, sin_t, bq,
                                                sgu_w_pairs, sgu_bias, on)
        attn_ctx = _attn_ctx(l, qt, k, vt)
        attn_lat, w1, w3, w2 = _attn_lat(l, qt, k, vt, ck, cv, qg, kg, moe_w1, moe_w3, moe_w2)
        w2 = w2.reshape(N_EXPERT_GROUPS, D_GROUP_HID, D_MODEL)
        x1, h2, route, gate4, cnt = _merge(l, xs, attn_ctx, attn_lat, sgu_n, cbz, conv_w, on, w_out_b, mod, g2,
                                           wr_cat, wr_hi, rb)
        counts = cnt[:, :N_EXPERT_GROUPS, 0]
        out = _moe(l, counts, h2, route, gate4, tri, w1, w3, w2, x1, mod, fn, final=(l == DEPTH - 1))
        xs = (out,)
        ctx_ks.append(k32.reshape(BATCH, SEQ, N_KV_HEADS, HEAD_DIM))
        ctx_vs.append(v32.reshape(BATCH, SEQ, N_KV_HEADS, HEAD_DIM))

    y_prompt, y_sample = out
    return (y_prompt.reshape(BATCH, SEQ, D_MODEL), y_sample.reshape(DEC_BATCH, DEC_SEQ, D_MODEL),
            jnp.stack(ctx_ks, axis=1), jnp.stack(ctx_vs, axis=1))
```

```python
import functools
import math

import jax
import jax.numpy as jnp
import numpy as np
from jax import lax
from jax.experimental import pallas as pl
from jax.experimental.pallas import tpu as pltpu

D_MODEL = 1024
BATCH = 16
SEQ = 256
DEPTH = 2
DEC_BATCH = 2
DEC_SEQ = 4096
PAST_LEN = 512
GRID_W = 64
N_HEADS = 8
N_KV_HEADS = 2
HEAD_DIM = 64
ROPE_AXIS_DIM = HEAD_DIM // 2
ROPE_THETA = 10000.0
D_ATTN = N_HEADS * HEAD_DIM
D_KV = N_KV_HEADS * HEAD_DIM
CHUNK = 128
N_SGU_GROUPS = 4
SGU_GROUP_DIM = 64
D_SGU = N_SGU_GROUPS * SGU_GROUP_DIM
D_CONV = 256
D_MIX = D_ATTN + D_SGU + D_CONV
D_IN = D_ATTN + 2 * D_KV + 2 * D_SGU + 3 * D_CONV
N_EXPERT_GROUPS = 4
EXPERTS_PER_GROUP = 4
N_EXPERTS = N_EXPERT_GROUPS * EXPERTS_PER_GROUP
D_EXPERT = 256
D_GROUP_HID = EXPERTS_PER_GROUP * D_EXPERT
EPS = 1e-6

T_CTX = BATCH * SEQ
T_LAT = DEC_BATCH * DEC_SEQ
T_ALL = T_CTX + T_LAT
TM = 512
N_TILES = T_ALL // TM
N_CTX_TILES = T_CTX // TM
LAT_TILES_PER_BATCH = DEC_SEQ // TM
LANES = 128
SUBLANES = 8
BF16_ROWS = 16
N_MOD_ROWS = 8
N_MOD_PARTS = 6
ROUTER_COLS = 128
VMEM_LIMIT = 48 * 1024 * 1024

OFF_Q = 0
OFF_KV = D_ATTN
OFF_SGU = OFF_KV + 2 * D_KV
OFF_CONV = OFF_SGU + 2 * D_SGU

BF16 = jnp.bfloat16
F32 = jnp.float32
NEG_BIG = -1e30
Q_SCALE = HEAD_DIM ** -0.5 * math.log2(math.e)


def _params(n_grid_dims):
    return pltpu.CompilerParams(
        dimension_semantics=("arbitrary",) * n_grid_dims,
        vmem_limit_bytes=VMEM_LIMIT,
    )


def _mod_row_of_tile(i, tile=TM):
    n_ctx = T_CTX // tile
    return jnp.where(i < n_ctx, 0, 1 + (i - n_ctx) // (DEC_SEQ // tile))


def _mod_part(mod_ref, part, row):
    return mod_ref[part, pl.ds(row, 1), :]


def _rope_block_of_tile(i):
    return jnp.where(i < N_CTX_TILES, 0, 1 + (i - N_CTX_TILES) % LAT_TILES_PER_BATCH)


def _rms(x, gain):
    ms = jnp.mean(x * x, axis=-1, keepdims=True)
    return x * lax.rsqrt(ms + EPS) * gain


def _ctx_tile(i):
    return jnp.minimum(i, N_CTX_TILES - 1)


def _lat_tile(i):
    return jnp.maximum(i - N_CTX_TILES, 0)


MOD_PARTS_PER_STEP = 2
MOD_TN = MOD_PARTS_PER_STEP * D_MODEL


def _mod_kernel(ct_ref, w_ref, b_ref, o_ref):
    c = ct_ref[...]
    s = c / (1.0 + jnp.exp(-c))
    row = lax.broadcasted_iota(jnp.int32, (N_MOD_ROWS, D_MODEL), 0)
    for part in range(MOD_PARTS_PER_STEP):
        cols = slice(part * D_MODEL, (part + 1) * D_MODEL)
        w = w_ref[:, cols]
        out = jnp.zeros((N_MOD_ROWS, D_MODEL), F32)
        for r in range(1 + DEC_BATCH):
            acc = jnp.sum(w * s[:, r:r + 1], axis=0, keepdims=True)
            out = jnp.where(row == r, acc, out)
        o_ref[part] = out + b_ref[:, cols]


def _modulation(cvecs_t, w_ada, b_ada):
    n_col = N_MOD_PARTS * D_MODEL
    return pl.pallas_call(
        _mod_kernel,
        grid=(DEPTH, n_col // MOD_TN),
        in_specs=[
            pl.BlockSpec((D_MODEL, N_MOD_ROWS), lambda l, j: (0, 0)),
            pl.BlockSpec((None, D_MODEL, MOD_TN), lambda l, j: (l, 0, j)),
            pl.BlockSpec((None, 1, MOD_TN), lambda l, j: (l, 0, j)),
        ],
        out_specs=pl.BlockSpec((None, MOD_PARTS_PER_STEP, N_MOD_ROWS, D_MODEL), lambda l, j: (l, j, 0, 0)),
        out_shape=jax.ShapeDtypeStruct((DEPTH, N_MOD_PARTS, N_MOD_ROWS, D_MODEL), F32),
        compiler_params=_params(2),
        name="adaln_modulation",
    )(cvecs_t, w_ada, b_ada.reshape(DEPTH, 1, n_col))


def _mod_spec(l):
    return pl.BlockSpec((None, N_MOD_PARTS, N_MOD_ROWS, D_MODEL), lambda *idx: (l, 0, 0, 0))


def _swap16(x):
    lane = lax.broadcasted_iota(jnp.int32, x.shape, 1)
    up = pltpu.roll(x, 16, 1)
    down = pltpu.roll(x, LANES - 16, 1)
    return jnp.where((lane & 16) != 0, up, down)


def _rope(x, cos_t, sin_t):
    cols = []
    for j in range(x.shape[1] // LANES):
        xc = x[:, j * LANES:(j + 1) * LANES]
        cols.append(xc * cos_t + _swap16(xc) * sin_t)
    return cols[0] if len(cols) == 1 else jnp.concatenate(cols, axis=1)


def _proj_kernel(*refs, split_x):
    n_x = 2 if split_x else 1
    x_refs, refs, (scr_even, scr_odd) = refs[:n_x], refs[n_x:-2], refs[-2:]
    step = functools.partial(_proj_step, x_refs, refs, split_x)
    i = pl.program_id(0)

    @pl.when(i == 0)
    def _():
        scr_odd[...] = jnp.zeros((TM, D_IN), F32)

    last = i == N_TILES

    @pl.when(jnp.logical_and(i % 2 == 0, jnp.logical_not(last)))
    def _():
        step(scr_even, scr_odd)

    @pl.when(jnp.logical_and(i % 2 == 1, jnp.logical_not(last)))
    def _():
        step(scr_odd, scr_even)

    @pl.when(last)
    def _():
        step(None, scr_even if N_TILES % 2 == 1 else scr_odd)


def _proj_step(x_refs, refs, split_x, new_ref, proj):
    (mod_ref, g1_ref, w_ref, qg_ref, kg_ref, cos_ref, sin_ref, bq_ref, sw_ref, sb_ref, on_ref,
     qt_ref, k_ref, vt_ref, k32_ref, v32_ref, sgu_ref, cbz_ref) = refs
    i = pl.program_id(0)
    cur = _stage1_tile(i)

    is_ctx = i <= N_CTX_TILES
    cos_t = cos_ref[...]
    sin_t = sin_ref[...]

    q = proj[:, OFF_Q:OFF_Q + D_ATTN]
    q2 = (q * q).astype(BF16)
    half = D_ATTN // 2
    qms = jnp.concatenate(
        [jnp.dot(q2[:, :half], bq_ref[:half, :half], preferred_element_type=F32),
         jnp.dot(q2[:, half:], bq_ref[half:, half:], preferred_element_type=F32)], axis=1)
    qn = q * lax.rsqrt(qms + EPS) * qg_ref[...]
    qr = _rope(qn, cos_t, sin_t) * Q_SCALE
    qt_ref[...] = qr.T.astype(BF16)

    k = proj[:, OFF_KV:OFF_KV + D_KV]
    v = proj[:, OFF_KV + D_KV:OFF_KV + 2 * D_KV]
    kms = jnp.dot((k * k).astype(BF16), bq_ref[:D_KV, :D_KV], preferred_element_type=F32)
    kn = k * lax.rsqrt(kms + EPS) * kg_ref[...]
    k_ref[...] = _rope(kn, cos_t, sin_t).astype(BF16)
    vt_ref[...] = v.T.astype(BF16)

    su = proj[:, OFF_SGU:OFF_SGU + D_SGU]
    sv = proj[:, OFF_SGU + D_SGU:OFF_SGU + 2 * D_SGU]
    n_chunks = TM // CHUNK
    sv_wide = jnp.concatenate([sv[n * CHUNK:(n + 1) * CHUNK, :] for n in range(n_chunks)], axis=1)
    grp = lax.broadcasted_iota(jnp.int32, (CHUNK, n_chunks * D_SGU), 1) // SGU_GROUP_DIM % N_SGU_GROUPS
    mixed_wide = jnp.zeros((CHUNK, n_chunks * D_SGU), F32)
    for pair in range(N_SGU_GROUPS // 2):
        rhs = jnp.concatenate(
            [jnp.where(grp == 2 * pair, sv_wide, 0.0), jnp.where(grp == 2 * pair + 1, sv_wide, 0.0)],
            axis=0).astype(BF16)
        mixed_wide = mixed_wide + jnp.dot(sw_ref[pair], rhs, preferred_element_type=F32)
    on_sgu = on_ref[:, D_ATTN:D_ATTN + D_SGU]
    for n in range(n_chunks):
        sgu = su[n * CHUNK:(n + 1) * CHUNK, :] * (mixed_wide[:, n * D_SGU:(n + 1) * D_SGU] + sb_ref[...])
        sgu_ref[n * CHUNK:(n + 1) * CHUNK, :] = _rms(sgu, on_sgu).astype(BF16)

    cbz_ref[:, :D_CONV] = proj[:, OFF_CONV:OFF_CONV + D_CONV]
    cbz_ref[:, D_CONV:] = (proj[:, OFF_CONV + D_CONV:OFF_CONV + 2 * D_CONV]
                           * proj[:, OFF_CONV + 2 * D_CONV:OFF_CONV + 3 * D_CONV])

    if new_ref is not None:
        x = jnp.where(cur < N_CTX_TILES, x_refs[0][...], x_refs[1][...]) if split_x else x_refs[0][...]
        mrow = _mod_row_of_tile(cur)
        h = _rms(x, g1_ref[...] * (1.0 + _mod_part(mod_ref, 1, mrow))) + _mod_part(mod_ref, 0, mrow)
        new_ref[...] = jnp.dot(h.astype(BF16), w_ref[...], preferred_element_type=F32)

    @pl.when(is_ctx)
    def _():
        k32_ref[...] = kn
        v32_ref[...] = v


def _x_specs(xs, tile_of_step=lambda i: i):
    if len(xs) == 1:
        return [pl.BlockSpec((TM, D_MODEL), lambda i: (tile_of_step(i), 0))]
    return [pl.BlockSpec((TM, D_MODEL), lambda i: (_ctx_tile(tile_of_step(i)), 0)),
            pl.BlockSpec((TM, D_MODEL), lambda i: (_lat_tile(tile_of_step(i)), 0))]


def _stage1_tile(i):
    return jnp.minimum(i, N_TILES - 1)


def _stage2_tile(i):
    return jnp.maximum(i - 1, 0)


def _proj(l, xs, mod, g1, w_in, qg, kg, cos_t, sin_t, bq, sgu_w, sgu_b, on):
    lay = lambda *rest: (lambda i: (l,) + rest)
    s2 = _stage2_tile
    return pl.pallas_call(
        functools.partial(_proj_kernel, split_x=len(xs) == 2),
        grid=(N_TILES + 1,),
        in_specs=_x_specs(xs, _stage1_tile) + [
            _mod_spec(l),
            pl.BlockSpec((None, 1, D_MODEL), lay(0, 0)),
            pl.BlockSpec((None, D_MODEL, D_IN), lay(0, 0)),
            pl.BlockSpec((None, 1, D_ATTN), lay(0, 0)),
            pl.BlockSpec((None, 1, D_KV), lay(0, 0)),
            pl.BlockSpec((TM, LANES), lambda i: (_rope_block_of_tile(s2(i)), 0)),
            pl.BlockSpec((TM, LANES), lambda i: (_rope_block_of_tile(s2(i)), 0)),
            pl.BlockSpec((D_ATTN, D_ATTN), lambda i: (0, 0)),
            pl.BlockSpec((None, N_SGU_GROUPS // 2, CHUNK, 2 * CHUNK), lay(0, 0, 0)),
            pl.BlockSpec((None, CHUNK, D_SGU), lay(0, 0)),
            pl.BlockSpec((None, 1, D_MIX), lay(0, 0)),
        ],
        out_specs=[
            pl.BlockSpec((None, D_ATTN, TM), lambda i: (s2(i), 0, 0)),
            pl.BlockSpec((TM, D_KV), lambda i: (s2(i), 0)),
            pl.BlockSpec((None, D_KV, TM), lambda i: (s2(i), 0, 0)),
            pl.BlockSpec((TM, D_KV), lambda i: (_ctx_tile(s2(i)), 0)),
            pl.BlockSpec((TM, D_KV), lambda i: (_ctx_tile(s2(i)), 0)),
            pl.BlockSpec((TM, D_SGU), lambda i: (s2(i), 0)),
            pl.BlockSpec((TM, 2 * D_CONV), lambda i: (s2(i), 0)),
        ],
        out_shape=[
            jax.ShapeDtypeStruct((N_TILES, D_ATTN, TM), BF16),
            jax.ShapeDtypeStruct((T_ALL, D_KV), BF16),
            jax.ShapeDtypeStruct((N_TILES, D_KV, TM), BF16),
            jax.ShapeDtypeStruct((T_CTX, D_KV), F32),
            jax.ShapeDtypeStruct((T_CTX, D_KV), F32),
            jax.ShapeDtypeStruct((T_ALL, D_SGU), BF16),
            jax.ShapeDtypeStruct((T_ALL, 2 * D_CONV), F32),
        ],
        scratch_shapes=[pltpu.VMEM((TM, D_IN), F32), pltpu.VMEM((TM, D_IN), F32)],
        compiler_params=_params(1),
        name=f"proj_l{l}",
    )(*xs, mod, g1, w_in, qg, kg, cos_t, sin_t, bq, sgu_w, sgu_b, on)


ACC_ROWS = HEAD_DIM + BF16_ROWS
N_PAIRS = N_HEADS // 2
PAIRS_PER_KV = N_PAIRS // N_KV_HEADS


def _pair_queries(qt, kv_idx, tq):
    qf = qt.astype(F32)
    mine = lax.broadcasted_iota(jnp.int32, (2 * HEAD_DIM, tq), 0) // HEAD_DIM == kv_idx
    qe = qf[:HEAD_DIM]
    qo = qf[HEAD_DIM:]
    rhs = jnp.concatenate(
        [jnp.where(mine, jnp.concatenate([qe, qe], axis=0), 0.0),
         jnp.where(mine, jnp.concatenate([qo, qo], axis=0), 0.0)], axis=1)
    return rhs.astype(BF16)


def _with_ones(vt_c):
    return jnp.concatenate([vt_c, jnp.ones((BF16_ROWS, vt_c.shape[1]), BF16)], axis=0)


def _flash_pairs(rhs_list, chunk_lists, tq, shift=None):
    n_pairs, n_chunks = len(rhs_list), len(chunk_lists[0])
    scores = lambda s, c: jnp.dot(chunk_lists[s][c][0](), rhs_list[s], preferred_element_type=F32)
    m = [jnp.full((1, 2 * tq), NEG_BIG, F32)] * n_pairs
    acc = [jnp.zeros((ACC_ROWS, 2 * tq), F32)] * n_pairs
    st = [scores(s, 0) for s in range(n_pairs)]
    for c in range(n_chunks):
        for s in range(n_pairs):
            if shift is None:
                m_new = jnp.maximum(m[s], jnp.max(st[s], axis=0, keepdims=True))
            st_next = scores(s, c + 1) if c + 1 < n_chunks else None
            if shift is None:
                alpha = jnp.exp2(m[s] - m_new)
                p = jnp.exp2(st[s] - m_new).astype(BF16)
                acc[s] = alpha * acc[s] + jnp.dot(chunk_lists[s][c][1](), p, preferred_element_type=F32)
                m[s] = m_new
            else:
                p = jnp.exp2(st[s] - shift).astype(BF16)
                acc[s] = acc[s] + jnp.dot(chunk_lists[s][c][1](), p, preferred_element_type=F32)
            st[s] = st_next
    outs = []
    for s in range(n_pairs):
        o = acc[s][:HEAD_DIM] * (1.0 / acc[s][HEAD_DIM:HEAD_DIM + 1])
        outs.append(jnp.concatenate([o[:, :tq], o[:, tq:]], axis=0))
    return outs


def _attn_ctx_kernel(qt_ref, k_ref, vt_ref, o_ref):
    tq = SEQ
    rhs_list, chunk_lists = [], []
    for pair in range(N_PAIRS):
        kv_idx = pair // PAIRS_PER_KV
        rhs_list.append(_pair_queries(qt_ref[pair * LANES:(pair + 1) * LANES, :], kv_idx, tq))
        chunk_lists.append([(lambda: k_ref[...],
                             lambda kv_idx=kv_idx: _with_ones(vt_ref[kv_idx * HEAD_DIM:(kv_idx + 1) * HEAD_DIM, :]))])
    for pair, o_t in enumerate(_flash_pairs(rhs_list, chunk_lists, tq)):
        o_ref[:, pair * LANES:(pair + 1) * LANES] = o_t.T.astype(BF16)


def _attn_ctx(l, qt, k, vt):
    halves = TM // SEQ
    return pl.pallas_call(
        _attn_ctx_kernel,
        grid=(BATCH,),
        in_specs=[
            pl.BlockSpec((None, D_ATTN, SEQ), lambda b: (b // halves, 0, b % halves)),
            pl.BlockSpec((SEQ, D_KV), lambda b: (b, 0)),
            pl.BlockSpec((None, D_KV, SEQ), lambda b: (b // halves, 0, b % halves)),
        ],
        out_specs=pl.BlockSpec((SEQ, D_ATTN), lambda b: (b, 0)),
        out_shape=jax.ShapeDtypeStruct((T_CTX, D_ATTN), BF16),
        compiler_params=_params(1),
        name=f"attn_ctx_l{l}",
    )(qt, k, vt)


KC = 512
LAT_PAIRS_PER_STEP = 2


MAX_SHIFT = 60.0
BOUND_SLACK = 1.05


def _score_bound_sq(qg_ref, kg_ref, ck):
    qg = qg_ref[...]
    kg = kg_ref[...]
    gq2 = jnp.max(qg * qg, axis=1, keepdims=True)
    k2 = HEAD_DIM * jnp.max(kg * kg, axis=1, keepdims=True)
    if ck is not None:
        sq = ck * ck
        first = lax.broadcasted_iota(jnp.int32, sq.shape, 1) < HEAD_DIM
        n0 = jnp.sum(jnp.where(first, sq, 0.0), axis=1, keepdims=True)
        n1 = jnp.sum(jnp.where(first, 0.0, sq), axis=1, keepdims=True)
        k2 = jnp.maximum(k2, jnp.max(jnp.maximum(n0, n1), axis=0, keepdims=True))
    return (BOUND_SLACK * Q_SCALE) ** 2 * HEAD_DIM * gq2 * k2


def _run_with_score_bound(run, bound):
    small = bound <= MAX_SHIFT

    @pl.when(small)
    def _():
        run(bound)

    @pl.when(jnp.logical_not(small))
    def _():
        run(None)


def _attn_lat_kernel(qt_ref, k_ref, vt_ref, ck_ref, cv_ref, qg_ref, kg_ref, w1_ref, w3_ref, w2_ref,
                     o_ref, w1b_ref, w3b_ref, w2b_ref, bound_ref):
    tq = TM

    def run(shift):
        w1b_ref[...] = w1_ref[...].astype(BF16)
        w3b_ref[...] = w3_ref[...].astype(BF16)
        w2b_ref[...] = w2_ref[...].astype(BF16)
        cvt = cv_ref[...].T
        rhs_list, chunk_lists = [], []
        for j in range(LAT_PAIRS_PER_STEP):
            kv_idx = (pl.program_id(2) * LAT_PAIRS_PER_STEP + j) // PAIRS_PER_KV
            rhs_list.append(_pair_queries(qt_ref[j * LANES:(j + 1) * LANES, :], kv_idx, tq))
            v_row = pl.multiple_of(kv_idx * HEAD_DIM, HEAD_DIM)

            def latent_chunk(c, v_row=v_row):
                tile, lane0 = (c * KC) // TM, (c * KC) % TM
                return (lambda: k_ref[c * KC:(c + 1) * KC, :],
                        lambda: _with_ones(vt_ref[tile, pl.ds(v_row, HEAD_DIM), lane0:lane0 + KC]))

            def cached_chunk(c, kv_idx=kv_idx):
                return (lambda: ck_ref[c * KC:(c + 1) * KC, :].astype(BF16),
                        lambda: _with_ones(jnp.where(kv_idx == 0, cvt[:HEAD_DIM, c * KC:(c + 1) * KC],
                                                     cvt[HEAD_DIM:, c * KC:(c + 1) * KC]).astype(BF16)))

            chunks = [latent_chunk(c) for c in range(DEC_SEQ // KC)]
            chunks += [cached_chunk(c) for c in range(PAST_LEN // KC)]
            chunk_lists.append(chunks)
        for j, o_t in enumerate(_flash_pairs(rhs_list, chunk_lists, tq, shift)):
            o_ref[:, j * LANES:(j + 1) * LANES] = o_t.T.astype(BF16)

    @pl.when(jnp.logical_and(pl.program_id(1) == 0, pl.program_id(2) == 0))
    def _():
        bound_ref[0] = jnp.sqrt(_score_bound_sq(qg_ref, kg_ref, ck_ref[...]))[0, 0]

    _run_with_score_bound(run, bound_ref[0])


def _attn_lat(l, qt, k, vt, cache_k, cache_v, qg, kg, moe_w1, moe_w3, moe_w2):
    n_q = DEC_SEQ // TM
    first = N_CTX_TILES
    ctx_blocks = T_CTX // DEC_SEQ
    width = LAT_PAIRS_PER_STEP * LANES
    n_p = N_PAIRS // LAT_PAIRS_PER_STEP
    per_expert = DEC_BATCH * n_q * n_p // N_EXPERTS
    step = lambda b, i, p: (b * n_q + i) * n_p + p
    w_in_spec = lambda rows, cols: pl.BlockSpec(
        (None, None, rows // per_expert, cols),
        lambda b, i, p: (l, step(b, i, p) // per_expert, step(b, i, p) % per_expert, 0))
    w_out_spec = lambda rows, cols: pl.BlockSpec(
        (None, rows // per_expert, cols),
        lambda b, i, p: (step(b, i, p) // per_expert, step(b, i, p) % per_expert, 0))
    return pl.pallas_call(
        _attn_lat_kernel,
        grid=(DEC_BATCH, n_q, n_p),
        in_specs=[
            pl.BlockSpec((None, width, TM), lambda b, i, p: (first + b * n_q + i, p, 0)),
            pl.BlockSpec((DEC_SEQ, D_KV), lambda b, i, p: (ctx_blocks + b, 0)),
            pl.BlockSpec((n_q, D_KV, TM), lambda b, i, p: (ctx_blocks + b, 0, 0)),
            pl.BlockSpec((None, None, PAST_LEN, D_KV), lambda b, i, p: (b, l, 0, 0)),
            pl.BlockSpec((None, None, PAST_LEN, D_KV), lambda b, i, p: (b, l, 0, 0)),
            pl.BlockSpec((None, 1, D_ATTN), lambda b, i, p: (l, 0, 0)),
            pl.BlockSpec((None, 1, D_KV), lambda b, i, p: (l, 0, 0)),
            w_in_spec(D_MODEL, D_EXPERT),
            w_in_spec(D_MODEL, D_EXPERT),
            w_in_spec(D_EXPERT, D_MODEL),
        ],
        out_specs=[
            pl.BlockSpec((TM, width), lambda b, i, p: (b * n_q + i, p)),
            w_out_spec(D_MODEL, D_EXPERT),
            w_out_spec(D_MODEL, D_EXPERT),
            w_out_spec(D_EXPERT, D_MODEL),
        ],
        out_shape=[
            jax.ShapeDtypeStruct((T_LAT, D_ATTN), BF16),
            jax.ShapeDtypeStruct((N_EXPERTS, D_MODEL, D_EXPERT), BF16),
            jax.ShapeDtypeStruct((N_EXPERTS, D_MODEL, D_EXPERT), BF16),
            jax.ShapeDtypeStruct((N_EXPERTS, D_EXPERT, D_MODEL), BF16),
        ],
        scratch_shapes=[pltpu.SMEM((1,), F32)],
        compiler_params=_params(3),
        name=f"attn_lat_l{l}",
    )(qt, k, vt, cache_k, cache_v, qg, kg, moe_w1, moe_w3, moe_w2)


def _first_max_index(vals, valid, rowf):
    masked = jnp.where(valid, vals, -jnp.inf)
    mx = jnp.max(masked, axis=0, keepdims=True)
    idx = jnp.min(jnp.where(masked == mx, rowf, float(SUBLANES)), axis=0, keepdims=True)
    return mx, idx


def _merge_kernel(*refs, split_x):
    n_x = 2 if split_x else 1
    x_refs, refs = refs[:n_x], refs[n_x:]
    (actx_ref, alat_ref, sgu_ref, cbz_ref, prev_ref, next_ref, cw_ref, on_ref,
     wo_ref, mod_ref, g2_ref, wr_ref, wrhi_ref, rb_ref, x1_ref, h2_ref, route_ref, gate4_ref, cnt_ref) = refs
    cur = pl.program_id(0)
    is_ctx = cur < N_CTX_TILES
    mrow = _mod_row_of_tile(cur)
    on = on_ref[...]

    cbz = cbz_ref[...]
    cb = cbz[:, :D_CONV]
    z = cbz[:, D_CONV:]
    seq_mask = jnp.where(is_ctx, SEQ - 1, DEC_SEQ - 1)
    row = lax.broadcasted_iota(jnp.int32, (TM, D_CONV), 0)
    pos = (cur * TM + row) & seq_mask
    z_prev = jnp.where(row == 0, prev_ref[SUBLANES - 1:SUBLANES, D_CONV:], pltpu.roll(z, 1, 0))
    z_next = jnp.where(row == TM - 1, next_ref[0:1, D_CONV:], pltpu.roll(z, TM - 1, 0))
    z_prev = jnp.where(pos == 0, 0.0, z_prev)
    z_next = jnp.where(pos == seq_mask, 0.0, z_next)
    cw = cw_ref[...]
    conv = cb * (z_prev * cw[0:1, :] + z * cw[1:2, :] + z_next * cw[2:3, :])
    conv_n = _rms(conv, on[:, D_ATTN + D_SGU:]).astype(BF16)

    attn = jnp.where(is_ctx, actx_ref[...], alat_ref[...]).astype(F32)
    attn_n = _rms(attn, on[:, :D_ATTN]).astype(BF16)
    merged = jnp.concatenate([attn_n, sgu_ref[...], conv_n], axis=1)
    x = jnp.where(is_ctx, x_refs[0][...], x_refs[1][...]) if split_x else x_refs[0][...]
    x1 = x + _mod_part(mod_ref, 2, mrow) * jnp.dot(merged, wo_ref[...], preferred_element_type=F32)
    x1_ref[...] = x1
    _route_tile(x1, mrow, mod_ref, g2_ref, wr_ref, wrhi_ref, rb_ref, h2_ref, route_ref, gate4_ref, cnt_ref)


def _route_tile(x1, mrow, mod_ref, g2_ref, wr_ref, wrhi_ref, rb_ref, h2_ref, route_ref, gate4_ref, cnt_ref):
    n = x1.shape[0]
    h2 = _rms(x1, g2_ref[...] * (1.0 + _mod_part(mod_ref, 4, mrow))) + _mod_part(mod_ref, 3, mrow)
    hb = h2.astype(BF16)
    h2_ref[...] = hb

    h_lo = (h2 - hb.astype(F32)).astype(BF16)
    a = jnp.dot(hb, wr_ref[...], preferred_element_type=F32)
    b = jnp.dot(h_lo, wrhi_ref[...], preferred_element_type=F32)
    logits = a[:, :ROUTER_COLS] + a[:, ROUTER_COLS:] + b + rb_ref[...]
    lt = logits.T

    rowi = lax.broadcasted_iota(jnp.int32, (SUBLANES, n), 0).astype(F32)
    valid = rowi < float(N_EXPERT_GROUPS)
    g_log = lt[0:SUBLANES]
    g_max, g_idx = _first_max_index(g_log, valid, rowi)
    p_g = 1.0 / jnp.sum(jnp.where(valid, jnp.exp(g_log - g_max), 0.0), axis=0, keepdims=True)
    e_log = jnp.zeros((SUBLANES, n), F32)
    for g in range(N_EXPERT_GROUPS):
        e_log = jnp.where(g_idx == g, lt[(g + 1) * SUBLANES:(g + 2) * SUBLANES], e_log)
    e_max, i1 = _first_max_index(e_log, valid, rowi)
    e_exp = jnp.where(valid, jnp.exp(e_log - e_max), 0.0)
    e_prob = e_exp / jnp.sum(e_exp, axis=0, keepdims=True)
    v1 = jnp.max(e_prob, axis=0, keepdims=True)
    rest = jnp.logical_and(valid, rowi != i1)
    v2, i2 = _first_max_index(e_prob, rest, rowi)
    denom = v1 + v2
    w_sel = p_g * jnp.where(rowi == i1, v1 / denom, jnp.where(rowi == i2, v2 / denom, 0.0))
    route_ref[...] = jnp.broadcast_to(g_idx, (SUBLANES, n))
    pad = jnp.zeros((LANES - SUBLANES, n), F32)
    gate4_ref[...] = jnp.concatenate([w_sel, pad], axis=0).T
    counts = jnp.sum(jnp.where(rowi == g_idx, 1.0, 0.0), axis=1, keepdims=True)
    cnt_ref[...] = jnp.broadcast_to(counts, (SUBLANES, LANES)).astype(jnp.int32)


def _merge(l, xs, attn_ctx, attn_lat, sgu_n, cbz, conv_w, on, w_out, mod, g2, wr_cat, wr_hi, rb):
    lay = lambda *rest: (lambda i: (l,) + rest)
    rows8 = TM // SUBLANES
    return pl.pallas_call(
        functools.partial(_merge_kernel, split_x=len(xs) == 2),
        grid=(N_TILES,),
        in_specs=_x_specs(xs) + [
            pl.BlockSpec((TM, D_ATTN), lambda i: (_ctx_tile(i), 0)),
            pl.BlockSpec((TM, D_ATTN), lambda i: (_lat_tile(i), 0)),
            pl.BlockSpec((TM, D_SGU), lambda i: (i, 0)),
            pl.BlockSpec((TM, 2 * D_CONV), lambda i: (i, 0)),
            pl.BlockSpec((SUBLANES, 2 * D_CONV), lambda i: (jnp.maximum(i * rows8 - 1, 0), 0)),
            pl.BlockSpec((SUBLANES, 2 * D_CONV),
                         lambda i: (jnp.minimum((i + 1) * rows8, T_ALL // SUBLANES - 1), 0)),
            pl.BlockSpec((None, 3, D_CONV), lay(0, 0)),
            pl.BlockSpec((None, 1, D_MIX), lay(0, 0)),
            pl.BlockSpec((None, D_MIX, D_MODEL), lay(0, 0)),
            _mod_spec(l),
            pl.BlockSpec((None, 1, D_MODEL), lay(0, 0)),
            pl.BlockSpec((None, D_MODEL, 2 * ROUTER_COLS), lay(0, 0)),
            pl.BlockSpec((None, D_MODEL, ROUTER_COLS), lay(0, 0)),
            pl.BlockSpec((None, 1, ROUTER_COLS), lay(0, 0)),
        ],
        out_specs=[
            pl.BlockSpec((TM, D_MODEL), lambda i: (i, 0)),
            pl.BlockSpec((TM, D_MODEL), lambda i: (i, 0)),
            pl.BlockSpec((None, SUBLANES, TM), lambda i: (i, 0, 0)),
            pl.BlockSpec((TM, LANES), lambda i: (i, 0)),
            pl.BlockSpec((None, SUBLANES, LANES), lambda i: (i, 0, 0)),
        ],
        out_shape=[
            jax.ShapeDtypeStruct((T_ALL, D_MODEL), F32),
            jax.ShapeDtypeStruct((T_ALL, D_MODEL), BF16),
            jax.ShapeDtypeStruct((N_TILES, SUBLANES, TM), F32),
            jax.ShapeDtypeStruct((T_ALL, LANES), F32),
            jax.ShapeDtypeStruct((N_TILES, SUBLANES, LANES), jnp.int32),
        ],
        compiler_params=_params(1),
        name=f"merge_l{l}",
    )(*xs, attn_ctx, attn_lat, sgu_n, cbz, cbz, cbz, conv_w, on, w_out, mod, g2, wr_cat, wr_hi, rb)


RB = 144
SORT_ROWS = TM + LANES
SORT_ROWS_PAD = -(-(SORT_ROWS + RB) // BF16_ROWS) * BF16_ROWS
MOE_VMEM_LIMIT = 56 * 1024 * 1024


def _split3(x):
    hi = x.astype(BF16)
    r1 = x - hi.astype(F32)
    mid = r1.astype(BF16)
    lo = (r1 - mid.astype(F32)).astype(BF16)
    return hi, mid, lo


def _moe_kernel(cnt_ref, h_ref, route_ref, gate4_ref, tri_ref, w1_ref, w3_ref, w2_ref, x1_ref, mod_ref, fn_ref,
                *refs, final):
    out_refs, (xs_ref, gs_ref, zs_ref) = refs[:-3], refs[-3:]
    i = pl.program_id(0)
    gate2 = _mod_part(mod_ref, 5, _mod_row_of_tile(i))
    counts = [cnt_ref[i, g] for g in range(N_EXPERT_GROUPS)]
    starts = [jnp.int32(0)]
    for g in range(N_EXPERT_GROUPS - 1):
        starts.append(starts[-1] + (counts[g] + (BF16_ROWS - 1)) // BF16_ROWS * BF16_ROWS)

    g_idx = route_ref[0:1, :]
    rowi = lax.broadcasted_iota(jnp.int32, (SUBLANES, TM), 0).astype(F32)
    onehot = jnp.where(rowi == g_idx, 1.0, 0.0)
    incl = jnp.dot(onehot.astype(BF16), tri_ref[...], preferred_element_type=F32)
    pos = jnp.sum(onehot * incl, axis=0, keepdims=True) - 1.0
    for g in range(1, N_EXPERT_GROUPS):
        pos = pos + jnp.where(g_idx == float(g), starts[g].astype(F32), 0.0)
    sub = lax.broadcasted_iota(jnp.int32, (SORT_ROWS, TM), 0).astype(F32)
    perm = jnp.where(sub == pos, 1.0, 0.0).astype(BF16)
    pos_col = jnp.broadcast_to(pos, (LANES, TM)).T
    lane = lax.broadcasted_iota(jnp.int32, (TM, SORT_ROWS), 1).astype(F32)
    perm_t = jnp.where(lane == jnp.concatenate([pos_col] * (SORT_ROWS // LANES), axis=1), 1.0, 0.0).astype(BF16)

    xs_ref[:SORT_ROWS, :] = jnp.dot(perm, h_ref[...], preferred_element_type=F32).astype(BF16)
    xs_ref[SORT_ROWS:, :] = jnp.zeros((SORT_ROWS_PAD - SORT_ROWS, D_MODEL), BF16)
    hi, mid, lo = _split3(gate4_ref[...])
    packed = (hi.astype(F32) + pltpu.roll(mid.astype(F32), EXPERTS_PER_GROUP, 1)
              + pltpu.roll(lo.astype(F32), 2 * EXPERTS_PER_GROUP, 1)).astype(BF16)
    gsp = jnp.dot(perm, packed, preferred_element_type=F32)
    gs_ref[:SORT_ROWS, :] = (gsp + pltpu.roll(gsp, LANES - EXPERTS_PER_GROUP, 1)
                             + pltpu.roll(gsp, LANES - 2 * EXPERTS_PER_GROUP, 1))
    gs_ref[SORT_ROWS:, :] = jnp.zeros((SORT_ROWS_PAD - SORT_ROWS, LANES), F32)
    zs_ref[...] = jnp.zeros((SORT_ROWS_PAD, D_MODEL), BF16)

    def block_ffn(g, blk):
        row0 = pl.multiple_of(starts[g] + blk * RB, BF16_ROWS)
        xb = xs_ref[pl.ds(row0, RB), :]
        gb = gs_ref[pl.ds(row0, RB), :]
        cols = []
        for e in range(EXPERTS_PER_GROUP):
            a = jnp.dot(xb, w1_ref[g * EXPERTS_PER_GROUP + e], preferred_element_type=F32)
            b = jnp.dot(xb, w3_ref[g * EXPERTS_PER_GROUP + e], preferred_element_type=F32)
            cols.append((a / (1.0 + jnp.exp(-a)) * b * gb[:, e:e + 1]).astype(BF16))
        z = jnp.dot(jnp.concatenate(cols, axis=1), w2_ref[g], preferred_element_type=F32)
        zs_ref[pl.ds(row0, RB), :] = z.astype(BF16)

    for g in range(N_EXPERT_GROUPS):
        def extra_block(blk, carry, g=g):
            block_ffn(g, blk)
            return carry

        lax.fori_loop(1, (counts[g] + (RB - 1)) // RB, extra_block, 0)
    for g in range(N_EXPERT_GROUPS):
        block_ffn(g, 0)

    y = jnp.dot(perm_t, zs_ref[:SORT_ROWS, :], preferred_element_type=F32)
    out = x1_ref[...] + gate2 * y

    if final:
        octx_ref, olat_ref = out_refs
        out = _rms(out, fn_ref[...])
        olat_ref[...] = out

        @pl.when(i < N_CTX_TILES)
        def _():
            octx_ref[...] = out
    else:
        out_refs[0][...] = out


def _moe(l, counts, h2, route, gate4, tri, w1, w3, w2, x1, mod, final_norm, final):
    tile = pl.BlockSpec((TM, D_MODEL), lambda i, c: (i, 0))
    resident = pl.Buffered(1)
    if final:
        out_specs = [pl.BlockSpec((TM, D_MODEL), lambda i, c: (_ctx_tile(i), 0)),
                     pl.BlockSpec((TM, D_MODEL), lambda i, c: (_lat_tile(i), 0))]
        out_shape = [jax.ShapeDtypeStruct((T_CTX, D_MODEL), F32), jax.ShapeDtypeStruct((T_LAT, D_MODEL), F32)]
    else:
        out_specs = tile
        out_shape = jax.ShapeDtypeStruct((T_ALL, D_MODEL), F32)
    grid_spec = pltpu.PrefetchScalarGridSpec(
        num_scalar_prefetch=1,
        grid=(N_TILES,),
        in_specs=[
            tile,
            pl.BlockSpec((None, SUBLANES, TM), lambda i, c: (i, 0, 0)),
            pl.BlockSpec((TM, LANES), lambda i, c: (i, 0)),
            pl.BlockSpec((TM, TM), lambda i, c: (0, 0), pipeline_mode=resident),
            pl.BlockSpec((N_EXPERTS, D_MODEL, D_EXPERT), lambda i, c: (0, 0, 0), pipeline_mode=resident),
            pl.BlockSpec((N_EXPERTS, D_MODEL, D_EXPERT), lambda i, c: (0, 0, 0), pipeline_mode=resident),
            pl.BlockSpec((N_EXPERT_GROUPS, D_GROUP_HID, D_MODEL), lambda i, c: (0, 0, 0), pipeline_mode=resident),
            tile,
            pl.BlockSpec((None, N_MOD_PARTS, N_MOD_ROWS, D_MODEL), lambda i, c: (l, 0, 0, 0)),
            pl.BlockSpec((1, D_MODEL), lambda i, c: (0, 0)),
        ],
        out_specs=out_specs,
        scratch_shapes=[pltpu.VMEM((SORT_ROWS_PAD, D_MODEL), BF16),
                        pltpu.VMEM((SORT_ROWS_PAD, LANES), F32),
                        pltpu.VMEM((SORT_ROWS_PAD, D_MODEL), BF16)],
    )
    return pl.pallas_call(
        functools.partial(_moe_kernel, final=final),
        grid_spec=grid_spec,
        out_shape=out_shape,
        compiler_params=pltpu.CompilerParams(dimension_semantics=("arbitrary",),
                                             vmem_limit_bytes=MOE_VMEM_LIMIT),
        name=f"moe_l{l}",
    )(counts, h2, route, gate4, tri, w1, w3, w2, x1, mod, final_norm)


def _rope_tables():
    rows = DEC_SEQ // GRID_W
    row = np.repeat(np.arange(rows, dtype=np.float32), GRID_W)
    col = np.tile(np.arange(GRID_W, dtype=np.float32), rows)
    half = ROPE_AXIS_DIM // 2
    inv_freq = (1.0 / (np.float32(ROPE_THETA) ** (np.arange(half, dtype=np.float32) * np.float32(2.0)
                                                  / np.float32(ROPE_AXIS_DIM)))).astype(np.float32)
    ar = row[:, None] * inv_freq
    ac = col[:, None] * inv_freq
    cos64 = np.concatenate([np.cos(ar), np.cos(ar), np.cos(ac), np.cos(ac)], axis=1)
    sin64 = np.concatenate([-np.sin(ar), np.sin(ar), -np.sin(ac), np.sin(ac)], axis=1)
    cos_t = np.concatenate([np.ones((TM, LANES), np.float32), np.tile(cos64, (1, 2))], axis=0)
    sin_t = np.concatenate([np.zeros((TM, LANES), np.float32), np.tile(sin64, (1, 2))], axis=0)
    return jnp.asarray(cos_t, F32), jnp.asarray(sin_t, F32)


def _router_weights(router_g_w, router_g_b, router_e_w, router_e_b):
    zw = jnp.zeros((DEPTH, D_MODEL, SUBLANES - N_EXPERT_GROUPS), F32)
    zb = jnp.zeros((DEPTH, SUBLANES - N_EXPERT_GROUPS), F32)
    w_cols, b_cols = [router_g_w, zw], [router_g_b, zb]
    for g in range(N_EXPERT_GROUPS):
        w_cols += [router_e_w[:, g], zw]
        b_cols += [router_e_b[:, g], zb]
    used = (1 + N_EXPERT_GROUPS) * SUBLANES
    w_cols.append(jnp.zeros((DEPTH, D_MODEL, ROUTER_COLS - used), F32))
    b_cols.append(jnp.zeros((DEPTH, ROUTER_COLS - used), F32))
    w = jnp.concatenate(w_cols, axis=2)
    b = jnp.concatenate(b_cols, axis=1).reshape(DEPTH, 1, ROUTER_COLS)
    w_hi = w.astype(BF16)
    w_lo = (w - w_hi.astype(F32)).astype(BF16)
    return jnp.concatenate([w_hi, w_lo], axis=2), w_hi, b


def kernel(x_prompt, x_sample, cache_k, cache_v, c, c_ctx, w_ada, b_ada, norm1, w_in, q_norm, k_norm,
           sgu_w, sgu_b, conv_w, out_norm, w_out, norm2, router_g_w, router_g_b, router_e_w,
           router_e_b, moe_w1, moe_w3, moe_w2, final_norm):
    xs = (x_prompt.reshape(T_CTX, D_MODEL), x_sample.reshape(T_LAT, D_MODEL))

    cvecs = jnp.concatenate([c_ctx[None, :], c, jnp.zeros((N_MOD_ROWS - 1 - DEC_BATCH, D_MODEL), F32)], axis=0)
    mod = _modulation(cvecs.T, w_ada, b_ada)

    w_in_b = w_in.astype(BF16)
    w_out_b = w_out.astype(BF16)
    g1 = norm1.reshape(DEPTH, 1, D_MODEL)
    g2 = norm2.reshape(DEPTH, 1, D_MODEL)
    on = out_norm.reshape(DEPTH, 1, D_MIX)
    qg = jnp.tile(q_norm, (1, N_HEADS)).reshape(DEPTH, 1, D_ATTN)
    kg = jnp.tile(k_norm, (1, N_KV_HEADS)).reshape(DEPTH, 1, D_KV)
    cos_t, sin_t = _rope_tables()
    bq = jnp.asarray(np.kron(np.eye(N_HEADS), np.full((HEAD_DIM, HEAD_DIM), 1.0 / HEAD_DIM)), BF16)
    sgu_w_pairs = sgu_w.astype(BF16).reshape(DEPTH, N_SGU_GROUPS // 2, 2, CHUNK, CHUNK)
    sgu_w_pairs = jnp.concatenate([sgu_w_pairs[:, :, 0], sgu_w_pairs[:, :, 1]], axis=-1)
    sgu_bias = jnp.repeat(jnp.swapaxes(sgu_b, 1, 2), SGU_GROUP_DIM, axis=2)
    wr_cat, wr_hi, rb = _router_weights(router_g_w, router_g_b, router_e_w, router_e_b)
    fn = final_norm.reshape(1, D_MODEL)
    tri = jnp.asarray(np.triu(np.ones((TM, TM), np.float32)), BF16)
    ck = cache_k.reshape(DEC_BATCH, DEPTH, PAST_LEN, D_KV)
    cv = cache_v.reshape(DEC_BATCH, DEPTH, PAST_LEN, D_KV)

    ctx_ks, ctx_vs = [], []
    for l in range(DEPTH):
        qt, k, vt, k32, v32, sgu_n, cbz = _proj(l, xs, mod, g1, w_in_b, qg, kg, cos_t, sin_t, bq,
                                                sgu_w_pairs, sgu_bias, on)
        attn_ctx = _attn_ctx(l, qt, k, vt)
        attn_lat, w1, w3, w2 = _attn_lat(l, qt, k, vt, ck, cv, qg, kg, moe_w1, moe_w3, moe_w2)
        w2 = w2.reshape(N_EXPERT_GROUPS, D_GROUP_HID, D_MODEL)
        x1, h2, route, gate4, cnt = _merge(l, xs, attn_ctx, attn_lat, sgu_n, cbz, conv_w, on, w_out_b, mod, g2,
                                           wr_cat, wr_hi, rb)
        counts = cnt[:, :N_EXPERT_GROUPS, 0]
        out = _moe(l, counts, h2, route, gate4, tri, w1, w3, w2, x1, mod, fn, final=(l == DEPTH - 1))
        xs = (out,)
        ctx_ks.append(k32.reshape(BATCH, SEQ, N_KV_HEADS, HEAD_DIM))
        ctx_vs.append(v32.reshape(BATCH, SEQ, N_KV_HEADS, HEAD_DIM))

    y_prompt, y_sample = out
    return (y_prompt.reshape(BATCH, SEQ, D_MODEL), y_sample.reshape(DEC_BATCH, DEC_SEQ, D_MODEL),
            jnp.stack(ctx_ks, axis=1), jnp.stack(ctx_vs, axis=1))
```

```python
import functools
import math

import jax
import jax.numpy as jnp
import numpy as np
from jax import lax
from jax.experimental import pallas as pl
from jax.experimental.pallas import tpu as pltpu

D_MODEL = 1024
BATCH = 16
SEQ = 256
DEPTH = 2
DEC_BATCH = 2
DEC_SEQ = 4096
PAST_LEN = 512
GRID_W = 64
N_HEADS = 8
N_KV_HEADS = 2
HEAD_DIM = 64
ROPE_AXIS_DIM = HEAD_DIM // 2
ROPE_THETA = 10000.0
D_ATTN = N_HEADS * HEAD_DIM
D_KV = N_KV_HEADS * HEAD_DIM
CHUNK = 128
N_SGU_GROUPS = 4
SGU_GROUP_DIM = 64
D_SGU = N_SGU_GROUPS * SGU_GROUP_DIM
D_CONV = 256
D_MIX = D_ATTN + D_SGU + D_CONV
D_IN = D_ATTN + 2 * D_KV + 2 * D_SGU + 3 * D_CONV
N_EXPERT_GROUPS = 4
EXPERTS_PER_GROUP = 4
N_EXPERTS = N_EXPERT_GROUPS * EXPERTS_PER_GROUP
D_EXPERT = 256
D_GROUP_HID = EXPERTS_PER_GROUP * D_EXPERT
EPS = 1e-6

T_CTX = BATCH * SEQ
T_LAT = DEC_BATCH * DEC_SEQ
T_ALL = T_CTX + T_LAT
TM = 512
N_TILES = T_ALL // TM
N_CTX_TILES = T_CTX // TM
LAT_TILES_PER_BATCH = DEC_SEQ // TM
LANES = 128
SUBLANES = 8
BF16_ROWS = 16
N_MOD_ROWS = 8
N_MOD_PARTS = 6
ROUTER_COLS = 128
VMEM_LIMIT = 48 * 1024 * 1024

OFF_Q = 0
OFF_KV = D_ATTN
OFF_SGU = OFF_KV + 2 * D_KV
OFF_CONV = OFF_SGU + 2 * D_SGU

BF16 = jnp.bfloat16
F32 = jnp.float32
NEG_BIG = -1e30
Q_SCALE = HEAD_DIM ** -0.5 * math.log2(math.e)


def _params(n_grid_dims):
    return pltpu.CompilerParams(
        dimension_semantics=("arbitrary",) * n_grid_dims,
        vmem_limit_bytes=VMEM_LIMIT,
    )


def _mod_row_of_tile(i, tile=TM):
    n_ctx = T_CTX // tile
    return jnp.where(i < n_ctx, 0, 1 + (i - n_ctx) // (DEC_SEQ // tile))


def _mod_part(mod_ref, part, row):
    return mod_ref[part, pl.ds(row, 1), :]


def _rope_block_of_tile(i):
    return jnp.where(i < N_CTX_TILES, 0, 1 + (i - N_CTX_TILES) % LAT_TILES_PER_BATCH)


def _rms(x, gain):
    ms = jnp.mean(x * x, axis=-1, keepdims=True)
    return x * lax.rsqrt(ms + EPS) * gain


def _ctx_tile(i):
    return jnp.minimum(i, N_CTX_TILES - 1)


def _lat_tile(i):
    return jnp.maximum(i - N_CTX_TILES, 0)


MOD_PARTS_PER_STEP = 2
MOD_TN = MOD_PARTS_PER_STEP * D_MODEL


def _mod_kernel(ct_ref, w_ref, b_ref, o_ref):
    c = ct_ref[...]
    s = c / (1.0 + jnp.exp(-c))
    row = lax.broadcasted_iota(jnp.int32, (N_MOD_ROWS, D_MODEL), 0)
    for part in range(MOD_PARTS_PER_STEP):
        cols = slice(part * D_MODEL, (part + 1) * D_MODEL)
        w = w_ref[:, cols]
        out = jnp.zeros((N_MOD_ROWS, D_MODEL), F32)
        for r in range(1 + DEC_BATCH):
            acc = jnp.sum(w * s[:, r:r + 1], axis=0, keepdims=True)
            out = jnp.where(row == r, acc, out)
        o_ref[part] = out + b_ref[:, cols]


def _modulation(cvecs_t, w_ada, b_ada):
    n_col = N_MOD_PARTS * D_MODEL
    return pl.pallas_call(
        _mod_kernel,
        grid=(DEPTH, n_col // MOD_TN),
        in_specs=[
            pl.BlockSpec((D_MODEL, N_MOD_ROWS), lambda l, j: (0, 0)),
            pl.BlockSpec((None, D_MODEL, MOD_TN), lambda l, j: (l, 0, j)),
            pl.BlockSpec((None, 1, MOD_TN), lambda l, j: (l, 0, j)),
        ],
        out_specs=pl.BlockSpec((None, MOD_PARTS_PER_STEP, N_MOD_ROWS, D_MODEL), lambda l, j: (l, j, 0, 0)),
        out_shape=jax.ShapeDtypeStruct((DEPTH, N_MOD_PARTS, N_MOD_ROWS, D_MODEL), F32),
        compiler_params=_params(2),
        name="adaln_modulation",
    )(cvecs_t, w_ada, b_ada.reshape(DEPTH, 1, n_col))


def _mod_spec(l):
    return pl.BlockSpec((None, N_MOD_PARTS, N_MOD_ROWS, D_MODEL), lambda *idx: (l, 0, 0, 0))


def _swap16(x):
    lane = lax.broadcasted_iota(jnp.int32, x.shape, 1)
    up = pltpu.roll(x, 16, 1)
    down = pltpu.roll(x, LANES - 16, 1)
    return jnp.where((lane & 16) != 0, up, down)


def _rope(x, cos_t, sin_t):
    cols = []
    for j in range(x.shape[1] // LANES):
        xc = x[:, j * LANES:(j + 1) * LANES]
        cols.append(xc * cos_t + _swap16(xc) * sin_t)
    return cols[0] if len(cols) == 1 else jnp.concatenate(cols, axis=1)


def _proj_kernel(*refs, split_x):
    n_x = 2 if split_x else 1
    x_refs, refs, (scr_even, scr_odd) = refs[:n_x], refs[n_x:-2], refs[-2:]
    step = functools.partial(_proj_step, x_refs, refs, split_x)
    i = pl.program_id(0)

    @pl.when(i == 0)
    def _():
        scr_odd[...] = jnp.zeros((TM, D_IN), F32)

    last = i == N_TILES

    @pl.when(jnp.logical_and(i % 2 == 0, jnp.logical_not(last)))
    def _():
        step(scr_even, scr_odd)

    @pl.when(jnp.logical_and(i % 2 == 1, jnp.logical_not(last)))
    def _():
        step(scr_odd, scr_even)

    @pl.when(last)
    def _():
        step(None, scr_even if N_TILES % 2 == 1 else scr_odd)


def _proj_step(x_refs, refs, split_x, new_ref, proj):
    (mod_ref, g1_ref, w_ref, qg_ref, kg_ref, cos_ref, sin_ref, bq_ref, sw_ref, sb_ref, on_ref,
     qt_ref, k_ref, vt_ref, k32_ref, v32_ref, sgu_ref, cbz_ref) = refs
    i = pl.program_id(0)
    cur = _stage1_tile(i)

    is_ctx = i <= N_CTX_TILES
    cos_t = cos_ref[...]
    sin_t = sin_ref[...]

    q = proj[:, OFF_Q:OFF_Q + D_ATTN]
    q2 = (q * q).astype(BF16)
    half = D_ATTN // 2
    qms = jnp.concatenate(
        [jnp.dot(q2[:, :half], bq_ref[:half, :half], preferred_element_type=F32),
         jnp.dot(q2[:, half:], bq_ref[half:, half:], preferred_element_type=F32)], axis=1)
    qn = q * lax.rsqrt(qms + EPS) * qg_ref[...]
    qr = _rope(qn, cos_t, sin_t) * Q_SCALE
    qt_ref[...] = qr.T.astype(BF16)

    k = proj[:, OFF_KV:OFF_KV + D_KV]
    v = proj[:, OFF_KV + D_KV:OFF_KV + 2 * D_KV]
    kms = jnp.dot((k * k).astype(BF16), bq_ref[:D_KV, :D_KV], preferred_element_type=F32)
    kn = k * lax.rsqrt(kms + EPS) * kg_ref[...]
    k_ref[...] = _rope(kn, cos_t, sin_t).astype(BF16)
    vt_ref[...] = v.T.astype(BF16)

    su = proj[:, OFF_SGU:OFF_SGU + D_SGU]
    sv = proj[:, OFF_SGU + D_SGU:OFF_SGU + 2 * D_SGU]
    n_chunks = TM // CHUNK
    sv_wide = jnp.concatenate([sv[n * CHUNK:(n + 1) * CHUNK, :] for n in range(n_chunks)], axis=1)
    grp = lax.broadcasted_iota(jnp.int32, (CHUNK, n_chunks * D_SGU), 1) // SGU_GROUP_DIM % N_SGU_GROUPS
    mixed_wide = jnp.zeros((CHUNK, n_chunks * D_SGU), F32)
    for pair in range(N_SGU_GROUPS // 2):
        rhs = jnp.concatenate(
            [jnp.where(grp == 2 * pair, sv_wide, 0.0), jnp.where(grp == 2 * pair + 1, sv_wide, 0.0)],
            axis=0).astype(BF16)
        mixed_wide = mixed_wide + jnp.dot(sw_ref[pair], rhs, preferred_element_type=F32)
    on_sgu = on_ref[:, D_ATTN:D_ATTN + D_SGU]
    for n in range(n_chunks):
        sgu = su[n * CHUNK:(n + 1) * CHUNK, :] * (mixed_wide[:, n * D_SGU:(n + 1) * D_SGU] + sb_ref[...])
        sgu_ref[n * CHUNK:(n + 1) * CHUNK, :] = _rms(sgu, on_sgu).astype(BF16)

    cbz_ref[:, :D_CONV] = proj[:, OFF_CONV:OFF_CONV + D_CONV]
    cbz_ref[:, D_CONV:] = (proj[:, OFF_CONV + D_CONV:OFF_CONV + 2 * D_CONV]
                           * proj[:, OFF_CONV + 2 * D_CONV:OFF_CONV + 3 * D_CONV])

    if new_ref is not None:
        x = jnp.where(cur < N_CTX_TILES, x_refs[0][...], x_refs[1][...]) if split_x else x_refs[0][...]
        mrow = _mod_row_of_tile(cur)
        h = _rms(x, g1_ref[...] * (1.0 + _mod_part(mod_ref, 1, mrow))) + _mod_part(mod_ref, 0, mrow)
        new_ref[...] = jnp.dot(h.astype(BF16), w_ref[...], preferred_element_type=F32)

    @pl.when(is_ctx)
    def _():
        k32_ref[...] = kn
        v32_ref[...] = v


def _x_specs(xs, tile_of_step=lambda i: i):
    if len(xs) == 1:
        return [pl.BlockSpec((TM, D_MODEL), lambda i: (tile_of_step(i), 0))]
    return [pl.BlockSpec((TM, D_MODEL), lambda i: (_ctx_tile(tile_of_step(i)), 0)),
            pl.BlockSpec((TM, D_MODEL), lambda i: (_lat_tile(tile_of_step(i)), 0))]


def _stage1_tile(i):
    return jnp.minimum(i, N_TILES - 1)


def _stage2_tile(i):
    return jnp.maximum(i - 1, 0)


def _proj(l, xs, mod, g1, w_in, qg, kg, cos_t, sin_t, bq, sgu_w, sgu_b, on):
    lay = lambda *rest: (lambda i: (l,) + rest)
    s2 = _stage2_tile
    return pl.pallas_call(
        functools.partial(_proj_kernel, split_x=len(xs) == 2),
        grid=(N_TILES + 1,),
        in_specs=_x_specs(xs, _stage1_tile) + [
            _mod_spec(l),
            pl.BlockSpec((None, 1, D_MODEL), lay(0, 0)),
            pl.BlockSpec((None, D_MODEL, D_IN), lay(0, 0)),
            pl.BlockSpec((None, 1, D_ATTN), lay(0, 0)),
            pl.BlockSpec((None, 1, D_KV), lay(0, 0)),
            pl.BlockSpec((TM, LANES), lambda i: (_rope_block_of_tile(s2(i)), 0)),
            pl.BlockSpec((TM, LANES), lambda i: (_rope_block_of_tile(s2(i)), 0)),
            pl.BlockSpec((D_ATTN, D_ATTN), lambda i: (0, 0)),
            pl.BlockSpec((None, N_SGU_GROUPS // 2, CHUNK, 2 * CHUNK), lay(0, 0, 0)),
            pl.BlockSpec((None, CHUNK, D_SGU), lay(0, 0)),
            pl.BlockSpec((None, 1, D_MIX), lay(0, 0)),
        ],
        out_specs=[
            pl.BlockSpec((None, D_ATTN, TM), lambda i: (s2(i), 0, 0)),
            pl.BlockSpec((TM, D_KV), lambda i: (s2(i), 0)),
            pl.BlockSpec((None, D_KV, TM), lambda i: (s2(i), 0, 0)),
            pl.BlockSpec((TM, D_KV), lambda i: (_ctx_tile(s2(i)), 0)),
            pl.BlockSpec((TM, D_KV), lambda i: (_ctx_tile(s2(i)), 0)),
            pl.BlockSpec((TM, D_SGU), lambda i: (s2(i), 0)),
            pl.BlockSpec((TM, 2 * D_CONV), lambda i: (s2(i), 0)),
        ],
        out_shape=[
            jax.ShapeDtypeStruct((N_TILES, D_ATTN, TM), BF16),
            jax.ShapeDtypeStruct((T_ALL, D_KV), BF16),
            jax.ShapeDtypeStruct((N_TILES, D_KV, TM), BF16),
            jax.ShapeDtypeStruct((T_CTX, D_KV), F32),
            jax.ShapeDtypeStruct((T_CTX, D_KV), F32),
            jax.ShapeDtypeStruct((T_ALL, D_SGU), BF16),
            jax.ShapeDtypeStruct((T_ALL, 2 * D_CONV), F32),
        ],
        scratch_shapes=[pltpu.VMEM((TM, D_IN), F32), pltpu.VMEM((TM, D_IN), F32)],
        compiler_params=_params(1),
        name=f"proj_l{l}",
    )(*xs, mod, g1, w_in, qg, kg, cos_t, sin_t, bq, sgu_w, sgu_b, on)


ACC_ROWS = HEAD_DIM + BF16_ROWS
N_PAIRS = N_HEADS // 2
PAIRS_PER_KV = N_PAIRS // N_KV_HEADS


def _pair_queries(qt, kv_idx, tq):
    qf = qt.astype(F32)
    mine = lax.broadcasted_iota(jnp.int32, (2 * HEAD_DIM, tq), 0) // HEAD_DIM == kv_idx
    qe = qf[:HEAD_DIM]
    qo = qf[HEAD_DIM:]
    rhs = jnp.concatenate(
        [jnp.where(mine, jnp.concatenate([qe, qe], axis=0), 0.0),
         jnp.where(mine, jnp.concatenate([qo, qo], axis=0), 0.0)], axis=1)
    return rhs.astype(BF16)


def _with_ones(vt_c):
    return jnp.concatenate([vt_c, jnp.ones((BF16_ROWS, vt_c.shape[1]), BF16)], axis=0)


def _flash_pairs(rhs_list, chunk_lists, tq, shift=None):
    n_pairs, n_chunks = len(rhs_list), len(chunk_lists[0])
    scores = lambda s, c: jnp.dot(chunk_lists[s][c][0](), rhs_list[s], preferred_element_type=F32)
    m = [jnp.full((1, 2 * tq), NEG_BIG, F32)] * n_pairs
    acc = [jnp.zeros((ACC_ROWS, 2 * tq), F32)] * n_pairs
    st = [scores(s, 0) for s in range(n_pairs)]
    for c in range(n_chunks):
        for s in range(n_pairs):
            if shift is None:
                m_new = jnp.maximum(m[s], jnp.max(st[s], axis=0, keepdims=True))
            st_next = scores(s, c + 1) if c + 1 < n_chunks else None
            if shift is None:
                alpha = jnp.exp2(m[s] - m_new)
                p = jnp.exp2(st[s] - m_new).astype(BF16)
                acc[s] = alpha * acc[s] + jnp.dot(chunk_lists[s][c][1](), p, preferred_element_type=F32)
                m[s] = m_new
            else:
                p = jnp.exp2(st[s] - shift).astype(BF16)
                acc[s] = acc[s] + jnp.dot(chunk_lists[s][c][1](), p, preferred_element_type=F32)
            st[s] = st_next
    outs = []
    for s in range(n_pairs):
        o = acc[s][:HEAD_DIM] * (1.0 / acc[s][HEAD_DIM:HEAD_DIM + 1])
        outs.append(jnp.concatenate([o[:, :tq], o[:, tq:]], axis=0))
    return outs


def _attn_ctx_kernel(qt_ref, k_ref, vt_ref, o_ref):
    tq = SEQ
    rhs_list, chunk_lists = [], []
    for pair in range(N_PAIRS):
        kv_idx = pair // PAIRS_PER_KV
        rhs_list.append(_pair_queries(qt_ref[pair * LANES:(pair + 1) * LANES, :], kv_idx, tq))
        chunk_lists.append([(lambda: k_ref[...],
                             lambda kv_idx=kv_idx: _with_ones(vt_ref[kv_idx * HEAD_DIM:(kv_idx + 1) * HEAD_DIM, :]))])
    for pair, o_t in enumerate(_flash_pairs(rhs_list, chunk_lists, tq)):
        o_ref[:, pair * LANES:(pair + 1) * LANES] = o_t.T.astype(BF16)


def _attn_ctx(l, qt, k, vt):
    halves = TM // SEQ
    return pl.pallas_call(
        _attn_ctx_kernel,
        grid=(BATCH,),
        in_specs=[
            pl.BlockSpec((None, D_ATTN, SEQ), lambda b: (b // halves, 0, b % halves)),
            pl.BlockSpec((SEQ, D_KV), lambda b: (b, 0)),
            pl.BlockSpec((None, D_KV, SEQ), lambda b: (b // halves, 0, b % halves)),
        ],
        out_specs=pl.BlockSpec((SEQ, D_ATTN), lambda b: (b, 0)),
        out_shape=jax.ShapeDtypeStruct((T_CTX, D_ATTN), BF16),
        compiler_params=_params(1),
        name=f"attn_ctx_l{l}",
    )(qt, k, vt)


KC = 512
LAT_PAIRS_PER_STEP = 4


MAX_SHIFT = 60.0
BOUND_SLACK = 1.05


def _score_bound_sq(qg_ref, kg_ref, ck):
    qg = qg_ref[...]
    kg = kg_ref[...]
    gq2 = jnp.max(qg * qg, axis=1, keepdims=True)
    k2 = HEAD_DIM * jnp.max(kg * kg, axis=1, keepdims=True)
    if ck is not None:
        sq = ck * ck
        first = lax.broadcasted_iota(jnp.int32, sq.shape, 1) < HEAD_DIM
        n0 = jnp.sum(jnp.where(first, sq, 0.0), axis=1, keepdims=True)
        n1 = jnp.sum(jnp.where(first, 0.0, sq), axis=1, keepdims=True)
        k2 = jnp.maximum(k2, jnp.max(jnp.maximum(n0, n1), axis=0, keepdims=True))
    return (BOUND_SLACK * Q_SCALE) ** 2 * HEAD_DIM * gq2 * k2


def _run_with_score_bound(run, bound):
    small = bound <= MAX_SHIFT

    @pl.when(small)
    def _():
        run(bound)

    @pl.when(jnp.logical_not(small))
    def _():
        run(None)


def _attn_lat_kernel(qt_ref, k_ref, vt_ref, ck_ref, cv_ref, qg_ref, kg_ref, w1_ref, w3_ref, w2_ref,
                     o_ref, w1b_ref, w3b_ref, w2b_ref, bound_ref):
    tq = TM

    def run(shift):
        w1b_ref[...] = w1_ref[...].astype(BF16)
        w3b_ref[...] = w3_ref[...].astype(BF16)
        w2b_ref[...] = w2_ref[...].astype(BF16)
        cvt = cv_ref[...].T
        rhs_list, chunk_lists = [], []
        for j in range(LAT_PAIRS_PER_STEP):
            kv_idx = (pl.program_id(2) * LAT_PAIRS_PER_STEP + j) // PAIRS_PER_KV
            rhs_list.append(_pair_queries(qt_ref[j * LANES:(j + 1) * LANES, :], kv_idx, tq))
            v_row = pl.multiple_of(kv_idx * HEAD_DIM, HEAD_DIM)

            def latent_chunk(c, v_row=v_row):
                tile, lane0 = (c * KC) // TM, (c * KC) % TM
                return (lambda: k_ref[c * KC:(c + 1) * KC, :],
                        lambda: _with_ones(vt_ref[tile, pl.ds(v_row, HEAD_DIM), lane0:lane0 + KC]))

            def cached_chunk(c, kv_idx=kv_idx):
                return (lambda: ck_ref[c * KC:(c + 1) * KC, :].astype(BF16),
                        lambda: _with_ones(jnp.where(kv_idx == 0, cvt[:HEAD_DIM, c * KC:(c + 1) * KC],
                                                     cvt[HEAD_DIM:, c * KC:(c + 1) * KC]).astype(BF16)))

            chunks = [latent_chunk(c) for c in range(DEC_SEQ // KC)]
            chunks += [cached_chunk(c) for c in range(PAST_LEN // KC)]
            chunk_lists.append(chunks)
        for j, o_t in enumerate(_flash_pairs(rhs_list, chunk_lists, tq, shift)):
            o_ref[:, j * LANES:(j + 1) * LANES] = o_t.T.astype(BF16)

    @pl.when(jnp.logical_and(pl.program_id(1) == 0, pl.program_id(2) == 0))
    def _():
        bound_ref[0] = jnp.sqrt(_score_bound_sq(qg_ref, kg_ref, ck_ref[...]))[0, 0]

    _run_with_score_bound(run, bound_ref[0])


def _attn_lat(l, qt, k, vt, cache_k, cache_v, qg, kg, moe_w1, moe_w3, moe_w2):
    n_q = DEC_SEQ // TM
    first = N_CTX_TILES
    ctx_blocks = T_CTX // DEC_SEQ
    width = LAT_PAIRS_PER_STEP * LANES
    n_p = N_PAIRS // LAT_PAIRS_PER_STEP
    per_expert = DEC_BATCH * n_q * n_p // N_EXPERTS
    step = lambda b, i, p: (b * n_q + i) * n_p + p
    w_in_spec = lambda rows, cols: pl.BlockSpec(
        (None, None, rows // per_expert, cols),
        lambda b, i, p: (l, step(b, i, p) // per_expert, step(b, i, p) % per_expert, 0))
    w_out_spec = lambda rows, cols: pl.BlockSpec(
        (None, rows // per_expert, cols),
        lambda b, i, p: (step(b, i, p) // per_expert, step(b, i, p) % per_expert, 0))
    return pl.pallas_call(
        _attn_lat_kernel,
        grid=(DEC_BATCH, n_q, n_p),
        in_specs=[
            pl.BlockSpec((None, width, TM), lambda b, i, p: (first + b * n_q + i, p, 0)),
            pl.BlockSpec((DEC_SEQ, D_KV), lambda b, i, p: (ctx_blocks + b, 0)),
            pl.BlockSpec((n_q, D_KV, TM), lambda b, i, p: (ctx_blocks + b, 0, 0)),
            pl.BlockSpec((None, None, PAST_LEN, D_KV), lambda b, i, p: (b, l, 0, 0)),
            pl.BlockSpec((None, None, PAST_LEN, D_KV), lambda b, i, p: (b, l, 0, 0)),
            pl.BlockSpec((None, 1, D_ATTN), lambda b, i, p: (l, 0, 0)),
            pl.BlockSpec((None, 1, D_KV), lambda b, i, p: (l, 0, 0)),
            w_in_spec(D_MODEL, D_EXPERT),
            w_in_spec(D_MODEL, D_EXPERT),
            w_in_spec(D_EXPERT, D_MODEL),
        ],
        out_specs=[
            pl.BlockSpec((TM, width), lambda b, i, p: (b * n_q + i, p)),
            w_out_spec(D_MODEL, D_EXPERT),
            w_out_spec(D_MODEL, D_EXPERT),
            w_out_spec(D_EXPERT, D_MODEL),
        ],
        out_shape=[
            jax.ShapeDtypeStruct((T_LAT, D_ATTN), BF16),
            jax.ShapeDtypeStruct((N_EXPERTS, D_MODEL, D_EXPERT), BF16),
            jax.ShapeDtypeStruct((N_EXPERTS, D_MODEL, D_EXPERT), BF16),
            jax.ShapeDtypeStruct((N_EXPERTS, D_EXPERT, D_MODEL), BF16),
        ],
        scratch_shapes=[pltpu.SMEM((1,), F32)],
        compiler_params=_params(3),
        name=f"attn_lat_l{l}",
    )(qt, k, vt, cache_k, cache_v, qg, kg, moe_w1, moe_w3, moe_w2)


def _first_max_index(vals, valid, rowf):
    masked = jnp.where(valid, vals, -jnp.inf)
    mx = jnp.max(masked, axis=0, keepdims=True)
    idx = jnp.min(jnp.where(masked == mx, rowf, float(SUBLANES)), axis=0, keepdims=True)
    return mx, idx


def _merge_kernel(*refs, split_x):
    n_x = 2 if split_x else 1
    x_refs, refs = refs[:n_x], refs[n_x:]
    (actx_ref, alat_ref, sgu_ref, cbz_ref, prev_ref, next_ref, cw_ref, on_ref,
     wo_ref, mod_ref, g2_ref, wr_ref, wrhi_ref, rb_ref, x1_ref, h2_ref, route_ref, gate4_ref, cnt_ref) = refs
    cur = pl.program_id(0)
    is_ctx = cur < N_CTX_TILES
    mrow = _mod_row_of_tile(cur)
    on = on_ref[...]

    cbz = cbz_ref[...]
    cb = cbz[:, :D_CONV]
    z = cbz[:, D_CONV:]
    seq_mask = jnp.where(is_ctx, SEQ - 1, DEC_SEQ - 1)
    row = lax.broadcasted_iota(jnp.int32, (TM, D_CONV), 0)
    pos = (cur * TM + row) & seq_mask
    z_prev = jnp.where(row == 0, prev_ref[SUBLANES - 1:SUBLANES, D_CONV:], pltpu.roll(z, 1, 0))
    z_next = jnp.where(row == TM - 1, next_ref[0:1, D_CONV:], pltpu.roll(z, TM - 1, 0))
    z_prev = jnp.where(pos == 0, 0.0, z_prev)
    z_next = jnp.where(pos == seq_mask, 0.0, z_next)
    cw = cw_ref[...]
    conv = cb * (z_prev * cw[0:1, :] + z * cw[1:2, :] + z_next * cw[2:3, :])
    conv_n = _rms(conv, on[:, D_ATTN + D_SGU:]).astype(BF16)

    attn = jnp.where(is_ctx, actx_ref[...], alat_ref[...]).astype(F32)
    attn_n = _rms(attn, on[:, :D_ATTN]).astype(BF16)
    merged = jnp.concatenate([attn_n, sgu_ref[...], conv_n], axis=1)
    x = jnp.where(is_ctx, x_refs[0][...], x_refs[1][...]) if split_x else x_refs[0][...]
    x1 = x + _mod_part(mod_ref, 2, mrow) * jnp.dot(merged, wo_ref[...], preferred_element_type=F32)
    x1_ref[...] = x1
    _route_tile(x1, mrow, mod_ref, g2_ref, wr_ref, wrhi_ref, rb_ref, h2_ref, route_ref, gate4_ref, cnt_ref)


def _route_tile(x1, mrow, mod_ref, g2_ref, wr_ref, wrhi_ref, rb_ref, h2_ref, route_ref, gate4_ref, cnt_ref):
    n = x1.shape[0]
    h2 = _rms(x1, g2_ref[...] * (1.0 + _mod_part(mod_ref, 4, mrow))) + _mod_part(mod_ref, 3, mrow)
    hb = h2.astype(BF16)
    h2_ref[...] = hb

    h_lo = (h2 - hb.astype(F32)).astype(BF16)
    a = jnp.dot(hb, wr_ref[...], preferred_element_type=F32)
    b = jnp.dot(h_lo, wrhi_ref[...], preferred_element_type=F32)
    logits = a[:, :ROUTER_COLS] + a[:, ROUTER_COLS:] + b + rb_ref[...]
    lt = logits.T

    rowi = lax.broadcasted_iota(jnp.int32, (SUBLANES, n), 0).astype(F32)
    valid = rowi < float(N_EXPERT_GROUPS)
    g_log = lt[0:SUBLANES]
    g_max, g_idx = _first_max_index(g_log, valid, rowi)
    p_g = 1.0 / jnp.sum(jnp.where(valid, jnp.exp(g_log - g_max), 0.0), axis=0, keepdims=True)
    e_log = jnp.zeros((SUBLANES, n), F32)
    for g in range(N_EXPERT_GROUPS):
        e_log = jnp.where(g_idx == g, lt[(g + 1) * SUBLANES:(g + 2) * SUBLANES], e_log)
    e_max, i1 = _first_max_index(e_log, valid, rowi)
    e_exp = jnp.where(valid, jnp.exp(e_log - e_max), 0.0)
    e_prob = e_exp / jnp.sum(e_exp, axis=0, keepdims=True)
    v1 = jnp.max(e_prob, axis=0, keepdims=True)
    rest = jnp.logical_and(valid, rowi != i1)
    v2, i2 = _first_max_index(e_prob, rest, rowi)
    denom = v1 + v2
    w_sel = p_g * jnp.where(rowi == i1, v1 / denom, jnp.where(rowi == i2, v2 / denom, 0.0))
    route_ref[...] = jnp.broadcast_to(g_idx, (SUBLANES, n))
    pad = jnp.zeros((LANES - SUBLANES, n), F32)
    gate4_ref[...] = jnp.concatenate([w_sel, pad], axis=0).T
    counts = jnp.sum(jnp.where(rowi == g_idx, 1.0, 0.0), axis=1, keepdims=True)
    cnt_ref[...] = jnp.broadcast_to(counts, (SUBLANES, LANES)).astype(jnp.int32)


def _merge(l, xs, attn_ctx, attn_lat, sgu_n, cbz, conv_w, on, w_out, mod, g2, wr_cat, wr_hi, rb):
    lay = lambda *rest: (lambda i: (l,) + rest)
    rows8 = TM // SUBLANES
    return pl.pallas_call(
        functools.partial(_merge_kernel, split_x=len(xs) == 2),
        grid=(N_TILES,),
        in_specs=_x_specs(xs) + [
            pl.BlockSpec((TM, D_ATTN), lambda i: (_ctx_tile(i), 0)),
            pl.BlockSpec((TM, D_ATTN), lambda i: (_lat_tile(i), 0)),
            pl.BlockSpec((TM, D_SGU), lambda i: (i, 0)),
            pl.BlockSpec((TM, 2 * D_CONV), lambda i: (i, 0)),
            pl.BlockSpec((SUBLANES, 2 * D_CONV), lambda i: (jnp.maximum(i * rows8 - 1, 0), 0)),
            pl.BlockSpec((SUBLANES, 2 * D_CONV),
                         lambda i: (jnp.minimum((i + 1) * rows8, T_ALL // SUBLANES - 1), 0)),
            pl.BlockSpec((None, 3, D_CONV), lay(0, 0)),
            pl.BlockSpec((None, 1, D_MIX), lay(0, 0)),
            pl.BlockSpec((None, D_MIX, D_MODEL), lay(0, 0)),
            _mod_spec(l),
            pl.BlockSpec((None, 1, D_MODEL), lay(0, 0)),
            pl.BlockSpec((None, D_MODEL, 2 * ROUTER_COLS), lay(0, 0)),
            pl.BlockSpec((None, D_MODEL, ROUTER_COLS), lay(0, 0)),
            pl.BlockSpec((None, 1, ROUTER_COLS), lay(0, 0)),
        ],
        out_specs=[
            pl.BlockSpec((TM, D_MODEL), lambda i: (i, 0)),
            pl.BlockSpec((TM, D_MODEL), lambda i: (i, 0)),
            pl.BlockSpec((None, SUBLANES, TM), lambda i: (i, 0, 0)),
            pl.BlockSpec((TM, LANES), lambda i: (i, 0)),
            pl.BlockSpec((None, SUBLANES, LANES), lambda i: (i, 0, 0)),
        ],
        out_shape=[
            jax.ShapeDtypeStruct((T_ALL, D_MODEL), F32),
            jax.ShapeDtypeStruct((T_ALL, D_MODEL), BF16),
            jax.ShapeDtypeStruct((N_TILES, SUBLANES, TM), F32),
            jax.ShapeDtypeStruct((T_ALL, LANES), F32),
            jax.ShapeDtypeStruct((N_TILES, SUBLANES, LANES), jnp.int32),
        ],
        compiler_params=_params(1),
        name=f"merge_l{l}",
    )(*xs, attn_ctx, attn_lat, sgu_n, cbz, cbz, cbz, conv_w, on, w_out, mod, g2, wr_cat, wr_hi, rb)


RB = 144
SORT_ROWS = TM + LANES
SORT_ROWS_PAD = -(-(SORT_ROWS + RB) // BF16_ROWS) * BF16_ROWS
MOE_VMEM_LIMIT = 56 * 1024 * 1024


def _split3(x):
    hi = x.astype(BF16)
    r1 = x - hi.astype(F32)
    mid = r1.astype(BF16)
    lo = (r1 - mid.astype(F32)).astype(BF16)
    return hi, mid, lo


def _moe_kernel(cnt_ref, h_ref, route_ref, gate4_ref, tri_ref, w1_ref, w3_ref, w2_ref, x1_ref, mod_ref, fn_ref,
                *refs, final):
    out_refs, (xs_ref, gs_ref, zs_ref) = refs[:-3], refs[-3:]
    i = pl.program_id(0)
    gate2 = _mod_part(mod_ref, 5, _mod_row_of_tile(i))
    counts = [cnt_ref[i, g] for g in range(N_EXPERT_GROUPS)]
    starts = [jnp.int32(0)]
    for g in range(N_EXPERT_GROUPS - 1):
        starts.append(starts[-1] + (counts[g] + (BF16_ROWS - 1)) // BF16_ROWS * BF16_ROWS)

    g_idx = route_ref[0:1, :]
    rowi = lax.broadcasted_iota(jnp.int32, (SUBLANES, TM), 0).astype(F32)
    onehot = jnp.where(rowi == g_idx, 1.0, 0.0)
    incl = jnp.dot(onehot.astype(BF16), tri_ref[...], preferred_element_type=F32)
    pos = jnp.sum(onehot * incl, axis=0, keepdims=True) - 1.0
    for g in range(1, N_EXPERT_GROUPS):
        pos = pos + jnp.where(g_idx == float(g), starts[g].astype(F32), 0.0)
    sub = lax.broadcasted_iota(jnp.int32, (SORT_ROWS, TM), 0).astype(F32)
    perm = jnp.where(sub == pos, 1.0, 0.0).astype(BF16)
    pos_col = jnp.broadcast_to(pos, (LANES, TM)).T
    lane = lax.broadcasted_iota(jnp.int32, (TM, SORT_ROWS), 1).astype(F32)
    perm_t = jnp.where(lane == jnp.concatenate([pos_col] * (SORT_ROWS // LANES), axis=1), 1.0, 0.0).astype(BF16)

    xs_ref[:SORT_ROWS, :] = jnp.dot(perm, h_ref[...], preferred_element_type=F32).astype(BF16)
    xs_ref[SORT_ROWS:, :] = jnp.zeros((SORT_ROWS_PAD - SORT_ROWS, D_MODEL), BF16)
    hi, mid, lo = _split3(gate4_ref[...])
    packed = (hi.astype(F32) + pltpu.roll(mid.astype(F32), EXPERTS_PER_GROUP, 1)
              + pltpu.roll(lo.astype(F32), 2 * EXPERTS_PER_GROUP, 1)).astype(BF16)
    gsp = jnp.dot(perm, packed, preferred_element_type=F32)
    gs_ref[:SORT_ROWS, :] = (gsp + pltpu.roll(gsp, LANES - EXPERTS_PER_GROUP, 1)
                             + pltpu.roll(gsp, LANES - 2 * EXPERTS_PER_GROUP, 1))
    gs_ref[SORT_ROWS:, :] = jnp.zeros((SORT_ROWS_PAD - SORT_ROWS, LANES), F32)
    zs_ref[...] = jnp.zeros((SORT_ROWS_PAD, D_MODEL), BF16)

    def block_ffn(g, blk):
        row0 = pl.multiple_of(starts[g] + blk * RB, BF16_ROWS)
        xb = xs_ref[pl.ds(row0, RB), :]
        gb = gs_ref[pl.ds(row0, RB), :]
        cols = []
        for e in range(EXPERTS_PER_GROUP):
            a = jnp.dot(xb, w1_ref[g * EXPERTS_PER_GROUP + e], preferred_element_type=F32)
            b = jnp.dot(xb, w3_ref[g * EXPERTS_PER_GROUP + e], preferred_element_type=F32)
            cols.append((a / (1.0 + jnp.exp(-a)) * b * gb[:, e:e + 1]).astype(BF16))
        z = jnp.dot(jnp.concatenate(cols, axis=1), w2_ref[g], preferred_element_type=F32)
        zs_ref[pl.ds(row0, RB), :] = z.astype(BF16)

    for g in range(N_EXPERT_GROUPS):
        def extra_block(blk, carry, g=g):
            block_ffn(g, blk)
            return carry

        lax.fori_loop(1, (counts[g] + (RB - 1)) // RB, extra_block, 0)
    for g in range(N_EXPERT_GROUPS):
        block_ffn(g, 0)

    y = jnp.dot(perm_t, zs_ref[:SORT_ROWS, :], preferred_element_type=F32)
    out = x1_ref[...] + gate2 * y

    if final:
        octx_ref, olat_ref = out_refs
        out = _rms(out, fn_ref[...])
        olat_ref[...] = out

        @pl.when(i < N_CTX_TILES)
        def _():
            octx_ref[...] = out
    else:
        out_refs[0][...] = out


def _moe(l, counts, h2, route, gate4, tri, w1, w3, w2, x1, mod, final_norm, final):
    tile = pl.BlockSpec((TM, D_MODEL), lambda i, c: (i, 0))
    resident = pl.Buffered(1)
    if final:
        out_specs = [pl.BlockSpec((TM, D_MODEL), lambda i, c: (_ctx_tile(i), 0)),
                     pl.BlockSpec((TM, D_MODEL), lambda i, c: (_lat_tile(i), 0))]
        out_shape = [jax.ShapeDtypeStruct((T_CTX, D_MODEL), F32), jax.ShapeDtypeStruct((T_LAT, D_MODEL), F32)]
    else:
        out_specs = tile
        out_shape = jax.ShapeDtypeStruct((T_ALL, D_MODEL), F32)
    grid_spec = pltpu.PrefetchScalarGridSpec(
        num_scalar_prefetch=1,
        grid=(N_TILES,),
        in_specs=[
            tile,
            pl.BlockSpec((None, SUBLANES, TM), lambda i, c: (i, 0, 0)),
            pl.BlockSpec((TM, LANES), lambda i, c: (i, 0)),
            pl.BlockSpec((TM, TM), lambda i, c: (0, 0), pipeline_mode=resident),
            pl.BlockSpec((N_EXPERTS, D_MODEL, D_EXPERT), lambda i, c: (0, 0, 0), pipeline_mode=resident),
            pl.BlockSpec((N_EXPERTS, D_MODEL, D_EXPERT), lambda i, c: (0, 0, 0), pipeline_mode=resident),
            pl.BlockSpec((N_EXPERT_GROUPS, D_GROUP_HID, D_MODEL), lambda i, c: (0, 0, 0), pipeline_mode=resident),
            tile,
            pl.BlockSpec((None, N_MOD_PARTS, N_MOD_ROWS, D_MODEL), lambda i, c: (l, 0, 0, 0)),
            pl.BlockSpec((1, D_MODEL), lambda i, c: (0, 0)),
        ],
        out_specs=out_specs,
        scratch_shapes=[pltpu.VMEM((SORT_ROWS_PAD, D_MODEL), BF16),
                        pltpu.VMEM((SORT_ROWS_PAD, LANES), F32),
                        pltpu.VMEM((SORT_ROWS_PAD, D_MODEL), BF16)],
    )
    return pl.pallas_call(
        functools.partial(_moe_kernel, final=final),
        grid_spec=grid_spec,
        out_shape=out_shape,
        compiler_params=pltpu.CompilerParams(dimension_semantics=("arbitrary",),
                                             vmem_limit_bytes=MOE_VMEM_LIMIT),
        name=f"moe_l{l}",
    )(counts, h2, route, gate4, tri, w1, w3, w2, x1, mod, final_norm)


def _rope_tables():
    rows = DEC_SEQ // GRID_W
    row = np.repeat(np.arange(rows, dtype=np.float32), GRID_W)
    col = np.tile(np.arange(GRID_W, dtype=np.float32), rows)
    half = ROPE_AXIS_DIM // 2
    inv_freq = (1.0 / (np.float32(ROPE_THETA) ** (np.arange(half, dtype=np.float32) * np.float32(2.0)
                                                  / np.float32(ROPE_AXIS_DIM)))).astype(np.float32)
    ar = row[:, None] * inv_freq
    ac = col[:, None] * inv_freq
    cos64 = np.concatenate([np.cos(ar), np.cos(ar), np.cos(ac), np.cos(ac)], axis=1)
    sin64 = np.concatenate([-np.sin(ar), np.sin(ar), -np.sin(ac), np.sin(ac)], axis=1)
    cos_t = np.concatenate([np.ones((TM, LANES), np.float32), np.tile(cos64, (1, 2))], axis=0)
    sin_t = np.concatenate([np.zeros((TM, LANES), np.float32), np.tile(sin64, (1, 2))], axis=0)
    return jnp.asarray(cos_t, F32), jnp.asarray(sin_t, F32)


def _router_weights(router_g_w, router_g_b, router_e_w, router_e_b):
    zw = jnp.zeros((DEPTH, D_MODEL, SUBLANES - N_EXPERT_GROUPS), F32)
    zb = jnp.zeros((DEPTH, SUBLANES - N_EXPERT_GROUPS), F32)
    w_cols, b_cols = [router_g_w, zw], [router_g_b, zb]
    for g in range(N_EXPERT_GROUPS):
        w_cols += [router_e_w[:, g], zw]
        b_cols += [router_e_b[:, g], zb]
    used = (1 + N_EXPERT_GROUPS) * SUBLANES
    w_cols.append(jnp.zeros((DEPTH, D_MODEL, ROUTER_COLS - used), F32))
    b_cols.append(jnp.zeros((DEPTH, ROUTER_COLS - used), F32))
    w = jnp.concatenate(w_cols, axis=2)
    b = jnp.concatenate(b_cols, axis=1).reshape(DEPTH, 1, ROUTER_COLS)
    w_hi = w.astype(BF16)
    w_lo = (w - w_hi.astype(F32)).astype(BF16)
    return jnp.concatenate([w_hi, w_lo], axis=2), w_hi, b


def kernel(x_prompt, x_sample, cache_k, cache_v, c, c_ctx, w_ada, b_ada, norm1, w_in, q_norm, k_norm,
           sgu_w, sgu_b, conv_w, out_norm, w_out, norm2, router_g_w, router_g_b, router_e_w,
           router_e_b, moe_w1, moe_w3, moe_w2, final_norm):
    xs = (x_prompt.reshape(T_CTX, D_MODEL), x_sample.reshape(T_LAT, D_MODEL))

    cvecs = jnp.concatenate([c_ctx[None, :], c, jnp.zeros((N_MOD_ROWS - 1 - DEC_BATCH, D_MODEL), F32)], axis=0)
    mod = _modulation(cvecs.T, w_ada, b_ada)

    w_in_b = w_in.astype(BF16)
    w_out_b = w_out.astype(BF16)
    g1 = norm1.reshape(DEPTH, 1, D_MODEL)
    g2 = norm2.reshape(DEPTH, 1, D_MODEL)
    on = out_norm.reshape(DEPTH, 1, D_MIX)
    qg = jnp.tile(q_norm, (1, N_HEADS)).reshape(DEPTH, 1, D_ATTN)
    kg = jnp.tile(k_norm, (1, N_KV_HEADS)).reshape(DEPTH, 1, D_KV)
    cos_t, sin_t = _rope_tables()
    bq = jnp.asarray(np.kron(np.eye(N_HEADS), np.full((HEAD_DIM, HEAD_DIM), 1.0 / HEAD_DIM)), BF16)
    sgu_w_pairs = sgu_w.astype(BF16).reshape(DEPTH, N_SGU_GROUPS // 2, 2, CHUNK, CHUNK)
    sgu_w_pairs = jnp.concatenate([sgu_w_pairs[:, :, 0], sgu_w_pairs[:, :, 1]], axis=-1)
    sgu_bias = jnp.repeat(jnp.swapaxes(sgu_b, 1, 2), SGU_GROUP_DIM, axis=2)
    wr_cat, wr_hi, rb = _router_weights(router_g_w, router_g_b, router_e_w, router_e_b)
    fn = final_norm.reshape(1, D_MODEL)
    tri = jnp.asarray(np.triu(np.ones((TM, TM), np.float32)), BF16)
    ck = cache_k.reshape(DEC_BATCH, DEPTH, PAST_LEN, D_KV)
    cv = cache_v.reshape(DEC_BATCH, DEPTH, PAST_LEN, D_KV)

    ctx_ks, ctx_vs = [], []
    for l in range(DEPTH):
        qt, k, vt, k32, v32, sgu_n, cbz = _proj(l, xs, mod, g1, w_in_b, qg, kg, cos_t, sin_t, bq,
                                                sgu_w_pairs, sgu_bias, on)
        attn_ctx = _attn_ctx(l, qt, k, vt)
        attn_lat, w1, w3, w2 = _attn_lat(l, qt, k, vt, ck, cv, qg, kg, moe_w1, moe_w3, moe_w2)
        w2 = w2.reshape(N_EXPERT_GROUPS, D_GROUP_HID, D_MODEL)
        x1, h2, route, gate4, cnt = _merge(l, xs, attn_ctx, attn_lat, sgu_n, cbz, conv_w, on, w_out_b, mod, g2,
                                           wr_cat, wr_hi, rb)
        counts = cnt[:, :N_EXPERT_GROUPS, 0]
        out = _moe(l, counts, h2, route, gate4, tri, w1, w3, w2, x1, mod, fn, final=(l == DEPTH - 1))
        xs = (out,)
        ctx_ks.append(k32.reshape(BATCH, SEQ, N_KV_HEADS, HEAD_DIM))
        ctx_vs.append(v32.reshape(BATCH, SEQ, N_KV_HEADS, HEAD_DIM))

    y_prompt, y_sample = out
    return (y_prompt.reshape(BATCH, SEQ, D_MODEL), y_sample.reshape(DEC_BATCH, DEC_SEQ, D_MODEL),
            jnp.stack(ctx_ks, axis=1), jnp.stack(ctx_vs, axis=1))
```

```python
import functools
import math

import jax
import jax.numpy as jnp
import numpy as np
from jax import lax
from jax.experimental import pallas as pl
from jax.experimental.pallas import tpu as pltpu

D_MODEL = 1024
BATCH = 16
SEQ = 256
DEPTH = 2
DEC_BATCH = 2
DEC_SEQ = 4096
PAST_LEN = 512
GRID_W = 64
N_HEADS = 8
N_KV_HEADS = 2
HEAD_DIM = 64
ROPE_AXIS_DIM = HEAD_DIM // 2
ROPE_THETA = 10000.0
D_ATTN = N_HEADS * HEAD_DIM
D_KV = N_KV_HEADS * HEAD_DIM
CHUNK = 128
N_SGU_GROUPS = 4
SGU_GROUP_DIM = 64
D_SGU = N_SGU_GROUPS * SGU_GROUP_DIM
D_CONV = 256
D_MIX = D_ATTN + D_SGU + D_CONV
D_IN = D_ATTN + 2 * D_KV + 2 * D_SGU + 3 * D_CONV
N_EXPERT_GROUPS = 4
EXPERTS_PER_GROUP = 4
N_EXPERTS = N_EXPERT_GROUPS * EXPERTS_PER_GROUP
D_EXPERT = 256
D_GROUP_HID = EXPERTS_PER_GROUP * D_EXPERT
EPS = 1e-6

T_CTX = BATCH * SEQ
T_LAT = DEC_BATCH * DEC_SEQ
T_ALL = T_CTX + T_LAT
TM = 512
N_TILES = T_ALL // TM
N_CTX_TILES = T_CTX // TM
LAT_TILES_PER_BATCH = DEC_SEQ // TM
LANES = 128
SUBLANES = 8
BF16_ROWS = 16
N_MOD_ROWS = 8
N_MOD_PARTS = 6
ROUTER_COLS = 128
VMEM_LIMIT = 48 * 1024 * 1024

OFF_Q = 0
OFF_KV = D_ATTN
OFF_SGU = OFF_KV + 2 * D_KV
OFF_CONV = OFF_SGU + 2 * D_SGU

BF16 = jnp.bfloat16
F32 = jnp.float32
NEG_BIG = -1e30
Q_SCALE = HEAD_DIM ** -0.5 * math.log2(math.e)


def _params(n_grid_dims):
    return pltpu.CompilerParams(
        dimension_semantics=("arbitrary",) * n_grid_dims,
        vmem_limit_bytes=VMEM_LIMIT,
    )


def _mod_row_of_tile(i, tile=TM):
    n_ctx = T_CTX // tile
    return jnp.where(i < n_ctx, 0, 1 + (i - n_ctx) // (DEC_SEQ // tile))


def _mod_part(mod_ref, part, row):
    return mod_ref[part, pl.ds(row, 1), :]


def _rope_block_of_tile(i):
    return jnp.where(i < N_CTX_TILES, 0, 1 + (i - N_CTX_TILES) % LAT_TILES_PER_BATCH)


def _rms(x, gain):
    ms = jnp.mean(x * x, axis=-1, keepdims=True)
    return x * lax.rsqrt(ms + EPS) * gain


def _ctx_tile(i):
    return jnp.minimum(i, N_CTX_TILES - 1)


def _lat_tile(i):
    return jnp.maximum(i - N_CTX_TILES, 0)


MOD_PARTS_PER_STEP = 2
MOD_TN = MOD_PARTS_PER_STEP * D_MODEL


def _mod_kernel(ct_ref, w_ref, b_ref, o_ref):
    c = ct_ref[...]
    s = c / (1.0 + jnp.exp(-c))
    row = lax.broadcasted_iota(jnp.int32, (N_MOD_ROWS, D_MODEL), 0)
    for part in range(MOD_PARTS_PER_STEP):
        cols = slice(part * D_MODEL, (part + 1) * D_MODEL)
        w = w_ref[:, cols]
        out = jnp.zeros((N_MOD_ROWS, D_MODEL), F32)
        for r in range(1 + DEC_BATCH):
            acc = jnp.sum(w * s[:, r:r + 1], axis=0, keepdims=True)
            out = jnp.where(row == r, acc, out)
        o_ref[part] = out + b_ref[:, cols]


def _modulation(cvecs_t, w_ada, b_ada):
    n_col = N_MOD_PARTS * D_MODEL
    return pl.pallas_call(
        _mod_kernel,
        grid=(DEPTH, n_col // MOD_TN),
        in_specs=[
            pl.BlockSpec((D_MODEL, N_MOD_ROWS), lambda l, j: (0, 0)),
            pl.BlockSpec((None, D_MODEL, MOD_TN), lambda l, j: (l, 0, j)),
            pl.BlockSpec((None, 1, MOD_TN), lambda l, j: (l, 0, j)),
        ],
        out_specs=pl.BlockSpec((None, MOD_PARTS_PER_STEP, N_MOD_ROWS, D_MODEL), lambda l, j: (l, j, 0, 0)),
        out_shape=jax.ShapeDtypeStruct((DEPTH, N_MOD_PARTS, N_MOD_ROWS, D_MODEL), F32),
        compiler_params=_params(2),
        name="adaln_modulation",
    )(cvecs_t, w_ada, b_ada.reshape(DEPTH, 1, n_col))


def _mod_spec(l):
    return pl.BlockSpec((None, N_MOD_PARTS, N_MOD_ROWS, D_MODEL), lambda *idx: (l, 0, 0, 0))


def _swap16(x):
    lane = lax.broadcasted_iota(jnp.int32, x.shape, 1)
    up = pltpu.roll(x, 16, 1)
    down = pltpu.roll(x, LANES - 16, 1)
    return jnp.where((lane & 16) != 0, up, down)


def _rope(x, cos_t, sin_t):
    cols = []
    for j in range(x.shape[1] // LANES):
        xc = x[:, j * LANES:(j + 1) * LANES]
        cols.append(xc * cos_t + _swap16(xc) * sin_t)
    return cols[0] if len(cols) == 1 else jnp.concatenate(cols, axis=1)


def _proj_kernel(*refs, split_x):
    n_x = 2 if split_x else 1
    x_refs, refs, (scr_even, scr_odd) = refs[:n_x], refs[n_x:-2], refs[-2:]
    step = functools.partial(_proj_step, x_refs, refs, split_x)
    i = pl.program_id(0)

    @pl.when(i == 0)
    def _():
        scr_odd[...] = jnp.zeros((TM, D_IN), F32)

    last = i == N_TILES

    @pl.when(jnp.logical_and(i % 2 == 0, jnp.logical_not(last)))
    def _():
        step(scr_even, scr_odd)

    @pl.when(jnp.logical_and(i % 2 == 1, jnp.logical_not(last)))
    def _():
        step(scr_odd, scr_even)

    @pl.when(last)
    def _():
        step(None, scr_even if N_TILES % 2 == 1 else scr_odd)


def _proj_step(x_refs, refs, split_x, new_ref, proj):
    (mod_ref, g1_ref, w_ref, qg_ref, kg_ref, cos_ref, sin_ref, bq_ref, sw_ref, sb_ref, on_ref,
     qt_ref, k_ref, vt_ref, k32_ref, v32_ref, sgu_ref, cbz_ref) = refs
    i = pl.program_id(0)
    cur = _stage1_tile(i)

    is_ctx = i <= N_CTX_TILES
    cos_t = cos_ref[...]
    sin_t = sin_ref[...]

    q = proj[:, OFF_Q:OFF_Q + D_ATTN]
    q2 = (q * q).astype(BF16)
    half = D_ATTN // 2
    qms = jnp.concatenate(
        [jnp.dot(q2[:, :half], bq_ref[:half, :half], preferred_element_type=F32),
         jnp.dot(q2[:, half:], bq_ref[half:, half:], preferred_element_type=F32)], axis=1)
    qn = q * lax.rsqrt(qms + EPS) * qg_ref[...]
    qr = _rope(qn, cos_t, sin_t) * Q_SCALE
    qt_ref[...] = qr.T.astype(BF16)

    k = proj[:, OFF_KV:OFF_KV + D_KV]
    v = proj[:, OFF_KV + D_KV:OFF_KV + 2 * D_KV]
    kms = jnp.dot((k * k).astype(BF16), bq_ref[:D_KV, :D_KV], preferred_element_type=F32)
    kn = k * lax.rsqrt(kms + EPS) * kg_ref[...]
    k_ref[...] = _rope(kn, cos_t, sin_t).astype(BF16)
    vt_ref[...] = v.T.astype(BF16)

    su = proj[:, OFF_SGU:OFF_SGU + D_SGU]
    sv = proj[:, OFF_SGU + D_SGU:OFF_SGU + 2 * D_SGU]
    n_chunks = TM // CHUNK
    sv_wide = jnp.concatenate([sv[n * CHUNK:(n + 1) * CHUNK, :] for n in range(n_chunks)], axis=1)
    grp = lax.broadcasted_iota(jnp.int32, (CHUNK, n_chunks * D_SGU), 1) // SGU_GROUP_DIM % N_SGU_GROUPS
    mixed_wide = jnp.zeros((CHUNK, n_chunks * D_SGU), F32)
    for pair in range(N_SGU_GROUPS // 2):
        rhs = jnp.concatenate(
            [jnp.where(grp == 2 * pair, sv_wide, 0.0), jnp.where(grp == 2 * pair + 1, sv_wide, 0.0)],
            axis=0).astype(BF16)
        mixed_wide = mixed_wide + jnp.dot(sw_ref[pair], rhs, preferred_element_type=F32)
    on_sgu = on_ref[:, D_ATTN:D_ATTN + D_SGU]
    for n in range(n_chunks):
        sgu = su[n * CHUNK:(n + 1) * CHUNK, :] * (mixed_wide[:, n * D_SGU:(n + 1) * D_SGU] + sb_ref[...])
        sgu_ref[n * CHUNK:(n + 1) * CHUNK, :] = _rms(sgu, on_sgu).astype(BF16)

    cbz_ref[:, :D_CONV] = proj[:, OFF_CONV:OFF_CONV + D_CONV]
    cbz_ref[:, D_CONV:] = (proj[:, OFF_CONV + D_CONV:OFF_CONV + 2 * D_CONV]
                           * proj[:, OFF_CONV + 2 * D_CONV:OFF_CONV + 3 * D_CONV])

    if new_ref is not None:
        x = jnp.where(cur < N_CTX_TILES, x_refs[0][...], x_refs[1][...]) if split_x else x_refs[0][...]
        mrow = _mod_row_of_tile(cur)
        h = _rms(x, g1_ref[...] * (1.0 + _mod_part(mod_ref, 1, mrow))) + _mod_part(mod_ref, 0, mrow)
        new_ref[...] = jnp.dot(h.astype(BF16), w_ref[...], preferred_element_type=F32)

    @pl.when(is_ctx)
    def _():
        k32_ref[...] = kn
        v32_ref[...] = v


def _x_specs(xs, tile_of_step=lambda i: i):
    if len(xs) == 1:
        return [pl.BlockSpec((TM, D_MODEL), lambda i: (tile_of_step(i), 0))]
    return [pl.BlockSpec((TM, D_MODEL), lambda i: (_ctx_tile(tile_of_step(i)), 0)),
            pl.BlockSpec((TM, D_MODEL), lambda i: (_lat_tile(tile_of_step(i)), 0))]


def _stage1_tile(i):
    return jnp.minimum(i, N_TILES - 1)


def _stage2_tile(i):
    return jnp.maximum(i - 1, 0)


def _proj(l, xs, mod, g1, w_in, qg, kg, cos_t, sin_t, bq, sgu_w, sgu_b, on):
    lay = lambda *rest: (lambda i: (l,) + rest)
    s2 = _stage2_tile
    return pl.pallas_call(
        functools.partial(_proj_kernel, split_x=len(xs) == 2),
        grid=(N_TILES + 1,),
        in_specs=_x_specs(xs, _stage1_tile) + [
            _mod_spec(l),
            pl.BlockSpec((None, 1, D_MODEL), lay(0, 0)),
            pl.BlockSpec((None, D_MODEL, D_IN), lay(0, 0)),
            pl.BlockSpec((None, 1, D_ATTN), lay(0, 0)),
            pl.BlockSpec((None, 1, D_KV), lay(0, 0)),
            pl.BlockSpec((TM, LANES), lambda i: (_rope_block_of_tile(s2(i)), 0)),
            pl.BlockSpec((TM, LANES), lambda i: (_rope_block_of_tile(s2(i)), 0)),
            pl.BlockSpec((D_ATTN, D_ATTN), lambda i: (0, 0)),
            pl.BlockSpec((None, N_SGU_GROUPS // 2, CHUNK, 2 * CHUNK), lay(0, 0, 0)),
            pl.BlockSpec((None, CHUNK, D_SGU), lay(0, 0)),
            pl.BlockSpec((None, 1, D_MIX), lay(0, 0)),
        ],
        out_specs=[
            pl.BlockSpec((None, D_ATTN, TM), lambda i: (s2(i), 0, 0)),
            pl.BlockSpec((TM, D_KV), lambda i: (s2(i), 0)),
            pl.BlockSpec((None, D_KV, TM), lambda i: (s2(i), 0, 0)),
            pl.BlockSpec((TM, D_KV), lambda i: (_ctx_tile(s2(i)), 0)),
            pl.BlockSpec((TM, D_KV), lambda i: (_ctx_tile(s2(i)), 0)),
            pl.BlockSpec((TM, D_SGU), lambda i: (s2(i), 0)),
            pl.BlockSpec((TM, 2 * D_CONV), lambda i: (s2(i), 0)),
        ],
        out_shape=[
            jax.ShapeDtypeStruct((N_TILES, D_ATTN, TM), BF16),
            jax.ShapeDtypeStruct((T_ALL, D_KV), BF16),
            jax.ShapeDtypeStruct((N_TILES, D_KV, TM), BF16),
            jax.ShapeDtypeStruct((T_CTX, D_KV), F32),
            jax.ShapeDtypeStruct((T_CTX, D_KV), F32),
            jax.ShapeDtypeStruct((T_ALL, D_SGU), BF16),
            jax.ShapeDtypeStruct((T_ALL, 2 * D_CONV), F32),
        ],
        scratch_shapes=[pltpu.VMEM((TM, D_IN), F32), pltpu.VMEM((TM, D_IN), F32)],
        compiler_params=_params(1),
        name=f"proj_l{l}",
    )(*xs, mod, g1, w_in, qg, kg, cos_t, sin_t, bq, sgu_w, sgu_b, on)


ACC_ROWS = HEAD_DIM + BF16_ROWS
N_PAIRS = N_HEADS // 2
PAIRS_PER_KV = N_PAIRS // N_KV_HEADS


def _pair_queries(qt, kv_idx, tq):
    qf = qt.astype(F32)
    mine = lax.broadcasted_iota(jnp.int32, (2 * HEAD_DIM, tq), 0) // HEAD_DIM == kv_idx
    qe = qf[:HEAD_DIM]
    qo = qf[HEAD_DIM:]
    rhs = jnp.concatenate(
        [jnp.where(mine, jnp.concatenate([qe, qe], axis=0), 0.0),
         jnp.where(mine, jnp.concatenate([qo, qo], axis=0), 0.0)], axis=1)
    return rhs.astype(BF16)


def _with_ones(vt_c):
    return jnp.concatenate([vt_c, jnp.ones((BF16_ROWS, vt_c.shape[1]), BF16)], axis=0)


def _flash_pairs(rhs_list, chunk_lists, tq, shift=None):
    n_pairs, n_chunks = len(rhs_list), len(chunk_lists[0])
    scores = lambda s, c: jnp.dot(chunk_lists[s][c][0](), rhs_list[s], preferred_element_type=F32)
    m = [jnp.full((1, 2 * tq), NEG_BIG, F32)] * n_pairs
    acc = [jnp.zeros((ACC_ROWS, 2 * tq), F32)] * n_pairs
    st = [scores(s, 0) for s in range(n_pairs)]
    for c in range(n_chunks):
        for s in range(n_pairs):
            if shift is None:
                m_new = jnp.maximum(m[s], jnp.max(st[s], axis=0, keepdims=True))
            st_next = scores(s, c + 1) if c + 1 < n_chunks else None
            if shift is None:
                alpha = jnp.exp2(m[s] - m_new)
                p = jnp.exp2(st[s] - m_new).astype(BF16)
                acc[s] = alpha * acc[s] + jnp.dot(chunk_lists[s][c][1](), p, preferred_element_type=F32)
                m[s] = m_new
            else:
                p = jnp.exp2(st[s] - shift).astype(BF16)
                acc[s] = acc[s] + jnp.dot(chunk_lists[s][c][1](), p, preferred_element_type=F32)
            st[s] = st_next
    outs = []
    for s in range(n_pairs):
        o = acc[s][:HEAD_DIM] * (1.0 / acc[s][HEAD_DIM:HEAD_DIM + 1])
        outs.append(jnp.concatenate([o[:, :tq], o[:, tq:]], axis=0))
    return outs


def _attn_ctx_kernel(qt_ref, k_ref, vt_ref, o_ref):
    tq = SEQ
    rhs_list, chunk_lists = [], []
    for pair in range(N_PAIRS):
        kv_idx = pair // PAIRS_PER_KV
        rhs_list.append(_pair_queries(qt_ref[pair * LANES:(pair + 1) * LANES, :], kv_idx, tq))
        chunk_lists.append([(lambda: k_ref[...],
                             lambda kv_idx=kv_idx: _with_ones(vt_ref[kv_idx * HEAD_DIM:(kv_idx + 1) * HEAD_DIM, :]))])
    for pair, o_t in enumerate(_flash_pairs(rhs_list, chunk_lists, tq)):
        o_ref[:, pair * LANES:(pair + 1) * LANES] = o_t.T.astype(BF16)


def _attn_ctx(l, qt, k, vt):
    halves = TM // SEQ
    return pl.pallas_call(
        _attn_ctx_kernel,
        grid=(BATCH,),
        in_specs=[
            pl.BlockSpec((None, D_ATTN, SEQ), lambda b: (b // halves, 0, b % halves)),
            pl.BlockSpec((SEQ, D_KV), lambda b: (b, 0)),
            pl.BlockSpec((None, D_KV, SEQ), lambda b: (b // halves, 0, b % halves)),
        ],
        out_specs=pl.BlockSpec((SEQ, D_ATTN), lambda b: (b, 0)),
        out_shape=jax.ShapeDtypeStruct((T_CTX, D_ATTN), BF16),
        compiler_params=_params(1),
        name=f"attn_ctx_l{l}",
    )(qt, k, vt)


KC = 512
LAT_PAIRS_PER_STEP = 2


MAX_SHIFT = 60.0
BOUND_SLACK = 1.05


def _score_bound_sq(qg_ref, kg_ref, ck):
    qg = qg_ref[...]
    kg = kg_ref[...]
    gq2 = jnp.max(qg * qg, axis=1, keepdims=True)
    k2 = HEAD_DIM * jnp.max(kg * kg, axis=1, keepdims=True)
    if ck is not None:
        sq = ck * ck
        first = lax.broadcasted_iota(jnp.int32, sq.shape, 1) < HEAD_DIM
        n0 = jnp.sum(jnp.where(first, sq, 0.0), axis=1, keepdims=True)
        n1 = jnp.sum(jnp.where(first, 0.0, sq), axis=1, keepdims=True)
        k2 = jnp.maximum(k2, jnp.max(jnp.maximum(n0, n1), axis=0, keepdims=True))
    return (BOUND_SLACK * Q_SCALE) ** 2 * HEAD_DIM * gq2 * k2


def _run_with_score_bound(run, bound):
    small = bound <= MAX_SHIFT

    @pl.when(small)
    def _():
        run(bound)

    @pl.when(jnp.logical_not(small))
    def _():
        run(None)


def _flash_pair_rolled(rhs, k_ref, vt_ref, v_row, tail_chunks, tq):
    assert KC == TM

    def update(carry, k_c, vt_c):
        m, acc = carry
        st = jnp.dot(k_c, rhs, preferred_element_type=F32)
        m_new = jnp.maximum(m, jnp.max(st, axis=0, keepdims=True))
        p = jnp.exp2(st - m_new).astype(BF16)
        return m_new, jnp.exp2(m - m_new) * acc + jnp.dot(vt_c, p, preferred_element_type=F32)

    def body(c, carry):
        k_c = k_ref[pl.ds(pl.multiple_of(c * KC, KC), KC), :]
        return update(carry, k_c, _with_ones(vt_ref[c, pl.ds(v_row, HEAD_DIM), :]))

    carry = (jnp.full((1, 2 * tq), NEG_BIG, F32), jnp.zeros((ACC_ROWS, 2 * tq), F32))
    carry = lax.fori_loop(0, DEC_SEQ // KC, body, carry)
    for get_k, get_vt in tail_chunks:
        carry = update(carry, get_k(), get_vt())
    acc = carry[1]
    o = acc[:HEAD_DIM] * (1.0 / acc[HEAD_DIM:HEAD_DIM + 1])
    return jnp.concatenate([o[:, :tq], o[:, tq:]], axis=0)


def _attn_lat_kernel(qt_ref, k_ref, vt_ref, ck_ref, cv_ref, qg_ref, kg_ref, w1_ref, w3_ref, w2_ref,
                     o_ref, w1b_ref, w3b_ref, w2b_ref, bound_ref):
    tq = TM

    def run(shift):
        w1b_ref[...] = w1_ref[...].astype(BF16)
        w3b_ref[...] = w3_ref[...].astype(BF16)
        w2b_ref[...] = w2_ref[...].astype(BF16)
        cvt = cv_ref[...].T
        rhs_list, chunk_lists, v_rows = [], [], []
        for j in range(LAT_PAIRS_PER_STEP):
            kv_idx = (pl.program_id(2) * LAT_PAIRS_PER_STEP + j) // PAIRS_PER_KV
            rhs_list.append(_pair_queries(qt_ref[j * LANES:(j + 1) * LANES, :], kv_idx, tq))
            v_row = pl.multiple_of(kv_idx * HEAD_DIM, HEAD_DIM)
            v_rows.append(v_row)

            def latent_chunk(c, v_row=v_row):
                tile, lane0 = (c * KC) // TM, (c * KC) % TM
                return (lambda: k_ref[c * KC:(c + 1) * KC, :],
                        lambda: _with_ones(vt_ref[tile, pl.ds(v_row, HEAD_DIM), lane0:lane0 + KC]))

            def cached_chunk(c, kv_idx=kv_idx):
                return (lambda: ck_ref[c * KC:(c + 1) * KC, :].astype(BF16),
                        lambda: _with_ones(jnp.where(kv_idx == 0, cvt[:HEAD_DIM, c * KC:(c + 1) * KC],
                                                     cvt[HEAD_DIM:, c * KC:(c + 1) * KC]).astype(BF16)))

            chunks = [latent_chunk(c) for c in range(DEC_SEQ // KC)]
            chunks += [cached_chunk(c) for c in range(PAST_LEN // KC)]
            chunk_lists.append(chunks)
        if shift is None:
            outs = [_flash_pair_rolled(rhs_list[j], k_ref, vt_ref, v_rows[j], chunk_lists[j][DEC_SEQ // KC:], tq)
                    for j in range(LAT_PAIRS_PER_STEP)]
        else:
            outs = _flash_pairs(rhs_list, chunk_lists, tq, shift)
        for j, o_t in enumerate(outs):
            o_ref[:, j * LANES:(j + 1) * LANES] = o_t.T.astype(BF16)

    @pl.when(jnp.logical_and(pl.program_id(1) == 0, pl.program_id(2) == 0))
    def _():
        bound_ref[0] = jnp.sqrt(_score_bound_sq(qg_ref, kg_ref, ck_ref[...]))[0, 0]

    _run_with_score_bound(run, bound_ref[0])


def _attn_lat(l, qt, k, vt, cache_k, cache_v, qg, kg, moe_w1, moe_w3, moe_w2):
    n_q = DEC_SEQ // TM
    first = N_CTX_TILES
    ctx_blocks = T_CTX // DEC_SEQ
    width = LAT_PAIRS_PER_STEP * LANES
    n_p = N_PAIRS // LAT_PAIRS_PER_STEP
    per_expert = DEC_BATCH * n_q * n_p // N_EXPERTS
    step = lambda b, i, p: (b * n_q + i) * n_p + p
    w_in_spec = lambda rows, cols: pl.BlockSpec(
        (None, None, rows // per_expert, cols),
        lambda b, i, p: (l, step(b, i, p) // per_expert, step(b, i, p) % per_expert, 0))
    w_out_spec = lambda rows, cols: pl.BlockSpec(
        (None, rows // per_expert, cols),
        lambda b, i, p: (step(b, i, p) // per_expert, step(b, i, p) % per_expert, 0))
    return pl.pallas_call(
        _attn_lat_kernel,
        grid=(DEC_BATCH, n_q, n_p),
        in_specs=[
            pl.BlockSpec((None, width, TM), lambda b, i, p: (first + b * n_q + i, p, 0)),
            pl.BlockSpec((DEC_SEQ, D_KV), lambda b, i, p: (ctx_blocks + b, 0)),
            pl.BlockSpec((n_q, D_KV, TM), lambda b, i, p: (ctx_blocks + b, 0, 0)),
            pl.BlockSpec((None, None, PAST_LEN, D_KV), lambda b, i, p: (b, l, 0, 0)),
            pl.BlockSpec((None, None, PAST_LEN, D_KV), lambda b, i, p: (b, l, 0, 0)),
            pl.BlockSpec((None, 1, D_ATTN), lambda b, i, p: (l, 0, 0)),
            pl.BlockSpec((None, 1, D_KV), lambda b, i, p: (l, 0, 0)),
            w_in_spec(D_MODEL, D_EXPERT),
            w_in_spec(D_MODEL, D_EXPERT),
            w_in_spec(D_EXPERT, D_MODEL),
        ],
        out_specs=[
            pl.BlockSpec((TM, width), lambda b, i, p: (b * n_q + i, p)),
            w_out_spec(D_MODEL, D_EXPERT),
            w_out_spec(D_MODEL, D_EXPERT),
            w_out_spec(D_EXPERT, D_MODEL),
        ],
        out_shape=[
            jax.ShapeDtypeStruct((T_LAT, D_ATTN), BF16),
            jax.ShapeDtypeStruct((N_EXPERTS, D_MODEL, D_EXPERT), BF16),
            jax.ShapeDtypeStruct((N_EXPERTS, D_MODEL, D_EXPERT), BF16),
            jax.ShapeDtypeStruct((N_EXPERTS, D_EXPERT, D_MODEL), BF16),
        ],
        scratch_shapes=[pltpu.SMEM((1,), F32)],
        compiler_params=_params(3),
        name=f"attn_lat_l{l}",
    )(qt, k, vt, cache_k, cache_v, qg, kg, moe_w1, moe_w3, moe_w2)


def _first_max_index(vals, valid, rowf):
    masked = jnp.where(valid, vals, -jnp.inf)
    mx = jnp.max(masked, axis=0, keepdims=True)
    idx = jnp.min(jnp.where(masked == mx, rowf, float(SUBLANES)), axis=0, keepdims=True)
    return mx, idx


def _merge_kernel(*refs, split_x):
    n_x = 2 if split_x else 1
    x_refs, refs = refs[:n_x], refs[n_x:]
    (actx_ref, alat_ref, sgu_ref, cbz_ref, prev_ref, next_ref, cw_ref, on_ref,
     wo_ref, mod_ref, g2_ref, wr_ref, wrhi_ref, rb_ref, x1_ref, h2_ref, route_ref, gate4_ref, cnt_ref) = refs
    cur = pl.program_id(0)
    is_ctx = cur < N_CTX_TILES
    mrow = _mod_row_of_tile(cur)
    on = on_ref[...]

    cbz = cbz_ref[...]
    cb = cbz[:, :D_CONV]
    z = cbz[:, D_CONV:]
    seq_mask = jnp.where(is_ctx, SEQ - 1, DEC_SEQ - 1)
    row = lax.broadcasted_iota(jnp.int32, (TM, D_CONV), 0)
    pos = (cur * TM + row) & seq_mask
    z_prev = jnp.where(row == 0, prev_ref[SUBLANES - 1:SUBLANES, D_CONV:], pltpu.roll(z, 1, 0))
    z_next = jnp.where(row == TM - 1, next_ref[0:1, D_CONV:], pltpu.roll(z, TM - 1, 0))
    z_prev = jnp.where(pos == 0, 0.0, z_prev)
    z_next = jnp.where(pos == seq_mask, 0.0, z_next)
    cw = cw_ref[...]
    conv = cb * (z_prev * cw[0:1, :] + z * cw[1:2, :] + z_next * cw[2:3, :])
    conv_n = _rms(conv, on[:, D_ATTN + D_SGU:]).astype(BF16)

    attn = jnp.where(is_ctx, actx_ref[...], alat_ref[...]).astype(F32)
    attn_n = _rms(attn, on[:, :D_ATTN]).astype(BF16)
    merged = jnp.concatenate([attn_n, sgu_ref[...], conv_n], axis=1)
    x = jnp.where(is_ctx, x_refs[0][...], x_refs[1][...]) if split_x else x_refs[0][...]
    x1 = x + _mod_part(mod_ref, 2, mrow) * jnp.dot(merged, wo_ref[...], preferred_element_type=F32)
    x1_ref[...] = x1
    _route_tile(x1, mrow, mod_ref, g2_ref, wr_ref, wrhi_ref, rb_ref, h2_ref, route_ref, gate4_ref, cnt_ref)


def _route_tile(x1, mrow, mod_ref, g2_ref, wr_ref, wrhi_ref, rb_ref, h2_ref, route_ref, gate4_ref, cnt_ref):
    n = x1.shape[0]
    h2 = _rms(x1, g2_ref[...] * (1.0 + _mod_part(mod_ref, 4, mrow))) + _mod_part(mod_ref, 3, mrow)
    hb = h2.astype(BF16)
    h2_ref[...] = hb

    h_lo = (h2 - hb.astype(F32)).astype(BF16)
    a = jnp.dot(hb, wr_ref[...], preferred_element_type=F32)
    b = jnp.dot(h_lo, wrhi_ref[...], preferred_element_type=F32)
    logits = a[:, :ROUTER_COLS] + a[:, ROUTER_COLS:] + b + rb_ref[...]
    lt = logits.T

    rowi = lax.broadcasted_iota(jnp.int32, (SUBLANES, n), 0).astype(F32)
    valid = rowi < float(N_EXPERT_GROUPS)
    g_log = lt[0:SUBLANES]
    g_max, g_idx = _first_max_index(g_log, valid, rowi)
    p_g = 1.0 / jnp.sum(jnp.where(valid, jnp.exp(g_log - g_max), 0.0), axis=0, keepdims=True)
    e_log = jnp.zeros((SUBLANES, n), F32)
    for g in range(N_EXPERT_GROUPS):
        e_log = jnp.where(g_idx == g, lt[(g + 1) * SUBLANES:(g + 2) * SUBLANES], e_log)
    e_max, i1 = _first_max_index(e_log, valid, rowi)
    e_exp = jnp.where(valid, jnp.exp(e_log - e_max), 0.0)
    e_prob = e_exp / jnp.sum(e_exp, axis=0, keepdims=True)
    v1 = jnp.max(e_prob, axis=0, keepdims=True)
    rest = jnp.logical_and(valid, rowi != i1)
    v2, i2 = _first_max_index(e_prob, rest, rowi)
    denom = v1 + v2
    w_sel = p_g * jnp.where(rowi == i1, v1 / denom, jnp.where(rowi == i2, v2 / denom, 0.0))
    route_ref[...] = jnp.broadcast_to(g_idx, (SUBLANES, n))
    pad = jnp.zeros((LANES - SUBLANES, n), F32)
    gate4_ref[...] = jnp.concatenate([w_sel, pad], axis=0).T
    counts = jnp.sum(jnp.where(rowi == g_idx, 1.0, 0.0), axis=1, keepdims=True)
    cnt_ref[...] = jnp.broadcast_to(counts, (SUBLANES, LANES)).astype(jnp.int32)


def _merge(l, xs, attn_ctx, attn_lat, sgu_n, cbz, conv_w, on, w_out, mod, g2, wr_cat, wr_hi, rb):
    lay = lambda *rest: (lambda i: (l,) + rest)
    rows8 = TM // SUBLANES
    return pl.pallas_call(
        functools.partial(_merge_kernel, split_x=len(xs) == 2),
        grid=(N_TILES,),
        in_specs=_x_specs(xs) + [
            pl.BlockSpec((TM, D_ATTN), lambda i: (_ctx_tile(i), 0)),
            pl.BlockSpec((TM, D_ATTN), lambda i: (_lat_tile(i), 0)),
            pl.BlockSpec((TM, D_SGU), lambda i: (i, 0)),
            pl.BlockSpec((TM, 2 * D_CONV), lambda i: (i, 0)),
            pl.BlockSpec((SUBLANES, 2 * D_CONV), lambda i: (jnp.maximum(i * rows8 - 1, 0), 0)),
            pl.BlockSpec((SUBLANES, 2 * D_CONV),
                         lambda i: (jnp.minimum((i + 1) * rows8, T_ALL // SUBLANES - 1), 0)),
            pl.BlockSpec((None, 3, D_CONV), lay(0, 0)),
            pl.BlockSpec((None, 1, D_MIX), lay(0, 0)),
            pl.BlockSpec((None, D_MIX, D_MODEL), lay(0, 0)),
            _mod_spec(l),
            pl.BlockSpec((None, 1, D_MODEL), lay(0, 0)),
            pl.BlockSpec((None, D_MODEL, 2 * ROUTER_COLS), lay(0, 0)),
            pl.BlockSpec((None, D_MODEL, ROUTER_COLS), lay(0, 0)),
            pl.BlockSpec((None, 1, ROUTER_COLS), lay(0, 0)),
        ],
        out_specs=[
            pl.BlockSpec((TM, D_MODEL), lambda i: (i, 0)),
            pl.BlockSpec((TM, D_MODEL), lambda i: (i, 0)),
            pl.BlockSpec((None, SUBLANES, TM), lambda i: (i, 0, 0)),
            pl.BlockSpec((TM, LANES), lambda i: (i, 0)),
            pl.BlockSpec((None, SUBLANES, LANES), lambda i: (i, 0, 0)),
        ],
        out_shape=[
            jax.ShapeDtypeStruct((T_ALL, D_MODEL), F32),
            jax.ShapeDtypeStruct((T_ALL, D_MODEL), BF16),
            jax.ShapeDtypeStruct((N_TILES, SUBLANES, TM), F32),
            jax.ShapeDtypeStruct((T_ALL, LANES), F32),
            jax.ShapeDtypeStruct((N_TILES, SUBLANES, LANES), jnp.int32),
        ],
        compiler_params=_params(1),
        name=f"merge_l{l}",
    )(*xs, attn_ctx, attn_lat, sgu_n, cbz, cbz, cbz, conv_w, on, w_out, mod, g2, wr_cat, wr_hi, rb)


RB = 144
SORT_ROWS = TM + LANES
SORT_ROWS_PAD = -(-(SORT_ROWS + RB) // BF16_ROWS) * BF16_ROWS
MOE_VMEM_LIMIT = 56 * 1024 * 1024


def _split3(x):
    hi = x.astype(BF16)
    r1 = x - hi.astype(F32)
    mid = r1.astype(BF16)
    lo = (r1 - mid.astype(F32)).astype(BF16)
    return hi, mid, lo


def _moe_kernel(cnt_ref, h_ref, route_ref, gate4_ref, tri_ref, w1_ref, w3_ref, w2_ref, x1_ref, mod_ref, fn_ref,
                *refs, final):
    out_refs, (xs_ref, gs_ref, zs_ref) = refs[:-3], refs[-3:]
    i = pl.program_id(0)
    gate2 = _mod_part(mod_ref, 5, _mod_row_of_tile(i))
    counts = [cnt_ref[i, g] for g in range(N_EXPERT_GROUPS)]
    starts = [jnp.int32(0)]
    for g in range(N_EXPERT_GROUPS - 1):
        starts.append(starts[-1] + (counts[g] + (BF16_ROWS - 1)) // BF16_ROWS * BF16_ROWS)

    g_idx = route_ref[0:1, :]
    rowi = lax.broadcasted_iota(jnp.int32, (SUBLANES, TM), 0).astype(F32)
    onehot = jnp.where(rowi == g_idx, 1.0, 0.0)
    incl = jnp.dot(onehot.astype(BF16), tri_ref[...], preferred_element_type=F32)
    pos = jnp.sum(onehot * incl, axis=0, keepdims=True) - 1.0
    for g in range(1, N_EXPERT_GROUPS):
        pos = pos + jnp.where(g_idx == float(g), starts[g].astype(F32), 0.0)
    sub = lax.broadcasted_iota(jnp.int32, (SORT_ROWS, TM), 0).astype(F32)
    perm = jnp.where(sub == pos, 1.0, 0.0).astype(BF16)
    pos_col = jnp.broadcast_to(pos, (LANES, TM)).T
    lane = lax.broadcasted_iota(jnp.int32, (TM, SORT_ROWS), 1).astype(F32)
    perm_t = jnp.where(lane == jnp.concatenate([pos_col] * (SORT_ROWS // LANES), axis=1), 1.0, 0.0).astype(BF16)

    xs_ref[:SORT_ROWS, :] = jnp.dot(perm, h_ref[...], preferred_element_type=F32).astype(BF16)
    xs_ref[SORT_ROWS:, :] = jnp.zeros((SORT_ROWS_PAD - SORT_ROWS, D_MODEL), BF16)
    hi, mid, lo = _split3(gate4_ref[...])
    packed = (hi.astype(F32) + pltpu.roll(mid.astype(F32), EXPERTS_PER_GROUP, 1)
              + pltpu.roll(lo.astype(F32), 2 * EXPERTS_PER_GROUP, 1)).astype(BF16)
    gsp = jnp.dot(perm, packed, preferred_element_type=F32)
    gs_ref[:SORT_ROWS, :] = (gsp + pltpu.roll(gsp, LANES - EXPERTS_PER_GROUP, 1)
                             + pltpu.roll(gsp, LANES - 2 * EXPERTS_PER_GROUP, 1))
    gs_ref[SORT_ROWS:, :] = jnp.zeros((SORT_ROWS_PAD - SORT_ROWS, LANES), F32)
    zs_ref[...] = jnp.zeros((SORT_ROWS_PAD, D_MODEL), BF16)

    def block_ffn(g, blk):
        row0 = pl.multiple_of(starts[g] + blk * RB, BF16_ROWS)
        xb = xs_ref[pl.ds(row0, RB), :]
        gb = gs_ref[pl.ds(row0, RB), :]
        cols = []
        for e in range(EXPERTS_PER_GROUP):
            a = jnp.dot(xb, w1_ref[g * EXPERTS_PER_GROUP + e], preferred_element_type=F32)
            b = jnp.dot(xb, w3_ref[g * EXPERTS_PER_GROUP + e], preferred_element_type=F32)
            cols.append((a / (1.0 + jnp.exp(-a)) * b * gb[:, e:e + 1]).astype(BF16))
        z = jnp.dot(jnp.concatenate(cols, axis=1), w2_ref[g], preferred_element_type=F32)
        zs_ref[pl.ds(row0, RB), :] = z.astype(BF16)

    for g in range(N_EXPERT_GROUPS):
        def extra_block(blk, carry, g=g):
            block_ffn(g, blk)
            return carry

        lax.fori_loop(1, (counts[g] + (RB - 1)) // RB, extra_block, 0)
    for g in range(N_EXPERT_GROUPS):
        block_ffn(g, 0)

    y = jnp.dot(perm_t, zs_ref[:SORT_ROWS, :], preferred_element_type=F32)
    out = x1_ref[...] + gate2 * y

    if final:
        octx_ref, olat_ref = out_refs
        out = _rms(out, fn_ref[...])
        olat_ref[...] = out

        @pl.when(i < N_CTX_TILES)
        def _():
            octx_ref[...] = out
    else:
        out_refs[0][...] = out


def _moe(l, counts, h2, route, gate4, tri, w1, w3, w2, x1, mod, final_norm, final):
    tile = pl.BlockSpec((TM, D_MODEL), lambda i, c: (i, 0))
    resident = pl.Buffered(1)
    if final:
        out_specs = [pl.BlockSpec((TM, D_MODEL), lambda i, c: (_ctx_tile(i), 0)),
                     pl.BlockSpec((TM, D_MODEL), lambda i, c: (_lat_tile(i), 0))]
        out_shape = [jax.ShapeDtypeStruct((T_CTX, D_MODEL), F32), jax.ShapeDtypeStruct((T_LAT, D_MODEL), F32)]
    else:
        out_specs = tile
        out_shape = jax.ShapeDtypeStruct((T_ALL, D_MODEL), F32)
    grid_spec = pltpu.PrefetchScalarGridSpec(
        num_scalar_prefetch=1,
        grid=(N_TILES,),
        in_specs=[
            tile,
            pl.BlockSpec((None, SUBLANES, TM), lambda i, c: (i, 0, 0)),
            pl.BlockSpec((TM, LANES), lambda i, c: (i, 0)),
            pl.BlockSpec((TM, TM), lambda i, c: (0, 0), pipeline_mode=resident),
            pl.BlockSpec((N_EXPERTS, D_MODEL, D_EXPERT), lambda i, c: (0, 0, 0), pipeline_mode=resident),
            pl.BlockSpec((N_EXPERTS, D_MODEL, D_EXPERT), lambda i, c: (0, 0, 0), pipeline_mode=resident),
            pl.BlockSpec((N_EXPERT_GROUPS, D_GROUP_HID, D_MODEL), lambda i, c: (0, 0, 0), pipeline_mode=resident),
            tile,
            pl.BlockSpec((None, N_MOD_PARTS, N_MOD_ROWS, D_MODEL), lambda i, c: (l, 0, 0, 0)),
            pl.BlockSpec((1, D_MODEL), lambda i, c: (0, 0)),
        ],
        out_specs=out_specs,
        scratch_shapes=[pltpu.VMEM((SORT_ROWS_PAD, D_MODEL), BF16),
                        pltpu.VMEM((SORT_ROWS_PAD, LANES), F32),
                        pltpu.VMEM((SORT_ROWS_PAD, D_MODEL), BF16)],
    )
    return pl.pallas_call(
        functools.partial(_moe_kernel, final=final),
        grid_spec=grid_spec,
        out_shape=out_shape,
        compiler_params=pltpu.CompilerParams(dimension_semantics=("arbitrary",),
                                             vmem_limit_bytes=MOE_VMEM_LIMIT),
        name=f"moe_l{l}",
    )(counts, h2, route, gate4, tri, w1, w3, w2, x1, mod, final_norm)


def _rope_tables():
    rows = DEC_SEQ // GRID_W
    row = np.repeat(np.arange(rows, dtype=np.float32), GRID_W)
    col = np.tile(np.arange(GRID_W, dtype=np.float32), rows)
    half = ROPE_AXIS_DIM // 2
    inv_freq = (1.0 / (np.float32(ROPE_THETA) ** (np.arange(half, dtype=np.float32) * np.float32(2.0)
                                                  / np.float32(ROPE_AXIS_DIM)))).astype(np.float32)
    ar = row[:, None] * inv_freq
    ac = col[:, None] * inv_freq
    cos64 = np.concatenate([np.cos(ar), np.cos(ar), np.cos(ac), np.cos(ac)], axis=1)
    sin64 = np.concatenate([-np.sin(ar), np.sin(ar), -np.sin(ac), np.sin(ac)], axis=1)
    cos_t = np.concatenate([np.ones((TM, LANES), np.float32), np.tile(cos64, (1, 2))], axis=0)
    sin_t = np.concatenate([np.zeros((TM, LANES), np.float32), np.tile(sin64, (1, 2))], axis=0)
    return jnp.asarray(cos_t, F32), jnp.asarray(sin_t, F32)


def _router_weights(router_g_w, router_g_b, router_e_w, router_e_b):
    zw = jnp.zeros((DEPTH, D_MODEL, SUBLANES - N_EXPERT_GROUPS), F32)
    zb = jnp.zeros((DEPTH, SUBLANES - N_EXPERT_GROUPS), F32)
    w_cols, b_cols = [router_g_w, zw], [router_g_b, zb]
    for g in range(N_EXPERT_GROUPS):
        w_cols += [router_e_w[:, g], zw]
        b_cols += [router_e_b[:, g], zb]
    used = (1 + N_EXPERT_GROUPS) * SUBLANES
    w_cols.append(jnp.zeros((DEPTH, D_MODEL, ROUTER_COLS - used), F32))
    b_cols.append(jnp.zeros((DEPTH, ROUTER_COLS - used), F32))
    w = jnp.concatenate(w_cols, axis=2)
    b = jnp.concatenate(b_cols, axis=1).reshape(DEPTH, 1, ROUTER_COLS)
    w_hi = w.astype(BF16)
    w_lo = (w - w_hi.astype(F32)).astype(BF16)
    return jnp.concatenate([w_hi, w_lo], axis=2), w_hi, b


def kernel(x_prompt, x_sample, cache_k, cache_v, c, c_ctx, w_ada, b_ada, norm1, w_in, q_norm, k_norm,
           sgu_w, sgu_b, conv_w, out_norm, w_out, norm2, router_g_w, router_g_b, router_e_w,
           router_e_b, moe_w1, moe_w3, moe_w2, final_norm):
    xs = (x_prompt.reshape(T_CTX, D_MODEL), x_sample.reshape(T_LAT, D_MODEL))

    cvecs = jnp.concatenate([c_ctx[None, :], c, jnp.zeros((N_MOD_ROWS - 1 - DEC_BATCH, D_MODEL), F32)], axis=0)
    mod = _modulation(cvecs.T, w_ada, b_ada)

    w_in_b = w_in.astype(BF16)
    w_out_b = w_out.astype(BF16)
    g1 = norm1.reshape(DEPTH, 1, D_MODEL)
    g2 = norm2.reshape(DEPTH, 1, D_MODEL)
    on = out_norm.reshape(DEPTH, 1, D_MIX)
    qg = jnp.tile(q_norm, (1, N_HEADS)).reshape(DEPTH, 1, D_ATTN)
    kg = jnp.tile(k_norm, (1, N_KV_HEADS)).reshape(DEPTH, 1, D_KV)
    cos_t, sin_t = _rope_tables()
    bq = jnp.asarray(np.kron(np.eye(N_HEADS), np.full((HEAD_DIM, HEAD_DIM), 1.0 / HEAD_DIM)), BF16)
    sgu_w_pairs = sgu_w.astype(BF16).reshape(DEPTH, N_SGU_GROUPS // 2, 2, CHUNK, CHUNK)
    sgu_w_pairs = jnp.concatenate([sgu_w_pairs[:, :, 0], sgu_w_pairs[:, :, 1]], axis=-1)
    sgu_bias = jnp.repeat(jnp.swapaxes(sgu_b, 1, 2), SGU_GROUP_DIM, axis=2)
    wr_cat, wr_hi, rb = _router_weights(router_g_w, router_g_b, router_e_w, router_e_b)
    fn = final_norm.reshape(1, D_MODEL)
    tri = jnp.asarray(np.triu(np.ones((TM, TM), np.float32)), BF16)
    ck = cache_k.reshape(DEC_BATCH, DEPTH, PAST_LEN, D_KV)
    cv = cache_v.reshape(DEC_BATCH, DEPTH, PAST_LEN, D_KV)

    ctx_ks, ctx_vs = [], []
    for l in range(DEPTH):
        qt, k, vt, k32, v32, sgu_n, cbz = _proj(l, xs, mod, g1, w_in_b, qg, kg, cos_t, sin_t, bq,
                                                sgu_w_pairs, sgu_bias, on)
        attn_ctx = _attn_ctx(l, qt, k, vt)
        attn_lat, w1, w3, w2 = _attn_lat(l, qt, k, vt, ck, cv, qg, kg, moe_w1, moe_w3, moe_w2)
        w2 = w2.reshape(N_EXPERT_GROUPS, D_GROUP_HID, D_MODEL)
        x1, h2, route, gate4, cnt = _merge(l, xs, attn_ctx, attn_lat, sgu_n, cbz, conv_w, on, w_out_b, mod, g2,
                                           wr_cat, wr_hi, rb)
        counts = cnt[:, :N_EXPERT_GROUPS, 0]
        out = _moe(l, counts, h2, route, gate4, tri, w1, w3, w2, x1, mod, fn, final=(l == DEPTH - 1))
        xs = (out,)
        ctx_ks.append(k32.reshape(BATCH, SEQ, N_KV_HEADS, HEAD_DIM))
        ctx_vs.append(v32.reshape(BATCH, SEQ, N_KV_HEADS, HEAD_DIM))

    y_prompt, y_sample = out
    return (y_prompt.reshape(BATCH, SEQ, D_MODEL), y_sample.reshape(DEC_BATCH, DEC_SEQ, D_MODEL),
            jnp.stack(ctx_ks, axis=1), jnp.stack(ctx_vs, axis=1))
```

```python
import functools
import math

import jax
import jax.numpy as jnp
import numpy as np
from jax import lax
from jax.experimental import pallas as pl
from jax.experimental.pallas import tpu as pltpu

D_MODEL = 1024
BATCH = 16
SEQ = 256
DEPTH = 2
DEC_BATCH = 2
DEC_SEQ = 4096
PAST_LEN = 512
GRID_W = 64
N_HEADS = 8
N_KV_HEADS = 2
HEAD_DIM = 64
ROPE_AXIS_DIM = HEAD_DIM // 2
ROPE_THETA = 10000.0
D_ATTN = N_HEADS * HEAD_DIM
D_KV = N_KV_HEADS * HEAD_DIM
CHUNK = 128
N_SGU_GROUPS = 4
SGU_GROUP_DIM = 64
D_SGU = N_SGU_GROUPS * SGU_GROUP_DIM
D_CONV = 256
D_MIX = D_ATTN + D_SGU + D_CONV
D_IN = D_ATTN + 2 * D_KV + 2 * D_SGU + 3 * D_CONV
N_EXPERT_GROUPS = 4
EXPERTS_PER_GROUP = 4
N_EXPERTS = N_EXPERT_GROUPS * EXPERTS_PER_GROUP
D_EXPERT = 256
D_GROUP_HID = EXPERTS_PER_GROUP * D_EXPERT
EPS = 1e-6

T_CTX = BATCH * SEQ
T_LAT = DEC_BATCH * DEC_SEQ
T_ALL = T_CTX + T_LAT
TM = 512
N_TILES = T_ALL // TM
N_CTX_TILES = T_CTX // TM
LAT_TILES_PER_BATCH = DEC_SEQ // TM
LANES = 128
SUBLANES = 8
BF16_ROWS = 16
N_MOD_ROWS = 8
N_MOD_PARTS = 6
ROUTER_COLS = 128
VMEM_LIMIT = 48 * 1024 * 1024

OFF_Q = 0
OFF_KV = D_ATTN
OFF_SGU = OFF_KV + 2 * D_KV
OFF_CONV = OFF_SGU + 2 * D_SGU

BF16 = jnp.bfloat16
F32 = jnp.float32
NEG_BIG = -1e30
Q_SCALE = HEAD_DIM ** -0.5 * math.log2(math.e)


def _params(n_grid_dims):
    return pltpu.CompilerParams(
        dimension_semantics=("arbitrary",) * n_grid_dims,
        vmem_limit_bytes=VMEM_LIMIT,
    )


def _mod_row_of_tile(i, tile=TM):
    n_ctx = T_CTX // tile
    return jnp.where(i < n_ctx, 0, 1 + (i - n_ctx) // (DEC_SEQ // tile))


def _mod_part(mod_ref, part, row):
    return mod_ref[part, pl.ds(row, 1), :]


def _rope_block_of_tile(i):
    return jnp.where(i < N_CTX_TILES, 0, 1 + (i - N_CTX_TILES) % LAT_TILES_PER_BATCH)


def _rms(x, gain):
    ms = jnp.mean(x * x, axis=-1, keepdims=True)
    return x * lax.rsqrt(ms + EPS) * gain


def _ctx_tile(i):
    return jnp.minimum(i, N_CTX_TILES - 1)


def _lat_tile(i):
    return jnp.maximum(i - N_CTX_TILES, 0)


MOD_PARTS_PER_STEP = 2
MOD_TN = MOD_PARTS_PER_STEP * D_MODEL


def _mod_kernel(ct_ref, w_ref, b_ref, o_ref):
    c = ct_ref[...]
    s = c / (1.0 + jnp.exp(-c))
    row = lax.broadcasted_iota(jnp.int32, (N_MOD_ROWS, D_MODEL), 0)
    for part in range(MOD_PARTS_PER_STEP):
        cols = slice(part * D_MODEL, (part + 1) * D_MODEL)
        w = w_ref[:, cols]
        out = jnp.zeros((N_MOD_ROWS, D_MODEL), F32)
        for r in range(1 + DEC_BATCH):
            acc = jnp.sum(w * s[:, r:r + 1], axis=0, keepdims=True)
            out = jnp.where(row == r, acc, out)
        o_ref[part] = out + b_ref[:, cols]


def _modulation(cvecs_t, w_ada, b_ada):
    n_col = N_MOD_PARTS * D_MODEL
    return pl.pallas_call(
        _mod_kernel,
        grid=(DEPTH, n_col // MOD_TN),
        in_specs=[
            pl.BlockSpec((D_MODEL, N_MOD_ROWS), lambda l, j: (0, 0)),
            pl.BlockSpec((None, D_MODEL, MOD_TN), lambda l, j: (l, 0, j)),
            pl.BlockSpec((None, 1, MOD_TN), lambda l, j: (l, 0, j)),
        ],
        out_specs=pl.BlockSpec((None, MOD_PARTS_PER_STEP, N_MOD_ROWS, D_MODEL), lambda l, j: (l, j, 0, 0)),
        out_shape=jax.ShapeDtypeStruct((DEPTH, N_MOD_PARTS, N_MOD_ROWS, D_MODEL), F32),
        compiler_params=_params(2),
        name="adaln_modulation",
    )(cvecs_t, w_ada, b_ada.reshape(DEPTH, 1, n_col))


def _mod_spec(l):
    return pl.BlockSpec((None, N_MOD_PARTS, N_MOD_ROWS, D_MODEL), lambda *idx: (l, 0, 0, 0))


def _swap16(x):
    lane = lax.broadcasted_iota(jnp.int32, x.shape, 1)
    up = pltpu.roll(x, 16, 1)
    down = pltpu.roll(x, LANES - 16, 1)
    return jnp.where((lane & 16) != 0, up, down)


def _rope(x, cos_t, sin_t):
    cols = []
    for j in range(x.shape[1] // LANES):
        xc = x[:, j * LANES:(j + 1) * LANES]
        cols.append(xc * cos_t + _swap16(xc) * sin_t)
    return cols[0] if len(cols) == 1 else jnp.concatenate(cols, axis=1)


def _proj_kernel(*refs, split_x):
    n_x = 2 if split_x else 1
    x_refs, refs, (scr_even, scr_odd) = refs[:n_x], refs[n_x:-2], refs[-2:]
    step = functools.partial(_proj_step, x_refs, refs, split_x)
    i = pl.program_id(0)

    @pl.when(i == 0)
    def _():
        scr_odd[...] = jnp.zeros((TM, D_IN), F32)

    last = i == N_TILES

    @pl.when(jnp.logical_and(i % 2 == 0, jnp.logical_not(last)))
    def _():
        step(scr_even, scr_odd)

    @pl.when(jnp.logical_and(i % 2 == 1, jnp.logical_not(last)))
    def _():
        step(scr_odd, scr_even)

    @pl.when(last)
    def _():
        step(None, scr_even if N_TILES % 2 == 1 else scr_odd)


def _proj_step(x_refs, refs, split_x, new_ref, proj):
    (mod_ref, g1_ref, w_ref, qg_ref, kg_ref, cos_ref, sin_ref, bq_ref, sw_ref, sb_ref, on_ref,
     qt_ref, k_ref, vt_ref, k32_ref, v32_ref, sgu_ref, cbz_ref) = refs
    i = pl.program_id(0)
    cur = _stage1_tile(i)

    is_ctx = i <= N_CTX_TILES
    cos_t = cos_ref[...]
    sin_t = sin_ref[...]

    q = proj[:, OFF_Q:OFF_Q + D_ATTN]
    q2 = (q * q).astype(BF16)
    half = D_ATTN // 2
    qms = jnp.concatenate(
        [jnp.dot(q2[:, :half], bq_ref[:half, :half], preferred_element_type=F32),
         jnp.dot(q2[:, half:], bq_ref[half:, half:], preferred_element_type=F32)], axis=1)
    qn = q * lax.rsqrt(qms + EPS) * qg_ref[...]
    qr = _rope(qn, cos_t, sin_t) * Q_SCALE
    qt_ref[...] = qr.T.astype(BF16)

    k = proj[:, OFF_KV:OFF_KV + D_KV]
    v = proj[:, OFF_KV + D_KV:OFF_KV + 2 * D_KV]
    kms = jnp.dot((k * k).astype(BF16), bq_ref[:D_KV, :D_KV], preferred_element_type=F32)
    kn = k * lax.rsqrt(kms + EPS) * kg_ref[...]
    k_ref[...] = _rope(kn, cos_t, sin_t).astype(BF16)
    vt_ref[...] = v.T.astype(BF16)

    su = proj[:, OFF_SGU:OFF_SGU + D_SGU]
    sv = proj[:, OFF_SGU + D_SGU:OFF_SGU + 2 * D_SGU]
    n_chunks = TM // CHUNK
    sv_wide = jnp.concatenate([sv[n * CHUNK:(n + 1) * CHUNK, :] for n in range(n_chunks)], axis=1)
    grp = lax.broadcasted_iota(jnp.int32, (CHUNK, n_chunks * D_SGU), 1) // SGU_GROUP_DIM % N_SGU_GROUPS
    mixed_wide = jnp.zeros((CHUNK, n_chunks * D_SGU), F32)
    for pair in range(N_SGU_GROUPS // 2):
        rhs = jnp.concatenate(
            [jnp.where(grp == 2 * pair, sv_wide, 0.0), jnp.where(grp == 2 * pair + 1, sv_wide, 0.0)],
            axis=0).astype(BF16)
        mixed_wide = mixed_wide + jnp.dot(sw_ref[pair], rhs, preferred_element_type=F32)
    on_sgu = on_ref[:, D_ATTN:D_ATTN + D_SGU]
    for n in range(n_chunks):
        sgu = su[n * CHUNK:(n + 1) * CHUNK, :] * (mixed_wide[:, n * D_SGU:(n + 1) * D_SGU] + sb_ref[...])
        sgu_ref[n * CHUNK:(n + 1) * CHUNK, :] = _rms(sgu, on_sgu).astype(BF16)

    cbz_ref[:, :D_CONV] = proj[:, OFF_CONV:OFF_CONV + D_CONV]
    cbz_ref[:, D_CONV:] = (proj[:, OFF_CONV + D_CONV:OFF_CONV + 2 * D_CONV]
                           * proj[:, OFF_CONV + 2 * D_CONV:OFF_CONV + 3 * D_CONV])

    if new_ref is not None:
        x = jnp.where(cur < N_CTX_TILES, x_refs[0][...], x_refs[1][...]) if split_x else x_refs[0][...]
        mrow = _mod_row_of_tile(cur)
        h = _rms(x, g1_ref[...] * (1.0 + _mod_part(mod_ref, 1, mrow))) + _mod_part(mod_ref, 0, mrow)
        new_ref[...] = jnp.dot(h.astype(BF16), w_ref[...], preferred_element_type=F32)

    @pl.when(is_ctx)
    def _():
        k32_ref[...] = kn
        v32_ref[...] = v


def _x_specs(xs, tile_of_step=lambda i: i):
    if len(xs) == 1:
        return [pl.BlockSpec((TM, D_MODEL), lambda i: (tile_of_step(i), 0))]
    return [pl.BlockSpec((TM, D_MODEL), lambda i: (_ctx_tile(tile_of_step(i)), 0)),
            pl.BlockSpec((TM, D_MODEL), lambda i: (_lat_tile(tile_of_step(i)), 0))]


def _stage1_tile(i):
    return jnp.minimum(i, N_TILES - 1)


def _stage2_tile(i):
    return jnp.maximum(i - 1, 0)


def _proj(l, xs, mod, g1, w_in, qg, kg, cos_t, sin_t, bq, sgu_w, sgu_b, on):
    lay = lambda *rest: (lambda i: (l,) + rest)
    s2 = _stage2_tile
    return pl.pallas_call(
        functools.partial(_proj_kernel, split_x=len(xs) == 2),
        grid=(N_TILES + 1,),
        in_specs=_x_specs(xs, _stage1_tile) + [
            _mod_spec(l),
            pl.BlockSpec((None, 1, D_MODEL), lay(0, 0)),
            pl.BlockSpec((None, D_MODEL, D_IN), lay(0, 0)),
            pl.BlockSpec((None, 1, D_ATTN), lay(0, 0)),
            pl.BlockSpec((None, 1, D_KV), lay(0, 0)),
            pl.BlockSpec((TM, LANES), lambda i: (_rope_block_of_tile(s2(i)), 0)),
            pl.BlockSpec((TM, LANES), lambda i: (_rope_block_of_tile(s2(i)), 0)),
            pl.BlockSpec((D_ATTN, D_ATTN), lambda i: (0, 0)),
            pl.BlockSpec((None, N_SGU_GROUPS // 2, CHUNK, 2 * CHUNK), lay(0, 0, 0)),
            pl.BlockSpec((None, CHUNK, D_SGU), lay(0, 0)),
            pl.BlockSpec((None, 1, D_MIX), lay(0, 0)),
        ],
        out_specs=[
            pl.BlockSpec((None, D_ATTN, TM), lambda i: (s2(i), 0, 0)),
            pl.BlockSpec((TM, D_KV), lambda i: (s2(i), 0)),
            pl.BlockSpec((None, D_KV, TM), lambda i: (s2(i), 0, 0)),
            pl.BlockSpec((TM, D_KV), lambda i: (_ctx_tile(s2(i)), 0)),
            pl.BlockSpec((TM, D_KV), lambda i: (_ctx_tile(s2(i)), 0)),
            pl.BlockSpec((TM, D_SGU), lambda i: (s2(i), 0)),
            pl.BlockSpec((TM, 2 * D_CONV), lambda i: (s2(i), 0)),
        ],
        out_shape=[
            jax.ShapeDtypeStruct((N_TILES, D_ATTN, TM), BF16),
            jax.ShapeDtypeStruct((T_ALL, D_KV), BF16),
            jax.ShapeDtypeStruct((N_TILES, D_KV, TM), BF16),
            jax.ShapeDtypeStruct((T_CTX, D_KV), F32),
            jax.ShapeDtypeStruct((T_CTX, D_KV), F32),
            jax.ShapeDtypeStruct((T_ALL, D_SGU), BF16),
            jax.ShapeDtypeStruct((T_ALL, 2 * D_CONV), F32),
        ],
        scratch_shapes=[pltpu.VMEM((TM, D_IN), F32), pltpu.VMEM((TM, D_IN), F32)],
        compiler_params=_params(1),
        name=f"proj_l{l}",
    )(*xs, mod, g1, w_in, qg, kg, cos_t, sin_t, bq, sgu_w, sgu_b, on)


ACC_ROWS = HEAD_DIM + BF16_ROWS
N_PAIRS = N_HEADS // 2
PAIRS_PER_KV = N_PAIRS // N_KV_HEADS


def _pair_queries(qt, kv_idx, tq):
    qf = qt.astype(F32)
    mine = lax.broadcasted_iota(jnp.int32, (2 * HEAD_DIM, tq), 0) // HEAD_DIM == kv_idx
    qe = qf[:HEAD_DIM]
    qo = qf[HEAD_DIM:]
    rhs = jnp.concatenate(
        [jnp.where(mine, jnp.concatenate([qe, qe], axis=0), 0.0),
         jnp.where(mine, jnp.concatenate([qo, qo], axis=0), 0.0)], axis=1)
    return rhs.astype(BF16)


def _with_ones(vt_c):
    return jnp.concatenate([vt_c, jnp.ones((BF16_ROWS, vt_c.shape[1]), BF16)], axis=0)


def _flash_pairs(rhs_list, chunk_lists, tq, shift=None):
    n_pairs, n_chunks = len(rhs_list), len(chunk_lists[0])
    scores = lambda s, c: jnp.dot(chunk_lists[s][c][0](), rhs_list[s], preferred_element_type=F32)
    m = [jnp.full((1, 2 * tq), NEG_BIG, F32)] * n_pairs
    acc = [jnp.zeros((ACC_ROWS, 2 * tq), F32)] * n_pairs
    st = [scores(s, 0) for s in range(n_pairs)]
    for c in range(n_chunks):
        for s in range(n_pairs):
            if shift is None:
                m_new = jnp.maximum(m[s], jnp.max(st[s], axis=0, keepdims=True))
            st_next = scores(s, c + 1) if c + 1 < n_chunks else None
            if shift is None:
                alpha = jnp.exp2(m[s] - m_new)
                p = jnp.exp2(st[s] - m_new).astype(BF16)
                acc[s] = alpha * acc[s] + jnp.dot(chunk_lists[s][c][1](), p, preferred_element_type=F32)
                m[s] = m_new
            else:
                p = jnp.exp2(st[s] - shift).astype(BF16)
                acc[s] = acc[s] + jnp.dot(chunk_lists[s][c][1](), p, preferred_element_type=F32)
            st[s] = st_next
    outs = []
    for s in range(n_pairs):
        o = acc[s][:HEAD_DIM] * (1.0 / acc[s][HEAD_DIM:HEAD_DIM + 1])
        outs.append(jnp.concatenate([o[:, :tq], o[:, tq:]], axis=0))
    return outs


def _attn_ctx_kernel(qt_ref, k_ref, vt_ref, o_ref):
    tq = SEQ
    rhs_list, chunk_lists = [], []
    for pair in range(N_PAIRS):
        kv_idx = pair // PAIRS_PER_KV
        rhs_list.append(_pair_queries(qt_ref[pair * LANES:(pair + 1) * LANES, :], kv_idx, tq))
        chunk_lists.append([(lambda: k_ref[...],
                             lambda kv_idx=kv_idx: _with_ones(vt_ref[kv_idx * HEAD_DIM:(kv_idx + 1) * HEAD_DIM, :]))])
    for pair, o_t in enumerate(_flash_pairs(rhs_list, chunk_lists, tq)):
        o_ref[:, pair * LANES:(pair + 1) * LANES] = o_t.T.astype(BF16)


def _attn_ctx(l, qt, k, vt):
    halves = TM // SEQ
    return pl.pallas_call(
        _attn_ctx_kernel,
        grid=(BATCH,),
        in_specs=[
            pl.BlockSpec((None, D_ATTN, SEQ), lambda b: (b // halves, 0, b % halves)),
            pl.BlockSpec((SEQ, D_KV), lambda b: (b, 0)),
            pl.BlockSpec((None, D_KV, SEQ), lambda b: (b // halves, 0, b % halves)),
        ],
        out_specs=pl.BlockSpec((SEQ, D_ATTN), lambda b: (b, 0)),
        out_shape=jax.ShapeDtypeStruct((T_CTX, D_ATTN), BF16),
        compiler_params=_params(1),
        name=f"attn_ctx_l{l}",
    )(qt, k, vt)


KC = 512
LAT_PAIRS_PER_STEP = 1


MAX_SHIFT = 60.0
BOUND_SLACK = 1.05


def _score_bound_sq(qg_ref, kg_ref, ck):
    qg = qg_ref[...]
    kg = kg_ref[...]
    gq2 = jnp.max(qg * qg, axis=1, keepdims=True)
    k2 = HEAD_DIM * jnp.max(kg * kg, axis=1, keepdims=True)
    if ck is not None:
        sq = ck * ck
        first = lax.broadcasted_iota(jnp.int32, sq.shape, 1) < HEAD_DIM
        n0 = jnp.sum(jnp.where(first, sq, 0.0), axis=1, keepdims=True)
        n1 = jnp.sum(jnp.where(first, 0.0, sq), axis=1, keepdims=True)
        k2 = jnp.maximum(k2, jnp.max(jnp.maximum(n0, n1), axis=0, keepdims=True))
    return (BOUND_SLACK * Q_SCALE) ** 2 * HEAD_DIM * gq2 * k2


def _run_with_score_bound(run, bound):
    small = bound <= MAX_SHIFT

    @pl.when(small)
    def _():
        run(bound)

    @pl.when(jnp.logical_not(small))
    def _():
        run(None)


def _flash_pair_rolled(rhs, k_ref, vt_ref, v_row, tail_chunks, tq):
    assert KC == TM

    def update(carry, k_c, vt_c):
        m, acc = carry
        st = jnp.dot(k_c, rhs, preferred_element_type=F32)
        m_new = jnp.maximum(m, jnp.max(st, axis=0, keepdims=True))
        p = jnp.exp2(st - m_new).astype(BF16)
        return m_new, jnp.exp2(m - m_new) * acc + jnp.dot(vt_c, p, preferred_element_type=F32)

    def body(c, carry):
        k_c = k_ref[pl.ds(pl.multiple_of(c * KC, KC), KC), :]
        return update(carry, k_c, _with_ones(vt_ref[c, pl.ds(v_row, HEAD_DIM), :]))

    carry = (jnp.full((1, 2 * tq), NEG_BIG, F32), jnp.zeros((ACC_ROWS, 2 * tq), F32))
    carry = lax.fori_loop(0, DEC_SEQ // KC, body, carry)
    for get_k, get_vt in tail_chunks:
        carry = update(carry, get_k(), get_vt())
    acc = carry[1]
    o = acc[:HEAD_DIM] * (1.0 / acc[HEAD_DIM:HEAD_DIM + 1])
    return jnp.concatenate([o[:, :tq], o[:, tq:]], axis=0)


def _attn_lat_kernel(qt_ref, k_ref, vt_ref, ck_ref, cv_ref, qg_ref, kg_ref, w1_ref, w3_ref, w2_ref,
                     o_ref, w1b_ref, w3b_ref, w2b_ref, bound_ref):
    tq = TM

    def run(shift):
        w1b_ref[...] = w1_ref[...].astype(BF16)
        w3b_ref[...] = w3_ref[...].astype(BF16)
        w2b_ref[...] = w2_ref[...].astype(BF16)
        cvt = cv_ref[...].T
        rhs_list, chunk_lists, v_rows = [], [], []
        for j in range(LAT_PAIRS_PER_STEP):
            kv_idx = (pl.program_id(2) * LAT_PAIRS_PER_STEP + j) // PAIRS_PER_KV
            rhs_list.append(_pair_queries(qt_ref[j * LANES:(j + 1) * LANES, :], kv_idx, tq))
            v_row = pl.multiple_of(kv_idx * HEAD_DIM, HEAD_DIM)
            v_rows.append(v_row)

            def latent_chunk(c, v_row=v_row):
                tile, lane0 = (c * KC) // TM, (c * KC) % TM
                return (lambda: k_ref[c * KC:(c + 1) * KC, :],
                        lambda: _with_ones(vt_ref[tile, pl.ds(v_row, HEAD_DIM), lane0:lane0 + KC]))

            def cached_chunk(c, kv_idx=kv_idx):
                return (lambda: ck_ref[c * KC:(c + 1) * KC, :].astype(BF16),
                        lambda: _with_ones(jnp.where(kv_idx == 0, cvt[:HEAD_DIM, c * KC:(c + 1) * KC],
                                                     cvt[HEAD_DIM:, c * KC:(c + 1) * KC]).astype(BF16)))

            chunks = [latent_chunk(c) for c in range(DEC_SEQ // KC)]
            chunks += [cached_chunk(c) for c in range(PAST_LEN // KC)]
            chunk_lists.append(chunks)
        if shift is None:
            outs = [_flash_pair_rolled(rhs_list[j], k_ref, vt_ref, v_rows[j], chunk_lists[j][DEC_SEQ // KC:], tq)
                    for j in range(LAT_PAIRS_PER_STEP)]
        else:
            outs = _flash_pairs(rhs_list, chunk_lists, tq, shift)
        for j, o_t in enumerate(outs):
            o_ref[:, j * LANES:(j + 1) * LANES] = o_t.T.astype(BF16)

    @pl.when(jnp.logical_and(pl.program_id(1) == 0, pl.program_id(2) == 0))
    def _():
        bound_ref[0] = jnp.sqrt(_score_bound_sq(qg_ref, kg_ref, ck_ref[...]))[0, 0]

    _run_with_score_bound(run, bound_ref[0])


def _attn_lat(l, qt, k, vt, cache_k, cache_v, qg, kg, moe_w1, moe_w3, moe_w2):
    n_q = DEC_SEQ // TM
    first = N_CTX_TILES
    ctx_blocks = T_CTX // DEC_SEQ
    width = LAT_PAIRS_PER_STEP * LANES
    n_p = N_PAIRS // LAT_PAIRS_PER_STEP
    per_expert = DEC_BATCH * n_q * n_p // N_EXPERTS
    step = lambda b, i, p: (b * n_q + i) * n_p + p
    w_in_spec = lambda rows, cols: pl.BlockSpec(
        (None, None, rows // per_expert, cols),
        lambda b, i, p: (l, step(b, i, p) // per_expert, step(b, i, p) % per_expert, 0))
    w_out_spec = lambda rows, cols: pl.BlockSpec(
        (None, rows // per_expert, cols),
        lambda b, i, p: (step(b, i, p) // per_expert, step(b, i, p) % per_expert, 0))
    return pl.pallas_call(
        _attn_lat_kernel,
        grid=(DEC_BATCH, n_q, n_p),
        in_specs=[
            pl.BlockSpec((None, width, TM), lambda b, i, p: (first + b * n_q + i, p, 0)),
            pl.BlockSpec((DEC_SEQ, D_KV), lambda b, i, p: (ctx_blocks + b, 0)),
            pl.BlockSpec((n_q, D_KV, TM), lambda b, i, p: (ctx_blocks + b, 0, 0)),
            pl.BlockSpec((None, None, PAST_LEN, D_KV), lambda b, i, p: (b, l, 0, 0)),
            pl.BlockSpec((None, None, PAST_LEN, D_KV), lambda b, i, p: (b, l, 0, 0)),
            pl.BlockSpec((None, 1, D_ATTN), lambda b, i, p: (l, 0, 0)),
            pl.BlockSpec((None, 1, D_KV), lambda b, i, p: (l, 0, 0)),
            w_in_spec(D_MODEL, D_EXPERT),
            w_in_spec(D_MODEL, D_EXPERT),
            w_in_spec(D_EXPERT, D_MODEL),
        ],
        out_specs=[
            pl.BlockSpec((TM, width), lambda b, i, p: (b * n_q + i, p)),
            w_out_spec(D_MODEL, D_EXPERT),
            w_out_spec(D_MODEL, D_EXPERT),
            w_out_spec(D_EXPERT, D_MODEL),
        ],
        out_shape=[
            jax.ShapeDtypeStruct((T_LAT, D_ATTN), BF16),
            jax.ShapeDtypeStruct((N_EXPERTS, D_MODEL, D_EXPERT), BF16),
            jax.ShapeDtypeStruct((N_EXPERTS, D_MODEL, D_EXPERT), BF16),
            jax.ShapeDtypeStruct((N_EXPERTS, D_EXPERT, D_MODEL), BF16),
        ],
        scratch_shapes=[pltpu.SMEM((1,), F32)],
        compiler_params=_params(3),
        name=f"attn_lat_l{l}",
    )(qt, k, vt, cache_k, cache_v, qg, kg, moe_w1, moe_w3, moe_w2)


def _first_max_index(vals, valid, rowf):
    masked = jnp.where(valid, vals, -jnp.inf)
    mx = jnp.max(masked, axis=0, keepdims=True)
    idx = jnp.min(jnp.where(masked == mx, rowf, float(SUBLANES)), axis=0, keepdims=True)
    return mx, idx


def _merge_kernel(*refs, split_x):
    n_x = 2 if split_x else 1
    x_refs, refs = refs[:n_x], refs[n_x:]
    (actx_ref, alat_ref, sgu_ref, cbz_ref, prev_ref, next_ref, cw_ref, on_ref,
     wo_ref, mod_ref, g2_ref, wr_ref, wrhi_ref, rb_ref, x1_ref, h2_ref, route_ref, gate4_ref, cnt_ref) = refs
    cur = pl.program_id(0)
    is_ctx = cur < N_CTX_TILES
    mrow = _mod_row_of_tile(cur)
    on = on_ref[...]

    cbz = cbz_ref[...]
    cb = cbz[:, :D_CONV]
    z = cbz[:, D_CONV:]
    seq_mask = jnp.where(is_ctx, SEQ - 1, DEC_SEQ - 1)
    row = lax.broadcasted_iota(jnp.int32, (TM, D_CONV), 0)
    pos = (cur * TM + row) & seq_mask
    z_prev = jnp.where(row == 0, prev_ref[SUBLANES - 1:SUBLANES, D_CONV:], pltpu.roll(z, 1, 0))
    z_next = jnp.where(row == TM - 1, next_ref[0:1, D_CONV:], pltpu.roll(z, TM - 1, 0))
    z_prev = jnp.where(pos == 0, 0.0, z_prev)
    z_next = jnp.where(pos == seq_mask, 0.0, z_next)
    cw = cw_ref[...]
    conv = cb * (z_prev * cw[0:1, :] + z * cw[1:2, :] + z_next * cw[2:3, :])
    conv_n = _rms(conv, on[:, D_ATTN + D_SGU:]).astype(BF16)

    attn = jnp.where(is_ctx, actx_ref[...], alat_ref[...]).astype(F32)
    attn_n = _rms(attn, on[:, :D_ATTN]).astype(BF16)
    merged = jnp.concatenate([attn_n, sgu_ref[...], conv_n], axis=1)
    x = jnp.where(is_ctx, x_refs[0][...], x_refs[1][...]) if split_x else x_refs[0][...]
    x1 = x + _mod_part(mod_ref, 2, mrow) * jnp.dot(merged, wo_ref[...], preferred_element_type=F32)
    x1_ref[...] = x1
    _route_tile(x1, mrow, mod_ref, g2_ref, wr_ref, wrhi_ref, rb_ref, h2_ref, route_ref, gate4_ref, cnt_ref)


def _route_tile(x1, mrow, mod_ref, g2_ref, wr_ref, wrhi_ref, rb_ref, h2_ref, route_ref, gate4_ref, cnt_ref):
    n = x1.shape[0]
    h2 = _rms(x1, g2_ref[...] * (1.0 + _mod_part(mod_ref, 4, mrow))) + _mod_part(mod_ref, 3, mrow)
    hb = h2.astype(BF16)
    h2_ref[...] = hb

    h_lo = (h2 - hb.astype(F32)).astype(BF16)
    a = jnp.dot(hb, wr_ref[...], preferred_element_type=F32)
    b = jnp.dot(h_lo, wrhi_ref[...], preferred_element_type=F32)
    logits = a[:, :ROUTER_COLS] + a[:, ROUTER_COLS:] + b + rb_ref[...]
    lt = logits.T

    rowi = lax.broadcasted_iota(jnp.int32, (SUBLANES, n), 0).astype(F32)
    valid = rowi < float(N_EXPERT_GROUPS)
    g_log = lt[0:SUBLANES]
    g_max, g_idx = _first_max_index(g_log, valid, rowi)
    p_g = 1.0 / jnp.sum(jnp.where(valid, jnp.exp(g_log - g_max), 0.0), axis=0, keepdims=True)
    e_log = jnp.zeros((SUBLANES, n), F32)
    for g in range(N_EXPERT_GROUPS):
        e_log = jnp.where(g_idx == g, lt[(g + 1) * SUBLANES:(g + 2) * SUBLANES], e_log)
    e_max, i1 = _first_max_index(e_log, valid, rowi)
    e_exp = jnp.where(valid, jnp.exp(e_log - e_max), 0.0)
    e_prob = e_exp / jnp.sum(e_exp, axis=0, keepdims=True)
    v1 = jnp.max(e_prob, axis=0, keepdims=True)
    rest = jnp.logical_and(valid, rowi != i1)
    v2, i2 = _first_max_index(e_prob, rest, rowi)
    denom = v1 + v2
    w_sel = p_g * jnp.where(rowi == i1, v1 / denom, jnp.where(rowi == i2, v2 / denom, 0.0))
    route_ref[...] = jnp.broadcast_to(g_idx, (SUBLANES, n))
    pad = jnp.zeros((LANES - SUBLANES, n), F32)
    gate4_ref[...] = jnp.concatenate([w_sel, pad], axis=0).T
    counts = jnp.sum(jnp.where(rowi == g_idx, 1.0, 0.0), axis=1, keepdims=True)
    cnt_ref[...] = jnp.broadcast_to(counts, (SUBLANES, LANES)).astype(jnp.int32)


def _merge(l, xs, attn_ctx, attn_lat, sgu_n, cbz, conv_w, on, w_out, mod, g2, wr_cat, wr_hi, rb):
    lay = lambda *rest: (lambda i: (l,) + rest)
    rows8 = TM // SUBLANES
    return pl.pallas_call(
        functools.partial(_merge_kernel, split_x=len(xs) == 2),
        grid=(N_TILES,),
        in_specs=_x_specs(xs) + [
            pl.BlockSpec((TM, D_ATTN), lambda i: (_ctx_tile(i), 0)),
            pl.BlockSpec((TM, D_ATTN), lambda i: (_lat_tile(i), 0)),
            pl.BlockSpec((TM, D_SGU), lambda i: (i, 0)),
            pl.BlockSpec((TM, 2 * D_CONV), lambda i: (i, 0)),
            pl.BlockSpec((SUBLANES, 2 * D_CONV), lambda i: (jnp.maximum(i * rows8 - 1, 0), 0)),
            pl.BlockSpec((SUBLANES, 2 * D_CONV),
                         lambda i: (jnp.minimum((i + 1) * rows8, T_ALL // SUBLANES - 1), 0)),
            pl.BlockSpec((None, 3, D_CONV), lay(0, 0)),
            pl.BlockSpec((None, 1, D_MIX), lay(0, 0)),
            pl.BlockSpec((None, D_MIX, D_MODEL), lay(0, 0)),
            _mod_spec(l),
            pl.BlockSpec((None, 1, D_MODEL), lay(0, 0)),
            pl.BlockSpec((None, D_MODEL, 2 * ROUTER_COLS), lay(0, 0)),
            pl.BlockSpec((None, D_MODEL, ROUTER_COLS), lay(0, 0)),
            pl.BlockSpec((None, 1, ROUTER_COLS), lay(0, 0)),
        ],
        out_specs=[
            pl.BlockSpec((TM, D_MODEL), lambda i: (i, 0)),
            pl.BlockSpec((TM, D_MODEL), lambda i: (i, 0)),
            pl.BlockSpec((None, SUBLANES, TM), lambda i: (i, 0, 0)),
            pl.BlockSpec((TM, LANES), lambda i: (i, 0)),
            pl.BlockSpec((None, SUBLANES, LANES), lambda i: (i, 0, 0)),
        ],
        out_shape=[
            jax.ShapeDtypeStruct((T_ALL, D_MODEL), F32),
            jax.ShapeDtypeStruct((T_ALL, D_MODEL), BF16),
            jax.ShapeDtypeStruct((N_TILES, SUBLANES, TM), F32),
            jax.ShapeDtypeStruct((T_ALL, LANES), F32),
            jax.ShapeDtypeStruct((N_TILES, SUBLANES, LANES), jnp.int32),
        ],
        compiler_params=_params(1),
        name=f"merge_l{l}",
    )(*xs, attn_ctx, attn_lat, sgu_n, cbz, cbz, cbz, conv_w, on, w_out, mod, g2, wr_cat, wr_hi, rb)


RB = 144
SORT_ROWS = TM + LANES
SORT_ROWS_PAD = -(-(SORT_ROWS + RB) // BF16_ROWS) * BF16_ROWS
MOE_VMEM_LIMIT = 56 * 1024 * 1024


def _split3(x):
    hi = x.astype(BF16)
    r1 = x - hi.astype(F32)
    mid = r1.astype(BF16)
    lo = (r1 - mid.astype(F32)).astype(BF16)
    return hi, mid, lo


def _moe_kernel(cnt_ref, h_ref, route_ref, gate4_ref, tri_ref, w1_ref, w3_ref, w2_ref, x1_ref, mod_ref, fn_ref,
                *refs, final):
    out_refs, (xs_ref, gs_ref, zs_ref) = refs[:-3], refs[-3:]
    i = pl.program_id(0)
    gate2 = _mod_part(mod_ref, 5, _mod_row_of_tile(i))
    counts = [cnt_ref[i, g] for g in range(N_EXPERT_GROUPS)]
    starts = [jnp.int32(0)]
    for g in range(N_EXPERT_GROUPS - 1):
        starts.append(starts[-1] + (counts[g] + (BF16_ROWS - 1)) // BF16_ROWS * BF16_ROWS)

    g_idx = route_ref[0:1, :]
    rowi = lax.broadcasted_iota(jnp.int32, (SUBLANES, TM), 0).astype(F32)
    onehot = jnp.where(rowi == g_idx, 1.0, 0.0)
    incl = jnp.dot(onehot.astype(BF16), tri_ref[...], preferred_element_type=F32)
    pos = jnp.sum(onehot * incl, axis=0, keepdims=True) - 1.0
    for g in range(1, N_EXPERT_GROUPS):
        pos = pos + jnp.where(g_idx == float(g), starts[g].astype(F32), 0.0)
    sub = lax.broadcasted_iota(jnp.int32, (SORT_ROWS, TM), 0).astype(F32)
    perm = jnp.where(sub == pos, 1.0, 0.0).astype(BF16)
    pos_col = jnp.broadcast_to(pos, (LANES, TM)).T
    lane = lax.broadcasted_iota(jnp.int32, (TM, SORT_ROWS), 1).astype(F32)
    perm_t = jnp.where(lane == jnp.concatenate([pos_col] * (SORT_ROWS // LANES), axis=1), 1.0, 0.0).astype(BF16)

    xs_ref[:SORT_ROWS, :] = jnp.dot(perm, h_ref[...], preferred_element_type=F32).astype(BF16)
    xs_ref[SORT_ROWS:, :] = jnp.zeros((SORT_ROWS_PAD - SORT_ROWS, D_MODEL), BF16)
    hi, mid, lo = _split3(gate4_ref[...])
    packed = (hi.astype(F32) + pltpu.roll(mid.astype(F32), EXPERTS_PER_GROUP, 1)
              + pltpu.roll(lo.astype(F32), 2 * EXPERTS_PER_GROUP, 1)).astype(BF16)
    gsp = jnp.dot(perm, packed, preferred_element_type=F32)
    gs_ref[:SORT_ROWS, :] = (gsp + pltpu.roll(gsp, LANES - EXPERTS_PER_GROUP, 1)
                             + pltpu.roll(gsp, LANES - 2 * EXPERTS_PER_GROUP, 1))
    gs_ref[SORT_ROWS:, :] = jnp.zeros((SORT_ROWS_PAD - SORT_ROWS, LANES), F32)
    zs_ref[...] = jnp.zeros((SORT_ROWS_PAD, D_MODEL), BF16)

    def block_ffn(g, blk):
        row0 = pl.multiple_of(starts[g] + blk * RB, BF16_ROWS)
        xb = xs_ref[pl.ds(row0, RB), :]
        gb = gs_ref[pl.ds(row0, RB), :]
        cols = []
        for e in range(EXPERTS_PER_GROUP):
            a = jnp.dot(xb, w1_ref[g * EXPERTS_PER_GROUP + e], preferred_element_type=F32)
            b = jnp.dot(xb, w3_ref[g * EXPERTS_PER_GROUP + e], preferred_element_type=F32)
            cols.append((a / (1.0 + jnp.exp(-a)) * b * gb[:, e:e + 1]).astype(BF16))
        z = jnp.dot(jnp.concatenate(cols, axis=1), w2_ref[g], preferred_element_type=F32)
        zs_ref[pl.ds(row0, RB), :] = z.astype(BF16)

    for g in range(N_EXPERT_GROUPS):
        def extra_block(blk, carry, g=g):
            block_ffn(g, blk)
            return carry

        lax.fori_loop(1, (counts[g] + (RB - 1)) // RB, extra_block, 0)
    for g in range(N_EXPERT_GROUPS):
        block_ffn(g, 0)

    y = jnp.dot(perm_t, zs_ref[:SORT_ROWS, :], preferred_element_type=F32)
    out = x1_ref[...] + gate2 * y

    if final:
        octx_ref, olat_ref = out_refs
        out = _rms(out, fn_ref[...])
        olat_ref[...] = out

        @pl.when(i < N_CTX_TILES)
        def _():
            octx_ref[...] = out
    else:
        out_refs[0][...] = out


def _moe(l, counts, h2, route, gate4, tri, w1, w3, w2, x1, mod, final_norm, final):
    tile = pl.BlockSpec((TM, D_MODEL), lambda i, c: (i, 0))
    resident = pl.Buffered(1)
    if final:
        out_specs = [pl.BlockSpec((TM, D_MODEL), lambda i, c: (_ctx_tile(i), 0)),
                     pl.BlockSpec((TM, D_MODEL), lambda i, c: (_lat_tile(i), 0))]
        out_shape = [jax.ShapeDtypeStruct((T_CTX, D_MODEL), F32), jax.ShapeDtypeStruct((T_LAT, D_MODEL), F32)]
    else:
        out_specs = tile
        out_shape = jax.ShapeDtypeStruct((T_ALL, D_MODEL), F32)
    grid_spec = pltpu.PrefetchScalarGridSpec(
        num_scalar_prefetch=1,
        grid=(N_TILES,),
        in_specs=[
            tile,
            pl.BlockSpec((None, SUBLANES, TM), lambda i, c: (i, 0, 0)),
            pl.BlockSpec((TM, LANES), lambda i, c: (i, 0)),
            pl.BlockSpec((TM, TM), lambda i, c: (0, 0), pipeline_mode=resident),
            pl.BlockSpec((N_EXPERTS, D_MODEL, D_EXPERT), lambda i, c: (0, 0, 0), pipeline_mode=resident),
            pl.BlockSpec((N_EXPERTS, D_MODEL, D_EXPERT), lambda i, c: (0, 0, 0), pipeline_mode=resident),
            pl.BlockSpec((N_EXPERT_GROUPS, D_GROUP_HID, D_MODEL), lambda i, c: (0, 0, 0), pipeline_mode=resident),
            tile,
            pl.BlockSpec((None, N_MOD_PARTS, N_MOD_ROWS, D_MODEL), lambda i, c: (l, 0, 0, 0)),
            pl.BlockSpec((1, D_MODEL), lambda i, c: (0, 0)),
        ],
        out_specs=out_specs,
        scratch_shapes=[pltpu.VMEM((SORT_ROWS_PAD, D_MODEL), BF16),
                        pltpu.VMEM((SORT_ROWS_PAD, LANES), F32),
                        pltpu.VMEM((SORT_ROWS_PAD, D_MODEL), BF16)],
    )
    return pl.pallas_call(
        functools.partial(_moe_kernel, final=final),
        grid_spec=grid_spec,
        out_shape=out_shape,
        compiler_params=pltpu.CompilerParams(dimension_semantics=("arbitrary",),
                                             vmem_limit_bytes=MOE_VMEM_LIMIT),
        name=f"moe_l{l}",
    )(counts, h2, route, gate4, tri, w1, w3, w2, x1, mod, final_norm)


def _rope_tables():
    rows = DEC_SEQ // GRID_W
    row = np.repeat(np.arange(rows, dtype=np.float32), GRID_W)
    col = np.tile(np.arange(GRID_W, dtype=np.float32), rows)
    half = ROPE_AXIS_DIM // 2
    inv_freq = (1.0 / (np.float32(ROPE_THETA) ** (np.arange(half, dtype=np.float32) * np.float32(2.0)
                                                  / np.float32(ROPE_AXIS_DIM)))).astype(np.float32)
    ar = row[:, None] * inv_freq
    ac = col[:, None] * inv_freq
    cos64 = np.concatenate([np.cos(ar), np.cos(ar), np.cos(ac), np.cos(ac)], axis=1)
    sin64 = np.concatenate([-np.sin(ar), np.sin(ar), -np.sin(ac), np.sin(ac)], axis=1)
    cos_t = np.concatenate([np.ones((TM, LANES), np.float32), np.tile(cos64, (1, 2))], axis=0)
    sin_t = np.concatenate([np.zeros((TM, LANES), np.float32), np.tile(sin64, (1, 2))], axis=0)
    return jnp.asarray(cos_t, F32), jnp.asarray(sin_t, F32)


def _router_weights(router_g_w, router_g_b, router_e_w, router_e_b):
    zw = jnp.zeros((DEPTH, D_MODEL, SUBLANES - N_EXPERT_GROUPS), F32)
    zb = jnp.zeros((DEPTH, SUBLANES - N_EXPERT_GROUPS), F32)
    w_cols, b_cols = [router_g_w, zw], [router_g_b, zb]
    for g in range(N_EXPERT_GROUPS):
        w_cols += [router_e_w[:, g], zw]
        b_cols += [router_e_b[:, g], zb]
    used = (1 + N_EXPERT_GROUPS) * SUBLANES
    w_cols.append(jnp.zeros((DEPTH, D_MODEL, ROUTER_COLS - used), F32))
    b_cols.append(jnp.zeros((DEPTH, ROUTER_COLS - used), F32))
    w = jnp.concatenate(w_cols, axis=2)
    b = jnp.concatenate(b_cols, axis=1).reshape(DEPTH, 1, ROUTER_COLS)
    w_hi = w.astype(BF16)
    w_lo = (w - w_hi.astype(F32)).astype(BF16)
    return jnp.concatenate([w_hi, w_lo], axis=2), w_hi, b


def kernel(x_prompt, x_sample, cache_k, cache_v, c, c_ctx, w_ada, b_ada, norm1, w_in, q_norm, k_norm,
           sgu_w, sgu_b, conv_w, out_norm, w_out, norm2, router_g_w, router_g_b, router_e_w,
           router_e_b, moe_w1, moe_w3, moe_w2, final_norm):
    xs = (x_prompt.reshape(T_CTX, D_MODEL), x_sample.reshape(T_LAT, D_MODEL))

    cvecs = jnp.concatenate([c_ctx[None, :], c, jnp.zeros((N_MOD_ROWS - 1 - DEC_BATCH, D_MODEL), F32)], axis=0)
    mod = _modulation(cvecs.T, w_ada, b_ada)

    w_in_b = w_in.astype(BF16)
    w_out_b = w_out.astype(BF16)
    g1 = norm1.reshape(DEPTH, 1, D_MODEL)
    g2 = norm2.reshape(DEPTH, 1, D_MODEL)
    on = out_norm.reshape(DEPTH, 1, D_MIX)
    qg = jnp.tile(q_norm, (1, N_HEADS)).reshape(DEPTH, 1, D_ATTN)
    kg = jnp.tile(k_norm, (1, N_KV_HEADS)).reshape(DEPTH, 1, D_KV)
    cos_t, sin_t = _rope_tables()
    bq = jnp.asarray(np.kron(np.eye(N_HEADS), np.full((HEAD_DIM, HEAD_DIM), 1.0 / HEAD_DIM)), BF16)
    sgu_w_pairs = sgu_w.astype(BF16).reshape(DEPTH, N_SGU_GROUPS // 2, 2, CHUNK, CHUNK)
    sgu_w_pairs = jnp.concatenate([sgu_w_pairs[:, :, 0], sgu_w_pairs[:, :, 1]], axis=-1)
    sgu_bias = jnp.repeat(jnp.swapaxes(sgu_b, 1, 2), SGU_GROUP_DIM, axis=2)
    wr_cat, wr_hi, rb = _router_weights(router_g_w, router_g_b, router_e_w, router_e_b)
    fn = final_norm.reshape(1, D_MODEL)
    tri = jnp.asarray(np.triu(np.ones((TM, TM), np.float32)), BF16)
    ck = cache_k.reshape(DEC_BATCH, DEPTH, PAST_LEN, D_KV)
    cv = cache_v.reshape(DEC_BATCH, DEPTH, PAST_LEN, D_KV)

    ctx_ks, ctx_vs = [], []
    for l in range(DEPTH):
        qt, k, vt, k32, v32, sgu_n, cbz = _proj(l, xs, mod, g1, w_in_b, qg, kg, cos_t, sin_t, bq,
                                                sgu_w_pairs, sgu_bias, on)
        attn_ctx = _attn_ctx(l, qt, k, vt)
        attn_lat, w1, w3, w2 = _attn_lat(l, qt, k, vt, ck, cv, qg, kg, moe_w1, moe_w3, moe_w2)
        w2 = w2.reshape(N_EXPERT_GROUPS, D_GROUP_HID, D_MODEL)
        x1, h2, route, gate4, cnt = _merge(l, xs, attn_ctx, attn_lat, sgu_n, cbz, conv_w, on, w_out_b, mod, g2,
                                           wr_cat, wr_hi, rb)
        counts = cnt[:, :N_EXPERT_GROUPS, 0]
        out = _moe(l, counts, h2, route, gate4, tri, w1, w3, w2, x1, mod, fn, final=(l == DEPTH - 1))
        xs = (out,)
        ctx_ks.append(k32.reshape(BATCH, SEQ, N_KV_HEADS, HEAD_DIM))
        ctx_vs.append(v32.reshape(BATCH, SEQ, N_KV_HEADS, HEAD_DIM))

    y_prompt, y_sample = out
    return (y_prompt.reshape(BATCH, SEQ, D_MODEL), y_sample.reshape(DEC_BATCH, DEC_SEQ, D_MODEL),
            jnp.stack(ctx_ks, axis=1), jnp.stack(ctx_vs, axis=1))
```

```python
import functools
import math

import jax
import jax.numpy as jnp
import numpy as np
from jax import lax
from jax.experimental import pallas as pl
from jax.experimental.pallas import tpu as pltpu

D_MODEL = 1024
BATCH = 16
SEQ = 256
DEPTH = 2
DEC_BATCH = 2
DEC_SEQ = 4096
PAST_LEN = 512
GRID_W = 64
N_HEADS = 8
N_KV_HEADS = 2
HEAD_DIM = 64
ROPE_AXIS_DIM = HEAD_DIM // 2
ROPE_THETA = 10000.0
D_ATTN = N_HEADS * HEAD_DIM
D_KV = N_KV_HEADS * HEAD_DIM
CHUNK = 128
N_SGU_GROUPS = 4
SGU_GROUP_DIM = 64
D_SGU = N_SGU_GROUPS * SGU_GROUP_DIM
D_CONV = 256
D_MIX = D_ATTN + D_SGU + D_CONV
D_IN = D_ATTN + 2 * D_KV + 2 * D_SGU + 3 * D_CONV
N_EXPERT_GROUPS = 4
EXPERTS_PER_GROUP = 4
N_EXPERTS = N_EXPERT_GROUPS * EXPERTS_PER_GROUP
D_EXPERT = 256
D_GROUP_HID = EXPERTS_PER_GROUP * D_EXPERT
EPS = 1e-6

T_CTX = BATCH * SEQ
T_LAT = DEC_BATCH * DEC_SEQ
T_ALL = T_CTX + T_LAT
TM = 512
N_TILES = T_ALL // TM
N_CTX_TILES = T_CTX // TM
LAT_TILES_PER_BATCH = DEC_SEQ // TM
LANES = 128
SUBLANES = 8
BF16_ROWS = 16
N_MOD_ROWS = 8
N_MOD_PARTS = 6
ROUTER_COLS = 128
VMEM_LIMIT = 48 * 1024 * 1024

OFF_Q = 0
OFF_KV = D_ATTN
OFF_SGU = OFF_KV + 2 * D_KV
OFF_CONV = OFF_SGU + 2 * D_SGU

BF16 = jnp.bfloat16
F32 = jnp.float32
NEG_BIG = -1e30
Q_SCALE = HEAD_DIM ** -0.5 * math.log2(math.e)


def _params(n_grid_dims):
    return pltpu.CompilerParams(
        dimension_semantics=("arbitrary",) * n_grid_dims,
        vmem_limit_bytes=VMEM_LIMIT,
    )


def _mod_row_of_tile(i, tile=TM):
    n_ctx = T_CTX // tile
    return jnp.where(i < n_ctx, 0, 1 + (i - n_ctx) // (DEC_SEQ // tile))


def _mod_part(mod_ref, part, row):
    return mod_ref[part, pl.ds(row, 1), :]


def _rope_block_of_tile(i):
    return jnp.where(i < N_CTX_TILES, 0, 1 + (i - N_CTX_TILES) % LAT_TILES_PER_BATCH)


def _rms(x, gain):
    ms = jnp.mean(x * x, axis=-1, keepdims=True)
    return x * lax.rsqrt(ms + EPS) * gain


def _ctx_tile(i):
    return jnp.minimum(i, N_CTX_TILES - 1)


def _lat_tile(i):
    return jnp.maximum(i - N_CTX_TILES, 0)


MOD_PARTS_PER_STEP = 2
MOD_TN = MOD_PARTS_PER_STEP * D_MODEL


def _mod_kernel(ct_ref, w_ref, b_ref, o_ref):
    c = ct_ref[...]
    s = c / (1.0 + jnp.exp(-c))
    row = lax.broadcasted_iota(jnp.int32, (N_MOD_ROWS, D_MODEL), 0)
    for part in range(MOD_PARTS_PER_STEP):
        cols = slice(part * D_MODEL, (part + 1) * D_MODEL)
        w = w_ref[:, cols]
        out = jnp.zeros((N_MOD_ROWS, D_MODEL), F32)
        for r in range(1 + DEC_BATCH):
            acc = jnp.sum(w * s[:, r:r + 1], axis=0, keepdims=True)
            out = jnp.where(row == r, acc, out)
        o_ref[part] = out + b_ref[:, cols]


def _modulation(cvecs_t, w_ada, b_ada):
    n_col = N_MOD_PARTS * D_MODEL
    return pl.pallas_call(
        _mod_kernel,
        grid=(DEPTH, n_col // MOD_TN),
        in_specs=[
            pl.BlockSpec((D_MODEL, N_MOD_ROWS), lambda l, j: (0, 0)),
            pl.BlockSpec((None, D_MODEL, MOD_TN), lambda l, j: (l, 0, j)),
            pl.BlockSpec((None, 1, MOD_TN), lambda l, j: (l, 0, j)),
        ],
        out_specs=pl.BlockSpec((None, MOD_PARTS_PER_STEP, N_MOD_ROWS, D_MODEL), lambda l, j: (l, j, 0, 0)),
        out_shape=jax.ShapeDtypeStruct((DEPTH, N_MOD_PARTS, N_MOD_ROWS, D_MODEL), F32),
        compiler_params=_params(2),
        name="adaln_modulation",
    )(cvecs_t, w_ada, b_ada.reshape(DEPTH, 1, n_col))


def _mod_spec(l):
    return pl.BlockSpec((None, N_MOD_PARTS, N_MOD_ROWS, D_MODEL), lambda *idx: (l, 0, 0, 0))


def _swap16(x):
    lane = lax.broadcasted_iota(jnp.int32, x.shape, 1)
    up = pltpu.roll(x, 16, 1)
    down = pltpu.roll(x, LANES - 16, 1)
    return jnp.where((lane & 16) != 0, up, down)


def _rope(x, cos_t, sin_t):
    cols = []
    for j in range(x.shape[1] // LANES):
        xc = x[:, j * LANES:(j + 1) * LANES]
        cols.append(xc * cos_t + _swap16(xc) * sin_t)
    return cols[0] if len(cols) == 1 else jnp.concatenate(cols, axis=1)


def _proj_kernel(*refs, split_x):
    n_x = 2 if split_x else 1
    x_refs, refs, (scr_even, scr_odd) = refs[:n_x], refs[n_x:-2], refs[-2:]
    step = functools.partial(_proj_step, x_refs, refs, split_x)
    i = pl.program_id(0)

    @pl.when(i == 0)
    def _():
        scr_odd[...] = jnp.zeros((TM, D_IN), F32)

    last = i == N_TILES

    @pl.when(jnp.logical_and(i % 2 == 0, jnp.logical_not(last)))
    def _():
        step(scr_even, scr_odd)

    @pl.when(jnp.logical_and(i % 2 == 1, jnp.logical_not(last)))
    def _():
        step(scr_odd, scr_even)

    @pl.when(last)
    def _():
        step(None, scr_even if N_TILES % 2 == 1 else scr_odd)


def _proj_step(x_refs, refs, split_x, new_ref, proj):
    (mod_ref, g1_ref, w_ref, qg_ref, kg_ref, cos_ref, sin_ref, bq_ref, sw_ref, sb_ref, on_ref,
     qt_ref, k_ref, vt_ref, k32_ref, v32_ref, sgu_ref, cbz_ref) = refs
    i = pl.program_id(0)
    cur = _stage1_tile(i)

    is_ctx = i <= N_CTX_TILES
    cos_t = cos_ref[...]
    sin_t = sin_ref[...]

    q = proj[:, OFF_Q:OFF_Q + D_ATTN]
    q2 = (q * q).astype(BF16)
    half = D_ATTN // 2
    qms = jnp.concatenate(
        [jnp.dot(q2[:, :half], bq_ref[:half, :half], preferred_element_type=F32),
         jnp.dot(q2[:, half:], bq_ref[half:, half:], preferred_element_type=F32)], axis=1)
    qn = q * lax.rsqrt(qms + EPS) * qg_ref[...]
    qr = _rope(qn, cos_t, sin_t) * Q_SCALE
    qt_ref[...] = qr.T.astype(BF16)

    k = proj[:, OFF_KV:OFF_KV + D_KV]
    v = proj[:, OFF_KV + D_KV:OFF_KV + 2 * D_KV]
    kms = jnp.dot((k * k).astype(BF16), bq_ref[:D_KV, :D_KV], preferred_element_type=F32)
    kn = k * lax.rsqrt(kms + EPS) * kg_ref[...]
    k_ref[...] = _rope(kn, cos_t, sin_t).astype(BF16)
    vt_ref[...] = v.T.astype(BF16)

    su = proj[:, OFF_SGU:OFF_SGU + D_SGU]
    sv = proj[:, OFF_SGU + D_SGU:OFF_SGU + 2 * D_SGU]
    n_chunks = TM // CHUNK
    sv_wide = jnp.concatenate([sv[n * CHUNK:(n + 1) * CHUNK, :] for n in range(n_chunks)], axis=1)
    grp = lax.broadcasted_iota(jnp.int32, (CHUNK, n_chunks * D_SGU), 1) // SGU_GROUP_DIM % N_SGU_GROUPS
    mixed_wide = jnp.zeros((CHUNK, n_chunks * D_SGU), F32)
    for pair in range(N_SGU_GROUPS // 2):
        rhs = jnp.concatenate(
            [jnp.where(grp == 2 * pair, sv_wide, 0.0), jnp.where(grp == 2 * pair + 1, sv_wide, 0.0)],
            axis=0).astype(BF16)
        mixed_wide = mixed_wide + jnp.dot(sw_ref[pair], rhs, preferred_element_type=F32)
    on_sgu = on_ref[:, D_ATTN:D_ATTN + D_SGU]
    for n in range(n_chunks):
        sgu = su[n * CHUNK:(n + 1) * CHUNK, :] * (mixed_wide[:, n * D_SGU:(n + 1) * D_SGU] + sb_ref[...])
        sgu_ref[n * CHUNK:(n + 1) * CHUNK, :] = _rms(sgu, on_sgu).astype(BF16)

    cbz_ref[:, :D_CONV] = proj[:, OFF_CONV:OFF_CONV + D_CONV]
    cbz_ref[:, D_CONV:] = (proj[:, OFF_CONV + D_CONV:OFF_CONV + 2 * D_CONV]
                           * proj[:, OFF_CONV + 2 * D_CONV:OFF_CONV + 3 * D_CONV])

    if new_ref is not None:
        x = jnp.where(cur < N_CTX_TILES, x_refs[0][...], x_refs[1][...]) if split_x else x_refs[0][...]
        mrow = _mod_row_of_tile(cur)
        h = _rms(x, g1_ref[...] * (1.0 + _mod_part(mod_ref, 1, mrow))) + _mod_part(mod_ref, 0, mrow)
        new_ref[...] = jnp.dot(h.astype(BF16), w_ref[...], preferred_element_type=F32)

    @pl.when(is_ctx)
    def _():
        k32_ref[...] = kn
        v32_ref[...] = v


def _x_specs(xs, tile_of_step=lambda i: i):
    if len(xs) == 1:
        return [pl.BlockSpec((TM, D_MODEL), lambda i: (tile_of_step(i), 0))]
    return [pl.BlockSpec((TM, D_MODEL), lambda i: (_ctx_tile(tile_of_step(i)), 0)),
            pl.BlockSpec((TM, D_MODEL), lambda i: (_lat_tile(tile_of_step(i)), 0))]


def _stage1_tile(i):
    return jnp.minimum(i, N_TILES - 1)


def _stage2_tile(i):
    return jnp.maximum(i - 1, 0)


def _proj(l, xs, mod, g1, w_in, qg, kg, cos_t, sin_t, bq, sgu_w, sgu_b, on):
    lay = lambda *rest: (lambda i: (l,) + rest)
    s2 = _stage2_tile
    return pl.pallas_call(
        functools.partial(_proj_kernel, split_x=len(xs) == 2),
        grid=(N_TILES + 1,),
        in_specs=_x_specs(xs, _stage1_tile) + [
            _mod_spec(l),
            pl.BlockSpec((None, 1, D_MODEL), lay(0, 0)),
            pl.BlockSpec((None, D_MODEL, D_IN), lay(0, 0)),
            pl.BlockSpec((None, 1, D_ATTN), lay(0, 0)),
            pl.BlockSpec((None, 1, D_KV), lay(0, 0)),
            pl.BlockSpec((TM, LANES), lambda i: (_rope_block_of_tile(s2(i)), 0)),
            pl.BlockSpec((TM, LANES), lambda i: (_rope_block_of_tile(s2(i)), 0)),
            pl.BlockSpec((D_ATTN, D_ATTN), lambda i: (0, 0)),
            pl.BlockSpec((None, N_SGU_GROUPS // 2, CHUNK, 2 * CHUNK), lay(0, 0, 0)),
            pl.BlockSpec((None, CHUNK, D_SGU), lay(0, 0)),
            pl.BlockSpec((None, 1, D_MIX), lay(0, 0)),
        ],
        out_specs=[
            pl.BlockSpec((None, D_ATTN, TM), lambda i: (s2(i), 0, 0)),
            pl.BlockSpec((TM, D_KV), lambda i: (s2(i), 0)),
            pl.BlockSpec((None, D_KV, TM), lambda i: (s2(i), 0, 0)),
            pl.BlockSpec((TM, D_KV), lambda i: (_ctx_tile(s2(i)), 0)),
            pl.BlockSpec((TM, D_KV), lambda i: (_ctx_tile(s2(i)), 0)),
            pl.BlockSpec((TM, D_SGU), lambda i: (s2(i), 0)),
            pl.BlockSpec((TM, 2 * D_CONV), lambda i: (s2(i), 0)),
        ],
        out_shape=[
            jax.ShapeDtypeStruct((N_TILES, D_ATTN, TM), BF16),
            jax.ShapeDtypeStruct((T_ALL, D_KV), BF16),
            jax.ShapeDtypeStruct((N_TILES, D_KV, TM), BF16),
            jax.ShapeDtypeStruct((T_CTX, D_KV), F32),
            jax.ShapeDtypeStruct((T_CTX, D_KV), F32),
            jax.ShapeDtypeStruct((T_ALL, D_SGU), BF16),
            jax.ShapeDtypeStruct((T_ALL, 2 * D_CONV), F32),
        ],
        scratch_shapes=[pltpu.VMEM((TM, D_IN), F32), pltpu.VMEM((TM, D_IN), F32)],
        compiler_params=_params(1),
        name=f"proj_l{l}",
    )(*xs, mod, g1, w_in, qg, kg, cos_t, sin_t, bq, sgu_w, sgu_b, on)


ACC_ROWS = HEAD_DIM + BF16_ROWS
N_PAIRS = N_HEADS // 2
PAIRS_PER_KV = N_PAIRS // N_KV_HEADS


def _pair_queries(qt, kv_idx, tq):
    qf = qt.astype(F32)
    mine = lax.broadcasted_iota(jnp.int32, (2 * HEAD_DIM, tq), 0) // HEAD_DIM == kv_idx
    qe = qf[:HEAD_DIM]
    qo = qf[HEAD_DIM:]
    rhs = jnp.concatenate(
        [jnp.where(mine, jnp.concatenate([qe, qe], axis=0), 0.0),
         jnp.where(mine, jnp.concatenate([qo, qo], axis=0), 0.0)], axis=1)
    return rhs.astype(BF16)


def _with_ones(vt_c):
    return jnp.concatenate([vt_c, jnp.ones((BF16_ROWS, vt_c.shape[1]), BF16)], axis=0)


def _flash_pairs(rhs_list, chunk_lists, tq, shift=None, query_block=None):
    n_pairs, n_chunks = len(rhs_list), len(chunk_lists[0])
    qb = 2 * tq if query_block is None else query_block
    streams = [(s, j) for s in range(n_pairs) for j in range(2 * tq // qb)]
    rhs = {(s, j): rhs_list[s][:, j * qb:(j + 1) * qb] for s, j in streams}
    scores = lambda t, c: jnp.dot(chunk_lists[t[0]][c][0](), rhs[t], preferred_element_type=F32)
    m = {t: jnp.full((1, qb), NEG_BIG, F32) for t in streams}
    acc = {t: jnp.zeros((ACC_ROWS, qb), F32) for t in streams}
    st = {t: scores(t, 0) for t in streams}
    for c in range(n_chunks):
        for t in streams:
            vt_c = chunk_lists[t[0]][c][1]()
            if shift is None:
                m_new = jnp.maximum(m[t], jnp.max(st[t], axis=0, keepdims=True))
            st_next = scores(t, c + 1) if c + 1 < n_chunks else None
            if shift is None:
                alpha = jnp.exp2(m[t] - m_new)
                p = jnp.exp2(st[t] - m_new).astype(BF16)
                acc[t] = alpha * acc[t] + jnp.dot(vt_c, p, preferred_element_type=F32)
                m[t] = m_new
            else:
                p = jnp.exp2(st[t] - shift).astype(BF16)
                acc[t] = acc[t] + jnp.dot(vt_c, p, preferred_element_type=F32)
            st[t] = st_next
    outs = []
    for s in range(n_pairs):
        a = jnp.concatenate([acc[t] for t in streams if t[0] == s], axis=1)
        o = a[:HEAD_DIM] * (1.0 / a[HEAD_DIM:HEAD_DIM + 1])
        outs.append(jnp.concatenate([o[:, :tq], o[:, tq:]], axis=0))
    return outs


def _attn_ctx_kernel(qt_ref, k_ref, vt_ref, o_ref):
    tq = SEQ
    rhs_list, chunk_lists = [], []
    for pair in range(N_PAIRS):
        kv_idx = pair // PAIRS_PER_KV
        rhs_list.append(_pair_queries(qt_ref[pair * LANES:(pair + 1) * LANES, :], kv_idx, tq))
        chunk_lists.append([(lambda: k_ref[...],
                             lambda kv_idx=kv_idx: _with_ones(vt_ref[kv_idx * HEAD_DIM:(kv_idx + 1) * HEAD_DIM, :]))])
    for pair, o_t in enumerate(_flash_pairs(rhs_list, chunk_lists, tq)):
        o_ref[:, pair * LANES:(pair + 1) * LANES] = o_t.T.astype(BF16)


def _attn_ctx(l, qt, k, vt):
    halves = TM // SEQ
    return pl.pallas_call(
        _attn_ctx_kernel,
        grid=(BATCH,),
        in_specs=[
            pl.BlockSpec((None, D_ATTN, SEQ), lambda b: (b // halves, 0, b % halves)),
            pl.BlockSpec((SEQ, D_KV), lambda b: (b, 0)),
            pl.BlockSpec((None, D_KV, SEQ), lambda b: (b // halves, 0, b % halves)),
        ],
        out_specs=pl.BlockSpec((SEQ, D_ATTN), lambda b: (b, 0)),
        out_shape=jax.ShapeDtypeStruct((T_CTX, D_ATTN), BF16),
        compiler_params=_params(1),
        name=f"attn_ctx_l{l}",
    )(qt, k, vt)


KC = 512
LAT_PAIRS_PER_STEP = 2
LAT_QUERY_BLOCK = 256


MAX_SHIFT = 60.0
BOUND_SLACK = 1.05


def _score_bound_sq(qg_ref, kg_ref, ck):
    qg = qg_ref[...]
    kg = kg_ref[...]
    gq2 = jnp.max(qg * qg, axis=1, keepdims=True)
    k2 = HEAD_DIM * jnp.max(kg * kg, axis=1, keepdims=True)
    if ck is not None:
        sq = ck * ck
        first = lax.broadcasted_iota(jnp.int32, sq.shape, 1) < HEAD_DIM
        n0 = jnp.sum(jnp.where(first, sq, 0.0), axis=1, keepdims=True)
        n1 = jnp.sum(jnp.where(first, 0.0, sq), axis=1, keepdims=True)
        k2 = jnp.maximum(k2, jnp.max(jnp.maximum(n0, n1), axis=0, keepdims=True))
    return (BOUND_SLACK * Q_SCALE) ** 2 * HEAD_DIM * gq2 * k2


def _run_with_score_bound(run, bound):
    small = bound <= MAX_SHIFT

    @pl.when(small)
    def _():
        run(bound)

    @pl.when(jnp.logical_not(small))
    def _():
        run(None)


def _flash_pair_rolled(rhs, k_ref, vt_ref, v_row, tail_chunks, tq):
    assert KC == TM

    def update(carry, k_c, vt_c):
        m, acc = carry
        st = jnp.dot(k_c, rhs, preferred_element_type=F32)
        m_new = jnp.maximum(m, jnp.max(st, axis=0, keepdims=True))
        p = jnp.exp2(st - m_new).astype(BF16)
        return m_new, jnp.exp2(m - m_new) * acc + jnp.dot(vt_c, p, preferred_element_type=F32)

    def body(c, carry):
        k_c = k_ref[pl.ds(pl.multiple_of(c * KC, KC), KC), :]
        return update(carry, k_c, _with_ones(vt_ref[c, pl.ds(v_row, HEAD_DIM), :]))

    carry = (jnp.full((1, 2 * tq), NEG_BIG, F32), jnp.zeros((ACC_ROWS, 2 * tq), F32))
    carry = lax.fori_loop(0, DEC_SEQ // KC, body, carry)
    for get_k, get_vt in tail_chunks:
        carry = update(carry, get_k(), get_vt())
    acc = carry[1]
    o = acc[:HEAD_DIM] * (1.0 / acc[HEAD_DIM:HEAD_DIM + 1])
    return jnp.concatenate([o[:, :tq], o[:, tq:]], axis=0)


def _attn_lat_kernel(qt_ref, k_ref, vt_ref, ck_ref, cv_ref, qg_ref, kg_ref, w1_ref, w3_ref, w2_ref,
                     o_ref, w1b_ref, w3b_ref, w2b_ref, bound_ref):
    tq = TM

    def run(shift):
        w1b_ref[...] = w1_ref[...].astype(BF16)
        w3b_ref[...] = w3_ref[...].astype(BF16)
        w2b_ref[...] = w2_ref[...].astype(BF16)
        cvt = cv_ref[...].T
        rhs_list, chunk_lists, v_rows = [], [], []
        for j in range(LAT_PAIRS_PER_STEP):
            kv_idx = (pl.program_id(2) * LAT_PAIRS_PER_STEP + j) // PAIRS_PER_KV
            rhs_list.append(_pair_queries(qt_ref[j * LANES:(j + 1) * LANES, :], kv_idx, tq))
            v_row = pl.multiple_of(kv_idx * HEAD_DIM, HEAD_DIM)
            v_rows.append(v_row)

            def latent_chunk(c, v_row=v_row):
                tile, lane0 = (c * KC) // TM, (c * KC) % TM
                return (lambda: k_ref[c * KC:(c + 1) * KC, :],
                        lambda: _with_ones(vt_ref[tile, pl.ds(v_row, HEAD_DIM), lane0:lane0 + KC]))

            def cached_chunk(c, kv_idx=kv_idx):
                return (lambda: ck_ref[c * KC:(c + 1) * KC, :].astype(BF16),
                        lambda: _with_ones(jnp.where(kv_idx == 0, cvt[:HEAD_DIM, c * KC:(c + 1) * KC],
                                                     cvt[HEAD_DIM:, c * KC:(c + 1) * KC]).astype(BF16)))

            chunks = [latent_chunk(c) for c in range(DEC_SEQ // KC)]
            chunks += [cached_chunk(c) for c in range(PAST_LEN // KC)]
            chunk_lists.append(chunks)
        if shift is None:
            outs = [_flash_pair_rolled(rhs_list[j], k_ref, vt_ref, v_rows[j], chunk_lists[j][DEC_SEQ // KC:], tq)
                    for j in range(LAT_PAIRS_PER_STEP)]
        else:
            outs = _flash_pairs(rhs_list, chunk_lists, tq, shift, query_block=LAT_QUERY_BLOCK)
        for j, o_t in enumerate(outs):
            o_ref[:, j * LANES:(j + 1) * LANES] = o_t.T.astype(BF16)

    @pl.when(jnp.logical_and(pl.program_id(1) == 0, pl.program_id(2) == 0))
    def _():
        bound_ref[0] = jnp.sqrt(_score_bound_sq(qg_ref, kg_ref, ck_ref[...]))[0, 0]

    _run_with_score_bound(run, bound_ref[0])


def _attn_lat(l, qt, k, vt, cache_k, cache_v, qg, kg, moe_w1, moe_w3, moe_w2):
    n_q = DEC_SEQ // TM
    first = N_CTX_TILES
    ctx_blocks = T_CTX // DEC_SEQ
    width = LAT_PAIRS_PER_STEP * LANES
    n_p = N_PAIRS // LAT_PAIRS_PER_STEP
    per_expert = DEC_BATCH * n_q * n_p // N_EXPERTS
    step = lambda b, i, p: (b * n_q + i) * n_p + p
    w_in_spec = lambda rows, cols: pl.BlockSpec(
        (None, None, rows // per_expert, cols),
        lambda b, i, p: (l, step(b, i, p) // per_expert, step(b, i, p) % per_expert, 0))
    w_out_spec = lambda rows, cols: pl.BlockSpec(
        (None, rows // per_expert, cols),
        lambda b, i, p: (step(b, i, p) // per_expert, step(b, i, p) % per_expert, 0))
    return pl.pallas_call(
        _attn_lat_kernel,
        grid=(DEC_BATCH, n_q, n_p),
        in_specs=[
            pl.BlockSpec((None, width, TM), lambda b, i, p: (first + b * n_q + i, p, 0)),
            pl.BlockSpec((DEC_SEQ, D_KV), lambda b, i, p: (ctx_blocks + b, 0)),
            pl.BlockSpec((n_q, D_KV, TM), lambda b, i, p: (ctx_blocks + b, 0, 0)),
            pl.BlockSpec((None, None, PAST_LEN, D_KV), lambda b, i, p: (b, l, 0, 0)),
            pl.BlockSpec((None, None, PAST_LEN, D_KV), lambda b, i, p: (b, l, 0, 0)),
            pl.BlockSpec((None, 1, D_ATTN), lambda b, i, p: (l, 0, 0)),
            pl.BlockSpec((None, 1, D_KV), lambda b, i, p: (l, 0, 0)),
            w_in_spec(D_MODEL, D_EXPERT),
            w_in_spec(D_MODEL, D_EXPERT),
            w_in_spec(D_EXPERT, D_MODEL),
        ],
        out_specs=[
            pl.BlockSpec((TM, width), lambda b, i, p: (b * n_q + i, p)),
            w_out_spec(D_MODEL, D_EXPERT),
            w_out_spec(D_MODEL, D_EXPERT),
            w_out_spec(D_EXPERT, D_MODEL),
        ],
        out_shape=[
            jax.ShapeDtypeStruct((T_LAT, D_ATTN), BF16),
            jax.ShapeDtypeStruct((N_EXPERTS, D_MODEL, D_EXPERT), BF16),
            jax.ShapeDtypeStruct((N_EXPERTS, D_MODEL, D_EXPERT), BF16),
            jax.ShapeDtypeStruct((N_EXPERTS, D_EXPERT, D_MODEL), BF16),
        ],
        scratch_shapes=[pltpu.SMEM((1,), F32)],
        compiler_params=_params(3),
        name=f"attn_lat_l{l}",
    )(qt, k, vt, cache_k, cache_v, qg, kg, moe_w1, moe_w3, moe_w2)


def _first_max_index(vals, valid, rowf):
    masked = jnp.where(valid, vals, -jnp.inf)
    mx = jnp.max(masked, axis=0, keepdims=True)
    idx = jnp.min(jnp.where(masked == mx, rowf, float(SUBLANES)), axis=0, keepdims=True)
    return mx, idx


def _merge_kernel(*refs, split_x):
    n_x = 2 if split_x else 1
    x_refs, refs = refs[:n_x], refs[n_x:]
    (actx_ref, alat_ref, sgu_ref, cbz_ref, prev_ref, next_ref, cw_ref, on_ref,
     wo_ref, mod_ref, g2_ref, wr_ref, wrhi_ref, rb_ref, x1_ref, h2_ref, route_ref, gate4_ref, cnt_ref) = refs
    cur = pl.program_id(0)
    is_ctx = cur < N_CTX_TILES
    mrow = _mod_row_of_tile(cur)
    on = on_ref[...]

    cbz = cbz_ref[...]
    cb = cbz[:, :D_CONV]
    z = cbz[:, D_CONV:]
    seq_mask = jnp.where(is_ctx, SEQ - 1, DEC_SEQ - 1)
    row = lax.broadcasted_iota(jnp.int32, (TM, D_CONV), 0)
    pos = (cur * TM + row) & seq_mask
    z_prev = jnp.where(row == 0, prev_ref[SUBLANES - 1:SUBLANES, D_CONV:], pltpu.roll(z, 1, 0))
    z_next = jnp.where(row == TM - 1, next_ref[0:1, D_CONV:], pltpu.roll(z, TM - 1, 0))
    z_prev = jnp.where(pos == 0, 0.0, z_prev)
    z_next = jnp.where(pos == seq_mask, 0.0, z_next)
    cw = cw_ref[...]
    conv = cb * (z_prev * cw[0:1, :] + z * cw[1:2, :] + z_next * cw[2:3, :])
    conv_n = _rms(conv, on[:, D_ATTN + D_SGU:]).astype(BF16)

    attn = jnp.where(is_ctx, actx_ref[...], alat_ref[...]).astype(F32)
    attn_n = _rms(attn, on[:, :D_ATTN]).astype(BF16)
    merged = jnp.concatenate([attn_n, sgu_ref[...], conv_n], axis=1)
    x = jnp.where(is_ctx, x_refs[0][...], x_refs[1][...]) if split_x else x_refs[0][...]
    x1 = x + _mod_part(mod_ref, 2, mrow) * jnp.dot(merged, wo_ref[...], preferred_element_type=F32)
    x1_ref[...] = x1
    _route_tile(x1, mrow, mod_ref, g2_ref, wr_ref, wrhi_ref, rb_ref, h2_ref, route_ref, gate4_ref, cnt_ref)


def _route_tile(x1, mrow, mod_ref, g2_ref, wr_ref, wrhi_ref, rb_ref, h2_ref, route_ref, gate4_ref, cnt_ref):
    n = x1.shape[0]
    h2 = _rms(x1, g2_ref[...] * (1.0 + _mod_part(mod_ref, 4, mrow))) + _mod_part(mod_ref, 3, mrow)
    hb = h2.astype(BF16)
    h2_ref[...] = hb

    h_lo = (h2 - hb.astype(F32)).astype(BF16)
    a = jnp.dot(hb, wr_ref[...], preferred_element_type=F32)
    b = jnp.dot(h_lo, wrhi_ref[...], preferred_element_type=F32)
    logits = a[:, :ROUTER_COLS] + a[:, ROUTER_COLS:] + b + rb_ref[...]
    lt = logits.T

    rowi = lax.broadcasted_iota(jnp.int32, (SUBLANES, n), 0).astype(F32)
    valid = rowi < float(N_EXPERT_GROUPS)
    g_log = lt[0:SUBLANES]
    g_max, g_idx = _first_max_index(g_log, valid, rowi)
    p_g = 1.0 / jnp.sum(jnp.where(valid, jnp.exp(g_log - g_max), 0.0), axis=0, keepdims=True)
    e_log = jnp.zeros((SUBLANES, n), F32)
    for g in range(N_EXPERT_GROUPS):
        e_log = jnp.where(g_idx == g, lt[(g + 1) * SUBLANES:(g + 2) * SUBLANES], e_log)
    e_max, i1 = _first_max_index(e_log, valid, rowi)
    e_exp = jnp.where(valid, jnp.exp(e_log - e_max), 0.0)
    e_prob = e_exp / jnp.sum(e_exp, axis=0, keepdims=True)
    v1 = jnp.max(e_prob, axis=0, keepdims=True)
    rest = jnp.logical_and(valid, rowi != i1)
    v2, i2 = _first_max_index(e_prob, rest, rowi)
    denom = v1 + v2
    w_sel = p_g * jnp.where(rowi == i1, v1 / denom, jnp.where(rowi == i2, v2 / denom, 0.0))
    route_ref[...] = jnp.broadcast_to(g_idx, (SUBLANES, n))
    pad = jnp.zeros((LANES - SUBLANES, n), F32)
    gate4_ref[...] = jnp.concatenate([w_sel, pad], axis=0).T
    counts = jnp.sum(jnp.where(rowi == g_idx, 1.0, 0.0), axis=1, keepdims=True)
    cnt_ref[...] = jnp.broadcast_to(counts, (SUBLANES, LANES)).astype(jnp.int32)


def _merge(l, xs, attn_ctx, attn_lat, sgu_n, cbz, conv_w, on, w_out, mod, g2, wr_cat, wr_hi, rb):
    lay = lambda *rest: (lambda i: (l,) + rest)
    rows8 = TM // SUBLANES
    return pl.pallas_call(
        functools.partial(_merge_kernel, split_x=len(xs) == 2),
        grid=(N_TILES,),
        in_specs=_x_specs(xs) + [
            pl.BlockSpec((TM, D_ATTN), lambda i: (_ctx_tile(i), 0)),
            pl.BlockSpec((TM, D_ATTN), lambda i: (_lat_tile(i), 0)),
            pl.BlockSpec((TM, D_SGU), lambda i: (i, 0)),
            pl.BlockSpec((TM, 2 * D_CONV), lambda i: (i, 0)),
            pl.BlockSpec((SUBLANES, 2 * D_CONV), lambda i: (jnp.maximum(i * rows8 - 1, 0), 0)),
            pl.BlockSpec((SUBLANES, 2 * D_CONV),
                         lambda i: (jnp.minimum((i + 1) * rows8, T_ALL // SUBLANES - 1), 0)),
            pl.BlockSpec((None, 3, D_CONV), lay(0, 0)),
            pl.BlockSpec((None, 1, D_MIX), lay(0, 0)),
            pl.BlockSpec((None, D_MIX, D_MODEL), lay(0, 0)),
            _mod_spec(l),
            pl.BlockSpec((None, 1, D_MODEL), lay(0, 0)),
            pl.BlockSpec((None, D_MODEL, 2 * ROUTER_COLS), lay(0, 0)),
            pl.BlockSpec((None, D_MODEL, ROUTER_COLS), lay(0, 0)),
            pl.BlockSpec((None, 1, ROUTER_COLS), lay(0, 0)),
        ],
        out_specs=[
            pl.BlockSpec((TM, D_MODEL), lambda i: (i, 0)),
            pl.BlockSpec((TM, D_MODEL), lambda i: (i, 0)),
            pl.BlockSpec((None, SUBLANES, TM), lambda i: (i, 0, 0)),
            pl.BlockSpec((TM, LANES), lambda i: (i, 0)),
            pl.BlockSpec((None, SUBLANES, LANES), lambda i: (i, 0, 0)),
        ],
        out_shape=[
            jax.ShapeDtypeStruct((T_ALL, D_MODEL), F32),
            jax.ShapeDtypeStruct((T_ALL, D_MODEL), BF16),
            jax.ShapeDtypeStruct((N_TILES, SUBLANES, TM), F32),
            jax.ShapeDtypeStruct((T_ALL, LANES), F32),
            jax.ShapeDtypeStruct((N_TILES, SUBLANES, LANES), jnp.int32),
        ],
        compiler_params=_params(1),
        name=f"merge_l{l}",
    )(*xs, attn_ctx, attn_lat, sgu_n, cbz, cbz, cbz, conv_w, on, w_out, mod, g2, wr_cat, wr_hi, rb)


RB = 144
SORT_ROWS = TM + LANES
SORT_ROWS_PAD = -(-(SORT_ROWS + RB) // BF16_ROWS) * BF16_ROWS
MOE_VMEM_LIMIT = 56 * 1024 * 1024


def _split3(x):
    hi = x.astype(BF16)
    r1 = x - hi.astype(F32)
    mid = r1.astype(BF16)
    lo = (r1 - mid.astype(F32)).astype(BF16)
    return hi, mid, lo


def _moe_kernel(cnt_ref, h_ref, route_ref, gate4_ref, tri_ref, w1_ref, w3_ref, w2_ref, x1_ref, mod_ref, fn_ref,
                *refs, final):
    out_refs, (xs_ref, gs_ref, zs_ref) = refs[:-3], refs[-3:]
    i = pl.program_id(0)
    gate2 = _mod_part(mod_ref, 5, _mod_row_of_tile(i))
    counts = [cnt_ref[i, g] for g in range(N_EXPERT_GROUPS)]
    starts = [jnp.int32(0)]
    for g in range(N_EXPERT_GROUPS - 1):
        starts.append(starts[-1] + (counts[g] + (BF16_ROWS - 1)) // BF16_ROWS * BF16_ROWS)

    g_idx = route_ref[0:1, :]
    rowi = lax.broadcasted_iota(jnp.int32, (SUBLANES, TM), 0).astype(F32)
    onehot = jnp.where(rowi == g_idx, 1.0, 0.0)
    incl = jnp.dot(onehot.astype(BF16), tri_ref[...], preferred_element_type=F32)
    pos = jnp.sum(onehot * incl, axis=0, keepdims=True) - 1.0
    for g in range(1, N_EXPERT_GROUPS):
        pos = pos + jnp.where(g_idx == float(g), starts[g].astype(F32), 0.0)
    sub = lax.broadcasted_iota(jnp.int32, (SORT_ROWS, TM), 0).astype(F32)
    perm = jnp.where(sub == pos, 1.0, 0.0).astype(BF16)
    pos_col = jnp.broadcast_to(pos, (LANES, TM)).T
    lane = lax.broadcasted_iota(jnp.int32, (TM, SORT_ROWS), 1).astype(F32)
    perm_t = jnp.where(lane == jnp.concatenate([pos_col] * (SORT_ROWS // LANES), axis=1), 1.0, 0.0).astype(BF16)

    xs_ref[:SORT_ROWS, :] = jnp.dot(perm, h_ref[...], preferred_element_type=F32).astype(BF16)
    xs_ref[SORT_ROWS:, :] = jnp.zeros((SORT_ROWS_PAD - SORT_ROWS, D_MODEL), BF16)
    hi, mid, lo = _split3(gate4_ref[...])
    packed = (hi.astype(F32) + pltpu.roll(mid.astype(F32), EXPERTS_PER_GROUP, 1)
              + pltpu.roll(lo.astype(F32), 2 * EXPERTS_PER_GROUP, 1)).astype(BF16)
    gsp = jnp.dot(perm, packed, preferred_element_type=F32)
    gs_ref[:SORT_ROWS, :] = (gsp + pltpu.roll(gsp, LANES - EXPERTS_PER_GROUP, 1)
                             + pltpu.roll(gsp, LANES - 2 * EXPERTS_PER_GROUP, 1))
    gs_ref[SORT_ROWS:, :] = jnp.zeros((SORT_ROWS_PAD - SORT_ROWS, LANES), F32)
    zs_ref[...] = jnp.zeros((SORT_ROWS_PAD, D_MODEL), BF16)

    def block_ffn(g, blk):
        row0 = pl.multiple_of(starts[g] + blk * RB, BF16_ROWS)
        xb = xs_ref[pl.ds(row0, RB), :]
        gb = gs_ref[pl.ds(row0, RB), :]
        cols = []
        for e in range(EXPERTS_PER_GROUP):
            a = jnp.dot(xb, w1_ref[g * EXPERTS_PER_GROUP + e], preferred_element_type=F32)
            b = jnp.dot(xb, w3_ref[g * EXPERTS_PER_GROUP + e], preferred_element_type=F32)
            cols.append((a / (1.0 + jnp.exp(-a)) * b * gb[:, e:e + 1]).astype(BF16))
        z = jnp.dot(jnp.concatenate(cols, axis=1), w2_ref[g], preferred_element_type=F32)
        zs_ref[pl.ds(row0, RB), :] = z.astype(BF16)

    for g in range(N_EXPERT_GROUPS):
        def extra_block(blk, carry, g=g):
            block_ffn(g, blk)
            return carry

        lax.fori_loop(1, (counts[g] + (RB - 1)) // RB, extra_block, 0)
    for g in range(N_EXPERT_GROUPS):
        block_ffn(g, 0)

    y = jnp.dot(perm_t, zs_ref[:SORT_ROWS, :], preferred_element_type=F32)
    out = x1_ref[...] + gate2 * y

    if final:
        octx_ref, olat_ref = out_refs
        out = _rms(out, fn_ref[...])
        olat_ref[...] = out

        @pl.when(i < N_CTX_TILES)
        def _():
            octx_ref[...] = out
    else:
        out_refs[0][...] = out


def _moe(l, counts, h2, route, gate4, tri, w1, w3, w2, x1, mod, final_norm, final):
    tile = pl.BlockSpec((TM, D_MODEL), lambda i, c: (i, 0))
    resident = pl.Buffered(1)
    if final:
        out_specs = [pl.BlockSpec((TM, D_MODEL), lambda i, c: (_ctx_tile(i), 0)),
                     pl.BlockSpec((TM, D_MODEL), lambda i, c: (_lat_tile(i), 0))]
        out_shape = [jax.ShapeDtypeStruct((T_CTX, D_MODEL), F32), jax.ShapeDtypeStruct((T_LAT, D_MODEL), F32)]
    else:
        out_specs = tile
        out_shape = jax.ShapeDtypeStruct((T_ALL, D_MODEL), F32)
    grid_spec = pltpu.PrefetchScalarGridSpec(
        num_scalar_prefetch=1,
        grid=(N_TILES,),
        in_specs=[
            tile,
            pl.BlockSpec((None, SUBLANES, TM), lambda i, c: (i, 0, 0)),
            pl.BlockSpec((TM, LANES), lambda i, c: (i, 0)),
            pl.BlockSpec((TM, TM), lambda i, c: (0, 0), pipeline_mode=resident),
            pl.BlockSpec((N_EXPERTS, D_MODEL, D_EXPERT), lambda i, c: (0, 0, 0), pipeline_mode=resident),
            pl.BlockSpec((N_EXPERTS, D_MODEL, D_EXPERT), lambda i, c: (0, 0, 0), pipeline_mode=resident),
            pl.BlockSpec((N_EXPERT_GROUPS, D_GROUP_HID, D_MODEL), lambda i, c: (0, 0, 0), pipeline_mode=resident),
            tile,
            pl.BlockSpec((None, N_MOD_PARTS, N_MOD_ROWS, D_MODEL), lambda i, c: (l, 0, 0, 0)),
            pl.BlockSpec((1, D_MODEL), lambda i, c: (0, 0)),
        ],
        out_specs=out_specs,
        scratch_shapes=[pltpu.VMEM((SORT_ROWS_PAD, D_MODEL), BF16),
                        pltpu.VMEM((SORT_ROWS_PAD, LANES), F32),
                        pltpu.VMEM((SORT_ROWS_PAD, D_MODEL), BF16)],
    )
    return pl.pallas_call(
        functools.partial(_moe_kernel, final=final),
        grid_spec=grid_spec,
        out_shape=out_shape,
        compiler_params=pltpu.CompilerParams(dimension_semantics=("arbitrary",),
                                             vmem_limit_bytes=MOE_VMEM_LIMIT),
        name=f"moe_l{l}",
    )(counts, h2, route, gate4, tri, w1, w3, w2, x1, mod, final_norm)


def _rope_tables():
    rows = DEC_SEQ // GRID_W
    row = np.repeat(np.arange(rows, dtype=np.float32), GRID_W)
    col = np.tile(np.arange(GRID_W, dtype=np.float32), rows)
    half = ROPE_AXIS_DIM // 2
    inv_freq = (1.0 / (np.float32(ROPE_THETA) ** (np.arange(half, dtype=np.float32) * np.float32(2.0)
                                                  / np.float32(ROPE_AXIS_DIM)))).astype(np.float32)
    ar = row[:, None] * inv_freq
    ac = col[:, None] * inv_freq
    cos64 = np.concatenate([np.cos(ar), np.cos(ar), np.cos(ac), np.cos(ac)], axis=1)
    sin64 = np.concatenate([-np.sin(ar), np.sin(ar), -np.sin(ac), np.sin(ac)], axis=1)
    cos_t = np.concatenate([np.ones((TM, LANES), np.float32), np.tile(cos64, (1, 2))], axis=0)
    sin_t = np.concatenate([np.zeros((TM, LANES), np.float32), np.tile(sin64, (1, 2))], axis=0)
    return jnp.asarray(cos_t, F32), jnp.asarray(sin_t, F32)


def _router_weights(router_g_w, router_g_b, router_e_w, router_e_b):
    zw = jnp.zeros((DEPTH, D_MODEL, SUBLANES - N_EXPERT_GROUPS), F32)
    zb = jnp.zeros((DEPTH, SUBLANES - N_EXPERT_GROUPS), F32)
    w_cols, b_cols = [router_g_w, zw], [router_g_b, zb]
    for g in range(N_EXPERT_GROUPS):
        w_cols += [router_e_w[:, g], zw]
        b_cols += [router_e_b[:, g], zb]
    used = (1 + N_EXPERT_GROUPS) * SUBLANES
    w_cols.append(jnp.zeros((DEPTH, D_MODEL, ROUTER_COLS - used), F32))
    b_cols.append(jnp.zeros((DEPTH, ROUTER_COLS - used), F32))
    w = jnp.concatenate(w_cols, axis=2)
    b = jnp.concatenate(b_cols, axis=1).reshape(DEPTH, 1, ROUTER_COLS)
    w_hi = w.astype(BF16)
    w_lo = (w - w_hi.astype(F32)).astype(BF16)
    return jnp.concatenate([w_hi, w_lo], axis=2), w_hi, b


def kernel(x_prompt, x_sample, cache_k, cache_v, c, c_ctx, w_ada, b_ada, norm1, w_in, q_norm, k_norm,
           sgu_w, sgu_b, conv_w, out_norm, w_out, norm2, router_g_w, router_g_b, router_e_w,
           router_e_b, moe_w1, moe_w3, moe_w2, final_norm):
    xs = (x_prompt.reshape(T_CTX, D_MODEL), x_sample.reshape(T_LAT, D_MODEL))

    cvecs = jnp.concatenate([c_ctx[None, :], c, jnp.zeros((N_MOD_ROWS - 1 - DEC_BATCH, D_MODEL), F32)], axis=0)
    mod = _modulation(cvecs.T, w_ada, b_ada)

    w_in_b = w_in.astype(BF16)
    w_out_b = w_out.astype(BF16)
    g1 = norm1.reshape(DEPTH, 1, D_MODEL)
    g2 = norm2.reshape(DEPTH, 1, D_MODEL)
    on = out_norm.reshape(DEPTH, 1, D_MIX)
    qg = jnp.tile(q_norm, (1, N_HEADS)).reshape(DEPTH, 1, D_ATTN)
    kg = jnp.tile(k_norm, (1, N_KV_HEADS)).reshape(DEPTH, 1, D_KV)
    cos_t, sin_t = _rope_tables()
    bq = jnp.asarray(np.kron(np.eye(N_HEADS), np.full((HEAD_DIM, HEAD_DIM), 1.0 / HEAD_DIM)), BF16)
    sgu_w_pairs = sgu_w.astype(BF16).reshape(DEPTH, N_SGU_GROUPS // 2, 2, CHUNK, CHUNK)
    sgu_w_pairs = jnp.concatenate([sgu_w_pairs[:, :, 0], sgu_w_pairs[:, :, 1]], axis=-1)
    sgu_bias = jnp.repeat(jnp.swapaxes(sgu_b, 1, 2), SGU_GROUP_DIM, axis=2)
    wr_cat, wr_hi, rb = _router_weights(router_g_w, router_g_b, router_e_w, router_e_b)
    fn = final_norm.reshape(1, D_MODEL)
    tri = jnp.asarray(np.triu(np.ones((TM, TM), np.float32)), BF16)
    ck = cache_k.reshape(DEC_BATCH, DEPTH, PAST_LEN, D_KV)
    cv = cache_v.reshape(DEC_BATCH, DEPTH, PAST_LEN, D_KV)

    ctx_ks, ctx_vs = [], []
    for l in range(DEPTH):
        qt, k, vt, k32, v32, sgu_n, cbz = _proj(l, xs, mod, g1, w_in_b, qg, kg, cos_t, sin_t, bq,
                                                sgu_w_pairs, sgu_bias, on)
        attn_ctx = _attn_ctx(l, qt, k, vt)
        attn_lat, w1, w3, w2 = _attn_lat(l, qt, k, vt, ck, cv, qg, kg, moe_w1, moe_w3, moe_w2)
        w2 = w2.reshape(N_EXPERT_GROUPS, D_GROUP_HID, D_MODEL)
        x1, h2, route, gate4, cnt = _merge(l, xs, attn_ctx, attn_lat, sgu_n, cbz, conv_w, on, w_out_b, mod, g2,
                                           wr_cat, wr_hi, rb)
        counts = cnt[:, :N_EXPERT_GROUPS, 0]
        out = _moe(l, counts, h2, route, gate4, tri, w1, w3, w2, x1, mod, fn, final=(l == DEPTH - 1))
        xs = (out,)
        ctx_ks.append(k32.reshape(BATCH, SEQ, N_KV_HEADS, HEAD_DIM))
        ctx_vs.append(v32.reshape(BATCH, SEQ, N_KV_HEADS, HEAD_DIM))

    y_prompt, y_sample = out
    return (y_prompt.reshape(BATCH, SEQ, D_MODEL), y_sample.reshape(DEC_BATCH, DEC_SEQ, D_MODEL),
            jnp.stack(ctx_ks, axis=1), jnp.stack(ctx_vs, axis=1))
```

```python
import functools
import math

import jax
import jax.numpy as jnp
import numpy as np
from jax import lax
from jax.experimental import pallas as pl
from jax.experimental.pallas import tpu as pltpu

D_MODEL = 1024
BATCH = 16
SEQ = 256
DEPTH = 2
DEC_BATCH = 2
DEC_SEQ = 4096
PAST_LEN = 512
GRID_W = 64
N_HEADS = 8
N_KV_HEADS = 2
HEAD_DIM = 64
ROPE_AXIS_DIM = HEAD_DIM // 2
ROPE_THETA = 10000.0
D_ATTN = N_HEADS * HEAD_DIM
D_KV = N_KV_HEADS * HEAD_DIM
CHUNK = 128
N_SGU_GROUPS = 4
SGU_GROUP_DIM = 64
D_SGU = N_SGU_GROUPS * SGU_GROUP_DIM
D_CONV = 256
D_MIX = D_ATTN + D_SGU + D_CONV
D_IN = D_ATTN + 2 * D_KV + 2 * D_SGU + 3 * D_CONV
N_EXPERT_GROUPS = 4
EXPERTS_PER_GROUP = 4
N_EXPERTS = N_EXPERT_GROUPS * EXPERTS_PER_GROUP
D_EXPERT = 256
D_GROUP_HID = EXPERTS_PER_GROUP * D_EXPERT
EPS = 1e-6

T_CTX = BATCH * SEQ
T_LAT = DEC_BATCH * DEC_SEQ
T_ALL = T_CTX + T_LAT
TM = 512
N_TILES = T_ALL // TM
N_CTX_TILES = T_CTX // TM
LAT_TILES_PER_BATCH = DEC_SEQ // TM
LANES = 128
SUBLANES = 8
BF16_ROWS = 16
N_MOD_ROWS = 8
N_MOD_PARTS = 6
ROUTER_COLS = 128
VMEM_LIMIT = 48 * 1024 * 1024

OFF_Q = 0
OFF_KV = D_ATTN
OFF_SGU = OFF_KV + 2 * D_KV
OFF_CONV = OFF_SGU + 2 * D_SGU

BF16 = jnp.bfloat16
F32 = jnp.float32
NEG_BIG = -1e30
Q_SCALE = HEAD_DIM ** -0.5 * math.log2(math.e)


def _params(n_grid_dims):
    return pltpu.CompilerParams(
        dimension_semantics=("arbitrary",) * n_grid_dims,
        vmem_limit_bytes=VMEM_LIMIT,
    )


def _mod_row_of_tile(i, tile=TM):
    n_ctx = T_CTX // tile
    return jnp.where(i < n_ctx, 0, 1 + (i - n_ctx) // (DEC_SEQ // tile))


def _mod_part(mod_ref, part, row):
    return mod_ref[part, pl.ds(row, 1), :]


def _rope_block_of_tile(i):
    return jnp.where(i < N_CTX_TILES, 0, 1 + (i - N_CTX_TILES) % LAT_TILES_PER_BATCH)


def _rms(x, gain):
    ms = jnp.mean(x * x, axis=-1, keepdims=True)
    return x * lax.rsqrt(ms + EPS) * gain


def _ctx_tile(i):
    return jnp.minimum(i, N_CTX_TILES - 1)


def _lat_tile(i):
    return jnp.maximum(i - N_CTX_TILES, 0)


MOD_PARTS_PER_STEP = 2
MOD_TN = MOD_PARTS_PER_STEP * D_MODEL


def _mod_kernel(ct_ref, w_ref, b_ref, o_ref):
    c = ct_ref[...]
    s = c / (1.0 + jnp.exp(-c))
    row = lax.broadcasted_iota(jnp.int32, (N_MOD_ROWS, D_MODEL), 0)
    for part in range(MOD_PARTS_PER_STEP):
        cols = slice(part * D_MODEL, (part + 1) * D_MODEL)
        w = w_ref[:, cols]
        out = jnp.zeros((N_MOD_ROWS, D_MODEL), F32)
        for r in range(1 + DEC_BATCH):
            acc = jnp.sum(w * s[:, r:r + 1], axis=0, keepdims=True)
            out = jnp.where(row == r, acc, out)
        o_ref[part] = out + b_ref[:, cols]


def _modulation(cvecs_t, w_ada, b_ada):
    n_col = N_MOD_PARTS * D_MODEL
    return pl.pallas_call(
        _mod_kernel,
        grid=(DEPTH, n_col // MOD_TN),
        in_specs=[
            pl.BlockSpec((D_MODEL, N_MOD_ROWS), lambda l, j: (0, 0)),
            pl.BlockSpec((None, D_MODEL, MOD_TN), lambda l, j: (l, 0, j)),
            pl.BlockSpec((None, 1, MOD_TN), lambda l, j: (l, 0, j)),
        ],
        out_specs=pl.BlockSpec((None, MOD_PARTS_PER_STEP, N_MOD_ROWS, D_MODEL), lambda l, j: (l, j, 0, 0)),
        out_shape=jax.ShapeDtypeStruct((DEPTH, N_MOD_PARTS, N_MOD_ROWS, D_MODEL), F32),
        compiler_params=_params(2),
        name="adaln_modulation",
    )(cvecs_t, w_ada, b_ada.reshape(DEPTH, 1, n_col))


def _mod_spec(l):
    return pl.BlockSpec((None, N_MOD_PARTS, N_MOD_ROWS, D_MODEL), lambda *idx: (l, 0, 0, 0))


def _swap16(x):
    lane = lax.broadcasted_iota(jnp.int32, x.shape, 1)
    up = pltpu.roll(x, 16, 1)
    down = pltpu.roll(x, LANES - 16, 1)
    return jnp.where((lane & 16) != 0, up, down)


def _rope(x, cos_t, sin_t):
    cols = []
    for j in range(x.shape[1] // LANES):
        xc = x[:, j * LANES:(j + 1) * LANES]
        cols.append(xc * cos_t + _swap16(xc) * sin_t)
    return cols[0] if len(cols) == 1 else jnp.concatenate(cols, axis=1)


def _proj_kernel(*refs, split_x):
    n_x = 2 if split_x else 1
    x_refs, refs, (scr_even, scr_odd) = refs[:n_x], refs[n_x:-2], refs[-2:]
    step = functools.partial(_proj_step, x_refs, refs, split_x)
    i = pl.program_id(0)

    @pl.when(i == 0)
    def _():
        scr_odd[...] = jnp.zeros((TM, D_IN), F32)

    last = i == N_TILES

    @pl.when(jnp.logical_and(i % 2 == 0, jnp.logical_not(last)))
    def _():
        step(scr_even, scr_odd)

    @pl.when(jnp.logical_and(i % 2 == 1, jnp.logical_not(last)))
    def _():
        step(scr_odd, scr_even)

    @pl.when(last)
    def _():
        step(None, scr_even if N_TILES % 2 == 1 else scr_odd)


def _proj_step(x_refs, refs, split_x, new_ref, proj):
    (mod_ref, g1_ref, w_ref, qg_ref, kg_ref, cos_ref, sin_ref, bq_ref, sw_ref, sb_ref, on_ref,
     qt_ref, k_ref, vt_ref, k32_ref, v32_ref, sgu_ref, cbz_ref) = refs
    i = pl.program_id(0)
    cur = _stage1_tile(i)

    is_ctx = i <= N_CTX_TILES
    cos_t = cos_ref[...]
    sin_t = sin_ref[...]

    q = proj[:, OFF_Q:OFF_Q + D_ATTN]
    q2 = (q * q).astype(BF16)
    half = D_ATTN // 2
    qms = jnp.concatenate(
        [jnp.dot(q2[:, :half], bq_ref[:half, :half], preferred_element_type=F32),
         jnp.dot(q2[:, half:], bq_ref[half:, half:], preferred_element_type=F32)], axis=1)
    qn = q * lax.rsqrt(qms + EPS) * qg_ref[...]
    qr = _rope(qn, cos_t, sin_t) * Q_SCALE
    qt_ref[...] = qr.T.astype(BF16)

    k = proj[:, OFF_KV:OFF_KV + D_KV]
    v = proj[:, OFF_KV + D_KV:OFF_KV + 2 * D_KV]
    kms = jnp.dot((k * k).astype(BF16), bq_ref[:D_KV, :D_KV], preferred_element_type=F32)
    kn = k * lax.rsqrt(kms + EPS) * kg_ref[...]
    k_ref[...] = _rope(kn, cos_t, sin_t).astype(BF16)
    vt_ref[...] = v.T.astype(BF16)

    su = proj[:, OFF_SGU:OFF_SGU + D_SGU]
    sv = proj[:, OFF_SGU + D_SGU:OFF_SGU + 2 * D_SGU]
    n_chunks = TM // CHUNK
    sv_wide = jnp.concatenate([sv[n * CHUNK:(n + 1) * CHUNK, :] for n in range(n_chunks)], axis=1)
    grp = lax.broadcasted_iota(jnp.int32, (CHUNK, n_chunks * D_SGU), 1) // SGU_GROUP_DIM % N_SGU_GROUPS
    mixed_wide = jnp.zeros((CHUNK, n_chunks * D_SGU), F32)
    for pair in range(N_SGU_GROUPS // 2):
        rhs = jnp.concatenate(
            [jnp.where(grp == 2 * pair, sv_wide, 0.0), jnp.where(grp == 2 * pair + 1, sv_wide, 0.0)],
            axis=0).astype(BF16)
        mixed_wide = mixed_wide + jnp.dot(sw_ref[pair], rhs, preferred_element_type=F32)
    on_sgu = on_ref[:, D_ATTN:D_ATTN + D_SGU]
    for n in range(n_chunks):
        sgu = su[n * CHUNK:(n + 1) * CHUNK, :] * (mixed_wide[:, n * D_SGU:(n + 1) * D_SGU] + sb_ref[...])
        sgu_ref[n * CHUNK:(n + 1) * CHUNK, :] = _rms(sgu, on_sgu).astype(BF16)

    cbz_ref[:, :D_CONV] = proj[:, OFF_CONV:OFF_CONV + D_CONV]
    cbz_ref[:, D_CONV:] = (proj[:, OFF_CONV + D_CONV:OFF_CONV + 2 * D_CONV]
                           * proj[:, OFF_CONV + 2 * D_CONV:OFF_CONV + 3 * D_CONV])

    if new_ref is not None:
        x = jnp.where(cur < N_CTX_TILES, x_refs[0][...], x_refs[1][...]) if split_x else x_refs[0][...]
        mrow = _mod_row_of_tile(cur)
        h = _rms(x, g1_ref[...] * (1.0 + _mod_part(mod_ref, 1, mrow))) + _mod_part(mod_ref, 0, mrow)
        new_ref[...] = jnp.dot(h.astype(BF16), w_ref[...], preferred_element_type=F32)

    @pl.when(is_ctx)
    def _():
        k32_ref[...] = kn
        v32_ref[...] = v


def _x_specs(xs, tile_of_step=lambda i: i):
    if len(xs) == 1:
        return [pl.BlockSpec((TM, D_MODEL), lambda i: (tile_of_step(i), 0))]
    return [pl.BlockSpec((TM, D_MODEL), lambda i: (_ctx_tile(tile_of_step(i)), 0)),
            pl.BlockSpec((TM, D_MODEL), lambda i: (_lat_tile(tile_of_step(i)), 0))]


def _stage1_tile(i):
    return jnp.minimum(i, N_TILES - 1)


def _stage2_tile(i):
    return jnp.maximum(i - 1, 0)


def _proj(l, xs, mod, g1, w_in, qg, kg, cos_t, sin_t, bq, sgu_w, sgu_b, on):
    lay = lambda *rest: (lambda i: (l,) + rest)
    s2 = _stage2_tile
    return pl.pallas_call(
        functools.partial(_proj_kernel, split_x=len(xs) == 2),
        grid=(N_TILES + 1,),
        in_specs=_x_specs(xs, _stage1_tile) + [
            _mod_spec(l),
            pl.BlockSpec((None, 1, D_MODEL), lay(0, 0)),
            pl.BlockSpec((None, D_MODEL, D_IN), lay(0, 0)),
            pl.BlockSpec((None, 1, D_ATTN), lay(0, 0)),
            pl.BlockSpec((None, 1, D_KV), lay(0, 0)),
            pl.BlockSpec((TM, LANES), lambda i: (_rope_block_of_tile(s2(i)), 0)),
            pl.BlockSpec((TM, LANES), lambda i: (_rope_block_of_tile(s2(i)), 0)),
            pl.BlockSpec((D_ATTN, D_ATTN), lambda i: (0, 0)),
            pl.BlockSpec((None, N_SGU_GROUPS // 2, CHUNK, 2 * CHUNK), lay(0, 0, 0)),
            pl.BlockSpec((None, CHUNK, D_SGU), lay(0, 0)),
            pl.BlockSpec((None, 1, D_MIX), lay(0, 0)),
        ],
        out_specs=[
            pl.BlockSpec((None, D_ATTN, TM), lambda i: (s2(i), 0, 0)),
            pl.BlockSpec((TM, D_KV), lambda i: (s2(i), 0)),
            pl.BlockSpec((None, D_KV, TM), lambda i: (s2(i), 0, 0)),
            pl.BlockSpec((TM, D_KV), lambda i: (_ctx_tile(s2(i)), 0)),
            pl.BlockSpec((TM, D_KV), lambda i: (_ctx_tile(s2(i)), 0)),
            pl.BlockSpec((TM, D_SGU), lambda i: (s2(i), 0)),
            pl.BlockSpec((TM, 2 * D_CONV), lambda i: (s2(i), 0)),
        ],
        out_shape=[
            jax.ShapeDtypeStruct((N_TILES, D_ATTN, TM), BF16),
            jax.ShapeDtypeStruct((T_ALL, D_KV), BF16),
            jax.ShapeDtypeStruct((N_TILES, D_KV, TM), BF16),
            jax.ShapeDtypeStruct((T_CTX, D_KV), F32),
            jax.ShapeDtypeStruct((T_CTX, D_KV), F32),
            jax.ShapeDtypeStruct((T_ALL, D_SGU), BF16),
            jax.ShapeDtypeStruct((T_ALL, 2 * D_CONV), F32),
        ],
        scratch_shapes=[pltpu.VMEM((TM, D_IN), F32), pltpu.VMEM((TM, D_IN), F32)],
        compiler_params=_params(1),
        name=f"proj_l{l}",
    )(*xs, mod, g1, w_in, qg, kg, cos_t, sin_t, bq, sgu_w, sgu_b, on)


ACC_ROWS = HEAD_DIM + BF16_ROWS
N_PAIRS = N_HEADS // 2
PAIRS_PER_KV = N_PAIRS // N_KV_HEADS


def _pair_queries(qt, kv_idx, tq):
    qf = qt.astype(F32)
    mine = lax.broadcasted_iota(jnp.int32, (2 * HEAD_DIM, tq), 0) // HEAD_DIM == kv_idx
    qe = qf[:HEAD_DIM]
    qo = qf[HEAD_DIM:]
    rhs = jnp.concatenate(
        [jnp.where(mine, jnp.concatenate([qe, qe], axis=0), 0.0),
         jnp.where(mine, jnp.concatenate([qo, qo], axis=0), 0.0)], axis=1)
    return rhs.astype(BF16)


def _with_ones(vt_c):
    return jnp.concatenate([vt_c, jnp.ones((BF16_ROWS, vt_c.shape[1]), BF16)], axis=0)


def _flash_pairs(rhs_list, chunk_lists, tq, shift=None, query_block=None):
    n_pairs, n_chunks = len(rhs_list), len(chunk_lists[0])
    qb = 2 * tq if query_block is None else query_block
    streams = [(s, j) for s in range(n_pairs) for j in range(2 * tq // qb)]
    rhs = {(s, j): rhs_list[s][:, j * qb:(j + 1) * qb] for s, j in streams}
    scores = lambda t, c: jnp.dot(chunk_lists[t[0]][c][0](), rhs[t], preferred_element_type=F32)
    m = {t: jnp.full((1, qb), NEG_BIG, F32) for t in streams}
    acc = {t: jnp.zeros((ACC_ROWS, qb), F32) for t in streams}
    st = {t: scores(t, 0) for t in streams}
    for c in range(n_chunks):
        for t in streams:
            vt_c = chunk_lists[t[0]][c][1]()
            if shift is None:
                m_new = jnp.maximum(m[t], jnp.max(st[t], axis=0, keepdims=True))
            st_next = scores(t, c + 1) if c + 1 < n_chunks else None
            if shift is None:
                alpha = jnp.exp2(m[t] - m_new)
                p = jnp.exp2(st[t] - m_new).astype(BF16)
                acc[t] = alpha * acc[t] + jnp.dot(vt_c, p, preferred_element_type=F32)
                m[t] = m_new
            else:
                p = jnp.exp2(st[t] - shift).astype(BF16)
                acc[t] = acc[t] + jnp.dot(vt_c, p, preferred_element_type=F32)
            st[t] = st_next
    outs = []
    for s in range(n_pairs):
        a = jnp.concatenate([acc[t] for t in streams if t[0] == s], axis=1)
        o = a[:HEAD_DIM] * (1.0 / a[HEAD_DIM:HEAD_DIM + 1])
        outs.append(jnp.concatenate([o[:, :tq], o[:, tq:]], axis=0))
    return outs


def _attn_ctx_kernel(qt_ref, k_ref, vt_ref, o_ref):
    tq = SEQ
    rhs_list, chunk_lists = [], []
    for pair in range(N_PAIRS):
        kv_idx = pair // PAIRS_PER_KV
        rhs_list.append(_pair_queries(qt_ref[pair * LANES:(pair + 1) * LANES, :], kv_idx, tq))
        chunk_lists.append([(lambda: k_ref[...],
                             lambda kv_idx=kv_idx: _with_ones(vt_ref[kv_idx * HEAD_DIM:(kv_idx + 1) * HEAD_DIM, :]))])
    for pair, o_t in enumerate(_flash_pairs(rhs_list, chunk_lists, tq)):
        o_ref[:, pair * LANES:(pair + 1) * LANES] = o_t.T.astype(BF16)


def _attn_ctx(l, qt, k, vt):
    halves = TM // SEQ
    return pl.pallas_call(
        _attn_ctx_kernel,
        grid=(BATCH,),
        in_specs=[
            pl.BlockSpec((None, D_ATTN, SEQ), lambda b: (b // halves, 0, b % halves)),
            pl.BlockSpec((SEQ, D_KV), lambda b: (b, 0)),
            pl.BlockSpec((None, D_KV, SEQ), lambda b: (b // halves, 0, b % halves)),
        ],
        out_specs=pl.BlockSpec((SEQ, D_ATTN), lambda b: (b, 0)),
        out_shape=jax.ShapeDtypeStruct((T_CTX, D_ATTN), BF16),
        compiler_params=_params(1),
        name=f"attn_ctx_l{l}",
    )(qt, k, vt)


KC = 256
LAT_PAIRS_PER_STEP = 2
LAT_QUERY_BLOCK = 256


MAX_SHIFT = 60.0
BOUND_SLACK = 1.05


def _score_bound_sq(qg_ref, kg_ref, ck):
    qg = qg_ref[...]
    kg = kg_ref[...]
    gq2 = jnp.max(qg * qg, axis=1, keepdims=True)
    k2 = HEAD_DIM * jnp.max(kg * kg, axis=1, keepdims=True)
    if ck is not None:
        sq = ck * ck
        first = lax.broadcasted_iota(jnp.int32, sq.shape, 1) < HEAD_DIM
        n0 = jnp.sum(jnp.where(first, sq, 0.0), axis=1, keepdims=True)
        n1 = jnp.sum(jnp.where(first, 0.0, sq), axis=1, keepdims=True)
        k2 = jnp.maximum(k2, jnp.max(jnp.maximum(n0, n1), axis=0, keepdims=True))
    return (BOUND_SLACK * Q_SCALE) ** 2 * HEAD_DIM * gq2 * k2


def _run_with_score_bound(run, bound):
    small = bound <= MAX_SHIFT

    @pl.when(small)
    def _():
        run(bound)

    @pl.when(jnp.logical_not(small))
    def _():
        run(None)


def _flash_pair_rolled(rhs, k_ref, vt_ref, v_row, tail_chunks, tq):
    def update(carry, k_c, vt_c):
        m, acc = carry
        st = jnp.dot(k_c, rhs, preferred_element_type=F32)
        m_new = jnp.maximum(m, jnp.max(st, axis=0, keepdims=True))
        p = jnp.exp2(st - m_new).astype(BF16)
        return m_new, jnp.exp2(m - m_new) * acc + jnp.dot(vt_c, p, preferred_element_type=F32)

    def body(c, carry):
        k_c = k_ref[pl.ds(pl.multiple_of(c * TM, TM), TM), :]
        return update(carry, k_c, _with_ones(vt_ref[c, pl.ds(v_row, HEAD_DIM), :]))

    carry = (jnp.full((1, 2 * tq), NEG_BIG, F32), jnp.zeros((ACC_ROWS, 2 * tq), F32))
    carry = lax.fori_loop(0, DEC_SEQ // TM, body, carry)
    for get_k, get_vt in tail_chunks:
        carry = update(carry, get_k(), get_vt())
    acc = carry[1]
    o = acc[:HEAD_DIM] * (1.0 / acc[HEAD_DIM:HEAD_DIM + 1])
    return jnp.concatenate([o[:, :tq], o[:, tq:]], axis=0)


def _attn_lat_kernel(qt_ref, k_ref, vt_ref, ck_ref, cv_ref, qg_ref, kg_ref, w1_ref, w3_ref, w2_ref,
                     o_ref, w1b_ref, w3b_ref, w2b_ref, bound_ref):
    tq = TM

    def run(shift):
        w1b_ref[...] = w1_ref[...].astype(BF16)
        w3b_ref[...] = w3_ref[...].astype(BF16)
        w2b_ref[...] = w2_ref[...].astype(BF16)
        cvt = cv_ref[...].T
        rhs_list, chunk_lists, v_rows = [], [], []
        for j in range(LAT_PAIRS_PER_STEP):
            kv_idx = (pl.program_id(2) * LAT_PAIRS_PER_STEP + j) // PAIRS_PER_KV
            rhs_list.append(_pair_queries(qt_ref[j * LANES:(j + 1) * LANES, :], kv_idx, tq))
            v_row = pl.multiple_of(kv_idx * HEAD_DIM, HEAD_DIM)
            v_rows.append(v_row)

            def latent_chunk(c, v_row=v_row):
                tile, lane0 = (c * KC) // TM, (c * KC) % TM
                return (lambda: k_ref[c * KC:(c + 1) * KC, :],
                        lambda: _with_ones(vt_ref[tile, pl.ds(v_row, HEAD_DIM), lane0:lane0 + KC]))

            def cached_chunk(c, kv_idx=kv_idx):
                return (lambda: ck_ref[c * KC:(c + 1) * KC, :].astype(BF16),
                        lambda: _with_ones(jnp.where(kv_idx == 0, cvt[:HEAD_DIM, c * KC:(c + 1) * KC],
                                                     cvt[HEAD_DIM:, c * KC:(c + 1) * KC]).astype(BF16)))

            chunks = [latent_chunk(c) for c in range(DEC_SEQ // KC)]
            chunks += [cached_chunk(c) for c in range(PAST_LEN // KC)]
            chunk_lists.append(chunks)
        if shift is None:
            outs = [_flash_pair_rolled(rhs_list[j], k_ref, vt_ref, v_rows[j], chunk_lists[j][DEC_SEQ // KC:], tq)
                    for j in range(LAT_PAIRS_PER_STEP)]
        else:
            outs = _flash_pairs(rhs_list, chunk_lists, tq, shift, query_block=LAT_QUERY_BLOCK)
        for j, o_t in enumerate(outs):
            o_ref[:, j * LANES:(j + 1) * LANES] = o_t.T.astype(BF16)

    @pl.when(jnp.logical_and(pl.program_id(1) == 0, pl.program_id(2) == 0))
    def _():
        bound_ref[0] = jnp.sqrt(_score_bound_sq(qg_ref, kg_ref, ck_ref[...]))[0, 0]

    _run_with_score_bound(run, bound_ref[0])


def _attn_lat(l, qt, k, vt, cache_k, cache_v, qg, kg, moe_w1, moe_w3, moe_w2):
    n_q = DEC_SEQ // TM
    first = N_CTX_TILES
    ctx_blocks = T_CTX // DEC_SEQ
    width = LAT_PAIRS_PER_STEP * LANES
    n_p = N_PAIRS // LAT_PAIRS_PER_STEP
    per_expert = DEC_BATCH * n_q * n_p // N_EXPERTS
    step = lambda b, i, p: (b * n_q + i) * n_p + p
    w_in_spec = lambda rows, cols: pl.BlockSpec(
        (None, None, rows // per_expert, cols),
        lambda b, i, p: (l, step(b, i, p) // per_expert, step(b, i, p) % per_expert, 0))
    w_out_spec = lambda rows, cols: pl.BlockSpec(
        (None, rows // per_expert, cols),
        lambda b, i, p: (step(b, i, p) // per_expert, step(b, i, p) % per_expert, 0))
    return pl.pallas_call(
        _attn_lat_kernel,
        grid=(DEC_BATCH, n_q, n_p),
        in_specs=[
            pl.BlockSpec((None, width, TM), lambda b, i, p: (first + b * n_q + i, p, 0)),
            pl.BlockSpec((DEC_SEQ, D_KV), lambda b, i, p: (ctx_blocks + b, 0)),
            pl.BlockSpec((n_q, D_KV, TM), lambda b, i, p: (ctx_blocks + b, 0, 0)),
            pl.BlockSpec((None, None, PAST_LEN, D_KV), lambda b, i, p: (b, l, 0, 0)),
            pl.BlockSpec((None, None, PAST_LEN, D_KV), lambda b, i, p: (b, l, 0, 0)),
            pl.BlockSpec((None, 1, D_ATTN), lambda b, i, p: (l, 0, 0)),
            pl.BlockSpec((None, 1, D_KV), lambda b, i, p: (l, 0, 0)),
            w_in_spec(D_MODEL, D_EXPERT),
            w_in_spec(D_MODEL, D_EXPERT),
            w_in_spec(D_EXPERT, D_MODEL),
        ],
        out_specs=[
            pl.BlockSpec((TM, width), lambda b, i, p: (b * n_q + i, p)),
            w_out_spec(D_MODEL, D_EXPERT),
            w_out_spec(D_MODEL, D_EXPERT),
            w_out_spec(D_EXPERT, D_MODEL),
        ],
        out_shape=[
            jax.ShapeDtypeStruct((T_LAT, D_ATTN), BF16),
            jax.ShapeDtypeStruct((N_EXPERTS, D_MODEL, D_EXPERT), BF16),
            jax.ShapeDtypeStruct((N_EXPERTS, D_MODEL, D_EXPERT), BF16),
            jax.ShapeDtypeStruct((N_EXPERTS, D_EXPERT, D_MODEL), BF16),
        ],
        scratch_shapes=[pltpu.SMEM((1,), F32)],
        compiler_params=_params(3),
        name=f"attn_lat_l{l}",
    )(qt, k, vt, cache_k, cache_v, qg, kg, moe_w1, moe_w3, moe_w2)


def _first_max_index(vals, valid, rowf):
    masked = jnp.where(valid, vals, -jnp.inf)
    mx = jnp.max(masked, axis=0, keepdims=True)
    idx = jnp.min(jnp.where(masked == mx, rowf, float(SUBLANES)), axis=0, keepdims=True)
    return mx, idx


def _merge_kernel(*refs, split_x):
    n_x = 2 if split_x else 1
    x_refs, refs = refs[:n_x], refs[n_x:]
    (actx_ref, alat_ref, sgu_ref, cbz_ref, prev_ref, next_ref, cw_ref, on_ref,
     wo_ref, mod_ref, g2_ref, wr_ref, wrhi_ref, rb_ref, x1_ref, h2_ref, route_ref, gate4_ref, cnt_ref) = refs
    cur = pl.program_id(0)
    is_ctx = cur < N_CTX_TILES
    mrow = _mod_row_of_tile(cur)
    on = on_ref[...]

    cbz = cbz_ref[...]
    cb = cbz[:, :D_CONV]
    z = cbz[:, D_CONV:]
    seq_mask = jnp.where(is_ctx, SEQ - 1, DEC_SEQ - 1)
    row = lax.broadcasted_iota(jnp.int32, (TM, D_CONV), 0)
    pos = (cur * TM + row) & seq_mask
    z_prev = jnp.where(row == 0, prev_ref[SUBLANES - 1:SUBLANES, D_CONV:], pltpu.roll(z, 1, 0))
    z_next = jnp.where(row == TM - 1, next_ref[0:1, D_CONV:], pltpu.roll(z, TM - 1, 0))
    z_prev = jnp.where(pos == 0, 0.0, z_prev)
    z_next = jnp.where(pos == seq_mask, 0.0, z_next)
    cw = cw_ref[...]
    conv = cb * (z_prev * cw[0:1, :] + z * cw[1:2, :] + z_next * cw[2:3, :])
    conv_n = _rms(conv, on[:, D_ATTN + D_SGU:]).astype(BF16)

    attn = jnp.where(is_ctx, actx_ref[...], alat_ref[...]).astype(F32)
    attn_n = _rms(attn, on[:, :D_ATTN]).astype(BF16)
    merged = jnp.concatenate([attn_n, sgu_ref[...], conv_n], axis=1)
    x = jnp.where(is_ctx, x_refs[0][...], x_refs[1][...]) if split_x else x_refs[0][...]
    x1 = x + _mod_part(mod_ref, 2, mrow) * jnp.dot(merged, wo_ref[...], preferred_element_type=F32)
    x1_ref[...] = x1
    _route_tile(x1, mrow, mod_ref, g2_ref, wr_ref, wrhi_ref, rb_ref, h2_ref, route_ref, gate4_ref, cnt_ref)


def _route_tile(x1, mrow, mod_ref, g2_ref, wr_ref, wrhi_ref, rb_ref, h2_ref, route_ref, gate4_ref, cnt_ref):
    n = x1.shape[0]
    h2 = _rms(x1, g2_ref[...] * (1.0 + _mod_part(mod_ref, 4, mrow))) + _mod_part(mod_ref, 3, mrow)
    hb = h2.astype(BF16)
    h2_ref[...] = hb

    h_lo = (h2 - hb.astype(F32)).astype(BF16)
    a = jnp.dot(hb, wr_ref[...], preferred_element_type=F32)
    b = jnp.dot(h_lo, wrhi_ref[...], preferred_element_type=F32)
    logits = a[:, :ROUTER_COLS] + a[:, ROUTER_COLS:] + b + rb_ref[...]
    lt = logits.T

    rowi = lax.broadcasted_iota(jnp.int32, (SUBLANES, n), 0).astype(F32)
    valid = rowi < float(N_EXPERT_GROUPS)
    g_log = lt[0:SUBLANES]
    g_max, g_idx = _first_max_index(g_log, valid, rowi)
    p_g = 1.0 / jnp.sum(jnp.where(valid, jnp.exp(g_log - g_max), 0.0), axis=0, keepdims=True)
    e_log = jnp.zeros((SUBLANES, n), F32)
    for g in range(N_EXPERT_GROUPS):
        e_log = jnp.where(g_idx == g, lt[(g + 1) * SUBLANES:(g + 2) * SUBLANES], e_log)
    e_max, i1 = _first_max_index(e_log, valid, rowi)
    e_exp = jnp.where(valid, jnp.exp(e_log - e_max), 0.0)
    e_prob = e_exp / jnp.sum(e_exp, axis=0, keepdims=True)
    v1 = jnp.max(e_prob, axis=0, keepdims=True)
    rest = jnp.logical_and(valid, rowi != i1)
    v2, i2 = _first_max_index(e_prob, rest, rowi)
    denom = v1 + v2
    w_sel = p_g * jnp.where(rowi == i1, v1 / denom, jnp.where(rowi == i2, v2 / denom, 0.0))
    route_ref[...] = jnp.broadcast_to(g_idx, (SUBLANES, n))
    pad = jnp.zeros((LANES - SUBLANES, n), F32)
    gate4_ref[...] = jnp.concatenate([w_sel, pad], axis=0).T
    counts = jnp.sum(jnp.where(rowi == g_idx, 1.0, 0.0), axis=1, keepdims=True)
    cnt_ref[...] = jnp.broadcast_to(counts, (SUBLANES, LANES)).astype(jnp.int32)


def _merge(l, xs, attn_ctx, attn_lat, sgu_n, cbz, conv_w, on, w_out, mod, g2, wr_cat, wr_hi, rb):
    lay = lambda *rest: (lambda i: (l,) + rest)
    rows8 = TM // SUBLANES
    return pl.pallas_call(
        functools.partial(_merge_kernel, split_x=len(xs) == 2),
        grid=(N_TILES,),
        in_specs=_x_specs(xs) + [
            pl.BlockSpec((TM, D_ATTN), lambda i: (_ctx_tile(i), 0)),
            pl.BlockSpec((TM, D_ATTN), lambda i: (_lat_tile(i), 0)),
            pl.BlockSpec((TM, D_SGU), lambda i: (i, 0)),
            pl.BlockSpec((TM, 2 * D_CONV), lambda i: (i, 0)),
            pl.BlockSpec((SUBLANES, 2 * D_CONV), lambda i: (jnp.maximum(i * rows8 - 1, 0), 0)),
            pl.BlockSpec((SUBLANES, 2 * D_CONV),
                         lambda i: (jnp.minimum((i + 1) * rows8, T_ALL // SUBLANES - 1), 0)),
            pl.BlockSpec((None, 3, D_CONV), lay(0, 0)),
            pl.BlockSpec((None, 1, D_MIX), lay(0, 0)),
            pl.BlockSpec((None, D_MIX, D_MODEL), lay(0, 0)),
            _mod_spec(l),
            pl.BlockSpec((None, 1, D_MODEL), lay(0, 0)),
            pl.BlockSpec((None, D_MODEL, 2 * ROUTER_COLS), lay(0, 0)),
            pl.BlockSpec((None, D_MODEL, ROUTER_COLS), lay(0, 0)),
            pl.BlockSpec((None, 1, ROUTER_COLS), lay(0, 0)),
        ],
        out_specs=[
            pl.BlockSpec((TM, D_MODEL), lambda i: (i, 0)),
            pl.BlockSpec((TM, D_MODEL), lambda i: (i, 0)),
            pl.BlockSpec((None, SUBLANES, TM), lambda i: (i, 0, 0)),
            pl.BlockSpec((TM, LANES), lambda i: (i, 0)),
            pl.BlockSpec((None, SUBLANES, LANES), lambda i: (i, 0, 0)),
        ],
        out_shape=[
            jax.ShapeDtypeStruct((T_ALL, D_MODEL), F32),
            jax.ShapeDtypeStruct((T_ALL, D_MODEL), BF16),
            jax.ShapeDtypeStruct((N_TILES, SUBLANES, TM), F32),
            jax.ShapeDtypeStruct((T_ALL, LANES), F32),
            jax.ShapeDtypeStruct((N_TILES, SUBLANES, LANES), jnp.int32),
        ],
        compiler_params=_params(1),
        name=f"merge_l{l}",
    )(*xs, attn_ctx, attn_lat, sgu_n, cbz, cbz, cbz, conv_w, on, w_out, mod, g2, wr_cat, wr_hi, rb)


RB = 144
SORT_ROWS = TM + LANES
SORT_ROWS_PAD = -(-(SORT_ROWS + RB) // BF16_ROWS) * BF16_ROWS
MOE_VMEM_LIMIT = 56 * 1024 * 1024


def _split3(x):
    hi = x.astype(BF16)
    r1 = x - hi.astype(F32)
    mid = r1.astype(BF16)
    lo = (r1 - mid.astype(F32)).astype(BF16)
    return hi, mid, lo


def _moe_kernel(cnt_ref, h_ref, route_ref, gate4_ref, tri_ref, w1_ref, w3_ref, w2_ref, x1_ref, mod_ref, fn_ref,
                *refs, final):
    out_refs, (xs_ref, gs_ref, zs_ref) = refs[:-3], refs[-3:]
    i = pl.program_id(0)
    gate2 = _mod_part(mod_ref, 5, _mod_row_of_tile(i))
    counts = [cnt_ref[i, g] for g in range(N_EXPERT_GROUPS)]
    starts = [jnp.int32(0)]
    for g in range(N_EXPERT_GROUPS - 1):
        starts.append(starts[-1] + (counts[g] + (BF16_ROWS - 1)) // BF16_ROWS * BF16_ROWS)

    g_idx = route_ref[0:1, :]
    rowi = lax.broadcasted_iota(jnp.int32, (SUBLANES, TM), 0).astype(F32)
    onehot = jnp.where(rowi == g_idx, 1.0, 0.0)
    incl = jnp.dot(onehot.astype(BF16), tri_ref[...], preferred_element_type=F32)
    pos = jnp.sum(onehot * incl, axis=0, keepdims=True) - 1.0
    for g in range(1, N_EXPERT_GROUPS):
        pos = pos + jnp.where(g_idx == float(g), starts[g].astype(F32), 0.0)
    sub = lax.broadcasted_iota(jnp.int32, (SORT_ROWS, TM), 0).astype(F32)
    perm = jnp.where(sub == pos, 1.0, 0.0).astype(BF16)
    pos_col = jnp.broadcast_to(pos, (LANES, TM)).T
    lane = lax.broadcasted_iota(jnp.int32, (TM, SORT_ROWS), 1).astype(F32)
    perm_t = jnp.where(lane == jnp.concatenate([pos_col] * (SORT_ROWS // LANES), axis=1), 1.0, 0.0).astype(BF16)

    xs_ref[:SORT_ROWS, :] = jnp.dot(perm, h_ref[...], preferred_element_type=F32).astype(BF16)
    xs_ref[SORT_ROWS:, :] = jnp.zeros((SORT_ROWS_PAD - SORT_ROWS, D_MODEL), BF16)
    hi, mid, lo = _split3(gate4_ref[...])
    packed = (hi.astype(F32) + pltpu.roll(mid.astype(F32), EXPERTS_PER_GROUP, 1)
              + pltpu.roll(lo.astype(F32), 2 * EXPERTS_PER_GROUP, 1)).astype(BF16)
    gsp = jnp.dot(perm, packed, preferred_element_type=F32)
    gs_ref[:SORT_ROWS, :] = (gsp + pltpu.roll(gsp, LANES - EXPERTS_PER_GROUP, 1)
                             + pltpu.roll(gsp, LANES - 2 * EXPERTS_PER_GROUP, 1))
    gs_ref[SORT_ROWS:, :] = jnp.zeros((SORT_ROWS_PAD - SORT_ROWS, LANES), F32)
    zs_ref[...] = jnp.zeros((SORT_ROWS_PAD, D_MODEL), BF16)

    def block_ffn(g, blk):
        row0 = pl.multiple_of(starts[g] + blk * RB, BF16_ROWS)
        xb = xs_ref[pl.ds(row0, RB), :]
        gb = gs_ref[pl.ds(row0, RB), :]
        cols = []
        for e in range(EXPERTS_PER_GROUP):
            a = jnp.dot(xb, w1_ref[g * EXPERTS_PER_GROUP + e], preferred_element_type=F32)
            b = jnp.dot(xb, w3_ref[g * EXPERTS_PER_GROUP + e], preferred_element_type=F32)
            cols.append((a / (1.0 + jnp.exp(-a)) * b * gb[:, e:e + 1]).astype(BF16))
        z = jnp.dot(jnp.concatenate(cols, axis=1), w2_ref[g], preferred_element_type=F32)
        zs_ref[pl.ds(row0, RB), :] = z.astype(BF16)

    for g in range(N_EXPERT_GROUPS):
        def extra_block(blk, carry, g=g):
            block_ffn(g, blk)
            return carry

        lax.fori_loop(1, (counts[g] + (RB - 1)) // RB, extra_block, 0)
    for g in range(N_EXPERT_GROUPS):
        block_ffn(g, 0)

    y = jnp.dot(perm_t, zs_ref[:SORT_ROWS, :], preferred_element_type=F32)
    out = x1_ref[...] + gate2 * y

    if final:
        octx_ref, olat_ref = out_refs
        out = _rms(out, fn_ref[...])
        olat_ref[...] = out

        @pl.when(i < N_CTX_TILES)
        def _():
            octx_ref[...] = out
    else:
        out_refs[0][...] = out


def _moe(l, counts, h2, route, gate4, tri, w1, w3, w2, x1, mod, final_norm, final):
    tile = pl.BlockSpec((TM, D_MODEL), lambda i, c: (i, 0))
    resident = pl.Buffered(1)
    if final:
        out_specs = [pl.BlockSpec((TM, D_MODEL), lambda i, c: (_ctx_tile(i), 0)),
                     pl.BlockSpec((TM, D_MODEL), lambda i, c: (_lat_tile(i), 0))]
        out_shape = [jax.ShapeDtypeStruct((T_CTX, D_MODEL), F32), jax.ShapeDtypeStruct((T_LAT, D_MODEL), F32)]
    else:
        out_specs = tile
        out_shape = jax.ShapeDtypeStruct((T_ALL, D_MODEL), F32)
    grid_spec = pltpu.PrefetchScalarGridSpec(
        num_scalar_prefetch=1,
        grid=(N_TILES,),
        in_specs=[
            tile,
            pl.BlockSpec((None, SUBLANES, TM), lambda i, c: (i, 0, 0)),
            pl.BlockSpec((TM, LANES), lambda i, c: (i, 0)),
            pl.BlockSpec((TM, TM), lambda i, c: (0, 0), pipeline_mode=resident),
            pl.BlockSpec((N_EXPERTS, D_MODEL, D_EXPERT), lambda i, c: (0, 0, 0), pipeline_mode=resident),
            pl.BlockSpec((N_EXPERTS, D_MODEL, D_EXPERT), lambda i, c: (0, 0, 0), pipeline_mode=resident),
            pl.BlockSpec((N_EXPERT_GROUPS, D_GROUP_HID, D_MODEL), lambda i, c: (0, 0, 0), pipeline_mode=resident),
            tile,
            pl.BlockSpec((None, N_MOD_PARTS, N_MOD_ROWS, D_MODEL), lambda i, c: (l, 0, 0, 0)),
            pl.BlockSpec((1, D_MODEL), lambda i, c: (0, 0)),
        ],
        out_specs=out_specs,
        scratch_shapes=[pltpu.VMEM((SORT_ROWS_PAD, D_MODEL), BF16),
                        pltpu.VMEM((SORT_ROWS_PAD, LANES), F32),
                        pltpu.VMEM((SORT_ROWS_PAD, D_MODEL), BF16)],
    )
    return pl.pallas_call(
        functools.partial(_moe_kernel, final=final),
        grid_spec=grid_spec,
        out_shape=out_shape,
        compiler_params=pltpu.CompilerParams(dimension_semantics=("arbitrary",),
                                             vmem_limit_bytes=MOE_VMEM_LIMIT),
        name=f"moe_l{l}",
    )(counts, h2, route, gate4, tri, w1, w3, w2, x1, mod, final_norm)


def _rope_tables():
    rows = DEC_SEQ // GRID_W
    row = np.repeat(np.arange(rows, dtype=np.float32), GRID_W)
    col = np.tile(np.arange(GRID_W, dtype=np.float32), rows)
    half = ROPE_AXIS_DIM // 2
    inv_freq = (1.0 / (np.float32(ROPE_THETA) ** (np.arange(half, dtype=np.float32) * np.float32(2.0)
                                                  / np.float32(ROPE_AXIS_DIM)))).astype(np.float32)
    ar = row[:, None] * inv_freq
    ac = col[:, None] * inv_freq
    cos64 = np.concatenate([np.cos(ar), np.cos(ar), np.cos(ac), np.cos(ac)], axis=1)
    sin64 = np.concatenate([-np.sin(ar), np.sin(ar), -np.sin(ac), np.sin(ac)], axis=1)
    cos_t = np.concatenate([np.ones((TM, LANES), np.float32), np.tile(cos64, (1, 2))], axis=0)
    sin_t = np.concatenate([np.zeros((TM, LANES), np.float32), np.tile(sin64, (1, 2))], axis=0)
    return jnp.asarray(cos_t, F32), jnp.asarray(sin_t, F32)


def _router_weights(router_g_w, router_g_b, router_e_w, router_e_b):
    zw = jnp.zeros((DEPTH, D_MODEL, SUBLANES - N_EXPERT_GROUPS), F32)
    zb = jnp.zeros((DEPTH, SUBLANES - N_EXPERT_GROUPS), F32)
    w_cols, b_cols = [router_g_w, zw], [router_g_b, zb]
    for g in range(N_EXPERT_GROUPS):
        w_cols += [router_e_w[:, g], zw]
        b_cols += [router_e_b[:, g], zb]
    used = (1 + N_EXPERT_GROUPS) * SUBLANES
    w_cols.append(jnp.zeros((DEPTH, D_MODEL, ROUTER_COLS - used), F32))
    b_cols.append(jnp.zeros((DEPTH, ROUTER_COLS - used), F32))
    w = jnp.concatenate(w_cols, axis=2)
    b = jnp.concatenate(b_cols, axis=1).reshape(DEPTH, 1, ROUTER_COLS)
    w_hi = w.astype(BF16)
    w_lo = (w - w_hi.astype(F32)).astype(BF16)
    return jnp.concatenate([w_hi, w_lo], axis=2), w_hi, b


def kernel(x_prompt, x_sample, cache_k, cache_v, c, c_ctx, w_ada, b_ada, norm1, w_in, q_norm, k_norm,
           sgu_w, sgu_b, conv_w, out_norm, w_out, norm2, router_g_w, router_g_b, router_e_w,
           router_e_b, moe_w1, moe_w3, moe_w2, final_norm):
    xs = (x_prompt.reshape(T_CTX, D_MODEL), x_sample.reshape(T_LAT, D_MODEL))

    cvecs = jnp.concatenate([c_ctx[None, :], c, jnp.zeros((N_MOD_ROWS - 1 - DEC_BATCH, D_MODEL), F32)], axis=0)
    mod = _modulation(cvecs.T, w_ada, b_ada)

    w_in_b = w_in.astype(BF16)
    w_out_b = w_out.astype(BF16)
    g1 = norm1.reshape(DEPTH, 1, D_MODEL)
    g2 = norm2.reshape(DEPTH, 1, D_MODEL)
    on = out_norm.reshape(DEPTH, 1, D_MIX)
    qg = jnp.tile(q_norm, (1, N_HEADS)).reshape(DEPTH, 1, D_ATTN)
    kg = jnp.tile(k_norm, (1, N_KV_HEADS)).reshape(DEPTH, 1, D_KV)
    cos_t, sin_t = _rope_tables()
    bq = jnp.asarray(np.kron(np.eye(N_HEADS), np.full((HEAD_DIM, HEAD_DIM), 1.0 / HEAD_DIM)), BF16)
    sgu_w_pairs = sgu_w.astype(BF16).reshape(DEPTH, N_SGU_GROUPS // 2, 2, CHUNK, CHUNK)
    sgu_w_pairs = jnp.concatenate([sgu_w_pairs[:, :, 0], sgu_w_pairs[:, :, 1]], axis=-1)
    sgu_bias = jnp.repeat(jnp.swapaxes(sgu_b, 1, 2), SGU_GROUP_DIM, axis=2)
    wr_cat, wr_hi, rb = _router_weights(router_g_w, router_g_b, router_e_w, router_e_b)
    fn = final_norm.reshape(1, D_MODEL)
    tri = jnp.asarray(np.triu(np.ones((TM, TM), np.float32)), BF16)
    ck = cache_k.reshape(DEC_BATCH, DEPTH, PAST_LEN, D_KV)
    cv = cache_v.reshape(DEC_BATCH, DEPTH, PAST_LEN, D_KV)

    ctx_ks, ctx_vs = [], []
    for l in range(DEPTH):
        qt, k, vt, k32, v32, sgu_n, cbz = _proj(l, xs, mod, g1, w_in_b, qg, kg, cos_t, sin_t, bq,
                                                sgu_w_pairs, sgu_bias, on)
        attn_ctx = _attn_ctx(l, qt, k, vt)
        attn_lat, w1, w3, w2 = _attn_lat(l, qt, k, vt, ck, cv, qg, kg, moe_w1, moe_w3, moe_w2)
        w2 = w2.reshape(N_EXPERT_GROUPS, D_GROUP_HID, D_MODEL)
        x1, h2, route, gate4, cnt = _merge(l, xs, attn_ctx, attn_lat, sgu_n, cbz, conv_w, on, w_out_b, mod, g2,
                                           wr_cat, wr_hi, rb)
        counts = cnt[:, :N_EXPERT_GROUPS, 0]
        out = _moe(l, counts, h2, route, gate4, tri, w1, w3, w2, x1, mod, fn, final=(l == DEPTH - 1))
        xs = (out,)
        ctx_ks.append(k32.reshape(BATCH, SEQ, N_KV_HEADS, HEAD_DIM))
        ctx_vs.append(v32.reshape(BATCH, SEQ, N_KV_HEADS, HEAD_DIM))

    y_prompt, y_sample = out
    return (y_prompt.reshape(BATCH, SEQ, D_MODEL), y_sample.reshape(DEC_BATCH, DEC_SEQ, D_MODEL),
            jnp.stack(ctx_ks, axis=1), jnp.stack(ctx_vs, axis=1))
```

```python
import functools
import math

import jax
import jax.numpy as jnp
import numpy as np
from jax import lax
from jax.experimental import pallas as pl
from jax.experimental.pallas import tpu as pltpu

D_MODEL = 1024
BATCH = 16
SEQ = 256
DEPTH = 2
DEC_BATCH = 2
DEC_SEQ = 4096
PAST_LEN = 512
GRID_W = 64
N_HEADS = 8
N_KV_HEADS = 2
HEAD_DIM = 64
ROPE_AXIS_DIM = HEAD_DIM // 2
ROPE_THETA = 10000.0
D_ATTN = N_HEADS * HEAD_DIM
D_KV = N_KV_HEADS * HEAD_DIM
CHUNK = 128
N_SGU_GROUPS = 4
SGU_GROUP_DIM = 64
D_SGU = N_SGU_GROUPS * SGU_GROUP_DIM
D_CONV = 256
D_MIX = D_ATTN + D_SGU + D_CONV
D_IN = D_ATTN + 2 * D_KV + 2 * D_SGU + 3 * D_CONV
N_EXPERT_GROUPS = 4
EXPERTS_PER_GROUP = 4
N_EXPERTS = N_EXPERT_GROUPS * EXPERTS_PER_GROUP
D_EXPERT = 256
D_GROUP_HID = EXPERTS_PER_GROUP * D_EXPERT
EPS = 1e-6

T_CTX = BATCH * SEQ
T_LAT = DEC_BATCH * DEC_SEQ
T_ALL = T_CTX + T_LAT
TM = 512
N_TILES = T_ALL // TM
N_CTX_TILES = T_CTX // TM
LAT_TILES_PER_BATCH = DEC_SEQ // TM
LANES = 128
SUBLANES = 8
BF16_ROWS = 16
N_MOD_ROWS = 8
N_MOD_PARTS = 6
ROUTER_COLS = 128
VMEM_LIMIT = 48 * 1024 * 1024

OFF_Q = 0
OFF_KV = D_ATTN
OFF_SGU = OFF_KV + 2 * D_KV
OFF_CONV = OFF_SGU + 2 * D_SGU

BF16 = jnp.bfloat16
F32 = jnp.float32
NEG_BIG = -1e30
Q_SCALE = HEAD_DIM ** -0.5 * math.log2(math.e)


def _params(n_grid_dims):
    return pltpu.CompilerParams(
        dimension_semantics=("arbitrary",) * n_grid_dims,
        vmem_limit_bytes=VMEM_LIMIT,
    )


def _mod_row_of_tile(i, tile=TM):
    n_ctx = T_CTX // tile
    return jnp.where(i < n_ctx, 0, 1 + (i - n_ctx) // (DEC_SEQ // tile))


def _mod_part(mod_ref, part, row):
    return mod_ref[part, pl.ds(row, 1), :]


def _rope_block_of_tile(i):
    return jnp.where(i < N_CTX_TILES, 0, 1 + (i - N_CTX_TILES) % LAT_TILES_PER_BATCH)


def _rms(x, gain):
    ms = jnp.mean(x * x, axis=-1, keepdims=True)
    return x * lax.rsqrt(ms + EPS) * gain


def _ctx_tile(i):
    return jnp.minimum(i, N_CTX_TILES - 1)


def _lat_tile(i):
    return jnp.maximum(i - N_CTX_TILES, 0)


MOD_PARTS_PER_STEP = 2
MOD_TN = MOD_PARTS_PER_STEP * D_MODEL


def _mod_kernel(ct_ref, w_ref, b_ref, o_ref):
    c = ct_ref[...]
    s = c / (1.0 + jnp.exp(-c))
    row = lax.broadcasted_iota(jnp.int32, (N_MOD_ROWS, D_MODEL), 0)
    for part in range(MOD_PARTS_PER_STEP):
        cols = slice(part * D_MODEL, (part + 1) * D_MODEL)
        w = w_ref[:, cols]
        out = jnp.zeros((N_MOD_ROWS, D_MODEL), F32)
        for r in range(1 + DEC_BATCH):
            acc = jnp.sum(w * s[:, r:r + 1], axis=0, keepdims=True)
            out = jnp.where(row == r, acc, out)
        o_ref[part] = out + b_ref[:, cols]


def _modulation(cvecs_t, w_ada, b_ada):
    n_col = N_MOD_PARTS * D_MODEL
    return pl.pallas_call(
        _mod_kernel,
        grid=(DEPTH, n_col // MOD_TN),
        in_specs=[
            pl.BlockSpec((D_MODEL, N_MOD_ROWS), lambda l, j: (0, 0)),
            pl.BlockSpec((None, D_MODEL, MOD_TN), lambda l, j: (l, 0, j)),
            pl.BlockSpec((None, 1, MOD_TN), lambda l, j: (l, 0, j)),
        ],
        out_specs=pl.BlockSpec((None, MOD_PARTS_PER_STEP, N_MOD_ROWS, D_MODEL), lambda l, j: (l, j, 0, 0)),
        out_shape=jax.ShapeDtypeStruct((DEPTH, N_MOD_PARTS, N_MOD_ROWS, D_MODEL), F32),
        compiler_params=_params(2),
        name="adaln_modulation",
    )(cvecs_t, w_ada, b_ada.reshape(DEPTH, 1, n_col))


def _mod_spec(l):
    return pl.BlockSpec((None, N_MOD_PARTS, N_MOD_ROWS, D_MODEL), lambda *idx: (l, 0, 0, 0))


def _swap16(x):
    lane = lax.broadcasted_iota(jnp.int32, x.shape, 1)
    up = pltpu.roll(x, 16, 1)
    down = pltpu.roll(x, LANES - 16, 1)
    return jnp.where((lane & 16) != 0, up, down)


def _rope(x, cos_t, sin_t):
    cols = []
    for j in range(x.shape[1] // LANES):
        xc = x[:, j * LANES:(j + 1) * LANES]
        cols.append(xc * cos_t + _swap16(xc) * sin_t)
    return cols[0] if len(cols) == 1 else jnp.concatenate(cols, axis=1)


def _proj_kernel(*refs, split_x):
    n_x = 2 if split_x else 1
    x_refs, refs, (scr_even, scr_odd) = refs[:n_x], refs[n_x:-2], refs[-2:]
    step = functools.partial(_proj_step, x_refs, refs, split_x)
    i = pl.program_id(0)

    @pl.when(i == 0)
    def _():
        scr_odd[...] = jnp.zeros((TM, D_IN), F32)

    last = i == N_TILES

    @pl.when(jnp.logical_and(i % 2 == 0, jnp.logical_not(last)))
    def _():
        step(scr_even, scr_odd)

    @pl.when(jnp.logical_and(i % 2 == 1, jnp.logical_not(last)))
    def _():
        step(scr_odd, scr_even)

    @pl.when(last)
    def _():
        step(None, scr_even if N_TILES % 2 == 1 else scr_odd)


def _proj_step(x_refs, refs, split_x, new_ref, proj):
    (mod_ref, g1_ref, w_ref, qg_ref, kg_ref, cos_ref, sin_ref, bq_ref, sw_ref, sb_ref, on_ref,
     qt_ref, k_ref, vt_ref, k32_ref, v32_ref, sgu_ref, cbz_ref) = refs
    i = pl.program_id(0)
    cur = _stage1_tile(i)

    is_ctx = i <= N_CTX_TILES
    cos_t = cos_ref[...]
    sin_t = sin_ref[...]

    q = proj[:, OFF_Q:OFF_Q + D_ATTN]
    q2 = (q * q).astype(BF16)
    half = D_ATTN // 2
    qms = jnp.concatenate(
        [jnp.dot(q2[:, :half], bq_ref[:half, :half], preferred_element_type=F32),
         jnp.dot(q2[:, half:], bq_ref[half:, half:], preferred_element_type=F32)], axis=1)
    qn = q * lax.rsqrt(qms + EPS) * qg_ref[...]
    qr = _rope(qn, cos_t, sin_t) * Q_SCALE
    qt_ref[...] = qr.T.astype(BF16)

    k = proj[:, OFF_KV:OFF_KV + D_KV]
    v = proj[:, OFF_KV + D_KV:OFF_KV + 2 * D_KV]
    kms = jnp.dot((k * k).astype(BF16), bq_ref[:D_KV, :D_KV], preferred_element_type=F32)
    kn = k * lax.rsqrt(kms + EPS) * kg_ref[...]
    k_ref[...] = _rope(kn, cos_t, sin_t).astype(BF16)
    vt_ref[...] = v.T.astype(BF16)

    su = proj[:, OFF_SGU:OFF_SGU + D_SGU]
    sv = proj[:, OFF_SGU + D_SGU:OFF_SGU + 2 * D_SGU]
    n_chunks = TM // CHUNK
    sv_wide = jnp.concatenate([sv[n * CHUNK:(n + 1) * CHUNK, :] for n in range(n_chunks)], axis=1)
    grp = lax.broadcasted_iota(jnp.int32, (CHUNK, n_chunks * D_SGU), 1) // SGU_GROUP_DIM % N_SGU_GROUPS
    mixed_wide = jnp.zeros((CHUNK, n_chunks * D_SGU), F32)
    for pair in range(N_SGU_GROUPS // 2):
        rhs = jnp.concatenate(
            [jnp.where(grp == 2 * pair, sv_wide, 0.0), jnp.where(grp == 2 * pair + 1, sv_wide, 0.0)],
            axis=0).astype(BF16)
        mixed_wide = mixed_wide + jnp.dot(sw_ref[pair], rhs, preferred_element_type=F32)
    on_sgu = on_ref[:, D_ATTN:D_ATTN + D_SGU]
    for n in range(n_chunks):
        sgu = su[n * CHUNK:(n + 1) * CHUNK, :] * (mixed_wide[:, n * D_SGU:(n + 1) * D_SGU] + sb_ref[...])
        sgu_ref[n * CHUNK:(n + 1) * CHUNK, :] = _rms(sgu, on_sgu).astype(BF16)

    cbz_ref[:, :D_CONV] = proj[:, OFF_CONV:OFF_CONV + D_CONV]
    cbz_ref[:, D_CONV:] = (proj[:, OFF_CONV + D_CONV:OFF_CONV + 2 * D_CONV]
                           * proj[:, OFF_CONV + 2 * D_CONV:OFF_CONV + 3 * D_CONV])

    if new_ref is not None:
        x = jnp.where(cur < N_CTX_TILES, x_refs[0][...], x_refs[1][...]) if split_x else x_refs[0][...]
        mrow = _mod_row_of_tile(cur)
        h = _rms(x, g1_ref[...] * (1.0 + _mod_part(mod_ref, 1, mrow))) + _mod_part(mod_ref, 0, mrow)
        new_ref[...] = jnp.dot(h.astype(BF16), w_ref[...], preferred_element_type=F32)

    @pl.when(is_ctx)
    def _():
        k32_ref[...] = kn
        v32_ref[...] = v


def _x_specs(xs, tile_of_step=lambda i: i):
    if len(xs) == 1:
        return [pl.BlockSpec((TM, D_MODEL), lambda i: (tile_of_step(i), 0))]
    return [pl.BlockSpec((TM, D_MODEL), lambda i: (_ctx_tile(tile_of_step(i)), 0)),
            pl.BlockSpec((TM, D_MODEL), lambda i: (_lat_tile(tile_of_step(i)), 0))]


def _stage1_tile(i):
    return jnp.minimum(i, N_TILES - 1)


def _stage2_tile(i):
    return jnp.maximum(i - 1, 0)


def _proj(l, xs, mod, g1, w_in, qg, kg, cos_t, sin_t, bq, sgu_w, sgu_b, on):
    lay = lambda *rest: (lambda i: (l,) + rest)
    s2 = _stage2_tile
    return pl.pallas_call(
        functools.partial(_proj_kernel, split_x=len(xs) == 2),
        grid=(N_TILES + 1,),
        in_specs=_x_specs(xs, _stage1_tile) + [
            _mod_spec(l),
            pl.BlockSpec((None, 1, D_MODEL), lay(0, 0)),
            pl.BlockSpec((None, D_MODEL, D_IN), lay(0, 0)),
            pl.BlockSpec((None, 1, D_ATTN), lay(0, 0)),
            pl.BlockSpec((None, 1, D_KV), lay(0, 0)),
            pl.BlockSpec((TM, LANES), lambda i: (_rope_block_of_tile(s2(i)), 0)),
            pl.BlockSpec((TM, LANES), lambda i: (_rope_block_of_tile(s2(i)), 0)),
            pl.BlockSpec((D_ATTN, D_ATTN), lambda i: (0, 0)),
            pl.BlockSpec((None, N_SGU_GROUPS // 2, CHUNK, 2 * CHUNK), lay(0, 0, 0)),
            pl.BlockSpec((None, CHUNK, D_SGU), lay(0, 0)),
            pl.BlockSpec((None, 1, D_MIX), lay(0, 0)),
        ],
        out_specs=[
            pl.BlockSpec((None, D_ATTN, TM), lambda i: (s2(i), 0, 0)),
            pl.BlockSpec((TM, D_KV), lambda i: (s2(i), 0)),
            pl.BlockSpec((None, D_KV, TM), lambda i: (s2(i), 0, 0)),
            pl.BlockSpec((TM, D_KV), lambda i: (_ctx_tile(s2(i)), 0)),
            pl.BlockSpec((TM, D_KV), lambda i: (_ctx_tile(s2(i)), 0)),
            pl.BlockSpec((TM, D_SGU), lambda i: (s2(i), 0)),
            pl.BlockSpec((TM, 2 * D_CONV), lambda i: (s2(i), 0)),
        ],
        out_shape=[
            jax.ShapeDtypeStruct((N_TILES, D_ATTN, TM), BF16),
            jax.ShapeDtypeStruct((T_ALL, D_KV), BF16),
            jax.ShapeDtypeStruct((N_TILES, D_KV, TM), BF16),
            jax.ShapeDtypeStruct((T_CTX, D_KV), F32),
            jax.ShapeDtypeStruct((T_CTX, D_KV), F32),
            jax.ShapeDtypeStruct((T_ALL, D_SGU), BF16),
            jax.ShapeDtypeStruct((T_ALL, 2 * D_CONV), F32),
        ],
        scratch_shapes=[pltpu.VMEM((TM, D_IN), F32), pltpu.VMEM((TM, D_IN), F32)],
        compiler_params=_params(1),
        name=f"proj_l{l}",
    )(*xs, mod, g1, w_in, qg, kg, cos_t, sin_t, bq, sgu_w, sgu_b, on)


ACC_ROWS = HEAD_DIM + BF16_ROWS
N_PAIRS = N_HEADS // 2
PAIRS_PER_KV = N_PAIRS // N_KV_HEADS


def _pair_queries(qt, kv_idx, tq):
    qf = qt.astype(F32)
    mine = lax.broadcasted_iota(jnp.int32, (2 * HEAD_DIM, tq), 0) // HEAD_DIM == kv_idx
    qe = qf[:HEAD_DIM]
    qo = qf[HEAD_DIM:]
    rhs = jnp.concatenate(
        [jnp.where(mine, jnp.concatenate([qe, qe], axis=0), 0.0),
         jnp.where(mine, jnp.concatenate([qo, qo], axis=0), 0.0)], axis=1)
    return rhs.astype(BF16)


def _with_ones(vt_c):
    return jnp.concatenate([vt_c, jnp.ones((BF16_ROWS, vt_c.shape[1]), BF16)], axis=0)


def _flash_pairs(rhs_list, chunk_lists, tq, shift=None, query_block=None):
    n_pairs, n_chunks = len(rhs_list), len(chunk_lists[0])
    qb = 2 * tq if query_block is None else query_block
    streams = [(s, j) for s in range(n_pairs) for j in range(2 * tq // qb)]
    rhs = {(s, j): rhs_list[s][:, j * qb:(j + 1) * qb] for s, j in streams}
    scores = lambda t, c: jnp.dot(chunk_lists[t[0]][c][0](), rhs[t], preferred_element_type=F32)
    m = {t: jnp.full((1, qb), NEG_BIG, F32) for t in streams}
    acc = {t: jnp.zeros((ACC_ROWS, qb), F32) for t in streams}
    st = {t: scores(t, 0) for t in streams}
    for c in range(n_chunks):
        for t in streams:
            vt_c = chunk_lists[t[0]][c][1]()
            if shift is None:
                m_new = jnp.maximum(m[t], jnp.max(st[t], axis=0, keepdims=True))
            st_next = scores(t, c + 1) if c + 1 < n_chunks else None
            if shift is None:
                alpha = jnp.exp2(m[t] - m_new)
                p = jnp.exp2(st[t] - m_new).astype(BF16)
                acc[t] = alpha * acc[t] + jnp.dot(vt_c, p, preferred_element_type=F32)
                m[t] = m_new
            else:
                p = jnp.exp2(st[t] - shift).astype(BF16)
                acc[t] = acc[t] + jnp.dot(vt_c, p, preferred_element_type=F32)
            st[t] = st_next
    outs = []
    for s in range(n_pairs):
        a = jnp.concatenate([acc[t] for t in streams if t[0] == s], axis=1)
        o = a[:HEAD_DIM] * (1.0 / a[HEAD_DIM:HEAD_DIM + 1])
        outs.append(jnp.concatenate([o[:, :tq], o[:, tq:]], axis=0))
    return outs


def _attn_ctx_kernel(qt_ref, k_ref, vt_ref, o_ref):
    tq = SEQ
    rhs_list, chunk_lists = [], []
    for pair in range(N_PAIRS):
        kv_idx = pair // PAIRS_PER_KV
        rhs_list.append(_pair_queries(qt_ref[pair * LANES:(pair + 1) * LANES, :], kv_idx, tq))
        chunk_lists.append([(lambda: k_ref[...],
                             lambda kv_idx=kv_idx: _with_ones(vt_ref[kv_idx * HEAD_DIM:(kv_idx + 1) * HEAD_DIM, :]))])
    for pair, o_t in enumerate(_flash_pairs(rhs_list, chunk_lists, tq, query_block=LAT_QUERY_BLOCK)):
        o_ref[:, pair * LANES:(pair + 1) * LANES] = o_t.T.astype(BF16)


def _attn_ctx(l, qt, k, vt):
    halves = TM // SEQ
    return pl.pallas_call(
        _attn_ctx_kernel,
        grid=(BATCH,),
        in_specs=[
            pl.BlockSpec((None, D_ATTN, SEQ), lambda b: (b // halves, 0, b % halves)),
            pl.BlockSpec((SEQ, D_KV), lambda b: (b, 0)),
            pl.BlockSpec((None, D_KV, SEQ), lambda b: (b // halves, 0, b % halves)),
        ],
        out_specs=pl.BlockSpec((SEQ, D_ATTN), lambda b: (b, 0)),
        out_shape=jax.ShapeDtypeStruct((T_CTX, D_ATTN), BF16),
        compiler_params=_params(1),
        name=f"attn_ctx_l{l}",
    )(qt, k, vt)


KC = 256
LAT_PAIRS_PER_STEP = 2
LAT_QUERY_BLOCK = 256


MAX_SHIFT = 60.0
BOUND_SLACK = 1.05


def _score_bound_sq(qg_ref, kg_ref, ck):
    qg = qg_ref[...]
    kg = kg_ref[...]
    gq2 = jnp.max(qg * qg, axis=1, keepdims=True)
    k2 = HEAD_DIM * jnp.max(kg * kg, axis=1, keepdims=True)
    if ck is not None:
        sq = ck * ck
        first = lax.broadcasted_iota(jnp.int32, sq.shape, 1) < HEAD_DIM
        n0 = jnp.sum(jnp.where(first, sq, 0.0), axis=1, keepdims=True)
        n1 = jnp.sum(jnp.where(first, 0.0, sq), axis=1, keepdims=True)
        k2 = jnp.maximum(k2, jnp.max(jnp.maximum(n0, n1), axis=0, keepdims=True))
    return (BOUND_SLACK * Q_SCALE) ** 2 * HEAD_DIM * gq2 * k2


def _run_with_score_bound(run, bound):
    small = bound <= MAX_SHIFT

    @pl.when(small)
    def _():
        run(bound)

    @pl.when(jnp.logical_not(small))
    def _():
        run(None)


def _flash_pair_rolled(rhs, k_ref, vt_ref, v_row, tail_chunks, tq):
    def update(carry, k_c, vt_c):
        m, acc = carry
        st = jnp.dot(k_c, rhs, preferred_element_type=F32)
        m_new = jnp.maximum(m, jnp.max(st, axis=0, keepdims=True))
        p = jnp.exp2(st - m_new).astype(BF16)
        return m_new, jnp.exp2(m - m_new) * acc + jnp.dot(vt_c, p, preferred_element_type=F32)

    def body(c, carry):
        k_c = k_ref[pl.ds(pl.multiple_of(c * TM, TM), TM), :]
        return update(carry, k_c, _with_ones(vt_ref[c, pl.ds(v_row, HEAD_DIM), :]))

    carry = (jnp.full((1, 2 * tq), NEG_BIG, F32), jnp.zeros((ACC_ROWS, 2 * tq), F32))
    carry = lax.fori_loop(0, DEC_SEQ // TM, body, carry)
    for get_k, get_vt in tail_chunks:
        carry = update(carry, get_k(), get_vt())
    acc = carry[1]
    o = acc[:HEAD_DIM] * (1.0 / acc[HEAD_DIM:HEAD_DIM + 1])
    return jnp.concatenate([o[:, :tq], o[:, tq:]], axis=0)


def _attn_lat_kernel(qt_ref, k_ref, vt_ref, ck_ref, cv_ref, qg_ref, kg_ref, w1_ref, w3_ref, w2_ref,
                     o_ref, w1b_ref, w3b_ref, w2b_ref, bound_ref):
    tq = TM

    def run(shift):
        w1b_ref[...] = w1_ref[...].astype(BF16)
        w3b_ref[...] = w3_ref[...].astype(BF16)
        w2b_ref[...] = w2_ref[...].astype(BF16)
        cvt = cv_ref[...].T
        rhs_list, chunk_lists, v_rows = [], [], []
        for j in range(LAT_PAIRS_PER_STEP):
            kv_idx = (pl.program_id(2) * LAT_PAIRS_PER_STEP + j) // PAIRS_PER_KV
            rhs_list.append(_pair_queries(qt_ref[j * LANES:(j + 1) * LANES, :], kv_idx, tq))
            v_row = pl.multiple_of(kv_idx * HEAD_DIM, HEAD_DIM)
            v_rows.append(v_row)

            def latent_chunk(c, v_row=v_row):
                tile, lane0 = (c * KC) // TM, (c * KC) % TM
                return (lambda: k_ref[c * KC:(c + 1) * KC, :],
                        lambda: _with_ones(vt_ref[tile, pl.ds(v_row, HEAD_DIM), lane0:lane0 + KC]))

            def cached_chunk(c, kv_idx=kv_idx):
                return (lambda: ck_ref[c * KC:(c + 1) * KC, :].astype(BF16),
                        lambda: _with_ones(jnp.where(kv_idx == 0, cvt[:HEAD_DIM, c * KC:(c + 1) * KC],
                                                     cvt[HEAD_DIM:, c * KC:(c + 1) * KC]).astype(BF16)))

            chunks = [latent_chunk(c) for c in range(DEC_SEQ // KC)]
            chunks += [cached_chunk(c) for c in range(PAST_LEN // KC)]
            chunk_lists.append(chunks)
        if shift is None:
            outs = [_flash_pair_rolled(rhs_list[j], k_ref, vt_ref, v_rows[j], chunk_lists[j][DEC_SEQ // KC:], tq)
                    for j in range(LAT_PAIRS_PER_STEP)]
        else:
            outs = _flash_pairs(rhs_list, chunk_lists, tq, shift, query_block=LAT_QUERY_BLOCK)
        for j, o_t in enumerate(outs):
            o_ref[:, j * LANES:(j + 1) * LANES] = o_t.T.astype(BF16)

    @pl.when(jnp.logical_and(pl.program_id(1) == 0, pl.program_id(2) == 0))
    def _():
        bound_ref[0] = jnp.sqrt(_score_bound_sq(qg_ref, kg_ref, ck_ref[...]))[0, 0]

    _run_with_score_bound(run, bound_ref[0])


def _attn_lat(l, qt, k, vt, cache_k, cache_v, qg, kg, moe_w1, moe_w3, moe_w2):
    n_q = DEC_SEQ // TM
    first = N_CTX_TILES
    ctx_blocks = T_CTX // DEC_SEQ
    width = LAT_PAIRS_PER_STEP * LANES
    n_p = N_PAIRS // LAT_PAIRS_PER_STEP
    per_expert = DEC_BATCH * n_q * n_p // N_EXPERTS
    step = lambda b, i, p: (b * n_q + i) * n_p + p
    w_in_spec = lambda rows, cols: pl.BlockSpec(
        (None, None, rows // per_expert, cols),
        lambda b, i, p: (l, step(b, i, p) // per_expert, step(b, i, p) % per_expert, 0))
    w_out_spec = lambda rows, cols: pl.BlockSpec(
        (None, rows // per_expert, cols),
        lambda b, i, p: (step(b, i, p) // per_expert, step(b, i, p) % per_expert, 0))
    return pl.pallas_call(
        _attn_lat_kernel,
        grid=(DEC_BATCH, n_q, n_p),
        in_specs=[
            pl.BlockSpec((None, width, TM), lambda b, i, p: (first + b * n_q + i, p, 0)),
            pl.BlockSpec((DEC_SEQ, D_KV), lambda b, i, p: (ctx_blocks + b, 0)),
            pl.BlockSpec((n_q, D_KV, TM), lambda b, i, p: (ctx_blocks + b, 0, 0)),
            pl.BlockSpec((None, None, PAST_LEN, D_KV), lambda b, i, p: (b, l, 0, 0)),
            pl.BlockSpec((None, None, PAST_LEN, D_KV), lambda b, i, p: (b, l, 0, 0)),
            pl.BlockSpec((None, 1, D_ATTN), lambda b, i, p: (l, 0, 0)),
            pl.BlockSpec((None, 1, D_KV), lambda b, i, p: (l, 0, 0)),
            w_in_spec(D_MODEL, D_EXPERT),
            w_in_spec(D_MODEL, D_EXPERT),
            w_in_spec(D_EXPERT, D_MODEL),
        ],
        out_specs=[
            pl.BlockSpec((TM, width), lambda b, i, p: (b * n_q + i, p)),
            w_out_spec(D_MODEL, D_EXPERT),
            w_out_spec(D_MODEL, D_EXPERT),
            w_out_spec(D_EXPERT, D_MODEL),
        ],
        out_shape=[
            jax.ShapeDtypeStruct((T_LAT, D_ATTN), BF16),
            jax.ShapeDtypeStruct((N_EXPERTS, D_MODEL, D_EXPERT), BF16),
            jax.ShapeDtypeStruct((N_EXPERTS, D_MODEL, D_EXPERT), BF16),
            jax.ShapeDtypeStruct((N_EXPERTS, D_EXPERT, D_MODEL), BF16),
        ],
        scratch_shapes=[pltpu.SMEM((1,), F32)],
        compiler_params=_params(3),
        name=f"attn_lat_l{l}",
    )(qt, k, vt, cache_k, cache_v, qg, kg, moe_w1, moe_w3, moe_w2)


def _first_max_index(vals, valid, rowf):
    masked = jnp.where(valid, vals, -jnp.inf)
    mx = jnp.max(masked, axis=0, keepdims=True)
    idx = jnp.min(jnp.where(masked == mx, rowf, float(SUBLANES)), axis=0, keepdims=True)
    return mx, idx


def _merge_kernel(*refs, split_x):
    n_x = 2 if split_x else 1
    x_refs, refs = refs[:n_x], refs[n_x:]
    (actx_ref, alat_ref, sgu_ref, cbz_ref, prev_ref, next_ref, cw_ref, on_ref,
     wo_ref, mod_ref, g2_ref, wr_ref, wrhi_ref, rb_ref, x1_ref, h2_ref, route_ref, gate4_ref, cnt_ref) = refs
    cur = pl.program_id(0)
    is_ctx = cur < N_CTX_TILES
    mrow = _mod_row_of_tile(cur)
    on = on_ref[...]

    cbz = cbz_ref[...]
    cb = cbz[:, :D_CONV]
    z = cbz[:, D_CONV:]
    seq_mask = jnp.where(is_ctx, SEQ - 1, DEC_SEQ - 1)
    row = lax.broadcasted_iota(jnp.int32, (TM, D_CONV), 0)
    pos = (cur * TM + row) & seq_mask
    z_prev = jnp.where(row == 0, prev_ref[SUBLANES - 1:SUBLANES, D_CONV:], pltpu.roll(z, 1, 0))
    z_next = jnp.where(row == TM - 1, next_ref[0:1, D_CONV:], pltpu.roll(z, TM - 1, 0))
    z_prev = jnp.where(pos == 0, 0.0, z_prev)
    z_next = jnp.where(pos == seq_mask, 0.0, z_next)
    cw = cw_ref[...]
    conv = cb * (z_prev * cw[0:1, :] + z * cw[1:2, :] + z_next * cw[2:3, :])
    conv_n = _rms(conv, on[:, D_ATTN + D_SGU:]).astype(BF16)

    attn = jnp.where(is_ctx, actx_ref[...], alat_ref[...]).astype(F32)
    attn_n = _rms(attn, on[:, :D_ATTN]).astype(BF16)
    merged = jnp.concatenate([attn_n, sgu_ref[...], conv_n], axis=1)
    x = jnp.where(is_ctx, x_refs[0][...], x_refs[1][...]) if split_x else x_refs[0][...]
    x1 = x + _mod_part(mod_ref, 2, mrow) * jnp.dot(merged, wo_ref[...], preferred_element_type=F32)
    x1_ref[...] = x1
    _route_tile(x1, mrow, mod_ref, g2_ref, wr_ref, wrhi_ref, rb_ref, h2_ref, route_ref, gate4_ref, cnt_ref)


def _route_tile(x1, mrow, mod_ref, g2_ref, wr_ref, wrhi_ref, rb_ref, h2_ref, route_ref, gate4_ref, cnt_ref):
    n = x1.shape[0]
    h2 = _rms(x1, g2_ref[...] * (1.0 + _mod_part(mod_ref, 4, mrow))) + _mod_part(mod_ref, 3, mrow)
    hb = h2.astype(BF16)
    h2_ref[...] = hb

    h_lo = (h2 - hb.astype(F32)).astype(BF16)
    a = jnp.dot(hb, wr_ref[...], preferred_element_type=F32)
    b = jnp.dot(h_lo, wrhi_ref[...], preferred_element_type=F32)
    logits = a[:, :ROUTER_COLS] + a[:, ROUTER_COLS:] + b + rb_ref[...]
    lt = logits.T

    rowi = lax.broadcasted_iota(jnp.int32, (SUBLANES, n), 0).astype(F32)
    valid = rowi < float(N_EXPERT_GROUPS)
    g_log = lt[0:SUBLANES]
    g_max, g_idx = _first_max_index(g_log, valid, rowi)
    p_g = 1.0 / jnp.sum(jnp.where(valid, jnp.exp(g_log - g_max), 0.0), axis=0, keepdims=True)
    e_log = jnp.zeros((SUBLANES, n), F32)
    for g in range(N_EXPERT_GROUPS):
        e_log = jnp.where(g_idx == g, lt[(g + 1) * SUBLANES:(g + 2) * SUBLANES], e_log)
    e_max, i1 = _first_max_index(e_log, valid, rowi)
    e_exp = jnp.where(valid, jnp.exp(e_log - e_max), 0.0)
    e_prob = e_exp / jnp.sum(e_exp, axis=0, keepdims=True)
    v1 = jnp.max(e_prob, axis=0, keepdims=True)
    rest = jnp.logical_and(valid, rowi != i1)
    v2, i2 = _first_max_index(e_prob, rest, rowi)
    denom = v1 + v2
    w_sel = p_g * jnp.where(rowi == i1, v1 / denom, jnp.where(rowi == i2, v2 / denom, 0.0))
    route_ref[...] = jnp.broadcast_to(g_idx, (SUBLANES, n))
    pad = jnp.zeros((LANES - SUBLANES, n), F32)
    gate4_ref[...] = jnp.concatenate([w_sel, pad], axis=0).T
    counts = jnp.sum(jnp.where(rowi == g_idx, 1.0, 0.0), axis=1, keepdims=True)
    cnt_ref[...] = jnp.broadcast_to(counts, (SUBLANES, LANES)).astype(jnp.int32)


def _merge(l, xs, attn_ctx, attn_lat, sgu_n, cbz, conv_w, on, w_out, mod, g2, wr_cat, wr_hi, rb):
    lay = lambda *rest: (lambda i: (l,) + rest)
    rows8 = TM // SUBLANES
    return pl.pallas_call(
        functools.partial(_merge_kernel, split_x=len(xs) == 2),
        grid=(N_TILES,),
        in_specs=_x_specs(xs) + [
            pl.BlockSpec((TM, D_ATTN), lambda i: (_ctx_tile(i), 0)),
            pl.BlockSpec((TM, D_ATTN), lambda i: (_lat_tile(i), 0)),
            pl.BlockSpec((TM, D_SGU), lambda i: (i, 0)),
            pl.BlockSpec((TM, 2 * D_CONV), lambda i: (i, 0)),
            pl.BlockSpec((SUBLANES, 2 * D_CONV), lambda i: (jnp.maximum(i * rows8 - 1, 0), 0)),
            pl.BlockSpec((SUBLANES, 2 * D_CONV),
                         lambda i: (jnp.minimum((i + 1) * rows8, T_ALL // SUBLANES - 1), 0)),
            pl.BlockSpec((None, 3, D_CONV), lay(0, 0)),
            pl.BlockSpec((None, 1, D_MIX), lay(0, 0)),
            pl.BlockSpec((None, D_MIX, D_MODEL), lay(0, 0)),
            _mod_spec(l),
            pl.BlockSpec((None, 1, D_MODEL), lay(0, 0)),
            pl.BlockSpec((None, D_MODEL, 2 * ROUTER_COLS), lay(0, 0)),
            pl.BlockSpec((None, D_MODEL, ROUTER_COLS), lay(0, 0)),
            pl.BlockSpec((None, 1, ROUTER_COLS), lay(0, 0)),
        ],
        out_specs=[
            pl.BlockSpec((TM, D_MODEL), lambda i: (i, 0)),
            pl.BlockSpec((TM, D_MODEL), lambda i: (i, 0)),
            pl.BlockSpec((None, SUBLANES, TM), lambda i: (i, 0, 0)),
            pl.BlockSpec((TM, LANES), lambda i: (i, 0)),
            pl.BlockSpec((None, SUBLANES, LANES), lambda i: (i, 0, 0)),
        ],
        out_shape=[
            jax.ShapeDtypeStruct((T_ALL, D_MODEL), F32),
            jax.ShapeDtypeStruct((T_ALL, D_MODEL), BF16),
            jax.ShapeDtypeStruct((N_TILES, SUBLANES, TM), F32),
            jax.ShapeDtypeStruct((T_ALL, LANES), F32),
            jax.ShapeDtypeStruct((N_TILES, SUBLANES, LANES), jnp.int32),
        ],
        compiler_params=_params(1),
        name=f"merge_l{l}",
    )(*xs, attn_ctx, attn_lat, sgu_n, cbz, cbz, cbz, conv_w, on, w_out, mod, g2, wr_cat, wr_hi, rb)


RB = 144
SORT_ROWS = TM + LANES
SORT_ROWS_PAD = -(-(SORT_ROWS + RB) // BF16_ROWS) * BF16_ROWS
MOE_VMEM_LIMIT = 56 * 1024 * 1024


def _split3(x):
    hi = x.astype(BF16)
    r1 = x - hi.astype(F32)
    mid = r1.astype(BF16)
    lo = (r1 - mid.astype(F32)).astype(BF16)
    return hi, mid, lo


def _moe_kernel(cnt_ref, h_ref, route_ref, gate4_ref, tri_ref, w1_ref, w3_ref, w2_ref, x1_ref, mod_ref, fn_ref,
                *refs, final):
    out_refs, (xs_ref, gs_ref, zs_ref) = refs[:-3], refs[-3:]
    i = pl.program_id(0)
    gate2 = _mod_part(mod_ref, 5, _mod_row_of_tile(i))
    counts = [cnt_ref[i, g] for g in range(N_EXPERT_GROUPS)]
    starts = [jnp.int32(0)]
    for g in range(N_EXPERT_GROUPS - 1):
        starts.append(starts[-1] + (counts[g] + (BF16_ROWS - 1)) // BF16_ROWS * BF16_ROWS)

    g_idx = route_ref[0:1, :]
    rowi = lax.broadcasted_iota(jnp.int32, (SUBLANES, TM), 0).astype(F32)
    onehot = jnp.where(rowi == g_idx, 1.0, 0.0)
    incl = jnp.dot(onehot.astype(BF16), tri_ref[...], preferred_element_type=F32)
    pos = jnp.sum(onehot * incl, axis=0, keepdims=True) - 1.0
    for g in range(1, N_EXPERT_GROUPS):
        pos = pos + jnp.where(g_idx == float(g), starts[g].astype(F32), 0.0)
    sub = lax.broadcasted_iota(jnp.int32, (SORT_ROWS, TM), 0).astype(F32)
    perm = jnp.where(sub == pos, 1.0, 0.0).astype(BF16)
    pos_col = jnp.broadcast_to(pos, (LANES, TM)).T
    lane = lax.broadcasted_iota(jnp.int32, (TM, SORT_ROWS), 1).astype(F32)
    perm_t = jnp.where(lane == jnp.concatenate([pos_col] * (SORT_ROWS // LANES), axis=1), 1.0, 0.0).astype(BF16)

    xs_ref[:SORT_ROWS, :] = jnp.dot(perm, h_ref[...], preferred_element_type=F32).astype(BF16)
    xs_ref[SORT_ROWS:, :] = jnp.zeros((SORT_ROWS_PAD - SORT_ROWS, D_MODEL), BF16)
    hi, mid, lo = _split3(gate4_ref[...])
    packed = (hi.astype(F32) + pltpu.roll(mid.astype(F32), EXPERTS_PER_GROUP, 1)
              + pltpu.roll(lo.astype(F32), 2 * EXPERTS_PER_GROUP, 1)).astype(BF16)
    gsp = jnp.dot(perm, packed, preferred_element_type=F32)
    gs_ref[:SORT_ROWS, :] = (gsp + pltpu.roll(gsp, LANES - EXPERTS_PER_GROUP, 1)
                             + pltpu.roll(gsp, LANES - 2 * EXPERTS_PER_GROUP, 1))
    gs_ref[SORT_ROWS:, :] = jnp.zeros((SORT_ROWS_PAD - SORT_ROWS, LANES), F32)
    zs_ref[...] = jnp.zeros((SORT_ROWS_PAD, D_MODEL), BF16)

    def block_ffn(g, blk):
        row0 = pl.multiple_of(starts[g] + blk * RB, BF16_ROWS)
        xb = xs_ref[pl.ds(row0, RB), :]
        gb = gs_ref[pl.ds(row0, RB), :]
        cols = []
        for e in range(EXPERTS_PER_GROUP):
            a = jnp.dot(xb, w1_ref[g * EXPERTS_PER_GROUP + e], preferred_element_type=F32)
            b = jnp.dot(xb, w3_ref[g * EXPERTS_PER_GROUP + e], preferred_element_type=F32)
            cols.append((a / (1.0 + jnp.exp(-a)) * b * gb[:, e:e + 1]).astype(BF16))
        z = jnp.dot(jnp.concatenate(cols, axis=1), w2_ref[g], preferred_element_type=F32)
        zs_ref[pl.ds(row0, RB), :] = z.astype(BF16)

    for g in range(N_EXPERT_GROUPS):
        def extra_block(blk, carry, g=g):
            block_ffn(g, blk)
            return carry

        lax.fori_loop(1, (counts[g] + (RB - 1)) // RB, extra_block, 0)
    for g in range(N_EXPERT_GROUPS):
        block_ffn(g, 0)

    y = jnp.dot(perm_t, zs_ref[:SORT_ROWS, :], preferred_element_type=F32)
    out = x1_ref[...] + gate2 * y

    if final:
        octx_ref, olat_ref = out_refs
        out = _rms(out, fn_ref[...])
        olat_ref[...] = out

        @pl.when(i < N_CTX_TILES)
        def _():
            octx_ref[...] = out
    else:
        out_refs[0][...] = out


def _moe(l, counts, h2, route, gate4, tri, w1, w3, w2, x1, mod, final_norm, final):
    tile = pl.BlockSpec((TM, D_MODEL), lambda i, c: (i, 0))
    resident = pl.Buffered(1)
    if final:
        out_specs = [pl.BlockSpec((TM, D_MODEL), lambda i, c: (_ctx_tile(i), 0)),
                     pl.BlockSpec((TM, D_MODEL), lambda i, c: (_lat_tile(i), 0))]
        out_shape = [jax.ShapeDtypeStruct((T_CTX, D_MODEL), F32), jax.ShapeDtypeStruct((T_LAT, D_MODEL), F32)]
    else:
        out_specs = tile
        out_shape = jax.ShapeDtypeStruct((T_ALL, D_MODEL), F32)
    grid_spec = pltpu.PrefetchScalarGridSpec(
        num_scalar_prefetch=1,
        grid=(N_TILES,),
        in_specs=[
            tile,
            pl.BlockSpec((None, SUBLANES, TM), lambda i, c: (i, 0, 0)),
            pl.BlockSpec((TM, LANES), lambda i, c: (i, 0)),
            pl.BlockSpec((TM, TM), lambda i, c: (0, 0), pipeline_mode=resident),
            pl.BlockSpec((N_EXPERTS, D_MODEL, D_EXPERT), lambda i, c: (0, 0, 0), pipeline_mode=resident),
            pl.BlockSpec((N_EXPERTS, D_MODEL, D_EXPERT), lambda i, c: (0, 0, 0), pipeline_mode=resident),
            pl.BlockSpec((N_EXPERT_GROUPS, D_GROUP_HID, D_MODEL), lambda i, c: (0, 0, 0), pipeline_mode=resident),
            tile,
            pl.BlockSpec((None, N_MOD_PARTS, N_MOD_ROWS, D_MODEL), lambda i, c: (l, 0, 0, 0)),
            pl.BlockSpec((1, D_MODEL), lambda i, c: (0, 0)),
        ],
        out_specs=out_specs,
        scratch_shapes=[pltpu.VMEM((SORT_ROWS_PAD, D_MODEL), BF16),
                        pltpu.VMEM((SORT_ROWS_PAD, LANES), F32),
                        pltpu.VMEM((SORT_ROWS_PAD, D_MODEL), BF16)],
    )
    return pl.pallas_call(
        functools.partial(_moe_kernel, final=final),
        grid_spec=grid_spec,
        out_shape=out_shape,
        compiler_params=pltpu.CompilerParams(dimension_semantics=("arbitrary",),
                                             vmem_limit_bytes=MOE_VMEM_LIMIT),
        name=f"moe_l{l}",
    )(counts, h2, route, gate4, tri, w1, w3, w2, x1, mod, final_norm)


def _rope_tables():
    rows = DEC_SEQ // GRID_W
    row = np.repeat(np.arange(rows, dtype=np.float32), GRID_W)
    col = np.tile(np.arange(GRID_W, dtype=np.float32), rows)
    half = ROPE_AXIS_DIM // 2
    inv_freq = (1.0 / (np.float32(ROPE_THETA) ** (np.arange(half, dtype=np.float32) * np.float32(2.0)
                                                  / np.float32(ROPE_AXIS_DIM)))).astype(np.float32)
    ar = row[:, None] * inv_freq
    ac = col[:, None] * inv_freq
    cos64 = np.concatenate([np.cos(ar), np.cos(ar), np.cos(ac), np.cos(ac)], axis=1)
    sin64 = np.concatenate([-np.sin(ar), np.sin(ar), -np.sin(ac), np.sin(ac)], axis=1)
    cos_t = np.concatenate([np.ones((TM, LANES), np.float32), np.tile(cos64, (1, 2))], axis=0)
    sin_t = np.concatenate([np.zeros((TM, LANES), np.float32), np.tile(sin64, (1, 2))], axis=0)
    return jnp.asarray(cos_t, F32), jnp.asarray(sin_t, F32)


def _router_weights(router_g_w, router_g_b, router_e_w, router_e_b):
    zw = jnp.zeros((DEPTH, D_MODEL, SUBLANES - N_EXPERT_GROUPS), F32)
    zb = jnp.zeros((DEPTH, SUBLANES - N_EXPERT_GROUPS), F32)
    w_cols, b_cols = [router_g_w, zw], [router_g_b, zb]
    for g in range(N_EXPERT_GROUPS):
        w_cols += [router_e_w[:, g], zw]
        b_cols += [router_e_b[:, g], zb]
    used = (1 + N_EXPERT_GROUPS) * SUBLANES
    w_cols.append(jnp.zeros((DEPTH, D_MODEL, ROUTER_COLS - used), F32))
    b_cols.append(jnp.zeros((DEPTH, ROUTER_COLS - used), F32))
    w = jnp.concatenate(w_cols, axis=2)
    b = jnp.concatenate(b_cols, axis=1).reshape(DEPTH, 1, ROUTER_COLS)
    w_hi = w.astype(BF16)
    w_lo = (w - w_hi.astype(F32)).astype(BF16)
    return jnp.concatenate([w_hi, w_lo], axis=2), w_hi, b


def kernel(x_prompt, x_sample, cache_k, cache_v, c, c_ctx, w_ada, b_ada, norm1, w_in, q_norm, k_norm,
           sgu_w, sgu_b, conv_w, out_norm, w_out, norm2, router_g_w, router_g_b, router_e_w,
           router_e_b, moe_w1, moe_w3, moe_w2, final_norm):
    xs = (x_prompt.reshape(T_CTX, D_MODEL), x_sample.reshape(T_LAT, D_MODEL))

    cvecs = jnp.concatenate([c_ctx[None, :], c, jnp.zeros((N_MOD_ROWS - 1 - DEC_BATCH, D_MODEL), F32)], axis=0)
    mod = _modulation(cvecs.T, w_ada, b_ada)

    w_in_b = w_in.astype(BF16)
    w_out_b = w_out.astype(BF16)
    g1 = norm1.reshape(DEPTH, 1, D_MODEL)
    g2 = norm2.reshape(DEPTH, 1, D_MODEL)
    on = out_norm.reshape(DEPTH, 1, D_MIX)
    qg = jnp.tile(q_norm, (1, N_HEADS)).reshape(DEPTH, 1, D_ATTN)
    kg = jnp.tile(k_norm, (1, N_KV_HEADS)).reshape(DEPTH, 1, D_KV)
    cos_t, sin_t = _rope_tables()
    bq = jnp.asarray(np.kron(np.eye(N_HEADS), np.full((HEAD_DIM, HEAD_DIM), 1.0 / HEAD_DIM)), BF16)
    sgu_w_pairs = sgu_w.astype(BF16).reshape(DEPTH, N_SGU_GROUPS // 2, 2, CHUNK, CHUNK)
    sgu_w_pairs = jnp.concatenate([sgu_w_pairs[:, :, 0], sgu_w_pairs[:, :, 1]], axis=-1)
    sgu_bias = jnp.repeat(jnp.swapaxes(sgu_b, 1, 2), SGU_GROUP_DIM, axis=2)
    wr_cat, wr_hi, rb = _router_weights(router_g_w, router_g_b, router_e_w, router_e_b)
    fn = final_norm.reshape(1, D_MODEL)
    tri = jnp.asarray(np.triu(np.ones((TM, TM), np.float32)), BF16)
    ck = cache_k.reshape(DEC_BATCH, DEPTH, PAST_LEN, D_KV)
    cv = cache_v.reshape(DEC_BATCH, DEPTH, PAST_LEN, D_KV)

    ctx_ks, ctx_vs = [], []
    for l in range(DEPTH):
        qt, k, vt, k32, v32, sgu_n, cbz = _proj(l, xs, mod, g1, w_in_b, qg, kg, cos_t, sin_t, bq,
                                                sgu_w_pairs, sgu_bias, on)
        attn_ctx = _attn_ctx(l, qt, k, vt)
        attn_lat, w1, w3, w2 = _attn_lat(l, qt, k, vt, ck, cv, qg, kg, moe_w1, moe_w3, moe_w2)
        w2 = w2.reshape(N_EXPERT_GROUPS, D_GROUP_HID, D_MODEL)
        x1, h2, route, gate4, cnt = _merge(l, xs, attn_ctx, attn_lat, sgu_n, cbz, conv_w, on, w_out_b, mod, g2,
                                           wr_cat, wr_hi, rb)
        counts = cnt[:, :N_EXPERT_GROUPS, 0]
        out = _moe(l, counts, h2, route, gate4, tri, w1, w3, w2, x1, mod, fn, final=(l == DEPTH - 1))
        xs = (out,)
        ctx_ks.append(k32.reshape(BATCH, SEQ, N_KV_HEADS, HEAD_DIM))
        ctx_vs.append(v32.reshape(BATCH, SEQ, N_KV_HEADS, HEAD_DIM))

    y_prompt, y_sample = out
    return (y_prompt.reshape(BATCH, SEQ, D_MODEL), y_sample.reshape(DEC_BATCH, DEC_SEQ, D_MODEL),
            jnp.stack(ctx_ks, axis=1), jnp.stack(ctx_vs, axis=1))
```
